```python
import math
import jax
import jax.numpy as jnp
from jax import lax
import numpy as np

D_MODEL = 1024
BATCH = 8
SEQ = 4096
DEPTH = 2

GRID_W = 64
CTX_LEN = 256
EPS = 1e-6
N_AB = (DEPTH + 1) // 2
N_RET = DEPTH // 2

GLA_HEADS = 4
GLA_DK = 64
GLA_DV = 128
GLA_RANK = 16
GLA_TAU = 16.0
GLA_CHUNK = 64
AB_QK = GLA_HEADS * GLA_DK
AB_V = GLA_HEADS * GLA_DV
S5_CH = D_MODEL // 2
S5_GROUP = 16
S5_GROUPS = S5_CH // S5_GROUP
S5_P = 64
AB_IN = 2 * AB_QK + 2 * AB_V + 2 * GLA_RANK + S5_CH
AB_MIX = AB_V + S5_CH
RET_HEADS = 4
RET_DK = D_MODEL // RET_HEADS
RET_DV = 2 * RET_DK
RET_CHUNK = 128
RET_QK = RET_HEADS * RET_DK
RET_MIX = RET_HEADS * RET_DV
RET_IN = 2 * RET_QK + 2 * RET_MIX
ROPE_BASE = 10000.0
N_EXPERTS = 32
TOP_K = 4
D_FF = 1024
SWIGLU_LIMIT = 7.0
SWIGLU_ALPHA = 1.702
MOE_BLOCK = 256

kernel_name = 'hybrid_gla_s5_retention_moe_dit'


def rms_norm(x, g):
    xf = x.astype(jnp.float32)
    y = xf * lax.rsqrt(jnp.mean(xf * xf, axis=-1, keepdims=True) + EPS)
    return y * g.astype(jnp.float32)


def modulate(h, shift, scale):
    return h * (1.0 + scale[:, None, :]) + shift[:, None, :]


def chunk_gated_recurrence(q, k, v, log_a, s0, chunk, with_output):
    n = q.shape[2] // chunk

    def blocks(t):
        t = t.astype(jnp.float32)
        return t.reshape(t.shape[0], t.shape[1], n, chunk, t.shape[-1])

    qc, kc, vc = blocks(q), blocks(k), blocks(v)
    b = jnp.cumsum(blocks(log_a), axis=3)
    b_last = b[:, :, :, -1:, :]
    k_state = kc * jnp.exp(b_last - b)
    to_scan = lambda t: jnp.moveaxis(t, 2, 0)
    if with_output:
        q_dec = qc * jnp.exp(b)
        k_inv = kc * jnp.exp(-b)
        lower = jnp.tril(jnp.ones((chunk, chunk), dtype=bool))
        scores = jnp.where(lower, jnp.einsum('bhnik,bhnjk->bhnij', q_dec, k_inv), 0.0)
        o_intra = jnp.einsum('bhnij,bhnjv->bhniv', scores, vc)

        def step(s, xs):
            q_b, k_b, v_b, d_b = xs
            o_b = jnp.einsum('bhik,bhkv->bhiv', q_b, s)
            s = s * jnp.exp(jnp.swapaxes(d_b, -1, -2)) + jnp.einsum('bhjk,bhjv->bhkv', k_b, v_b)
            return s, o_b

        s_fin, o_inter = lax.scan(step, s0, (to_scan(q_dec), to_scan(k_state), to_scan(vc), to_scan(b_last)))
        o = o_intra + jnp.moveaxis(o_inter, 0, 2)
        return o.reshape(o.shape[0], o.shape[1], n * chunk, o.shape[-1]), s_fin

    def step_state(s, xs):
        k_b, v_b, d_b = xs
        return s * jnp.exp(jnp.swapaxes(d_b, -1, -2)) + jnp.einsum('bhjk,bhjv->bhkv', k_b, v_b), None

    s_fin, _ = lax.scan(step_state, s0, (to_scan(k_state), to_scan(vc), to_scan(b_last)))
    return None, s_fin


def prefix_recurrence(ctx_in, lat_in, chunk, reverse, ctx_out):
    flip = (lambda t: jnp.flip(t, axis=2)) if reverse else (lambda t: t)
    q_c, k_c, v_c, a_c = (flip(t) for t in ctx_in)
    q_l, k_l, v_l, a_l = (flip(t) for t in lat_in)
    s0 = jnp.zeros((k_c.shape[0], k_c.shape[1], k_c.shape[-1], v_c.shape[-1]), jnp.float32)
    o_c, s_c = chunk_gated_recurrence(q_c, k_c, v_c, a_c, s0, chunk, ctx_out)
    o_l, _ = chunk_gated_recurrence(q_l, k_l, v_l, a_l, s_c, chunk, True)
    return (flip(o_c) if ctx_out else None), flip(o_l)


def gla_heads(q, k, v, low, wa, ba):
    b, n, _ = q.shape

    def heads(t, dh):
        return t.reshape(b, n, GLA_HEADS, dh).transpose(0, 2, 1, 3)

    lows = jnp.split(low, 2, axis=-1)
    log_a = [heads(jax.nn.log_sigmoid((lows[d] @ wa[d] + ba[d]).astype(jnp.float32)) / GLA_TAU, GLA_DK)
             for d in range(2)]
    return heads(q, GLA_DK) * GLA_DK ** -0.5, heads(k, GLA_DK), heads(v, GLA_DV), log_a


def gla_output(o, g, norm_g):
    b, h, n, dv = o.shape
    o = o * lax.rsqrt(jnp.mean(o * o, axis=-1, keepdims=True) + EPS) * norm_g.astype(jnp.float32)
    return o.transpose(0, 2, 1, 3).reshape(b, n, h * dv) * jax.nn.silu(g.astype(jnp.float32))


def zoh(lam_re, lam_im, log_step):
    step = jnp.exp(log_step.astype(jnp.float32))[:, None]
    lam_re, lam_im = lam_re.astype(jnp.float32), lam_im.astype(jnp.float32)
    mag = jnp.exp(lam_re * step)
    lb_re, lb_im = mag * jnp.cos(lam_im * step), mag * jnp.sin(lam_im * step)
    den = lam_re * lam_re + lam_im * lam_im
    f_re = ((lb_re - 1.0) * lam_re + lb_im * lam_im) / den
    f_im = (lb_im * lam_re - (lb_re - 1.0) * lam_im) / den
    return lb_re, lb_im, f_re, f_im


def complex_diag_scan(a_re, a_im, b_re, b_im):
    n = b_re.shape[0]
    a_re = jnp.broadcast_to(a_re[None, None], (n, 1) + a_re.shape)
    a_im = jnp.broadcast_to(a_im[None, None], (n, 1) + a_im.shape)

    def combine(e1, e2):
        a1r, a1i, b1r, b1i = e1
        a2r, a2i, b2r, b2i = e2
        return (a2r * a1r - a2i * a1i, a2r * a1i + a2i * a1r,
                a2r * b1r - a2i * b1i + b2r, a2r * b1i + a2i * b1r + b2i)

    _, _, s_re, s_im = lax.associative_scan(combine, (a_re, a_im, b_re, b_im), axis=0)
    return s_re, s_im


def s5_bidirectional(u_ctx, u_lat, lam_re, lam_im, log_step, b_re, b_im, c_re, c_im, d_skip, glu_w, glu_b, ctx_out):
    def groups(u):
        return u.astype(jnp.float32).reshape(u.shape[0], u.shape[1], S5_GROUPS, S5_GROUP)

    uc, ul = groups(u_ctx), groups(u_lat)
    y_c, y_l = [], []
    for d in range(2):
        lb_re, lb_im, f_re, f_im = zoh(lam_re[d], lam_im[d], log_step[d])
        bb_re = f_re[..., None] * b_re - f_im[..., None] * b_im
        bb_im = f_re[..., None] * b_im + f_im[..., None] * b_re

        def drive(u):
            br = jnp.einsum('blgc,gpc->lbgp', u, bb_re)
            bi = jnp.einsum('blgc,gpc->lbgp', u, bb_im)
            return (br[::-1], bi[::-1]) if d == 1 else (br, bi)

        def readout(s_re, s_im):
            y = jnp.einsum('lbgp,gcp->blgc', s_re, c_re[d]) - jnp.einsum('lbgp,gcp->blgc', s_im, c_im[d])
            return y[:, ::-1] if d == 1 else y

        sc_re, sc_im = complex_diag_scan(lb_re, lb_im, *drive(uc))
        bl_re, bl_im = drive(ul)
        bl_re = bl_re.at[0].add(lb_re * sc_re[-1] - lb_im * sc_im[-1])
        bl_im = bl_im.at[0].add(lb_re * sc_im[-1] + lb_im * sc_re[-1])
        sl_re, sl_im = complex_diag_scan(lb_re, lb_im, bl_re, bl_im)
        y_l.append(readout(sl_re, sl_im))
        if ctx_out:
            y_c.append(readout(sc_re, sc_im))

    def finish(ys, u):
        y = ys[0] + ys[1] + d_skip.astype(jnp.float32).reshape(S5_GROUPS, S5_GROUP) * u
        y = jax.nn.gelu(y.reshape(y.shape[0], y.shape[1], S5_CH))
        return y * jax.nn.sigmoid(y @ glu_w + glu_b)

    return (finish(y_c, uc) if ctx_out else None), finish(y_l, ul)


def mixer_gla_s5(h_ctx, h_lat, w_in, w_out, gla_wa, gla_ba, gla_norm_g, lam_re, lam_im, log_step,
                 b_re, b_im, c_re, c_im, d_skip, glu_w, glu_b, ctx_out):
    split_at = [AB_QK, 2 * AB_QK, 2 * AB_QK + AB_V, 2 * AB_QK + 2 * AB_V, 2 * AB_QK + 2 * AB_V + 2 * GLA_RANK]
    q_c, k_c, v_c, g_c, low_c, u_c = jnp.split(h_ctx @ w_in, split_at, axis=-1)
    q_l, k_l, v_l, g_l, low_l, u_l = jnp.split(h_lat @ w_in, split_at, axis=-1)
    qc, kc, vc, la_c = gla_heads(q_c, k_c, v_c, low_c, gla_wa, gla_ba)
    ql, kl, vl, la_l = gla_heads(q_l, k_l, v_l, low_l, gla_wa, gla_ba)
    outs = [prefix_recurrence((qc, kc, vc, la_c[d]), (ql, kl, vl, la_l[d]), GLA_CHUNK, d == 1, ctx_out)
            for d in range(2)]
    s5_c, s5_l = s5_bidirectional(u_c, u_l, lam_re, lam_im, log_step, b_re, b_im, c_re, c_im,
                                  d_skip, glu_w, glu_b, ctx_out)
    y_lat = jnp.concatenate([gla_output(outs[0][1] + outs[1][1], g_l, gla_norm_g), s5_l], axis=-1) @ w_out
    y_ctx = None
    if ctx_out:
        y_ctx = jnp.concatenate([gla_output(outs[0][0] + outs[1][0], g_c, gla_norm_g), s5_c], axis=-1) @ w_out
    return y_ctx, y_lat


def axial_rope(t, row, col):
    half = t.shape[-1] // 2
    n_freq = half // 2
    inv_freq = ROPE_BASE ** (-jnp.arange(n_freq, dtype=jnp.float32) / n_freq)

    def rotate(tp, pos):
        ang = pos.astype(jnp.float32)[:, None] * inv_freq
        cos, sin = jnp.cos(ang)[None, :, None, :], jnp.sin(ang)[None, :, None, :]
        t1, t2 = tp[..., :n_freq], tp[..., n_freq:]
        return jnp.concatenate([t1 * cos - t2 * sin, t1 * sin + t2 * cos], axis=-1)

    return jnp.concatenate([rotate(t[..., :half], row), rotate(t[..., half:], col)], axis=-1)


def mixer_retention(h_ctx, h_lat, w_in, w_out, decay_logit, norm_g, row, col, ctx_out):
    split_at = [RET_QK, 2 * RET_QK, 2 * RET_QK + RET_MIX]

    def prep(h, rope):
        b, n, _ = h.shape
        q, k, v, g = jnp.split(h @ w_in, split_at, axis=-1)
        q = q.reshape(b, n, RET_HEADS, RET_DK)
        k = k.reshape(b, n, RET_HEADS, RET_DK)
        v = v.reshape(b, n, RET_HEADS, RET_DV)
        if rope:
            q, k = axial_rope(q, row, col), axial_rope(k, row, col)
        k = k * RET_DK ** -0.5
        return q.transpose(0, 2, 1, 3), k.transpose(0, 2, 1, 3), v.transpose(0, 2, 1, 3), g

    qc, kc, vc, gc = prep(h_ctx, False)
    ql, kl, vl, gl = prep(h_lat, True)
    log_gamma = jax.nn.log_sigmoid(decay_logit.astype(jnp.float32))

    def decay(d, n):
        return jnp.broadcast_to(log_gamma[d][None, :, None, None], (1, RET_HEADS, n, 1))

    outs = [prefix_recurrence((qc, kc, vc, decay(d, qc.shape[2])), (ql, kl, vl, decay(d, ql.shape[2])),
                              RET_CHUNK, d == 1, ctx_out) for d in range(2)]

    def out(o, g):
        b, h, n, dv = o.shape
        mu = jnp.mean(o, axis=-1, keepdims=True)
        var = jnp.mean(jnp.square(o - mu), axis=-1, keepdims=True)
        o = ((o - mu) * lax.rsqrt(var + EPS)).transpose(0, 2, 1, 3).reshape(b, n, h * dv)
        return (o * norm_g.astype(jnp.float32) * jax.nn.silu(g.astype(jnp.float32))) @ w_out

    y_lat = out(outs[0][1] + outs[1][1], gl)
    y_ctx = out(outs[0][0] + outs[1][0], gc) if ctx_out else None
    return y_ctx, y_lat


def moe_ffn(h, w_router, b_router, w_gu, b_gu, w_down, b_down):
    n, d = h.shape
    logits = (h @ w_router + b_router).astype(jnp.float32)
    top_logit, top_e = lax.top_k(logits, TOP_K)
    weights = jax.nn.softmax(top_logit, axis=-1)
    flat_e = top_e.reshape(-1)
    order = jnp.argsort(flat_e)
    e_sorted = flat_e[order]
    counts = jnp.bincount(flat_e, length=N_EXPERTS)
    padded = (counts + MOE_BLOCK - 1) // MOE_BLOCK * MOE_BLOCK
    pad_end = jnp.cumsum(padded)
    pad_start = pad_end - padded
    start = jnp.cumsum(counts) - counts
    dest = pad_start[e_sorted] + jnp.arange(n * TOP_K, dtype=jnp.int32) - start[e_sorted]
    n_blocks = (n * TOP_K + MOE_BLOCK - 1) // MOE_BLOCK + N_EXPERTS
    slot_tok = jnp.full((n_blocks * MOE_BLOCK,), n, jnp.int32).at[dest].set((order // TOP_K).astype(jnp.int32))
    slot_w = jnp.zeros((n_blocks * MOE_BLOCK,), jnp.float32).at[dest].set(weights.reshape(-1)[order])
    block_e = jnp.minimum(jnp.searchsorted(pad_end, jnp.arange(n_blocks, dtype=jnp.int32) * MOE_BLOCK,
                                           side='right'), N_EXPERTS - 1)
    h_pad = jnp.concatenate([h, jnp.zeros((1, d), h.dtype)], axis=0)
    xb = h_pad[slot_tok].reshape(n_blocks, MOE_BLOCK, d)

    def expert_block(args):
        x_blk, e = args
        gu = x_blk @ w_gu[e] + b_gu[e]
        gate = jnp.minimum(gu[:, :D_FF], SWIGLU_LIMIT)
        lin = jnp.clip(gu[:, D_FF:], -SWIGLU_LIMIT, SWIGLU_LIMIT)
        act = gate * jax.nn.sigmoid(SWIGLU_ALPHA * gate) * (lin + 1.0)
        return act @ w_down[e] + b_down[e]

    yb = lax.map(expert_block, (xb, block_e)).reshape(-1, d)
    y = jnp.zeros((n + 1, d), yb.dtype).at[slot_tok].add(yb * slot_w[:, None].astype(yb.dtype))
    return y[:n]


def setup_inputs(seed: int = 0) -> dict:
    key = jax.random.key(seed)
    keys = iter(jax.random.split(key, 64))
    f32 = jnp.float32

    def nrm(shape, scale):
        return scale * jax.random.normal(next(keys), shape, f32)

    def gain(shape):
        return 1.0 + nrm(shape, 0.02)

    lam_re = -0.5 + nrm((N_AB, 2, S5_GROUPS, S5_P), 0.01)
    lam_im = math.pi * jnp.arange(S5_P, dtype=f32) + nrm((N_AB, 2, S5_GROUPS, S5_P), 0.01)
    log_step = jax.random.uniform(next(keys), (N_AB, 2, S5_GROUPS), f32, math.log(1e-3), math.log(1e-1))
    ret_logit = jnp.log(2.0 ** (5.0 + jnp.arange(RET_HEADS, dtype=f32)) - 1.0) + nrm((N_RET, 2, RET_HEADS), 0.01)
    return {
        'x': nrm((BATCH, SEQ, D_MODEL), 1.0),
        'c': nrm((BATCH, D_MODEL), 1.0),
        'ctx': nrm((BATCH, CTX_LEN, D_MODEL), 1.0),
        'c_ctx': nrm((D_MODEL,), 1.0),
        'ada_w': nrm((DEPTH, D_MODEL, 6 * D_MODEL), 0.5 * D_MODEL ** -0.5),
        'ada_b': nrm((DEPTH, 6 * D_MODEL), 0.02),
        'norm1_g': gain((DEPTH, D_MODEL)),
        'norm2_g': gain((DEPTH, D_MODEL)),
        'ab_w_in': nrm((N_AB, D_MODEL, AB_IN), D_MODEL ** -0.5),
        'ab_w_out': nrm((N_AB, AB_MIX, D_MODEL), AB_MIX ** -0.5),
        'gla_wa': nrm((N_AB, 2, GLA_RANK, AB_QK), GLA_RANK ** -0.5),
        'gla_ba': nrm((N_AB, 2, AB_QK), 0.01),
        'gla_norm_g': gain((N_AB, GLA_DV)),
        's5_lam_re': lam_re,
        's5_lam_im': lam_im,
        's5_log_step': log_step,
        's5_b_re': nrm((N_AB, S5_GROUPS, S5_P, S5_GROUP), (2 * S5_GROUP) ** -0.5),
        's5_b_im': nrm((N_AB, S5_GROUPS, S5_P, S5_GROUP), (2 * S5_GROUP) ** -0.5),
        's5_c_re': nrm((N_AB, 2, S5_GROUPS, S5_GROUP, S5_P), (2 * S5_P) ** -0.5),
        's5_c_im': nrm((N_AB, 2, S5_GROUPS, S5_GROUP, S5_P), (2 * S5_P) ** -0.5),
        's5_d': nrm((N_AB, S5_CH), 1.0),
        's5_glu_w': nrm((N_AB, S5_CH, S5_CH), S5_CH ** -0.5),
        's5_glu_b': nrm((N_AB, S5_CH), 0.01),
        'ret_w_in': nrm((N_RET, D_MODEL, RET_IN), D_MODEL ** -0.5),
        'ret_w_out': nrm((N_RET, RET_MIX, D_MODEL), RET_MIX ** -0.5),
        'ret_decay_logit': ret_logit,
        'ret_norm_g': gain((N_RET, RET_MIX)),
        'moe_w_router': nrm((DEPTH, D_MODEL, N_EXPERTS), D_MODEL ** -0.5),
        'moe_b_router': nrm((DEPTH, N_EXPERTS), 0.01),
        'moe_w_gu': nrm((DEPTH, N_EXPERTS, D_MODEL, 2 * D_FF), D_MODEL ** -0.5),
        'moe_b_gu': nrm((DEPTH, N_EXPERTS, 2 * D_FF), 0.01),
        'moe_w_down': nrm((DEPTH, N_EXPERTS, D_FF, D_MODEL), D_FF ** -0.5),
        'moe_b_down': nrm((DEPTH, N_EXPERTS, D_MODEL), 0.01),
        'final_norm_g': gain((D_MODEL,)),
    }


def reference(x, c, ctx, c_ctx, ada_w, ada_b, norm1_g, norm2_g,
              ab_w_in, ab_w_out, gla_wa, gla_ba, gla_norm_g,
              s5_lam_re, s5_lam_im, s5_log_step, s5_b_re, s5_b_im, s5_c_re, s5_c_im,
              s5_d, s5_glu_w, s5_glu_b,
              ret_w_in, ret_w_out, ret_decay_logit, ret_norm_g,
              moe_w_router, moe_b_router, moe_w_gu, moe_b_gu, moe_w_down, moe_b_down,
              final_norm_g):
    batch, n_lat, d = x.shape
    n_ctx = ctx.shape[1]
    rows = n_lat // GRID_W
    row = jnp.broadcast_to(jnp.arange(rows, dtype=jnp.int32)[:, None], (rows, GRID_W)).reshape(-1)
    col = jnp.broadcast_to(jnp.arange(GRID_W, dtype=jnp.int32)[None, :], (rows, GRID_W)).reshape(-1)
    x_lat, x_ctx = x, ctx
    for layer in range(DEPTH):
        ctx_out = layer < DEPTH - 1
        i = layer // 2
        mod_lat = (jax.nn.silu(c) @ ada_w[layer] + ada_b[layer]).astype(jnp.float32)
        mod_ctx = (jax.nn.silu(c_ctx)[None, :] @ ada_w[layer] + ada_b[layer]).astype(jnp.float32)
        sh1_l, sc1_l, g1_l, sh2_l, sc2_l, g2_l = jnp.split(mod_lat, 6, axis=-1)
        sh1_c, sc1_c, g1_c, sh2_c, sc2_c, g2_c = jnp.split(mod_ctx, 6, axis=-1)
        h_lat = modulate(rms_norm(x_lat, norm1_g[layer]), sh1_l, sc1_l)
        h_ctx = modulate(rms_norm(x_ctx, norm1_g[layer]), sh1_c, sc1_c)
        if layer % 2 == 0:
            m_ctx, m_lat = mixer_gla_s5(h_ctx, h_lat, ab_w_in[i], ab_w_out[i], gla_wa[i], gla_ba[i], gla_norm_g[i],
                                        s5_lam_re[i], s5_lam_im[i], s5_log_step[i], s5_b_re[i], s5_b_im[i],
                                        s5_c_re[i], s5_c_im[i], s5_d[i], s5_glu_w[i], s5_glu_b[i], ctx_out)
        else:
            m_ctx, m_lat = mixer_retention(h_ctx, h_lat, ret_w_in[i], ret_w_out[i], ret_decay_logit[i],
                                           ret_norm_g[i], row, col, ctx_out)
        x_lat = (x_lat + g1_l[:, None, :] * m_lat).astype(x.dtype)
        h_lat = modulate(rms_norm(x_lat, norm2_g[layer]), sh2_l, sc2_l).reshape(-1, d)
        if ctx_out:
            x_ctx = (x_ctx + g1_c[:, None, :] * m_ctx).astype(ctx.dtype)
            h_ctx = modulate(rms_norm(x_ctx, norm2_g[layer]), sh2_c, sc2_c).reshape(-1, d)
            y = moe_ffn(jnp.concatenate([h_ctx, h_lat], axis=0), moe_w_router[layer], moe_b_router[layer],
                        moe_w_gu[layer], moe_b_gu[layer], moe_w_down[layer], moe_b_down[layer])
            x_ctx = (x_ctx + g2_c[:, None, :] * y[:batch * n_ctx].reshape(x_ctx.shape)).astype(ctx.dtype)
            y_lat = y[batch * n_ctx:]
        else:
            y_lat = moe_ffn(h_lat, moe_w_router[layer], moe_b_router[layer],
                            moe_w_gu[layer], moe_b_gu[layer], moe_w_down[layer], moe_b_down[layer])
        x_lat = (x_lat + g2_l[:, None, :] * y_lat.reshape(x_lat.shape)).astype(x.dtype)
    return rms_norm(x_lat, final_norm_g).astype(x.dtype)
```

```python
import functools
import math

import jax
import jax.numpy as jnp
from jax import lax
from jax.experimental import pallas as pl
from jax.experimental.pallas import tpu as pltpu

F32, BF16, I32 = jnp.float32, jnp.bfloat16, jnp.int32

D_MODEL = 1024
GRID_W = 64
EPS = 1e-6
GLA_HEADS, GLA_DK, GLA_DV, GLA_RANK, GLA_TAU, GLA_CHUNK = 4, 64, 128, 16, 16.0, 64
AB_QK, AB_V = GLA_HEADS * GLA_DK, GLA_HEADS * GLA_DV
S5_CH, S5_GROUP, S5_GROUPS, S5_P = 512, 16, 32, 64
S5_CHUNK = 16
RET_HEADS, RET_DK, RET_DV, RET_CHUNK = 4, 256, 512, 128
RET_QK, RET_MIX = RET_HEADS * RET_DK, RET_HEADS * RET_DV
ROPE_BASE = 10000.0
N_EXPERTS, TOP_K, D_FF = 32, 4, 1024
SWIGLU_LIMIT, SWIGLU_ALPHA = 7.0, 1.702
MOE_BLOCK = 256
TOKEN_TILE = 256
ADA_TILE = 768
VMEM_LIMIT = 56 * 1024 * 1024


def _params(*sem):
    return pltpu.CompilerParams(dimension_semantics=sem, vmem_limit_bytes=VMEM_LIMIT)


def _dot(a, b):
    return jnp.dot(a, b, preferred_element_type=F32)


def _dot_nt(a, b):
    return lax.dot_general(a, b, (((1,), (1,)), ((), ())), preferred_element_type=F32)


def _dot_tn(a, b):
    return lax.dot_general(a, b, (((0,), (0,)), ((), ())), preferred_element_type=F32)


def _split(a):
    hi = a.astype(BF16)
    return hi, (a - hi.astype(F32)).astype(BF16)


def _dot3(a, b, dot=_dot):
    ah, al = _split(a)
    bh, bl = _split(b)
    return dot(ah, bh) + (dot(ah, bl) + dot(al, bh))


def _silu(x):
    return x * jax.nn.sigmoid(x)


def _norm_mod(x, g, shift, scale):
    r = lax.rsqrt(jnp.mean(x * x, axis=-1, keepdims=True) + EPS)
    return (x * r * g) * (1.0 + scale) + shift


def _const_spec(shape):
    nd = len(shape)
    return pl.BlockSpec(shape, lambda *_: (0,) * nd)


def _ada_body(c_ref, w_ref, b_ref, o_ref):
    o_ref[0] = _dot3(_silu(c_ref[...]), w_ref[0]) + b_ref[0]


def _ada_mod(cvec, ada_w, ada_b):
    depth, d, n6 = ada_w.shape
    rows = cvec.shape[0]
    return pl.pallas_call(
        _ada_body,
        grid=(depth, n6 // ADA_TILE),
        in_specs=[_const_spec((rows, d)),
                  pl.BlockSpec((1, d, ADA_TILE), lambda l, j: (l, 0, j)),
                  pl.BlockSpec((1, 1, ADA_TILE), lambda l, j: (l, 0, j))],
        out_specs=pl.BlockSpec((1, rows, ADA_TILE), lambda l, j: (l, 0, j)),
        out_shape=jax.ShapeDtypeStruct((depth, rows, n6), F32),
        compiler_params=_params("arbitrary", "arbitrary"),
        name="ada_mod",
    )(cvec, ada_w, ada_b.reshape(depth, 1, n6))


def _tile_specs(nct, d):
    x_spec = pl.BlockSpec((1, TOKEN_TILE, d), lambda b, i: (b, i, 0))
    mod_spec = pl.BlockSpec((1, 1, 6, d), lambda b, i: (b, (i >= nct).astype(I32), 0, 0))
    return x_spec, mod_spec


def _inproj0_body(x_ref, mod_ref, g_ref, *refs):
    n = len(refs) // 2
    m = mod_ref[0, 0]
    h = _norm_mod(x_ref[0], g_ref[...], m[0:1], m[1:2]).astype(BF16)
    for w_ref, o_ref in zip(refs[:n], refs[n:]):
        o_ref[0] = _dot(h, w_ref[...]).astype(o_ref.dtype)


def _inproj0(x_all, mods, norm_g, weights, out_dtypes, nct):
    bsz, t, d = x_all.shape
    x_spec, mod_spec = _tile_specs(nct, d)
    return pl.pallas_call(
        _inproj0_body,
        grid=(bsz, t // TOKEN_TILE),
        in_specs=[x_spec, mod_spec, _const_spec((1, d))] + [_const_spec(w.shape) for w in weights],
        out_specs=[pl.BlockSpec((1, TOKEN_TILE, w.shape[1]), lambda b, i: (b, i, 0)) for w in weights],
        out_shape=[jax.ShapeDtypeStruct((bsz, t, w.shape[1]), dt) for w, dt in zip(weights, out_dtypes)],
        compiler_params=_params("arbitrary", "arbitrary"),
        name="inproj_gla_s5",
    )(x_all, mods, norm_g.reshape(1, d), *weights)


def _rope(acc, cos_ref, sin_ref, o_ref, scale):
    for grp in range(acc.shape[1] // 128):
        half = grp % 2
        xg = acc[:, grp * 128:(grp + 1) * 128]
        cs = cos_ref[:, half * 128:(half + 1) * 128]
        sn = sin_ref[:, half * 128:(half + 1) * 128]
        out = xg * cs + pltpu.roll(xg, 64, 1) * sn
        o_ref[0, :, grp * 128:(grp + 1) * 128] = (out * scale).astype(o_ref.dtype)


def _inproj1_body(x_ref, mod_ref, g_ref, cos_ref, sin_ref, wq, wk, wv, wg, oq, ok, ov, og):
    m = mod_ref[0, 0]
    h = _norm_mod(x_ref[0], g_ref[...], m[0:1], m[1:2]).astype(BF16)
    _rope(_dot(h, wq[...]), cos_ref, sin_ref, oq, 1.0)
    _rope(_dot(h, wk[...]), cos_ref, sin_ref, ok, RET_DK ** -0.5)
    ov[0] = _dot(h, wv[...]).astype(ov.dtype)
    og[0] = _dot(h, wg[...]).astype(og.dtype)


def _inproj1(x_all, mods, norm_g, cos_t, sin_t, weights, nct):
    bsz, t, d = x_all.shape
    x_spec, mod_spec = _tile_specs(nct, d)
    tab_spec = pl.BlockSpec((TOKEN_TILE, RET_DK), lambda b, i: (i, 0))
    return pl.pallas_call(
        _inproj1_body,
        grid=(bsz, t // TOKEN_TILE),
        in_specs=[x_spec, mod_spec, _const_spec((1, d)), tab_spec, tab_spec] + [_const_spec(w.shape) for w in weights],
        out_specs=[pl.BlockSpec((1, TOKEN_TILE, w.shape[1]), lambda b, i: (b, i, 0)) for w in weights],
        out_shape=[jax.ShapeDtypeStruct((bsz, t, w.shape[1]), BF16) for w in weights],
        compiler_params=_params("arbitrary", "arbitrary"),
        name="inproj_retention",
    )(x_all, mods, norm_g.reshape(1, d), cos_t, sin_t, *weights)


def _backward_chunk(n, n_ctx_chunks, n_chunks):
    return jnp.where(n < n_ctx_chunks, n_ctx_chunks - 1 - n, n_chunks - 1 - (n - n_ctx_chunks))


def _gla_body(q_f, k_f, v_f, low_f, q_b, k_b, v_b, low_b, wa_ref, ba_ref, hmask_ref, bdmask_ref,
              o_f, o_b, st_f, st_b):
    c = GLA_CHUNK

    @pl.when(pl.program_id(1) == 0)
    def _():
        st_f[...] = jnp.zeros_like(st_f)
        st_b[...] = jnp.zeros_like(st_b)

    ii = lax.broadcasted_iota(I32, (c, c), 0)
    jj = lax.broadcasted_iota(I32, (c, c), 1)
    r4 = lax.broadcasted_iota(I32, (GLA_HEADS * c, c), 0) & (c - 1)
    c4 = lax.broadcasted_iota(I32, (GLA_HEADS * c, c), 1)
    dirs = ((q_f, k_f, v_f, low_f, o_f, st_f), (q_b, k_b, v_b, low_b, o_b, st_b))
    for d, (q_ref, k_ref, v_ref, low_ref, o_ref, st_ref) in enumerate(dirs):
        seen = (jj <= ii) if d == 0 else (jj >= ii)
        seen4 = (c4 <= r4) if d == 0 else (c4 >= r4)
        z = _dot3(low_ref[0], wa_ref[d]) + ba_ref[d]
        log_a = (jnp.minimum(z, 0.0) - jnp.log1p(jnp.exp(-jnp.abs(z)))) * (1.0 / GLA_TAU)
        tri = jnp.where(seen, 1.0, 0.0).astype(BF16)
        la_hi, la_lo = _split(log_a)
        cum = _dot(tri, la_hi) + _dot(tri, la_lo)
        tot = jnp.sum(log_a, axis=0, keepdims=True)
        k = k_ref[0]
        q_dec = q_ref[0] * (GLA_DK ** -0.5) * jnp.exp(cum)
        k_inv = (k * jnp.exp(-cum)).astype(BF16)
        k_state = (k * jnp.exp(tot - cum)).astype(BF16)
        q_heads = (jnp.concatenate([q_dec] * GLA_HEADS, axis=0) * hmask_ref[...]).astype(BF16)
        scores = jnp.where(seen4, _dot_nt(q_heads, k_inv), 0.0).astype(BF16)
        v = v_ref[0]
        st = st_ref[...]
        o_intra = jnp.concatenate(
            [_dot(scores[h * c:(h + 1) * c], v[:, h * GLA_DV:(h + 1) * GLA_DV]) for h in range(GLA_HEADS)], axis=1)
        o_ref[0] = o_intra + _dot_nt(q_dec.astype(BF16), st.astype(BF16))
        st_ref[...] = st * jnp.exp(tot) + bdmask_ref[...] * _dot_tn(v, k_state)


def _gla(q, k, v, low, wa_pad, ba, n_ctx):
    bsz, t, _ = q.shape
    nc, ncc = t // GLA_CHUNK, n_ctx // GLA_CHUNK
    fwd = lambda b, n: (b, n, 0)
    bwd = lambda b, n: (b, _backward_chunk(n, ncc, nc), 0)
    hmask = (jnp.arange(AB_QK)[:, None] // GLA_CHUNK == jnp.arange(AB_QK)[None, :] // GLA_DK).astype(F32)
    bdmask = (jnp.arange(AB_V)[:, None] // GLA_DV == jnp.arange(AB_QK)[None, :] // GLA_DK).astype(F32)

    def specs(idx):
        return [pl.BlockSpec((1, GLA_CHUNK, AB_QK), idx), pl.BlockSpec((1, GLA_CHUNK, AB_QK), idx),
                pl.BlockSpec((1, GLA_CHUNK, AB_V), idx), pl.BlockSpec((1, GLA_CHUNK, 2 * GLA_RANK), idx)]

    return pl.pallas_call(
        _gla_body,
        grid=(bsz, nc),
        in_specs=specs(fwd) + specs(bwd) + [_const_spec(wa_pad.shape), _const_spec(ba.shape),
                                            _const_spec(hmask.shape), _const_spec(bdmask.shape)],
        out_specs=[pl.BlockSpec((1, GLA_CHUNK, AB_V), fwd), pl.BlockSpec((1, GLA_CHUNK, AB_V), bwd)],
        out_shape=[jax.ShapeDtypeStruct((bsz, t, AB_V), F32)] * 2,
        scratch_shapes=[pltpu.VMEM((AB_V, AB_QK), F32)] * 2,
        compiler_params=_params("arbitrary", "arbitrary"),
        name="gla_scan",
    )(q, k, v, low, q, k, v, low, wa_pad, ba, hmask, bdmask)


def _s5_operators(lam_re, lam_im, log_step, b_re, b_im, c_re, c_im):
    hp = lax.Precision.HIGHEST
    ln = S5_CHUNK
    step = jnp.exp(log_step.astype(F32))[..., None]
    lam_re, lam_im = lam_re.astype(F32), lam_im.astype(F32)
    mag = jnp.exp(lam_re * step)
    a_re, a_im = mag * jnp.cos(lam_im * step), mag * jnp.sin(lam_im * step)
    den = lam_re * lam_re + lam_im * lam_im
    f_re = ((a_re - 1.0) * lam_re + a_im * lam_im) / den
    f_im = (a_im * lam_re - (a_re - 1.0) * lam_im) / den
    bb_re = f_re[..., None] * b_re - f_im[..., None] * b_im
    bb_im = f_re[..., None] * b_im + f_im[..., None] * b_re
    pw_re, pw_im = [jnp.ones_like(a_re)], [jnp.zeros_like(a_im)]
    for _ in range(ln):
        pr, pi = pw_re[-1], pw_im[-1]
        pw_re.append(pr * a_re - pi * a_im)
        pw_im.append(pr * a_im + pi * a_re)
    pw_re, pw_im = jnp.stack(pw_re, 1), jnp.stack(pw_im, 1)
    ca_re = c_re[:, None] * pw_re[:, :, :, None, :] - c_im[:, None] * pw_im[:, :, :, None, :]
    ca_im = c_re[:, None] * pw_im[:, :, :, None, :] + c_im[:, None] * pw_re[:, :, :, None, :]
    kern = (jnp.einsum('dtgcp,dgpe->dtgec', ca_re[:, :ln], bb_re, precision=hp)
            - jnp.einsum('dtgcp,dgpe->dtgec', ca_im[:, :ln], bb_im, precision=hp))
    pos = jnp.arange(ln)
    i_out, j_in = pos[None, :], pos[:, None]
    tz, wx, wy, ac = [], [], [], []
    for d in range(2):
        lag = (i_out - j_in) if d == 0 else (j_in - i_out)
        blk = jnp.where((lag >= 0)[:, :, None, None, None], kern[d][jnp.clip(lag, 0, ln - 1)], 0.0)
        tz.append(blk.transpose(2, 0, 3, 1, 4).reshape(S5_GROUPS, ln * S5_GROUP, ln * S5_GROUP))
        p_in = (ln - 1 - pos) if d == 0 else pos
        ar, ai = pw_re[d][p_in], pw_im[d][p_in]
        x_re = ar[..., None] * bb_re[d][None] - ai[..., None] * bb_im[d][None]
        x_im = ar[..., None] * bb_im[d][None] + ai[..., None] * bb_re[d][None]
        to_rows = lambda m: m.transpose(1, 0, 3, 2).reshape(S5_GROUPS, ln * S5_GROUP, S5_P)
        wx.append(jnp.concatenate([to_rows(x_re), to_rows(x_im), to_rows(x_im), to_rows(x_re)], axis=-1))
        p_out = (pos + 1) if d == 0 else (ln - pos)
        to_cols = lambda m: m.transpose(1, 3, 0, 2).reshape(S5_GROUPS, S5_P, ln * S5_GROUP)
        wy.append(jnp.concatenate([to_cols(ca_re[d][p_out]), -to_cols(ca_im[d][p_out])], axis=1))
        lr, li = pw_re[d][ln], pw_im[d][ln]
        rows = [jnp.concatenate([lr, lr], -1), jnp.concatenate([-li, li], -1), jnp.concatenate([li, -li], -1)]
        ac.append(jnp.stack(rows + [jnp.zeros_like(rows[0])] * 5, axis=1))
    return (jnp.stack(tz).astype(BF16), jnp.stack(wx).astype(BF16), jnp.stack(wy).astype(BF16), jnp.stack(ac))


def _s5_body(ncs_ctx, ncs, rows, u_ref, tz_ref, wx_ref, wy_ref, ac_ref, y_ref, xx_f, xx_b, sin_f, sin_b):
    u = u_ref[0]
    xx_f[...] = _dot(u, wx_ref[0, 0])
    xx_b[...] = _dot(u, wx_ref[1, 0])
    ac_f, ac_b = ac_ref[0, 0], ac_ref[1, 0]
    half = 2 * S5_P

    def advance(ac, s, s_sw, xx):
        return (ac[0:1] * s + ac[1:2] * s_sw + xx[:, :half], ac[0:1] * s_sw + ac[2:3] * s + xx[:, half:])

    def step(n, carry):
        s_f, sw_f, s_b, sw_b = carry
        r_f = pl.multiple_of(n * rows, rows)
        r_b = pl.multiple_of(_backward_chunk(n, ncs_ctx, ncs) * rows, rows)
        sin_f[pl.ds(r_f, rows), :] = s_f
        sin_b[pl.ds(r_b, rows), :] = s_b
        s_f, sw_f = advance(ac_f, s_f, sw_f, xx_f[pl.ds(r_f, rows), :])
        s_b, sw_b = advance(ac_b, s_b, sw_b, xx_b[pl.ds(r_b, rows), :])
        return s_f, sw_f, s_b, sw_b

    zero = jnp.zeros((rows, half), F32)
    lax.fori_loop(0, ncs, step, (zero, zero, zero, zero))
    y_ref[0] = (_dot(u, tz_ref[0, 0]) + _dot(u, tz_ref[1, 0])
                + _dot(sin_f[...].astype(BF16), wy_ref[0, 0]) + _dot(sin_b[...].astype(BF16), wy_ref[1, 0]))


def _s5(u, ops, n_ctx):
    tz, wx, wy, ac = ops
    bsz, t, _ = u.shape
    ln, lanes = S5_CHUNK, S5_CHUNK * S5_GROUP
    ncs, ncs_ctx = t // ln, n_ctx // ln
    m = ncs * bsz
    ug = u.astype(BF16).reshape(bsz, ncs, ln, S5_GROUPS, S5_GROUP).transpose(3, 1, 0, 2, 4).reshape(S5_GROUPS, m, lanes)
    dir_spec = lambda shape: pl.BlockSpec((2, 1) + shape, lambda g: (0, g, 0, 0))
    y = pl.pallas_call(
        functools.partial(_s5_body, ncs_ctx, ncs, bsz),
        grid=(S5_GROUPS,),
        in_specs=[pl.BlockSpec((1, m, lanes), lambda g: (g, 0, 0)), dir_spec((lanes, lanes)),
                  dir_spec((lanes, 4 * S5_P)), dir_spec((2 * S5_P, lanes)), dir_spec((8, 2 * S5_P))],
        out_specs=pl.BlockSpec((1, m, lanes), lambda g: (g, 0, 0)),
        out_shape=jax.ShapeDtypeStruct((S5_GROUPS, m, lanes), F32),
        scratch_shapes=[pltpu.VMEM((m, 4 * S5_P), F32)] * 2 + [pltpu.VMEM((m, 2 * S5_P), F32)] * 2,
        compiler_params=_params("arbitrary"),
        name="s5_scan",
    )(ug, tz, wx, wy, ac)
    return y.reshape(S5_GROUPS, ncs, bsz, ln, S5_GROUP).transpose(2, 1, 3, 0, 4).reshape(bsz, t, S5_CH)


def _ret_body(ncc, q_f, k_f, v_f, q_b, k_b, v_b, dmat_ref, rsc_ref, csc_ref, gam_ref, o_f, o_b, st_f, st_b):
    n = pl.program_id(1)

    @pl.when(n == 0)
    def _():
        st_f[...] = jnp.zeros_like(st_f)
        st_b[...] = jnp.zeros_like(st_b)

    dirs = ((q_f, k_f, v_f, o_f, st_f), (q_b, k_b, v_b, o_b, st_b))

    @pl.when(n >= ncc)
    def _():
        for d, (q_ref, k_ref, v_ref, o_ref, st_ref) in enumerate(dirs):
            for h in range(RET_HEADS):
                qh = q_ref[0, :, h * RET_DK:(h + 1) * RET_DK]
                kh = k_ref[0, :, h * RET_DK:(h + 1) * RET_DK]
                vh = v_ref[0, :, h * RET_DV:(h + 1) * RET_DV]
                scores = (_dot_nt(qh, kh) * dmat_ref[d, h]).astype(BF16)
                o = _dot(scores, vh) + rsc_ref[d, h] * _dot(qh, st_ref[h].astype(BF16))
                o_ref[0, :, h * RET_DV:(h + 1) * RET_DV] = o.astype(o_ref.dtype)

    for d, (q_ref, k_ref, v_ref, o_ref, st_ref) in enumerate(dirs):
        for h in range(RET_HEADS):
            kh = k_ref[0, :, h * RET_DK:(h + 1) * RET_DK]
            vh = v_ref[0, :, h * RET_DV:(h + 1) * RET_DV]
            k_state = (kh.astype(F32) * csc_ref[d, h]).astype(BF16)
            st_ref[h] = st_ref[h] * gam_ref[d, h] + _dot_tn(k_state, vh)


def _retention(q, k, v, decay_logit, n_ctx):
    bsz, t, _ = q.shape
    c = RET_CHUNK
    nc, ncc = t // c, n_ctx // c
    nl = nc - ncc
    log_gamma = jax.nn.log_sigmoid(decay_logit.astype(F32))[:, :, None, None]
    i = jnp.arange(c, dtype=F32)
    lag = i[:, None] - i[None, :]
    lag = jnp.stack([lag, -lag])[:, None]
    dmat = jnp.where(lag >= 0, jnp.exp(log_gamma * jnp.maximum(lag, 0.0)), 0.0)
    done = jnp.stack([i + 1.0, c - i])[:, None, :, None]
    rsc = jnp.exp(log_gamma * done)
    csc = jnp.exp(log_gamma * (c - done))
    gam = jnp.exp(log_gamma[:, :, 0, 0] * c)
    fwd = lambda b, n: (b, n, 0)
    bwd = lambda b, n: (b, _backward_chunk(n, ncc, nc), 0)
    o_fwd = lambda b, n: (b, jnp.maximum(n - ncc, 0), 0)
    o_bwd = lambda b, n: (b, nl - 1 - jnp.maximum(n - ncc, 0), 0)

    def specs(idx):
        return [pl.BlockSpec((1, c, RET_QK), idx), pl.BlockSpec((1, c, RET_QK), idx), pl.BlockSpec((1, c, RET_MIX), idx)]

    return pl.pallas_call(
        functools.partial(_ret_body, ncc),
        grid=(bsz, nc),
        in_specs=specs(fwd) + specs(bwd) + [_const_spec(dmat.shape), _const_spec(rsc.shape), _const_spec(csc.shape),
                                            pl.BlockSpec(memory_space=pltpu.SMEM)],
        out_specs=[pl.BlockSpec((1, c, RET_MIX), o_fwd), pl.BlockSpec((1, c, RET_MIX), o_bwd)],
        out_shape=[jax.ShapeDtypeStruct((bsz, nl * c, RET_MIX), BF16)] * 2,
        scratch_shapes=[pltpu.VMEM((RET_HEADS, RET_DK, RET_DV), F32)] * 2,
        compiler_params=_params("arbitrary", "arbitrary"),
        name="retention_scan",
    )(q, k, v, q, k, v, dmat, rsc, csc, gam)


def _route(x, mixed, mod, n2g_ref, wr_ref, br_ref, x1_ref, h2_ref, e_ref, w_ref, r_ref, cnt_ref):
    tm = x.shape[0]
    x1 = x + mod[2:3] * mixed
    x1_ref[0] = x1
    h2 = _norm_mod(x1, n2g_ref[...], mod[3:4], mod[4:5])
    h2_ref[0] = h2.astype(BF16)
    logits = _dot3(wr_ref[...], h2, dot=_dot_nt) + br_ref[...]
    ie = lax.broadcasted_iota(I32, logits.shape, 0)
    tops, picks = [], []
    for _ in range(TOP_K):
        mx = jnp.max(logits, axis=0, keepdims=True)
        pick = jnp.min(jnp.where(logits == mx, ie, N_EXPERTS), axis=0, keepdims=True)
        tops.append(mx)
        picks.append(pick)
        logits = jnp.where(ie == pick, -jnp.inf, logits)
    ex = [jnp.exp(tk - tops[0]) for tk in tops]
    den = ex[0] + ex[1] + ex[2] + ex[3]
    for kk in range(TOP_K):
        w_ref[0, kk:kk + 1, :] = ex[kk] / den
        e_ref[0, kk:kk + 1, :] = picks[kk]

    @pl.when((pl.program_id(0) == 0) & (pl.program_id(1) == 0))
    def _():
        cnt_ref[...] = jnp.zeros_like(cnt_ref)

    earlier = (lax.broadcasted_iota(I32, (tm, tm), 0) < lax.broadcasted_iota(I32, (tm, tm), 1))
    earlier = jnp.where(earlier, 1.0, 0.0).astype(BF16)
    run = cnt_ref[:, 0:1]
    for kk, pick in enumerate(picks):
        onehot = jnp.where(ie == pick, 1.0, 0.0)
        before = _dot(onehot.astype(BF16), earlier) + run
        r_ref[0, kk:kk + 1, :] = jnp.sum(onehot * before, axis=0, keepdims=True).astype(I32)
        run = run + jnp.sum(onehot, axis=1, keepdims=True)
    cnt_ref[...] = jnp.broadcast_to(run, cnt_ref.shape)


def _mix0_body(x_ref, mod_ref, of_ref, ob_ref, g_ref, ys_ref, u_ref, gng_ref, dsk_ref, gluw_ref, glub_ref,
               wo_ref, n2g_ref, wr_ref, br_ref, x1_ref, h2_ref, e_ref, w_ref, r_ref, cnt_ref):
    o = of_ref[0] + ob_ref[0]
    heads = []
    for h in range(GLA_HEADS):
        oh = o[:, h * GLA_DV:(h + 1) * GLA_DV]
        heads.append(oh * lax.rsqrt(jnp.mean(oh * oh, axis=-1, keepdims=True) + EPS))
    gla = jnp.concatenate(heads, axis=1) * gng_ref[...] * _silu(g_ref[0].astype(F32))
    y = jax.nn.gelu(ys_ref[0] + dsk_ref[...] * u_ref[0].astype(F32))
    y = y * jax.nn.sigmoid(_dot(y.astype(BF16), gluw_ref[...]) + glub_ref[...])
    mixed = _dot(gla.astype(BF16), wo_ref[0:AB_V]) + _dot(y.astype(BF16), wo_ref[AB_V:AB_V + S5_CH])
    _route(x_ref[0], mixed, mod_ref[0, 0], n2g_ref, wr_ref, br_ref, x1_ref, h2_ref, e_ref, w_ref, r_ref, cnt_ref)


def _mix1_body(x_ref, mod_ref, of_ref, ob_ref, g_ref, ng_ref, wo_ref, n2g_ref, wr_ref, br_ref,
               x1_ref, h2_ref, e_ref, w_ref, r_ref, cnt_ref):
    mixed = None
    for h in range(RET_HEADS):
        sl = slice(h * RET_DV, (h + 1) * RET_DV)
        oh = of_ref[0, :, sl].astype(F32) + ob_ref[0, :, sl].astype(F32)
        mu = jnp.mean(oh, axis=-1, keepdims=True)
        cen = oh - mu
        var = jnp.mean(cen * cen, axis=-1, keepdims=True)
        gated = cen * lax.rsqrt(var + EPS) * ng_ref[:, sl] * _silu(g_ref[0, :, sl].astype(F32))
        part = _dot(gated.astype(BF16), wo_ref[sl])
        mixed = part if mixed is None else mixed + part
    _route(x_ref[0], mixed, mod_ref[0, 0], n2g_ref, wr_ref, br_ref, x1_ref, h2_ref, e_ref, w_ref, r_ref, cnt_ref)


def _mix_call(body, name, x_all, mods, tiles, acts, consts, norm2_g, w_router, b_router, n_tok, seg_tile0):
    bsz, _, d = x_all.shape
    tm = TOKEN_TILE
    off = lambda b, i: (b, i + seg_tile0, 0)
    loc = lambda b, i: (b, i, 0)
    ntl = bsz * tiles
    flat = lambda b, i: (b * tiles + i, 0, 0)
    in_specs = [pl.BlockSpec((1, tm, d), off),
                pl.BlockSpec((1, 1, 6, d), lambda b, i: (b, ((i + seg_tile0) >= n_tok).astype(I32), 0, 0))]
    args = [x_all, mods]
    for arr, offset in acts:
        in_specs.append(pl.BlockSpec((1, tm, arr.shape[2]), off if offset else loc))
        args.append(arr)
    tail = list(consts) + [norm2_g.reshape(1, d), w_router.T, b_router.reshape(N_EXPERTS, 1)]
    in_specs += [_const_spec(a.shape) for a in tail]
    args += tail
    tok_out = pl.BlockSpec((1, TOP_K, tm), flat)
    return pl.pallas_call(
        body,
        grid=(bsz, tiles),
        in_specs=in_specs,
        out_specs=[pl.BlockSpec((1, tm, d), loc), pl.BlockSpec((1, tm, d), loc), tok_out, tok_out, tok_out,
                   _const_spec((N_EXPERTS, 128))],
        out_shape=[jax.ShapeDtypeStruct((bsz, tiles * tm, d), F32), jax.ShapeDtypeStruct((bsz, tiles * tm, d), BF16),
                   jax.ShapeDtypeStruct((ntl, TOP_K, tm), I32), jax.ShapeDtypeStruct((ntl, TOP_K, tm), F32),
                   jax.ShapeDtypeStruct((ntl, TOP_K, tm), I32), jax.ShapeDtypeStruct((N_EXPERTS, 128), F32)],
        compiler_params=_params("arbitrary", "arbitrary"),
        name=name,
    )(*args)


def _expert_body(be_ref, nu_ref, x_ref, wgu_ref, bgu_ref, wd_ref, bd_ref, o_ref):
    i = pl.program_id(0)

    @pl.when(i < nu_ref[0])
    def _():
        gu = _dot(x_ref[...], wgu_ref[0]) + bgu_ref[0]
        gate = jnp.minimum(gu[:, :D_FF], SWIGLU_LIMIT)
        lin = jnp.clip(gu[:, D_FF:], -SWIGLU_LIMIT, SWIGLU_LIMIT)
        act = gate * jax.nn.sigmoid(SWIGLU_ALPHA * gate) * (lin + 1.0)
        o_ref[...] = _dot(act.astype(BF16), wd_ref[0]) + bd_ref[0]

    @pl.when(i >= nu_ref[0])
    def _():
        o_ref[...] = jnp.zeros_like(o_ref)


def _experts(xb, block_e, n_used, w_gu, b_gu, w_down, b_down):
    n_slots, d = xb.shape
    n_blocks = n_slots // MOE_BLOCK
    by_expert = lambda i, be, nu: (be[i], 0, 0)
    return pl.pallas_call(
        _expert_body,
        grid_spec=pltpu.PrefetchScalarGridSpec(
            num_scalar_prefetch=2,
            grid=(n_blocks,),
            in_specs=[pl.BlockSpec((MOE_BLOCK, d), lambda i, be, nu: (i, 0)),
                      pl.BlockSpec((1, d, 2 * D_FF), by_expert), pl.BlockSpec((1, 1, 2 * D_FF), by_expert),
                      pl.BlockSpec((1, D_FF, d), by_expert), pl.BlockSpec((1, 1, d), by_expert)],
            out_specs=pl.BlockSpec((MOE_BLOCK, d), lambda i, be, nu: (i, 0))),
        out_shape=jax.ShapeDtypeStruct((n_slots, d), F32),
        compiler_params=_params("arbitrary"),
        name="moe_experts",
    )(block_e, n_used, xb, w_gu, b_gu.reshape(N_EXPERTS, 1, 2 * D_FF), w_down, b_down.reshape(N_EXPERTS, 1, d))


def _combine_body(final, x1_ref, mod_ref, yk_ref, w_ref, *refs):
    y = yk_ref[0, 0] * w_ref[0, :, 0:1]
    for k in range(1, TOP_K):
        y = y + yk_ref[k, 0] * w_ref[0, :, k:k + 1]
    x2 = x1_ref[0] + mod_ref[0, 0][5:6] * y
    if final:
        fg_ref, o_ref = refs
        x2 = x2 * lax.rsqrt(jnp.mean(x2 * x2, axis=-1, keepdims=True) + EPS) * fg_ref[...]
    else:
        (o_ref,) = refs
    o_ref[0] = x2


def _combine(x1, mods, yk, w_tok, seg_tile0, n_tok, final_g):
    bsz, t, d = x1.shape
    tm = TOKEN_TILE
    tiles = t // tm
    loc = lambda b, i: (b, i, 0)
    in_specs = [pl.BlockSpec((1, tm, d), loc),
                pl.BlockSpec((1, 1, 6, d), lambda b, i: (b, ((i + seg_tile0) >= n_tok).astype(I32), 0, 0)),
                pl.BlockSpec((TOP_K, 1, tm, d), lambda b, i: (0, b, i, 0)),
                pl.BlockSpec((1, tm, TOP_K), loc)]
    args = [x1, mods, yk.reshape(TOP_K, bsz, t, d), w_tok.reshape(bsz, t, TOP_K)]
    if final_g is not None:
        in_specs.append(_const_spec((1, d)))
        args.append(final_g.reshape(1, d))
    return pl.pallas_call(
        functools.partial(_combine_body, final_g is not None),
        grid=(bsz, tiles),
        in_specs=in_specs,
        out_specs=pl.BlockSpec((1, tm, d), loc),
        out_shape=jax.ShapeDtypeStruct((bsz, t, d), F32),
        compiler_params=_params("arbitrary", "arbitrary"),
        name="moe_combine",
    )(*args)


def _moe(h2, e_tl, w_tl, r_tl, cnt, w_gu, b_gu, w_down, b_down):
    bsz, t, d = h2.shape
    n = bsz * t
    flat = lambda a: a.transpose(1, 0, 2).reshape(TOP_K, n)
    e_k, w_k, r_k = flat(e_tl), flat(w_tl), flat(r_tl)
    counts = cnt[:, 0].astype(I32)
    padded = (counts + MOE_BLOCK - 1) // MOE_BLOCK * MOE_BLOCK
    pad_end = jnp.cumsum(padded)
    pad_start = pad_end - padded
    n_blocks = (n * TOP_K + MOE_BLOCK - 1) // MOE_BLOCK + N_EXPERTS
    block_e = jnp.minimum(jnp.searchsorted(pad_end, jnp.arange(n_blocks, dtype=I32) * MOE_BLOCK, side='right'),
                          N_EXPERTS - 1).astype(I32)
    n_used = (pad_end[-1:] // MOE_BLOCK).astype(I32)
    dest = pad_start[e_k] + r_k
    tok = jnp.broadcast_to(jnp.arange(n, dtype=I32)[None, :], (TOP_K, n))
    slot_tok = jnp.zeros((n_blocks * MOE_BLOCK,), I32).at[dest.reshape(-1)].set(tok.reshape(-1))
    xb = h2.reshape(n, d)[slot_tok]
    yb = _experts(xb, block_e, n_used, w_gu, b_gu, w_down, b_down)
    return yb[dest], w_k.T


def _rope_tables(n_ctx, n_lat):
    n_freq = RET_DK // 4
    inv_freq = ROPE_BASE ** (-jnp.arange(n_freq, dtype=F32) / n_freq)
    pos = jnp.arange(n_lat, dtype=I32)
    cos, sin = [], []
    for p in (pos // GRID_W, pos % GRID_W):
        ang = p.astype(F32)[:, None] * inv_freq
        cos += [jnp.cos(ang), jnp.cos(ang)]
        sin += [-jnp.sin(ang), jnp.sin(ang)]
    cos, sin = jnp.concatenate(cos, axis=1), jnp.concatenate(sin, axis=1)
    return (jnp.concatenate([jnp.ones((n_ctx, RET_DK), F32), cos], axis=0),
            jnp.concatenate([jnp.zeros((n_ctx, RET_DK), F32), sin], axis=0))


def kernel(x, c, ctx, c_ctx, ada_w, ada_b, norm1_g, norm2_g, ab_w_in, ab_w_out, gla_wa, gla_ba, gla_norm_g, s5_lam_re, s5_lam_im, s5_log_step, s5_b_re, s5_b_im, s5_c_re, s5_c_im, s5_d, s5_glu_w, s5_glu_b, ret_w_in, ret_w_out, ret_decay_logit, ret_norm_g, moe_w_router, moe_b_router, moe_w_gu, moe_b_gu, moe_w_down, moe_b_down, final_norm_g):
    bsz, n_lat, d = x.shape
    n_ctx = ctx.shape[1]
    depth = ada_w.shape[0]
    assert depth == 2 and d == D_MODEL and bsz == 8, "kernels are laid out for the stated problem shape"
    assert n_ctx % TOKEN_TILE == 0 and n_lat % TOKEN_TILE == 0 and n_lat % GRID_W == 0
    t = n_ctx + n_lat
    nct = n_ctx // TOKEN_TILE

    cvec = jnp.zeros((16, d), F32).at[:bsz].set(c).at[bsz].set(c_ctx)
    mod = _ada_mod(cvec, ada_w, ada_b).reshape(depth, 16, 6, d)
    mods = [jnp.stack([jnp.broadcast_to(mod[l, bsz], (bsz, 6, d)), mod[l, :bsz]], axis=1) for l in range(depth)]

    x_all = jnp.concatenate([ctx, x], axis=1)

    w_in = ab_w_in[0].astype(BF16)
    cuts = [0, AB_QK, 2 * AB_QK, 2 * AB_QK + AB_V, 2 * AB_QK + 2 * AB_V, 2 * AB_QK + 2 * AB_V + 2 * GLA_RANK,
            w_in.shape[1]]
    pieces = [w_in[:, a:b] for a, b in zip(cuts[:-1], cuts[1:])]
    q, k, v, g, low, u = _inproj0(x_all, mods[0], norm1_g[0], pieces, [F32, F32, BF16, BF16, F32, F32], nct)
    wa_pad = jnp.zeros((2, 2 * GLA_RANK, AB_QK), F32)
    wa_pad = wa_pad.at[0, :GLA_RANK].set(gla_wa[0, 0]).at[1, GLA_RANK:].set(gla_wa[0, 1])
    o_f, o_b = _gla(q, k, v, low, wa_pad, gla_ba[0].reshape(2, 1, AB_QK), n_ctx)
    ops = _s5_operators(s5_lam_re[0], s5_lam_im[0], s5_log_step[0], s5_b_re[0], s5_b_im[0], s5_c_re[0], s5_c_im[0])
    ys = _s5(u, ops, n_ctx)
    consts = [jnp.tile(gla_norm_g[0], GLA_HEADS).reshape(1, AB_V), s5_d[0].reshape(1, S5_CH),
              s5_glu_w[0].astype(BF16), s5_glu_b[0].reshape(1, S5_CH), ab_w_out[0].astype(BF16)]
    x1, h2, e_tl, w_tl, r_tl, cnt = _mix_call(
        _mix0_body, "mix_gla_s5", x_all, mods[0], t // TOKEN_TILE,
        [(o_f, False), (o_b, False), (g, False), (ys, False), (u, False)], consts,
        norm2_g[0], moe_w_router[0], moe_b_router[0], nct, 0)
    yk, w_tok = _moe(h2, e_tl, w_tl, r_tl, cnt, moe_w_gu[0].astype(BF16), moe_b_gu[0],
                     moe_w_down[0].astype(BF16), moe_b_down[0])
    x_all = _combine(x1, mods[0], yk, w_tok, 0, nct, None)

    w_in = ret_w_in[0].astype(BF16)
    cuts = [0, RET_QK, 2 * RET_QK, 2 * RET_QK + RET_MIX, w_in.shape[1]]
    pieces = [w_in[:, a:b] for a, b in zip(cuts[:-1], cuts[1:])]
    cos_t, sin_t = _rope_tables(n_ctx, n_lat)
    q, k, v, g = _inproj1(x_all, mods[1], norm1_g[1], cos_t, sin_t, pieces, nct)
    o_f, o_b = _retention(q, k, v, ret_decay_logit[0], n_ctx)
    consts = [ret_norm_g[0].reshape(1, RET_MIX), ret_w_out[0].astype(BF16)]
    x1, h2, e_tl, w_tl, r_tl, cnt = _mix_call(
        _mix1_body, "mix_retention", x_all, mods[1], n_lat // TOKEN_TILE,
        [(o_f, False), (o_b, False), (g, True)], consts,
        norm2_g[1], moe_w_router[1], moe_b_router[1], nct, nct)
    yk, w_tok = _moe(h2, e_tl, w_tl, r_tl, cnt, moe_w_gu[1].astype(BF16), moe_b_gu[1],
                     moe_w_down[1].astype(BF16), moe_b_down[1])
    return _combine(x1, mods[1], yk, w_tok, nct, nct, final_norm_g)
```

```python
import functools
import math

import jax
import jax.numpy as jnp
from jax import lax
from jax.experimental import pallas as pl
from jax.experimental.pallas import tpu as pltpu

F32, BF16, I32 = jnp.float32, jnp.bfloat16, jnp.int32

D_MODEL = 1024
GRID_W = 64
EPS = 1e-6
GLA_HEADS, GLA_DK, GLA_DV, GLA_RANK, GLA_TAU, GLA_CHUNK = 4, 64, 128, 16, 16.0, 64
AB_QK, AB_V = GLA_HEADS * GLA_DK, GLA_HEADS * GLA_DV
S5_CH, S5_GROUP, S5_GROUPS, S5_P = 512, 16, 32, 64
S5_CHUNK = 16
RET_HEADS, RET_DK, RET_DV, RET_CHUNK = 4, 256, 512, 128
RET_QK, RET_MIX = RET_HEADS * RET_DK, RET_HEADS * RET_DV
ROPE_BASE = 10000.0
N_EXPERTS, TOP_K, D_FF = 32, 4, 1024
SWIGLU_LIMIT, SWIGLU_ALPHA = 7.0, 1.702
MOE_BLOCK = 256
TOKEN_TILE = 256
ADA_TILE = 768
VMEM_LIMIT = 56 * 1024 * 1024


def _params(*sem):
    return pltpu.CompilerParams(dimension_semantics=sem, vmem_limit_bytes=VMEM_LIMIT)


def _dot(a, b):
    return jnp.dot(a, b, preferred_element_type=F32)


def _dot_nt(a, b):
    return lax.dot_general(a, b, (((1,), (1,)), ((), ())), preferred_element_type=F32)


def _dot_tn(a, b):
    return lax.dot_general(a, b, (((0,), (0,)), ((), ())), preferred_element_type=F32)


def _split(a):
    hi = a.astype(BF16)
    return hi, (a - hi.astype(F32)).astype(BF16)


def _dot3(a, b, dot=_dot):
    ah, al = _split(a)
    bh, bl = _split(b)
    return dot(ah, bh) + (dot(ah, bl) + dot(al, bh))


def _silu(x):
    return x * jax.nn.sigmoid(x)


def _norm_mod(x, g, shift, scale):
    r = lax.rsqrt(jnp.mean(x * x, axis=-1, keepdims=True) + EPS)
    return (x * r * g) * (1.0 + scale) + shift


def _const_spec(shape):
    nd = len(shape)
    return pl.BlockSpec(shape, lambda *_: (0,) * nd)


def _ada_body(c_ref, w_ref, b_ref, o_ref):
    o_ref[0] = _dot3(_silu(c_ref[...]), w_ref[0]) + b_ref[0]


def _ada_mod(cvec, ada_w, ada_b):
    depth, d, n6 = ada_w.shape
    rows = cvec.shape[0]
    return pl.pallas_call(
        _ada_body,
        grid=(depth, n6 // ADA_TILE),
        in_specs=[_const_spec((rows, d)),
                  pl.BlockSpec((1, d, ADA_TILE), lambda l, j: (l, 0, j)),
                  pl.BlockSpec((1, 1, ADA_TILE), lambda l, j: (l, 0, j))],
        out_specs=pl.BlockSpec((1, rows, ADA_TILE), lambda l, j: (l, 0, j)),
        out_shape=jax.ShapeDtypeStruct((depth, rows, n6), F32),
        compiler_params=_params("arbitrary", "arbitrary"),
        name="ada_mod",
    )(cvec, ada_w, ada_b.reshape(depth, 1, n6))


def _tile_specs(nct, d):
    x_spec = pl.BlockSpec((1, TOKEN_TILE, d), lambda b, i: (b, i, 0))
    mod_spec = pl.BlockSpec((1, 1, 6, d), lambda b, i: (b, (i >= nct).astype(I32), 0, 0))
    return x_spec, mod_spec


def _inproj0_body(x_ref, mod_ref, g_ref, *refs):
    n = len(refs) // 2
    m = mod_ref[0, 0]
    h = _norm_mod(x_ref[0], g_ref[...], m[0:1], m[1:2]).astype(BF16)
    for w_ref, o_ref in zip(refs[:n], refs[n:]):
        o_ref[0] = _dot(h, w_ref[...]).astype(o_ref.dtype)


def _inproj0(x_all, mods, norm_g, weights, out_dtypes, nct):
    bsz, t, d = x_all.shape
    x_spec, mod_spec = _tile_specs(nct, d)
    return pl.pallas_call(
        _inproj0_body,
        grid=(bsz, t // TOKEN_TILE),
        in_specs=[x_spec, mod_spec, _const_spec((1, d))] + [_const_spec(w.shape) for w in weights],
        out_specs=[pl.BlockSpec((1, TOKEN_TILE, w.shape[1]), lambda b, i: (b, i, 0)) for w in weights],
        out_shape=[jax.ShapeDtypeStruct((bsz, t, w.shape[1]), dt) for w, dt in zip(weights, out_dtypes)],
        compiler_params=_params("arbitrary", "arbitrary"),
        name="inproj_gla_s5",
    )(x_all, mods, norm_g.reshape(1, d), *weights)


def _rope(acc, cos_ref, sin_ref, o_ref, scale):
    for grp in range(acc.shape[1] // 128):
        half = grp % 2
        xg = acc[:, grp * 128:(grp + 1) * 128]
        cs = cos_ref[:, half * 128:(half + 1) * 128]
        sn = sin_ref[:, half * 128:(half + 1) * 128]
        out = xg * cs + pltpu.roll(xg, 64, 1) * sn
        o_ref[0, :, grp * 128:(grp + 1) * 128] = (out * scale).astype(o_ref.dtype)


def _inproj1_body(x_ref, mod_ref, g_ref, cos_ref, sin_ref, wq, wk, wv, wg, oq, ok, ov, og):
    m = mod_ref[0, 0]
    h = _norm_mod(x_ref[0], g_ref[...], m[0:1], m[1:2]).astype(BF16)
    _rope(_dot(h, wq[...]), cos_ref, sin_ref, oq, 1.0)
    _rope(_dot(h, wk[...]), cos_ref, sin_ref, ok, RET_DK ** -0.5)
    ov[0] = _dot(h, wv[...]).astype(ov.dtype)
    og[0] = _dot(h, wg[...]).astype(og.dtype)


def _inproj1(x_all, mods, norm_g, cos_t, sin_t, weights, nct):
    bsz, t, d = x_all.shape
    x_spec, mod_spec = _tile_specs(nct, d)
    tab_spec = pl.BlockSpec((TOKEN_TILE, RET_DK), lambda b, i: (i, 0))
    return pl.pallas_call(
        _inproj1_body,
        grid=(bsz, t // TOKEN_TILE),
        in_specs=[x_spec, mod_spec, _const_spec((1, d)), tab_spec, tab_spec] + [_const_spec(w.shape) for w in weights],
        out_specs=[pl.BlockSpec((1, TOKEN_TILE, w.shape[1]), lambda b, i: (b, i, 0)) for w in weights],
        out_shape=[jax.ShapeDtypeStruct((bsz, t, w.shape[1]), BF16) for w in weights],
        compiler_params=_params("arbitrary", "arbitrary"),
        name="inproj_retention",
    )(x_all, mods, norm_g.reshape(1, d), cos_t, sin_t, *weights)


def _backward_chunk(n, n_ctx_chunks, n_chunks):
    return jnp.where(n < n_ctx_chunks, n_ctx_chunks - 1 - n, n_chunks - 1 - (n - n_ctx_chunks))


def _gla_body(q_f, k_f, v_f, low_f, q_b, k_b, v_b, low_b, wa_ref, ba_ref, hmask_ref, bdmask_ref,
              o_f, o_b, st_f, st_b):
    c = GLA_CHUNK

    @pl.when(pl.program_id(1) == 0)
    def _():
        st_f[...] = jnp.zeros_like(st_f)
        st_b[...] = jnp.zeros_like(st_b)

    ii = lax.broadcasted_iota(I32, (c, c), 0)
    jj = lax.broadcasted_iota(I32, (c, c), 1)
    r4 = lax.broadcasted_iota(I32, (GLA_HEADS * c, c), 0) & (c - 1)
    c4 = lax.broadcasted_iota(I32, (GLA_HEADS * c, c), 1)
    dirs = ((q_f, k_f, v_f, low_f, o_f, st_f), (q_b, k_b, v_b, low_b, o_b, st_b))
    for d, (q_ref, k_ref, v_ref, low_ref, o_ref, st_ref) in enumerate(dirs):
        seen = (jj <= ii) if d == 0 else (jj >= ii)
        seen4 = (c4 <= r4) if d == 0 else (c4 >= r4)
        z = _dot3(low_ref[0], wa_ref[d]) + ba_ref[d]
        log_a = (jnp.minimum(z, 0.0) - jnp.log1p(jnp.exp(-jnp.abs(z)))) * (1.0 / GLA_TAU)
        tri = jnp.where(seen, 1.0, 0.0).astype(BF16)
        la_hi, la_lo = _split(log_a)
        cum = _dot(tri, la_hi) + _dot(tri, la_lo)
        tot = jnp.sum(log_a, axis=0, keepdims=True)
        k = k_ref[0]
        q_dec = q_ref[0] * (GLA_DK ** -0.5) * jnp.exp(cum)
        k_inv = (k * jnp.exp(-cum)).astype(BF16)
        k_state = (k * jnp.exp(tot - cum)).astype(BF16)
        q_heads = (jnp.concatenate([q_dec] * GLA_HEADS, axis=0) * hmask_ref[...]).astype(BF16)
        scores = jnp.where(seen4, _dot_nt(q_heads, k_inv), 0.0).astype(BF16)
        v = v_ref[0]
        st = st_ref[...]
        o_intra = jnp.concatenate(
            [_dot(scores[h * c:(h + 1) * c], v[:, h * GLA_DV:(h + 1) * GLA_DV]) for h in range(GLA_HEADS)], axis=1)
        o_ref[0] = o_intra + _dot_nt(q_dec.astype(BF16), st.astype(BF16))
        st_ref[...] = st * jnp.exp(tot) + bdmask_ref[...] * _dot_tn(v, k_state)


def _gla(q, k, v, low, wa_pad, ba, n_ctx):
    bsz, t, _ = q.shape
    nc, ncc = t // GLA_CHUNK, n_ctx // GLA_CHUNK
    fwd = lambda b, n: (b, n, 0)
    bwd = lambda b, n: (b, _backward_chunk(n, ncc, nc), 0)
    hmask = (jnp.arange(AB_QK)[:, None] // GLA_CHUNK == jnp.arange(AB_QK)[None, :] // GLA_DK).astype(F32)
    bdmask = (jnp.arange(AB_V)[:, None] // GLA_DV == jnp.arange(AB_QK)[None, :] // GLA_DK).astype(F32)

    def specs(idx):
        return [pl.BlockSpec((1, GLA_CHUNK, AB_QK), idx), pl.BlockSpec((1, GLA_CHUNK, AB_QK), idx),
                pl.BlockSpec((1, GLA_CHUNK, AB_V), idx), pl.BlockSpec((1, GLA_CHUNK, 2 * GLA_RANK), idx)]

    return pl.pallas_call(
        _gla_body,
        grid=(bsz, nc),
        in_specs=specs(fwd) + specs(bwd) + [_const_spec(wa_pad.shape), _const_spec(ba.shape),
                                            _const_spec(hmask.shape), _const_spec(bdmask.shape)],
        out_specs=[pl.BlockSpec((1, GLA_CHUNK, AB_V), fwd), pl.BlockSpec((1, GLA_CHUNK, AB_V), bwd)],
        out_shape=[jax.ShapeDtypeStruct((bsz, t, AB_V), F32)] * 2,
        scratch_shapes=[pltpu.VMEM((AB_V, AB_QK), F32)] * 2,
        compiler_params=_params("arbitrary", "arbitrary"),
        name="gla_scan",
    )(q, k, v, low, q, k, v, low, wa_pad, ba, hmask, bdmask)


def _s5_operators(lam_re, lam_im, log_step, b_re, b_im, c_re, c_im):
    hp = lax.Precision.HIGHEST
    ln = S5_CHUNK
    step = jnp.exp(log_step.astype(F32))[..., None]
    lam_re, lam_im = lam_re.astype(F32), lam_im.astype(F32)
    mag = jnp.exp(lam_re * step)
    a_re, a_im = mag * jnp.cos(lam_im * step), mag * jnp.sin(lam_im * step)
    den = lam_re * lam_re + lam_im * lam_im
    f_re = ((a_re - 1.0) * lam_re + a_im * lam_im) / den
    f_im = (a_im * lam_re - (a_re - 1.0) * lam_im) / den
    bb_re = f_re[..., None] * b_re - f_im[..., None] * b_im
    bb_im = f_re[..., None] * b_im + f_im[..., None] * b_re
    pw_re, pw_im = [jnp.ones_like(a_re)], [jnp.zeros_like(a_im)]
    for _ in range(ln):
        pr, pi = pw_re[-1], pw_im[-1]
        pw_re.append(pr * a_re - pi * a_im)
        pw_im.append(pr * a_im + pi * a_re)
    pw_re, pw_im = jnp.stack(pw_re, 1), jnp.stack(pw_im, 1)
    ca_re = c_re[:, None] * pw_re[:, :, :, None, :] - c_im[:, None] * pw_im[:, :, :, None, :]
    ca_im = c_re[:, None] * pw_im[:, :, :, None, :] + c_im[:, None] * pw_re[:, :, :, None, :]
    kern = (jnp.einsum('dtgcp,dgpe->dtgec', ca_re[:, :ln], bb_re, precision=hp)
            - jnp.einsum('dtgcp,dgpe->dtgec', ca_im[:, :ln], bb_im, precision=hp))
    pos = jnp.arange(ln)
    i_out, j_in = pos[None, :], pos[:, None]
    tz, wx, wy, ac = [], [], [], []
    for d in range(2):
        lag = (i_out - j_in) if d == 0 else (j_in - i_out)
        blk = jnp.where((lag >= 0)[:, :, None, None, None], kern[d][jnp.clip(lag, 0, ln - 1)], 0.0)
        tz.append(blk.transpose(2, 0, 3, 1, 4).reshape(S5_GROUPS, ln * S5_GROUP, ln * S5_GROUP))
        p_in = (ln - 1 - pos) if d == 0 else pos
        ar, ai = pw_re[d][p_in], pw_im[d][p_in]
        x_re = ar[..., None] * bb_re[d][None] - ai[..., None] * bb_im[d][None]
        x_im = ar[..., None] * bb_im[d][None] + ai[..., None] * bb_re[d][None]
        to_rows = lambda m: m.transpose(1, 0, 3, 2).reshape(S5_GROUPS, ln * S5_GROUP, S5_P)
        wx.append(jnp.concatenate([to_rows(x_re), to_rows(x_im), to_rows(x_im), to_rows(x_re)], axis=-1))
        p_out = (pos + 1) if d == 0 else (ln - pos)
        to_cols = lambda m: m.transpose(1, 3, 0, 2).reshape(S5_GROUPS, S5_P, ln * S5_GROUP)
        wy.append(jnp.concatenate([to_cols(ca_re[d][p_out]), -to_cols(ca_im[d][p_out])], axis=1))
        lr, li = pw_re[d][ln], pw_im[d][ln]
        rows = [jnp.concatenate([lr, lr], -1), jnp.concatenate([-li, li], -1), jnp.concatenate([li, -li], -1)]
        ac.append(jnp.stack(rows + [jnp.zeros_like(rows[0])] * 5, axis=1))
    return (jnp.stack(tz).astype(BF16), jnp.stack(wx).astype(BF16), jnp.stack(wy).astype(BF16), jnp.stack(ac))


def _s5_body(ncs_ctx, ncs, rows, u_ref, tz_ref, wx_ref, wy_ref, ac_ref, y_ref, xx_f, xx_b, sin_f, sin_b):
    u = u_ref[0]
    xx_f[...] = _dot(u, wx_ref[0, 0])
    xx_b[...] = _dot(u, wx_ref[1, 0])
    ac_f, ac_b = ac_ref[0, 0], ac_ref[1, 0]
    half = 2 * S5_P

    def advance(ac, s, s_sw, xx):
        return (ac[0:1] * s + ac[1:2] * s_sw + xx[:, :half], ac[0:1] * s_sw + ac[2:3] * s + xx[:, half:])

    def step(n, carry):
        s_f, sw_f, s_b, sw_b = carry
        r_f = pl.multiple_of(n * rows, rows)
        r_b = pl.multiple_of(_backward_chunk(n, ncs_ctx, ncs) * rows, rows)
        sin_f[pl.ds(r_f, rows), :] = s_f
        sin_b[pl.ds(r_b, rows), :] = s_b
        s_f, sw_f = advance(ac_f, s_f, sw_f, xx_f[pl.ds(r_f, rows), :])
        s_b, sw_b = advance(ac_b, s_b, sw_b, xx_b[pl.ds(r_b, rows), :])
        return s_f, sw_f, s_b, sw_b

    zero = jnp.zeros((rows, half), F32)
    lax.fori_loop(0, ncs, step, (zero, zero, zero, zero))
    y_ref[0] = (_dot(u, tz_ref[0, 0]) + _dot(u, tz_ref[1, 0])
                + _dot(sin_f[...].astype(BF16), wy_ref[0, 0]) + _dot(sin_b[...].astype(BF16), wy_ref[1, 0]))


def _s5(u, ops, n_ctx):
    tz, wx, wy, ac = ops
    bsz, t, _ = u.shape
    ln, lanes = S5_CHUNK, S5_CHUNK * S5_GROUP
    ncs, ncs_ctx = t // ln, n_ctx // ln
    m = ncs * bsz
    ug = u.astype(BF16).reshape(bsz, ncs, ln, S5_GROUPS, S5_GROUP).transpose(3, 1, 0, 2, 4).reshape(S5_GROUPS, m, lanes)
    dir_spec = lambda shape: pl.BlockSpec((2, 1) + shape, lambda g: (0, g, 0, 0))
    y = pl.pallas_call(
        functools.partial(_s5_body, ncs_ctx, ncs, bsz),
        grid=(S5_GROUPS,),
        in_specs=[pl.BlockSpec((1, m, lanes), lambda g: (g, 0, 0)), dir_spec((lanes, lanes)),
                  dir_spec((lanes, 4 * S5_P)), dir_spec((2 * S5_P, lanes)), dir_spec((8, 2 * S5_P))],
        out_specs=pl.BlockSpec((1, m, lanes), lambda g: (g, 0, 0)),
        out_shape=jax.ShapeDtypeStruct((S5_GROUPS, m, lanes), F32),
        scratch_shapes=[pltpu.VMEM((m, 4 * S5_P), F32)] * 2 + [pltpu.VMEM((m, 2 * S5_P), F32)] * 2,
        compiler_params=_params("arbitrary"),
        name="s5_scan",
    )(ug, tz, wx, wy, ac)
    return y.reshape(S5_GROUPS, ncs, bsz, ln, S5_GROUP).transpose(2, 1, 3, 0, 4).reshape(bsz, t, S5_CH)


def _ret_body(ncc, q_f, k_f, v_f, q_b, k_b, v_b, dmat_ref, rsc_ref, csc_ref, gam_ref, o_f, o_b, st_f, st_b):
    n = pl.program_id(1)

    @pl.when(n == 0)
    def _():
        st_f[...] = jnp.zeros_like(st_f)
        st_b[...] = jnp.zeros_like(st_b)

    dirs = ((q_f, k_f, v_f, o_f, st_f), (q_b, k_b, v_b, o_b, st_b))

    @pl.when(n >= ncc)
    def _():
        for d, (q_ref, k_ref, v_ref, o_ref, st_ref) in enumerate(dirs):
            for h in range(RET_HEADS):
                qh = q_ref[0, :, h * RET_DK:(h + 1) * RET_DK]
                kh = k_ref[0, :, h * RET_DK:(h + 1) * RET_DK]
                vh = v_ref[0, :, h * RET_DV:(h + 1) * RET_DV]
                scores = (_dot_nt(qh, kh) * dmat_ref[d, h]).astype(BF16)
                o = _dot(scores, vh) + rsc_ref[d, h] * _dot(qh, st_ref[h].astype(BF16))
                o_ref[0, :, h * RET_DV:(h + 1) * RET_DV] = o.astype(o_ref.dtype)

    for d, (q_ref, k_ref, v_ref, o_ref, st_ref) in enumerate(dirs):
        for h in range(RET_HEADS):
            kh = k_ref[0, :, h * RET_DK:(h + 1) * RET_DK]
            vh = v_ref[0, :, h * RET_DV:(h + 1) * RET_DV]
            k_state = (kh.astype(F32) * csc_ref[d, h]).astype(BF16)
            st_ref[h] = st_ref[h] * gam_ref[d, h] + _dot_tn(k_state, vh)


def _retention(q, k, v, decay_logit, n_ctx):
    bsz, t, _ = q.shape
    c = RET_CHUNK
    nc, ncc = t // c, n_ctx // c
    nl = nc - ncc
    log_gamma = jax.nn.log_sigmoid(decay_logit.astype(F32))[:, :, None, None]
    i = jnp.arange(c, dtype=F32)
    lag = i[:, None] - i[None, :]
    lag = jnp.stack([lag, -lag])[:, None]
    dmat = jnp.where(lag >= 0, jnp.exp(log_gamma * jnp.maximum(lag, 0.0)), 0.0)
    done = jnp.stack([i + 1.0, c - i])[:, None, :, None]
    rsc = jnp.exp(log_gamma * done)
    csc = jnp.exp(log_gamma * (c - done))
    gam = jnp.exp(log_gamma[:, :, 0, 0] * c)
    fwd = lambda b, n: (b, n, 0)
    bwd = lambda b, n: (b, _backward_chunk(n, ncc, nc), 0)
    o_fwd = lambda b, n: (b, jnp.maximum(n - ncc, 0), 0)
    o_bwd = lambda b, n: (b, nl - 1 - jnp.maximum(n - ncc, 0), 0)

    def specs(idx):
        return [pl.BlockSpec((1, c, RET_QK), idx), pl.BlockSpec((1, c, RET_QK), idx), pl.BlockSpec((1, c, RET_MIX), idx)]

    return pl.pallas_call(
        functools.partial(_ret_body, ncc),
        grid=(bsz, nc),
        in_specs=specs(fwd) + specs(bwd) + [_const_spec(dmat.shape), _const_spec(rsc.shape), _const_spec(csc.shape),
                                            pl.BlockSpec(memory_space=pltpu.SMEM)],
        out_specs=[pl.BlockSpec((1, c, RET_MIX), o_fwd), pl.BlockSpec((1, c, RET_MIX), o_bwd)],
        out_shape=[jax.ShapeDtypeStruct((bsz, nl * c, RET_MIX), BF16)] * 2,
        scratch_shapes=[pltpu.VMEM((RET_HEADS, RET_DK, RET_DV), F32)] * 2,
        compiler_params=_params("arbitrary", "arbitrary"),
        name="retention_scan",
    )(q, k, v, q, k, v, dmat, rsc, csc, gam)


def _route(x, mixed, mod, n2g_ref, wr_ref, br_ref, x1_ref, h2_ref, e_ref, w_ref, r_ref, cnt_ref):
    tm = x.shape[0]
    x1 = x + mod[2:3] * mixed
    x1_ref[0] = x1
    h2 = _norm_mod(x1, n2g_ref[...], mod[3:4], mod[4:5])
    h2_ref[0] = h2.astype(BF16)
    logits = _dot3(wr_ref[...], h2, dot=_dot_nt) + br_ref[...]
    ie = lax.broadcasted_iota(I32, logits.shape, 0)
    tops, picks = [], []
    for _ in range(TOP_K):
        mx = jnp.max(logits, axis=0, keepdims=True)
        pick = jnp.min(jnp.where(logits == mx, ie, N_EXPERTS), axis=0, keepdims=True)
        tops.append(mx)
        picks.append(pick)
        logits = jnp.where(ie == pick, -jnp.inf, logits)
    ex = [jnp.exp(tk - tops[0]) for tk in tops]
    den = ex[0] + ex[1] + ex[2] + ex[3]
    for kk in range(TOP_K):
        w_ref[0, kk:kk + 1, :] = ex[kk] / den
        e_ref[0, kk:kk + 1, :] = picks[kk]

    @pl.when((pl.program_id(0) == 0) & (pl.program_id(1) == 0))
    def _():
        cnt_ref[...] = jnp.zeros_like(cnt_ref)

    earlier = (lax.broadcasted_iota(I32, (tm, tm), 0) < lax.broadcasted_iota(I32, (tm, tm), 1))
    earlier = jnp.where(earlier, 1.0, 0.0).astype(BF16)
    run = cnt_ref[:, 0:1]
    for kk, pick in enumerate(picks):
        onehot = jnp.where(ie == pick, 1.0, 0.0)
        before = _dot(onehot.astype(BF16), earlier) + run
        r_ref[0, kk:kk + 1, :] = jnp.sum(onehot * before, axis=0, keepdims=True).astype(I32)
        run = run + jnp.sum(onehot, axis=1, keepdims=True)
    cnt_ref[...] = jnp.broadcast_to(run, cnt_ref.shape)


def _mix0_body(x_ref, mod_ref, of_ref, ob_ref, g_ref, ys_ref, u_ref, gng_ref, dsk_ref, gluw_ref, glub_ref,
               wo_ref, n2g_ref, wr_ref, br_ref, x1_ref, h2_ref, e_ref, w_ref, r_ref, cnt_ref):
    o = of_ref[0] + ob_ref[0]
    heads = []
    for h in range(GLA_HEADS):
        oh = o[:, h * GLA_DV:(h + 1) * GLA_DV]
        heads.append(oh * lax.rsqrt(jnp.mean(oh * oh, axis=-1, keepdims=True) + EPS))
    gla = jnp.concatenate(heads, axis=1) * gng_ref[...] * _silu(g_ref[0].astype(F32))
    y = jax.nn.gelu(ys_ref[0] + dsk_ref[...] * u_ref[0].astype(F32))
    y = y * jax.nn.sigmoid(_dot(y.astype(BF16), gluw_ref[...]) + glub_ref[...])
    mixed = _dot(gla.astype(BF16), wo_ref[0:AB_V]) + _dot(y.astype(BF16), wo_ref[AB_V:AB_V + S5_CH])
    _route(x_ref[0], mixed, mod_ref[0, 0], n2g_ref, wr_ref, br_ref, x1_ref, h2_ref, e_ref, w_ref, r_ref, cnt_ref)


def _mix1_body(x_ref, mod_ref, of_ref, ob_ref, g_ref, ng_ref, wo_ref, n2g_ref, wr_ref, br_ref,
               x1_ref, h2_ref, e_ref, w_ref, r_ref, cnt_ref):
    mixed = None
    for h in range(RET_HEADS):
        sl = slice(h * RET_DV, (h + 1) * RET_DV)
        oh = of_ref[0, :, sl].astype(F32) + ob_ref[0, :, sl].astype(F32)
        mu = jnp.mean(oh, axis=-1, keepdims=True)
        cen = oh - mu
        var = jnp.mean(cen * cen, axis=-1, keepdims=True)
        gated = cen * lax.rsqrt(var + EPS) * ng_ref[:, sl] * _silu(g_ref[0, :, sl].astype(F32))
        part = _dot(gated.astype(BF16), wo_ref[sl])
        mixed = part if mixed is None else mixed + part
    _route(x_ref[0], mixed, mod_ref[0, 0], n2g_ref, wr_ref, br_ref, x1_ref, h2_ref, e_ref, w_ref, r_ref, cnt_ref)


def _mix_call(body, name, x_all, mods, tiles, acts, consts, norm2_g, w_router, b_router, n_tok, seg_tile0):
    bsz, _, d = x_all.shape
    tm = TOKEN_TILE
    off = lambda b, i: (b, i + seg_tile0, 0)
    loc = lambda b, i: (b, i, 0)
    ntl = bsz * tiles
    flat = lambda b, i: (b * tiles + i, 0, 0)
    in_specs = [pl.BlockSpec((1, tm, d), off),
                pl.BlockSpec((1, 1, 6, d), lambda b, i: (b, ((i + seg_tile0) >= n_tok).astype(I32), 0, 0))]
    args = [x_all, mods]
    for arr, offset in acts:
        in_specs.append(pl.BlockSpec((1, tm, arr.shape[2]), off if offset else loc))
        args.append(arr)
    tail = list(consts) + [norm2_g.reshape(1, d), w_router.T, b_router.reshape(N_EXPERTS, 1)]
    in_specs += [_const_spec(a.shape) for a in tail]
    args += tail
    tok_out = pl.BlockSpec((1, TOP_K, tm), flat)
    return pl.pallas_call(
        body,
        grid=(bsz, tiles),
        in_specs=in_specs,
        out_specs=[pl.BlockSpec((1, tm, d), loc), pl.BlockSpec((1, tm, d), loc), tok_out, tok_out, tok_out,
                   _const_spec((N_EXPERTS, 128))],
        out_shape=[jax.ShapeDtypeStruct((bsz, tiles * tm, d), F32), jax.ShapeDtypeStruct((bsz, tiles * tm, d), BF16),
                   jax.ShapeDtypeStruct((ntl, TOP_K, tm), I32), jax.ShapeDtypeStruct((ntl, TOP_K, tm), F32),
                   jax.ShapeDtypeStruct((ntl, TOP_K, tm), I32), jax.ShapeDtypeStruct((N_EXPERTS, 128), F32)],
        compiler_params=_params("arbitrary", "arbitrary"),
        name=name,
    )(*args)


def _cast_rows(src_ref, dst_ref, rows):
    def chunk(j, carry):
        r = pl.multiple_of(j * rows, rows)
        dst_ref[pl.ds(r, rows), :] = src_ref[0, 0, pl.ds(r, rows), :].astype(BF16)
        return carry

    lax.fori_loop(0, dst_ref.shape[0] // rows, chunk, 0)


def _expert_body(be_ref, nu_ref, x_ref, wgu_ref, bgu_ref, wd_ref, bd_ref, o_ref, wgu_bf, wd_bf):
    i = pl.program_id(0)
    live = i < nu_ref[0]
    new_expert = (i == 0) | (be_ref[i] != be_ref[jnp.maximum(i - 1, 0)])

    @pl.when(live & new_expert)
    def _():
        _cast_rows(wgu_ref, wgu_bf, 128)
        _cast_rows(wd_ref, wd_bf, 128)

    @pl.when(live)
    def _():
        gu = _dot(x_ref[...], wgu_bf[...]) + bgu_ref[0, 0]
        gate = jnp.minimum(gu[:, :D_FF], SWIGLU_LIMIT)
        lin = jnp.clip(gu[:, D_FF:], -SWIGLU_LIMIT, SWIGLU_LIMIT)
        act = gate * jax.nn.sigmoid(SWIGLU_ALPHA * gate) * (lin + 1.0)
        o_ref[...] = _dot(act.astype(BF16), wd_bf[...]) + bd_ref[0, 0]

    @pl.when(i >= nu_ref[0])
    def _():
        o_ref[...] = jnp.zeros_like(o_ref)


def _experts(xb, block_e, n_used, layer, w_gu, b_gu, w_down, b_down):
    n_slots, d = xb.shape
    n_blocks = n_slots // MOE_BLOCK
    depth = w_gu.shape[0]
    by_expert = lambda i, be, nu: (layer, be[i], 0, 0)
    return pl.pallas_call(
        _expert_body,
        grid_spec=pltpu.PrefetchScalarGridSpec(
            num_scalar_prefetch=2,
            grid=(n_blocks,),
            in_specs=[pl.BlockSpec((MOE_BLOCK, d), lambda i, be, nu: (i, 0)),
                      pl.BlockSpec((1, 1, d, 2 * D_FF), by_expert), pl.BlockSpec((1, 1, 1, 2 * D_FF), by_expert),
                      pl.BlockSpec((1, 1, D_FF, d), by_expert), pl.BlockSpec((1, 1, 1, d), by_expert)],
            out_specs=pl.BlockSpec((MOE_BLOCK, d), lambda i, be, nu: (i, 0)),
            scratch_shapes=[pltpu.VMEM((d, 2 * D_FF), BF16), pltpu.VMEM((D_FF, d), BF16)]),
        out_shape=jax.ShapeDtypeStruct((n_slots, d), F32),
        compiler_params=_params("arbitrary"),
        name="moe_experts",
    )(block_e, n_used, xb, w_gu, b_gu.reshape(depth, N_EXPERTS, 1, 2 * D_FF), w_down,
      b_down.reshape(depth, N_EXPERTS, 1, d))


def _combine_body(final, x1_ref, mod_ref, yk_ref, w_ref, *refs):
    y = yk_ref[0, 0] * w_ref[0, :, 0:1]
    for k in range(1, TOP_K):
        y = y + yk_ref[k, 0] * w_ref[0, :, k:k + 1]
    x2 = x1_ref[0] + mod_ref[0, 0][5:6] * y
    if final:
        fg_ref, o_ref = refs
        x2 = x2 * lax.rsqrt(jnp.mean(x2 * x2, axis=-1, keepdims=True) + EPS) * fg_ref[...]
    else:
        (o_ref,) = refs
    o_ref[0] = x2


def _combine(x1, mods, yk, w_tok, seg_tile0, n_tok, final_g):
    bsz, t, d = x1.shape
    tm = TOKEN_TILE
    tiles = t // tm
    loc = lambda b, i: (b, i, 0)
    in_specs = [pl.BlockSpec((1, tm, d), loc),
                pl.BlockSpec((1, 1, 6, d), lambda b, i: (b, ((i + seg_tile0) >= n_tok).astype(I32), 0, 0)),
                pl.BlockSpec((TOP_K, 1, tm, d), lambda b, i: (0, b, i, 0)),
                pl.BlockSpec((1, tm, TOP_K), loc)]
    args = [x1, mods, yk.reshape(TOP_K, bsz, t, d), w_tok.reshape(bsz, t, TOP_K)]
    if final_g is not None:
        in_specs.append(_const_spec((1, d)))
        args.append(final_g.reshape(1, d))
    return pl.pallas_call(
        functools.partial(_combine_body, final_g is not None),
        grid=(bsz, tiles),
        in_specs=in_specs,
        out_specs=pl.BlockSpec((1, tm, d), loc),
        out_shape=jax.ShapeDtypeStruct((bsz, t, d), F32),
        compiler_params=_params("arbitrary", "arbitrary"),
        name="moe_combine",
    )(*args)


def _moe(h2, e_tl, w_tl, r_tl, cnt, layer, w_gu, b_gu, w_down, b_down):
    bsz, t, d = h2.shape
    n = bsz * t
    flat = lambda a: a.transpose(1, 0, 2).reshape(TOP_K, n)
    e_k, w_k, r_k = flat(e_tl), flat(w_tl), flat(r_tl)
    counts = cnt[:, 0].astype(I32)
    padded = (counts + MOE_BLOCK - 1) // MOE_BLOCK * MOE_BLOCK
    pad_end = jnp.cumsum(padded)
    pad_start = pad_end - padded
    n_blocks = (n * TOP_K + MOE_BLOCK - 1) // MOE_BLOCK + N_EXPERTS
    block_start = jnp.arange(n_blocks, dtype=I32) * MOE_BLOCK
    block_e = jnp.minimum(jnp.sum((pad_end[None, :] <= block_start[:, None]).astype(I32), axis=1), N_EXPERTS - 1)
    n_used = (pad_end[-1:] // MOE_BLOCK).astype(I32)
    dest = pad_start[e_k] + r_k
    tok = jnp.broadcast_to(jnp.arange(n, dtype=I32)[None, :], (TOP_K, n))
    slot_tok = jnp.zeros((n_blocks * MOE_BLOCK,), I32).at[dest.reshape(-1)].set(tok.reshape(-1))
    xb = h2.reshape(n, d)[slot_tok]
    yb = _experts(xb, block_e, n_used, layer, w_gu, b_gu, w_down, b_down)
    return yb[dest], w_k.T


def _rope_tables(n_ctx, n_lat):
    n_freq = RET_DK // 4
    inv_freq = ROPE_BASE ** (-jnp.arange(n_freq, dtype=F32) / n_freq)
    pos = jnp.arange(n_lat, dtype=I32)
    cos, sin = [], []
    for p in (pos // GRID_W, pos % GRID_W):
        ang = p.astype(F32)[:, None] * inv_freq
        cos += [jnp.cos(ang), jnp.cos(ang)]
        sin += [-jnp.sin(ang), jnp.sin(ang)]
    cos, sin = jnp.concatenate(cos, axis=1), jnp.concatenate(sin, axis=1)
    return (jnp.concatenate([jnp.ones((n_ctx, RET_DK), F32), cos], axis=0),
            jnp.concatenate([jnp.zeros((n_ctx, RET_DK), F32), sin], axis=0))


def kernel(x, c, ctx, c_ctx, ada_w, ada_b, norm1_g, norm2_g, ab_w_in, ab_w_out, gla_wa, gla_ba, gla_norm_g, s5_lam_re, s5_lam_im, s5_log_step, s5_b_re, s5_b_im, s5_c_re, s5_c_im, s5_d, s5_glu_w, s5_glu_b, ret_w_in, ret_w_out, ret_decay_logit, ret_norm_g, moe_w_router, moe_b_router, moe_w_gu, moe_b_gu, moe_w_down, moe_b_down, final_norm_g):
    bsz, n_lat, d = x.shape
    n_ctx = ctx.shape[1]
    depth = ada_w.shape[0]
    assert depth == 2 and d == D_MODEL and bsz == 8, "kernels are laid out for the stated problem shape"
    assert n_ctx % TOKEN_TILE == 0 and n_lat % TOKEN_TILE == 0 and n_lat % GRID_W == 0
    t = n_ctx + n_lat
    nct = n_ctx // TOKEN_TILE

    cvec = jnp.zeros((16, d), F32).at[:bsz].set(c).at[bsz].set(c_ctx)
    mod = _ada_mod(cvec, ada_w, ada_b).reshape(depth, 16, 6, d)
    mods = [jnp.stack([jnp.broadcast_to(mod[l, bsz], (bsz, 6, d)), mod[l, :bsz]], axis=1) for l in range(depth)]

    x_all = jnp.concatenate([ctx, x], axis=1)

    w_in = ab_w_in[0].astype(BF16)
    cuts = [0, AB_QK, 2 * AB_QK, 2 * AB_QK + AB_V, 2 * AB_QK + 2 * AB_V, 2 * AB_QK + 2 * AB_V + 2 * GLA_RANK,
            w_in.shape[1]]
    pieces = [w_in[:, a:b] for a, b in zip(cuts[:-1], cuts[1:])]
    q, k, v, g, low, u = _inproj0(x_all, mods[0], norm1_g[0], pieces, [F32, F32, BF16, BF16, F32, F32], nct)
    wa_pad = jnp.zeros((2, 2 * GLA_RANK, AB_QK), F32)
    wa_pad = wa_pad.at[0, :GLA_RANK].set(gla_wa[0, 0]).at[1, GLA_RANK:].set(gla_wa[0, 1])
    o_f, o_b = _gla(q, k, v, low, wa_pad, gla_ba[0].reshape(2, 1, AB_QK), n_ctx)
    ops = _s5_operators(s5_lam_re[0], s5_lam_im[0], s5_log_step[0], s5_b_re[0], s5_b_im[0], s5_c_re[0], s5_c_im[0])
    ys = _s5(u, ops, n_ctx)
    consts = [jnp.tile(gla_norm_g[0], GLA_HEADS).reshape(1, AB_V), s5_d[0].reshape(1, S5_CH),
              s5_glu_w[0].astype(BF16), s5_glu_b[0].reshape(1, S5_CH), ab_w_out[0].astype(BF16)]
    x1, h2, e_tl, w_tl, r_tl, cnt = _mix_call(
        _mix0_body, "mix_gla_s5", x_all, mods[0], t // TOKEN_TILE,
        [(o_f, False), (o_b, False), (g, False), (ys, False), (u, False)], consts,
        norm2_g[0], moe_w_router[0], moe_b_router[0], nct, 0)
    yk, w_tok = _moe(h2, e_tl, w_tl, r_tl, cnt, 0, moe_w_gu, moe_b_gu, moe_w_down, moe_b_down)
    x_all = _combine(x1, mods[0], yk, w_tok, 0, nct, None)

    w_in = ret_w_in[0].astype(BF16)
    cuts = [0, RET_QK, 2 * RET_QK, 2 * RET_QK + RET_MIX, w_in.shape[1]]
    pieces = [w_in[:, a:b] for a, b in zip(cuts[:-1], cuts[1:])]
    cos_t, sin_t = _rope_tables(n_ctx, n_lat)
    q, k, v, g = _inproj1(x_all, mods[1], norm1_g[1], cos_t, sin_t, pieces, nct)
    o_f, o_b = _retention(q, k, v, ret_decay_logit[0], n_ctx)
    consts = [ret_norm_g[0].reshape(1, RET_MIX), ret_w_out[0].astype(BF16)]
    x1, h2, e_tl, w_tl, r_tl, cnt = _mix_call(
        _mix1_body, "mix_retention", x_all, mods[1], n_lat // TOKEN_TILE,
        [(o_f, False), (o_b, False), (g, True)], consts,
        norm2_g[1], moe_w_router[1], moe_b_router[1], nct, nct)
    yk, w_tok = _moe(h2, e_tl, w_tl, r_tl, cnt, 1, moe_w_gu, moe_b_gu, moe_w_down, moe_b_down)
    return _combine(x1, mods[1], yk, w_tok, nct, nct, final_norm_g)
```

```python
import functools
import math

import jax
import jax.numpy as jnp
from jax import lax
from jax.experimental import pallas as pl
from jax.experimental.pallas import tpu as pltpu
from jax.experimental.pallas import tpu_sc as plsc

F32, BF16, I32, U32 = jnp.float32, jnp.bfloat16, jnp.int32, jnp.uint32

D_MODEL = 1024
GRID_W = 64
EPS = 1e-6
GLA_HEADS, GLA_DK, GLA_DV, GLA_RANK, GLA_TAU, GLA_CHUNK = 4, 64, 128, 16, 16.0, 64
AB_QK, AB_V = GLA_HEADS * GLA_DK, GLA_HEADS * GLA_DV
S5_CH, S5_GROUP, S5_GROUPS, S5_P = 512, 16, 32, 64
S5_CHUNK = 16
RET_HEADS, RET_DK, RET_DV, RET_CHUNK = 4, 256, 512, 128
RET_QK, RET_MIX = RET_HEADS * RET_DK, RET_HEADS * RET_DV
ROPE_BASE = 10000.0
N_EXPERTS, TOP_K, D_FF = 32, 4, 1024
SWIGLU_LIMIT, SWIGLU_ALPHA = 7.0, 1.702
MOE_BLOCK = 256
TOKEN_TILE = 256
ADA_TILE = 768
VMEM_LIMIT = 56 * 1024 * 1024
SC_CORES, SC_SUBCORES = 2, 16
SC_WORKERS = SC_CORES * SC_SUBCORES
SC_CHUNK = 64


def _params(*sem):
    return pltpu.CompilerParams(dimension_semantics=sem, vmem_limit_bytes=VMEM_LIMIT)


def _dot(a, b):
    return jnp.dot(a, b, preferred_element_type=F32)


def _dot_nt(a, b):
    return lax.dot_general(a, b, (((1,), (1,)), ((), ())), preferred_element_type=F32)


def _dot_tn(a, b):
    return lax.dot_general(a, b, (((0,), (0,)), ((), ())), preferred_element_type=F32)


def _split(a):
    hi = a.astype(BF16)
    return hi, (a - hi.astype(F32)).astype(BF16)


def _dot3(a, b, dot=_dot):
    ah, al = _split(a)
    bh, bl = _split(b)
    return dot(ah, bh) + (dot(ah, bl) + dot(al, bh))


def _pack_rows(x):
    h = x.shape[1] // 2
    lo = lax.bitcast_convert_type(x[:, 0:h].astype(BF16).astype(F32), U32)
    hi = lax.bitcast_convert_type(x[:, h:2 * h].astype(BF16).astype(F32), U32)
    return hi | (lo >> 16)


def _unpack_rows(p):
    lo = lax.bitcast_convert_type(p << 16, F32)
    hi = lax.bitcast_convert_type(p & jnp.uint32(0xFFFF0000), F32)
    return lo, hi


def _silu(x):
    return x * jax.nn.sigmoid(x)


def _norm_mod(x, g, shift, scale):
    r = lax.rsqrt(jnp.mean(x * x, axis=-1, keepdims=True) + EPS)
    return (x * r * g) * (1.0 + scale) + shift


def _const_spec(shape):
    nd = len(shape)
    return pl.BlockSpec(shape, lambda *_: (0,) * nd)


def _ada_body(c_ref, w_ref, b_ref, o_ref):
    o_ref[0] = _dot3(_silu(c_ref[...]), w_ref[0]) + b_ref[0]


def _ada_mod(cvec, ada_w, ada_b):
    depth, d, n6 = ada_w.shape
    rows = cvec.shape[0]
    return pl.pallas_call(
        _ada_body,
        grid=(depth, n6 // ADA_TILE),
        in_specs=[_const_spec((rows, d)),
                  pl.BlockSpec((1, d, ADA_TILE), lambda l, j: (l, 0, j)),
                  pl.BlockSpec((1, 1, ADA_TILE), lambda l, j: (l, 0, j))],
        out_specs=pl.BlockSpec((1, rows, ADA_TILE), lambda l, j: (l, 0, j)),
        out_shape=jax.ShapeDtypeStruct((depth, rows, n6), F32),
        compiler_params=_params("arbitrary", "arbitrary"),
        name="ada_mod",
    )(cvec, ada_w, ada_b.reshape(depth, 1, n6))


def _tile_specs(nct, d):
    x_spec = pl.BlockSpec((1, TOKEN_TILE, d), lambda b, i: (b, i, 0))
    mod_spec = pl.BlockSpec((1, 1, 6, d), lambda b, i: (b, (i >= nct).astype(I32), 0, 0))
    return x_spec, mod_spec


def _inproj0_body(x_ref, mod_ref, g_ref, *refs):
    n = len(refs) // 2
    m = mod_ref[0, 0]
    h = _norm_mod(x_ref[0], g_ref[...], m[0:1], m[1:2]).astype(BF16)
    for w_ref, o_ref in zip(refs[:n], refs[n:]):
        o_ref[0] = _dot(h, w_ref[...]).astype(o_ref.dtype)


def _inproj0(x_all, mods, norm_g, weights, out_dtypes, nct):
    bsz, t, d = x_all.shape
    x_spec, mod_spec = _tile_specs(nct, d)
    return pl.pallas_call(
        _inproj0_body,
        grid=(bsz, t // TOKEN_TILE),
        in_specs=[x_spec, mod_spec, _const_spec((1, d))] + [_const_spec(w.shape) for w in weights],
        out_specs=[pl.BlockSpec((1, TOKEN_TILE, w.shape[1]), lambda b, i: (b, i, 0)) for w in weights],
        out_shape=[jax.ShapeDtypeStruct((bsz, t, w.shape[1]), dt) for w, dt in zip(weights, out_dtypes)],
        compiler_params=_params("arbitrary", "arbitrary"),
        name="inproj_gla_s5",
    )(x_all, mods, norm_g.reshape(1, d), *weights)


def _rope(acc, cos_ref, sin_ref, o_ref, scale):
    for grp in range(acc.shape[1] // 128):
        half = grp % 2
        xg = acc[:, grp * 128:(grp + 1) * 128]
        cs = cos_ref[:, half * 128:(half + 1) * 128]
        sn = sin_ref[:, half * 128:(half + 1) * 128]
        out = xg * cs + pltpu.roll(xg, 64, 1) * sn
        o_ref[0, :, grp * 128:(grp + 1) * 128] = (out * scale).astype(o_ref.dtype)


def _inproj1_body(x_ref, mod_ref, g_ref, cos_ref, sin_ref, wq, wk, wv, wg, oq, ok, ov, og):
    m = mod_ref[0, 0]
    h = _norm_mod(x_ref[0], g_ref[...], m[0:1], m[1:2]).astype(BF16)
    _rope(_dot(h, wq[...]), cos_ref, sin_ref, oq, 1.0)
    _rope(_dot(h, wk[...]), cos_ref, sin_ref, ok, RET_DK ** -0.5)
    ov[0] = _dot(h, wv[...]).astype(ov.dtype)
    og[0] = _dot(h, wg[...]).astype(og.dtype)


def _inproj1(x_all, mods, norm_g, cos_t, sin_t, weights, nct):
    bsz, t, d = x_all.shape
    x_spec, mod_spec = _tile_specs(nct, d)
    tab_spec = pl.BlockSpec((TOKEN_TILE, RET_DK), lambda b, i: (i, 0))
    return pl.pallas_call(
        _inproj1_body,
        grid=(bsz, t // TOKEN_TILE),
        in_specs=[x_spec, mod_spec, _const_spec((1, d)), tab_spec, tab_spec] + [_const_spec(w.shape) for w in weights],
        out_specs=[pl.BlockSpec((1, TOKEN_TILE, w.shape[1]), lambda b, i: (b, i, 0)) for w in weights],
        out_shape=[jax.ShapeDtypeStruct((bsz, t, w.shape[1]), BF16) for w in weights],
        compiler_params=_params("arbitrary", "arbitrary"),
        name="inproj_retention",
    )(x_all, mods, norm_g.reshape(1, d), cos_t, sin_t, *weights)


def _backward_chunk(n, n_ctx_chunks, n_chunks):
    return jnp.where(n < n_ctx_chunks, n_ctx_chunks - 1 - n, n_chunks - 1 - (n - n_ctx_chunks))


def _gla_body(q_f, k_f, v_f, low_f, q_b, k_b, v_b, low_b, wa_ref, ba_ref, hmask_ref, bdmask_ref,
              o_f, o_b, st_f, st_b):
    c = GLA_CHUNK

    @pl.when(pl.program_id(1) == 0)
    def _():
        st_f[...] = jnp.zeros_like(st_f)
        st_b[...] = jnp.zeros_like(st_b)

    ii = lax.broadcasted_iota(I32, (c, c), 0)
    jj = lax.broadcasted_iota(I32, (c, c), 1)
    r4 = lax.broadcasted_iota(I32, (GLA_HEADS * c, c), 0) & (c - 1)
    c4 = lax.broadcasted_iota(I32, (GLA_HEADS * c, c), 1)
    dirs = ((q_f, k_f, v_f, low_f, o_f, st_f), (q_b, k_b, v_b, low_b, o_b, st_b))
    for d, (q_ref, k_ref, v_ref, low_ref, o_ref, st_ref) in enumerate(dirs):
        seen = (jj <= ii) if d == 0 else (jj >= ii)
        seen4 = (c4 <= r4) if d == 0 else (c4 >= r4)
        z = _dot3(low_ref[0], wa_ref[d]) + ba_ref[d]
        log_a = (jnp.minimum(z, 0.0) - jnp.log1p(jnp.exp(-jnp.abs(z)))) * (1.0 / GLA_TAU)
        tri = jnp.where(seen, 1.0, 0.0).astype(BF16)
        la_hi, la_lo = _split(log_a)
        cum = _dot(tri, la_hi) + _dot(tri, la_lo)
        tot = jnp.sum(log_a, axis=0, keepdims=True)
        k = k_ref[0]
        q_dec = q_ref[0] * (GLA_DK ** -0.5) * jnp.exp(cum)
        k_inv = (k * jnp.exp(-cum)).astype(BF16)
        k_state = (k * jnp.exp(tot - cum)).astype(BF16)
        q_heads = (jnp.concatenate([q_dec] * GLA_HEADS, axis=0) * hmask_ref[...]).astype(BF16)
        scores = jnp.where(seen4, _dot_nt(q_heads, k_inv), 0.0).astype(BF16)
        v = v_ref[0]
        st = st_ref[...]
        o_intra = jnp.concatenate(
            [_dot(scores[h * c:(h + 1) * c], v[:, h * GLA_DV:(h + 1) * GLA_DV]) for h in range(GLA_HEADS)], axis=1)
        o_ref[0] = o_intra + _dot_nt(q_dec.astype(BF16), st.astype(BF16))
        st_ref[...] = st * jnp.exp(tot) + bdmask_ref[...] * _dot_tn(v, k_state)


def _gla(q, k, v, low, wa_pad, ba, n_ctx):
    bsz, t, _ = q.shape
    nc, ncc = t // GLA_CHUNK, n_ctx // GLA_CHUNK
    fwd = lambda b, n: (b, n, 0)
    bwd = lambda b, n: (b, _backward_chunk(n, ncc, nc), 0)
    hmask = (jnp.arange(AB_QK)[:, None] // GLA_CHUNK == jnp.arange(AB_QK)[None, :] // GLA_DK).astype(F32)
    bdmask = (jnp.arange(AB_V)[:, None] // GLA_DV == jnp.arange(AB_QK)[None, :] // GLA_DK).astype(F32)

    def specs(idx):
        return [pl.BlockSpec((1, GLA_CHUNK, AB_QK), idx), pl.BlockSpec((1, GLA_CHUNK, AB_QK), idx),
                pl.BlockSpec((1, GLA_CHUNK, AB_V), idx), pl.BlockSpec((1, GLA_CHUNK, 2 * GLA_RANK), idx)]

    return pl.pallas_call(
        _gla_body,
        grid=(bsz, nc),
        in_specs=specs(fwd) + specs(bwd) + [_const_spec(wa_pad.shape), _const_spec(ba.shape),
                                            _const_spec(hmask.shape), _const_spec(bdmask.shape)],
        out_specs=[pl.BlockSpec((1, GLA_CHUNK, AB_V), fwd), pl.BlockSpec((1, GLA_CHUNK, AB_V), bwd)],
        out_shape=[jax.ShapeDtypeStruct((bsz, t, AB_V), F32)] * 2,
        scratch_shapes=[pltpu.VMEM((AB_V, AB_QK), F32)] * 2,
        compiler_params=_params("arbitrary", "arbitrary"),
        name="gla_scan",
    )(q, k, v, low, q, k, v, low, wa_pad, ba, hmask, bdmask)


def _s5_operators(lam_re, lam_im, log_step, b_re, b_im, c_re, c_im):
    hp = lax.Precision.HIGHEST
    ln = S5_CHUNK
    step = jnp.exp(log_step.astype(F32))[..., None]
    lam_re, lam_im = lam_re.astype(F32), lam_im.astype(F32)
    mag = jnp.exp(lam_re * step)
    a_re, a_im = mag * jnp.cos(lam_im * step), mag * jnp.sin(lam_im * step)
    den = lam_re * lam_re + lam_im * lam_im
    f_re = ((a_re - 1.0) * lam_re + a_im * lam_im) / den
    f_im = (a_im * lam_re - (a_re - 1.0) * lam_im) / den
    bb_re = f_re[..., None] * b_re - f_im[..., None] * b_im
    bb_im = f_re[..., None] * b_im + f_im[..., None] * b_re
    pw_re, pw_im = [jnp.ones_like(a_re)], [jnp.zeros_like(a_im)]
    for _ in range(ln):
        pr, pi = pw_re[-1], pw_im[-1]
        pw_re.append(pr * a_re - pi * a_im)
        pw_im.append(pr * a_im + pi * a_re)
    pw_re, pw_im = jnp.stack(pw_re, 1), jnp.stack(pw_im, 1)
    ca_re = c_re[:, None] * pw_re[:, :, :, None, :] - c_im[:, None] * pw_im[:, :, :, None, :]
    ca_im = c_re[:, None] * pw_im[:, :, :, None, :] + c_im[:, None] * pw_re[:, :, :, None, :]
    kern = (jnp.einsum('dtgcp,dgpe->dtgec', ca_re[:, :ln], bb_re, precision=hp)
            - jnp.einsum('dtgcp,dgpe->dtgec', ca_im[:, :ln], bb_im, precision=hp))
    pos = jnp.arange(ln)
    i_out, j_in = pos[None, :], pos[:, None]
    tz, wx, wy, ac = [], [], [], []
    for d in range(2):
        lag = (i_out - j_in) if d == 0 else (j_in - i_out)
        blk = jnp.where((lag >= 0)[:, :, None, None, None], kern[d][jnp.clip(lag, 0, ln - 1)], 0.0)
        tz.append(blk.transpose(2, 0, 3, 1, 4).reshape(S5_GROUPS, ln * S5_GROUP, ln * S5_GROUP))
        p_in = (ln - 1 - pos) if d == 0 else pos
        ar, ai = pw_re[d][p_in], pw_im[d][p_in]
        x_re = ar[..., None] * bb_re[d][None] - ai[..., None] * bb_im[d][None]
        x_im = ar[..., None] * bb_im[d][None] + ai[..., None] * bb_re[d][None]
        to_rows = lambda m: m.transpose(1, 0, 3, 2).reshape(S5_GROUPS, ln * S5_GROUP, S5_P)
        wx.append(jnp.concatenate([to_rows(x_re), to_rows(x_im), to_rows(x_im), to_rows(x_re)], axis=-1))
        p_out = (pos + 1) if d == 0 else (ln - pos)
        to_cols = lambda m: m.transpose(1, 3, 0, 2).reshape(S5_GROUPS, S5_P, ln * S5_GROUP)
        wy.append(jnp.concatenate([to_cols(ca_re[d][p_out]), -to_cols(ca_im[d][p_out])], axis=1))
        lr, li = pw_re[d][ln], pw_im[d][ln]
        rows = [jnp.concatenate([lr, lr], -1), jnp.concatenate([-li, li], -1), jnp.concatenate([li, -li], -1)]
        ac.append(jnp.stack(rows + [jnp.zeros_like(rows[0])] * 5, axis=1))
    return (jnp.stack(tz).astype(BF16), jnp.stack(wx).astype(BF16), jnp.stack(wy).astype(BF16), jnp.stack(ac))


def _s5_body(ncs_ctx, ncs, rows, u_ref, tz_ref, wx_ref, wy_ref, ac_ref, y_ref, xx_f, xx_b, sin_f, sin_b):
    u = u_ref[0]
    xx_f[...] = _dot(u, wx_ref[0, 0])
    xx_b[...] = _dot(u, wx_ref[1, 0])
    ac_f, ac_b = ac_ref[0, 0], ac_ref[1, 0]
    half = 2 * S5_P

    def advance(ac, s, s_sw, xx):
        return (ac[0:1] * s + ac[1:2] * s_sw + xx[:, :half], ac[0:1] * s_sw + ac[2:3] * s + xx[:, half:])

    def step(n, carry):
        s_f, sw_f, s_b, sw_b = carry
        r_f = pl.multiple_of(n * rows, rows)
        r_b = pl.multiple_of(_backward_chunk(n, ncs_ctx, ncs) * rows, rows)
        sin_f[pl.ds(r_f, rows), :] = s_f
        sin_b[pl.ds(r_b, rows), :] = s_b
        s_f, sw_f = advance(ac_f, s_f, sw_f, xx_f[pl.ds(r_f, rows), :])
        s_b, sw_b = advance(ac_b, s_b, sw_b, xx_b[pl.ds(r_b, rows), :])
        return s_f, sw_f, s_b, sw_b

    zero = jnp.zeros((rows, half), F32)
    lax.fori_loop(0, ncs, step, (zero, zero, zero, zero))
    y_ref[0] = (_dot(u, tz_ref[0, 0]) + _dot(u, tz_ref[1, 0])
                + _dot(sin_f[...].astype(BF16), wy_ref[0, 0]) + _dot(sin_b[...].astype(BF16), wy_ref[1, 0]))


def _s5(u, ops, n_ctx):
    tz, wx, wy, ac = ops
    bsz, t, _ = u.shape
    ln, lanes = S5_CHUNK, S5_CHUNK * S5_GROUP
    ncs, ncs_ctx = t // ln, n_ctx // ln
    m = ncs * bsz
    ug = u.astype(BF16).reshape(bsz, ncs, ln, S5_GROUPS, S5_GROUP).transpose(3, 1, 0, 2, 4).reshape(S5_GROUPS, m, lanes)
    dir_spec = lambda shape: pl.BlockSpec((2, 1) + shape, lambda g: (0, g, 0, 0))
    y = pl.pallas_call(
        functools.partial(_s5_body, ncs_ctx, ncs, bsz),
        grid=(S5_GROUPS,),
        in_specs=[pl.BlockSpec((1, m, lanes), lambda g: (g, 0, 0)), dir_spec((lanes, lanes)),
                  dir_spec((lanes, 4 * S5_P)), dir_spec((2 * S5_P, lanes)), dir_spec((8, 2 * S5_P))],
        out_specs=pl.BlockSpec((1, m, lanes), lambda g: (g, 0, 0)),
        out_shape=jax.ShapeDtypeStruct((S5_GROUPS, m, lanes), F32),
        scratch_shapes=[pltpu.VMEM((m, 4 * S5_P), F32)] * 2 + [pltpu.VMEM((m, 2 * S5_P), F32)] * 2,
        compiler_params=_params("arbitrary"),
        name="s5_scan",
    )(ug, tz, wx, wy, ac)
    return y.reshape(S5_GROUPS, ncs, bsz, ln, S5_GROUP).transpose(2, 1, 3, 0, 4).reshape(bsz, t, S5_CH)


def _ret_body(ncc, q_f, k_f, v_f, q_b, k_b, v_b, dmat_ref, rsc_ref, csc_ref, gam_ref, o_f, o_b, st_f, st_b):
    n = pl.program_id(1)

    @pl.when(n == 0)
    def _():
        st_f[...] = jnp.zeros_like(st_f)
        st_b[...] = jnp.zeros_like(st_b)

    dirs = ((q_f, k_f, v_f, o_f, st_f), (q_b, k_b, v_b, o_b, st_b))

    @pl.when(n >= ncc)
    def _():
        for d, (q_ref, k_ref, v_ref, o_ref, st_ref) in enumerate(dirs):
            for h in range(RET_HEADS):
                qh = q_ref[0, :, h * RET_DK:(h + 1) * RET_DK]
                kh = k_ref[0, :, h * RET_DK:(h + 1) * RET_DK]
                vh = v_ref[0, :, h * RET_DV:(h + 1) * RET_DV]
                scores = (_dot_nt(qh, kh) * dmat_ref[d, h]).astype(BF16)
                o = _dot(scores, vh) + rsc_ref[d, h] * _dot(qh, st_ref[h].astype(BF16))
                o_ref[0, :, h * RET_DV:(h + 1) * RET_DV] = o.astype(o_ref.dtype)

    for d, (q_ref, k_ref, v_ref, o_ref, st_ref) in enumerate(dirs):
        for h in range(RET_HEADS):
            kh = k_ref[0, :, h * RET_DK:(h + 1) * RET_DK]
            vh = v_ref[0, :, h * RET_DV:(h + 1) * RET_DV]
            k_state = (kh.astype(F32) * csc_ref[d, h]).astype(BF16)
            st_ref[h] = st_ref[h] * gam_ref[d, h] + _dot_tn(k_state, vh)


def _retention(q, k, v, decay_logit, n_ctx):
    bsz, t, _ = q.shape
    c = RET_CHUNK
    nc, ncc = t // c, n_ctx // c
    nl = nc - ncc
    log_gamma = jax.nn.log_sigmoid(decay_logit.astype(F32))[:, :, None, None]
    i = jnp.arange(c, dtype=F32)
    lag = i[:, None] - i[None, :]
    lag = jnp.stack([lag, -lag])[:, None]
    dmat = jnp.where(lag >= 0, jnp.exp(log_gamma * jnp.maximum(lag, 0.0)), 0.0)
    done = jnp.stack([i + 1.0, c - i])[:, None, :, None]
    rsc = jnp.exp(log_gamma * done)
    csc = jnp.exp(log_gamma * (c - done))
    gam = jnp.exp(log_gamma[:, :, 0, 0] * c)
    fwd = lambda b, n: (b, n, 0)
    bwd = lambda b, n: (b, _backward_chunk(n, ncc, nc), 0)
    o_fwd = lambda b, n: (b, jnp.maximum(n - ncc, 0), 0)
    o_bwd = lambda b, n: (b, nl - 1 - jnp.maximum(n - ncc, 0), 0)

    def specs(idx):
        return [pl.BlockSpec((1, c, RET_QK), idx), pl.BlockSpec((1, c, RET_QK), idx), pl.BlockSpec((1, c, RET_MIX), idx)]

    return pl.pallas_call(
        functools.partial(_ret_body, ncc),
        grid=(bsz, nc),
        in_specs=specs(fwd) + specs(bwd) + [_const_spec(dmat.shape), _const_spec(rsc.shape), _const_spec(csc.shape),
                                            pl.BlockSpec(memory_space=pltpu.SMEM)],
        out_specs=[pl.BlockSpec((1, c, RET_MIX), o_fwd), pl.BlockSpec((1, c, RET_MIX), o_bwd)],
        out_shape=[jax.ShapeDtypeStruct((bsz, nl * c, RET_MIX), BF16)] * 2,
        scratch_shapes=[pltpu.VMEM((RET_HEADS, RET_DK, RET_DV), F32)] * 2,
        compiler_params=_params("arbitrary", "arbitrary"),
        name="retention_scan",
    )(q, k, v, q, k, v, dmat, rsc, csc, gam)


def _route(x, mixed, mod, n2g_ref, wr_ref, br_ref, x1_ref, h2_ref, e_ref, w_ref, r_ref, cnt_ref):
    tm = x.shape[0]
    x1 = x + mod[2:3] * mixed
    x1_ref[0] = x1
    h2 = _norm_mod(x1, n2g_ref[...], mod[3:4], mod[4:5])
    h2_ref[0] = _pack_rows(h2)
    logits = _dot3(wr_ref[...], h2, dot=_dot_nt) + br_ref[...]
    ie = lax.broadcasted_iota(I32, logits.shape, 0)
    tops, picks = [], []
    for _ in range(TOP_K):
        mx = jnp.max(logits, axis=0, keepdims=True)
        pick = jnp.min(jnp.where(logits == mx, ie, N_EXPERTS), axis=0, keepdims=True)
        tops.append(mx)
        picks.append(pick)
        logits = jnp.where(ie == pick, -jnp.inf, logits)
    ex = [jnp.exp(tk - tops[0]) for tk in tops]
    den = ex[0] + ex[1] + ex[2] + ex[3]
    for kk in range(TOP_K):
        w_ref[0, kk:kk + 1, :] = ex[kk] / den
        e_ref[0, kk:kk + 1, :] = picks[kk]

    @pl.when((pl.program_id(0) == 0) & (pl.program_id(1) == 0))
    def _():
        cnt_ref[...] = jnp.zeros_like(cnt_ref)

    earlier = (lax.broadcasted_iota(I32, (tm, tm), 0) < lax.broadcasted_iota(I32, (tm, tm), 1))
    earlier = jnp.where(earlier, 1.0, 0.0).astype(BF16)
    run = cnt_ref[:, 0:1]
    for kk, pick in enumerate(picks):
        onehot = jnp.where(ie == pick, 1.0, 0.0)
        before = _dot(onehot.astype(BF16), earlier) + run
        r_ref[0, kk:kk + 1, :] = jnp.sum(onehot * before, axis=0, keepdims=True).astype(I32)
        run = run + jnp.sum(onehot, axis=1, keepdims=True)
    cnt_ref[...] = jnp.broadcast_to(run, cnt_ref.shape)


def _mix0_body(x_ref, mod_ref, of_ref, ob_ref, g_ref, ys_ref, u_ref, gng_ref, dsk_ref, gluw_ref, glub_ref,
               wo_ref, n2g_ref, wr_ref, br_ref, x1_ref, h2_ref, e_ref, w_ref, r_ref, cnt_ref):
    o = of_ref[0] + ob_ref[0]
    heads = []
    for h in range(GLA_HEADS):
        oh = o[:, h * GLA_DV:(h + 1) * GLA_DV]
        heads.append(oh * lax.rsqrt(jnp.mean(oh * oh, axis=-1, keepdims=True) + EPS))
    gla = jnp.concatenate(heads, axis=1) * gng_ref[...] * _silu(g_ref[0].astype(F32))
    y = jax.nn.gelu(ys_ref[0] + dsk_ref[...] * u_ref[0].astype(F32))
    y = y * jax.nn.sigmoid(_dot(y.astype(BF16), gluw_ref[...]) + glub_ref[...])
    mixed = _dot(gla.astype(BF16), wo_ref[0:AB_V]) + _dot(y.astype(BF16), wo_ref[AB_V:AB_V + S5_CH])
    _route(x_ref[0], mixed, mod_ref[0, 0], n2g_ref, wr_ref, br_ref, x1_ref, h2_ref, e_ref, w_ref, r_ref, cnt_ref)


def _mix1_body(x_ref, mod_ref, of_ref, ob_ref, g_ref, ng_ref, wo_ref, n2g_ref, wr_ref, br_ref,
               x1_ref, h2_ref, e_ref, w_ref, r_ref, cnt_ref):
    mixed = None
    for h in range(RET_HEADS):
        sl = slice(h * RET_DV, (h + 1) * RET_DV)
        oh = of_ref[0, :, sl].astype(F32) + ob_ref[0, :, sl].astype(F32)
        mu = jnp.mean(oh, axis=-1, keepdims=True)
        cen = oh - mu
        var = jnp.mean(cen * cen, axis=-1, keepdims=True)
        gated = cen * lax.rsqrt(var + EPS) * ng_ref[:, sl] * _silu(g_ref[0, :, sl].astype(F32))
        part = _dot(gated.astype(BF16), wo_ref[sl])
        mixed = part if mixed is None else mixed + part
    _route(x_ref[0], mixed, mod_ref[0, 0], n2g_ref, wr_ref, br_ref, x1_ref, h2_ref, e_ref, w_ref, r_ref, cnt_ref)


def _mix_call(body, name, x_all, mods, tiles, acts, consts, norm2_g, w_router, b_router, n_tok, seg_tile0):
    bsz, _, d = x_all.shape
    tm = TOKEN_TILE
    off = lambda b, i: (b, i + seg_tile0, 0)
    loc = lambda b, i: (b, i, 0)
    ntl = bsz * tiles
    flat = lambda b, i: (b * tiles + i, 0, 0)
    in_specs = [pl.BlockSpec((1, tm, d), off),
                pl.BlockSpec((1, 1, 6, d), lambda b, i: (b, ((i + seg_tile0) >= n_tok).astype(I32), 0, 0))]
    args = [x_all, mods]
    for arr, offset in acts:
        in_specs.append(pl.BlockSpec((1, tm, arr.shape[2]), off if offset else loc))
        args.append(arr)
    tail = list(consts) + [norm2_g.reshape(1, d), w_router.T, b_router.reshape(N_EXPERTS, 1)]
    in_specs += [_const_spec(a.shape) for a in tail]
    args += tail
    tok_out = pl.BlockSpec((1, TOP_K, tm), flat)
    return pl.pallas_call(
        body,
        grid=(bsz, tiles),
        in_specs=in_specs,
        out_specs=[pl.BlockSpec((1, tm, d), loc), pl.BlockSpec((1, tm, d // 2), loc), tok_out, tok_out, tok_out,
                   _const_spec((N_EXPERTS, 128))],
        out_shape=[jax.ShapeDtypeStruct((bsz, tiles * tm, d), F32), jax.ShapeDtypeStruct((bsz, tiles * tm, d // 2), U32),
                   jax.ShapeDtypeStruct((ntl, TOP_K, tm), I32), jax.ShapeDtypeStruct((ntl, TOP_K, tm), F32),
                   jax.ShapeDtypeStruct((ntl, TOP_K, tm), I32), jax.ShapeDtypeStruct((N_EXPERTS, 128), F32)],
        compiler_params=_params("arbitrary", "arbitrary"),
        name=name,
    )(*args)


def _cast_rows(src_ref, dst_ref, rows):
    def chunk(j, carry):
        r = pl.multiple_of(j * rows, rows)
        dst_ref[pl.ds(r, rows), :] = src_ref[0, 0, pl.ds(r, rows), :].astype(BF16)
        return carry

    lax.fori_loop(0, dst_ref.shape[0] // rows, chunk, 0)


def _expert_body(be_ref, nu_ref, x_ref, wgu_ref, bgu_ref, wd_ref, bd_ref, o_ref, wgu_bf, wd_bf):
    i = pl.program_id(0)
    live = i < nu_ref[0]
    new_expert = (i == 0) | (be_ref[i] != be_ref[jnp.maximum(i - 1, 0)])

    @pl.when(live & new_expert)
    def _():
        _cast_rows(wgu_ref, wgu_bf, 128)
        _cast_rows(wd_ref, wd_bf, 128)

    @pl.when(live)
    def _():
        x_lo, x_hi = _unpack_rows(x_ref[...])
        half = x_lo.shape[1]
        gu = (_dot(x_lo.astype(BF16), wgu_bf[0:half]) + _dot(x_hi.astype(BF16), wgu_bf[half:2 * half])
              + bgu_ref[0, 0])
        gate = jnp.minimum(gu[:, :D_FF], SWIGLU_LIMIT)
        lin = jnp.clip(gu[:, D_FF:], -SWIGLU_LIMIT, SWIGLU_LIMIT)
        act = gate * jax.nn.sigmoid(SWIGLU_ALPHA * gate) * (lin + 1.0)
        y = _dot(act.astype(BF16), wd_bf[...]) + bd_ref[0, 0]
        o_ref[...] = _pack_rows(y)

    @pl.when(i >= nu_ref[0])
    def _():
        o_ref[...] = jnp.zeros_like(o_ref)


def _experts(xb, block_e, n_used, layer, w_gu, b_gu, w_down, b_down):
    n_slots, half = xb.shape
    d = 2 * half
    n_blocks = n_slots // MOE_BLOCK
    depth = w_gu.shape[0]
    by_expert = lambda i, be, nu: (layer, be[i], 0, 0)
    return pl.pallas_call(
        _expert_body,
        grid_spec=pltpu.PrefetchScalarGridSpec(
            num_scalar_prefetch=2,
            grid=(n_blocks,),
            in_specs=[pl.BlockSpec((MOE_BLOCK, half), lambda i, be, nu: (i, 0)),
                      pl.BlockSpec((1, 1, d, 2 * D_FF), by_expert), pl.BlockSpec((1, 1, 1, 2 * D_FF), by_expert),
                      pl.BlockSpec((1, 1, D_FF, d), by_expert), pl.BlockSpec((1, 1, 1, d), by_expert)],
            out_specs=pl.BlockSpec((MOE_BLOCK, half), lambda i, be, nu: (i, 0)),
            scratch_shapes=[pltpu.VMEM((d, 2 * D_FF), BF16), pltpu.VMEM((D_FF, d), BF16)]),
        out_shape=jax.ShapeDtypeStruct((n_slots, half), U32),
        compiler_params=_params("arbitrary"),
        name="moe_experts",
    )(block_e, n_used, xb, w_gu, b_gu.reshape(depth, N_EXPERTS, 1, 2 * D_FF), w_down,
      b_down.reshape(depth, N_EXPERTS, 1, d))


def _combine_body(final, x1_ref, mod_ref, yk_ref, w_ref, *refs):
    d = x1_ref.shape[2]
    half = d // 2
    y_lo, y_hi = None, None
    for k in range(TOP_K):
        lo, hi = _unpack_rows(yk_ref[k, 0])
        wk = w_ref[0, :, k:k + 1]
        y_lo = lo * wk if y_lo is None else y_lo + lo * wk
        y_hi = hi * wk if y_hi is None else y_hi + hi * wk
    g2 = mod_ref[0, 0][5:6]
    x2_lo = x1_ref[0, :, 0:half] + g2[:, 0:half] * y_lo
    x2_hi = x1_ref[0, :, half:d] + g2[:, half:d] * y_hi
    if final:
        fg_ref, o_ref = refs
        ms = (jnp.sum(x2_lo * x2_lo, axis=-1, keepdims=True) + jnp.sum(x2_hi * x2_hi, axis=-1, keepdims=True)) / d
        r = lax.rsqrt(ms + EPS)
        x2_lo = x2_lo * r * fg_ref[:, 0:half]
        x2_hi = x2_hi * r * fg_ref[:, half:d]
    else:
        (o_ref,) = refs
    o_ref[0, :, 0:half] = x2_lo
    o_ref[0, :, half:d] = x2_hi


def _combine(x1, mods, yk, w_tok, seg_tile0, n_tok, final_g):
    bsz, t, d = x1.shape
    tm = TOKEN_TILE
    tiles = t // tm
    loc = lambda b, i: (b, i, 0)
    in_specs = [pl.BlockSpec((1, tm, d), loc),
                pl.BlockSpec((1, 1, 6, d), lambda b, i: (b, ((i + seg_tile0) >= n_tok).astype(I32), 0, 0)),
                pl.BlockSpec((TOP_K, 1, tm, d // 2), lambda b, i: (0, b, i, 0)),
                pl.BlockSpec((1, tm, TOP_K), loc)]
    args = [x1, mods, yk.reshape(TOP_K, bsz, t, d // 2), w_tok.reshape(bsz, t, TOP_K)]
    if final_g is not None:
        in_specs.append(_const_spec((1, d)))
        args.append(final_g.reshape(1, d))
    return pl.pallas_call(
        functools.partial(_combine_body, final_g is not None),
        grid=(bsz, tiles),
        in_specs=in_specs,
        out_specs=pl.BlockSpec((1, tm, d), loc),
        out_shape=jax.ShapeDtypeStruct((bsz, t, d), F32),
        compiler_params=_params("arbitrary", "arbitrary"),
        name="moe_combine",
    )(*args)


def _sc_mesh():
    return plsc.VectorSubcoreMesh(core_axis_name="core", subcore_axis_name="subcore",
                                  num_cores=SC_CORES, num_subcores=SC_SUBCORES)


def _sc_worker_base(per_worker):
    return (lax.axis_index("subcore") * SC_CORES + lax.axis_index("core")) * per_worker


def _sc_dispatch(rows, dest, n_slots):
    n, w = rows.shape
    per_worker = n // SC_WORKERS
    assert per_worker * SC_WORKERS == n and per_worker % SC_CHUNK == 0

    @functools.partial(
        pl.kernel, mesh=_sc_mesh(), out_type=jax.ShapeDtypeStruct((n_slots, w), rows.dtype),
        scratch_types=[pltpu.VMEM((SC_CHUNK,), I32)] * TOP_K + [pltpu.VMEM((SC_CHUNK, w), rows.dtype),
                                                                pltpu.SemaphoreType.DMA],
        name="moe_dispatch")
    def scatter_rows(rows_hbm, dest_hbm, out_hbm, *scratch):
        idx_refs, buf, sem = scratch[:TOP_K], scratch[TOP_K], scratch[TOP_K + 1]
        base0 = _sc_worker_base(per_worker)

        @pl.loop(0, per_worker // SC_CHUNK)
        def _(j):
            base = base0 + j * SC_CHUNK
            pltpu.sync_copy(rows_hbm.at[pl.ds(base, SC_CHUNK)], buf)
            for k, idx in enumerate(idx_refs):
                pltpu.sync_copy(dest_hbm.at[pl.ds(k * n + base, SC_CHUNK)], idx)
            copies = [pltpu.async_copy(buf, out_hbm.at[idx], sem) for idx in idx_refs]
            for cp in copies:
                cp.wait()

    return scatter_rows(rows, dest)


def _sc_gather(table, idx):
    n = idx.shape[0]
    w = table.shape[1]
    per_worker = n // SC_WORKERS
    assert per_worker * SC_WORKERS == n and per_worker % SC_CHUNK == 0

    @functools.partial(
        pl.kernel, mesh=_sc_mesh(), out_type=jax.ShapeDtypeStruct((n, w), table.dtype),
        scratch_types=[pltpu.VMEM((SC_CHUNK,), I32), pltpu.VMEM((SC_CHUNK, w), table.dtype), pltpu.SemaphoreType.DMA],
        name="moe_gather")
    def gather_rows(table_hbm, idx_hbm, out_hbm, idx_v, buf, sem):
        base0 = _sc_worker_base(per_worker)

        @pl.loop(0, per_worker // SC_CHUNK)
        def _(j):
            base = base0 + j * SC_CHUNK
            pltpu.sync_copy(idx_hbm.at[pl.ds(base, SC_CHUNK)], idx_v)
            pltpu.async_copy(table_hbm.at[idx_v], buf, sem).wait()
            pltpu.sync_copy(buf, out_hbm.at[pl.ds(base, SC_CHUNK)])

    return gather_rows(table, idx)


def _moe(h2, e_tl, w_tl, r_tl, cnt, layer, w_gu, b_gu, w_down, b_down):
    bsz, t, half = h2.shape
    n = bsz * t
    flat = lambda a: a.transpose(1, 0, 2).reshape(TOP_K, n)
    e_k, w_k, r_k = flat(e_tl), flat(w_tl), flat(r_tl)
    counts = cnt[:, 0].astype(I32)
    padded = (counts + MOE_BLOCK - 1) // MOE_BLOCK * MOE_BLOCK
    pad_end = jnp.cumsum(padded)
    pad_start = pad_end - padded
    n_blocks = (n * TOP_K + MOE_BLOCK - 1) // MOE_BLOCK + N_EXPERTS
    block_start = jnp.arange(n_blocks, dtype=I32) * MOE_BLOCK
    block_e = jnp.minimum(jnp.sum((pad_end[None, :] <= block_start[:, None]).astype(I32), axis=1), N_EXPERTS - 1)
    n_used = (pad_end[-1:] // MOE_BLOCK).astype(I32)
    dest = (pad_start[e_k] + r_k).reshape(TOP_K * n)
    xb = _sc_dispatch(h2.reshape(n, half), dest, n_blocks * MOE_BLOCK)
    yb = _experts(xb, block_e, n_used, layer, w_gu, b_gu, w_down, b_down)
    return _sc_gather(yb, dest), w_k.T


def _rope_tables(n_ctx, n_lat):
    n_freq = RET_DK // 4
    inv_freq = ROPE_BASE ** (-jnp.arange(n_freq, dtype=F32) / n_freq)
    pos = jnp.arange(n_lat, dtype=I32)
    cos, sin = [], []
    for p in (pos // GRID_W, pos % GRID_W):
        ang = p.astype(F32)[:, None] * inv_freq
        cos += [jnp.cos(ang), jnp.cos(ang)]
        sin += [-jnp.sin(ang), jnp.sin(ang)]
    cos, sin = jnp.concatenate(cos, axis=1), jnp.concatenate(sin, axis=1)
    return (jnp.concatenate([jnp.ones((n_ctx, RET_DK), F32), cos], axis=0),
            jnp.concatenate([jnp.zeros((n_ctx, RET_DK), F32), sin], axis=0))


def kernel(x, c, ctx, c_ctx, ada_w, ada_b, norm1_g, norm2_g, ab_w_in, ab_w_out, gla_wa, gla_ba, gla_norm_g, s5_lam_re, s5_lam_im, s5_log_step, s5_b_re, s5_b_im, s5_c_re, s5_c_im, s5_d, s5_glu_w, s5_glu_b, ret_w_in, ret_w_out, ret_decay_logit, ret_norm_g, moe_w_router, moe_b_router, moe_w_gu, moe_b_gu, moe_w_down, moe_b_down, final_norm_g):
    bsz, n_lat, d = x.shape
    n_ctx = ctx.shape[1]
    depth = ada_w.shape[0]
    assert depth == 2 and d == D_MODEL and bsz == 8, "kernels are laid out for the stated problem shape"
    assert n_ctx % TOKEN_TILE == 0 and n_lat % TOKEN_TILE == 0 and n_lat % GRID_W == 0
    t = n_ctx + n_lat
    nct = n_ctx // TOKEN_TILE

    cvec = jnp.zeros((16, d), F32).at[:bsz].set(c).at[bsz].set(c_ctx)
    mod = _ada_mod(cvec, ada_w, ada_b).reshape(depth, 16, 6, d)
    mods = [jnp.stack([jnp.broadcast_to(mod[l, bsz], (bsz, 6, d)), mod[l, :bsz]], axis=1) for l in range(depth)]

    x_all = jnp.concatenate([ctx, x], axis=1)

    w_in = ab_w_in[0].astype(BF16)
    cuts = [0, AB_QK, 2 * AB_QK, 2 * AB_QK + AB_V, 2 * AB_QK + 2 * AB_V, 2 * AB_QK + 2 * AB_V + 2 * GLA_RANK,
            w_in.shape[1]]
    pieces = [w_in[:, a:b] for a, b in zip(cuts[:-1], cuts[1:])]
    q, k, v, g, low, u = _inproj0(x_all, mods[0], norm1_g[0], pieces, [F32, F32, BF16, BF16, F32, F32], nct)
    wa_pad = jnp.zeros((2, 2 * GLA_RANK, AB_QK), F32)
    wa_pad = wa_pad.at[0, :GLA_RANK].set(gla_wa[0, 0]).at[1, GLA_RANK:].set(gla_wa[0, 1])
    o_f, o_b = _gla(q, k, v, low, wa_pad, gla_ba[0].reshape(2, 1, AB_QK), n_ctx)
    ops = _s5_operators(s5_lam_re[0], s5_lam_im[0], s5_log_step[0], s5_b_re[0], s5_b_im[0], s5_c_re[0], s5_c_im[0])
    ys = _s5(u, ops, n_ctx)
    consts = [jnp.tile(gla_norm_g[0], GLA_HEADS).reshape(1, AB_V), s5_d[0].reshape(1, S5_CH),
              s5_glu_w[0].astype(BF16), s5_glu_b[0].reshape(1, S5_CH), ab_w_out[0].astype(BF16)]
    x1, h2, e_tl, w_tl, r_tl, cnt = _mix_call(
        _mix0_body, "mix_gla_s5", x_all, mods[0], t // TOKEN_TILE,
        [(o_f, False), (o_b, False), (g, False), (ys, False), (u, False)], consts,
        norm2_g[0], moe_w_router[0], moe_b_router[0], nct, 0)
    yk, w_tok = _moe(h2, e_tl, w_tl, r_tl, cnt, 0, moe_w_gu, moe_b_gu, moe_w_down, moe_b_down)
    x_all = _combine(x1, mods[0], yk, w_tok, 0, nct, None)

    w_in = ret_w_in[0].astype(BF16)
    cuts = [0, RET_QK, 2 * RET_QK, 2 * RET_QK + RET_MIX, w_in.shape[1]]
    pieces = [w_in[:, a:b] for a, b in zip(cuts[:-1], cuts[1:])]
    cos_t, sin_t = _rope_tables(n_ctx, n_lat)
    q, k, v, g = _inproj1(x_all, mods[1], norm1_g[1], cos_t, sin_t, pieces, nct)
    o_f, o_b = _retention(q, k, v, ret_decay_logit[0], n_ctx)
    consts = [ret_norm_g[0].reshape(1, RET_MIX), ret_w_out[0].astype(BF16)]
    x1, h2, e_tl, w_tl, r_tl, cnt = _mix_call(
        _mix1_body, "mix_retention", x_all, mods[1], n_lat // TOKEN_TILE,
        [(o_f, False), (o_b, False), (g, True)], consts,
        norm2_g[1], moe_w_router[1], moe_b_router[1], nct, nct)
    yk, w_tok = _moe(h2, e_tl, w_tl, r_tl, cnt, 1, moe_w_gu, moe_b_gu, moe_w_down, moe_b_down)
    return _combine(x1, mods[1], yk, w_tok, nct, nct, final_norm_g)
```

```python
import functools
import math

import jax
import jax.numpy as jnp
from jax import lax
from jax.experimental import pallas as pl
from jax.experimental.pallas import tpu as pltpu
from jax.experimental.pallas import tpu_sc as plsc

F32, BF16, I32, U32 = jnp.float32, jnp.bfloat16, jnp.int32, jnp.uint32

D_MODEL = 1024
GRID_W = 64
EPS = 1e-6
GLA_HEADS, GLA_DK, GLA_DV, GLA_RANK, GLA_TAU, GLA_CHUNK = 4, 64, 128, 16, 16.0, 64
GLA_BATCH = 2
AB_QK, AB_V = GLA_HEADS * GLA_DK, GLA_HEADS * GLA_DV
S5_CH, S5_GROUP, S5_GROUPS, S5_P = 512, 16, 32, 64
S5_CHUNK = 16
RET_HEADS, RET_DK, RET_DV = 4, 256, 512
RET_CHUNK = 256
RET_QK, RET_MIX = RET_HEADS * RET_DK, RET_HEADS * RET_DV
ROPE_BASE = 10000.0
N_EXPERTS, TOP_K, D_FF = 32, 4, 1024
SWIGLU_LIMIT, SWIGLU_ALPHA = 7.0, 1.702
MOE_BLOCK = 256
TOKEN_TILE = 256
ADA_TILE = 768
VMEM_LIMIT = 56 * 1024 * 1024
SC_CORES, SC_SUBCORES = 2, 16
SC_WORKERS = SC_CORES * SC_SUBCORES
SC_CHUNK = 64


def _params(*sem):
    return pltpu.CompilerParams(dimension_semantics=sem, vmem_limit_bytes=VMEM_LIMIT)


def _dot(a, b):
    return jnp.dot(a, b, preferred_element_type=F32)


def _dot_nt(a, b):
    return lax.dot_general(a, b, (((1,), (1,)), ((), ())), preferred_element_type=F32)


def _dot_tn(a, b):
    return lax.dot_general(a, b, (((0,), (0,)), ((), ())), preferred_element_type=F32)


def _split(a):
    hi = a.astype(BF16)
    return hi, (a - hi.astype(F32)).astype(BF16)


def _dot3(a, b, dot=_dot):
    ah, al = _split(a)
    bh, bl = _split(b)
    return dot(ah, bh) + (dot(ah, bl) + dot(al, bh))


def _pack_rows(x):
    h = x.shape[1] // 2
    lo = lax.bitcast_convert_type(x[:, 0:h].astype(BF16).astype(F32), U32)
    hi = lax.bitcast_convert_type(x[:, h:2 * h].astype(BF16).astype(F32), U32)
    return hi | (lo >> 16)


def _unpack_rows(p):
    lo = lax.bitcast_convert_type(p << 16, F32)
    hi = lax.bitcast_convert_type(p & jnp.uint32(0xFFFF0000), F32)
    return lo, hi


def _silu(x):
    return x * jax.nn.sigmoid(x)


def _norm_mod(x, g, shift, scale):
    r = lax.rsqrt(jnp.mean(x * x, axis=-1, keepdims=True) + EPS)
    return (x * r * g) * (1.0 + scale) + shift


def _const_spec(shape):
    nd = len(shape)
    return pl.BlockSpec(shape, lambda *_: (0,) * nd)


def _ada_body(c_ref, w_ref, b_ref, o_ref):
    o_ref[0] = _dot3(_silu(c_ref[...]), w_ref[0]) + b_ref[0]


def _ada_mod(cvec, ada_w, ada_b):
    depth, d, n6 = ada_w.shape
    rows = cvec.shape[0]
    return pl.pallas_call(
        _ada_body,
        grid=(depth, n6 // ADA_TILE),
        in_specs=[_const_spec((rows, d)),
                  pl.BlockSpec((1, d, ADA_TILE), lambda l, j: (l, 0, j)),
                  pl.BlockSpec((1, 1, ADA_TILE), lambda l, j: (l, 0, j))],
        out_specs=pl.BlockSpec((1, rows, ADA_TILE), lambda l, j: (l, 0, j)),
        out_shape=jax.ShapeDtypeStruct((depth, rows, n6), F32),
        compiler_params=_params("arbitrary", "arbitrary"),
        name="ada_mod",
    )(cvec, ada_w, ada_b.reshape(depth, 1, n6))


def _tile_specs(nct, d):
    x_spec = pl.BlockSpec((1, TOKEN_TILE, d), lambda b, i: (b, i, 0))
    mod_spec = pl.BlockSpec((1, 1, 6, d), lambda b, i: (b, (i >= nct).astype(I32), 0, 0))
    return x_spec, mod_spec


def _inproj0_body(x_ref, mod_ref, g_ref, *refs):
    n = len(refs) // 2
    m = mod_ref[0, 0]
    h = _norm_mod(x_ref[0], g_ref[...], m[0:1], m[1:2]).astype(BF16)
    for w_ref, o_ref in zip(refs[:n], refs[n:]):
        o_ref[0] = _dot(h, w_ref[...]).astype(o_ref.dtype)


def _inproj0(x_all, mods, norm_g, weights, out_dtypes, nct):
    bsz, t, d = x_all.shape
    x_spec, mod_spec = _tile_specs(nct, d)
    return pl.pallas_call(
        _inproj0_body,
        grid=(bsz, t // TOKEN_TILE),
        in_specs=[x_spec, mod_spec, _const_spec((1, d))] + [_const_spec(w.shape) for w in weights],
        out_specs=[pl.BlockSpec((1, TOKEN_TILE, w.shape[1]), lambda b, i: (b, i, 0)) for w in weights],
        out_shape=[jax.ShapeDtypeStruct((bsz, t, w.shape[1]), dt) for w, dt in zip(weights, out_dtypes)],
        compiler_params=_params("arbitrary", "arbitrary"),
        name="inproj_gla_s5",
    )(x_all, mods, norm_g.reshape(1, d), *weights)


def _rope(acc, cos_ref, sin_ref, o_ref, scale):
    for grp in range(acc.shape[1] // 128):
        half = grp % 2
        xg = acc[:, grp * 128:(grp + 1) * 128]
        cs = cos_ref[:, half * 128:(half + 1) * 128]
        sn = sin_ref[:, half * 128:(half + 1) * 128]
        out = xg * cs + pltpu.roll(xg, 64, 1) * sn
        o_ref[0, :, grp * 128:(grp + 1) * 128] = (out * scale).astype(o_ref.dtype)


def _inproj1_body(x_ref, mod_ref, g_ref, cos_ref, sin_ref, wq, wk, wv, wg, oq, ok, ov, og):
    m = mod_ref[0, 0]
    h = _norm_mod(x_ref[0], g_ref[...], m[0:1], m[1:2]).astype(BF16)
    _rope(_dot(h, wq[...]), cos_ref, sin_ref, oq, 1.0)
    _rope(_dot(h, wk[...]), cos_ref, sin_ref, ok, RET_DK ** -0.5)
    ov[0] = _dot(h, wv[...]).astype(ov.dtype)
    og[0] = _dot(h, wg[...]).astype(og.dtype)


def _inproj1(x_all, mods, norm_g, cos_t, sin_t, weights, nct):
    bsz, t, d = x_all.shape
    x_spec, mod_spec = _tile_specs(nct, d)
    tab_spec = pl.BlockSpec((TOKEN_TILE, RET_DK), lambda b, i: (i, 0))
    return pl.pallas_call(
        _inproj1_body,
        grid=(bsz, t // TOKEN_TILE),
        in_specs=[x_spec, mod_spec, _const_spec((1, d)), tab_spec, tab_spec] + [_const_spec(w.shape) for w in weights],
        out_specs=[pl.BlockSpec((1, TOKEN_TILE, w.shape[1]), lambda b, i: (b, i, 0)) for w in weights],
        out_shape=[jax.ShapeDtypeStruct((bsz, t, w.shape[1]), BF16) for w in weights],
        compiler_params=_params("arbitrary", "arbitrary"),
        name="inproj_retention",
    )(x_all, mods, norm_g.reshape(1, d), cos_t, sin_t, *weights)


def _backward_chunk(n, n_ctx_chunks, n_chunks):
    return jnp.where(n < n_ctx_chunks, n_ctx_chunks - 1 - n, n_chunks - 1 - (n - n_ctx_chunks))


def _gla_body(q_f, k_f, v_f, low_f, q_b, k_b, v_b, low_b, wa_ref, ba_ref, hmask_ref, bdmask_ref,
              o_f, o_b, st_f, st_b):
    c = GLA_CHUNK

    @pl.when(pl.program_id(1) == 0)
    def _():
        st_f[...] = jnp.zeros_like(st_f)
        st_b[...] = jnp.zeros_like(st_b)

    ii = lax.broadcasted_iota(I32, (c, c), 0)
    jj = lax.broadcasted_iota(I32, (c, c), 1)
    r4 = lax.broadcasted_iota(I32, (GLA_HEADS * c, c), 0) & (c - 1)
    c4 = lax.broadcasted_iota(I32, (GLA_HEADS * c, c), 1)
    dirs = ((q_f, k_f, v_f, low_f, o_f, st_f), (q_b, k_b, v_b, low_b, o_b, st_b))
    for bb in range(q_f.shape[0]):
        for d, (q_ref, k_ref, v_ref, low_ref, o_ref, st_ref) in enumerate(dirs):
            seen = (jj <= ii) if d == 0 else (jj >= ii)
            seen4 = (c4 <= r4) if d == 0 else (c4 >= r4)
            z = _dot3(low_ref[bb], wa_ref[d]) + ba_ref[d]
            log_a = (jnp.minimum(z, 0.0) - jnp.log1p(jnp.exp(-jnp.abs(z)))) * (1.0 / GLA_TAU)
            tri = jnp.where(seen, 1.0, 0.0).astype(BF16)
            la_hi, la_lo = _split(log_a)
            cum = _dot(tri, la_hi) + _dot(tri, la_lo)
            tot = jnp.sum(log_a, axis=0, keepdims=True)
            k = k_ref[bb]
            q_dec = q_ref[bb] * (GLA_DK ** -0.5) * jnp.exp(cum)
            k_inv = (k * jnp.exp(-cum)).astype(BF16)
            k_state = (k * jnp.exp(tot - cum)).astype(BF16)
            q_heads = (jnp.concatenate([q_dec] * GLA_HEADS, axis=0) * hmask_ref[...]).astype(BF16)
            scores = jnp.where(seen4, _dot_nt(q_heads, k_inv), 0.0).astype(BF16)
            v = v_ref[bb]
            st = st_ref[bb]
            o_intra = jnp.concatenate(
                [_dot(scores[h * c:(h + 1) * c], v[:, h * GLA_DV:(h + 1) * GLA_DV]) for h in range(GLA_HEADS)],
                axis=1)
            o_ref[bb] = o_intra + _dot_nt(q_dec.astype(BF16), st.astype(BF16))
            st_ref[bb] = st * jnp.exp(tot) + bdmask_ref[...] * _dot_tn(v, k_state)


def _gla(q, k, v, low, wa_pad, ba, n_ctx):
    bsz, t, _ = q.shape
    nc, ncc = t // GLA_CHUNK, n_ctx // GLA_CHUNK
    gb = GLA_BATCH
    fwd = lambda b, n: (b, n, 0)
    bwd = lambda b, n: (b, _backward_chunk(n, ncc, nc), 0)
    hmask = (jnp.arange(AB_QK)[:, None] // GLA_CHUNK == jnp.arange(AB_QK)[None, :] // GLA_DK).astype(F32)
    bdmask = (jnp.arange(AB_V)[:, None] // GLA_DV == jnp.arange(AB_QK)[None, :] // GLA_DK).astype(F32)

    def specs(idx):
        return [pl.BlockSpec((gb, GLA_CHUNK, AB_QK), idx), pl.BlockSpec((gb, GLA_CHUNK, AB_QK), idx),
                pl.BlockSpec((gb, GLA_CHUNK, AB_V), idx), pl.BlockSpec((gb, GLA_CHUNK, 2 * GLA_RANK), idx)]

    return pl.pallas_call(
        _gla_body,
        grid=(bsz // gb, nc),
        in_specs=specs(fwd) + specs(bwd) + [_const_spec(wa_pad.shape), _const_spec(ba.shape),
                                            _const_spec(hmask.shape), _const_spec(bdmask.shape)],
        out_specs=[pl.BlockSpec((gb, GLA_CHUNK, AB_V), fwd), pl.BlockSpec((gb, GLA_CHUNK, AB_V), bwd)],
        out_shape=[jax.ShapeDtypeStruct((bsz, t, AB_V), F32)] * 2,
        scratch_shapes=[pltpu.VMEM((gb, AB_V, AB_QK), F32)] * 2,
        compiler_params=_params("arbitrary", "arbitrary"),
        name="gla_scan",
    )(q, k, v, low, q, k, v, low, wa_pad, ba, hmask, bdmask)


def _s5_operators(lam_re, lam_im, log_step, b_re, b_im, c_re, c_im):
    hp = lax.Precision.HIGHEST
    ln = S5_CHUNK
    step = jnp.exp(log_step.astype(F32))[..., None]
    lam_re, lam_im = lam_re.astype(F32), lam_im.astype(F32)
    mag = jnp.exp(lam_re * step)
    a_re, a_im = mag * jnp.cos(lam_im * step), mag * jnp.sin(lam_im * step)
    den = lam_re * lam_re + lam_im * lam_im
    f_re = ((a_re - 1.0) * lam_re + a_im * lam_im) / den
    f_im = (a_im * lam_re - (a_re - 1.0) * lam_im) / den
    bb_re = f_re[..., None] * b_re - f_im[..., None] * b_im
    bb_im = f_re[..., None] * b_im + f_im[..., None] * b_re
    pw_re, pw_im = [jnp.ones_like(a_re)], [jnp.zeros_like(a_im)]
    for _ in range(ln):
        pr, pi = pw_re[-1], pw_im[-1]
        pw_re.append(pr * a_re - pi * a_im)
        pw_im.append(pr * a_im + pi * a_re)
    pw_re, pw_im = jnp.stack(pw_re, 1), jnp.stack(pw_im, 1)
    ca_re = c_re[:, None] * pw_re[:, :, :, None, :] - c_im[:, None] * pw_im[:, :, :, None, :]
    ca_im = c_re[:, None] * pw_im[:, :, :, None, :] + c_im[:, None] * pw_re[:, :, :, None, :]
    kern = (jnp.einsum('dtgcp,dgpe->dtgec', ca_re[:, :ln], bb_re, precision=hp)
            - jnp.einsum('dtgcp,dgpe->dtgec', ca_im[:, :ln], bb_im, precision=hp))
    pos = jnp.arange(ln)
    i_out, j_in = pos[None, :], pos[:, None]
    tz, wx, wy, ac = [], [], [], []
    for d in range(2):
        lag = (i_out - j_in) if d == 0 else (j_in - i_out)
        blk = jnp.where((lag >= 0)[:, :, None, None, None], kern[d][jnp.clip(lag, 0, ln - 1)], 0.0)
        tz.append(blk.transpose(2, 0, 3, 1, 4).reshape(S5_GROUPS, ln * S5_GROUP, ln * S5_GROUP))
        p_in = (ln - 1 - pos) if d == 0 else pos
        ar, ai = pw_re[d][p_in], pw_im[d][p_in]
        x_re = ar[..., None] * bb_re[d][None] - ai[..., None] * bb_im[d][None]
        x_im = ar[..., None] * bb_im[d][None] + ai[..., None] * bb_re[d][None]
        to_rows = lambda m: m.transpose(1, 0, 3, 2).reshape(S5_GROUPS, ln * S5_GROUP, S5_P)
        wx.append(jnp.concatenate([to_rows(x_re), to_rows(x_im), to_rows(x_im), to_rows(x_re)], axis=-1))
        p_out = (pos + 1) if d == 0 else (ln - pos)
        to_cols = lambda m: m.transpose(1, 3, 0, 2).reshape(S5_GROUPS, S5_P, ln * S5_GROUP)
        wy.append(jnp.concatenate([to_cols(ca_re[d][p_out]), -to_cols(ca_im[d][p_out])], axis=1))
        lr, li = pw_re[d][ln], pw_im[d][ln]
        rows = [jnp.concatenate([lr, lr], -1), jnp.concatenate([-li, li], -1), jnp.concatenate([li, -li], -1)]
        ac.append(jnp.stack(rows + [jnp.zeros_like(rows[0])] * 5, axis=1))
    return (jnp.stack(tz).astype(BF16), jnp.stack(wx).astype(BF16), jnp.stack(wy).astype(BF16), jnp.stack(ac))


def _s5_body(ncs_ctx, ncs, rows, u_ref, tz_ref, wx_ref, wy_ref, ac_ref, y_ref, xx_f, xx_b, sin_f, sin_b):
    u = u_ref[0]
    xx_f[...] = _dot(u, wx_ref[0, 0])
    xx_b[...] = _dot(u, wx_ref[1, 0])
    ac_f, ac_b = ac_ref[0, 0], ac_ref[1, 0]
    half = 2 * S5_P

    def advance(ac, s, s_sw, xx):
        return (ac[0:1] * s + ac[1:2] * s_sw + xx[:, :half], ac[0:1] * s_sw + ac[2:3] * s + xx[:, half:])

    def step(n, carry):
        s_f, sw_f, s_b, sw_b = carry
        r_f = pl.multiple_of(n * rows, rows)
        r_b = pl.multiple_of(_backward_chunk(n, ncs_ctx, ncs) * rows, rows)
        sin_f[pl.ds(r_f, rows), :] = s_f
        sin_b[pl.ds(r_b, rows), :] = s_b
        s_f, sw_f = advance(ac_f, s_f, sw_f, xx_f[pl.ds(r_f, rows), :])
        s_b, sw_b = advance(ac_b, s_b, sw_b, xx_b[pl.ds(r_b, rows), :])
        return s_f, sw_f, s_b, sw_b

    zero = jnp.zeros((rows, half), F32)
    lax.fori_loop(0, ncs, step, (zero, zero, zero, zero))
    y_ref[0] = (_dot(u, tz_ref[0, 0]) + _dot(u, tz_ref[1, 0])
                + _dot(sin_f[...].astype(BF16), wy_ref[0, 0]) + _dot(sin_b[...].astype(BF16), wy_ref[1, 0]))


def _s5(u, ops, n_ctx):
    tz, wx, wy, ac = ops
    bsz, t, _ = u.shape
    ln, lanes = S5_CHUNK, S5_CHUNK * S5_GROUP
    ncs, ncs_ctx = t // ln, n_ctx // ln
    m = ncs * bsz
    ug = u.astype(BF16).reshape(bsz, ncs, ln, S5_GROUPS, S5_GROUP).transpose(3, 1, 0, 2, 4).reshape(S5_GROUPS, m, lanes)
    dir_spec = lambda shape: pl.BlockSpec((2, 1) + shape, lambda g: (0, g, 0, 0))
    y = pl.pallas_call(
        functools.partial(_s5_body, ncs_ctx, ncs, bsz),
        grid=(S5_GROUPS,),
        in_specs=[pl.BlockSpec((1, m, lanes), lambda g: (g, 0, 0)), dir_spec((lanes, lanes)),
                  dir_spec((lanes, 4 * S5_P)), dir_spec((2 * S5_P, lanes)), dir_spec((8, 2 * S5_P))],
        out_specs=pl.BlockSpec((1, m, lanes), lambda g: (g, 0, 0)),
        out_shape=jax.ShapeDtypeStruct((S5_GROUPS, m, lanes), F32),
        scratch_shapes=[pltpu.VMEM((m, 4 * S5_P), F32)] * 2 + [pltpu.VMEM((m, 2 * S5_P), F32)] * 2,
        compiler_params=_params("arbitrary"),
        name="s5_scan",
    )(ug, tz, wx, wy, ac)
    return y.reshape(S5_GROUPS, ncs, bsz, ln, S5_GROUP).transpose(2, 1, 3, 0, 4).reshape(bsz, t, S5_CH)


def _ret_body(ncc, q_f, k_f, v_f, q_b, k_b, v_b, dmat_ref, rsc_ref, csc_ref, gam_ref, o_f, o_b, st_f, st_b):
    n = pl.program_id(1)

    @pl.when(n == 0)
    def _():
        st_f[...] = jnp.zeros_like(st_f)
        st_b[...] = jnp.zeros_like(st_b)

    dirs = ((q_f, k_f, v_f, o_f, st_f), (q_b, k_b, v_b, o_b, st_b))

    @pl.when(n >= ncc)
    def _():
        for d, (q_ref, k_ref, v_ref, o_ref, st_ref) in enumerate(dirs):
            for h in range(RET_HEADS):
                qh = q_ref[0, :, h * RET_DK:(h + 1) * RET_DK]
                kh = k_ref[0, :, h * RET_DK:(h + 1) * RET_DK]
                vh = v_ref[0, :, h * RET_DV:(h + 1) * RET_DV]
                scores = (_dot_nt(qh, kh) * dmat_ref[d, h]).astype(BF16)
                o = _dot(scores, vh) + rsc_ref[d, h] * _dot(qh, st_ref[h].astype(BF16))
                o_ref[0, :, h * RET_DV:(h + 1) * RET_DV] = o.astype(o_ref.dtype)

    for d, (q_ref, k_ref, v_ref, o_ref, st_ref) in enumerate(dirs):
        for h in range(RET_HEADS):
            kh = k_ref[0, :, h * RET_DK:(h + 1) * RET_DK]
            vh = v_ref[0, :, h * RET_DV:(h + 1) * RET_DV]
            k_state = (kh.astype(F32) * csc_ref[d, h]).astype(BF16)
            st_ref[h] = st_ref[h] * gam_ref[d, h] + _dot_tn(k_state, vh)


def _retention(q, k, v, decay_logit, n_ctx):
    bsz, t, _ = q.shape
    c = RET_CHUNK
    nc, ncc = t // c, n_ctx // c
    nl = nc - ncc
    log_gamma = jax.nn.log_sigmoid(decay_logit.astype(F32))[:, :, None, None]
    i = jnp.arange(c, dtype=F32)
    lag = i[:, None] - i[None, :]
    lag = jnp.stack([lag, -lag])[:, None]
    dmat = jnp.where(lag >= 0, jnp.exp(log_gamma * jnp.maximum(lag, 0.0)), 0.0)
    done = jnp.stack([i + 1.0, c - i])[:, None, :, None]
    rsc = jnp.exp(log_gamma * done)
    csc = jnp.exp(log_gamma * (c - done))
    gam = jnp.exp(log_gamma[:, :, 0, 0] * c)
    fwd = lambda b, n: (b, n, 0)
    bwd = lambda b, n: (b, _backward_chunk(n, ncc, nc), 0)
    o_fwd = lambda b, n: (b, jnp.maximum(n - ncc, 0), 0)
    o_bwd = lambda b, n: (b, nl - 1 - jnp.maximum(n - ncc, 0), 0)

    def specs(idx):
        return [pl.BlockSpec((1, c, RET_QK), idx), pl.BlockSpec((1, c, RET_QK), idx), pl.BlockSpec((1, c, RET_MIX), idx)]

    return pl.pallas_call(
        functools.partial(_ret_body, ncc),
        grid=(bsz, nc),
        in_specs=specs(fwd) + specs(bwd) + [_const_spec(dmat.shape), _const_spec(rsc.shape), _const_spec(csc.shape),
                                            pl.BlockSpec(memory_space=pltpu.SMEM)],
        out_specs=[pl.BlockSpec((1, c, RET_MIX), o_fwd), pl.BlockSpec((1, c, RET_MIX), o_bwd)],
        out_shape=[jax.ShapeDtypeStruct((bsz, nl * c, RET_MIX), BF16)] * 2,
        scratch_shapes=[pltpu.VMEM((RET_HEADS, RET_DK, RET_DV), F32)] * 2,
        compiler_params=_params("arbitrary", "arbitrary"),
        name="retention_scan",
    )(q, k, v, q, k, v, dmat, rsc, csc, gam)


def _route(x, mixed, mod, n2g_ref, wr_ref, br_ref, x1_ref, h2_ref, e_ref, w_ref, r_ref, cnt_ref):
    tm = x.shape[0]
    x1 = x + mod[2:3] * mixed
    x1_ref[0] = x1
    h2 = _norm_mod(x1, n2g_ref[...], mod[3:4], mod[4:5])
    h2_ref[0] = _pack_rows(h2)
    logits = _dot3(wr_ref[...], h2, dot=_dot_nt) + br_ref[...]
    ie = lax.broadcasted_iota(I32, logits.shape, 0)
    tops, picks = [], []
    for _ in range(TOP_K):
        mx = jnp.max(logits, axis=0, keepdims=True)
        pick = jnp.min(jnp.where(logits == mx, ie, N_EXPERTS), axis=0, keepdims=True)
        tops.append(mx)
        picks.append(pick)
        logits = jnp.where(ie == pick, -jnp.inf, logits)
    ex = [jnp.exp(tk - tops[0]) for tk in tops]
    den = ex[0] + ex[1] + ex[2] + ex[3]
    for kk in range(TOP_K):
        w_ref[0, kk:kk + 1, :] = ex[kk] / den
        e_ref[0, kk:kk + 1, :] = picks[kk]

    @pl.when((pl.program_id(0) == 0) & (pl.program_id(1) == 0))
    def _():
        cnt_ref[...] = jnp.zeros_like(cnt_ref)

    earlier = (lax.broadcasted_iota(I32, (tm, tm), 0) < lax.broadcasted_iota(I32, (tm, tm), 1))
    earlier = jnp.where(earlier, 1.0, 0.0).astype(BF16)
    run = cnt_ref[:, 0:1]
    for kk, pick in enumerate(picks):
        onehot = jnp.where(ie == pick, 1.0, 0.0)
        before = _dot(onehot.astype(BF16), earlier) + run
        r_ref[0, kk:kk + 1, :] = jnp.sum(onehot * before, axis=0, keepdims=True).astype(I32)
        run = run + jnp.sum(onehot, axis=1, keepdims=True)
    cnt_ref[...] = jnp.broadcast_to(run, cnt_ref.shape)


def _mix0_body(x_ref, mod_ref, of_ref, ob_ref, g_ref, ys_ref, u_ref, gng_ref, dsk_ref, gluw_ref, glub_ref,
               wo_ref, n2g_ref, wr_ref, br_ref, x1_ref, h2_ref, e_ref, w_ref, r_ref, cnt_ref):
    o = of_ref[0] + ob_ref[0]
    heads = []
    for h in range(GLA_HEADS):
        oh = o[:, h * GLA_DV:(h + 1) * GLA_DV]
        heads.append(oh * lax.rsqrt(jnp.mean(oh * oh, axis=-1, keepdims=True) + EPS))
    gla = jnp.concatenate(heads, axis=1) * gng_ref[...] * _silu(g_ref[0].astype(F32))
    y = jax.nn.gelu(ys_ref[0] + dsk_ref[...] * u_ref[0].astype(F32))
    y = y * jax.nn.sigmoid(_dot(y.astype(BF16), gluw_ref[...]) + glub_ref[...])
    mixed = _dot(gla.astype(BF16), wo_ref[0:AB_V]) + _dot(y.astype(BF16), wo_ref[AB_V:AB_V + S5_CH])
    _route(x_ref[0], mixed, mod_ref[0, 0], n2g_ref, wr_ref, br_ref, x1_ref, h2_ref, e_ref, w_ref, r_ref, cnt_ref)


def _mix1_body(x_ref, mod_ref, of_ref, ob_ref, g_ref, ng_ref, wo_ref, n2g_ref, wr_ref, br_ref,
               x1_ref, h2_ref, e_ref, w_ref, r_ref, cnt_ref):
    mixed = None
    for h in range(RET_HEADS):
        sl = slice(h * RET_DV, (h + 1) * RET_DV)
        oh = of_ref[0, :, sl].astype(F32) + ob_ref[0, :, sl].astype(F32)
        mu = jnp.mean(oh, axis=-1, keepdims=True)
        cen = oh - mu
        var = jnp.mean(cen * cen, axis=-1, keepdims=True)
        gated = cen * lax.rsqrt(var + EPS) * ng_ref[:, sl] * _silu(g_ref[0, :, sl].astype(F32))
        part = _dot(gated.astype(BF16), wo_ref[sl])
        mixed = part if mixed is None else mixed + part
    _route(x_ref[0], mixed, mod_ref[0, 0], n2g_ref, wr_ref, br_ref, x1_ref, h2_ref, e_ref, w_ref, r_ref, cnt_ref)


def _mix_call(body, name, x_all, mods, tiles, acts, consts, norm2_g, w_router, b_router, n_tok, seg_tile0):
    bsz, _, d = x_all.shape
    tm = TOKEN_TILE
    off = lambda b, i: (b, i + seg_tile0, 0)
    loc = lambda b, i: (b, i, 0)
    ntl = bsz * tiles
    flat = lambda b, i: (b * tiles + i, 0, 0)
    in_specs = [pl.BlockSpec((1, tm, d), off),
                pl.BlockSpec((1, 1, 6, d), lambda b, i: (b, ((i + seg_tile0) >= n_tok).astype(I32), 0, 0))]
    args = [x_all, mods]
    for arr, offset in acts:
        in_specs.append(pl.BlockSpec((1, tm, arr.shape[2]), off if offset else loc))
        args.append(arr)
    tail = list(consts) + [norm2_g.reshape(1, d), w_router.T, b_router.reshape(N_EXPERTS, 1)]
    in_specs += [_const_spec(a.shape) for a in tail]
    args += tail
    tok_out = pl.BlockSpec((1, TOP_K, tm), flat)
    return pl.pallas_call(
        body,
        grid=(bsz, tiles),
        in_specs=in_specs,
        out_specs=[pl.BlockSpec((1, tm, d), loc), pl.BlockSpec((1, tm, d // 2), loc), tok_out, tok_out, tok_out,
                   _const_spec((N_EXPERTS, 128))],
        out_shape=[jax.ShapeDtypeStruct((bsz, tiles * tm, d), F32), jax.ShapeDtypeStruct((bsz, tiles * tm, d // 2), U32),
                   jax.ShapeDtypeStruct((ntl, TOP_K, tm), I32), jax.ShapeDtypeStruct((ntl, TOP_K, tm), F32),
                   jax.ShapeDtypeStruct((ntl, TOP_K, tm), I32), jax.ShapeDtypeStruct((N_EXPERTS, 128), F32)],
        compiler_params=_params("arbitrary", "arbitrary"),
        name=name,
    )(*args)


def _cast_rows(src_ref, dst_ref, rows):
    def chunk(j, carry):
        r = pl.multiple_of(j * rows, rows)
        dst_ref[pl.ds(r, rows), :] = src_ref[0, 0, pl.ds(r, rows), :].astype(BF16)
        return carry

    lax.fori_loop(0, dst_ref.shape[0] // rows, chunk, 0)


def _expert_body(be_ref, nu_ref, x_ref, wgu_ref, bgu_ref, wd_ref, bd_ref, o_ref, wgu_bf, wd_bf):
    i = pl.program_id(0)
    live = i < nu_ref[0]
    new_expert = (i == 0) | (be_ref[i] != be_ref[jnp.maximum(i - 1, 0)])

    @pl.when(live & new_expert)
    def _():
        _cast_rows(wgu_ref, wgu_bf, 128)
        _cast_rows(wd_ref, wd_bf, 128)

    @pl.when(live)
    def _():
        x_lo, x_hi = _unpack_rows(x_ref[...])
        half = x_lo.shape[1]
        gu = (_dot(x_lo.astype(BF16), wgu_bf[0:half]) + _dot(x_hi.astype(BF16), wgu_bf[half:2 * half])
              + bgu_ref[0, 0])
        gate = jnp.minimum(gu[:, :D_FF], SWIGLU_LIMIT)
        lin = jnp.clip(gu[:, D_FF:], -SWIGLU_LIMIT, SWIGLU_LIMIT)
        act = gate * jax.nn.sigmoid(SWIGLU_ALPHA * gate) * (lin + 1.0)
        y = _dot(act.astype(BF16), wd_bf[...]) + bd_ref[0, 0]
        o_ref[...] = _pack_rows(y)

    @pl.when(i >= nu_ref[0])
    def _():
        o_ref[...] = jnp.zeros_like(o_ref)


def _experts(xb, block_e, n_used, layer, w_gu, b_gu, w_down, b_down):
    n_slots, half = xb.shape
    d = 2 * half
    n_blocks = n_slots // MOE_BLOCK
    depth = w_gu.shape[0]
    by_expert = lambda i, be, nu: (layer, be[i], 0, 0)
    return pl.pallas_call(
        _expert_body,
        grid_spec=pltpu.PrefetchScalarGridSpec(
            num_scalar_prefetch=2,
            grid=(n_blocks,),
            in_specs=[pl.BlockSpec((MOE_BLOCK, half), lambda i, be, nu: (i, 0)),
                      pl.BlockSpec((1, 1, d, 2 * D_FF), by_expert), pl.BlockSpec((1, 1, 1, 2 * D_FF), by_expert),
                      pl.BlockSpec((1, 1, D_FF, d), by_expert), pl.BlockSpec((1, 1, 1, d), by_expert)],
            out_specs=pl.BlockSpec((MOE_BLOCK, half), lambda i, be, nu: (i, 0)),
            scratch_shapes=[pltpu.VMEM((d, 2 * D_FF), BF16), pltpu.VMEM((D_FF, d), BF16)]),
        out_shape=jax.ShapeDtypeStruct((n_slots, half), U32),
        compiler_params=_params("arbitrary"),
        name="moe_experts",
    )(block_e, n_used, xb, w_gu, b_gu.reshape(depth, N_EXPERTS, 1, 2 * D_FF), w_down,
      b_down.reshape(depth, N_EXPERTS, 1, d))


def _combine_body(final, x1_ref, mod_ref, yk_ref, w_ref, *refs):
    d = x1_ref.shape[2]
    half = d // 2
    y_lo, y_hi = None, None
    for k in range(TOP_K):
        lo, hi = _unpack_rows(yk_ref[k, 0])
        wk = w_ref[0, :, k:k + 1]
        y_lo = lo * wk if y_lo is None else y_lo + lo * wk
        y_hi = hi * wk if y_hi is None else y_hi + hi * wk
    g2 = mod_ref[0, 0][5:6]
    x2_lo = x1_ref[0, :, 0:half] + g2[:, 0:half] * y_lo
    x2_hi = x1_ref[0, :, half:d] + g2[:, half:d] * y_hi
    if final:
        fg_ref, o_ref = refs
        ms = (jnp.sum(x2_lo * x2_lo, axis=-1, keepdims=True) + jnp.sum(x2_hi * x2_hi, axis=-1, keepdims=True)) / d
        r = lax.rsqrt(ms + EPS)
        x2_lo = x2_lo * r * fg_ref[:, 0:half]
        x2_hi = x2_hi * r * fg_ref[:, half:d]
    else:
        (o_ref,) = refs
    o_ref[0, :, 0:half] = x2_lo
    o_ref[0, :, half:d] = x2_hi


def _combine(x1, mods, yk, w_tok, seg_tile0, n_tok, final_g):
    bsz, t, d = x1.shape
    tm = TOKEN_TILE
    tiles = t // tm
    loc = lambda b, i: (b, i, 0)
    in_specs = [pl.BlockSpec((1, tm, d), loc),
                pl.BlockSpec((1, 1, 6, d), lambda b, i: (b, ((i + seg_tile0) >= n_tok).astype(I32), 0, 0)),
                pl.BlockSpec((TOP_K, 1, tm, d // 2), lambda b, i: (0, b, i, 0)),
                pl.BlockSpec((1, tm, TOP_K), loc)]
    args = [x1, mods, yk.reshape(TOP_K, bsz, t, d // 2), w_tok.reshape(bsz, t, TOP_K)]
    if final_g is not None:
        in_specs.append(_const_spec((1, d)))
        args.append(final_g.reshape(1, d))
    return pl.pallas_call(
        functools.partial(_combine_body, final_g is not None),
        grid=(bsz, tiles),
        in_specs=in_specs,
        out_specs=pl.BlockSpec((1, tm, d), loc),
        out_shape=jax.ShapeDtypeStruct((bsz, t, d), F32),
        compiler_params=_params("arbitrary", "arbitrary"),
        name="moe_combine",
    )(*args)


def _sc_mesh():
    return plsc.VectorSubcoreMesh(core_axis_name="core", subcore_axis_name="subcore",
                                  num_cores=SC_CORES, num_subcores=SC_SUBCORES)


def _sc_worker_base(per_worker):
    return (lax.axis_index("subcore") * SC_CORES + lax.axis_index("core")) * per_worker


def _sc_dispatch(rows, dest, n_slots):
    n, w = rows.shape
    per_worker = n // SC_WORKERS
    assert per_worker * SC_WORKERS == n and per_worker % SC_CHUNK == 0

    @functools.partial(
        pl.kernel, mesh=_sc_mesh(), out_type=jax.ShapeDtypeStruct((n_slots, w), rows.dtype),
        scratch_types=[pltpu.VMEM((SC_CHUNK,), I32)] * TOP_K + [pltpu.VMEM((SC_CHUNK, w), rows.dtype),
                                                                pltpu.SemaphoreType.DMA],
        name="moe_dispatch")
    def scatter_rows(rows_hbm, dest_hbm, out_hbm, *scratch):
        idx_refs, buf, sem = scratch[:TOP_K], scratch[TOP_K], scratch[TOP_K + 1]
        base0 = _sc_worker_base(per_worker)

        @pl.loop(0, per_worker // SC_CHUNK)
        def _(j):
            base = base0 + j * SC_CHUNK
            pltpu.sync_copy(rows_hbm.at[pl.ds(base, SC_CHUNK)], buf)
            for k, idx in enumerate(idx_refs):
                pltpu.sync_copy(dest_hbm.at[pl.ds(k * n + base, SC_CHUNK)], idx)
            copies = [pltpu.async_copy(buf, out_hbm.at[idx], sem) for idx in idx_refs]
            for cp in copies:
                cp.wait()

    return scatter_rows(rows, dest)


def _sc_gather(table, idx):
    n = idx.shape[0]
    w = table.shape[1]
    per_worker = n // SC_WORKERS
    assert per_worker * SC_WORKERS == n and per_worker % SC_CHUNK == 0

    @functools.partial(
        pl.kernel, mesh=_sc_mesh(), out_type=jax.ShapeDtypeStruct((n, w), table.dtype),
        scratch_types=[pltpu.VMEM((SC_CHUNK,), I32), pltpu.VMEM((SC_CHUNK, w), table.dtype), pltpu.SemaphoreType.DMA],
        name="moe_gather")
    def gather_rows(table_hbm, idx_hbm, out_hbm, idx_v, buf, sem):
        base0 = _sc_worker_base(per_worker)

        @pl.loop(0, per_worker // SC_CHUNK)
        def _(j):
            base = base0 + j * SC_CHUNK
            pltpu.sync_copy(idx_hbm.at[pl.ds(base, SC_CHUNK)], idx_v)
            pltpu.async_copy(table_hbm.at[idx_v], buf, sem).wait()
            pltpu.sync_copy(buf, out_hbm.at[pl.ds(base, SC_CHUNK)])

    return gather_rows(table, idx)


def _moe(h2, e_tl, w_tl, r_tl, cnt, layer, w_gu, b_gu, w_down, b_down):
    bsz, t, half = h2.shape
    n = bsz * t
    flat = lambda a: a.transpose(1, 0, 2).reshape(TOP_K, n)
    e_k, w_k, r_k = flat(e_tl), flat(w_tl), flat(r_tl)
    counts = cnt[:, 0].astype(I32)
    padded = (counts + MOE_BLOCK - 1) // MOE_BLOCK * MOE_BLOCK
    pad_end = jnp.cumsum(padded)
    pad_start = pad_end - padded
    n_blocks = (n * TOP_K + MOE_BLOCK - 1) // MOE_BLOCK + N_EXPERTS
    block_start = jnp.arange(n_blocks, dtype=I32) * MOE_BLOCK
    block_e = jnp.minimum(jnp.sum((pad_end[None, :] <= block_start[:, None]).astype(I32), axis=1), N_EXPERTS - 1)
    n_used = (pad_end[-1:] // MOE_BLOCK).astype(I32)
    start_k = jnp.sum(jnp.where(e_k[..., None] == jnp.arange(N_EXPERTS, dtype=I32), pad_start, 0), axis=-1)
    dest = (start_k + r_k).reshape(TOP_K * n)
    xb = _sc_dispatch(h2.reshape(n, half), dest, n_blocks * MOE_BLOCK)
    yb = _experts(xb, block_e, n_used, layer, w_gu, b_gu, w_down, b_down)
    return _sc_gather(yb, dest), w_k.T


def _rope_tables(n_ctx, n_lat):
    n_freq = RET_DK // 4
    inv_freq = ROPE_BASE ** (-jnp.arange(n_freq, dtype=F32) / n_freq)
    pos = jnp.arange(n_lat, dtype=I32)
    cos, sin = [], []
    for p in (pos // GRID_W, pos % GRID_W):
        ang = p.astype(F32)[:, None] * inv_freq
        cos += [jnp.cos(ang), jnp.cos(ang)]
        sin += [-jnp.sin(ang), jnp.sin(ang)]
    cos, sin = jnp.concatenate(cos, axis=1), jnp.concatenate(sin, axis=1)
    return (jnp.concatenate([jnp.ones((n_ctx, RET_DK), F32), cos], axis=0),
            jnp.concatenate([jnp.zeros((n_ctx, RET_DK), F32), sin], axis=0))


def kernel(x, c, ctx, c_ctx, ada_w, ada_b, norm1_g, norm2_g, ab_w_in, ab_w_out, gla_wa, gla_ba, gla_norm_g, s5_lam_re, s5_lam_im, s5_log_step, s5_b_re, s5_b_im, s5_c_re, s5_c_im, s5_d, s5_glu_w, s5_glu_b, ret_w_in, ret_w_out, ret_decay_logit, ret_norm_g, moe_w_router, moe_b_router, moe_w_gu, moe_b_gu, moe_w_down, moe_b_down, final_norm_g):
    bsz, n_lat, d = x.shape
    n_ctx = ctx.shape[1]
    depth = ada_w.shape[0]
    assert depth == 2 and d == D_MODEL and bsz == 8, "kernels are laid out for the stated problem shape"
    assert n_ctx % TOKEN_TILE == 0 and n_lat % TOKEN_TILE == 0 and n_lat % GRID_W == 0
    t = n_ctx + n_lat
    nct = n_ctx // TOKEN_TILE

    cvec = jnp.zeros((16, d), F32).at[:bsz].set(c).at[bsz].set(c_ctx)
    mod = _ada_mod(cvec, ada_w, ada_b).reshape(depth, 16, 6, d)
    mods = [jnp.stack([jnp.broadcast_to(mod[l, bsz], (bsz, 6, d)), mod[l, :bsz]], axis=1) for l in range(depth)]

    x_all = jnp.concatenate([ctx, x], axis=1)

    w_in = ab_w_in[0].astype(BF16)
    cuts = [0, AB_QK, 2 * AB_QK, 2 * AB_QK + AB_V, 2 * AB_QK + 2 * AB_V, 2 * AB_QK + 2 * AB_V + 2 * GLA_RANK,
            w_in.shape[1]]
    pieces = [w_in[:, a:b] for a, b in zip(cuts[:-1], cuts[1:])]
    q, k, v, g, low, u = _inproj0(x_all, mods[0], norm1_g[0], pieces, [F32, F32, BF16, BF16, F32, F32], nct)
    wa_pad = jnp.zeros((2, 2 * GLA_RANK, AB_QK), F32)
    wa_pad = wa_pad.at[0, :GLA_RANK].set(gla_wa[0, 0]).at[1, GLA_RANK:].set(gla_wa[0, 1])
    o_f, o_b = _gla(q, k, v, low, wa_pad, gla_ba[0].reshape(2, 1, AB_QK), n_ctx)
    ops = _s5_operators(s5_lam_re[0], s5_lam_im[0], s5_log_step[0], s5_b_re[0], s5_b_im[0], s5_c_re[0], s5_c_im[0])
    ys = _s5(u, ops, n_ctx)
    consts = [jnp.tile(gla_norm_g[0], GLA_HEADS).reshape(1, AB_V), s5_d[0].reshape(1, S5_CH),
              s5_glu_w[0].astype(BF16), s5_glu_b[0].reshape(1, S5_CH), ab_w_out[0].astype(BF16)]
    x1, h2, e_tl, w_tl, r_tl, cnt = _mix_call(
        _mix0_body, "mix_gla_s5", x_all, mods[0], t // TOKEN_TILE,
        [(o_f, False), (o_b, False), (g, False), (ys, False), (u, False)], consts,
        norm2_g[0], moe_w_router[0], moe_b_router[0], nct, 0)
    yk, w_tok = _moe(h2, e_tl, w_tl, r_tl, cnt, 0, moe_w_gu, moe_b_gu, moe_w_down, moe_b_down)
    x_all = _combine(x1, mods[0], yk, w_tok, 0, nct, None)

    w_in = ret_w_in[0].astype(BF16)
    cuts = [0, RET_QK, 2 * RET_QK, 2 * RET_QK + RET_MIX, w_in.shape[1]]
    pieces = [w_in[:, a:b] for a, b in zip(cuts[:-1], cuts[1:])]
    cos_t, sin_t = _rope_tables(n_ctx, n_lat)
    q, k, v, g = _inproj1(x_all, mods[1], norm1_g[1], cos_t, sin_t, pieces, nct)
    o_f, o_b = _retention(q, k, v, ret_decay_logit[0], n_ctx)
    consts = [ret_norm_g[0].reshape(1, RET_MIX), ret_w_out[0].astype(BF16)]
    x1, h2, e_tl, w_tl, r_tl, cnt = _mix_call(
        _mix1_body, "mix_retention", x_all, mods[1], n_lat // TOKEN_TILE,
        [(o_f, False), (o_b, False), (g, True)], consts,
        norm2_g[1], moe_w_router[1], moe_b_router[1], nct, nct)
    yk, w_tok = _moe(h2, e_tl, w_tl, r_tl, cnt, 1, moe_w_gu, moe_b_gu, moe_w_down, moe_b_down)
    return _combine(x1, mods[1], yk, w_tok, nct, nct, final_norm_g)
```

```python
import functools
import math

import jax
import jax.numpy as jnp
from jax import lax
from jax.experimental import pallas as pl
from jax.experimental.pallas import tpu as pltpu
from jax.experimental.pallas import tpu_sc as plsc

F32, BF16, I32, U32 = jnp.float32, jnp.bfloat16, jnp.int32, jnp.uint32

D_MODEL = 1024
GRID_W = 64
EPS = 1e-6
GLA_HEADS, GLA_DK, GLA_DV, GLA_RANK, GLA_TAU, GLA_CHUNK = 4, 64, 128, 16, 16.0, 64
GLA_BATCH = 2
AB_QK, AB_V = GLA_HEADS * GLA_DK, GLA_HEADS * GLA_DV
S5_CH, S5_GROUP, S5_GROUPS, S5_P = 512, 16, 32, 64
S5_CHUNK = 16
RET_HEADS, RET_DK, RET_DV = 4, 256, 512
RET_CHUNK = 256
RET_QK, RET_MIX = RET_HEADS * RET_DK, RET_HEADS * RET_DV
ROPE_BASE = 10000.0
N_EXPERTS, TOP_K, D_FF = 32, 4, 1024
SWIGLU_LIMIT, SWIGLU_ALPHA = 7.0, 1.702
MOE_BLOCK = 512
TOKEN_TILE = 256
ADA_TILE = 768
VMEM_LIMIT = 56 * 1024 * 1024
SC_CORES, SC_SUBCORES = 2, 16
SC_WORKERS = SC_CORES * SC_SUBCORES
SC_CHUNK = 64


def _params(*sem):
    return pltpu.CompilerParams(dimension_semantics=sem, vmem_limit_bytes=VMEM_LIMIT)


def _dot(a, b):
    return jnp.dot(a, b, preferred_element_type=F32)


def _dot_nt(a, b):
    return lax.dot_general(a, b, (((1,), (1,)), ((), ())), preferred_element_type=F32)


def _dot_tn(a, b):
    return lax.dot_general(a, b, (((0,), (0,)), ((), ())), preferred_element_type=F32)


def _split(a):
    hi = a.astype(BF16)
    return hi, (a - hi.astype(F32)).astype(BF16)


def _dot3(a, b, dot=_dot):
    ah, al = _split(a)
    bh, bl = _split(b)
    return dot(ah, bh) + (dot(ah, bl) + dot(al, bh))


def _pack_rows(x):
    h = x.shape[1] // 2
    lo = lax.bitcast_convert_type(x[:, 0:h].astype(BF16).astype(F32), U32)
    hi = lax.bitcast_convert_type(x[:, h:2 * h].astype(BF16).astype(F32), U32)
    return hi | (lo >> 16)


def _unpack_rows(p):
    lo = lax.bitcast_convert_type(p << 16, F32)
    hi = lax.bitcast_convert_type(p & jnp.uint32(0xFFFF0000), F32)
    return lo, hi


def _silu(x):
    return x * jax.nn.sigmoid(x)


def _norm_mod(x, g, shift, scale):
    r = lax.rsqrt(jnp.mean(x * x, axis=-1, keepdims=True) + EPS)
    return (x * r * g) * (1.0 + scale) + shift


def _const_spec(shape):
    nd = len(shape)
    return pl.BlockSpec(shape, lambda *_: (0,) * nd)


def _ada_body(c_ref, w_ref, b_ref, o_ref):
    o_ref[0] = _dot3(_silu(c_ref[...]), w_ref[0]) + b_ref[0]


def _ada_mod(cvec, ada_w, ada_b):
    depth, d, n6 = ada_w.shape
    rows = cvec.shape[0]
    return pl.pallas_call(
        _ada_body,
        grid=(depth, n6 // ADA_TILE),
        in_specs=[_const_spec((rows, d)),
                  pl.BlockSpec((1, d, ADA_TILE), lambda l, j: (l, 0, j)),
                  pl.BlockSpec((1, 1, ADA_TILE), lambda l, j: (l, 0, j))],
        out_specs=pl.BlockSpec((1, rows, ADA_TILE), lambda l, j: (l, 0, j)),
        out_shape=jax.ShapeDtypeStruct((depth, rows, n6), F32),
        compiler_params=_params("arbitrary", "arbitrary"),
        name="ada_mod",
    )(cvec, ada_w, ada_b.reshape(depth, 1, n6))


def _tile_specs(nct, d):
    x_spec = pl.BlockSpec((1, TOKEN_TILE, d), lambda b, i: (b, i, 0))
    mod_spec = pl.BlockSpec((1, 1, 6, d), lambda b, i: (b, (i >= nct).astype(I32), 0, 0))
    return x_spec, mod_spec


def _inproj0_body(x_ref, mod_ref, g_ref, *refs):
    n = len(refs) // 2
    m = mod_ref[0, 0]
    h = _norm_mod(x_ref[0], g_ref[...], m[0:1], m[1:2]).astype(BF16)
    for w_ref, o_ref in zip(refs[:n], refs[n:]):
        o_ref[0] = _dot(h, w_ref[...]).astype(o_ref.dtype)


def _inproj0(x_all, mods, norm_g, weights, out_dtypes, nct):
    bsz, t, d = x_all.shape
    x_spec, mod_spec = _tile_specs(nct, d)
    return pl.pallas_call(
        _inproj0_body,
        grid=(bsz, t // TOKEN_TILE),
        in_specs=[x_spec, mod_spec, _const_spec((1, d))] + [_const_spec(w.shape) for w in weights],
        out_specs=[pl.BlockSpec((1, TOKEN_TILE, w.shape[1]), lambda b, i: (b, i, 0)) for w in weights],
        out_shape=[jax.ShapeDtypeStruct((bsz, t, w.shape[1]), dt) for w, dt in zip(weights, out_dtypes)],
        compiler_params=_params("arbitrary", "arbitrary"),
        name="inproj_gla_s5",
    )(x_all, mods, norm_g.reshape(1, d), *weights)


def _rope(acc, cos_ref, sin_ref, o_ref, scale):
    for grp in range(acc.shape[1] // 128):
        half = grp % 2
        xg = acc[:, grp * 128:(grp + 1) * 128]
        cs = cos_ref[:, half * 128:(half + 1) * 128]
        sn = sin_ref[:, half * 128:(half + 1) * 128]
        out = xg * cs + pltpu.roll(xg, 64, 1) * sn
        o_ref[0, :, grp * 128:(grp + 1) * 128] = (out * scale).astype(o_ref.dtype)


def _inproj1_body(x_ref, mod_ref, g_ref, cos_ref, sin_ref, wq, wk, wv, wg, oq, ok, ov, og):
    m = mod_ref[0, 0]
    h = _norm_mod(x_ref[0], g_ref[...], m[0:1], m[1:2]).astype(BF16)
    _rope(_dot(h, wq[...]), cos_ref, sin_ref, oq, 1.0)
    _rope(_dot(h, wk[...]), cos_ref, sin_ref, ok, RET_DK ** -0.5)
    ov[0] = _dot(h, wv[...]).astype(ov.dtype)
    og[0] = _dot(h, wg[...]).astype(og.dtype)


def _inproj1(x_all, mods, norm_g, cos_t, sin_t, weights, nct):
    bsz, t, d = x_all.shape
    x_spec, mod_spec = _tile_specs(nct, d)
    tab_spec = pl.BlockSpec((TOKEN_TILE, RET_DK), lambda b, i: (i, 0))
    return pl.pallas_call(
        _inproj1_body,
        grid=(bsz, t // TOKEN_TILE),
        in_specs=[x_spec, mod_spec, _const_spec((1, d)), tab_spec, tab_spec] + [_const_spec(w.shape) for w in weights],
        out_specs=[pl.BlockSpec((1, TOKEN_TILE, w.shape[1]), lambda b, i: (b, i, 0)) for w in weights],
        out_shape=[jax.ShapeDtypeStruct((bsz, t, w.shape[1]), BF16) for w in weights],
        compiler_params=_params("arbitrary", "arbitrary"),
        name="inproj_retention",
    )(x_all, mods, norm_g.reshape(1, d), cos_t, sin_t, *weights)


def _backward_chunk(n, n_ctx_chunks, n_chunks):
    return jnp.where(n < n_ctx_chunks, n_ctx_chunks - 1 - n, n_chunks - 1 - (n - n_ctx_chunks))


def _gla_body(q_f, k_f, v_f, low_f, q_b, k_b, v_b, low_b, wa_ref, ba_ref, hmask_ref, bdmask_ref,
              o_f, o_b, st_f, st_b):
    c = GLA_CHUNK

    @pl.when(pl.program_id(1) == 0)
    def _():
        st_f[...] = jnp.zeros_like(st_f)
        st_b[...] = jnp.zeros_like(st_b)

    ii = lax.broadcasted_iota(I32, (c, c), 0)
    jj = lax.broadcasted_iota(I32, (c, c), 1)
    r4 = lax.broadcasted_iota(I32, (GLA_HEADS * c, c), 0) & (c - 1)
    c4 = lax.broadcasted_iota(I32, (GLA_HEADS * c, c), 1)
    dirs = ((q_f, k_f, v_f, low_f, o_f, st_f), (q_b, k_b, v_b, low_b, o_b, st_b))
    for bb in range(q_f.shape[0]):
        for d, (q_ref, k_ref, v_ref, low_ref, o_ref, st_ref) in enumerate(dirs):
            seen = (jj <= ii) if d == 0 else (jj >= ii)
            seen4 = (c4 <= r4) if d == 0 else (c4 >= r4)
            z = _dot3(low_ref[bb], wa_ref[d]) + ba_ref[d]
            log_a = (jnp.minimum(z, 0.0) - jnp.log1p(jnp.exp(-jnp.abs(z)))) * (1.0 / GLA_TAU)
            tri = jnp.where(seen, 1.0, 0.0).astype(BF16)
            la_hi, la_lo = _split(log_a)
            cum = _dot(tri, la_hi) + _dot(tri, la_lo)
            tot = jnp.sum(log_a, axis=0, keepdims=True)
            k = k_ref[bb]
            q_dec = q_ref[bb] * (GLA_DK ** -0.5) * jnp.exp(cum)
            k_inv = (k * jnp.exp(-cum)).astype(BF16)
            k_state = (k * jnp.exp(tot - cum)).astype(BF16)
            q_heads = (jnp.concatenate([q_dec] * GLA_HEADS, axis=0) * hmask_ref[...]).astype(BF16)
            scores = jnp.where(seen4, _dot_nt(q_heads, k_inv), 0.0).astype(BF16)
            v = v_ref[bb]
            st = st_ref[bb]
            o_intra = jnp.concatenate(
                [_dot(scores[h * c:(h + 1) * c], v[:, h * GLA_DV:(h + 1) * GLA_DV]) for h in range(GLA_HEADS)],
                axis=1)
            o_ref[bb] = o_intra + _dot_nt(q_dec.astype(BF16), st.astype(BF16))
            st_ref[bb] = st * jnp.exp(tot) + bdmask_ref[...] * _dot_tn(v, k_state)


def _gla(q, k, v, low, wa_pad, ba, n_ctx):
    bsz, t, _ = q.shape
    nc, ncc = t // GLA_CHUNK, n_ctx // GLA_CHUNK
    gb = GLA_BATCH
    fwd = lambda b, n: (b, n, 0)
    bwd = lambda b, n: (b, _backward_chunk(n, ncc, nc), 0)
    hmask = (jnp.arange(AB_QK)[:, None] // GLA_CHUNK == jnp.arange(AB_QK)[None, :] // GLA_DK).astype(F32)
    bdmask = (jnp.arange(AB_V)[:, None] // GLA_DV == jnp.arange(AB_QK)[None, :] // GLA_DK).astype(F32)

    def specs(idx):
        return [pl.BlockSpec((gb, GLA_CHUNK, AB_QK), idx), pl.BlockSpec((gb, GLA_CHUNK, AB_QK), idx),
                pl.BlockSpec((gb, GLA_CHUNK, AB_V), idx), pl.BlockSpec((gb, GLA_CHUNK, 2 * GLA_RANK), idx)]

    return pl.pallas_call(
        _gla_body,
        grid=(bsz // gb, nc),
        in_specs=specs(fwd) + specs(bwd) + [_const_spec(wa_pad.shape), _const_spec(ba.shape),
                                            _const_spec(hmask.shape), _const_spec(bdmask.shape)],
        out_specs=[pl.BlockSpec((gb, GLA_CHUNK, AB_V), fwd), pl.BlockSpec((gb, GLA_CHUNK, AB_V), bwd)],
        out_shape=[jax.ShapeDtypeStruct((bsz, t, AB_V), F32)] * 2,
        scratch_shapes=[pltpu.VMEM((gb, AB_V, AB_QK), F32)] * 2,
        compiler_params=_params("arbitrary", "arbitrary"),
        name="gla_scan",
    )(q, k, v, low, q, k, v, low, wa_pad, ba, hmask, bdmask)


def _s5_operators(lam_re, lam_im, log_step, b_re, b_im, c_re, c_im):
    hp = lax.Precision.HIGHEST
    ln = S5_CHUNK
    step = jnp.exp(log_step.astype(F32))[..., None]
    lam_re, lam_im = lam_re.astype(F32), lam_im.astype(F32)
    mag = jnp.exp(lam_re * step)
    a_re, a_im = mag * jnp.cos(lam_im * step), mag * jnp.sin(lam_im * step)
    den = lam_re * lam_re + lam_im * lam_im
    f_re = ((a_re - 1.0) * lam_re + a_im * lam_im) / den
    f_im = (a_im * lam_re - (a_re - 1.0) * lam_im) / den
    bb_re = f_re[..., None] * b_re - f_im[..., None] * b_im
    bb_im = f_re[..., None] * b_im + f_im[..., None] * b_re
    pw_re, pw_im = [jnp.ones_like(a_re)], [jnp.zeros_like(a_im)]
    for _ in range(ln):
        pr, pi = pw_re[-1], pw_im[-1]
        pw_re.append(pr * a_re - pi * a_im)
        pw_im.append(pr * a_im + pi * a_re)
    pw_re, pw_im = jnp.stack(pw_re, 1), jnp.stack(pw_im, 1)
    ca_re = c_re[:, None] * pw_re[:, :, :, None, :] - c_im[:, None] * pw_im[:, :, :, None, :]
    ca_im = c_re[:, None] * pw_im[:, :, :, None, :] + c_im[:, None] * pw_re[:, :, :, None, :]
    kern = (jnp.einsum('dtgcp,dgpe->dtgec', ca_re[:, :ln], bb_re, precision=hp)
            - jnp.einsum('dtgcp,dgpe->dtgec', ca_im[:, :ln], bb_im, precision=hp))
    pos = jnp.arange(ln)
    i_out, j_in = pos[None, :], pos[:, None]
    tz, wx, wy, ac = [], [], [], []
    for d in range(2):
        lag = (i_out - j_in) if d == 0 else (j_in - i_out)
        blk = jnp.where((lag >= 0)[:, :, None, None, None], kern[d][jnp.clip(lag, 0, ln - 1)], 0.0)
        tz.append(blk.transpose(2, 0, 3, 1, 4).reshape(S5_GROUPS, ln * S5_GROUP, ln * S5_GROUP))
        p_in = (ln - 1 - pos) if d == 0 else pos
        ar, ai = pw_re[d][p_in], pw_im[d][p_in]
        x_re = ar[..., None] * bb_re[d][None] - ai[..., None] * bb_im[d][None]
        x_im = ar[..., None] * bb_im[d][None] + ai[..., None] * bb_re[d][None]
        to_rows = lambda m: m.transpose(1, 0, 3, 2).reshape(S5_GROUPS, ln * S5_GROUP, S5_P)
        wx.append(jnp.concatenate([to_rows(x_re), to_rows(x_im), to_rows(x_im), to_rows(x_re)], axis=-1))
        p_out = (pos + 1) if d == 0 else (ln - pos)
        to_cols = lambda m: m.transpose(1, 3, 0, 2).reshape(S5_GROUPS, S5_P, ln * S5_GROUP)
        wy.append(jnp.concatenate([to_cols(ca_re[d][p_out]), -to_cols(ca_im[d][p_out])], axis=1))
        lr, li = pw_re[d][ln], pw_im[d][ln]
        rows = [jnp.concatenate([lr, lr], -1), jnp.concatenate([-li, li], -1), jnp.concatenate([li, -li], -1)]
        ac.append(jnp.stack(rows + [jnp.zeros_like(rows[0])] * 5, axis=1))
    return (jnp.stack(tz).astype(BF16), jnp.stack(wx).astype(BF16), jnp.stack(wy).astype(BF16), jnp.stack(ac))


def _s5_body(ncs_ctx, ncs, rows, u_ref, tz_ref, wx_ref, wy_ref, ac_ref, y_ref, xx_f, xx_b, sin_f, sin_b):
    u = u_ref[0]
    xx_f[...] = _dot(u, wx_ref[0, 0])
    xx_b[...] = _dot(u, wx_ref[1, 0])
    ac_f, ac_b = ac_ref[0, 0], ac_ref[1, 0]
    half = 2 * S5_P

    def advance(ac, s, s_sw, xx):
        return (ac[0:1] * s + ac[1:2] * s_sw + xx[:, :half], ac[0:1] * s_sw + ac[2:3] * s + xx[:, half:])

    def step(n, carry):
        s_f, sw_f, s_b, sw_b = carry
        r_f = pl.multiple_of(n * rows, rows)
        r_b = pl.multiple_of(_backward_chunk(n, ncs_ctx, ncs) * rows, rows)
        sin_f[pl.ds(r_f, rows), :] = s_f
        sin_b[pl.ds(r_b, rows), :] = s_b
        s_f, sw_f = advance(ac_f, s_f, sw_f, xx_f[pl.ds(r_f, rows), :])
        s_b, sw_b = advance(ac_b, s_b, sw_b, xx_b[pl.ds(r_b, rows), :])
        return s_f, sw_f, s_b, sw_b

    zero = jnp.zeros((rows, half), F32)
    lax.fori_loop(0, ncs, step, (zero, zero, zero, zero))
    y_ref[0] = (_dot(u, tz_ref[0, 0]) + _dot(u, tz_ref[1, 0])
                + _dot(sin_f[...].astype(BF16), wy_ref[0, 0]) + _dot(sin_b[...].astype(BF16), wy_ref[1, 0]))


def _s5(u, ops, n_ctx):
    tz, wx, wy, ac = ops
    bsz, t, _ = u.shape
    ln, lanes = S5_CHUNK, S5_CHUNK * S5_GROUP
    ncs, ncs_ctx = t // ln, n_ctx // ln
    m = ncs * bsz
    ug = u.astype(BF16).reshape(bsz, ncs, ln, S5_GROUPS, S5_GROUP).transpose(3, 1, 0, 2, 4).reshape(S5_GROUPS, m, lanes)
    dir_spec = lambda shape: pl.BlockSpec((2, 1) + shape, lambda g: (0, g, 0, 0))
    y = pl.pallas_call(
        functools.partial(_s5_body, ncs_ctx, ncs, bsz),
        grid=(S5_GROUPS,),
        in_specs=[pl.BlockSpec((1, m, lanes), lambda g: (g, 0, 0)), dir_spec((lanes, lanes)),
                  dir_spec((lanes, 4 * S5_P)), dir_spec((2 * S5_P, lanes)), dir_spec((8, 2 * S5_P))],
        out_specs=pl.BlockSpec((1, m, lanes), lambda g: (g, 0, 0)),
        out_shape=jax.ShapeDtypeStruct((S5_GROUPS, m, lanes), F32),
        scratch_shapes=[pltpu.VMEM((m, 4 * S5_P), F32)] * 2 + [pltpu.VMEM((m, 2 * S5_P), F32)] * 2,
        compiler_params=_params("arbitrary"),
        name="s5_scan",
    )(ug, tz, wx, wy, ac)
    return y.reshape(S5_GROUPS, ncs, bsz, ln, S5_GROUP).transpose(2, 1, 3, 0, 4).reshape(bsz, t, S5_CH)


def _ret_body(ncc, q_f, k_f, v_f, q_b, k_b, v_b, dmat_ref, rsc_ref, csc_ref, gam_ref, o_f, o_b, st_f, st_b):
    n = pl.program_id(1)

    @pl.when(n == 0)
    def _():
        st_f[...] = jnp.zeros_like(st_f)
        st_b[...] = jnp.zeros_like(st_b)

    dirs = ((q_f, k_f, v_f, o_f, st_f), (q_b, k_b, v_b, o_b, st_b))

    @pl.when(n >= ncc)
    def _():
        for d, (q_ref, k_ref, v_ref, o_ref, st_ref) in enumerate(dirs):
            for h in range(RET_HEADS):
                qh = q_ref[0, :, h * RET_DK:(h + 1) * RET_DK]
                kh = k_ref[0, :, h * RET_DK:(h + 1) * RET_DK]
                vh = v_ref[0, :, h * RET_DV:(h + 1) * RET_DV]
                scores = (_dot_nt(qh, kh) * dmat_ref[d, h]).astype(BF16)
                o = _dot(scores, vh) + rsc_ref[d, h] * _dot(qh, st_ref[h].astype(BF16))
                o_ref[0, :, h * RET_DV:(h + 1) * RET_DV] = o.astype(o_ref.dtype)

    for d, (q_ref, k_ref, v_ref, o_ref, st_ref) in enumerate(dirs):
        for h in range(RET_HEADS):
            kh = k_ref[0, :, h * RET_DK:(h + 1) * RET_DK]
            vh = v_ref[0, :, h * RET_DV:(h + 1) * RET_DV]
            k_state = (kh.astype(F32) * csc_ref[d, h]).astype(BF16)
            st_ref[h] = st_ref[h] * gam_ref[d, h] + _dot_tn(k_state, vh)


def _retention(q, k, v, decay_logit, n_ctx):
    bsz, t, _ = q.shape
    c = RET_CHUNK
    nc, ncc = t // c, n_ctx // c
    nl = nc - ncc
    log_gamma = jax.nn.log_sigmoid(decay_logit.astype(F32))[:, :, None, None]
    i = jnp.arange(c, dtype=F32)
    lag = i[:, None] - i[None, :]
    lag = jnp.stack([lag, -lag])[:, None]
    dmat = jnp.where(lag >= 0, jnp.exp(log_gamma * jnp.maximum(lag, 0.0)), 0.0)
    done = jnp.stack([i + 1.0, c - i])[:, None, :, None]
    rsc = jnp.exp(log_gamma * done)
    csc = jnp.exp(log_gamma * (c - done))
    gam = jnp.exp(log_gamma[:, :, 0, 0] * c)
    fwd = lambda b, n: (b, n, 0)
    bwd = lambda b, n: (b, _backward_chunk(n, ncc, nc), 0)
    o_fwd = lambda b, n: (b, jnp.maximum(n - ncc, 0), 0)
    o_bwd = lambda b, n: (b, nl - 1 - jnp.maximum(n - ncc, 0), 0)

    def specs(idx):
        return [pl.BlockSpec((1, c, RET_QK), idx), pl.BlockSpec((1, c, RET_QK), idx), pl.BlockSpec((1, c, RET_MIX), idx)]

    return pl.pallas_call(
        functools.partial(_ret_body, ncc),
        grid=(bsz, nc),
        in_specs=specs(fwd) + specs(bwd) + [_const_spec(dmat.shape), _const_spec(rsc.shape), _const_spec(csc.shape),
                                            pl.BlockSpec(memory_space=pltpu.SMEM)],
        out_specs=[pl.BlockSpec((1, c, RET_MIX), o_fwd), pl.BlockSpec((1, c, RET_MIX), o_bwd)],
        out_shape=[jax.ShapeDtypeStruct((bsz, nl * c, RET_MIX), BF16)] * 2,
        scratch_shapes=[pltpu.VMEM((RET_HEADS, RET_DK, RET_DV), F32)] * 2,
        compiler_params=_params("arbitrary", "arbitrary"),
        name="retention_scan",
    )(q, k, v, q, k, v, dmat, rsc, csc, gam)


def _route(x, mixed, mod, n2g_ref, wr_ref, br_ref, x1_ref, h2_ref, e_ref, w_ref, r_ref, cnt_ref):
    tm = x.shape[0]
    x1 = x + mod[2:3] * mixed
    x1_ref[0] = x1
    h2 = _norm_mod(x1, n2g_ref[...], mod[3:4], mod[4:5])
    h2_ref[0] = _pack_rows(h2)
    logits = _dot3(wr_ref[...], h2, dot=_dot_nt) + br_ref[...]
    ie = lax.broadcasted_iota(I32, logits.shape, 0)
    tops, picks = [], []
    for _ in range(TOP_K):
        mx = jnp.max(logits, axis=0, keepdims=True)
        pick = jnp.min(jnp.where(logits == mx, ie, N_EXPERTS), axis=0, keepdims=True)
        tops.append(mx)
        picks.append(pick)
        logits = jnp.where(ie == pick, -jnp.inf, logits)
    ex = [jnp.exp(tk - tops[0]) for tk in tops]
    den = ex[0] + ex[1] + ex[2] + ex[3]
    for kk in range(TOP_K):
        w_ref[0, kk:kk + 1, :] = ex[kk] / den
        e_ref[0, kk:kk + 1, :] = picks[kk]

    @pl.when((pl.program_id(0) == 0) & (pl.program_id(1) == 0))
    def _():
        cnt_ref[...] = jnp.zeros_like(cnt_ref)

    earlier = (lax.broadcasted_iota(I32, (tm, tm), 0) < lax.broadcasted_iota(I32, (tm, tm), 1))
    earlier = jnp.where(earlier, 1.0, 0.0).astype(BF16)
    run = cnt_ref[:, 0:1]
    for kk, pick in enumerate(picks):
        onehot = jnp.where(ie == pick, 1.0, 0.0)
        before = _dot(onehot.astype(BF16), earlier) + run
        r_ref[0, kk:kk + 1, :] = jnp.sum(onehot * before, axis=0, keepdims=True).astype(I32)
        run = run + jnp.sum(onehot, axis=1, keepdims=True)
    cnt_ref[...] = jnp.broadcast_to(run, cnt_ref.shape)


def _mix0_body(x_ref, mod_ref, of_ref, ob_ref, g_ref, ys_ref, u_ref, gng_ref, dsk_ref, gluw_ref, glub_ref,
               wo_ref, n2g_ref, wr_ref, br_ref, x1_ref, h2_ref, e_ref, w_ref, r_ref, cnt_ref):
    o = of_ref[0] + ob_ref[0]
    heads = []
    for h in range(GLA_HEADS):
        oh = o[:, h * GLA_DV:(h + 1) * GLA_DV]
        heads.append(oh * lax.rsqrt(jnp.mean(oh * oh, axis=-1, keepdims=True) + EPS))
    gla = jnp.concatenate(heads, axis=1) * gng_ref[...] * _silu(g_ref[0].astype(F32))
    y = jax.nn.gelu(ys_ref[0] + dsk_ref[...] * u_ref[0].astype(F32))
    y = y * jax.nn.sigmoid(_dot(y.astype(BF16), gluw_ref[...]) + glub_ref[...])
    mixed = _dot(gla.astype(BF16), wo_ref[0:AB_V]) + _dot(y.astype(BF16), wo_ref[AB_V:AB_V + S5_CH])
    _route(x_ref[0], mixed, mod_ref[0, 0], n2g_ref, wr_ref, br_ref, x1_ref, h2_ref, e_ref, w_ref, r_ref, cnt_ref)


def _mix1_body(x_ref, mod_ref, of_ref, ob_ref, g_ref, ng_ref, wo_ref, n2g_ref, wr_ref, br_ref,
               x1_ref, h2_ref, e_ref, w_ref, r_ref, cnt_ref):
    mixed = None
    for h in range(RET_HEADS):
        sl = slice(h * RET_DV, (h + 1) * RET_DV)
        oh = of_ref[0, :, sl].astype(F32) + ob_ref[0, :, sl].astype(F32)
        mu = jnp.mean(oh, axis=-1, keepdims=True)
        cen = oh - mu
        var = jnp.mean(cen * cen, axis=-1, keepdims=True)
        gated = cen * lax.rsqrt(var + EPS) * ng_ref[:, sl] * _silu(g_ref[0, :, sl].astype(F32))
        part = _dot(gated.astype(BF16), wo_ref[sl])
        mixed = part if mixed is None else mixed + part
    _route(x_ref[0], mixed, mod_ref[0, 0], n2g_ref, wr_ref, br_ref, x1_ref, h2_ref, e_ref, w_ref, r_ref, cnt_ref)


def _mix_call(body, name, x_all, mods, tiles, acts, consts, norm2_g, w_router, b_router, n_tok, seg_tile0):
    bsz, _, d = x_all.shape
    tm = TOKEN_TILE
    off = lambda b, i: (b, i + seg_tile0, 0)
    loc = lambda b, i: (b, i, 0)
    ntl = bsz * tiles
    flat = lambda b, i: (b * tiles + i, 0, 0)
    in_specs = [pl.BlockSpec((1, tm, d), off),
                pl.BlockSpec((1, 1, 6, d), lambda b, i: (b, ((i + seg_tile0) >= n_tok).astype(I32), 0, 0))]
    args = [x_all, mods]
    for arr, offset in acts:
        in_specs.append(pl.BlockSpec((1, tm, arr.shape[2]), off if offset else loc))
        args.append(arr)
    tail = list(consts) + [norm2_g.reshape(1, d), w_router.T, b_router.reshape(N_EXPERTS, 1)]
    in_specs += [_const_spec(a.shape) for a in tail]
    args += tail
    tok_out = pl.BlockSpec((1, TOP_K, tm), flat)
    return pl.pallas_call(
        body,
        grid=(bsz, tiles),
        in_specs=in_specs,
        out_specs=[pl.BlockSpec((1, tm, d), loc), pl.BlockSpec((1, tm, d // 2), loc), tok_out, tok_out, tok_out,
                   _const_spec((N_EXPERTS, 128))],
        out_shape=[jax.ShapeDtypeStruct((bsz, tiles * tm, d), F32), jax.ShapeDtypeStruct((bsz, tiles * tm, d // 2), U32),
                   jax.ShapeDtypeStruct((ntl, TOP_K, tm), I32), jax.ShapeDtypeStruct((ntl, TOP_K, tm), F32),
                   jax.ShapeDtypeStruct((ntl, TOP_K, tm), I32), jax.ShapeDtypeStruct((N_EXPERTS, 128), F32)],
        compiler_params=_params("arbitrary", "arbitrary"),
        name=name,
    )(*args)


def _cast_rows(src_ref, dst_ref, rows):
    def chunk(j, carry):
        r = pl.multiple_of(j * rows, rows)
        dst_ref[pl.ds(r, rows), :] = src_ref[0, 0, pl.ds(r, rows), :].astype(BF16)
        return carry

    lax.fori_loop(0, dst_ref.shape[0] // rows, chunk, 0)


def _expert_body(be_ref, nu_ref, x_ref, wgu_ref, bgu_ref, wd_ref, bd_ref, o_ref, wgu_bf, wd_bf):
    i = pl.program_id(0)
    live = i < nu_ref[0]
    new_expert = (i == 0) | (be_ref[i] != be_ref[jnp.maximum(i - 1, 0)])

    @pl.when(live & new_expert)
    def _():
        _cast_rows(wgu_ref, wgu_bf, 128)
        _cast_rows(wd_ref, wd_bf, 128)

    @pl.when(live)
    def _():
        x_lo, x_hi = _unpack_rows(x_ref[...])
        half = x_lo.shape[1]
        gu = (_dot(x_lo.astype(BF16), wgu_bf[0:half]) + _dot(x_hi.astype(BF16), wgu_bf[half:2 * half])
              + bgu_ref[0, 0])
        gate = jnp.minimum(gu[:, :D_FF], SWIGLU_LIMIT)
        lin = jnp.clip(gu[:, D_FF:], -SWIGLU_LIMIT, SWIGLU_LIMIT)
        act = gate * jax.nn.sigmoid(SWIGLU_ALPHA * gate) * (lin + 1.0)
        y = _dot(act.astype(BF16), wd_bf[...]) + bd_ref[0, 0]
        o_ref[...] = _pack_rows(y)

    @pl.when(i >= nu_ref[0])
    def _():
        o_ref[...] = jnp.zeros_like(o_ref)


def _experts(xb, block_e, n_used, layer, w_gu, b_gu, w_down, b_down):
    n_slots, half = xb.shape
    d = 2 * half
    n_blocks = n_slots // MOE_BLOCK
    depth = w_gu.shape[0]
    by_expert = lambda i, be, nu: (layer, be[i], 0, 0)
    return pl.pallas_call(
        _expert_body,
        grid_spec=pltpu.PrefetchScalarGridSpec(
            num_scalar_prefetch=2,
            grid=(n_blocks,),
            in_specs=[pl.BlockSpec((MOE_BLOCK, half), lambda i, be, nu: (i, 0)),
                      pl.BlockSpec((1, 1, d, 2 * D_FF), by_expert), pl.BlockSpec((1, 1, 1, 2 * D_FF), by_expert),
                      pl.BlockSpec((1, 1, D_FF, d), by_expert), pl.BlockSpec((1, 1, 1, d), by_expert)],
            out_specs=pl.BlockSpec((MOE_BLOCK, half), lambda i, be, nu: (i, 0)),
            scratch_shapes=[pltpu.VMEM((d, 2 * D_FF), BF16), pltpu.VMEM((D_FF, d), BF16)]),
        out_shape=jax.ShapeDtypeStruct((n_slots, half), U32),
        compiler_params=_params("arbitrary"),
        name="moe_experts",
    )(block_e, n_used, xb, w_gu, b_gu.reshape(depth, N_EXPERTS, 1, 2 * D_FF), w_down,
      b_down.reshape(depth, N_EXPERTS, 1, d))


def _combine_body(final, x1_ref, mod_ref, yk_ref, w_ref, *refs):
    d = x1_ref.shape[2]
    half = d // 2
    y_lo, y_hi = None, None
    for k in range(TOP_K):
        lo, hi = _unpack_rows(yk_ref[k, 0])
        wk = w_ref[0, :, k:k + 1]
        y_lo = lo * wk if y_lo is None else y_lo + lo * wk
        y_hi = hi * wk if y_hi is None else y_hi + hi * wk
    g2 = mod_ref[0, 0][5:6]
    x2_lo = x1_ref[0, :, 0:half] + g2[:, 0:half] * y_lo
    x2_hi = x1_ref[0, :, half:d] + g2[:, half:d] * y_hi
    if final:
        fg_ref, o_ref = refs
        ms = (jnp.sum(x2_lo * x2_lo, axis=-1, keepdims=True) + jnp.sum(x2_hi * x2_hi, axis=-1, keepdims=True)) / d
        r = lax.rsqrt(ms + EPS)
        x2_lo = x2_lo * r * fg_ref[:, 0:half]
        x2_hi = x2_hi * r * fg_ref[:, half:d]
    else:
        (o_ref,) = refs
    o_ref[0, :, 0:half] = x2_lo
    o_ref[0, :, half:d] = x2_hi


def _combine(x1, mods, yk, w_tok, seg_tile0, n_tok, final_g):
    bsz, t, d = x1.shape
    tm = TOKEN_TILE
    tiles = t // tm
    loc = lambda b, i: (b, i, 0)
    in_specs = [pl.BlockSpec((1, tm, d), loc),
                pl.BlockSpec((1, 1, 6, d), lambda b, i: (b, ((i + seg_tile0) >= n_tok).astype(I32), 0, 0)),
                pl.BlockSpec((TOP_K, 1, tm, d // 2), lambda b, i: (0, b, i, 0)),
                pl.BlockSpec((1, tm, TOP_K), loc)]
    args = [x1, mods, yk.reshape(TOP_K, bsz, t, d // 2), w_tok.reshape(bsz, t, TOP_K)]
    if final_g is not None:
        in_specs.append(_const_spec((1, d)))
        args.append(final_g.reshape(1, d))
    return pl.pallas_call(
        functools.partial(_combine_body, final_g is not None),
        grid=(bsz, tiles),
        in_specs=in_specs,
        out_specs=pl.BlockSpec((1, tm, d), loc),
        out_shape=jax.ShapeDtypeStruct((bsz, t, d), F32),
        compiler_params=_params("arbitrary", "arbitrary"),
        name="moe_combine",
    )(*args)


def _sc_mesh():
    return plsc.VectorSubcoreMesh(core_axis_name="core", subcore_axis_name="subcore",
                                  num_cores=SC_CORES, num_subcores=SC_SUBCORES)


def _sc_worker_base(per_worker):
    return (lax.axis_index("subcore") * SC_CORES + lax.axis_index("core")) * per_worker


def _sc_dispatch(rows, dest, n_slots):
    n, w = rows.shape
    per_worker = n // SC_WORKERS
    assert per_worker * SC_WORKERS == n and per_worker % SC_CHUNK == 0

    @functools.partial(
        pl.kernel, mesh=_sc_mesh(), out_type=jax.ShapeDtypeStruct((n_slots, w), rows.dtype),
        scratch_types=[pltpu.VMEM((SC_CHUNK,), I32)] * TOP_K + [pltpu.VMEM((SC_CHUNK, w), rows.dtype),
                                                                pltpu.SemaphoreType.DMA],
        name="moe_dispatch")
    def scatter_rows(rows_hbm, dest_hbm, out_hbm, *scratch):
        idx_refs, buf, sem = scratch[:TOP_K], scratch[TOP_K], scratch[TOP_K + 1]
        base0 = _sc_worker_base(per_worker)

        @pl.loop(0, per_worker // SC_CHUNK)
        def _(j):
            base = base0 + j * SC_CHUNK
            pltpu.sync_copy(rows_hbm.at[pl.ds(base, SC_CHUNK)], buf)
            for k, idx in enumerate(idx_refs):
                pltpu.sync_copy(dest_hbm.at[pl.ds(k * n + base, SC_CHUNK)], idx)
            copies = [pltpu.async_copy(buf, out_hbm.at[idx], sem) for idx in idx_refs]
            for cp in copies:
                cp.wait()

    return scatter_rows(rows, dest)


def _sc_gather(table, idx):
    n = idx.shape[0]
    w = table.shape[1]
    per_worker = n // SC_WORKERS
    n_chunks = per_worker // SC_CHUNK
    assert per_worker * SC_WORKERS == n and n_chunks * SC_CHUNK == per_worker and n_chunks % 2 == 0

    @functools.partial(
        pl.kernel, mesh=_sc_mesh(), out_type=jax.ShapeDtypeStruct((n, w), table.dtype),
        scratch_types=([pltpu.VMEM((SC_CHUNK,), I32)] * 2 + [pltpu.VMEM((SC_CHUNK, w), table.dtype)] * 2
                       + [pltpu.SemaphoreType.DMA] * 4),
        name="moe_gather")
    def gather_rows(table_hbm, idx_hbm, out_hbm, idx0, idx1, buf0, buf1, gsem0, gsem1, wsem0, wsem1):
        base0 = _sc_worker_base(per_worker)

        def gather_copy(idx_v, buf, sem):
            return pltpu.make_async_copy(table_hbm.at[idx_v], buf, sem)

        def write_copy(j, buf, sem):
            return pltpu.make_async_copy(buf, out_hbm.at[pl.ds(base0 + j * SC_CHUNK, SC_CHUNK)], sem)

        def start_gather(j, idx_v, buf, sem):
            pltpu.sync_copy(idx_hbm.at[pl.ds(base0 + j * SC_CHUNK, SC_CHUNK)], idx_v)
            gather_copy(idx_v, buf, sem).start()

        start_gather(0, idx0, buf0, gsem0)

        @pl.loop(0, n_chunks, step=2)
        def _(j):
            @pl.when(j > 0)
            def _():
                write_copy(j - 1, buf1, wsem1).wait()
            start_gather(j + 1, idx1, buf1, gsem1)
            gather_copy(idx0, buf0, gsem0).wait()
            write_copy(j, buf0, wsem0).start()

            @pl.when(j + 2 < n_chunks)
            def _():
                write_copy(j, buf0, wsem0).wait()
                start_gather(j + 2, idx0, buf0, gsem0)
            gather_copy(idx1, buf1, gsem1).wait()
            write_copy(j + 1, buf1, wsem1).start()

        write_copy(n_chunks - 2, buf0, wsem0).wait()
        write_copy(n_chunks - 1, buf1, wsem1).wait()

    return gather_rows(table, idx)


def _moe(h2, e_tl, w_tl, r_tl, cnt, layer, w_gu, b_gu, w_down, b_down):
    bsz, t, half = h2.shape
    n = bsz * t
    flat = lambda a: a.transpose(1, 0, 2).reshape(TOP_K, n)
    e_k, w_k, r_k = flat(e_tl), flat(w_tl), flat(r_tl)
    counts = cnt[:, 0].astype(I32)
    padded = (counts + MOE_BLOCK - 1) // MOE_BLOCK * MOE_BLOCK
    pad_end = jnp.cumsum(padded)
    pad_start = pad_end - padded
    n_blocks = (n * TOP_K + MOE_BLOCK - 1) // MOE_BLOCK + N_EXPERTS
    block_start = jnp.arange(n_blocks, dtype=I32) * MOE_BLOCK
    block_e = jnp.minimum(jnp.sum((pad_end[None, :] <= block_start[:, None]).astype(I32), axis=1), N_EXPERTS - 1)
    n_used = (pad_end[-1:] // MOE_BLOCK).astype(I32)
    start_k = jnp.sum(jnp.where(e_k[..., None] == jnp.arange(N_EXPERTS, dtype=I32), pad_start, 0), axis=-1)
    dest = (start_k + r_k).reshape(TOP_K * n)
    xb = _sc_dispatch(h2.reshape(n, half), dest, n_blocks * MOE_BLOCK)
    yb = _experts(xb, block_e, n_used, layer, w_gu, b_gu, w_down, b_down)
    return _sc_gather(yb, dest), w_k.T


def _rope_tables(n_ctx, n_lat):
    n_freq = RET_DK // 4
    inv_freq = ROPE_BASE ** (-jnp.arange(n_freq, dtype=F32) / n_freq)
    pos = jnp.arange(n_lat, dtype=I32)
    cos, sin = [], []
    for p in (pos // GRID_W, pos % GRID_W):
        ang = p.astype(F32)[:, None] * inv_freq
        cos += [jnp.cos(ang), jnp.cos(ang)]
        sin += [-jnp.sin(ang), jnp.sin(ang)]
    cos, sin = jnp.concatenate(cos, axis=1), jnp.concatenate(sin, axis=1)
    return (jnp.concatenate([jnp.ones((n_ctx, RET_DK), F32), cos], axis=0),
            jnp.concatenate([jnp.zeros((n_ctx, RET_DK), F32), sin], axis=0))


def kernel(x, c, ctx, c_ctx, ada_w, ada_b, norm1_g, norm2_g, ab_w_in, ab_w_out, gla_wa, gla_ba, gla_norm_g, s5_lam_re, s5_lam_im, s5_log_step, s5_b_re, s5_b_im, s5_c_re, s5_c_im, s5_d, s5_glu_w, s5_glu_b, ret_w_in, ret_w_out, ret_decay_logit, ret_norm_g, moe_w_router, moe_b_router, moe_w_gu, moe_b_gu, moe_w_down, moe_b_down, final_norm_g):
    bsz, n_lat, d = x.shape
    n_ctx = ctx.shape[1]
    depth = ada_w.shape[0]
    assert depth == 2 and d == D_MODEL and bsz == 8, "kernels are laid out for the stated problem shape"
    assert n_ctx % TOKEN_TILE == 0 and n_lat % TOKEN_TILE == 0 and n_lat % GRID_W == 0
    t = n_ctx + n_lat
    nct = n_ctx // TOKEN_TILE

    cvec = jnp.zeros((16, d), F32).at[:bsz].set(c).at[bsz].set(c_ctx)
    mod = _ada_mod(cvec, ada_w, ada_b).reshape(depth, 16, 6, d)
    mods = [jnp.stack([jnp.broadcast_to(mod[l, bsz], (bsz, 6, d)), mod[l, :bsz]], axis=1) for l in range(depth)]

    x_all = jnp.concatenate([ctx, x], axis=1)

    w_in = ab_w_in[0].astype(BF16)
    cuts = [0, AB_QK, 2 * AB_QK, 2 * AB_QK + AB_V, 2 * AB_QK + 2 * AB_V, 2 * AB_QK + 2 * AB_V + 2 * GLA_RANK,
            w_in.shape[1]]
    pieces = [w_in[:, a:b] for a, b in zip(cuts[:-1], cuts[1:])]
    q, k, v, g, low, u = _inproj0(x_all, mods[0], norm1_g[0], pieces, [F32, F32, BF16, BF16, F32, F32], nct)
    wa_pad = jnp.zeros((2, 2 * GLA_RANK, AB_QK), F32)
    wa_pad = wa_pad.at[0, :GLA_RANK].set(gla_wa[0, 0]).at[1, GLA_RANK:].set(gla_wa[0, 1])
    o_f, o_b = _gla(q, k, v, low, wa_pad, gla_ba[0].reshape(2, 1, AB_QK), n_ctx)
    ops = _s5_operators(s5_lam_re[0], s5_lam_im[0], s5_log_step[0], s5_b_re[0], s5_b_im[0], s5_c_re[0], s5_c_im[0])
    ys = _s5(u, ops, n_ctx)
    consts = [jnp.tile(gla_norm_g[0], GLA_HEADS).reshape(1, AB_V), s5_d[0].reshape(1, S5_CH),
              s5_glu_w[0].astype(BF16), s5_glu_b[0].reshape(1, S5_CH), ab_w_out[0].astype(BF16)]
    x1, h2, e_tl, w_tl, r_tl, cnt = _mix_call(
        _mix0_body, "mix_gla_s5", x_all, mods[0], t // TOKEN_TILE,
        [(o_f, False), (o_b, False), (g, False), (ys, False), (u, False)], consts,
        norm2_g[0], moe_w_router[0], moe_b_router[0], nct, 0)
    yk, w_tok = _moe(h2, e_tl, w_tl, r_tl, cnt, 0, moe_w_gu, moe_b_gu, moe_w_down, moe_b_down)
    x_all = _combine(x1, mods[0], yk, w_tok, 0, nct, None)

    w_in = ret_w_in[0].astype(BF16)
    cuts = [0, RET_QK, 2 * RET_QK, 2 * RET_QK + RET_MIX, w_in.shape[1]]
    pieces = [w_in[:, a:b] for a, b in zip(cuts[:-1], cuts[1:])]
    cos_t, sin_t = _rope_tables(n_ctx, n_lat)
    q, k, v, g = _inproj1(x_all, mods[1], norm1_g[1], cos_t, sin_t, pieces, nct)
    o_f, o_b = _retention(q, k, v, ret_decay_logit[0], n_ctx)
    consts = [ret_norm_g[0].reshape(1, RET_MIX), ret_w_out[0].astype(BF16)]
    x1, h2, e_tl, w_tl, r_tl, cnt = _mix_call(
        _mix1_body, "mix_retention", x_all, mods[1], n_lat // TOKEN_TILE,
        [(o_f, False), (o_b, False), (g, True)], consts,
        norm2_g[1], moe_w_router[1], moe_b_router[1], nct, nct)
    yk, w_tok = _moe(h2, e_tl, w_tl, r_tl, cnt, 1, moe_w_gu, moe_b_gu, moe_w_down, moe_b_down)
    return _combine(x1, mods[1], yk, w_tok, nct, nct, final_norm_g)
```

```python
import functools
import math

import jax
import jax.numpy as jnp
from jax import lax
from jax.experimental import pallas as pl
from jax.experimental.pallas import tpu as pltpu
from jax.experimental.pallas import tpu_sc as plsc

F32, BF16, I32, U32 = jnp.float32, jnp.bfloat16, jnp.int32, jnp.uint32

D_MODEL = 1024
GRID_W = 64
EPS = 1e-6
GLA_HEADS, GLA_DK, GLA_DV, GLA_RANK, GLA_TAU, GLA_CHUNK = 4, 64, 128, 16, 16.0, 64
GLA_BATCH = 4
AB_QK, AB_V = GLA_HEADS * GLA_DK, GLA_HEADS * GLA_DV
S5_CH, S5_GROUP, S5_GROUPS, S5_P = 512, 16, 32, 64
S5_CHUNK = 16
RET_HEADS, RET_DK, RET_DV = 4, 256, 512
RET_CHUNK = 256
RET_QK, RET_MIX = RET_HEADS * RET_DK, RET_HEADS * RET_DV
ROPE_BASE = 10000.0
N_EXPERTS, TOP_K, D_FF = 32, 4, 1024
SWIGLU_LIMIT, SWIGLU_ALPHA = 7.0, 1.702
MOE_BLOCK = 512
TOKEN_TILE = 256
ADA_TILE = 768
VMEM_LIMIT = 56 * 1024 * 1024
SC_CORES, SC_SUBCORES = 2, 16
SC_WORKERS = SC_CORES * SC_SUBCORES
SC_CHUNK = 64


def _params(*sem):
    return pltpu.CompilerParams(dimension_semantics=sem, vmem_limit_bytes=VMEM_LIMIT)


def _dot(a, b):
    return jnp.dot(a, b, preferred_element_type=F32)


def _dot_nt(a, b):
    return lax.dot_general(a, b, (((1,), (1,)), ((), ())), preferred_element_type=F32)


def _dot_tn(a, b):
    return lax.dot_general(a, b, (((0,), (0,)), ((), ())), preferred_element_type=F32)


def _split(a):
    hi = a.astype(BF16)
    return hi, (a - hi.astype(F32)).astype(BF16)


def _dot3(a, b, dot=_dot):
    ah, al = _split(a)
    bh, bl = _split(b)
    return dot(ah, bh) + (dot(ah, bl) + dot(al, bh))


def _pack_rows(x):
    h = x.shape[1] // 2
    lo = lax.bitcast_convert_type(x[:, 0:h].astype(BF16).astype(F32), U32)
    hi = lax.bitcast_convert_type(x[:, h:2 * h].astype(BF16).astype(F32), U32)
    return hi | (lo >> 16)


def _unpack_rows(p):
    lo = lax.bitcast_convert_type(p << 16, F32)
    hi = lax.bitcast_convert_type(p & jnp.uint32(0xFFFF0000), F32)
    return lo, hi


def _silu(x):
    return x * jax.nn.sigmoid(x)


def _norm_mod(x, g, shift, scale):
    r = lax.rsqrt(jnp.mean(x * x, axis=-1, keepdims=True) + EPS)
    return (x * r * g) * (1.0 + scale) + shift


def _const_spec(shape):
    nd = len(shape)
    return pl.BlockSpec(shape, lambda *_: (0,) * nd)


def _ada_body(c_ref, w_ref, b_ref, o_ref):
    o_ref[0] = _dot3(_silu(c_ref[...]), w_ref[0]) + b_ref[0]


def _ada_mod(cvec, ada_w, ada_b):
    depth, d, n6 = ada_w.shape
    rows = cvec.shape[0]
    return pl.pallas_call(
        _ada_body,
        grid=(depth, n6 // ADA_TILE),
        in_specs=[_const_spec((rows, d)),
                  pl.BlockSpec((1, d, ADA_TILE), lambda l, j: (l, 0, j)),
                  pl.BlockSpec((1, 1, ADA_TILE), lambda l, j: (l, 0, j))],
        out_specs=pl.BlockSpec((1, rows, ADA_TILE), lambda l, j: (l, 0, j)),
        out_shape=jax.ShapeDtypeStruct((depth, rows, n6), F32),
        compiler_params=_params("arbitrary", "arbitrary"),
        name="ada_mod",
    )(cvec, ada_w, ada_b.reshape(depth, 1, n6))


def _tile_specs(nct, d):
    x_spec = pl.BlockSpec((1, TOKEN_TILE, d), lambda b, i: (b, i, 0))
    mod_spec = pl.BlockSpec((1, 1, 6, d), lambda b, i: (b, (i >= nct).astype(I32), 0, 0))
    return x_spec, mod_spec


def _inproj0_body(x_ref, mod_ref, g_ref, wq, wk, wv, wg, wlow, wu, wa_ref, ba_ref, tri_ref, ones_ref,
                  qd_f, ki_f, ks_f, ed_f, qd_b, ki_b, ks_b, ed_b, ov, og, ou):
    m = mod_ref[0, 0]
    h = _norm_mod(x_ref[0], g_ref[...], m[0:1], m[1:2]).astype(BF16)
    ov[0] = _dot(h, wv[...]).astype(ov.dtype)
    og[0] = _dot(h, wg[...]).astype(og.dtype)
    ou[0] = _dot(h, wu[...])
    q = _dot(h, wq[...]) * (GLA_DK ** -0.5)
    k = _dot(h, wk[...])
    low = _dot(h, wlow[...])
    outs = ((qd_f, ki_f, ks_f, ed_f), (qd_b, ki_b, ks_b, ed_b))
    for d, (qd_ref, ki_ref, ks_ref, ed_ref) in enumerate(outs):
        z = _dot3(low, wa_ref[d]) + ba_ref[d]
        log_a = (jnp.minimum(z, 0.0) - jnp.log1p(jnp.exp(-jnp.abs(z)))) * (1.0 / GLA_TAU)
        la_hi, la_lo = _split(log_a)
        cum = _dot(tri_ref[d], la_hi) + _dot(tri_ref[d], la_lo)
        tot = _dot(ones_ref[...], la_hi) + _dot(ones_ref[...], la_lo)
        qd_ref[0] = (q * jnp.exp(cum)).astype(BF16)
        ki_ref[0] = (k * jnp.exp(-cum)).astype(BF16)
        ks_ref[0] = (k * jnp.exp(tot - cum)).astype(BF16)
        for ch in range(TOKEN_TILE // GLA_CHUNK):
            ed_ref[0, ch] = jnp.exp(tot[ch * GLA_CHUNK:ch * GLA_CHUNK + 1])


def _inproj0(x_all, mods, norm_g, weights, wa_pad, ba, nct):
    bsz, t, d = x_all.shape
    tm = TOKEN_TILE
    x_spec, mod_spec = _tile_specs(nct, d)
    pos = jnp.arange(tm)
    same_chunk = (pos[:, None] // GLA_CHUNK) == (pos[None, :] // GLA_CHUNK)
    tri = jnp.stack([same_chunk & (pos[None, :] <= pos[:, None]),
                     same_chunk & (pos[None, :] >= pos[:, None])]).astype(BF16)
    ones = same_chunk.astype(BF16)
    consts = list(weights) + [wa_pad, ba, tri, ones]
    tok = lambda w, dt: (pl.BlockSpec((1, tm, w), lambda b, i: (b, i, 0)), jax.ShapeDtypeStruct((bsz, t, w), dt))
    per_chunk = (pl.BlockSpec((1, tm // GLA_CHUNK, 1, AB_QK), lambda b, i: (b, i, 0, 0)),
                 jax.ShapeDtypeStruct((bsz, t // GLA_CHUNK, 1, AB_QK), F32))
    one_dir = [tok(AB_QK, BF16)] * 3 + [per_chunk]
    outs = one_dir + one_dir + [tok(AB_V, BF16), tok(AB_V, BF16), tok(S5_CH, F32)]
    return pl.pallas_call(
        _inproj0_body,
        grid=(bsz, t // tm),
        in_specs=[x_spec, mod_spec, _const_spec((1, d))] + [_const_spec(a.shape) for a in consts],
        out_specs=[o[0] for o in outs],
        out_shape=[o[1] for o in outs],
        compiler_params=_params("arbitrary", "arbitrary"),
        name="inproj_gla_s5",
    )(x_all, mods, norm_g.reshape(1, d), *consts)


def _rope(acc, cos_ref, sin_ref, o_ref, scale):
    for grp in range(acc.shape[1] // 128):
        half = grp % 2
        xg = acc[:, grp * 128:(grp + 1) * 128]
        cs = cos_ref[:, half * 128:(half + 1) * 128]
        sn = sin_ref[:, half * 128:(half + 1) * 128]
        out = xg * cs + pltpu.roll(xg, 64, 1) * sn
        o_ref[0, :, grp * 128:(grp + 1) * 128] = (out * scale).astype(o_ref.dtype)


def _inproj1_body(x_ref, mod_ref, g_ref, cos_ref, sin_ref, wq, wk, wv, wg, oq, ok, ov, og):
    m = mod_ref[0, 0]
    h = _norm_mod(x_ref[0], g_ref[...], m[0:1], m[1:2]).astype(BF16)
    _rope(_dot(h, wq[...]), cos_ref, sin_ref, oq, 1.0)
    _rope(_dot(h, wk[...]), cos_ref, sin_ref, ok, RET_DK ** -0.5)
    ov[0] = _dot(h, wv[...]).astype(ov.dtype)
    og[0] = _dot(h, wg[...]).astype(og.dtype)


def _inproj1(x_all, mods, norm_g, cos_t, sin_t, weights, nct):
    bsz, t, d = x_all.shape
    x_spec, mod_spec = _tile_specs(nct, d)
    tab_spec = pl.BlockSpec((TOKEN_TILE, RET_DK), lambda b, i: (i, 0))
    return pl.pallas_call(
        _inproj1_body,
        grid=(bsz, t // TOKEN_TILE),
        in_specs=[x_spec, mod_spec, _const_spec((1, d)), tab_spec, tab_spec] + [_const_spec(w.shape) for w in weights],
        out_specs=[pl.BlockSpec((1, TOKEN_TILE, w.shape[1]), lambda b, i: (b, i, 0)) for w in weights],
        out_shape=[jax.ShapeDtypeStruct((bsz, t, w.shape[1]), BF16) for w in weights],
        compiler_params=_params("arbitrary", "arbitrary"),
        name="inproj_retention",
    )(x_all, mods, norm_g.reshape(1, d), cos_t, sin_t, *weights)


def _backward_chunk(n, n_ctx_chunks, n_chunks):
    return jnp.where(n < n_ctx_chunks, n_ctx_chunks - 1 - n, n_chunks - 1 - (n - n_ctx_chunks))


def _gla_body(qd_f, ki_f, ks_f, ed_f, v_f, qd_b, ki_b, ks_b, ed_b, v_b, hmask_ref, bdmask_ref, o_f, o_b, st_f, st_b):
    c = GLA_CHUNK

    @pl.when(pl.program_id(1) == 0)
    def _():
        st_f[...] = jnp.zeros_like(st_f)
        st_b[...] = jnp.zeros_like(st_b)

    r4 = lax.broadcasted_iota(I32, (GLA_HEADS * c, c), 0) & (c - 1)
    c4 = lax.broadcasted_iota(I32, (GLA_HEADS * c, c), 1)
    dirs = ((qd_f, ki_f, ks_f, ed_f, v_f, o_f, st_f), (qd_b, ki_b, ks_b, ed_b, v_b, o_b, st_b))
    chains = [(bb, d) + dirs[d] for bb in range(qd_f.shape[0]) for d in range(2)]
    scores, inter, grow = [], [], []
    for bb, d, qd_ref, ki_ref, ks_ref, ed_ref, v_ref, o_ref, st_ref in chains:
        q_dec = qd_ref[bb]
        q_heads = jnp.concatenate([q_dec] * GLA_HEADS, axis=0) * hmask_ref[...]
        seen4 = (c4 <= r4) if d == 0 else (c4 >= r4)
        scores.append(jnp.where(seen4, _dot_nt(q_heads, ki_ref[bb]), 0.0).astype(BF16))
        inter.append(_dot_nt(q_dec, st_ref[bb].astype(BF16)))
        grow.append(_dot_tn(v_ref[bb], ks_ref[bb]))
    for (bb, d, qd_ref, ki_ref, ks_ref, ed_ref, v_ref, o_ref, st_ref), sc, o_inter, dst in zip(chains, scores, inter, grow):
        v = v_ref[bb]
        o_intra = jnp.concatenate(
            [_dot(sc[h * c:(h + 1) * c], v[:, h * GLA_DV:(h + 1) * GLA_DV]) for h in range(GLA_HEADS)], axis=1)
        o_ref[bb] = o_intra + o_inter
        st_ref[bb] = st_ref[bb] * ed_ref[bb, 0] + bdmask_ref[...] * dst


def _gla(per_dir, v, n_ctx):
    bsz, t, _ = v.shape
    nc, ncc = t // GLA_CHUNK, n_ctx // GLA_CHUNK
    gb = GLA_BATCH
    fwd = lambda b, n: (b, n, 0)
    bwd = lambda b, n: (b, _backward_chunk(n, ncc, nc), 0)
    hmask = (jnp.arange(AB_QK)[:, None] // GLA_CHUNK == jnp.arange(AB_QK)[None, :] // GLA_DK).astype(BF16)
    bdmask = (jnp.arange(AB_V)[:, None] // GLA_DV == jnp.arange(AB_QK)[None, :] // GLA_DK).astype(F32)

    def specs(idx):
        idx4 = lambda b, n: idx(b, n) + (0,)
        return [pl.BlockSpec((gb, GLA_CHUNK, AB_QK), idx)] * 3 + [pl.BlockSpec((gb, 1, 1, AB_QK), idx4),
                                                                  pl.BlockSpec((gb, GLA_CHUNK, AB_V), idx)]

    return pl.pallas_call(
        _gla_body,
        grid=(bsz // gb, nc),
        in_specs=specs(fwd) + specs(bwd) + [_const_spec(hmask.shape), _const_spec(bdmask.shape)],
        out_specs=[pl.BlockSpec((gb, GLA_CHUNK, AB_V), fwd), pl.BlockSpec((gb, GLA_CHUNK, AB_V), bwd)],
        out_shape=[jax.ShapeDtypeStruct((bsz, t, AB_V), F32)] * 2,
        scratch_shapes=[pltpu.VMEM((gb, AB_V, AB_QK), F32)] * 2,
        compiler_params=_params("arbitrary", "arbitrary"),
        name="gla_scan",
    )(*per_dir[0], v, *per_dir[1], v, hmask, bdmask)


def _s5_operators(lam_re, lam_im, log_step, b_re, b_im, c_re, c_im):
    hp = lax.Precision.HIGHEST
    ln = S5_CHUNK
    step = jnp.exp(log_step.astype(F32))[..., None]
    lam_re, lam_im = lam_re.astype(F32), lam_im.astype(F32)
    mag = jnp.exp(lam_re * step)
    a_re, a_im = mag * jnp.cos(lam_im * step), mag * jnp.sin(lam_im * step)
    den = lam_re * lam_re + lam_im * lam_im
    f_re = ((a_re - 1.0) * lam_re + a_im * lam_im) / den
    f_im = (a_im * lam_re - (a_re - 1.0) * lam_im) / den
    bb_re = f_re[..., None] * b_re - f_im[..., None] * b_im
    bb_im = f_re[..., None] * b_im + f_im[..., None] * b_re
    pw_re, pw_im = [jnp.ones_like(a_re)], [jnp.zeros_like(a_im)]
    for _ in range(ln):
        pr, pi = pw_re[-1], pw_im[-1]
        pw_re.append(pr * a_re - pi * a_im)
        pw_im.append(pr * a_im + pi * a_re)
    pw_re, pw_im = jnp.stack(pw_re, 1), jnp.stack(pw_im, 1)
    ca_re = c_re[:, None] * pw_re[:, :, :, None, :] - c_im[:, None] * pw_im[:, :, :, None, :]
    ca_im = c_re[:, None] * pw_im[:, :, :, None, :] + c_im[:, None] * pw_re[:, :, :, None, :]
    kern = (jnp.einsum('dtgcp,dgpe->dtgec', ca_re[:, :ln], bb_re, precision=hp)
            - jnp.einsum('dtgcp,dgpe->dtgec', ca_im[:, :ln], bb_im, precision=hp))
    pos = jnp.arange(ln)
    i_out, j_in = pos[None, :], pos[:, None]
    tz, wx, wy, ac = [], [], [], []
    for d in range(2):
        lag = (i_out - j_in) if d == 0 else (j_in - i_out)
        blk = jnp.where((lag >= 0)[:, :, None, None, None], kern[d][jnp.clip(lag, 0, ln - 1)], 0.0)
        tz.append(blk.transpose(2, 0, 3, 1, 4).reshape(S5_GROUPS, ln * S5_GROUP, ln * S5_GROUP))
        p_in = (ln - 1 - pos) if d == 0 else pos
        ar, ai = pw_re[d][p_in], pw_im[d][p_in]
        x_re = ar[..., None] * bb_re[d][None] - ai[..., None] * bb_im[d][None]
        x_im = ar[..., None] * bb_im[d][None] + ai[..., None] * bb_re[d][None]
        to_rows = lambda m: m.transpose(1, 0, 3, 2).reshape(S5_GROUPS, ln * S5_GROUP, S5_P)
        wx.append(jnp.concatenate([to_rows(x_re), to_rows(x_im), to_rows(x_im), to_rows(x_re)], axis=-1))
        p_out = (pos + 1) if d == 0 else (ln - pos)
        to_cols = lambda m: m.transpose(1, 3, 0, 2).reshape(S5_GROUPS, S5_P, ln * S5_GROUP)
        wy.append(jnp.concatenate([to_cols(ca_re[d][p_out]), -to_cols(ca_im[d][p_out])], axis=1))
        lr, li = pw_re[d][ln], pw_im[d][ln]
        rows = [jnp.concatenate([lr, lr], -1), jnp.concatenate([-li, li], -1), jnp.concatenate([li, -li], -1)]
        ac.append(jnp.stack(rows + [jnp.zeros_like(rows[0])] * 5, axis=1))
    return (jnp.stack(tz).astype(BF16), jnp.stack(wx).astype(BF16), jnp.stack(wy).astype(BF16), jnp.stack(ac))


def _s5_body(ncs_ctx, ncs, rows, u_ref, tz_ref, wx_ref, wy_ref, ac_ref, y_ref, xx_f, xx_b, sin_f, sin_b):
    u = u_ref[0]
    xx_f[...] = _dot(u, wx_ref[0, 0])
    xx_b[...] = _dot(u, wx_ref[1, 0])
    ac_f, ac_b = ac_ref[0, 0], ac_ref[1, 0]
    half = 2 * S5_P

    def advance(ac, s, s_sw, xx):
        return (ac[0:1] * s + ac[1:2] * s_sw + xx[:, :half], ac[0:1] * s_sw + ac[2:3] * s + xx[:, half:])

    def step(n, carry):
        s_f, sw_f, s_b, sw_b = carry
        r_f = pl.multiple_of(n * rows, rows)
        r_b = pl.multiple_of(_backward_chunk(n, ncs_ctx, ncs) * rows, rows)
        sin_f[pl.ds(r_f, rows), :] = s_f
        sin_b[pl.ds(r_b, rows), :] = s_b
        s_f, sw_f = advance(ac_f, s_f, sw_f, xx_f[pl.ds(r_f, rows), :])
        s_b, sw_b = advance(ac_b, s_b, sw_b, xx_b[pl.ds(r_b, rows), :])
        return s_f, sw_f, s_b, sw_b

    zero = jnp.zeros((rows, half), F32)
    lax.fori_loop(0, ncs, step, (zero, zero, zero, zero))
    y_ref[0] = (_dot(u, tz_ref[0, 0]) + _dot(u, tz_ref[1, 0])
                + _dot(sin_f[...].astype(BF16), wy_ref[0, 0]) + _dot(sin_b[...].astype(BF16), wy_ref[1, 0]))


def _s5(u, ops, n_ctx):
    tz, wx, wy, ac = ops
    bsz, t, _ = u.shape
    ln, lanes = S5_CHUNK, S5_CHUNK * S5_GROUP
    ncs, ncs_ctx = t // ln, n_ctx // ln
    m = ncs * bsz
    ug = u.astype(BF16).reshape(bsz, ncs, ln, S5_GROUPS, S5_GROUP).transpose(3, 1, 0, 2, 4).reshape(S5_GROUPS, m, lanes)
    dir_spec = lambda shape: pl.BlockSpec((2, 1) + shape, lambda g: (0, g, 0, 0))
    y = pl.pallas_call(
        functools.partial(_s5_body, ncs_ctx, ncs, bsz),
        grid=(S5_GROUPS,),
        in_specs=[pl.BlockSpec((1, m, lanes), lambda g: (g, 0, 0)), dir_spec((lanes, lanes)),
                  dir_spec((lanes, 4 * S5_P)), dir_spec((2 * S5_P, lanes)), dir_spec((8, 2 * S5_P))],
        out_specs=pl.BlockSpec((1, m, lanes), lambda g: (g, 0, 0)),
        out_shape=jax.ShapeDtypeStruct((S5_GROUPS, m, lanes), F32),
        scratch_shapes=[pltpu.VMEM((m, 4 * S5_P), F32)] * 2 + [pltpu.VMEM((m, 2 * S5_P), F32)] * 2,
        compiler_params=_params("arbitrary"),
        name="s5_scan",
    )(ug, tz, wx, wy, ac)
    return y.reshape(S5_GROUPS, ncs, bsz, ln, S5_GROUP).transpose(2, 1, 3, 0, 4).reshape(bsz, t, S5_CH)


def _ret_body(ncc, q_f, k_f, v_f, q_b, k_b, v_b, dmat_ref, rsc_ref, csc_ref, gam_ref, o_f, o_b, st_f, st_b):
    n = pl.program_id(1)

    @pl.when(n == 0)
    def _():
        st_f[...] = jnp.zeros_like(st_f)
        st_b[...] = jnp.zeros_like(st_b)

    dirs = ((q_f, k_f, v_f, o_f, st_f), (q_b, k_b, v_b, o_b, st_b))

    @pl.when(n >= ncc)
    def _():
        for d, (q_ref, k_ref, v_ref, o_ref, st_ref) in enumerate(dirs):
            for h in range(RET_HEADS):
                qh = q_ref[0, :, h * RET_DK:(h + 1) * RET_DK]
                kh = k_ref[0, :, h * RET_DK:(h + 1) * RET_DK]
                vh = v_ref[0, :, h * RET_DV:(h + 1) * RET_DV]
                scores = (_dot_nt(qh, kh) * dmat_ref[d, h]).astype(BF16)
                o = _dot(scores, vh) + rsc_ref[d, h] * _dot(qh, st_ref[h].astype(BF16))
                o_ref[0, :, h * RET_DV:(h + 1) * RET_DV] = o.astype(o_ref.dtype)

    for d, (q_ref, k_ref, v_ref, o_ref, st_ref) in enumerate(dirs):
        for h in range(RET_HEADS):
            kh = k_ref[0, :, h * RET_DK:(h + 1) * RET_DK]
            vh = v_ref[0, :, h * RET_DV:(h + 1) * RET_DV]
            k_state = (kh.astype(F32) * csc_ref[d, h]).astype(BF16)
            st_ref[h] = st_ref[h] * gam_ref[d, h] + _dot_tn(k_state, vh)


def _retention(q, k, v, decay_logit, n_ctx):
    bsz, t, _ = q.shape
    c = RET_CHUNK
    nc, ncc = t // c, n_ctx // c
    nl = nc - ncc
    log_gamma = jax.nn.log_sigmoid(decay_logit.astype(F32))[:, :, None, None]
    i = jnp.arange(c, dtype=F32)
    lag = i[:, None] - i[None, :]
    lag = jnp.stack([lag, -lag])[:, None]
    dmat = jnp.where(lag >= 0, jnp.exp(log_gamma * jnp.maximum(lag, 0.0)), 0.0)
    done = jnp.stack([i + 1.0, c - i])[:, None, :, None]
    rsc = jnp.exp(log_gamma * done)
    csc = jnp.exp(log_gamma * (c - done))
    gam = jnp.exp(log_gamma[:, :, 0, 0] * c)
    fwd = lambda b, n: (b, n, 0)
    bwd = lambda b, n: (b, _backward_chunk(n, ncc, nc), 0)
    o_fwd = lambda b, n: (b, jnp.maximum(n - ncc, 0), 0)
    o_bwd = lambda b, n: (b, nl - 1 - jnp.maximum(n - ncc, 0), 0)

    def specs(idx):
        return [pl.BlockSpec((1, c, RET_QK), idx), pl.BlockSpec((1, c, RET_QK), idx), pl.BlockSpec((1, c, RET_MIX), idx)]

    return pl.pallas_call(
        functools.partial(_ret_body, ncc),
        grid=(bsz, nc),
        in_specs=specs(fwd) + specs(bwd) + [_const_spec(dmat.shape), _const_spec(rsc.shape), _const_spec(csc.shape),
                                            pl.BlockSpec(memory_space=pltpu.SMEM)],
        out_specs=[pl.BlockSpec((1, c, RET_MIX), o_fwd), pl.BlockSpec((1, c, RET_MIX), o_bwd)],
        out_shape=[jax.ShapeDtypeStruct((bsz, nl * c, RET_MIX), BF16)] * 2,
        scratch_shapes=[pltpu.VMEM((RET_HEADS, RET_DK, RET_DV), F32)] * 2,
        compiler_params=_params("arbitrary", "arbitrary"),
        name="retention_scan",
    )(q, k, v, q, k, v, dmat, rsc, csc, gam)


def _route(x, mixed, mod, n2g_ref, wr_ref, br_ref, x1_ref, h2_ref, e_ref, w_ref, r_ref, cnt_ref):
    tm = x.shape[0]
    x1 = x + mod[2:3] * mixed
    x1_ref[0] = x1
    h2 = _norm_mod(x1, n2g_ref[...], mod[3:4], mod[4:5])
    h2_ref[0] = _pack_rows(h2)
    logits = _dot3(wr_ref[...], h2, dot=_dot_nt) + br_ref[...]
    ie = lax.broadcasted_iota(I32, logits.shape, 0)
    tops, picks = [], []
    for _ in range(TOP_K):
        mx = jnp.max(logits, axis=0, keepdims=True)
        pick = jnp.min(jnp.where(logits == mx, ie, N_EXPERTS), axis=0, keepdims=True)
        tops.append(mx)
        picks.append(pick)
        logits = jnp.where(ie == pick, -jnp.inf, logits)
    ex = [jnp.exp(tk - tops[0]) for tk in tops]
    den = ex[0] + ex[1] + ex[2] + ex[3]
    for kk in range(TOP_K):
        w_ref[0, kk:kk + 1, :] = ex[kk] / den
        e_ref[0, kk:kk + 1, :] = picks[kk]

    @pl.when((pl.program_id(0) == 0) & (pl.program_id(1) == 0))
    def _():
        cnt_ref[...] = jnp.zeros_like(cnt_ref)

    earlier = (lax.broadcasted_iota(I32, (tm, tm), 0) < lax.broadcasted_iota(I32, (tm, tm), 1))
    earlier = jnp.where(earlier, 1.0, 0.0).astype(BF16)
    run = cnt_ref[:, 0:1]
    for kk, pick in enumerate(picks):
        onehot = jnp.where(ie == pick, 1.0, 0.0)
        before = _dot(onehot.astype(BF16), earlier) + run
        r_ref[0, kk:kk + 1, :] = jnp.sum(onehot * before, axis=0, keepdims=True).astype(I32)
        run = run + jnp.sum(onehot, axis=1, keepdims=True)
    cnt_ref[...] = jnp.broadcast_to(run, cnt_ref.shape)


def _mix0_body(x_ref, mod_ref, of_ref, ob_ref, g_ref, ys_ref, u_ref, gng_ref, dsk_ref, gluw_ref, glub_ref,
               wo_ref, n2g_ref, wr_ref, br_ref, x1_ref, h2_ref, e_ref, w_ref, r_ref, cnt_ref):
    o = of_ref[0] + ob_ref[0]
    heads = []
    for h in range(GLA_HEADS):
        oh = o[:, h * GLA_DV:(h + 1) * GLA_DV]
        heads.append(oh * lax.rsqrt(jnp.mean(oh * oh, axis=-1, keepdims=True) + EPS))
    gla = jnp.concatenate(heads, axis=1) * gng_ref[...] * _silu(g_ref[0].astype(F32))
    y = jax.nn.gelu(ys_ref[0] + dsk_ref[...] * u_ref[0].astype(F32))
    y = y * jax.nn.sigmoid(_dot(y.astype(BF16), gluw_ref[...]) + glub_ref[...])
    mixed = _dot(gla.astype(BF16), wo_ref[0:AB_V]) + _dot(y.astype(BF16), wo_ref[AB_V:AB_V + S5_CH])
    _route(x_ref[0], mixed, mod_ref[0, 0], n2g_ref, wr_ref, br_ref, x1_ref, h2_ref, e_ref, w_ref, r_ref, cnt_ref)


def _mix1_body(x_ref, mod_ref, of_ref, ob_ref, g_ref, ng_ref, wo_ref, n2g_ref, wr_ref, br_ref,
               x1_ref, h2_ref, e_ref, w_ref, r_ref, cnt_ref):
    mixed = None
    for h in range(RET_HEADS):
        sl = slice(h * RET_DV, (h + 1) * RET_DV)
        oh = of_ref[0, :, sl].astype(F32) + ob_ref[0, :, sl].astype(F32)
        mu = jnp.mean(oh, axis=-1, keepdims=True)
        cen = oh - mu
        var = jnp.mean(cen * cen, axis=-1, keepdims=True)
        gated = cen * lax.rsqrt(var + EPS) * ng_ref[:, sl] * _silu(g_ref[0, :, sl].astype(F32))
        part = _dot(gated.astype(BF16), wo_ref[sl])
        mixed = part if mixed is None else mixed + part
    _route(x_ref[0], mixed, mod_ref[0, 0], n2g_ref, wr_ref, br_ref, x1_ref, h2_ref, e_ref, w_ref, r_ref, cnt_ref)


def _mix_call(body, name, x_all, mods, tiles, acts, consts, norm2_g, w_router, b_router, n_tok, seg_tile0):
    bsz, _, d = x_all.shape
    tm = TOKEN_TILE
    off = lambda b, i: (b, i + seg_tile0, 0)
    loc = lambda b, i: (b, i, 0)
    ntl = bsz * tiles
    flat = lambda b, i: (b * tiles + i, 0, 0)
    in_specs = [pl.BlockSpec((1, tm, d), off),
                pl.BlockSpec((1, 1, 6, d), lambda b, i: (b, ((i + seg_tile0) >= n_tok).astype(I32), 0, 0))]
    args = [x_all, mods]
    for arr, offset in acts:
        in_specs.append(pl.BlockSpec((1, tm, arr.shape[2]), off if offset else loc))
        args.append(arr)
    tail = list(consts) + [norm2_g.reshape(1, d), w_router.T, b_router.reshape(N_EXPERTS, 1)]
    in_specs += [_const_spec(a.shape) for a in tail]
    args += tail
    tok_out = pl.BlockSpec((1, TOP_K, tm), flat)
    return pl.pallas_call(
        body,
        grid=(bsz, tiles),
        in_specs=in_specs,
        out_specs=[pl.BlockSpec((1, tm, d), loc), pl.BlockSpec((1, tm, d // 2), loc), tok_out, tok_out, tok_out,
                   _const_spec((N_EXPERTS, 128))],
        out_shape=[jax.ShapeDtypeStruct((bsz, tiles * tm, d), F32), jax.ShapeDtypeStruct((bsz, tiles * tm, d // 2), U32),
                   jax.ShapeDtypeStruct((ntl, TOP_K, tm), I32), jax.ShapeDtypeStruct((ntl, TOP_K, tm), F32),
                   jax.ShapeDtypeStruct((ntl, TOP_K, tm), I32), jax.ShapeDtypeStruct((N_EXPERTS, 128), F32)],
        compiler_params=_params("arbitrary", "arbitrary"),
        name=name,
    )(*args)


def _cast_rows(src_ref, dst_ref, rows):
    def chunk(j, carry):
        r = pl.multiple_of(j * rows, rows)
        dst_ref[pl.ds(r, rows), :] = src_ref[0, 0, pl.ds(r, rows), :].astype(BF16)
        return carry

    lax.fori_loop(0, dst_ref.shape[0] // rows, chunk, 0)


def _expert_body(be_ref, nu_ref, x_ref, wgu_ref, bgu_ref, wd_ref, bd_ref, o_ref, wgu_bf, wd_bf):
    i = pl.program_id(0)
    live = i < nu_ref[0]
    new_expert = (i == 0) | (be_ref[i] != be_ref[jnp.maximum(i - 1, 0)])

    @pl.when(live & new_expert)
    def _():
        _cast_rows(wgu_ref, wgu_bf, 128)
        _cast_rows(wd_ref, wd_bf, 128)

    @pl.when(live)
    def _():
        x_lo, x_hi = _unpack_rows(x_ref[...])
        half = x_lo.shape[1]
        gu = (_dot(x_lo.astype(BF16), wgu_bf[0:half]) + _dot(x_hi.astype(BF16), wgu_bf[half:2 * half])
              + bgu_ref[0, 0])
        gate = jnp.minimum(gu[:, :D_FF], SWIGLU_LIMIT)
        lin = jnp.clip(gu[:, D_FF:], -SWIGLU_LIMIT, SWIGLU_LIMIT)
        act = gate * jax.nn.sigmoid(SWIGLU_ALPHA * gate) * (lin + 1.0)
        y = _dot(act.astype(BF16), wd_bf[...]) + bd_ref[0, 0]
        o_ref[...] = _pack_rows(y)

    @pl.when(i >= nu_ref[0])
    def _():
        o_ref[...] = jnp.zeros_like(o_ref)


def _experts(xb, block_e, n_used, layer, w_gu, b_gu, w_down, b_down):
    n_slots, half = xb.shape
    d = 2 * half
    n_blocks = n_slots // MOE_BLOCK
    depth = w_gu.shape[0]
    by_expert = lambda i, be, nu: (layer, be[i], 0, 0)
    return pl.pallas_call(
        _expert_body,
        grid_spec=pltpu.PrefetchScalarGridSpec(
            num_scalar_prefetch=2,
            grid=(n_blocks,),
            in_specs=[pl.BlockSpec((MOE_BLOCK, half), lambda i, be, nu: (i, 0)),
                      pl.BlockSpec((1, 1, d, 2 * D_FF), by_expert), pl.BlockSpec((1, 1, 1, 2 * D_FF), by_expert),
                      pl.BlockSpec((1, 1, D_FF, d), by_expert), pl.BlockSpec((1, 1, 1, d), by_expert)],
            out_specs=pl.BlockSpec((MOE_BLOCK, half), lambda i, be, nu: (i, 0)),
            scratch_shapes=[pltpu.VMEM((d, 2 * D_FF), BF16), pltpu.VMEM((D_FF, d), BF16)]),
        out_shape=jax.ShapeDtypeStruct((n_slots, half), U32),
        compiler_params=_params("arbitrary"),
        name="moe_experts",
    )(block_e, n_used, xb, w_gu, b_gu.reshape(depth, N_EXPERTS, 1, 2 * D_FF), w_down,
      b_down.reshape(depth, N_EXPERTS, 1, d))


def _combine_body(final, x1_ref, mod_ref, yk_ref, w_ref, *refs):
    d = x1_ref.shape[2]
    half = d // 2
    y_lo, y_hi = None, None
    for k in range(TOP_K):
        lo, hi = _unpack_rows(yk_ref[k, 0])
        wk = w_ref[0, :, k:k + 1]
        y_lo = lo * wk if y_lo is None else y_lo + lo * wk
        y_hi = hi * wk if y_hi is None else y_hi + hi * wk
    g2 = mod_ref[0, 0][5:6]
    x2_lo = x1_ref[0, :, 0:half] + g2[:, 0:half] * y_lo
    x2_hi = x1_ref[0, :, half:d] + g2[:, half:d] * y_hi
    if final:
        fg_ref, o_ref = refs
        ms = (jnp.sum(x2_lo * x2_lo, axis=-1, keepdims=True) + jnp.sum(x2_hi * x2_hi, axis=-1, keepdims=True)) / d
        r = lax.rsqrt(ms + EPS)
        x2_lo = x2_lo * r * fg_ref[:, 0:half]
        x2_hi = x2_hi * r * fg_ref[:, half:d]
    else:
        (o_ref,) = refs
    o_ref[0, :, 0:half] = x2_lo
    o_ref[0, :, half:d] = x2_hi


def _combine(x1, mods, yk, w_tok, seg_tile0, n_tok, final_g):
    bsz, t, d = x1.shape
    tm = TOKEN_TILE
    tiles = t // tm
    loc = lambda b, i: (b, i, 0)
    in_specs = [pl.BlockSpec((1, tm, d), loc),
                pl.BlockSpec((1, 1, 6, d), lambda b, i: (b, ((i + seg_tile0) >= n_tok).astype(I32), 0, 0)),
                pl.BlockSpec((TOP_K, 1, tm, d // 2), lambda b, i: (0, b, i, 0)),
                pl.BlockSpec((1, tm, TOP_K), loc)]
    args = [x1, mods, yk.reshape(TOP_K, bsz, t, d // 2), w_tok.reshape(bsz, t, TOP_K)]
    if final_g is not None:
        in_specs.append(_const_spec((1, d)))
        args.append(final_g.reshape(1, d))
    return pl.pallas_call(
        functools.partial(_combine_body, final_g is not None),
        grid=(bsz, tiles),
        in_specs=in_specs,
        out_specs=pl.BlockSpec((1, tm, d), loc),
        out_shape=jax.ShapeDtypeStruct((bsz, t, d), F32),
        compiler_params=_params("arbitrary", "arbitrary"),
        name="moe_combine",
    )(*args)


def _sc_mesh():
    return plsc.VectorSubcoreMesh(core_axis_name="core", subcore_axis_name="subcore",
                                  num_cores=SC_CORES, num_subcores=SC_SUBCORES)


def _sc_worker_base(per_worker):
    return (lax.axis_index("subcore") * SC_CORES + lax.axis_index("core")) * per_worker


def _sc_dispatch(rows, dest, n_slots):
    n, w = rows.shape
    per_worker = n // SC_WORKERS
    assert per_worker * SC_WORKERS == n and per_worker % SC_CHUNK == 0

    @functools.partial(
        pl.kernel, mesh=_sc_mesh(), out_type=jax.ShapeDtypeStruct((n_slots, w), rows.dtype),
        scratch_types=[pltpu.VMEM((SC_CHUNK,), I32)] * TOP_K + [pltpu.VMEM((SC_CHUNK, w), rows.dtype),
                                                                pltpu.SemaphoreType.DMA],
        name="moe_dispatch")
    def scatter_rows(rows_hbm, dest_hbm, out_hbm, *scratch):
        idx_refs, buf, sem = scratch[:TOP_K], scratch[TOP_K], scratch[TOP_K + 1]
        base0 = _sc_worker_base(per_worker)

        @pl.loop(0, per_worker // SC_CHUNK)
        def _(j):
            base = base0 + j * SC_CHUNK
            pltpu.sync_copy(rows_hbm.at[pl.ds(base, SC_CHUNK)], buf)
            for k, idx in enumerate(idx_refs):
                pltpu.sync_copy(dest_hbm.at[pl.ds(k * n + base, SC_CHUNK)], idx)
            copies = [pltpu.async_copy(buf, out_hbm.at[idx], sem) for idx in idx_refs]
            for cp in copies:
                cp.wait()

    return scatter_rows(rows, dest)


def _sc_gather(table, idx):
    n = idx.shape[0]
    w = table.shape[1]
    per_worker = n // SC_WORKERS
    n_chunks = per_worker // SC_CHUNK
    assert per_worker * SC_WORKERS == n and n_chunks * SC_CHUNK == per_worker and n_chunks % 2 == 0

    @functools.partial(
        pl.kernel, mesh=_sc_mesh(), out_type=jax.ShapeDtypeStruct((n, w), table.dtype),
        scratch_types=([pltpu.VMEM((SC_CHUNK,), I32)] * 2 + [pltpu.VMEM((SC_CHUNK, w), table.dtype)] * 2
                       + [pltpu.SemaphoreType.DMA] * 4),
        name="moe_gather")
    def gather_rows(table_hbm, idx_hbm, out_hbm, idx0, idx1, buf0, buf1, gsem0, gsem1, wsem0, wsem1):
        base0 = _sc_worker_base(per_worker)

        def gather_copy(idx_v, buf, sem):
            return pltpu.make_async_copy(table_hbm.at[idx_v], buf, sem)

        def write_copy(j, buf, sem):
            return pltpu.make_async_copy(buf, out_hbm.at[pl.ds(base0 + j * SC_CHUNK, SC_CHUNK)], sem)

        def start_gather(j, idx_v, buf, sem):
            pltpu.sync_copy(idx_hbm.at[pl.ds(base0 + j * SC_CHUNK, SC_CHUNK)], idx_v)
            gather_copy(idx_v, buf, sem).start()

        start_gather(0, idx0, buf0, gsem0)

        @pl.loop(0, n_chunks, step=2)
        def _(j):
            @pl.when(j > 0)
            def _():
                write_copy(j - 1, buf1, wsem1).wait()
            start_gather(j + 1, idx1, buf1, gsem1)
            gather_copy(idx0, buf0, gsem0).wait()
            write_copy(j, buf0, wsem0).start()

            @pl.when(j + 2 < n_chunks)
            def _():
                write_copy(j, buf0, wsem0).wait()
                start_gather(j + 2, idx0, buf0, gsem0)
            gather_copy(idx1, buf1, gsem1).wait()
            write_copy(j + 1, buf1, wsem1).start()

        write_copy(n_chunks - 2, buf0, wsem0).wait()
        write_copy(n_chunks - 1, buf1, wsem1).wait()

    return gather_rows(table, idx)


def _moe(h2, e_tl, w_tl, r_tl, cnt, layer, w_gu, b_gu, w_down, b_down):
    bsz, t, half = h2.shape
    n = bsz * t
    flat = lambda a: a.transpose(1, 0, 2).reshape(TOP_K, n)
    e_k, w_k, r_k = flat(e_tl), flat(w_tl), flat(r_tl)
    counts = cnt[:, 0].astype(I32)
    padded = (counts + MOE_BLOCK - 1) // MOE_BLOCK * MOE_BLOCK
    pad_end = jnp.cumsum(padded)
    pad_start = pad_end - padded
    n_blocks = (n * TOP_K + MOE_BLOCK - 1) // MOE_BLOCK + N_EXPERTS
    block_start = jnp.arange(n_blocks, dtype=I32) * MOE_BLOCK
    block_e = jnp.minimum(jnp.sum((pad_end[None, :] <= block_start[:, None]).astype(I32), axis=1), N_EXPERTS - 1)
    n_used = (pad_end[-1:] // MOE_BLOCK).astype(I32)
    start_k = jnp.sum(jnp.where(e_k[..., None] == jnp.arange(N_EXPERTS, dtype=I32), pad_start, 0), axis=-1)
    dest = (start_k + r_k).reshape(TOP_K * n)
    xb = _sc_dispatch(h2.reshape(n, half), dest, n_blocks * MOE_BLOCK)
    yb = _experts(xb, block_e, n_used, layer, w_gu, b_gu, w_down, b_down)
    return _sc_gather(yb, dest), w_k.T


def _rope_tables(n_ctx, n_lat):
    n_freq = RET_DK // 4
    inv_freq = ROPE_BASE ** (-jnp.arange(n_freq, dtype=F32) / n_freq)
    pos = jnp.arange(n_lat, dtype=I32)
    cos, sin = [], []
    for p in (pos // GRID_W, pos % GRID_W):
        ang = p.astype(F32)[:, None] * inv_freq
        cos += [jnp.cos(ang), jnp.cos(ang)]
        sin += [-jnp.sin(ang), jnp.sin(ang)]
    cos, sin = jnp.concatenate(cos, axis=1), jnp.concatenate(sin, axis=1)
    return (jnp.concatenate([jnp.ones((n_ctx, RET_DK), F32), cos], axis=0),
            jnp.concatenate([jnp.zeros((n_ctx, RET_DK), F32), sin], axis=0))


def kernel(x, c, ctx, c_ctx, ada_w, ada_b, norm1_g, norm2_g, ab_w_in, ab_w_out, gla_wa, gla_ba, gla_norm_g, s5_lam_re, s5_lam_im, s5_log_step, s5_b_re, s5_b_im, s5_c_re, s5_c_im, s5_d, s5_glu_w, s5_glu_b, ret_w_in, ret_w_out, ret_decay_logit, ret_norm_g, moe_w_router, moe_b_router, moe_w_gu, moe_b_gu, moe_w_down, moe_b_down, final_norm_g):
    bsz, n_lat, d = x.shape
    n_ctx = ctx.shape[1]
    depth = ada_w.shape[0]
    assert depth == 2 and d == D_MODEL and bsz == 8, "kernels are laid out for the stated problem shape"
    assert n_ctx % TOKEN_TILE == 0 and n_lat % TOKEN_TILE == 0 and n_lat % GRID_W == 0
    t = n_ctx + n_lat
    nct = n_ctx // TOKEN_TILE

    cvec = jnp.zeros((16, d), F32).at[:bsz].set(c).at[bsz].set(c_ctx)
    mod = _ada_mod(cvec, ada_w, ada_b).reshape(depth, 16, 6, d)
    mods = [jnp.stack([jnp.broadcast_to(mod[l, bsz], (bsz, 6, d)), mod[l, :bsz]], axis=1) for l in range(depth)]

    x_all = jnp.concatenate([ctx, x], axis=1)

    w_in = ab_w_in[0].astype(BF16)
    cuts = [0, AB_QK, 2 * AB_QK, 2 * AB_QK + AB_V, 2 * AB_QK + 2 * AB_V, 2 * AB_QK + 2 * AB_V + 2 * GLA_RANK,
            w_in.shape[1]]
    pieces = [w_in[:, a:b] for a, b in zip(cuts[:-1], cuts[1:])]
    wa_pad = jnp.zeros((2, 2 * GLA_RANK, AB_QK), F32)
    wa_pad = wa_pad.at[0, :GLA_RANK].set(gla_wa[0, 0]).at[1, GLA_RANK:].set(gla_wa[0, 1])
    outs = _inproj0(x_all, mods[0], norm1_g[0], pieces, wa_pad, gla_ba[0].reshape(2, 1, AB_QK), nct)
    v, g, u = outs[8:]
    o_f, o_b = _gla((outs[0:4], outs[4:8]), v, n_ctx)
    ops = _s5_operators(s5_lam_re[0], s5_lam_im[0], s5_log_step[0], s5_b_re[0], s5_b_im[0], s5_c_re[0], s5_c_im[0])
    ys = _s5(u, ops, n_ctx)
    consts = [jnp.tile(gla_norm_g[0], GLA_HEADS).reshape(1, AB_V), s5_d[0].reshape(1, S5_CH),
              s5_glu_w[0].astype(BF16), s5_glu_b[0].reshape(1, S5_CH), ab_w_out[0].astype(BF16)]
    x1, h2, e_tl, w_tl, r_tl, cnt = _mix_call(
        _mix0_body, "mix_gla_s5", x_all, mods[0], t // TOKEN_TILE,
        [(o_f, False), (o_b, False), (g, False), (ys, False), (u, False)], consts,
        norm2_g[0], moe_w_router[0], moe_b_router[0], nct, 0)
    yk, w_tok = _moe(h2, e_tl, w_tl, r_tl, cnt, 0, moe_w_gu, moe_b_gu, moe_w_down, moe_b_down)
    x_all = _combine(x1, mods[0], yk, w_tok, 0, nct, None)

    w_in = ret_w_in[0].astype(BF16)
    cuts = [0, RET_QK, 2 * RET_QK, 2 * RET_QK + RET_MIX, w_in.shape[1]]
    pieces = [w_in[:, a:b] for a, b in zip(cuts[:-1], cuts[1:])]
    cos_t, sin_t = _rope_tables(n_ctx, n_lat)
    q, k, v, g = _inproj1(x_all, mods[1], norm1_g[1], cos_t, sin_t, pieces, nct)
    o_f, o_b = _retention(q, k, v, ret_decay_logit[0], n_ctx)
    consts = [ret_norm_g[0].reshape(1, RET_MIX), ret_w_out[0].astype(BF16)]
    x1, h2, e_tl, w_tl, r_tl, cnt = _mix_call(
        _mix1_body, "mix_retention", x_all, mods[1], n_lat // TOKEN_TILE,
        [(o_f, False), (o_b, False), (g, True)], consts,
        norm2_g[1], moe_w_router[1], moe_b_router[1], nct, nct)
    yk, w_tok = _moe(h2, e_tl, w_tl, r_tl, cnt, 1, moe_w_gu, moe_b_gu, moe_w_down, moe_b_down)
    return _combine(x1, mods[1], yk, w_tok, nct, nct, final_norm_g)
```

```python
import functools
import math

import jax
import jax.numpy as jnp
from jax import lax
from jax.experimental import pallas as pl
from jax.experimental.pallas import tpu as pltpu
from jax.experimental.pallas import tpu_sc as plsc

F32, BF16, I32, U32 = jnp.float32, jnp.bfloat16, jnp.int32, jnp.uint32

D_MODEL = 1024
GRID_W = 64
EPS = 1e-6
GLA_HEADS, GLA_DK, GLA_DV, GLA_RANK, GLA_TAU, GLA_CHUNK = 4, 64, 128, 16, 16.0, 64
GLA_BATCH = 4
AB_QK, AB_V = GLA_HEADS * GLA_DK, GLA_HEADS * GLA_DV
S5_CH, S5_GROUP, S5_GROUPS, S5_P = 512, 16, 32, 64
S5_CHUNK = 16
S5_FOLD_BATCH = 4
RET_HEADS, RET_DK, RET_DV = 4, 256, 512
RET_CHUNK = 256
RET_QK, RET_MIX = RET_HEADS * RET_DK, RET_HEADS * RET_DV
ROPE_BASE = 10000.0
N_EXPERTS, TOP_K, D_FF = 32, 4, 1024
SWIGLU_LIMIT, SWIGLU_ALPHA = 7.0, 1.702
MOE_BLOCK = 512
TOKEN_TILE = 256
ADA_TILE = 768
VMEM_LIMIT = 56 * 1024 * 1024
SC_CORES, SC_SUBCORES = 2, 16
SC_WORKERS = SC_CORES * SC_SUBCORES
SC_CHUNK = 64


def _params(*sem):
    return pltpu.CompilerParams(dimension_semantics=sem, vmem_limit_bytes=VMEM_LIMIT)


def _dot(a, b):
    return jnp.dot(a, b, preferred_element_type=F32)


def _dot_nt(a, b):
    return lax.dot_general(a, b, (((1,), (1,)), ((), ())), preferred_element_type=F32)


def _dot_tn(a, b):
    return lax.dot_general(a, b, (((0,), (0,)), ((), ())), preferred_element_type=F32)


def _split(a):
    hi = a.astype(BF16)
    return hi, (a - hi.astype(F32)).astype(BF16)


def _dot3(a, b, dot=_dot):
    ah, al = _split(a)
    bh, bl = _split(b)
    return dot(ah, bh) + (dot(ah, bl) + dot(al, bh))


def _pack_rows(x):
    h = x.shape[1] // 2
    lo = lax.bitcast_convert_type(x[:, 0:h].astype(BF16).astype(F32), U32)
    hi = lax.bitcast_convert_type(x[:, h:2 * h].astype(BF16).astype(F32), U32)
    return hi | (lo >> 16)


def _unpack_rows(p):
    lo = lax.bitcast_convert_type(p << 16, F32)
    hi = lax.bitcast_convert_type(p & jnp.uint32(0xFFFF0000), F32)
    return lo, hi


def _silu(x):
    return x * jax.nn.sigmoid(x)


def _norm_mod(x, g, shift, scale):
    r = lax.rsqrt(jnp.mean(x * x, axis=-1, keepdims=True) + EPS)
    return (x * r * g) * (1.0 + scale) + shift


def _const_spec(shape):
    nd = len(shape)
    return pl.BlockSpec(shape, lambda *_: (0,) * nd)


def _ada_body(c_ref, w_ref, b_ref, o_ref):
    o_ref[0] = _dot3(_silu(c_ref[...]), w_ref[0]) + b_ref[0]


def _ada_mod(cvec, ada_w, ada_b):
    depth, d, n6 = ada_w.shape
    rows = cvec.shape[0]
    return pl.pallas_call(
        _ada_body,
        grid=(depth, n6 // ADA_TILE),
        in_specs=[_const_spec((rows, d)),
                  pl.BlockSpec((1, d, ADA_TILE), lambda l, j: (l, 0, j)),
                  pl.BlockSpec((1, 1, ADA_TILE), lambda l, j: (l, 0, j))],
        out_specs=pl.BlockSpec((1, rows, ADA_TILE), lambda l, j: (l, 0, j)),
        out_shape=jax.ShapeDtypeStruct((depth, rows, n6), F32),
        compiler_params=_params("arbitrary", "arbitrary"),
        name="ada_mod",
    )(cvec, ada_w, ada_b.reshape(depth, 1, n6))


def _tile_specs(nct, d):
    x_spec = pl.BlockSpec((1, TOKEN_TILE, d), lambda b, i: (b, i, 0))
    mod_spec = pl.BlockSpec((1, 1, 6, d), lambda b, i: (b, (i >= nct).astype(I32), 0, 0))
    return x_spec, mod_spec


def _inproj0_body(x_ref, mod_ref, g_ref, wq, wk, wv, wg, wlow, wu, wa_ref, ba_ref, tri_ref, ones_ref,
                  qd_f, ki_f, ks_f, ed_f, qd_b, ki_b, ks_b, ed_b, ov, og, ou):
    m = mod_ref[0, 0]
    h = _norm_mod(x_ref[0], g_ref[...], m[0:1], m[1:2]).astype(BF16)
    ov[0] = _dot(h, wv[...]).astype(ov.dtype)
    og[0] = _dot(h, wg[...]).astype(og.dtype)
    u = _dot(h, wu[...])
    for qb in range(ou.shape[0]):
        ou[qb, 0] = u[:, qb * 128:(qb + 1) * 128]
    q = _dot(h, wq[...]) * (GLA_DK ** -0.5)
    k = _dot(h, wk[...])
    low = _dot(h, wlow[...])
    outs = ((qd_f, ki_f, ks_f, ed_f), (qd_b, ki_b, ks_b, ed_b))
    for d, (qd_ref, ki_ref, ks_ref, ed_ref) in enumerate(outs):
        z = _dot3(low, wa_ref[d]) + ba_ref[d]
        log_a = (jnp.minimum(z, 0.0) - jnp.log1p(jnp.exp(-jnp.abs(z)))) * (1.0 / GLA_TAU)
        la_hi, la_lo = _split(log_a)
        cum = _dot(tri_ref[d], la_hi) + _dot(tri_ref[d], la_lo)
        tot = _dot(ones_ref[...], la_hi) + _dot(ones_ref[...], la_lo)
        qd_ref[0] = (q * jnp.exp(cum)).astype(BF16)
        ki_ref[0] = (k * jnp.exp(-cum)).astype(BF16)
        ks_ref[0] = (k * jnp.exp(tot - cum)).astype(BF16)
        for ch in range(TOKEN_TILE // GLA_CHUNK):
            ed_ref[0, ch] = jnp.exp(tot[ch * GLA_CHUNK:ch * GLA_CHUNK + 1])


def _inproj0(x_all, mods, norm_g, weights, wa_pad, ba, nct):
    bsz, t, d = x_all.shape
    tm = TOKEN_TILE
    x_spec, mod_spec = _tile_specs(nct, d)
    pos = jnp.arange(tm)
    same_chunk = (pos[:, None] // GLA_CHUNK) == (pos[None, :] // GLA_CHUNK)
    tri = jnp.stack([same_chunk & (pos[None, :] <= pos[:, None]),
                     same_chunk & (pos[None, :] >= pos[:, None])]).astype(BF16)
    ones = same_chunk.astype(BF16)
    consts = list(weights) + [wa_pad, ba, tri, ones]
    tok = lambda w, dt: (pl.BlockSpec((1, tm, w), lambda b, i: (b, i, 0)), jax.ShapeDtypeStruct((bsz, t, w), dt))
    per_chunk = (pl.BlockSpec((1, tm // GLA_CHUNK, 1, AB_QK), lambda b, i: (b, i, 0, 0)),
                 jax.ShapeDtypeStruct((bsz, t // GLA_CHUNK, 1, AB_QK), F32))
    one_dir = [tok(AB_QK, BF16)] * 3 + [per_chunk]
    u_blocks = (pl.BlockSpec((S5_CH // 128, 1, tm, 128), lambda b, i: (0, b, i, 0)),
                jax.ShapeDtypeStruct((S5_CH // 128, bsz, t, 128), F32))
    outs = one_dir + one_dir + [tok(AB_V, BF16), tok(AB_V, BF16), u_blocks]
    return pl.pallas_call(
        _inproj0_body,
        grid=(bsz, t // tm),
        in_specs=[x_spec, mod_spec, _const_spec((1, d))] + [_const_spec(a.shape) for a in consts],
        out_specs=[o[0] for o in outs],
        out_shape=[o[1] for o in outs],
        compiler_params=_params("arbitrary", "arbitrary"),
        name="inproj_gla_s5",
    )(x_all, mods, norm_g.reshape(1, d), *consts)


def _rope(acc, cos_ref, sin_ref, o_ref, scale):
    for grp in range(acc.shape[1] // 128):
        half = grp % 2
        xg = acc[:, grp * 128:(grp + 1) * 128]
        cs = cos_ref[:, half * 128:(half + 1) * 128]
        sn = sin_ref[:, half * 128:(half + 1) * 128]
        out = xg * cs + pltpu.roll(xg, 64, 1) * sn
        o_ref[0, :, grp * 128:(grp + 1) * 128] = (out * scale).astype(o_ref.dtype)


def _inproj1_body(x_ref, mod_ref, g_ref, cos_ref, sin_ref, wq, wk, wv, wg, oq, ok, ov, og):
    m = mod_ref[0, 0]
    h = _norm_mod(x_ref[0], g_ref[...], m[0:1], m[1:2]).astype(BF16)
    _rope(_dot(h, wq[...]), cos_ref, sin_ref, oq, 1.0)
    _rope(_dot(h, wk[...]), cos_ref, sin_ref, ok, RET_DK ** -0.5)
    ov[0] = _dot(h, wv[...]).astype(ov.dtype)
    og[0] = _dot(h, wg[...]).astype(og.dtype)


def _inproj1(x_all, mods, norm_g, cos_t, sin_t, weights, nct):
    bsz, t, d = x_all.shape
    x_spec, mod_spec = _tile_specs(nct, d)
    tab_spec = pl.BlockSpec((TOKEN_TILE, RET_DK), lambda b, i: (i, 0))
    return pl.pallas_call(
        _inproj1_body,
        grid=(bsz, t // TOKEN_TILE),
        in_specs=[x_spec, mod_spec, _const_spec((1, d)), tab_spec, tab_spec] + [_const_spec(w.shape) for w in weights],
        out_specs=[pl.BlockSpec((1, TOKEN_TILE, w.shape[1]), lambda b, i: (b, i, 0)) for w in weights],
        out_shape=[jax.ShapeDtypeStruct((bsz, t, w.shape[1]), BF16) for w in weights],
        compiler_params=_params("arbitrary", "arbitrary"),
        name="inproj_retention",
    )(x_all, mods, norm_g.reshape(1, d), cos_t, sin_t, *weights)


def _backward_chunk(n, n_ctx_chunks, n_chunks):
    return jnp.where(n < n_ctx_chunks, n_ctx_chunks - 1 - n, n_chunks - 1 - (n - n_ctx_chunks))


def _gla_body(qd_f, ki_f, ks_f, ed_f, v_f, qd_b, ki_b, ks_b, ed_b, v_b, hmask_ref, bdmask_ref, o_f, o_b, st_f, st_b):
    c = GLA_CHUNK

    @pl.when(pl.program_id(1) == 0)
    def _():
        st_f[...] = jnp.zeros_like(st_f)
        st_b[...] = jnp.zeros_like(st_b)

    r4 = lax.broadcasted_iota(I32, (GLA_HEADS * c, c), 0) & (c - 1)
    c4 = lax.broadcasted_iota(I32, (GLA_HEADS * c, c), 1)
    dirs = ((qd_f, ki_f, ks_f, ed_f, v_f, o_f, st_f), (qd_b, ki_b, ks_b, ed_b, v_b, o_b, st_b))
    chains = [(bb, d) + dirs[d] for bb in range(qd_f.shape[0]) for d in range(2)]
    scores, inter, grow = [], [], []
    for bb, d, qd_ref, ki_ref, ks_ref, ed_ref, v_ref, o_ref, st_ref in chains:
        q_dec = qd_ref[bb]
        q_heads = jnp.concatenate([q_dec] * GLA_HEADS, axis=0) * hmask_ref[...]
        seen4 = (c4 <= r4) if d == 0 else (c4 >= r4)
        scores.append(jnp.where(seen4, _dot_nt(q_heads, ki_ref[bb]), 0.0).astype(BF16))
        inter.append(_dot_nt(q_dec, st_ref[bb].astype(BF16)))
        grow.append(_dot_tn(v_ref[bb], ks_ref[bb]))
    for (bb, d, qd_ref, ki_ref, ks_ref, ed_ref, v_ref, o_ref, st_ref), sc, o_inter, dst in zip(chains, scores, inter, grow):
        v = v_ref[bb]
        o_intra = jnp.concatenate(
            [_dot(sc[h * c:(h + 1) * c], v[:, h * GLA_DV:(h + 1) * GLA_DV]) for h in range(GLA_HEADS)], axis=1)
        o_ref[bb] = o_intra + o_inter
        st_ref[bb] = st_ref[bb] * ed_ref[bb, 0] + bdmask_ref[...] * dst


def _gla(per_dir, v, n_ctx):
    bsz, t, _ = v.shape
    nc, ncc = t // GLA_CHUNK, n_ctx // GLA_CHUNK
    gb = GLA_BATCH
    fwd = lambda b, n: (b, n, 0)
    bwd = lambda b, n: (b, _backward_chunk(n, ncc, nc), 0)
    hmask = (jnp.arange(AB_QK)[:, None] // GLA_CHUNK == jnp.arange(AB_QK)[None, :] // GLA_DK).astype(BF16)
    bdmask = (jnp.arange(AB_V)[:, None] // GLA_DV == jnp.arange(AB_QK)[None, :] // GLA_DK).astype(F32)

    def specs(idx):
        idx4 = lambda b, n: idx(b, n) + (0,)
        return [pl.BlockSpec((gb, GLA_CHUNK, AB_QK), idx)] * 3 + [pl.BlockSpec((gb, 1, 1, AB_QK), idx4),
                                                                  pl.BlockSpec((gb, GLA_CHUNK, AB_V), idx)]

    return pl.pallas_call(
        _gla_body,
        grid=(bsz // gb, nc),
        in_specs=specs(fwd) + specs(bwd) + [_const_spec(hmask.shape), _const_spec(bdmask.shape)],
        out_specs=[pl.BlockSpec((gb, GLA_CHUNK, AB_V), fwd), pl.BlockSpec((gb, GLA_CHUNK, AB_V), bwd)],
        out_shape=[jax.ShapeDtypeStruct((bsz, t, AB_V), F32)] * 2,
        scratch_shapes=[pltpu.VMEM((gb, AB_V, AB_QK), F32)] * 2,
        compiler_params=_params("arbitrary", "arbitrary"),
        name="gla_scan",
    )(*per_dir[0], v, *per_dir[1], v, hmask, bdmask)


def _s5_operators(lam_re, lam_im, log_step, b_re, b_im, c_re, c_im):
    hp = lax.Precision.HIGHEST
    ln = S5_CHUNK
    step = jnp.exp(log_step.astype(F32))[..., None]
    lam_re, lam_im = lam_re.astype(F32), lam_im.astype(F32)
    mag = jnp.exp(lam_re * step)
    a_re, a_im = mag * jnp.cos(lam_im * step), mag * jnp.sin(lam_im * step)
    den = lam_re * lam_re + lam_im * lam_im
    f_re = ((a_re - 1.0) * lam_re + a_im * lam_im) / den
    f_im = (a_im * lam_re - (a_re - 1.0) * lam_im) / den
    bb_re = f_re[..., None] * b_re - f_im[..., None] * b_im
    bb_im = f_re[..., None] * b_im + f_im[..., None] * b_re
    pw_re, pw_im = [jnp.ones_like(a_re)], [jnp.zeros_like(a_im)]
    for _ in range(ln):
        pr, pi = pw_re[-1], pw_im[-1]
        pw_re.append(pr * a_re - pi * a_im)
        pw_im.append(pr * a_im + pi * a_re)
    pw_re, pw_im = jnp.stack(pw_re, 1), jnp.stack(pw_im, 1)
    ca_re = c_re[:, None] * pw_re[:, :, :, None, :] - c_im[:, None] * pw_im[:, :, :, None, :]
    ca_im = c_re[:, None] * pw_im[:, :, :, None, :] + c_im[:, None] * pw_re[:, :, :, None, :]
    kern = (jnp.einsum('dtgcp,dgpe->dtgec', ca_re[:, :ln], bb_re, precision=hp)
            - jnp.einsum('dtgcp,dgpe->dtgec', ca_im[:, :ln], bb_im, precision=hp))
    pos = jnp.arange(ln)
    i_out, j_in = pos[None, :], pos[:, None]
    tz, wx, wy, ac = [], [], [], []
    for d in range(2):
        lag = (i_out - j_in) if d == 0 else (j_in - i_out)
        blk = jnp.where((lag >= 0)[:, :, None, None, None], kern[d][jnp.clip(lag, 0, ln - 1)], 0.0)
        tz.append(blk.transpose(2, 0, 3, 1, 4).reshape(S5_GROUPS, ln * S5_GROUP, ln * S5_GROUP))
        p_in = (ln - 1 - pos) if d == 0 else pos
        ar, ai = pw_re[d][p_in], pw_im[d][p_in]
        x_re = ar[..., None] * bb_re[d][None] - ai[..., None] * bb_im[d][None]
        x_im = ar[..., None] * bb_im[d][None] + ai[..., None] * bb_re[d][None]
        to_rows = lambda m: m.transpose(1, 0, 3, 2).reshape(S5_GROUPS, ln * S5_GROUP, S5_P)
        wx.append(jnp.concatenate([to_rows(x_re), to_rows(x_im), to_rows(x_im), to_rows(x_re)], axis=-1))
        p_out = (pos + 1) if d == 0 else (ln - pos)
        to_cols = lambda m: m.transpose(1, 3, 0, 2).reshape(S5_GROUPS, S5_P, ln * S5_GROUP)
        wy.append(jnp.concatenate([to_cols(ca_re[d][p_out]), -to_cols(ca_im[d][p_out])], axis=1))
        lr, li = pw_re[d][ln], pw_im[d][ln]
        rows = [jnp.concatenate([lr, lr], -1), jnp.concatenate([-li, li], -1), jnp.concatenate([li, -li], -1)]
        ac.append(jnp.stack(rows + [jnp.zeros_like(rows[0])] * 5, axis=1))
    return (jnp.stack(tz).astype(BF16), jnp.stack(wx).astype(BF16), jnp.stack(wy).astype(BF16), jnp.stack(ac))


def _s5_placement(pall):
    rows, cols = pall.shape[1], pall.shape[2]
    row = lax.broadcasted_iota(I32, (rows, cols), 0)
    col = lax.broadcasted_iota(I32, (rows, cols), 1)
    same_token = (row >> 7) == (col >> 4)
    for g8 in range(pall.shape[0]):
        pall[g8] = jnp.where(same_token & ((row & 127) == g8 * S5_GROUP + (col & (S5_GROUP - 1))), 1.0, 0.0).astype(BF16)


def _first_step():
    return (pl.program_id(0) == 0) & (pl.program_id(1) == 0)


def _s5_fold_body(ncs, u_ref, o_ref, pall, ucat):
    @pl.when(_first_step())
    def _():
        _s5_placement(pall)

    for b in range(u_ref.shape[1]):
        for j in range(S5_CHUNK):
            ucat[b * ncs:(b + 1) * ncs, j * 128:(j + 1) * 128] = u_ref[0, b, pl.ds(j, ncs, stride=S5_CHUNK), :].astype(BF16)
    for g8 in range(pall.shape[0]):
        o_ref[g8] = _dot(ucat[...], pall[g8]).astype(BF16)


def _s5_unfold_body(ncs, y_ref, o_ref, pall):
    @pl.when(_first_step())
    def _():
        _s5_placement(pall)

    def token(i, carry):
        r0 = pl.multiple_of(i * 128, 128)
        acc = _dot_nt(y_ref[0], pall[0, pl.ds(r0, 128), :])
        for g8 in range(1, pall.shape[0]):
            acc = acc + _dot_nt(y_ref[g8], pall[g8, pl.ds(r0, 128), :])
        for b in range(o_ref.shape[1]):
            o_ref[0, b, pl.ds(i, ncs, stride=S5_CHUNK), :] = acc[b * ncs:(b + 1) * ncs]
        return carry

    lax.fori_loop(0, S5_CHUNK, token, 0)


def _s5_body(ncs_ctx, ncs, rows, u_ref, tz_ref, wx_ref, wy_ref, ac_ref, y_ref, xx_f, xs_f, xx_b, xs_b, sin_f, sin_b):
    u = u_ref[0]
    half = 2 * S5_P
    for d, (xx, xs) in enumerate(((xx_f, xs_f), (xx_b, xs_b))):
        r = _dot(u, wx_ref[d, 0])
        xx[...] = r[:, :half]
        xs[...] = r[:, half:]
    ac_f, ac_b = ac_ref[0, 0], ac_ref[1, 0]

    def advance(ac, s, s_sw, x, x_sw):
        return ac[0:1] * s + ac[1:2] * s_sw + x, ac[0:1] * s_sw + ac[2:3] * s + x_sw

    def step(n, carry):
        s_f, sw_f, s_b, sw_b = carry
        at_f = pl.ds(n, rows, stride=ncs)
        at_b = pl.ds(_backward_chunk(n, ncs_ctx, ncs), rows, stride=ncs)
        sin_f[at_f, :] = s_f
        sin_b[at_b, :] = s_b
        s_f, sw_f = advance(ac_f, s_f, sw_f, xx_f[at_f, :], xs_f[at_f, :])
        s_b, sw_b = advance(ac_b, s_b, sw_b, xx_b[at_b, :], xs_b[at_b, :])
        return s_f, sw_f, s_b, sw_b

    zero = jnp.zeros((rows, half), F32)
    lax.fori_loop(0, ncs, step, (zero, zero, zero, zero))
    y_ref[0] = (_dot(u, tz_ref[0, 0]) + _dot(u, tz_ref[1, 0])
                + _dot(sin_f[...].astype(BF16), wy_ref[0, 0]) + _dot(sin_b[...].astype(BF16), wy_ref[1, 0])).astype(BF16)


def _s5(u4, ops, n_ctx):
    tz, wx, wy, ac = ops
    nq, bsz, t, _ = u4.shape
    ln, lanes = S5_CHUNK, S5_CHUNK * S5_GROUP
    gq = S5_GROUPS // nq
    ncs, ncs_ctx = t // ln, n_ctx // ln
    m = ncs * bsz
    hb = S5_FOLD_BATCH
    tok_spec = pl.BlockSpec((1, hb, t, 128), lambda q, h: (q, h, 0, 0))
    grp_spec = pl.BlockSpec((gq, hb * ncs, lanes), lambda q, h: (q, h, 0))
    pall = pltpu.VMEM((gq, ln * 128, lanes), BF16)
    ug = pl.pallas_call(
        functools.partial(_s5_fold_body, ncs),
        grid=(nq, bsz // hb),
        in_specs=[tok_spec],
        out_specs=grp_spec,
        out_shape=jax.ShapeDtypeStruct((S5_GROUPS, m, lanes), BF16),
        scratch_shapes=[pall, pltpu.VMEM((hb * ncs, ln * 128), BF16)],
        compiler_params=_params("arbitrary", "arbitrary"),
        name="s5_fold",
    )(u4)
    dir_spec = lambda shape: pl.BlockSpec((2, 1) + shape, lambda g: (0, g, 0, 0))
    yg = pl.pallas_call(
        functools.partial(_s5_body, ncs_ctx, ncs, bsz),
        grid=(S5_GROUPS,),
        in_specs=[pl.BlockSpec((1, m, lanes), lambda g: (g, 0, 0)), dir_spec((lanes, lanes)),
                  dir_spec((lanes, 4 * S5_P)), dir_spec((2 * S5_P, lanes)), dir_spec((8, 2 * S5_P))],
        out_specs=pl.BlockSpec((1, m, lanes), lambda g: (g, 0, 0)),
        out_shape=jax.ShapeDtypeStruct((S5_GROUPS, m, lanes), BF16),
        scratch_shapes=[pltpu.VMEM((m, 2 * S5_P), F32)] * 6,
        compiler_params=_params("arbitrary"),
        name="s5_scan",
    )(ug, tz, wx, wy, ac)
    return pl.pallas_call(
        functools.partial(_s5_unfold_body, ncs),
        grid=(nq, bsz // hb),
        in_specs=[grp_spec],
        out_specs=tok_spec,
        out_shape=jax.ShapeDtypeStruct(u4.shape, F32),
        scratch_shapes=[pall],
        compiler_params=_params("arbitrary", "arbitrary"),
        name="s5_unfold",
    )(yg)


def _ret_body(ncc, q_f, k_f, v_f, q_b, k_b, v_b, dmat_ref, rsc_ref, csc_ref, gam_ref, o_f, o_b, st_f, st_b):
    n = pl.program_id(1)

    @pl.when(n == 0)
    def _():
        st_f[...] = jnp.zeros_like(st_f)
        st_b[...] = jnp.zeros_like(st_b)

    dirs = ((q_f, k_f, v_f, o_f, st_f), (q_b, k_b, v_b, o_b, st_b))

    @pl.when(n >= ncc)
    def _():
        for d, (q_ref, k_ref, v_ref, o_ref, st_ref) in enumerate(dirs):
            for h in range(RET_HEADS):
                qh = q_ref[0, :, h * RET_DK:(h + 1) * RET_DK]
                kh = k_ref[0, :, h * RET_DK:(h + 1) * RET_DK]
                vh = v_ref[0, :, h * RET_DV:(h + 1) * RET_DV]
                scores = (_dot_nt(qh, kh) * dmat_ref[d, h]).astype(BF16)
                o = _dot(scores, vh) + rsc_ref[d, h] * _dot(qh, st_ref[h].astype(BF16))
                o_ref[0, :, h * RET_DV:(h + 1) * RET_DV] = o.astype(o_ref.dtype)

    for d, (q_ref, k_ref, v_ref, o_ref, st_ref) in enumerate(dirs):
        for h in range(RET_HEADS):
            kh = k_ref[0, :, h * RET_DK:(h + 1) * RET_DK]
            vh = v_ref[0, :, h * RET_DV:(h + 1) * RET_DV]
            k_state = (kh.astype(F32) * csc_ref[d, h]).astype(BF16)
            st_ref[h] = st_ref[h] * gam_ref[d, h] + _dot_tn(k_state, vh)


def _retention(q, k, v, decay_logit, n_ctx):
    bsz, t, _ = q.shape
    c = RET_CHUNK
    nc, ncc = t // c, n_ctx // c
    nl = nc - ncc
    log_gamma = jax.nn.log_sigmoid(decay_logit.astype(F32))[:, :, None, None]
    i = jnp.arange(c, dtype=F32)
    lag = i[:, None] - i[None, :]
    lag = jnp.stack([lag, -lag])[:, None]
    dmat = jnp.where(lag >= 0, jnp.exp(log_gamma * jnp.maximum(lag, 0.0)), 0.0)
    done = jnp.stack([i + 1.0, c - i])[:, None, :, None]
    rsc = jnp.exp(log_gamma * done)
    csc = jnp.exp(log_gamma * (c - done))
    gam = jnp.exp(log_gamma[:, :, 0, 0] * c)
    fwd = lambda b, n: (b, n, 0)
    bwd = lambda b, n: (b, _backward_chunk(n, ncc, nc), 0)
    o_fwd = lambda b, n: (b, jnp.maximum(n - ncc, 0), 0)
    o_bwd = lambda b, n: (b, nl - 1 - jnp.maximum(n - ncc, 0), 0)

    def specs(idx):
        return [pl.BlockSpec((1, c, RET_QK), idx), pl.BlockSpec((1, c, RET_QK), idx), pl.BlockSpec((1, c, RET_MIX), idx)]

    return pl.pallas_call(
        functools.partial(_ret_body, ncc),
        grid=(bsz, nc),
        in_specs=specs(fwd) + specs(bwd) + [_const_spec(dmat.shape), _const_spec(rsc.shape), _const_spec(csc.shape),
                                            pl.BlockSpec(memory_space=pltpu.SMEM)],
        out_specs=[pl.BlockSpec((1, c, RET_MIX), o_fwd), pl.BlockSpec((1, c, RET_MIX), o_bwd)],
        out_shape=[jax.ShapeDtypeStruct((bsz, nl * c, RET_MIX), BF16)] * 2,
        scratch_shapes=[pltpu.VMEM((RET_HEADS, RET_DK, RET_DV), F32)] * 2,
        compiler_params=_params("arbitrary", "arbitrary"),
        name="retention_scan",
    )(q, k, v, q, k, v, dmat, rsc, csc, gam)


def _route(x, mixed, mod, n2g_ref, wr_ref, br_ref, x1_ref, h2_ref, e_ref, w_ref, r_ref, cnt_ref):
    tm = x.shape[0]
    x1 = x + mod[2:3] * mixed
    x1_ref[0] = x1
    h2 = _norm_mod(x1, n2g_ref[...], mod[3:4], mod[4:5])
    h2_ref[0] = _pack_rows(h2)
    logits = _dot3(wr_ref[...], h2, dot=_dot_nt) + br_ref[...]
    ie = lax.broadcasted_iota(I32, logits.shape, 0)
    tops, picks = [], []
    for _ in range(TOP_K):
        mx = jnp.max(logits, axis=0, keepdims=True)
        pick = jnp.min(jnp.where(logits == mx, ie, N_EXPERTS), axis=0, keepdims=True)
        tops.append(mx)
        picks.append(pick)
        logits = jnp.where(ie == pick, -jnp.inf, logits)
    ex = [jnp.exp(tk - tops[0]) for tk in tops]
    den = ex[0] + ex[1] + ex[2] + ex[3]
    for kk in range(TOP_K):
        w_ref[0, kk:kk + 1, :] = ex[kk] / den
        e_ref[0, kk:kk + 1, :] = picks[kk]

    @pl.when((pl.program_id(0) == 0) & (pl.program_id(1) == 0))
    def _():
        cnt_ref[...] = jnp.zeros_like(cnt_ref)

    earlier = (lax.broadcasted_iota(I32, (tm, tm), 0) < lax.broadcasted_iota(I32, (tm, tm), 1))
    earlier = jnp.where(earlier, 1.0, 0.0).astype(BF16)
    run = cnt_ref[:, 0:1]
    for kk, pick in enumerate(picks):
        onehot = jnp.where(ie == pick, 1.0, 0.0)
        before = _dot(onehot.astype(BF16), earlier) + run
        r_ref[0, kk:kk + 1, :] = jnp.sum(onehot * before, axis=0, keepdims=True).astype(I32)
        run = run + jnp.sum(onehot, axis=1, keepdims=True)
    cnt_ref[...] = jnp.broadcast_to(run, cnt_ref.shape)


def _mix0_body(x_ref, mod_ref, of_ref, ob_ref, g_ref, ys_ref, u_ref, gng_ref, dsk_ref, gluw_ref, glub_ref,
               wo_ref, n2g_ref, wr_ref, br_ref, x1_ref, h2_ref, e_ref, w_ref, r_ref, cnt_ref):
    o = of_ref[0] + ob_ref[0]
    heads = []
    for h in range(GLA_HEADS):
        oh = o[:, h * GLA_DV:(h + 1) * GLA_DV]
        heads.append(oh * lax.rsqrt(jnp.mean(oh * oh, axis=-1, keepdims=True) + EPS))
    gla = jnp.concatenate(heads, axis=1) * gng_ref[...] * _silu(g_ref[0].astype(F32))
    lane_blocks = lambda ref: jnp.concatenate([ref[qb, 0] for qb in range(ref.shape[0])], axis=1)
    y = jax.nn.gelu(lane_blocks(ys_ref) + dsk_ref[...] * lane_blocks(u_ref))
    y = y * jax.nn.sigmoid(_dot(y.astype(BF16), gluw_ref[...]) + glub_ref[...])
    mixed = _dot(gla.astype(BF16), wo_ref[0:AB_V]) + _dot(y.astype(BF16), wo_ref[AB_V:AB_V + S5_CH])
    _route(x_ref[0], mixed, mod_ref[0, 0], n2g_ref, wr_ref, br_ref, x1_ref, h2_ref, e_ref, w_ref, r_ref, cnt_ref)


def _mix1_body(x_ref, mod_ref, of_ref, ob_ref, g_ref, ng_ref, wo_ref, n2g_ref, wr_ref, br_ref,
               x1_ref, h2_ref, e_ref, w_ref, r_ref, cnt_ref):
    mixed = None
    for h in range(RET_HEADS):
        sl = slice(h * RET_DV, (h + 1) * RET_DV)
        oh = of_ref[0, :, sl].astype(F32) + ob_ref[0, :, sl].astype(F32)
        mu = jnp.mean(oh, axis=-1, keepdims=True)
        cen = oh - mu
        var = jnp.mean(cen * cen, axis=-1, keepdims=True)
        gated = cen * lax.rsqrt(var + EPS) * ng_ref[:, sl] * _silu(g_ref[0, :, sl].astype(F32))
        part = _dot(gated.astype(BF16), wo_ref[sl])
        mixed = part if mixed is None else mixed + part
    _route(x_ref[0], mixed, mod_ref[0, 0], n2g_ref, wr_ref, br_ref, x1_ref, h2_ref, e_ref, w_ref, r_ref, cnt_ref)


def _mix_call(body, name, x_all, mods, tiles, acts, consts, norm2_g, w_router, b_router, n_tok, seg_tile0):
    bsz, _, d = x_all.shape
    tm = TOKEN_TILE
    off = lambda b, i: (b, i + seg_tile0, 0)
    loc = lambda b, i: (b, i, 0)
    ntl = bsz * tiles
    flat = lambda b, i: (b * tiles + i, 0, 0)
    in_specs = [pl.BlockSpec((1, tm, d), off),
                pl.BlockSpec((1, 1, 6, d), lambda b, i: (b, ((i + seg_tile0) >= n_tok).astype(I32), 0, 0))]
    args = [x_all, mods]
    for arr, offset in acts:
        if arr.ndim == 4:
            in_specs.append(pl.BlockSpec((arr.shape[0], 1, tm, arr.shape[3]), lambda b, i: (0, b, i, 0)))
        else:
            in_specs.append(pl.BlockSpec((1, tm, arr.shape[2]), off if offset else loc))
        args.append(arr)
    tail = list(consts) + [norm2_g.reshape(1, d), w_router.T, b_router.reshape(N_EXPERTS, 1)]
    in_specs += [_const_spec(a.shape) for a in tail]
    args += tail
    tok_out = pl.BlockSpec((1, TOP_K, tm), flat)
    return pl.pallas_call(
        body,
        grid=(bsz, tiles),
        in_specs=in_specs,
        out_specs=[pl.BlockSpec((1, tm, d), loc), pl.BlockSpec((1, tm, d // 2), loc), tok_out, tok_out, tok_out,
                   _const_spec((N_EXPERTS, 128))],
        out_shape=[jax.ShapeDtypeStruct((bsz, tiles * tm, d), F32), jax.ShapeDtypeStruct((bsz, tiles * tm, d // 2), U32),
                   jax.ShapeDtypeStruct((ntl, TOP_K, tm), I32), jax.ShapeDtypeStruct((ntl, TOP_K, tm), F32),
                   jax.ShapeDtypeStruct((ntl, TOP_K, tm), I32), jax.ShapeDtypeStruct((N_EXPERTS, 128), F32)],
        compiler_params=_params("arbitrary", "arbitrary"),
        name=name,
    )(*args)


def _cast_rows(src_ref, dst_ref, rows):
    def chunk(j, carry):
        r = pl.multiple_of(j * rows, rows)
        dst_ref[pl.ds(r, rows), :] = src_ref[0, 0, pl.ds(r, rows), :].astype(BF16)
        return carry

    lax.fori_loop(0, dst_ref.shape[0] // rows, chunk, 0)


def _expert_body(be_ref, nu_ref, x_ref, wgu_ref, bgu_ref, wd_ref, bd_ref, o_ref, wgu_bf, wd_bf):
    i = pl.program_id(0)
    live = i < nu_ref[0]
    new_expert = (i == 0) | (be_ref[i] != be_ref[jnp.maximum(i - 1, 0)])

    @pl.when(live & new_expert)
    def _():
        _cast_rows(wgu_ref, wgu_bf, 128)
        _cast_rows(wd_ref, wd_bf, 128)

    @pl.when(live)
    def _():
        x_lo, x_hi = _unpack_rows(x_ref[...])
        half = x_lo.shape[1]
        gu = (_dot(x_lo.astype(BF16), wgu_bf[0:half]) + _dot(x_hi.astype(BF16), wgu_bf[half:2 * half])
              + bgu_ref[0, 0])
        gate = jnp.minimum(gu[:, :D_FF], SWIGLU_LIMIT)
        lin = jnp.clip(gu[:, D_FF:], -SWIGLU_LIMIT, SWIGLU_LIMIT)
        act = gate * jax.nn.sigmoid(SWIGLU_ALPHA * gate) * (lin + 1.0)
        y = _dot(act.astype(BF16), wd_bf[...]) + bd_ref[0, 0]
        o_ref[...] = _pack_rows(y)

    @pl.when(i >= nu_ref[0])
    def _():
        o_ref[...] = jnp.zeros_like(o_ref)


def _experts(xb, block_e, n_used, layer, w_gu, b_gu, w_down, b_down):
    n_slots, half = xb.shape
    d = 2 * half
    n_blocks = n_slots // MOE_BLOCK
    depth = w_gu.shape[0]
    by_expert = lambda i, be, nu: (layer, be[i], 0, 0)
    return pl.pallas_call(
        _expert_body,
        grid_spec=pltpu.PrefetchScalarGridSpec(
            num_scalar_prefetch=2,
            grid=(n_blocks,),
            in_specs=[pl.BlockSpec((MOE_BLOCK, half), lambda i, be, nu: (i, 0)),
                      pl.BlockSpec((1, 1, d, 2 * D_FF), by_expert), pl.BlockSpec((1, 1, 1, 2 * D_FF), by_expert),
                      pl.BlockSpec((1, 1, D_FF, d), by_expert), pl.BlockSpec((1, 1, 1, d), by_expert)],
            out_specs=pl.BlockSpec((MOE_BLOCK, half), lambda i, be, nu: (i, 0)),
            scratch_shapes=[pltpu.VMEM((d, 2 * D_FF), BF16), pltpu.VMEM((D_FF, d), BF16)]),
        out_shape=jax.ShapeDtypeStruct((n_slots, half), U32),
        compiler_params=_params("arbitrary"),
        name="moe_experts",
    )(block_e, n_used, xb, w_gu, b_gu.reshape(depth, N_EXPERTS, 1, 2 * D_FF), w_down,
      b_down.reshape(depth, N_EXPERTS, 1, d))


def _combine_body(final, x1_ref, mod_ref, yk_ref, w_ref, *refs):
    d = x1_ref.shape[2]
    half = d // 2
    y_lo, y_hi = None, None
    for k in range(TOP_K):
        lo, hi = _unpack_rows(yk_ref[k, 0])
        wk = w_ref[0, :, k:k + 1]
        y_lo = lo * wk if y_lo is None else y_lo + lo * wk
        y_hi = hi * wk if y_hi is None else y_hi + hi * wk
    g2 = mod_ref[0, 0][5:6]
    x2_lo = x1_ref[0, :, 0:half] + g2[:, 0:half] * y_lo
    x2_hi = x1_ref[0, :, half:d] + g2[:, half:d] * y_hi
    if final:
        fg_ref, o_ref = refs
        ms = (jnp.sum(x2_lo * x2_lo, axis=-1, keepdims=True) + jnp.sum(x2_hi * x2_hi, axis=-1, keepdims=True)) / d
        r = lax.rsqrt(ms + EPS)
        x2_lo = x2_lo * r * fg_ref[:, 0:half]
        x2_hi = x2_hi * r * fg_ref[:, half:d]
    else:
        (o_ref,) = refs
    o_ref[0, :, 0:half] = x2_lo
    o_ref[0, :, half:d] = x2_hi


def _combine(x1, mods, yk, w_tok, seg_tile0, n_tok, final_g):
    bsz, t, d = x1.shape
    tm = TOKEN_TILE
    tiles = t // tm
    loc = lambda b, i: (b, i, 0)
    in_specs = [pl.BlockSpec((1, tm, d), loc),
                pl.BlockSpec((1, 1, 6, d), lambda b, i: (b, ((i + seg_tile0) >= n_tok).astype(I32), 0, 0)),
                pl.BlockSpec((TOP_K, 1, tm, d // 2), lambda b, i: (0, b, i, 0)),
                pl.BlockSpec((1, tm, TOP_K), loc)]
    args = [x1, mods, yk.reshape(TOP_K, bsz, t, d // 2), w_tok.reshape(bsz, t, TOP_K)]
    if final_g is not None:
        in_specs.append(_const_spec((1, d)))
        args.append(final_g.reshape(1, d))
    return pl.pallas_call(
        functools.partial(_combine_body, final_g is not None),
        grid=(bsz, tiles),
        in_specs=in_specs,
        out_specs=pl.BlockSpec((1, tm, d), loc),
        out_shape=jax.ShapeDtypeStruct((bsz, t, d), F32),
        compiler_params=_params("arbitrary", "arbitrary"),
        name="moe_combine",
    )(*args)


def _sc_mesh():
    return plsc.VectorSubcoreMesh(core_axis_name="core", subcore_axis_name="subcore",
                                  num_cores=SC_CORES, num_subcores=SC_SUBCORES)


def _sc_worker_base(per_worker):
    return (lax.axis_index("subcore") * SC_CORES + lax.axis_index("core")) * per_worker


def _sc_dispatch(rows, dest, n_slots):
    n, w = rows.shape
    per_worker = n // SC_WORKERS
    assert per_worker * SC_WORKERS == n and per_worker % SC_CHUNK == 0

    @functools.partial(
        pl.kernel, mesh=_sc_mesh(), out_type=jax.ShapeDtypeStruct((n_slots, w), rows.dtype),
        scratch_types=[pltpu.VMEM((SC_CHUNK,), I32)] * TOP_K + [pltpu.VMEM((SC_CHUNK, w), rows.dtype),
                                                                pltpu.SemaphoreType.DMA],
        name="moe_dispatch")
    def scatter_rows(rows_hbm, dest_hbm, out_hbm, *scratch):
        idx_refs, buf, sem = scratch[:TOP_K], scratch[TOP_K], scratch[TOP_K + 1]
        base0 = _sc_worker_base(per_worker)

        @pl.loop(0, per_worker // SC_CHUNK)
        def _(j):
            base = base0 + j * SC_CHUNK
            pltpu.sync_copy(rows_hbm.at[pl.ds(base, SC_CHUNK)], buf)
            for k, idx in enumerate(idx_refs):
                pltpu.sync_copy(dest_hbm.at[pl.ds(k * n + base, SC_CHUNK)], idx)
            copies = [pltpu.async_copy(buf, out_hbm.at[idx], sem) for idx in idx_refs]
            for cp in copies:
                cp.wait()

    return scatter_rows(rows, dest)


def _sc_gather(table, idx):
    n = idx.shape[0]
    w = table.shape[1]
    per_worker = n // SC_WORKERS
    n_chunks = per_worker // SC_CHUNK
    assert per_worker * SC_WORKERS == n and n_chunks * SC_CHUNK == per_worker and n_chunks % 2 == 0

    @functools.partial(
        pl.kernel, mesh=_sc_mesh(), out_type=jax.ShapeDtypeStruct((n, w), table.dtype),
        scratch_types=([pltpu.VMEM((SC_CHUNK,), I32)] * 2 + [pltpu.VMEM((SC_CHUNK, w), table.dtype)] * 2
                       + [pltpu.SemaphoreType.DMA] * 4),
        name="moe_gather")
    def gather_rows(table_hbm, idx_hbm, out_hbm, idx0, idx1, buf0, buf1, gsem0, gsem1, wsem0, wsem1):
        base0 = _sc_worker_base(per_worker)

        def gather_copy(idx_v, buf, sem):
            return pltpu.make_async_copy(table_hbm.at[idx_v], buf, sem)

        def write_copy(j, buf, sem):
            return pltpu.make_async_copy(buf, out_hbm.at[pl.ds(base0 + j * SC_CHUNK, SC_CHUNK)], sem)

        def start_gather(j, idx_v, buf, sem):
            pltpu.sync_copy(idx_hbm.at[pl.ds(base0 + j * SC_CHUNK, SC_CHUNK)], idx_v)
            gather_copy(idx_v, buf, sem).start()

        start_gather(0, idx0, buf0, gsem0)

        @pl.loop(0, n_chunks, step=2)
        def _(j):
            @pl.when(j > 0)
            def _():
                write_copy(j - 1, buf1, wsem1).wait()
            start_gather(j + 1, idx1, buf1, gsem1)
            gather_copy(idx0, buf0, gsem0).wait()
            write_copy(j, buf0, wsem0).start()

            @pl.when(j + 2 < n_chunks)
            def _():
                write_copy(j, buf0, wsem0).wait()
                start_gather(j + 2, idx0, buf0, gsem0)
            gather_copy(idx1, buf1, gsem1).wait()
            write_copy(j + 1, buf1, wsem1).start()

        write_copy(n_chunks - 2, buf0, wsem0).wait()
        write_copy(n_chunks - 1, buf1, wsem1).wait()

    return gather_rows(table, idx)


def _moe(h2, e_tl, w_tl, r_tl, cnt, layer, w_gu, b_gu, w_down, b_down):
    bsz, t, half = h2.shape
    n = bsz * t
    flat = lambda a: a.transpose(1, 0, 2).reshape(TOP_K, n)
    e_k, w_k, r_k = flat(e_tl), flat(w_tl), flat(r_tl)
    counts = cnt[:, 0].astype(I32)
    padded = (counts + MOE_BLOCK - 1) // MOE_BLOCK * MOE_BLOCK
    pad_end = jnp.cumsum(padded)
    pad_start = pad_end - padded
    n_blocks = (n * TOP_K + MOE_BLOCK - 1) // MOE_BLOCK + N_EXPERTS
    block_start = jnp.arange(n_blocks, dtype=I32) * MOE_BLOCK
    block_e = jnp.minimum(jnp.sum((pad_end[None, :] <= block_start[:, None]).astype(I32), axis=1), N_EXPERTS - 1)
    n_used = (pad_end[-1:] // MOE_BLOCK).astype(I32)
    start_k = jnp.sum(jnp.where(e_k[..., None] == jnp.arange(N_EXPERTS, dtype=I32), pad_start, 0), axis=-1)
    dest = (start_k + r_k).reshape(TOP_K * n)
    xb = _sc_dispatch(h2.reshape(n, half), dest, n_blocks * MOE_BLOCK)
    yb = _experts(xb, block_e, n_used, layer, w_gu, b_gu, w_down, b_down)
    return _sc_gather(yb, dest), w_k.T


def _rope_tables(n_ctx, n_lat):
    n_freq = RET_DK // 4
    inv_freq = ROPE_BASE ** (-jnp.arange(n_freq, dtype=F32) / n_freq)
    pos = jnp.arange(n_lat, dtype=I32)
    cos, sin = [], []
    for p in (pos // GRID_W, pos % GRID_W):
        ang = p.astype(F32)[:, None] * inv_freq
        cos += [jnp.cos(ang), jnp.cos(ang)]
        sin += [-jnp.sin(ang), jnp.sin(ang)]
    cos, sin = jnp.concatenate(cos, axis=1), jnp.concatenate(sin, axis=1)
    return (jnp.concatenate([jnp.ones((n_ctx, RET_DK), F32), cos], axis=0),
            jnp.concatenate([jnp.zeros((n_ctx, RET_DK), F32), sin], axis=0))


def kernel(x, c, ctx, c_ctx, ada_w, ada_b, norm1_g, norm2_g, ab_w_in, ab_w_out, gla_wa, gla_ba, gla_norm_g, s5_lam_re, s5_lam_im, s5_log_step, s5_b_re, s5_b_im, s5_c_re, s5_c_im, s5_d, s5_glu_w, s5_glu_b, ret_w_in, ret_w_out, ret_decay_logit, ret_norm_g, moe_w_router, moe_b_router, moe_w_gu, moe_b_gu, moe_w_down, moe_b_down, final_norm_g):
    bsz, n_lat, d = x.shape
    n_ctx = ctx.shape[1]
    depth = ada_w.shape[0]
    assert depth == 2 and d == D_MODEL and bsz == 8, "kernels are laid out for the stated problem shape"
    assert n_ctx % TOKEN_TILE == 0 and n_lat % TOKEN_TILE == 0 and n_lat % GRID_W == 0
    t = n_ctx + n_lat
    nct = n_ctx // TOKEN_TILE

    cvec = jnp.zeros((16, d), F32).at[:bsz].set(c).at[bsz].set(c_ctx)
    mod = _ada_mod(cvec, ada_w, ada_b).reshape(depth, 16, 6, d)
    mods = [jnp.stack([jnp.broadcast_to(mod[l, bsz], (bsz, 6, d)), mod[l, :bsz]], axis=1) for l in range(depth)]

    x_all = jnp.concatenate([ctx, x], axis=1)

    w_in = ab_w_in[0].astype(BF16)
    cuts = [0, AB_QK, 2 * AB_QK, 2 * AB_QK + AB_V, 2 * AB_QK + 2 * AB_V, 2 * AB_QK + 2 * AB_V + 2 * GLA_RANK,
            w_in.shape[1]]
    pieces = [w_in[:, a:b] for a, b in zip(cuts[:-1], cuts[1:])]
    wa_pad = jnp.zeros((2, 2 * GLA_RANK, AB_QK), F32)
    wa_pad = wa_pad.at[0, :GLA_RANK].set(gla_wa[0, 0]).at[1, GLA_RANK:].set(gla_wa[0, 1])
    outs = _inproj0(x_all, mods[0], norm1_g[0], pieces, wa_pad, gla_ba[0].reshape(2, 1, AB_QK), nct)
    v, g, u = outs[8:]
    o_f, o_b = _gla((outs[0:4], outs[4:8]), v, n_ctx)
    ops = _s5_operators(s5_lam_re[0], s5_lam_im[0], s5_log_step[0], s5_b_re[0], s5_b_im[0], s5_c_re[0], s5_c_im[0])
    ys = _s5(u, ops, n_ctx)
    consts = [jnp.tile(gla_norm_g[0], GLA_HEADS).reshape(1, AB_V), s5_d[0].reshape(1, S5_CH),
              s5_glu_w[0].astype(BF16), s5_glu_b[0].reshape(1, S5_CH), ab_w_out[0].astype(BF16)]
    x1, h2, e_tl, w_tl, r_tl, cnt = _mix_call(
        _mix0_body, "mix_gla_s5", x_all, mods[0], t // TOKEN_TILE,
        [(o_f, False), (o_b, False), (g, False), (ys, False), (u, False)], consts,
        norm2_g[0], moe_w_router[0], moe_b_router[0], nct, 0)
    yk, w_tok = _moe(h2, e_tl, w_tl, r_tl, cnt, 0, moe_w_gu, moe_b_gu, moe_w_down, moe_b_down)
    x_all = _combine(x1, mods[0], yk, w_tok, 0, nct, None)

    w_in = ret_w_in[0].astype(BF16)
    cuts = [0, RET_QK, 2 * RET_QK, 2 * RET_QK + RET_MIX, w_in.shape[1]]
    pieces = [w_in[:, a:b] for a, b in zip(cuts[:-1], cuts[1:])]
    cos_t, sin_t = _rope_tables(n_ctx, n_lat)
    q, k, v, g = _inproj1(x_all, mods[1], norm1_g[1], cos_t, sin_t, pieces, nct)
    o_f, o_b = _retention(q, k, v, ret_decay_logit[0], n_ctx)
    consts = [ret_norm_g[0].reshape(1, RET_MIX), ret_w_out[0].astype(BF16)]
    x1, h2, e_tl, w_tl, r_tl, cnt = _mix_call(
        _mix1_body, "mix_retention", x_all, mods[1], n_lat // TOKEN_TILE,
        [(o_f, False), (o_b, False), (g, True)], consts,
        norm2_g[1], moe_w_router[1], moe_b_router[1], nct, nct)
    yk, w_tok = _moe(h2, e_tl, w_tl, r_tl, cnt, 1, moe_w_gu, moe_b_gu, moe_w_down, moe_b_down)
    return _combine(x1, mods[1], yk, w_tok, nct, nct, final_norm_g)
```

```python
import functools
import math

import jax
import jax.numpy as jnp
from jax import lax
from jax.experimental import pallas as pl
from jax.experimental.pallas import tpu as pltpu
from jax.experimental.pallas import tpu_sc as plsc

F32, BF16, I32, U32 = jnp.float32, jnp.bfloat16, jnp.int32, jnp.uint32

D_MODEL = 1024
GRID_W = 64
EPS = 1e-6
GLA_HEADS, GLA_DK, GLA_DV, GLA_RANK, GLA_TAU, GLA_CHUNK = 4, 64, 128, 16, 16.0, 64
GLA_BATCH = 4
AB_QK, AB_V = GLA_HEADS * GLA_DK, GLA_HEADS * GLA_DV
S5_CH, S5_GROUP, S5_GROUPS, S5_P = 512, 16, 32, 64
S5_CHUNK = 16
S5_FOLD_BATCH = 4
RET_HEADS, RET_DK, RET_DV = 4, 256, 512
RET_CHUNK = 256
RET_QK, RET_MIX = RET_HEADS * RET_DK, RET_HEADS * RET_DV
ROPE_BASE = 10000.0
N_EXPERTS, TOP_K, D_FF = 32, 4, 1024
SWIGLU_LIMIT, SWIGLU_ALPHA = 7.0, 1.702
MOE_BLOCK = 512
TOKEN_TILE = 256
ADA_TILE = 768
VMEM_LIMIT = 56 * 1024 * 1024
SC_CORES, SC_SUBCORES = 2, 16
SC_WORKERS = SC_CORES * SC_SUBCORES
SC_CHUNK = 64


def _params(*sem):
    return pltpu.CompilerParams(dimension_semantics=sem, vmem_limit_bytes=VMEM_LIMIT)


def _dot(a, b):
    return jnp.dot(a, b, preferred_element_type=F32)


def _dot_nt(a, b):
    return lax.dot_general(a, b, (((1,), (1,)), ((), ())), preferred_element_type=F32)


def _dot_tn(a, b):
    return lax.dot_general(a, b, (((0,), (0,)), ((), ())), preferred_element_type=F32)


def _split(a):
    hi = a.astype(BF16)
    return hi, (a - hi.astype(F32)).astype(BF16)


def _dot3(a, b, dot=_dot):
    ah, al = _split(a)
    bh, bl = _split(b)
    return dot(ah, bh) + (dot(ah, bl) + dot(al, bh))


def _pack_rows(x):
    h = x.shape[1] // 2
    lo = lax.bitcast_convert_type(x[:, 0:h].astype(BF16).astype(F32), U32)
    hi = lax.bitcast_convert_type(x[:, h:2 * h].astype(BF16).astype(F32), U32)
    return hi | (lo >> 16)


def _unpack_rows(p):
    lo = lax.bitcast_convert_type(p << 16, F32)
    hi = lax.bitcast_convert_type(p & jnp.uint32(0xFFFF0000), F32)
    return lo, hi


def _silu(x):
    return x * jax.nn.sigmoid(x)


def _norm_mod(x, g, shift, scale):
    r = lax.rsqrt(jnp.mean(x * x, axis=-1, keepdims=True) + EPS)
    return (x * r * g) * (1.0 + scale) + shift


def _const_spec(shape):
    nd = len(shape)
    return pl.BlockSpec(shape, lambda *_: (0,) * nd)


def _ada_body(c_ref, w_ref, b_ref, o_ref):
    o_ref[0] = _dot3(_silu(c_ref[...]), w_ref[0]) + b_ref[0]


def _ada_mod(cvec, ada_w, ada_b):
    depth, d, n6 = ada_w.shape
    rows = cvec.shape[0]
    return pl.pallas_call(
        _ada_body,
        grid=(depth, n6 // ADA_TILE),
        in_specs=[_const_spec((rows, d)),
                  pl.BlockSpec((1, d, ADA_TILE), lambda l, j: (l, 0, j)),
                  pl.BlockSpec((1, 1, ADA_TILE), lambda l, j: (l, 0, j))],
        out_specs=pl.BlockSpec((1, rows, ADA_TILE), lambda l, j: (l, 0, j)),
        out_shape=jax.ShapeDtypeStruct((depth, rows, n6), F32),
        compiler_params=_params("arbitrary", "arbitrary"),
        name="ada_mod",
    )(cvec, ada_w, ada_b.reshape(depth, 1, n6))


def _tile_specs(nct, d):
    x_spec = pl.BlockSpec((1, TOKEN_TILE, d), lambda b, i: (b, i, 0))
    mod_spec = pl.BlockSpec((1, 1, 6, d), lambda b, i: (b, (i >= nct).astype(I32), 0, 0))
    return x_spec, mod_spec


def _split_specs(nct, d):
    ctx_spec = pl.BlockSpec((1, TOKEN_TILE, d), lambda b, i: (b, jnp.minimum(i, nct - 1), 0))
    lat_spec = pl.BlockSpec((1, TOKEN_TILE, d), lambda b, i: (b, jnp.maximum(i - nct, 0), 0))
    return ctx_spec, lat_spec


def _stream_tile(nct, ctx_ref, lat_ref):
    return jnp.where(pl.program_id(1) < nct, ctx_ref[0], lat_ref[0])


def _inproj0_body(nct, ctx_ref, lat_ref, mod_ref, g_ref, wq, wk, wv, wg, wlow, wu, wa_ref, ba_ref, tri_ref, ones_ref,
                  qd_f, ki_f, ks_f, ed_f, qd_b, ki_b, ks_b, ed_b, ov, og, ou):
    m = mod_ref[0, 0]
    h = _norm_mod(_stream_tile(nct, ctx_ref, lat_ref), g_ref[...], m[0:1], m[1:2]).astype(BF16)
    ov[0] = _dot(h, wv[...]).astype(ov.dtype)
    og[0] = _dot(h, wg[...]).astype(og.dtype)
    u = _dot(h, wu[...])
    for qb in range(ou.shape[0]):
        ou[qb, 0] = u[:, qb * 128:(qb + 1) * 128]
    q = _dot(h, wq[...]) * (GLA_DK ** -0.5)
    k = _dot(h, wk[...])
    low = _dot(h, wlow[...])
    outs = ((qd_f, ki_f, ks_f, ed_f), (qd_b, ki_b, ks_b, ed_b))
    for d, (qd_ref, ki_ref, ks_ref, ed_ref) in enumerate(outs):
        z = _dot3(low, wa_ref[d]) + ba_ref[d]
        log_a = (jnp.minimum(z, 0.0) - jnp.log1p(jnp.exp(-jnp.abs(z)))) * (1.0 / GLA_TAU)
        la_hi, la_lo = _split(log_a)
        cum = _dot(tri_ref[d], la_hi) + _dot(tri_ref[d], la_lo)
        tot = _dot(ones_ref[...], la_hi) + _dot(ones_ref[...], la_lo)
        qd_ref[0] = (q * jnp.exp(cum)).astype(BF16)
        ki_ref[0] = (k * jnp.exp(-cum)).astype(BF16)
        ks_ref[0] = (k * jnp.exp(tot - cum)).astype(BF16)
        for ch in range(TOKEN_TILE // GLA_CHUNK):
            ed_ref[0, ch] = jnp.exp(tot[ch * GLA_CHUNK:ch * GLA_CHUNK + 1])


def _inproj0(ctx, x, mods, norm_g, weights, wa_pad, ba, nct):
    bsz, n_lat, d = x.shape
    t = ctx.shape[1] + n_lat
    tm = TOKEN_TILE
    _, mod_spec = _tile_specs(nct, d)
    ctx_spec, lat_spec = _split_specs(nct, d)
    pos = jnp.arange(tm)
    same_chunk = (pos[:, None] // GLA_CHUNK) == (pos[None, :] // GLA_CHUNK)
    tri = jnp.stack([same_chunk & (pos[None, :] <= pos[:, None]),
                     same_chunk & (pos[None, :] >= pos[:, None])]).astype(BF16)
    ones = same_chunk.astype(BF16)
    consts = list(weights) + [wa_pad, ba, tri, ones]
    tok = lambda w, dt: (pl.BlockSpec((1, tm, w), lambda b, i: (b, i, 0)), jax.ShapeDtypeStruct((bsz, t, w), dt))
    per_chunk = (pl.BlockSpec((1, tm // GLA_CHUNK, 1, AB_QK), lambda b, i: (b, i, 0, 0)),
                 jax.ShapeDtypeStruct((bsz, t // GLA_CHUNK, 1, AB_QK), F32))
    one_dir = [tok(AB_QK, BF16)] * 3 + [per_chunk]
    u_blocks = (pl.BlockSpec((S5_CH // 128, 1, tm, 128), lambda b, i: (0, b, i, 0)),
                jax.ShapeDtypeStruct((S5_CH // 128, bsz, t, 128), F32))
    outs = one_dir + one_dir + [tok(AB_V, BF16), tok(AB_V, BF16), u_blocks]
    return pl.pallas_call(
        functools.partial(_inproj0_body, nct),
        grid=(bsz, t // tm),
        in_specs=[ctx_spec, lat_spec, mod_spec, _const_spec((1, d))] + [_const_spec(a.shape) for a in consts],
        out_specs=[o[0] for o in outs],
        out_shape=[o[1] for o in outs],
        compiler_params=_params("arbitrary", "arbitrary"),
        name="inproj_gla_s5",
    )(ctx, x, mods, norm_g.reshape(1, d), *consts)


def _rope(acc, cos_ref, sin_ref, o_ref, scale):
    for grp in range(acc.shape[1] // 128):
        half = grp % 2
        xg = acc[:, grp * 128:(grp + 1) * 128]
        cs = cos_ref[:, half * 128:(half + 1) * 128]
        sn = sin_ref[:, half * 128:(half + 1) * 128]
        out = xg * cs + pltpu.roll(xg, 64, 1) * sn
        o_ref[0, :, grp * 128:(grp + 1) * 128] = (out * scale).astype(o_ref.dtype)


def _inproj1_body(x_ref, mod_ref, g_ref, cos_ref, sin_ref, wq, wk, wv, wg, oq, ok, ov, og):
    m = mod_ref[0, 0]
    h = _norm_mod(x_ref[0], g_ref[...], m[0:1], m[1:2]).astype(BF16)
    _rope(_dot(h, wq[...]), cos_ref, sin_ref, oq, 1.0)
    _rope(_dot(h, wk[...]), cos_ref, sin_ref, ok, RET_DK ** -0.5)
    ov[0] = _dot(h, wv[...]).astype(ov.dtype)
    og[0] = _dot(h, wg[...]).astype(og.dtype)


def _inproj1(x_all, mods, norm_g, cos_t, sin_t, weights, nct):
    bsz, t, d = x_all.shape
    x_spec, mod_spec = _tile_specs(nct, d)
    tab_spec = pl.BlockSpec((TOKEN_TILE, RET_DK), lambda b, i: (i, 0))
    return pl.pallas_call(
        _inproj1_body,
        grid=(bsz, t // TOKEN_TILE),
        in_specs=[x_spec, mod_spec, _const_spec((1, d)), tab_spec, tab_spec] + [_const_spec(w.shape) for w in weights],
        out_specs=[pl.BlockSpec((1, TOKEN_TILE, w.shape[1]), lambda b, i: (b, i, 0)) for w in weights],
        out_shape=[jax.ShapeDtypeStruct((bsz, t, w.shape[1]), BF16) for w in weights],
        compiler_params=_params("arbitrary", "arbitrary"),
        name="inproj_retention",
    )(x_all, mods, norm_g.reshape(1, d), cos_t, sin_t, *weights)


def _backward_chunk(n, n_ctx_chunks, n_chunks):
    return jnp.where(n < n_ctx_chunks, n_ctx_chunks - 1 - n, n_chunks - 1 - (n - n_ctx_chunks))


def _gla_body(qd_f, ki_f, ks_f, ed_f, v_f, qd_b, ki_b, ks_b, ed_b, v_b, hmask_ref, bdmask_ref, o_f, o_b, st_f, st_b):
    c = GLA_CHUNK

    @pl.when(pl.program_id(1) == 0)
    def _():
        st_f[...] = jnp.zeros_like(st_f)
        st_b[...] = jnp.zeros_like(st_b)

    r4 = lax.broadcasted_iota(I32, (GLA_HEADS * c, c), 0) & (c - 1)
    c4 = lax.broadcasted_iota(I32, (GLA_HEADS * c, c), 1)
    dirs = ((qd_f, ki_f, ks_f, ed_f, v_f, o_f, st_f), (qd_b, ki_b, ks_b, ed_b, v_b, o_b, st_b))
    chains = [(bb, d) + dirs[d] for bb in range(qd_f.shape[0]) for d in range(2)]
    scores, inter, grow = [], [], []
    for bb, d, qd_ref, ki_ref, ks_ref, ed_ref, v_ref, o_ref, st_ref in chains:
        q_dec = qd_ref[bb]
        q_heads = jnp.concatenate([q_dec] * GLA_HEADS, axis=0) * hmask_ref[...]
        seen4 = (c4 <= r4) if d == 0 else (c4 >= r4)
        scores.append(jnp.where(seen4, _dot_nt(q_heads, ki_ref[bb]), 0.0).astype(BF16))
        inter.append(_dot_nt(q_dec, st_ref[bb].astype(BF16)))
        grow.append(_dot_tn(v_ref[bb], ks_ref[bb]))
    for (bb, d, qd_ref, ki_ref, ks_ref, ed_ref, v_ref, o_ref, st_ref), sc, o_inter, dst in zip(chains, scores, inter, grow):
        v = v_ref[bb]
        o_intra = jnp.concatenate(
            [_dot(sc[h * c:(h + 1) * c], v[:, h * GLA_DV:(h + 1) * GLA_DV]) for h in range(GLA_HEADS)], axis=1)
        o_ref[bb] = o_intra + o_inter
        st_ref[bb] = st_ref[bb] * ed_ref[bb, 0] + bdmask_ref[...] * dst


def _gla(per_dir, v, n_ctx):
    bsz, t, _ = v.shape
    nc, ncc = t // GLA_CHUNK, n_ctx // GLA_CHUNK
    gb = GLA_BATCH
    fwd = lambda b, n: (b, n, 0)
    bwd = lambda b, n: (b, _backward_chunk(n, ncc, nc), 0)
    hmask = (jnp.arange(AB_QK)[:, None] // GLA_CHUNK == jnp.arange(AB_QK)[None, :] // GLA_DK).astype(BF16)
    bdmask = (jnp.arange(AB_V)[:, None] // GLA_DV == jnp.arange(AB_QK)[None, :] // GLA_DK).astype(F32)

    def specs(idx):
        idx4 = lambda b, n: idx(b, n) + (0,)
        return [pl.BlockSpec((gb, GLA_CHUNK, AB_QK), idx)] * 3 + [pl.BlockSpec((gb, 1, 1, AB_QK), idx4),
                                                                  pl.BlockSpec((gb, GLA_CHUNK, AB_V), idx)]

    return pl.pallas_call(
        _gla_body,
        grid=(bsz // gb, nc),
        in_specs=specs(fwd) + specs(bwd) + [_const_spec(hmask.shape), _const_spec(bdmask.shape)],
        out_specs=[pl.BlockSpec((gb, GLA_CHUNK, AB_V), fwd), pl.BlockSpec((gb, GLA_CHUNK, AB_V), bwd)],
        out_shape=[jax.ShapeDtypeStruct((bsz, t, AB_V), F32)] * 2,
        scratch_shapes=[pltpu.VMEM((gb, AB_V, AB_QK), F32)] * 2,
        compiler_params=_params("arbitrary", "arbitrary"),
        name="gla_scan",
    )(*per_dir[0], v, *per_dir[1], v, hmask, bdmask)


def _cmul(x, y):
    return x[0] * y[0] - x[1] * y[1], x[0] * y[1] + x[1] * y[0]


def _s5_operators(lam_re, lam_im, log_step, b_re, b_im, c_re, c_im):
    ln = S5_CHUNK
    step = jnp.exp(log_step.astype(F32))[..., None]
    lam_re, lam_im = lam_re.astype(F32), lam_im.astype(F32)
    mag = jnp.exp(lam_re * step)
    a = (mag * jnp.cos(lam_im * step), mag * jnp.sin(lam_im * step))
    den = lam_re * lam_re + lam_im * lam_im
    f_re = ((a[0] - 1.0) * lam_re + a[1] * lam_im) / den
    f_im = (a[1] * lam_re - (a[0] - 1.0) * lam_im) / den
    bt_re, bt_im = b_re.transpose(0, 2, 1), b_im.transpose(0, 2, 1)
    bb = _cmul((f_re[:, :, None, :], f_im[:, :, None, :]), (bt_re, bt_im))
    bbt = jnp.concatenate([bb[0], -bb[1]], axis=-1)
    pw = (a[0][:, :, None, :], a[1][:, :, None, :])
    while pw[0].shape[2] < ln:
        top = (pw[0][:, :, -1:, :], pw[1][:, :, -1:, :])
        nxt = _cmul(top, pw)
        pw = (jnp.concatenate([pw[0], nxt[0]], axis=2), jnp.concatenate([pw[1], nxt[1]], axis=2))
    pw = (jnp.concatenate([jnp.ones_like(pw[0][:, :, :1]), pw[0]], axis=2),
          jnp.concatenate([jnp.zeros_like(pw[1][:, :, :1]), pw[1]], axis=2))
    ca = _cmul((c_re[:, :, None], c_im[:, :, None]), (pw[0][:, :, :, None, :], pw[1][:, :, :, None, :]))
    by_dir = lambda arr, lo, flip_d: jnp.stack([jnp.flip(arr[d, :, lo:lo + ln], axis=1) if d == flip_d
                                                else arr[d, :, lo:lo + ln] for d in range(2)])
    rows = lambda arr: arr.reshape(2, S5_GROUPS, ln * S5_GROUP, 2 * S5_P)
    cab = rows(by_dir(jnp.concatenate([ca[0], ca[1]], axis=-1), 0, 1))
    cab2 = rows(by_dir(jnp.concatenate([ca[0], -ca[1]], axis=-1), 1, 1)).astype(BF16)
    pwx = by_dir(jnp.concatenate([pw[0], pw[1]], axis=-1), 0, 0)
    lr, li = pw[0][:, :, ln], pw[1][:, :, ln]
    ac_rows = [jnp.concatenate([lr, lr], -1), jnp.concatenate([-li, li], -1), jnp.concatenate([li, -li], -1)]
    ac = jnp.stack(ac_rows + [jnp.zeros_like(ac_rows[0])] * 5, axis=2)
    return cab, cab2, bbt, pwx, ac


def _s5_group_operators(cab_ref, bbt_ref, pwx_ref, tz, wx):
    ln, ch, p = S5_CHUNK, S5_GROUP, S5_P
    lane = lax.broadcasted_iota(I32, (ch, ln * ch), 1)
    for d in range(2):
        kern = _dot3(bbt_ref[d, 0], cab_ref[d, 0], dot=_dot_nt)
        bt = bbt_ref[d, 0]
        b_re, b_im = bt[:, 0:p], -bt[:, p:2 * p]
        for j in range(ln):
            if d == 0:
                blk = jnp.where(lane >= j * ch, kern if j == 0 else pltpu.roll(kern, j * ch, 1), 0.0)
            else:
                blk = jnp.where(lane < (j + 1) * ch, kern if j == ln - 1 else pltpu.roll(kern, (j + 1) * ch, 1), 0.0)
            tz[d, j * ch:(j + 1) * ch, :] = blk.astype(BF16)
            pr, pi = pwx_ref[d, 0, j:j + 1, 0:p], pwx_ref[d, 0, j:j + 1, p:2 * p]
            x_re, x_im = pr * b_re - pi * b_im, pr * b_im + pi * b_re
            wx[d, j * ch:(j + 1) * ch, :] = jnp.concatenate([x_re, x_im, x_im, x_re], axis=1).astype(BF16)


def _s5_placement(pall):
    rows, cols = pall.shape[1], pall.shape[2]
    row = lax.broadcasted_iota(I32, (rows, cols), 0)
    col = lax.broadcasted_iota(I32, (rows, cols), 1)
    same_token = (row >> 7) == (col >> 4)
    for g8 in range(pall.shape[0]):
        pall[g8] = jnp.where(same_token & ((row & 127) == g8 * S5_GROUP + (col & (S5_GROUP - 1))), 1.0, 0.0).astype(BF16)


def _first_step():
    return (pl.program_id(0) == 0) & (pl.program_id(1) == 0)


def _s5_fold_body(ncs, u_ref, o_ref, pall, ucat):
    @pl.when(_first_step())
    def _():
        _s5_placement(pall)

    for b in range(u_ref.shape[1]):
        for j in range(S5_CHUNK):
            ucat[b * ncs:(b + 1) * ncs, j * 128:(j + 1) * 128] = u_ref[0, b, pl.ds(j, ncs, stride=S5_CHUNK), :].astype(BF16)
    for g8 in range(pall.shape[0]):
        o_ref[g8] = _dot(ucat[...], pall[g8]).astype(BF16)


def _s5_unfold_body(ncs, y_ref, o_ref, pall):
    @pl.when(_first_step())
    def _():
        _s5_placement(pall)

    def token(i, carry):
        r0 = pl.multiple_of(i * 128, 128)
        acc = _dot_nt(y_ref[0], pall[0, pl.ds(r0, 128), :])
        for g8 in range(1, pall.shape[0]):
            acc = acc + _dot_nt(y_ref[g8], pall[g8, pl.ds(r0, 128), :])
        for b in range(o_ref.shape[1]):
            o_ref[0, b, pl.ds(i, ncs, stride=S5_CHUNK), :] = acc[b * ncs:(b + 1) * ncs]
        return carry

    lax.fori_loop(0, S5_CHUNK, token, 0)


def _s5_body(ncs_ctx, ncs, rows, u_ref, cab_ref, cab2_ref, bbt_ref, pwx_ref, ac_ref, y_ref,
             tz, wx, xx_f, xs_f, xx_b, xs_b, sin_f, sin_b):
    _s5_group_operators(cab_ref, bbt_ref, pwx_ref, tz, wx)
    u = u_ref[0]
    half = 2 * S5_P
    for d, (xx, xs) in enumerate(((xx_f, xs_f), (xx_b, xs_b))):
        r = _dot(u, wx[d])
        xx[...] = r[:, :half]
        xs[...] = r[:, half:]
    ac_f, ac_b = ac_ref[0, 0], ac_ref[1, 0]

    def advance(ac, s, s_sw, x, x_sw):
        return ac[0:1] * s + ac[1:2] * s_sw + x, ac[0:1] * s_sw + ac[2:3] * s + x_sw

    def step(n, carry):
        s_f, sw_f, s_b, sw_b = carry
        at_f = pl.ds(n, rows, stride=ncs)
        at_b = pl.ds(_backward_chunk(n, ncs_ctx, ncs), rows, stride=ncs)
        sin_f[at_f, :] = s_f
        sin_b[at_b, :] = s_b
        s_f, sw_f = advance(ac_f, s_f, sw_f, xx_f[at_f, :], xs_f[at_f, :])
        s_b, sw_b = advance(ac_b, s_b, sw_b, xx_b[at_b, :], xs_b[at_b, :])
        return s_f, sw_f, s_b, sw_b

    zero = jnp.zeros((rows, half), F32)
    lax.fori_loop(0, ncs, step, (zero, zero, zero, zero))
    y_ref[0] = (_dot(u, tz[0]) + _dot(u, tz[1]) + _dot_nt(sin_f[...].astype(BF16), cab2_ref[0, 0])
                + _dot_nt(sin_b[...].astype(BF16), cab2_ref[1, 0])).astype(BF16)


def _s5(u4, ops, n_ctx):
    nq, bsz, t, _ = u4.shape
    ln, lanes = S5_CHUNK, S5_CHUNK * S5_GROUP
    gq = S5_GROUPS // nq
    ncs, ncs_ctx = t // ln, n_ctx // ln
    m = ncs * bsz
    hb = S5_FOLD_BATCH
    tok_spec = pl.BlockSpec((1, hb, t, 128), lambda q, h: (q, h, 0, 0))
    grp_spec = pl.BlockSpec((gq, hb * ncs, lanes), lambda q, h: (q, h, 0))
    pall = pltpu.VMEM((gq, ln * 128, lanes), BF16)
    ug = pl.pallas_call(
        functools.partial(_s5_fold_body, ncs),
        grid=(nq, bsz // hb),
        in_specs=[tok_spec],
        out_specs=grp_spec,
        out_shape=jax.ShapeDtypeStruct((S5_GROUPS, m, lanes), BF16),
        scratch_shapes=[pall, pltpu.VMEM((hb * ncs, ln * 128), BF16)],
        compiler_params=_params("arbitrary", "arbitrary"),
        name="s5_fold",
    )(u4)
    dir_spec = lambda arr: pl.BlockSpec((2, 1) + arr.shape[2:], lambda g: (0, g, 0, 0))
    yg = pl.pallas_call(
        functools.partial(_s5_body, ncs_ctx, ncs, bsz),
        grid=(S5_GROUPS,),
        in_specs=[pl.BlockSpec((1, m, lanes), lambda g: (g, 0, 0))] + [dir_spec(arr) for arr in ops],
        out_specs=pl.BlockSpec((1, m, lanes), lambda g: (g, 0, 0)),
        out_shape=jax.ShapeDtypeStruct((S5_GROUPS, m, lanes), BF16),
        scratch_shapes=[pltpu.VMEM((2, lanes, lanes), BF16)] * 2 + [pltpu.VMEM((m, 2 * S5_P), F32)] * 6,
        compiler_params=_params("arbitrary"),
        name="s5_scan",
    )(ug, *ops)
    return pl.pallas_call(
        functools.partial(_s5_unfold_body, ncs),
        grid=(nq, bsz // hb),
        in_specs=[grp_spec],
        out_specs=tok_spec,
        out_shape=jax.ShapeDtypeStruct(u4.shape, F32),
        scratch_shapes=[pall],
        compiler_params=_params("arbitrary", "arbitrary"),
        name="s5_unfold",
    )(yg)


def _ret_body(ncc, q_f, k_f, v_f, q_b, k_b, v_b, dmat_ref, rsc_ref, csc_ref, gam_ref, o_f, o_b, st_f, st_b):
    n = pl.program_id(1)

    @pl.when(n == 0)
    def _():
        st_f[...] = jnp.zeros_like(st_f)
        st_b[...] = jnp.zeros_like(st_b)

    dirs = ((q_f, k_f, v_f, o_f, st_f), (q_b, k_b, v_b, o_b, st_b))

    @pl.when(n >= ncc)
    def _():
        for d, (q_ref, k_ref, v_ref, o_ref, st_ref) in enumerate(dirs):
            for h in range(RET_HEADS):
                qh = q_ref[0, :, h * RET_DK:(h + 1) * RET_DK]
                kh = k_ref[0, :, h * RET_DK:(h + 1) * RET_DK]
                vh = v_ref[0, :, h * RET_DV:(h + 1) * RET_DV]
                scores = (_dot_nt(qh, kh) * dmat_ref[d, h]).astype(BF16)
                o = _dot(scores, vh) + rsc_ref[d, h] * _dot(qh, st_ref[h].astype(BF16))
                o_ref[0, :, h * RET_DV:(h + 1) * RET_DV] = o.astype(o_ref.dtype)

    for d, (q_ref, k_ref, v_ref, o_ref, st_ref) in enumerate(dirs):
        for h in range(RET_HEADS):
            kh = k_ref[0, :, h * RET_DK:(h + 1) * RET_DK]
            vh = v_ref[0, :, h * RET_DV:(h + 1) * RET_DV]
            k_state = (kh.astype(F32) * csc_ref[d, h]).astype(BF16)
            st_ref[h] = st_ref[h] * gam_ref[d, h] + _dot_tn(k_state, vh)


def _retention(q, k, v, decay_logit, n_ctx):
    bsz, t, _ = q.shape
    c = RET_CHUNK
    nc, ncc = t // c, n_ctx // c
    nl = nc - ncc
    log_gamma = jax.nn.log_sigmoid(decay_logit.astype(F32))[:, :, None, None]
    i = jnp.arange(c, dtype=F32)
    lag = i[:, None] - i[None, :]
    lag = jnp.stack([lag, -lag])[:, None]
    dmat = jnp.where(lag >= 0, jnp.exp(log_gamma * jnp.maximum(lag, 0.0)), 0.0)
    done = jnp.stack([i + 1.0, c - i])[:, None, :, None]
    rsc = jnp.exp(log_gamma * done)
    csc = jnp.exp(log_gamma * (c - done))
    gam = jnp.exp(log_gamma[:, :, 0, 0] * c)
    fwd = lambda b, n: (b, n, 0)
    bwd = lambda b, n: (b, _backward_chunk(n, ncc, nc), 0)
    o_fwd = lambda b, n: (b, jnp.maximum(n - ncc, 0), 0)
    o_bwd = lambda b, n: (b, nl - 1 - jnp.maximum(n - ncc, 0), 0)

    def specs(idx):
        return [pl.BlockSpec((1, c, RET_QK), idx), pl.BlockSpec((1, c, RET_QK), idx), pl.BlockSpec((1, c, RET_MIX), idx)]

    return pl.pallas_call(
        functools.partial(_ret_body, ncc),
        grid=(bsz, nc),
        in_specs=specs(fwd) + specs(bwd) + [_const_spec(dmat.shape), _const_spec(rsc.shape), _const_spec(csc.shape),
                                            pl.BlockSpec(memory_space=pltpu.SMEM)],
        out_specs=[pl.BlockSpec((1, c, RET_MIX), o_fwd), pl.BlockSpec((1, c, RET_MIX), o_bwd)],
        out_shape=[jax.ShapeDtypeStruct((bsz, nl * c, RET_MIX), BF16)] * 2,
        scratch_shapes=[pltpu.VMEM((RET_HEADS, RET_DK, RET_DV), F32)] * 2,
        compiler_params=_params("arbitrary", "arbitrary"),
        name="retention_scan",
    )(q, k, v, q, k, v, dmat, rsc, csc, gam)


def _route(x, mixed, mod, n2g_ref, wr_ref, br_ref, x1_ref, h2_ref, e_ref, w_ref, r_ref, cnt_ref):
    tm = x.shape[0]
    x1 = x + mod[2:3] * mixed
    x1_ref[0] = x1
    h2 = _norm_mod(x1, n2g_ref[...], mod[3:4], mod[4:5])
    h2_ref[0] = _pack_rows(h2)
    logits = _dot3(wr_ref[...], h2, dot=_dot_nt) + br_ref[...]
    ie = lax.broadcasted_iota(I32, logits.shape, 0)
    tops, picks = [], []
    for _ in range(TOP_K):
        mx = jnp.max(logits, axis=0, keepdims=True)
        pick = jnp.min(jnp.where(logits == mx, ie, N_EXPERTS), axis=0, keepdims=True)
        tops.append(mx)
        picks.append(pick)
        logits = jnp.where(ie == pick, -jnp.inf, logits)
    ex = [jnp.exp(tk - tops[0]) for tk in tops]
    den = ex[0] + ex[1] + ex[2] + ex[3]
    for kk in range(TOP_K):
        w_ref[0, kk:kk + 1, :] = ex[kk] / den
        e_ref[0, kk:kk + 1, :] = picks[kk]

    @pl.when((pl.program_id(0) == 0) & (pl.program_id(1) == 0))
    def _():
        cnt_ref[...] = jnp.zeros_like(cnt_ref)

    earlier = (lax.broadcasted_iota(I32, (tm, tm), 0) < lax.broadcasted_iota(I32, (tm, tm), 1))
    earlier = jnp.where(earlier, 1.0, 0.0).astype(BF16)
    run = cnt_ref[:, 0:1]
    for kk, pick in enumerate(picks):
        onehot = jnp.where(ie == pick, 1.0, 0.0)
        before = _dot(onehot.astype(BF16), earlier) + run
        r_ref[0, kk:kk + 1, :] = jnp.sum(onehot * before, axis=0, keepdims=True).astype(I32)
        run = run + jnp.sum(onehot, axis=1, keepdims=True)
    cnt_ref[...] = jnp.broadcast_to(run, cnt_ref.shape)


def _mix0_body(nct, ctx_ref, lat_ref, mod_ref, of_ref, ob_ref, g_ref, ys_ref, u_ref, gng_ref, dsk_ref, gluw_ref,
               glub_ref, wo_ref, n2g_ref, wr_ref, br_ref, x1_ref, h2_ref, e_ref, w_ref, r_ref, cnt_ref):
    o = of_ref[0] + ob_ref[0]
    heads = []
    for h in range(GLA_HEADS):
        oh = o[:, h * GLA_DV:(h + 1) * GLA_DV]
        heads.append(oh * lax.rsqrt(jnp.mean(oh * oh, axis=-1, keepdims=True) + EPS))
    gla = jnp.concatenate(heads, axis=1) * gng_ref[...] * _silu(g_ref[0].astype(F32))
    lane_blocks = lambda ref: jnp.concatenate([ref[qb, 0] for qb in range(ref.shape[0])], axis=1)
    y = jax.nn.gelu(lane_blocks(ys_ref) + dsk_ref[...] * lane_blocks(u_ref))
    y = y * jax.nn.sigmoid(_dot(y.astype(BF16), gluw_ref[...]) + glub_ref[...])
    mixed = _dot(gla.astype(BF16), wo_ref[0:AB_V]) + _dot(y.astype(BF16), wo_ref[AB_V:AB_V + S5_CH])
    _route(_stream_tile(nct, ctx_ref, lat_ref), mixed, mod_ref[0, 0], n2g_ref, wr_ref, br_ref,
           x1_ref, h2_ref, e_ref, w_ref, r_ref, cnt_ref)


def _mix1_body(x_ref, mod_ref, of_ref, ob_ref, g_ref, ng_ref, wo_ref, n2g_ref, wr_ref, br_ref,
               x1_ref, h2_ref, e_ref, w_ref, r_ref, cnt_ref):
    mixed = None
    for h in range(RET_HEADS):
        sl = slice(h * RET_DV, (h + 1) * RET_DV)
        oh = of_ref[0, :, sl].astype(F32) + ob_ref[0, :, sl].astype(F32)
        mu = jnp.mean(oh, axis=-1, keepdims=True)
        cen = oh - mu
        var = jnp.mean(cen * cen, axis=-1, keepdims=True)
        gated = cen * lax.rsqrt(var + EPS) * ng_ref[:, sl] * _silu(g_ref[0, :, sl].astype(F32))
        part = _dot(gated.astype(BF16), wo_ref[sl])
        mixed = part if mixed is None else mixed + part
    _route(x_ref[0], mixed, mod_ref[0, 0], n2g_ref, wr_ref, br_ref, x1_ref, h2_ref, e_ref, w_ref, r_ref, cnt_ref)


def _mix_call(body, name, stream, mods, tiles, acts, consts, norm2_g, w_router, b_router, n_tok, seg_tile0):
    bsz, _, d = stream[-1].shape
    tm = TOKEN_TILE
    off = lambda b, i: (b, i + seg_tile0, 0)
    loc = lambda b, i: (b, i, 0)
    ntl = bsz * tiles
    flat = lambda b, i: (b * tiles + i, 0, 0)
    in_specs = list(_split_specs(n_tok, d)) if len(stream) == 2 else [pl.BlockSpec((1, tm, d), off)]
    in_specs.append(pl.BlockSpec((1, 1, 6, d), lambda b, i: (b, ((i + seg_tile0) >= n_tok).astype(I32), 0, 0)))
    args = list(stream) + [mods]
    for arr, offset in acts:
        if arr.ndim == 4:
            in_specs.append(pl.BlockSpec((arr.shape[0], 1, tm, arr.shape[3]), lambda b, i: (0, b, i, 0)))
        else:
            in_specs.append(pl.BlockSpec((1, tm, arr.shape[2]), off if offset else loc))
        args.append(arr)
    tail = list(consts) + [norm2_g.reshape(1, d), w_router.T, b_router.reshape(N_EXPERTS, 1)]
    in_specs += [_const_spec(a.shape) for a in tail]
    args += tail
    tok_out = pl.BlockSpec((1, TOP_K, tm), flat)
    return pl.pallas_call(
        body,
        grid=(bsz, tiles),
        in_specs=in_specs,
        out_specs=[pl.BlockSpec((1, tm, d), loc), pl.BlockSpec((1, tm, d // 2), loc), tok_out, tok_out, tok_out,
                   _const_spec((N_EXPERTS, 128))],
        out_shape=[jax.ShapeDtypeStruct((bsz, tiles * tm, d), F32), jax.ShapeDtypeStruct((bsz, tiles * tm, d // 2), U32),
                   jax.ShapeDtypeStruct((ntl, TOP_K, tm), I32), jax.ShapeDtypeStruct((ntl, TOP_K, tm), F32),
                   jax.ShapeDtypeStruct((ntl, TOP_K, tm), I32), jax.ShapeDtypeStruct((N_EXPERTS, 128), F32)],
        compiler_params=_params("arbitrary", "arbitrary"),
        name=name,
    )(*args)


def _cast_rows(src_ref, dst_ref, rows):
    def chunk(j, carry):
        r = pl.multiple_of(j * rows, rows)
        dst_ref[pl.ds(r, rows), :] = src_ref[0, 0, pl.ds(r, rows), :].astype(BF16)
        return carry

    lax.fori_loop(0, dst_ref.shape[0] // rows, chunk, 0)


def _expert_body(be_ref, nu_ref, x_ref, wgu_ref, bgu_ref, wd_ref, bd_ref, o_ref, wgu_bf, wd_bf):
    i = pl.program_id(0)
    live = i < nu_ref[0]
    new_expert = (i == 0) | (be_ref[i] != be_ref[jnp.maximum(i - 1, 0)])

    @pl.when(live & new_expert)
    def _():
        _cast_rows(wgu_ref, wgu_bf, 128)
        _cast_rows(wd_ref, wd_bf, 128)

    @pl.when(live)
    def _():
        x_lo, x_hi = _unpack_rows(x_ref[...])
        half = x_lo.shape[1]
        gu = (_dot(x_lo.astype(BF16), wgu_bf[0:half]) + _dot(x_hi.astype(BF16), wgu_bf[half:2 * half])
              + bgu_ref[0, 0])
        gate = jnp.minimum(gu[:, :D_FF], SWIGLU_LIMIT)
        lin = jnp.clip(gu[:, D_FF:], -SWIGLU_LIMIT, SWIGLU_LIMIT)
        act = gate * jax.nn.sigmoid(SWIGLU_ALPHA * gate) * (lin + 1.0)
        y = _dot(act.astype(BF16), wd_bf[...]) + bd_ref[0, 0]
        o_ref[...] = _pack_rows(y)

    @pl.when(i >= nu_ref[0])
    def _():
        o_ref[...] = jnp.zeros_like(o_ref)


def _experts(xb, block_e, n_used, layer, w_gu, b_gu, w_down, b_down):
    n_slots, half = xb.shape
    d = 2 * half
    n_blocks = n_slots // MOE_BLOCK
    depth = w_gu.shape[0]
    by_expert = lambda i, be, nu: (layer, be[i], 0, 0)
    return pl.pallas_call(
        _expert_body,
        grid_spec=pltpu.PrefetchScalarGridSpec(
            num_scalar_prefetch=2,
            grid=(n_blocks,),
            in_specs=[pl.BlockSpec((MOE_BLOCK, half), lambda i, be, nu: (i, 0)),
                      pl.BlockSpec((1, 1, d, 2 * D_FF), by_expert), pl.BlockSpec((1, 1, 1, 2 * D_FF), by_expert),
                      pl.BlockSpec((1, 1, D_FF, d), by_expert), pl.BlockSpec((1, 1, 1, d), by_expert)],
            out_specs=pl.BlockSpec((MOE_BLOCK, half), lambda i, be, nu: (i, 0)),
            scratch_shapes=[pltpu.VMEM((d, 2 * D_FF), BF16), pltpu.VMEM((D_FF, d), BF16)]),
        out_shape=jax.ShapeDtypeStruct((n_slots, half), U32),
        compiler_params=_params("arbitrary"),
        name="moe_experts",
    )(block_e, n_used, xb, w_gu, b_gu.reshape(depth, N_EXPERTS, 1, 2 * D_FF), w_down,
      b_down.reshape(depth, N_EXPERTS, 1, d))


def _combine_body(final, x1_ref, mod_ref, yk_ref, w_ref, *refs):
    d = x1_ref.shape[2]
    half = d // 2
    y_lo, y_hi = None, None
    for k in range(TOP_K):
        lo, hi = _unpack_rows(yk_ref[k, 0])
        wk = w_ref[0, :, k:k + 1]
        y_lo = lo * wk if y_lo is None else y_lo + lo * wk
        y_hi = hi * wk if y_hi is None else y_hi + hi * wk
    g2 = mod_ref[0, 0][5:6]
    x2_lo = x1_ref[0, :, 0:half] + g2[:, 0:half] * y_lo
    x2_hi = x1_ref[0, :, half:d] + g2[:, half:d] * y_hi
    if final:
        fg_ref, o_ref = refs
        ms = (jnp.sum(x2_lo * x2_lo, axis=-1, keepdims=True) + jnp.sum(x2_hi * x2_hi, axis=-1, keepdims=True)) / d
        r = lax.rsqrt(ms + EPS)
        x2_lo = x2_lo * r * fg_ref[:, 0:half]
        x2_hi = x2_hi * r * fg_ref[:, half:d]
    else:
        (o_ref,) = refs
    o_ref[0, :, 0:half] = x2_lo
    o_ref[0, :, half:d] = x2_hi


def _combine(x1, mods, yk, w_tok, seg_tile0, n_tok, final_g):
    bsz, t, d = x1.shape
    tm = TOKEN_TILE
    tiles = t // tm
    loc = lambda b, i: (b, i, 0)
    in_specs = [pl.BlockSpec((1, tm, d), loc),
                pl.BlockSpec((1, 1, 6, d), lambda b, i: (b, ((i + seg_tile0) >= n_tok).astype(I32), 0, 0)),
                pl.BlockSpec((TOP_K, 1, tm, d // 2), lambda b, i: (0, b, i, 0)),
                pl.BlockSpec((1, tm, TOP_K), loc)]
    args = [x1, mods, yk.reshape(TOP_K, bsz, t, d // 2), w_tok.reshape(bsz, t, TOP_K)]
    if final_g is not None:
        in_specs.append(_const_spec((1, d)))
        args.append(final_g.reshape(1, d))
    return pl.pallas_call(
        functools.partial(_combine_body, final_g is not None),
        grid=(bsz, tiles),
        in_specs=in_specs,
        out_specs=pl.BlockSpec((1, tm, d), loc),
        out_shape=jax.ShapeDtypeStruct((bsz, t, d), F32),
        compiler_params=_params("arbitrary", "arbitrary"),
        name="moe_combine",
    )(*args)


def _sc_mesh():
    return plsc.VectorSubcoreMesh(core_axis_name="core", subcore_axis_name="subcore",
                                  num_cores=SC_CORES, num_subcores=SC_SUBCORES)


def _sc_worker_base(per_worker):
    return (lax.axis_index("subcore") * SC_CORES + lax.axis_index("core")) * per_worker


def _sc_dispatch(rows, dest, n_slots):
    n, w = rows.shape
    per_worker = n // SC_WORKERS
    assert per_worker * SC_WORKERS == n and per_worker % SC_CHUNK == 0

    @functools.partial(
        pl.kernel, mesh=_sc_mesh(), out_type=jax.ShapeDtypeStruct((n_slots, w), rows.dtype),
        scratch_types=[pltpu.VMEM((SC_CHUNK,), I32)] * TOP_K + [pltpu.VMEM((SC_CHUNK, w), rows.dtype),
                                                                pltpu.SemaphoreType.DMA],
        name="moe_dispatch")
    def scatter_rows(rows_hbm, dest_hbm, out_hbm, *scratch):
        idx_refs, buf, sem = scratch[:TOP_K], scratch[TOP_K], scratch[TOP_K + 1]
        base0 = _sc_worker_base(per_worker)

        @pl.loop(0, per_worker // SC_CHUNK)
        def _(j):
            base = base0 + j * SC_CHUNK
            pltpu.sync_copy(rows_hbm.at[pl.ds(base, SC_CHUNK)], buf)
            for k, idx in enumerate(idx_refs):
                pltpu.sync_copy(dest_hbm.at[pl.ds(k * n + base, SC_CHUNK)], idx)
            copies = [pltpu.async_copy(buf, out_hbm.at[idx], sem) for idx in idx_refs]
            for cp in copies:
                cp.wait()

    return scatter_rows(rows, dest)


def _sc_gather(table, idx):
    n = idx.shape[0]
    w = table.shape[1]
    per_worker = n // SC_WORKERS
    n_chunks = per_worker // SC_CHUNK
    assert per_worker * SC_WORKERS == n and n_chunks * SC_CHUNK == per_worker and n_chunks % 2 == 0

    @functools.partial(
        pl.kernel, mesh=_sc_mesh(), out_type=jax.ShapeDtypeStruct((n, w), table.dtype),
        scratch_types=([pltpu.VMEM((SC_CHUNK,), I32)] * 2 + [pltpu.VMEM((SC_CHUNK, w), table.dtype)] * 2
                       + [pltpu.SemaphoreType.DMA] * 4),
        name="moe_gather")
    def gather_rows(table_hbm, idx_hbm, out_hbm, idx0, idx1, buf0, buf1, gsem0, gsem1, wsem0, wsem1):
        base0 = _sc_worker_base(per_worker)

        def gather_copy(idx_v, buf, sem):
            return pltpu.make_async_copy(table_hbm.at[idx_v], buf, sem)

        def write_copy(j, buf, sem):
            return pltpu.make_async_copy(buf, out_hbm.at[pl.ds(base0 + j * SC_CHUNK, SC_CHUNK)], sem)

        def start_gather(j, idx_v, buf, sem):
            pltpu.sync_copy(idx_hbm.at[pl.ds(base0 + j * SC_CHUNK, SC_CHUNK)], idx_v)
            gather_copy(idx_v, buf, sem).start()

        start_gather(0, idx0, buf0, gsem0)

        @pl.loop(0, n_chunks, step=2)
        def _(j):
            @pl.when(j > 0)
            def _():
                write_copy(j - 1, buf1, wsem1).wait()
            start_gather(j + 1, idx1, buf1, gsem1)
            gather_copy(idx0, buf0, gsem0).wait()
            write_copy(j, buf0, wsem0).start()

            @pl.when(j + 2 < n_chunks)
            def _():
                write_copy(j, buf0, wsem0).wait()
                start_gather(j + 2, idx0, buf0, gsem0)
            gather_copy(idx1, buf1, gsem1).wait()
            write_copy(j + 1, buf1, wsem1).start()

        write_copy(n_chunks - 2, buf0, wsem0).wait()
        write_copy(n_chunks - 1, buf1, wsem1).wait()

    return gather_rows(table, idx)


def _moe(h2, e_tl, w_tl, r_tl, cnt, layer, w_gu, b_gu, w_down, b_down):
    bsz, t, half = h2.shape
    n = bsz * t
    flat = lambda a: a.transpose(1, 0, 2).reshape(TOP_K, n)
    e_k, w_k, r_k = flat(e_tl), flat(w_tl), flat(r_tl)
    counts = cnt[:, 0].astype(I32)
    padded = (counts + MOE_BLOCK - 1) // MOE_BLOCK * MOE_BLOCK
    pad_end = jnp.cumsum(padded)
    pad_start = pad_end - padded
    n_blocks = (n * TOP_K + MOE_BLOCK - 1) // MOE_BLOCK + N_EXPERTS
    block_start = jnp.arange(n_blocks, dtype=I32) * MOE_BLOCK
    block_e = jnp.minimum(jnp.sum((pad_end[None, :] <= block_start[:, None]).astype(I32), axis=1), N_EXPERTS - 1)
    n_used = (pad_end[-1:] // MOE_BLOCK).astype(I32)
    start_k = jnp.sum(jnp.where(e_k[..., None] == jnp.arange(N_EXPERTS, dtype=I32), pad_start, 0), axis=-1)
    dest = (start_k + r_k).reshape(TOP_K * n)
    xb = _sc_dispatch(h2.reshape(n, half), dest, n_blocks * MOE_BLOCK)
    yb = _experts(xb, block_e, n_used, layer, w_gu, b_gu, w_down, b_down)
    return _sc_gather(yb, dest), w_k.T


def _rope_tables(n_ctx, n_lat):
    n_freq = RET_DK // 4
    inv_freq = ROPE_BASE ** (-jnp.arange(n_freq, dtype=F32) / n_freq)
    pos = jnp.arange(n_lat, dtype=I32)
    cos, sin = [], []
    for p in (pos // GRID_W, pos % GRID_W):
        ang = p.astype(F32)[:, None] * inv_freq
        cos += [jnp.cos(ang), jnp.cos(ang)]
        sin += [-jnp.sin(ang), jnp.sin(ang)]
    cos, sin = jnp.concatenate(cos, axis=1), jnp.concatenate(sin, axis=1)
    return (jnp.concatenate([jnp.ones((n_ctx, RET_DK), F32), cos], axis=0),
            jnp.concatenate([jnp.zeros((n_ctx, RET_DK), F32), sin], axis=0))


def kernel(x, c, ctx, c_ctx, ada_w, ada_b, norm1_g, norm2_g, ab_w_in, ab_w_out, gla_wa, gla_ba, gla_norm_g, s5_lam_re, s5_lam_im, s5_log_step, s5_b_re, s5_b_im, s5_c_re, s5_c_im, s5_d, s5_glu_w, s5_glu_b, ret_w_in, ret_w_out, ret_decay_logit, ret_norm_g, moe_w_router, moe_b_router, moe_w_gu, moe_b_gu, moe_w_down, moe_b_down, final_norm_g):
    bsz, n_lat, d = x.shape
    n_ctx = ctx.shape[1]
    depth = ada_w.shape[0]
    assert depth == 2 and d == D_MODEL and bsz == 8, "kernels are laid out for the stated problem shape"
    assert n_ctx % TOKEN_TILE == 0 and n_lat % TOKEN_TILE == 0 and n_lat % GRID_W == 0
    t = n_ctx + n_lat
    nct = n_ctx // TOKEN_TILE

    cvec = jnp.zeros((16, d), F32).at[:bsz].set(c).at[bsz].set(c_ctx)
    mod = _ada_mod(cvec, ada_w, ada_b).reshape(depth, 16, 6, d)
    mods = [jnp.stack([jnp.broadcast_to(mod[l, bsz], (bsz, 6, d)), mod[l, :bsz]], axis=1) for l in range(depth)]

    w_in = ab_w_in[0].astype(BF16)
    cuts = [0, AB_QK, 2 * AB_QK, 2 * AB_QK + AB_V, 2 * AB_QK + 2 * AB_V, 2 * AB_QK + 2 * AB_V + 2 * GLA_RANK,
            w_in.shape[1]]
    pieces = [w_in[:, a:b] for a, b in zip(cuts[:-1], cuts[1:])]
    wa_pad = jnp.zeros((2, 2 * GLA_RANK, AB_QK), F32)
    wa_pad = wa_pad.at[0, :GLA_RANK].set(gla_wa[0, 0]).at[1, GLA_RANK:].set(gla_wa[0, 1])
    outs = _inproj0(ctx, x, mods[0], norm1_g[0], pieces, wa_pad, gla_ba[0].reshape(2, 1, AB_QK), nct)
    v, g, u = outs[8:]
    o_f, o_b = _gla((outs[0:4], outs[4:8]), v, n_ctx)
    ops = _s5_operators(s5_lam_re[0], s5_lam_im[0], s5_log_step[0], s5_b_re[0], s5_b_im[0], s5_c_re[0], s5_c_im[0])
    ys = _s5(u, ops, n_ctx)
    consts = [jnp.tile(gla_norm_g[0], GLA_HEADS).reshape(1, AB_V), s5_d[0].reshape(1, S5_CH),
              s5_glu_w[0].astype(BF16), s5_glu_b[0].reshape(1, S5_CH), ab_w_out[0].astype(BF16)]
    x1, h2, e_tl, w_tl, r_tl, cnt = _mix_call(
        functools.partial(_mix0_body, nct), "mix_gla_s5", (ctx, x), mods[0], t // TOKEN_TILE,
        [(o_f, False), (o_b, False), (g, False), (ys, False), (u, False)], consts,
        norm2_g[0], moe_w_router[0], moe_b_router[0], nct, 0)
    yk, w_tok = _moe(h2, e_tl, w_tl, r_tl, cnt, 0, moe_w_gu, moe_b_gu, moe_w_down, moe_b_down)
    x_all = _combine(x1, mods[0], yk, w_tok, 0, nct, None)

    w_in = ret_w_in[0].astype(BF16)
    cuts = [0, RET_QK, 2 * RET_QK, 2 * RET_QK + RET_MIX, w_in.shape[1]]
    pieces = [w_in[:, a:b] for a, b in zip(cuts[:-1], cuts[1:])]
    cos_t, sin_t = _rope_tables(n_ctx, n_lat)
    q, k, v, g = _inproj1(x_all, mods[1], norm1_g[1], cos_t, sin_t, pieces, nct)
    o_f, o_b = _retention(q, k, v, ret_decay_logit[0], n_ctx)
    consts = [ret_norm_g[0].reshape(1, RET_MIX), ret_w_out[0].astype(BF16)]
    x1, h2, e_tl, w_tl, r_tl, cnt = _mix_call(
        _mix1_body, "mix_retention", (x_all,), mods[1], n_lat // TOKEN_TILE,
        [(o_f, False), (o_b, False), (g, True)], consts,
        norm2_g[1], moe_w_router[1], moe_b_router[1], nct, nct)
    yk, w_tok = _moe(h2, e_tl, w_tl, r_tl, cnt, 1, moe_w_gu, moe_b_gu, moe_w_down, moe_b_down)
    return _combine(x1, mods[1], yk, w_tok, nct, nct, final_norm_g)
```

```python
import functools
import math

import jax
import jax.numpy as jnp
from jax import lax
from jax.experimental import pallas as pl
from jax.experimental.pallas import tpu as pltpu
from jax.experimental.pallas import tpu_sc as plsc

F32, BF16, I32, U32 = jnp.float32, jnp.bfloat16, jnp.int32, jnp.uint32

D_MODEL = 1024
GRID_W = 64
EPS = 1e-6
GLA_HEADS, GLA_DK, GLA_DV, GLA_RANK, GLA_TAU, GLA_CHUNK = 4, 64, 128, 16, 16.0, 64
GLA_BATCH = 4
AB_QK, AB_V = GLA_HEADS * GLA_DK, GLA_HEADS * GLA_DV
S5_CH, S5_GROUP, S5_GROUPS, S5_P = 512, 16, 32, 64
S5_CHUNK = 16
S5_FOLD_BATCH = 4
S5_SCAN_GROUPS = 2
RET_HEADS, RET_DK, RET_DV = 4, 256, 512
RET_CHUNK = 256
RET_QK, RET_MIX = RET_HEADS * RET_DK, RET_HEADS * RET_DV
ROPE_BASE = 10000.0
N_EXPERTS, TOP_K, D_FF = 32, 4, 1024
SWIGLU_LIMIT, SWIGLU_ALPHA = 7.0, 1.702
MOE_BLOCK = 512
TOKEN_TILE = 256
ADA_TILE = 768
VMEM_LIMIT = 56 * 1024 * 1024
SC_CORES, SC_SUBCORES = 2, 16
SC_WORKERS = SC_CORES * SC_SUBCORES
SC_CHUNK = 64


def _params(*sem):
    return pltpu.CompilerParams(dimension_semantics=sem, vmem_limit_bytes=VMEM_LIMIT)


def _dot(a, b):
    return jnp.dot(a, b, preferred_element_type=F32)


def _dot_nt(a, b):
    return lax.dot_general(a, b, (((1,), (1,)), ((), ())), preferred_element_type=F32)


def _dot_tn(a, b):
    return lax.dot_general(a, b, (((0,), (0,)), ((), ())), preferred_element_type=F32)


def _split(a):
    hi = a.astype(BF16)
    return hi, (a - hi.astype(F32)).astype(BF16)


def _dot3(a, b, dot=_dot):
    ah, al = _split(a)
    bh, bl = _split(b)
    return dot(ah, bh) + (dot(ah, bl) + dot(al, bh))


def _pack_rows(x):
    h = x.shape[1] // 2
    lo = lax.bitcast_convert_type(x[:, 0:h].astype(BF16).astype(F32), U32)
    hi = lax.bitcast_convert_type(x[:, h:2 * h].astype(BF16).astype(F32), U32)
    return hi | (lo >> 16)


def _unpack_rows(p):
    lo = lax.bitcast_convert_type(p << 16, F32)
    hi = lax.bitcast_convert_type(p & jnp.uint32(0xFFFF0000), F32)
    return lo, hi


def _silu(x):
    return x * jax.nn.sigmoid(x)


def _norm_mod(x, g, shift, scale):
    r = lax.rsqrt(jnp.mean(x * x, axis=-1, keepdims=True) + EPS)
    return (x * r * g) * (1.0 + scale) + shift


def _const_spec(shape):
    nd = len(shape)
    return pl.BlockSpec(shape, lambda *_: (0,) * nd)


def _ada_body(c_ref, w_ref, b_ref, o_ref):
    o_ref[0] = _dot3(_silu(c_ref[...]), w_ref[0]) + b_ref[0]


def _ada_mod(cvec, ada_w, ada_b):
    depth, d, n6 = ada_w.shape
    rows = cvec.shape[0]
    return pl.pallas_call(
        _ada_body,
        grid=(depth, n6 // ADA_TILE),
        in_specs=[_const_spec((rows, d)),
                  pl.BlockSpec((1, d, ADA_TILE), lambda l, j: (l, 0, j)),
                  pl.BlockSpec((1, 1, ADA_TILE), lambda l, j: (l, 0, j))],
        out_specs=pl.BlockSpec((1, rows, ADA_TILE), lambda l, j: (l, 0, j)),
        out_shape=jax.ShapeDtypeStruct((depth, rows, n6), F32),
        compiler_params=_params("arbitrary", "arbitrary"),
        name="ada_mod",
    )(cvec, ada_w, ada_b.reshape(depth, 1, n6))


def _tile_specs(nct, d):
    x_spec = pl.BlockSpec((1, TOKEN_TILE, d), lambda b, i: (b, i, 0))
    mod_spec = pl.BlockSpec((1, 1, 6, d), lambda b, i: (b, (i >= nct).astype(I32), 0, 0))
    return x_spec, mod_spec


def _split_specs(nct, d):
    ctx_spec = pl.BlockSpec((1, TOKEN_TILE, d), lambda b, i: (b, jnp.minimum(i, nct - 1), 0))
    lat_spec = pl.BlockSpec((1, TOKEN_TILE, d), lambda b, i: (b, jnp.maximum(i - nct, 0), 0))
    return ctx_spec, lat_spec


def _stream_tile(nct, ctx_ref, lat_ref):
    return jnp.where(pl.program_id(1) < nct, ctx_ref[0], lat_ref[0])


def _inproj0_body(nct, ctx_ref, lat_ref, mod_ref, g_ref, wq, wk, wv, wg, wlow, wu, wa_ref, ba_ref, tri_ref, ones_ref,
                  qd_f, ki_f, ks_f, ed_f, qd_b, ki_b, ks_b, ed_b, ov, og, ou):
    m = mod_ref[0, 0]
    h = _norm_mod(_stream_tile(nct, ctx_ref, lat_ref), g_ref[...], m[0:1], m[1:2]).astype(BF16)
    ov[0] = _dot(h, wv[...]).astype(ov.dtype)
    og[0] = _dot(h, wg[...]).astype(og.dtype)
    u = _dot(h, wu[...])
    for qb in range(ou.shape[0]):
        ou[qb, 0] = u[:, qb * 128:(qb + 1) * 128]
    q = _dot(h, wq[...]) * (GLA_DK ** -0.5)
    k = _dot(h, wk[...])
    low = _dot(h, wlow[...])
    outs = ((qd_f, ki_f, ks_f, ed_f), (qd_b, ki_b, ks_b, ed_b))
    for d, (qd_ref, ki_ref, ks_ref, ed_ref) in enumerate(outs):
        z = _dot3(low, wa_ref[d]) + ba_ref[d]
        log_a = (jnp.minimum(z, 0.0) - jnp.log1p(jnp.exp(-jnp.abs(z)))) * (1.0 / GLA_TAU)
        la_hi, la_lo = _split(log_a)
        cum = _dot(tri_ref[d], la_hi) + _dot(tri_ref[d], la_lo)
        tot = _dot(ones_ref[...], la_hi) + _dot(ones_ref[...], la_lo)
        qd_ref[0] = (q * jnp.exp(cum)).astype(BF16)
        ki_ref[0] = (k * jnp.exp(-cum)).astype(BF16)
        ks_ref[0] = (k * jnp.exp(tot - cum)).astype(BF16)
        for ch in range(TOKEN_TILE // GLA_CHUNK):
            ed_ref[0, ch] = jnp.exp(tot[ch * GLA_CHUNK:ch * GLA_CHUNK + 1])


def _inproj0(ctx, x, mods, norm_g, weights, wa_pad, ba, nct):
    bsz, n_lat, d = x.shape
    t = ctx.shape[1] + n_lat
    tm = TOKEN_TILE
    _, mod_spec = _tile_specs(nct, d)
    ctx_spec, lat_spec = _split_specs(nct, d)
    pos = jnp.arange(tm)
    same_chunk = (pos[:, None] // GLA_CHUNK) == (pos[None, :] // GLA_CHUNK)
    tri = jnp.stack([same_chunk & (pos[None, :] <= pos[:, None]),
                     same_chunk & (pos[None, :] >= pos[:, None])]).astype(BF16)
    ones = same_chunk.astype(BF16)
    consts = list(weights) + [wa_pad, ba, tri, ones]
    tok = lambda w, dt: (pl.BlockSpec((1, tm, w), lambda b, i: (b, i, 0)), jax.ShapeDtypeStruct((bsz, t, w), dt))
    per_chunk = (pl.BlockSpec((1, tm // GLA_CHUNK, 1, AB_QK), lambda b, i: (b, i, 0, 0)),
                 jax.ShapeDtypeStruct((bsz, t // GLA_CHUNK, 1, AB_QK), F32))
    one_dir = [tok(AB_QK, BF16)] * 3 + [per_chunk]
    u_blocks = (pl.BlockSpec((S5_CH // 128, 1, tm, 128), lambda b, i: (0, b, i, 0)),
                jax.ShapeDtypeStruct((S5_CH // 128, bsz, t, 128), F32))
    outs = one_dir + one_dir + [tok(AB_V, BF16), tok(AB_V, BF16), u_blocks]
    return pl.pallas_call(
        functools.partial(_inproj0_body, nct),
        grid=(bsz, t // tm),
        in_specs=[ctx_spec, lat_spec, mod_spec, _const_spec((1, d))] + [_const_spec(a.shape) for a in consts],
        out_specs=[o[0] for o in outs],
        out_shape=[o[1] for o in outs],
        compiler_params=_params("arbitrary", "arbitrary"),
        name="inproj_gla_s5",
    )(ctx, x, mods, norm_g.reshape(1, d), *consts)


def _rope(acc, cos_ref, sin_ref, o_ref, scale):
    for grp in range(acc.shape[1] // 128):
        half = grp % 2
        xg = acc[:, grp * 128:(grp + 1) * 128]
        cs = cos_ref[:, half * 128:(half + 1) * 128]
        sn = sin_ref[:, half * 128:(half + 1) * 128]
        out = xg * cs + pltpu.roll(xg, 64, 1) * sn
        o_ref[0, :, grp * 128:(grp + 1) * 128] = (out * scale).astype(o_ref.dtype)


def _inproj1_body(x_ref, mod_ref, g_ref, cos_ref, sin_ref, wq, wk, wv, wg, oq, ok, ov, og):
    m = mod_ref[0, 0]
    h = _norm_mod(x_ref[0], g_ref[...], m[0:1], m[1:2]).astype(BF16)
    _rope(_dot(h, wq[...]), cos_ref, sin_ref, oq, 1.0)
    _rope(_dot(h, wk[...]), cos_ref, sin_ref, ok, RET_DK ** -0.5)
    ov[0] = _dot(h, wv[...]).astype(ov.dtype)
    og[0] = _dot(h, wg[...]).astype(og.dtype)


def _inproj1(x_all, mods, norm_g, cos_t, sin_t, weights, nct):
    bsz, t, d = x_all.shape
    x_spec, mod_spec = _tile_specs(nct, d)
    tab_spec = pl.BlockSpec((TOKEN_TILE, RET_DK), lambda b, i: (i, 0))
    return pl.pallas_call(
        _inproj1_body,
        grid=(bsz, t // TOKEN_TILE),
        in_specs=[x_spec, mod_spec, _const_spec((1, d)), tab_spec, tab_spec] + [_const_spec(w.shape) for w in weights],
        out_specs=[pl.BlockSpec((1, TOKEN_TILE, w.shape[1]), lambda b, i: (b, i, 0)) for w in weights],
        out_shape=[jax.ShapeDtypeStruct((bsz, t, w.shape[1]), BF16) for w in weights],
        compiler_params=_params("arbitrary", "arbitrary"),
        name="inproj_retention",
    )(x_all, mods, norm_g.reshape(1, d), cos_t, sin_t, *weights)


def _backward_chunk(n, n_ctx_chunks, n_chunks):
    return jnp.where(n < n_ctx_chunks, n_ctx_chunks - 1 - n, n_chunks - 1 - (n - n_ctx_chunks))


def _gla_body(qd_f, ki_f, ks_f, ed_f, v_f, qd_b, ki_b, ks_b, ed_b, v_b, hmask_ref, bdmask_ref, o_f, o_b, st_f, st_b):
    c = GLA_CHUNK

    @pl.when(pl.program_id(1) == 0)
    def _():
        st_f[...] = jnp.zeros_like(st_f)
        st_b[...] = jnp.zeros_like(st_b)

    r4 = lax.broadcasted_iota(I32, (GLA_HEADS * c, c), 0) & (c - 1)
    c4 = lax.broadcasted_iota(I32, (GLA_HEADS * c, c), 1)
    dirs = ((qd_f, ki_f, ks_f, ed_f, v_f, o_f, st_f), (qd_b, ki_b, ks_b, ed_b, v_b, o_b, st_b))
    chains = [(bb, d) + dirs[d] for bb in range(qd_f.shape[0]) for d in range(2)]
    scores, inter, grow = [], [], []
    for bb, d, qd_ref, ki_ref, ks_ref, ed_ref, v_ref, o_ref, st_ref in chains:
        q_dec = qd_ref[bb]
        q_heads = jnp.concatenate([q_dec] * GLA_HEADS, axis=0) * hmask_ref[...]
        seen4 = (c4 <= r4) if d == 0 else (c4 >= r4)
        scores.append(jnp.where(seen4, _dot_nt(q_heads, ki_ref[bb]), 0.0).astype(BF16))
        inter.append(_dot_nt(q_dec, st_ref[bb].astype(BF16)))
        grow.append(_dot_tn(v_ref[bb], ks_ref[bb]))
    for (bb, d, qd_ref, ki_ref, ks_ref, ed_ref, v_ref, o_ref, st_ref), sc, o_inter, dst in zip(chains, scores, inter, grow):
        v = v_ref[bb]
        o_intra = jnp.concatenate(
            [_dot(sc[h * c:(h + 1) * c], v[:, h * GLA_DV:(h + 1) * GLA_DV]) for h in range(GLA_HEADS)], axis=1)
        o_ref[bb] = o_intra + o_inter
        st_ref[bb] = st_ref[bb] * ed_ref[bb, 0] + bdmask_ref[...] * dst


def _gla(per_dir, v, n_ctx):
    bsz, t, _ = v.shape
    nc, ncc = t // GLA_CHUNK, n_ctx // GLA_CHUNK
    gb = GLA_BATCH
    fwd = lambda b, n: (b, n, 0)
    bwd = lambda b, n: (b, _backward_chunk(n, ncc, nc), 0)
    hmask = (jnp.arange(AB_QK)[:, None] // GLA_CHUNK == jnp.arange(AB_QK)[None, :] // GLA_DK).astype(BF16)
    bdmask = (jnp.arange(AB_V)[:, None] // GLA_DV == jnp.arange(AB_QK)[None, :] // GLA_DK).astype(F32)

    def specs(idx):
        idx4 = lambda b, n: idx(b, n) + (0,)
        return [pl.BlockSpec((gb, GLA_CHUNK, AB_QK), idx)] * 3 + [pl.BlockSpec((gb, 1, 1, AB_QK), idx4),
                                                                  pl.BlockSpec((gb, GLA_CHUNK, AB_V), idx)]

    return pl.pallas_call(
        _gla_body,
        grid=(bsz // gb, nc),
        in_specs=specs(fwd) + specs(bwd) + [_const_spec(hmask.shape), _const_spec(bdmask.shape)],
        out_specs=[pl.BlockSpec((gb, GLA_CHUNK, AB_V), fwd), pl.BlockSpec((gb, GLA_CHUNK, AB_V), bwd)],
        out_shape=[jax.ShapeDtypeStruct((bsz, t, AB_V), F32)] * 2,
        scratch_shapes=[pltpu.VMEM((gb, AB_V, AB_QK), F32)] * 2,
        compiler_params=_params("arbitrary", "arbitrary"),
        name="gla_scan",
    )(*per_dir[0], v, *per_dir[1], v, hmask, bdmask)


def _cmul(x, y):
    return x[0] * y[0] - x[1] * y[1], x[0] * y[1] + x[1] * y[0]


def _s5_operators(lam_re, lam_im, log_step, b_re, b_im, c_re, c_im):
    ln = S5_CHUNK
    step = jnp.exp(log_step.astype(F32))[..., None]
    lam_re, lam_im = lam_re.astype(F32), lam_im.astype(F32)
    mag = jnp.exp(lam_re * step)
    a = (mag * jnp.cos(lam_im * step), mag * jnp.sin(lam_im * step))
    den = lam_re * lam_re + lam_im * lam_im
    f_re = ((a[0] - 1.0) * lam_re + a[1] * lam_im) / den
    f_im = (a[1] * lam_re - (a[0] - 1.0) * lam_im) / den
    bt_re, bt_im = b_re.transpose(0, 2, 1), b_im.transpose(0, 2, 1)
    bb = _cmul((f_re[:, :, None, :], f_im[:, :, None, :]), (bt_re, bt_im))
    bbt = jnp.concatenate([bb[0], -bb[1]], axis=-1)
    pw = (a[0][:, :, None, :], a[1][:, :, None, :])
    while pw[0].shape[2] < ln:
        top = (pw[0][:, :, -1:, :], pw[1][:, :, -1:, :])
        nxt = _cmul(top, pw)
        pw = (jnp.concatenate([pw[0], nxt[0]], axis=2), jnp.concatenate([pw[1], nxt[1]], axis=2))
    pw = (jnp.concatenate([jnp.ones_like(pw[0][:, :, :1]), pw[0]], axis=2),
          jnp.concatenate([jnp.zeros_like(pw[1][:, :, :1]), pw[1]], axis=2))
    ca = _cmul((c_re[:, :, None], c_im[:, :, None]), (pw[0][:, :, :, None, :], pw[1][:, :, :, None, :]))
    by_dir = lambda arr, lo, flip_d: jnp.stack([jnp.flip(arr[d, :, lo:lo + ln], axis=1) if d == flip_d
                                                else arr[d, :, lo:lo + ln] for d in range(2)])
    rows = lambda arr: arr.reshape(2, S5_GROUPS, ln * S5_GROUP, 2 * S5_P)
    cab = rows(by_dir(jnp.concatenate([ca[0], ca[1]], axis=-1), 0, 1))
    cab2 = rows(by_dir(jnp.concatenate([ca[0], -ca[1]], axis=-1), 1, 1)).astype(BF16)
    pwx = by_dir(jnp.concatenate([pw[0], pw[1]], axis=-1), 0, 0)
    lr, li = pw[0][:, :, ln], pw[1][:, :, ln]
    ac_rows = [jnp.concatenate([lr, lr], -1), jnp.concatenate([-li, li], -1), jnp.concatenate([li, -li], -1)]
    ac = jnp.stack(ac_rows + [jnp.zeros_like(ac_rows[0])] * 5, axis=2)
    return cab, cab2, bbt, pwx, ac


def _s5_group_operators(gg, cab_ref, bbt_ref, pwx_ref, tz, wx):
    ln, ch, p = S5_CHUNK, S5_GROUP, S5_P
    lane = lax.broadcasted_iota(I32, (ch, ln * ch), 1)
    for d in range(2):
        kern = _dot3(bbt_ref[d, gg], cab_ref[d, gg], dot=_dot_nt)
        bt = bbt_ref[d, gg]
        b_re, b_im = bt[:, 0:p], -bt[:, p:2 * p]
        for j in range(ln):
            if d == 0:
                blk = jnp.where(lane >= j * ch, kern if j == 0 else pltpu.roll(kern, j * ch, 1), 0.0)
            else:
                blk = jnp.where(lane < (j + 1) * ch, kern if j == ln - 1 else pltpu.roll(kern, (j + 1) * ch, 1), 0.0)
            tz[gg, d, j * ch:(j + 1) * ch, :] = blk.astype(BF16)
            pr, pi = pwx_ref[d, gg, j:j + 1, 0:p], pwx_ref[d, gg, j:j + 1, p:2 * p]
            x_re, x_im = pr * b_re - pi * b_im, pr * b_im + pi * b_re
            wx[gg, d, j * ch:(j + 1) * ch, :] = jnp.concatenate([x_re, x_im, x_im, x_re], axis=1).astype(BF16)


def _s5_placement(pall):
    rows, cols = pall.shape[1], pall.shape[2]
    row = lax.broadcasted_iota(I32, (rows, cols), 0)
    col = lax.broadcasted_iota(I32, (rows, cols), 1)
    same_token = (row >> 7) == (col >> 4)
    for g8 in range(pall.shape[0]):
        pall[g8] = jnp.where(same_token & ((row & 127) == g8 * S5_GROUP + (col & (S5_GROUP - 1))), 1.0, 0.0).astype(BF16)


def _first_step():
    return (pl.program_id(0) == 0) & (pl.program_id(1) == 0)


def _s5_fold_body(ncs, u_ref, o_ref, pall, ucat):
    @pl.when(_first_step())
    def _():
        _s5_placement(pall)

    for b in range(u_ref.shape[1]):
        for j in range(S5_CHUNK):
            ucat[b * ncs:(b + 1) * ncs, j * 128:(j + 1) * 128] = u_ref[0, b, pl.ds(j, ncs, stride=S5_CHUNK), :].astype(BF16)
    for g8 in range(pall.shape[0]):
        o_ref[g8] = _dot(ucat[...], pall[g8]).astype(BF16)


def _s5_unfold_body(ncs, y_ref, o_ref, pall):
    @pl.when(_first_step())
    def _():
        _s5_placement(pall)

    def token_pair(i2, carry):
        r0 = pl.multiple_of(i2 * 256, 256)
        acc = _dot_nt(y_ref[0], pall[0, pl.ds(r0, 256), :])
        for g8 in range(1, pall.shape[0]):
            acc = acc + _dot_nt(y_ref[g8], pall[g8, pl.ds(r0, 256), :])
        for b in range(o_ref.shape[1]):
            for par in range(2):
                o_ref[0, b, pl.ds(2 * i2 + par, ncs, stride=S5_CHUNK), :] = (
                    acc[b * ncs:(b + 1) * ncs, par * 128:(par + 1) * 128])
        return carry

    lax.fori_loop(0, S5_CHUNK // 2, token_pair, 0)


def _s5_body(ncs_ctx, ncs, rows, u_ref, cab_ref, cab2_ref, bbt_ref, pwx_ref, ac_ref, y_ref, tz, wx, *vecs):
    half = 2 * S5_P
    n_groups = u_ref.shape[0]
    groups = [vecs[6 * gg:6 * gg + 6] for gg in range(n_groups)]
    for gg, (xx_f, xs_f, xx_b, xs_b, _, _) in enumerate(groups):
        _s5_group_operators(gg, cab_ref, bbt_ref, pwx_ref, tz, wx)
        for d, (xx, xs) in enumerate(((xx_f, xs_f), (xx_b, xs_b))):
            r = _dot(u_ref[gg], wx[gg, d])
            xx[...] = r[:, :half]
            xs[...] = r[:, half:]

    def advance(ac, s, s_sw, x, x_sw):
        return ac[0:1] * s + ac[1:2] * s_sw + x, ac[0:1] * s_sw + ac[2:3] * s + x_sw

    def step(n, carry):
        at_f = pl.ds(n, rows, stride=ncs)
        at_b = pl.ds(_backward_chunk(n, ncs_ctx, ncs), rows, stride=ncs)
        out = []
        for gg, (xx_f, xs_f, xx_b, xs_b, sin_f, sin_b) in enumerate(groups):
            s_f, sw_f, s_b, sw_b = carry[4 * gg:4 * gg + 4]
            sin_f[at_f, :] = s_f
            sin_b[at_b, :] = s_b
            out += advance(ac_ref[0, gg], s_f, sw_f, xx_f[at_f, :], xs_f[at_f, :])
            out += advance(ac_ref[1, gg], s_b, sw_b, xx_b[at_b, :], xs_b[at_b, :])
        return tuple(out)

    zero = jnp.zeros((rows, half), F32)
    lax.fori_loop(0, ncs, step, (zero,) * (4 * n_groups))
    for gg, (_, _, _, _, sin_f, sin_b) in enumerate(groups):
        u = u_ref[gg]
        y_ref[gg] = (_dot(u, tz[gg, 0]) + _dot(u, tz[gg, 1]) + _dot_nt(sin_f[...].astype(BF16), cab2_ref[0, gg])
                     + _dot_nt(sin_b[...].astype(BF16), cab2_ref[1, gg])).astype(BF16)


def _s5(u4, ops, n_ctx):
    nq, bsz, t, _ = u4.shape
    ln, lanes = S5_CHUNK, S5_CHUNK * S5_GROUP
    gq = S5_GROUPS // nq
    ncs, ncs_ctx = t // ln, n_ctx // ln
    m = ncs * bsz
    hb = S5_FOLD_BATCH
    tok_spec = pl.BlockSpec((1, hb, t, 128), lambda q, h: (q, h, 0, 0))
    grp_spec = pl.BlockSpec((gq, hb * ncs, lanes), lambda q, h: (q, h, 0))
    pall = pltpu.VMEM((gq, ln * 128, lanes), BF16)
    ug = pl.pallas_call(
        functools.partial(_s5_fold_body, ncs),
        grid=(nq, bsz // hb),
        in_specs=[tok_spec],
        out_specs=grp_spec,
        out_shape=jax.ShapeDtypeStruct((S5_GROUPS, m, lanes), BF16),
        scratch_shapes=[pall, pltpu.VMEM((hb * ncs, ln * 128), BF16)],
        compiler_params=_params("arbitrary", "arbitrary"),
        name="s5_fold",
    )(u4)
    sg = S5_SCAN_GROUPS
    dir_spec = lambda arr: pl.BlockSpec((2, sg) + arr.shape[2:], lambda g: (0, g, 0, 0))
    yg = pl.pallas_call(
        functools.partial(_s5_body, ncs_ctx, ncs, bsz),
        grid=(S5_GROUPS // sg,),
        in_specs=[pl.BlockSpec((sg, m, lanes), lambda g: (g, 0, 0))] + [dir_spec(arr) for arr in ops],
        out_specs=pl.BlockSpec((sg, m, lanes), lambda g: (g, 0, 0)),
        out_shape=jax.ShapeDtypeStruct((S5_GROUPS, m, lanes), BF16),
        scratch_shapes=[pltpu.VMEM((sg, 2, lanes, lanes), BF16)] * 2 + [pltpu.VMEM((m, 2 * S5_P), F32)] * (6 * sg),
        compiler_params=_params("arbitrary"),
        name="s5_scan",
    )(ug, *ops)
    return pl.pallas_call(
        functools.partial(_s5_unfold_body, ncs),
        grid=(nq, bsz // hb),
        in_specs=[grp_spec],
        out_specs=tok_spec,
        out_shape=jax.ShapeDtypeStruct(u4.shape, F32),
        scratch_shapes=[pall],
        compiler_params=_params("arbitrary", "arbitrary"),
        name="s5_unfold",
    )(yg)


def _ret_body(ncc, q_f, k_f, v_f, q_b, k_b, v_b, dmat_ref, rsc_ref, csc_ref, gam_ref, o_f, o_b, st_f, st_b):
    n = pl.program_id(1)

    @pl.when(n == 0)
    def _():
        st_f[...] = jnp.zeros_like(st_f)
        st_b[...] = jnp.zeros_like(st_b)

    dirs = ((q_f, k_f, v_f, o_f, st_f), (q_b, k_b, v_b, o_b, st_b))

    @pl.when(n >= ncc)
    def _():
        for d, (q_ref, k_ref, v_ref, o_ref, st_ref) in enumerate(dirs):
            for h in range(RET_HEADS):
                qh = q_ref[0, :, h * RET_DK:(h + 1) * RET_DK]
                kh = k_ref[0, :, h * RET_DK:(h + 1) * RET_DK]
                vh = v_ref[0, :, h * RET_DV:(h + 1) * RET_DV]
                scores = (_dot_nt(qh, kh) * dmat_ref[d, h]).astype(BF16)
                o = _dot(scores, vh) + rsc_ref[d, h] * _dot(qh, st_ref[h].astype(BF16))
                o_ref[0, :, h * RET_DV:(h + 1) * RET_DV] = o.astype(o_ref.dtype)

    for d, (q_ref, k_ref, v_ref, o_ref, st_ref) in enumerate(dirs):
        for h in range(RET_HEADS):
            kh = k_ref[0, :, h * RET_DK:(h + 1) * RET_DK]
            vh = v_ref[0, :, h * RET_DV:(h + 1) * RET_DV]
            k_state = (kh.astype(F32) * csc_ref[d, h]).astype(BF16)
            st_ref[h] = st_ref[h] * gam_ref[d, h] + _dot_tn(k_state, vh)


def _retention(q, k, v, decay_logit, n_ctx):
    bsz, t, _ = q.shape
    c = RET_CHUNK
    nc, ncc = t // c, n_ctx // c
    nl = nc - ncc
    log_gamma = jax.nn.log_sigmoid(decay_logit.astype(F32))[:, :, None, None]
    i = jnp.arange(c, dtype=F32)
    lag = i[:, None] - i[None, :]
    lag = jnp.stack([lag, -lag])[:, None]
    dmat = jnp.where(lag >= 0, jnp.exp(log_gamma * jnp.maximum(lag, 0.0)), 0.0)
    done = jnp.stack([i + 1.0, c - i])[:, None, :, None]
    rsc = jnp.exp(log_gamma * done)
    csc = jnp.exp(log_gamma * (c - done))
    gam = jnp.exp(log_gamma[:, :, 0, 0] * c)
    fwd = lambda b, n: (b, n, 0)
    bwd = lambda b, n: (b, _backward_chunk(n, ncc, nc), 0)
    o_fwd = lambda b, n: (b, jnp.maximum(n - ncc, 0), 0)
    o_bwd = lambda b, n: (b, nl - 1 - jnp.maximum(n - ncc, 0), 0)

    def specs(idx):
        return [pl.BlockSpec((1, c, RET_QK), idx), pl.BlockSpec((1, c, RET_QK), idx), pl.BlockSpec((1, c, RET_MIX), idx)]

    return pl.pallas_call(
        functools.partial(_ret_body, ncc),
        grid=(bsz, nc),
        in_specs=specs(fwd) + specs(bwd) + [_const_spec(dmat.shape), _const_spec(rsc.shape), _const_spec(csc.shape),
                                            pl.BlockSpec(memory_space=pltpu.SMEM)],
        out_specs=[pl.BlockSpec((1, c, RET_MIX), o_fwd), pl.BlockSpec((1, c, RET_MIX), o_bwd)],
        out_shape=[jax.ShapeDtypeStruct((bsz, nl * c, RET_MIX), BF16)] * 2,
        scratch_shapes=[pltpu.VMEM((RET_HEADS, RET_DK, RET_DV), F32)] * 2,
        compiler_params=_params("arbitrary", "arbitrary"),
        name="retention_scan",
    )(q, k, v, q, k, v, dmat, rsc, csc, gam)


def _route(x, mixed, mod, n2g_ref, wr_ref, br_ref, x1_ref, h2_ref, e_ref, w_ref, r_ref, cnt_ref):
    tm = x.shape[0]
    x1 = x + mod[2:3] * mixed
    x1_ref[0] = x1
    h2 = _norm_mod(x1, n2g_ref[...], mod[3:4], mod[4:5])
    h2_ref[0] = _pack_rows(h2)
    logits = _dot3(wr_ref[...], h2, dot=_dot_nt) + br_ref[...]
    ie = lax.broadcasted_iota(I32, logits.shape, 0)
    tops, picks = [], []
    for _ in range(TOP_K):
        mx = jnp.max(logits, axis=0, keepdims=True)
        pick = jnp.min(jnp.where(logits == mx, ie, N_EXPERTS), axis=0, keepdims=True)
        tops.append(mx)
        picks.append(pick)
        logits = jnp.where(ie == pick, -jnp.inf, logits)
    ex = [jnp.exp(tk - tops[0]) for tk in tops]
    den = ex[0] + ex[1] + ex[2] + ex[3]
    for kk in range(TOP_K):
        w_ref[0, kk:kk + 1, :] = ex[kk] / den
        e_ref[0, kk:kk + 1, :] = picks[kk]

    @pl.when((pl.program_id(0) == 0) & (pl.program_id(1) == 0))
    def _():
        cnt_ref[...] = jnp.zeros_like(cnt_ref)

    earlier = (lax.broadcasted_iota(I32, (tm, tm), 0) < lax.broadcasted_iota(I32, (tm, tm), 1))
    earlier = jnp.where(earlier, 1.0, 0.0).astype(BF16)
    run = cnt_ref[:, 0:1]
    for kk, pick in enumerate(picks):
        onehot = jnp.where(ie == pick, 1.0, 0.0)
        before = _dot(onehot.astype(BF16), earlier) + run
        r_ref[0, kk:kk + 1, :] = jnp.sum(onehot * before, axis=0, keepdims=True).astype(I32)
        run = run + jnp.sum(onehot, axis=1, keepdims=True)
    cnt_ref[...] = jnp.broadcast_to(run, cnt_ref.shape)


def _mix0_body(nct, ctx_ref, lat_ref, mod_ref, of_ref, ob_ref, g_ref, ys_ref, u_ref, gng_ref, dsk_ref, gluw_ref,
               glub_ref, wo_ref, n2g_ref, wr_ref, br_ref, x1_ref, h2_ref, e_ref, w_ref, r_ref, cnt_ref):
    o = of_ref[0] + ob_ref[0]
    heads = []
    for h in range(GLA_HEADS):
        oh = o[:, h * GLA_DV:(h + 1) * GLA_DV]
        heads.append(oh * lax.rsqrt(jnp.mean(oh * oh, axis=-1, keepdims=True) + EPS))
    gla = jnp.concatenate(heads, axis=1) * gng_ref[...] * _silu(g_ref[0].astype(F32))
    lane_blocks = lambda ref: jnp.concatenate([ref[qb, 0] for qb in range(ref.shape[0])], axis=1)
    y = jax.nn.gelu(lane_blocks(ys_ref) + dsk_ref[...] * lane_blocks(u_ref))
    y = y * jax.nn.sigmoid(_dot(y.astype(BF16), gluw_ref[...]) + glub_ref[...])
    mixed = _dot(gla.astype(BF16), wo_ref[0:AB_V]) + _dot(y.astype(BF16), wo_ref[AB_V:AB_V + S5_CH])
    _route(_stream_tile(nct, ctx_ref, lat_ref), mixed, mod_ref[0, 0], n2g_ref, wr_ref, br_ref,
           x1_ref, h2_ref, e_ref, w_ref, r_ref, cnt_ref)


def _mix1_body(x_ref, mod_ref, of_ref, ob_ref, g_ref, ng_ref, wo_ref, n2g_ref, wr_ref, br_ref,
               x1_ref, h2_ref, e_ref, w_ref, r_ref, cnt_ref):
    mixed = None
    for h in range(RET_HEADS):
        sl = slice(h * RET_DV, (h + 1) * RET_DV)
        oh = of_ref[0, :, sl].astype(F32) + ob_ref[0, :, sl].astype(F32)
        mu = jnp.mean(oh, axis=-1, keepdims=True)
        cen = oh - mu
        var = jnp.mean(cen * cen, axis=-1, keepdims=True)
        gated = cen * lax.rsqrt(var + EPS) * ng_ref[:, sl] * _silu(g_ref[0, :, sl].astype(F32))
        part = _dot(gated.astype(BF16), wo_ref[sl])
        mixed = part if mixed is None else mixed + part
    _route(x_ref[0], mixed, mod_ref[0, 0], n2g_ref, wr_ref, br_ref, x1_ref, h2_ref, e_ref, w_ref, r_ref, cnt_ref)


def _mix_call(body, name, stream, mods, tiles, acts, consts, norm2_g, w_router, b_router, n_tok, seg_tile0):
    bsz, _, d = stream[-1].shape
    tm = TOKEN_TILE
    off = lambda b, i: (b, i + seg_tile0, 0)
    loc = lambda b, i: (b, i, 0)
    ntl = bsz * tiles
    flat = lambda b, i: (b * tiles + i, 0, 0)
    in_specs = list(_split_specs(n_tok, d)) if len(stream) == 2 else [pl.BlockSpec((1, tm, d), off)]
    in_specs.append(pl.BlockSpec((1, 1, 6, d), lambda b, i: (b, ((i + seg_tile0) >= n_tok).astype(I32), 0, 0)))
    args = list(stream) + [mods]
    for arr, offset in acts:
        if arr.ndim == 4:
            in_specs.append(pl.BlockSpec((arr.shape[0], 1, tm, arr.shape[3]), lambda b, i: (0, b, i, 0)))
        else:
            in_specs.append(pl.BlockSpec((1, tm, arr.shape[2]), off if offset else loc))
        args.append(arr)
    tail = list(consts) + [norm2_g.reshape(1, d), w_router.T, b_router.reshape(N_EXPERTS, 1)]
    in_specs += [_const_spec(a.shape) for a in tail]
    args += tail
    tok_out = pl.BlockSpec((1, TOP_K, tm), flat)
    return pl.pallas_call(
        body,
        grid=(bsz, tiles),
        in_specs=in_specs,
        out_specs=[pl.BlockSpec((1, tm, d), loc), pl.BlockSpec((1, tm, d // 2), loc), tok_out, tok_out, tok_out,
                   _const_spec((N_EXPERTS, 128))],
        out_shape=[jax.ShapeDtypeStruct((bsz, tiles * tm, d), F32), jax.ShapeDtypeStruct((bsz, tiles * tm, d // 2), U32),
                   jax.ShapeDtypeStruct((ntl, TOP_K, tm), I32), jax.ShapeDtypeStruct((ntl, TOP_K, tm), F32),
                   jax.ShapeDtypeStruct((ntl, TOP_K, tm), I32), jax.ShapeDtypeStruct((N_EXPERTS, 128), F32)],
        compiler_params=_params("arbitrary", "arbitrary"),
        name=name,
    )(*args)


def _cast_rows(src_ref, dst_ref, rows):
    def chunk(j, carry):
        r = pl.multiple_of(j * rows, rows)
        dst_ref[pl.ds(r, rows), :] = src_ref[0, 0, pl.ds(r, rows), :].astype(BF16)
        return carry

    lax.fori_loop(0, dst_ref.shape[0] // rows, chunk, 0)


def _expert_body(be_ref, nu_ref, x_ref, wgu_ref, bgu_ref, wd_ref, bd_ref, o_ref, wgu_bf, wd_bf):
    i = pl.program_id(0)
    live = i < nu_ref[0]
    new_expert = (i == 0) | (be_ref[i] != be_ref[jnp.maximum(i - 1, 0)])

    @pl.when(live & new_expert)
    def _():
        _cast_rows(wgu_ref, wgu_bf, 128)
        _cast_rows(wd_ref, wd_bf, 128)

    @pl.when(live)
    def _():
        x_lo, x_hi = _unpack_rows(x_ref[...])
        half = x_lo.shape[1]
        gu = (_dot(x_lo.astype(BF16), wgu_bf[0:half]) + _dot(x_hi.astype(BF16), wgu_bf[half:2 * half])
              + bgu_ref[0, 0])
        gate = jnp.minimum(gu[:, :D_FF], SWIGLU_LIMIT)
        lin = jnp.clip(gu[:, D_FF:], -SWIGLU_LIMIT, SWIGLU_LIMIT)
        act = gate * jax.nn.sigmoid(SWIGLU_ALPHA * gate) * (lin + 1.0)
        y = _dot(act.astype(BF16), wd_bf[...]) + bd_ref[0, 0]
        o_ref[...] = _pack_rows(y)

    @pl.when(i >= nu_ref[0])
    def _():
        o_ref[...] = jnp.zeros_like(o_ref)


def _experts(xb, block_e, n_used, layer, w_gu, b_gu, w_down, b_down):
    n_slots, half = xb.shape
    d = 2 * half
    n_blocks = n_slots // MOE_BLOCK
    depth = w_gu.shape[0]
    by_expert = lambda i, be, nu: (layer, be[i], 0, 0)
    return pl.pallas_call(
        _expert_body,
        grid_spec=pltpu.PrefetchScalarGridSpec(
            num_scalar_prefetch=2,
            grid=(n_blocks,),
            in_specs=[pl.BlockSpec((MOE_BLOCK, half), lambda i, be, nu: (i, 0)),
                      pl.BlockSpec((1, 1, d, 2 * D_FF), by_expert), pl.BlockSpec((1, 1, 1, 2 * D_FF), by_expert),
                      pl.BlockSpec((1, 1, D_FF, d), by_expert), pl.BlockSpec((1, 1, 1, d), by_expert)],
            out_specs=pl.BlockSpec((MOE_BLOCK, half), lambda i, be, nu: (i, 0)),
            scratch_shapes=[pltpu.VMEM((d, 2 * D_FF), BF16), pltpu.VMEM((D_FF, d), BF16)]),
        out_shape=jax.ShapeDtypeStruct((n_slots, half), U32),
        compiler_params=_params("arbitrary"),
        name="moe_experts",
    )(block_e, n_used, xb, w_gu, b_gu.reshape(depth, N_EXPERTS, 1, 2 * D_FF), w_down,
      b_down.reshape(depth, N_EXPERTS, 1, d))


def _combine_body(final, x1_ref, mod_ref, yk_ref, w_ref, *refs):
    d = x1_ref.shape[2]
    half = d // 2
    y_lo, y_hi = None, None
    for k in range(TOP_K):
        lo, hi = _unpack_rows(yk_ref[k, 0])
        wk = w_ref[0, :, k:k + 1]
        y_lo = lo * wk if y_lo is None else y_lo + lo * wk
        y_hi = hi * wk if y_hi is None else y_hi + hi * wk
    g2 = mod_ref[0, 0][5:6]
    x2_lo = x1_ref[0, :, 0:half] + g2[:, 0:half] * y_lo
    x2_hi = x1_ref[0, :, half:d] + g2[:, half:d] * y_hi
    if final:
        fg_ref, o_ref = refs
        ms = (jnp.sum(x2_lo * x2_lo, axis=-1, keepdims=True) + jnp.sum(x2_hi * x2_hi, axis=-1, keepdims=True)) / d
        r = lax.rsqrt(ms + EPS)
        x2_lo = x2_lo * r * fg_ref[:, 0:half]
        x2_hi = x2_hi * r * fg_ref[:, half:d]
    else:
        (o_ref,) = refs
    o_ref[0, :, 0:half] = x2_lo
    o_ref[0, :, half:d] = x2_hi


def _combine(x1, mods, yk, w_tok, seg_tile0, n_tok, final_g):
    bsz, t, d = x1.shape
    tm = TOKEN_TILE
    tiles = t // tm
    loc = lambda b, i: (b, i, 0)
    in_specs = [pl.BlockSpec((1, tm, d), loc),
                pl.BlockSpec((1, 1, 6, d), lambda b, i: (b, ((i + seg_tile0) >= n_tok).astype(I32), 0, 0)),
                pl.BlockSpec((TOP_K, 1, tm, d // 2), lambda b, i: (0, b, i, 0)),
                pl.BlockSpec((1, tm, TOP_K), loc)]
    args = [x1, mods, yk.reshape(TOP_K, bsz, t, d // 2), w_tok.reshape(bsz, t, TOP_K)]
    if final_g is not None:
        in_specs.append(_const_spec((1, d)))
        args.append(final_g.reshape(1, d))
    return pl.pallas_call(
        functools.partial(_combine_body, final_g is not None),
        grid=(bsz, tiles),
        in_specs=in_specs,
        out_specs=pl.BlockSpec((1, tm, d), loc),
        out_shape=jax.ShapeDtypeStruct((bsz, t, d), F32),
        compiler_params=_params("arbitrary", "arbitrary"),
        name="moe_combine",
    )(*args)


def _sc_mesh():
    return plsc.VectorSubcoreMesh(core_axis_name="core", subcore_axis_name="subcore",
                                  num_cores=SC_CORES, num_subcores=SC_SUBCORES)


def _sc_worker_base(per_worker):
    return (lax.axis_index("subcore") * SC_CORES + lax.axis_index("core")) * per_worker


def _sc_dispatch(rows, dest, n_slots):
    n, w = rows.shape
    chunk = SC_CHUNK // 2
    per_worker = n // SC_WORKERS
    n_chunks = per_worker // chunk
    assert per_worker * SC_WORKERS == n and n_chunks * chunk == per_worker and n_chunks % 2 == 0

    @functools.partial(
        pl.kernel, mesh=_sc_mesh(), out_type=jax.ShapeDtypeStruct((n_slots, w), rows.dtype),
        scratch_types=([pltpu.VMEM((chunk,), I32)] * (2 * TOP_K) + [pltpu.VMEM((chunk, w), rows.dtype)] * 2
                       + [pltpu.SemaphoreType.DMA] * 2),
        name="moe_dispatch")
    def scatter_rows(rows_hbm, dest_hbm, out_hbm, *scratch):
        slots = [(scratch[s * TOP_K:(s + 1) * TOP_K], scratch[2 * TOP_K + s], scratch[2 * TOP_K + 2 + s])
                 for s in range(2)]
        base0 = _sc_worker_base(per_worker)

        def load(j, slot):
            idx_refs, buf, _ = slot
            base = base0 + j * chunk
            pltpu.sync_copy(rows_hbm.at[pl.ds(base, chunk)], buf)
            for k, idx in enumerate(idx_refs):
                pltpu.sync_copy(dest_hbm.at[pl.ds(k * n + base, chunk)], idx)

        def scatters(slot):
            idx_refs, buf, sem = slot
            return [pltpu.make_async_copy(buf, out_hbm.at[idx], sem) for idx in idx_refs]

        load(0, slots[0])

        @pl.loop(0, n_chunks, step=2)
        def _(j):
            for cp in scatters(slots[0]):
                cp.start()
            load(j + 1, slots[1])
            for cp in scatters(slots[0]):
                cp.wait()
            for cp in scatters(slots[1]):
                cp.start()

            @pl.when(j + 2 < n_chunks)
            def _():
                load(j + 2, slots[0])
            for cp in scatters(slots[1]):
                cp.wait()

    return scatter_rows(rows, dest)


def _sc_gather(table, idx):
    n = idx.shape[0]
    w = table.shape[1]
    per_worker = n // SC_WORKERS
    n_chunks = per_worker // SC_CHUNK
    assert per_worker * SC_WORKERS == n and n_chunks * SC_CHUNK == per_worker and n_chunks % 2 == 0

    @functools.partial(
        pl.kernel, mesh=_sc_mesh(), out_type=jax.ShapeDtypeStruct((n, w), table.dtype),
        scratch_types=([pltpu.VMEM((SC_CHUNK,), I32)] * 2 + [pltpu.VMEM((SC_CHUNK, w), table.dtype)] * 2
                       + [pltpu.SemaphoreType.DMA] * 4),
        name="moe_gather")
    def gather_rows(table_hbm, idx_hbm, out_hbm, idx0, idx1, buf0, buf1, gsem0, gsem1, wsem0, wsem1):
        base0 = _sc_worker_base(per_worker)

        def gather_copy(idx_v, buf, sem):
            return pltpu.make_async_copy(table_hbm.at[idx_v], buf, sem)

        def write_copy(j, buf, sem):
            return pltpu.make_async_copy(buf, out_hbm.at[pl.ds(base0 + j * SC_CHUNK, SC_CHUNK)], sem)

        def start_gather(j, idx_v, buf, sem):
            pltpu.sync_copy(idx_hbm.at[pl.ds(base0 + j * SC_CHUNK, SC_CHUNK)], idx_v)
            gather_copy(idx_v, buf, sem).start()

        start_gather(0, idx0, buf0, gsem0)

        @pl.loop(0, n_chunks, step=2)
        def _(j):
            @pl.when(j > 0)
            def _():
                write_copy(j - 1, buf1, wsem1).wait()
            start_gather(j + 1, idx1, buf1, gsem1)
            gather_copy(idx0, buf0, gsem0).wait()
            write_copy(j, buf0, wsem0).start()

            @pl.when(j + 2 < n_chunks)
            def _():
                write_copy(j, buf0, wsem0).wait()
                start_gather(j + 2, idx0, buf0, gsem0)
            gather_copy(idx1, buf1, gsem1).wait()
            write_copy(j + 1, buf1, wsem1).start()

        write_copy(n_chunks - 2, buf0, wsem0).wait()
        write_copy(n_chunks - 1, buf1, wsem1).wait()

    return gather_rows(table, idx)


def _moe(h2, e_tl, w_tl, r_tl, cnt, layer, w_gu, b_gu, w_down, b_down):
    bsz, t, half = h2.shape
    n = bsz * t
    flat = lambda a: a.transpose(1, 0, 2).reshape(TOP_K, n)
    e_k, w_k, r_k = flat(e_tl), flat(w_tl), flat(r_tl)
    counts = cnt[:, 0].astype(I32)
    padded = (counts + MOE_BLOCK - 1) // MOE_BLOCK * MOE_BLOCK
    pad_end = jnp.cumsum(padded)
    pad_start = pad_end - padded
    n_blocks = (n * TOP_K + MOE_BLOCK - 1) // MOE_BLOCK + N_EXPERTS
    block_start = jnp.arange(n_blocks, dtype=I32) * MOE_BLOCK
    block_e = jnp.minimum(jnp.sum((pad_end[None, :] <= block_start[:, None]).astype(I32), axis=1), N_EXPERTS - 1)
    n_used = (pad_end[-1:] // MOE_BLOCK).astype(I32)
    start_k = jnp.sum(jnp.where(e_k[..., None] == jnp.arange(N_EXPERTS, dtype=I32), pad_start, 0), axis=-1)
    dest = (start_k + r_k).reshape(TOP_K * n)
    xb = _sc_dispatch(h2.reshape(n, half), dest, n_blocks * MOE_BLOCK)
    yb = _experts(xb, block_e, n_used, layer, w_gu, b_gu, w_down, b_down)
    return _sc_gather(yb, dest), w_k.T


def _rope_tables(n_ctx, n_lat):
    n_freq = RET_DK // 4
    inv_freq = ROPE_BASE ** (-jnp.arange(n_freq, dtype=F32) / n_freq)
    pos = jnp.arange(n_lat, dtype=I32)
    cos, sin = [], []
    for p in (pos // GRID_W, pos % GRID_W):
        ang = p.astype(F32)[:, None] * inv_freq
        cos += [jnp.cos(ang), jnp.cos(ang)]
        sin += [-jnp.sin(ang), jnp.sin(ang)]
    cos, sin = jnp.concatenate(cos, axis=1), jnp.concatenate(sin, axis=1)
    return (jnp.concatenate([jnp.ones((n_ctx, RET_DK), F32), cos], axis=0),
            jnp.concatenate([jnp.zeros((n_ctx, RET_DK), F32), sin], axis=0))


def kernel(x, c, ctx, c_ctx, ada_w, ada_b, norm1_g, norm2_g, ab_w_in, ab_w_out, gla_wa, gla_ba, gla_norm_g, s5_lam_re, s5_lam_im, s5_log_step, s5_b_re, s5_b_im, s5_c_re, s5_c_im, s5_d, s5_glu_w, s5_glu_b, ret_w_in, ret_w_out, ret_decay_logit, ret_norm_g, moe_w_router, moe_b_router, moe_w_gu, moe_b_gu, moe_w_down, moe_b_down, final_norm_g):
    bsz, n_lat, d = x.shape
    n_ctx = ctx.shape[1]
    depth = ada_w.shape[0]
    assert depth == 2 and d == D_MODEL and bsz == 8, "kernels are laid out for the stated problem shape"
    assert n_ctx % TOKEN_TILE == 0 and n_lat % TOKEN_TILE == 0 and n_lat % GRID_W == 0
    t = n_ctx + n_lat
    nct = n_ctx // TOKEN_TILE

    cvec = jnp.zeros((16, d), F32).at[:bsz].set(c).at[bsz].set(c_ctx)
    mod = _ada_mod(cvec, ada_w, ada_b).reshape(depth, 16, 6, d)
    mods = [jnp.stack([jnp.broadcast_to(mod[l, bsz], (bsz, 6, d)), mod[l, :bsz]], axis=1) for l in range(depth)]

    w_in = ab_w_in[0].astype(BF16)
    cuts = [0, AB_QK, 2 * AB_QK, 2 * AB_QK + AB_V, 2 * AB_QK + 2 * AB_V, 2 * AB_QK + 2 * AB_V + 2 * GLA_RANK,
            w_in.shape[1]]
    pieces = [w_in[:, a:b] for a, b in zip(cuts[:-1], cuts[1:])]
    wa_pad = jnp.zeros((2, 2 * GLA_RANK, AB_QK), F32)
    wa_pad = wa_pad.at[0, :GLA_RANK].set(gla_wa[0, 0]).at[1, GLA_RANK:].set(gla_wa[0, 1])
    outs = _inproj0(ctx, x, mods[0], norm1_g[0], pieces, wa_pad, gla_ba[0].reshape(2, 1, AB_QK), nct)
    v, g, u = outs[8:]
    o_f, o_b = _gla((outs[0:4], outs[4:8]), v, n_ctx)
    ops = _s5_operators(s5_lam_re[0], s5_lam_im[0], s5_log_step[0], s5_b_re[0], s5_b_im[0], s5_c_re[0], s5_c_im[0])
    ys = _s5(u, ops, n_ctx)
    consts = [jnp.tile(gla_norm_g[0], GLA_HEADS).reshape(1, AB_V), s5_d[0].reshape(1, S5_CH),
              s5_glu_w[0].astype(BF16), s5_glu_b[0].reshape(1, S5_CH), ab_w_out[0].astype(BF16)]
    x1, h2, e_tl, w_tl, r_tl, cnt = _mix_call(
        functools.partial(_mix0_body, nct), "mix_gla_s5", (ctx, x), mods[0], t // TOKEN_TILE,
        [(o_f, False), (o_b, False), (g, False), (ys, False), (u, False)], consts,
        norm2_g[0], moe_w_router[0], moe_b_router[0], nct, 0)
    yk, w_tok = _moe(h2, e_tl, w_tl, r_tl, cnt, 0, moe_w_gu, moe_b_gu, moe_w_down, moe_b_down)
    x_all = _combine(x1, mods[0], yk, w_tok, 0, nct, None)

    w_in = ret_w_in[0].astype(BF16)
    cuts = [0, RET_QK, 2 * RET_QK, 2 * RET_QK + RET_MIX, w_in.shape[1]]
    pieces = [w_in[:, a:b] for a, b in zip(cuts[:-1], cuts[1:])]
    cos_t, sin_t = _rope_tables(n_ctx, n_lat)
    q, k, v, g = _inproj1(x_all, mods[1], norm1_g[1], cos_t, sin_t, pieces, nct)
    o_f, o_b = _retention(q, k, v, ret_decay_logit[0], n_ctx)
    consts = [ret_norm_g[0].reshape(1, RET_MIX), ret_w_out[0].astype(BF16)]
    x1, h2, e_tl, w_tl, r_tl, cnt = _mix_call(
        _mix1_body, "mix_retention", (x_all,), mods[1], n_lat // TOKEN_TILE,
        [(o_f, False), (o_b, False), (g, True)], consts,
        norm2_g[1], moe_w_router[1], moe_b_router[1], nct, nct)
    yk, w_tok = _moe(h2, e_tl, w_tl, r_tl, cnt, 1, moe_w_gu, moe_b_gu, moe_w_down, moe_b_down)
    return _combine(x1, mods[1], yk, w_tok, nct, nct, final_norm_g)
```

```python
import functools
import math

import jax
import jax.numpy as jnp
from jax import lax
from jax.experimental import pallas as pl
from jax.experimental.pallas import tpu as pltpu
from jax.experimental.pallas import tpu_sc as plsc

F32, BF16, I32, U32 = jnp.float32, jnp.bfloat16, jnp.int32, jnp.uint32

D_MODEL = 1024
GRID_W = 64
EPS = 1e-6
GLA_HEADS, GLA_DK, GLA_DV, GLA_RANK, GLA_TAU, GLA_CHUNK = 4, 64, 128, 16, 16.0, 64
GLA_BATCH = 4
AB_QK, AB_V = GLA_HEADS * GLA_DK, GLA_HEADS * GLA_DV
S5_CH, S5_GROUP, S5_GROUPS, S5_P = 512, 16, 32, 64
S5_CHUNK = 16
S5_FOLD_BATCH = 4
S5_SCAN_GROUPS = 2
RET_HEADS, RET_DK, RET_DV = 4, 256, 512
RET_CHUNK = 256
RET_QK, RET_MIX = RET_HEADS * RET_DK, RET_HEADS * RET_DV
ROPE_BASE = 10000.0
N_EXPERTS, TOP_K, D_FF = 32, 4, 1024
SWIGLU_LIMIT, SWIGLU_ALPHA = 7.0, 1.702
MOE_BLOCK = 512
TOKEN_TILE = 256
ADA_TILE = 768
VMEM_LIMIT = 56 * 1024 * 1024
SC_CORES, SC_SUBCORES = 2, 16
SC_WORKERS = SC_CORES * SC_SUBCORES
SC_CHUNK = 64
COMBINE_PARTS = 2


def _params(*sem):
    return pltpu.CompilerParams(dimension_semantics=sem, vmem_limit_bytes=VMEM_LIMIT)


def _dot(a, b):
    return jnp.dot(a, b, preferred_element_type=F32)


def _dot_nt(a, b):
    return lax.dot_general(a, b, (((1,), (1,)), ((), ())), preferred_element_type=F32)


def _dot_tn(a, b):
    return lax.dot_general(a, b, (((0,), (0,)), ((), ())), preferred_element_type=F32)


def _split(a):
    hi = a.astype(BF16)
    return hi, (a - hi.astype(F32)).astype(BF16)


def _dot3(a, b, dot=_dot):
    ah, al = _split(a)
    bh, bl = _split(b)
    return dot(ah, bh) + (dot(ah, bl) + dot(al, bh))


def _pack_rows(x):
    h = x.shape[1] // 2
    lo = lax.bitcast_convert_type(x[:, 0:h].astype(BF16).astype(F32), U32)
    hi = lax.bitcast_convert_type(x[:, h:2 * h].astype(BF16).astype(F32), U32)
    return hi | (lo >> 16)


def _unpack_rows(p):
    lo = lax.bitcast_convert_type(p << 16, F32)
    hi = lax.bitcast_convert_type(p & jnp.uint32(0xFFFF0000), F32)
    return lo, hi


def _silu(x):
    return x * jax.nn.sigmoid(x)


def _norm_mod(x, g, shift, scale):
    r = lax.rsqrt(jnp.mean(x * x, axis=-1, keepdims=True) + EPS)
    return (x * r * g) * (1.0 + scale) + shift


def _const_spec(shape):
    nd = len(shape)
    return pl.BlockSpec(shape, lambda *_: (0,) * nd)


def _ada_body(c_ref, w_ref, b_ref, o_ref):
    o_ref[0] = _dot3(_silu(c_ref[...]), w_ref[0]) + b_ref[0]


def _ada_mod(cvec, ada_w, ada_b):
    depth, d, n6 = ada_w.shape
    rows = cvec.shape[0]
    return pl.pallas_call(
        _ada_body,
        grid=(depth, n6 // ADA_TILE),
        in_specs=[_const_spec((rows, d)),
                  pl.BlockSpec((1, d, ADA_TILE), lambda l, j: (l, 0, j)),
                  pl.BlockSpec((1, 1, ADA_TILE), lambda l, j: (l, 0, j))],
        out_specs=pl.BlockSpec((1, rows, ADA_TILE), lambda l, j: (l, 0, j)),
        out_shape=jax.ShapeDtypeStruct((depth, rows, n6), F32),
        compiler_params=_params("arbitrary", "arbitrary"),
        name="ada_mod",
    )(cvec, ada_w, ada_b.reshape(depth, 1, n6))


def _tile_specs(nct, d):
    x_spec = pl.BlockSpec((1, TOKEN_TILE, d), lambda b, i: (b, i, 0))
    mod_spec = pl.BlockSpec((1, 1, 6, d), lambda b, i: (b, (i >= nct).astype(I32), 0, 0))
    return x_spec, mod_spec


def _split_specs(nct, d):
    ctx_spec = pl.BlockSpec((1, TOKEN_TILE, d), lambda b, i: (b, jnp.minimum(i, nct - 1), 0))
    lat_spec = pl.BlockSpec((1, TOKEN_TILE, d), lambda b, i: (b, jnp.maximum(i - nct, 0), 0))
    return ctx_spec, lat_spec


def _stream_tile(nct, ctx_ref, lat_ref):
    return jnp.where(pl.program_id(1) < nct, ctx_ref[0], lat_ref[0])


def _inproj0_body(nct, ctx_ref, lat_ref, mod_ref, g_ref, wq, wk, wv, wg, wlow, wu, wa_ref, ba_ref, tri_ref, ones_ref,
                  qd_f, ki_f, ks_f, ed_f, qd_b, ki_b, ks_b, ed_b, ov, og, ou):
    m = mod_ref[0, 0]
    h = _norm_mod(_stream_tile(nct, ctx_ref, lat_ref), g_ref[...], m[0:1], m[1:2]).astype(BF16)
    ov[0] = _dot(h, wv[...]).astype(ov.dtype)
    og[0] = _dot(h, wg[...]).astype(og.dtype)
    u = _dot(h, wu[...])
    for qb in range(ou.shape[0]):
        ou[qb, 0] = u[:, qb * 128:(qb + 1) * 128]
    q = _dot(h, wq[...]) * (GLA_DK ** -0.5)
    k = _dot(h, wk[...])
    low = _dot(h, wlow[...])
    outs = ((qd_f, ki_f, ks_f, ed_f), (qd_b, ki_b, ks_b, ed_b))
    for d, (qd_ref, ki_ref, ks_ref, ed_ref) in enumerate(outs):
        z = _dot3(low, wa_ref[d]) + ba_ref[d]
        log_a = (jnp.minimum(z, 0.0) - jnp.log1p(jnp.exp(-jnp.abs(z)))) * (1.0 / GLA_TAU)
        la_hi, la_lo = _split(log_a)
        cum = _dot(tri_ref[d], la_hi) + _dot(tri_ref[d], la_lo)
        tot = _dot(ones_ref[...], la_hi) + _dot(ones_ref[...], la_lo)
        qd_ref[0] = (q * jnp.exp(cum)).astype(BF16)
        ki_ref[0] = (k * jnp.exp(-cum)).astype(BF16)
        ks_ref[0] = (k * jnp.exp(tot - cum)).astype(BF16)
        for ch in range(TOKEN_TILE // GLA_CHUNK):
            ed_ref[0, ch] = jnp.exp(tot[ch * GLA_CHUNK:ch * GLA_CHUNK + 1])


def _inproj0(ctx, x, mods, norm_g, weights, wa_pad, ba, nct):
    bsz, n_lat, d = x.shape
    t = ctx.shape[1] + n_lat
    tm = TOKEN_TILE
    _, mod_spec = _tile_specs(nct, d)
    ctx_spec, lat_spec = _split_specs(nct, d)
    pos = jnp.arange(tm)
    same_chunk = (pos[:, None] // GLA_CHUNK) == (pos[None, :] // GLA_CHUNK)
    tri = jnp.stack([same_chunk & (pos[None, :] <= pos[:, None]),
                     same_chunk & (pos[None, :] >= pos[:, None])]).astype(BF16)
    ones = same_chunk.astype(BF16)
    consts = list(weights) + [wa_pad, ba, tri, ones]
    tok = lambda w, dt: (pl.BlockSpec((1, tm, w), lambda b, i: (b, i, 0)), jax.ShapeDtypeStruct((bsz, t, w), dt))
    per_chunk = (pl.BlockSpec((1, tm // GLA_CHUNK, 1, AB_QK), lambda b, i: (b, i, 0, 0)),
                 jax.ShapeDtypeStruct((bsz, t // GLA_CHUNK, 1, AB_QK), F32))
    one_dir = [tok(AB_QK, BF16)] * 3 + [per_chunk]
    u_blocks = (pl.BlockSpec((S5_CH // 128, 1, tm, 128), lambda b, i: (0, b, i, 0)),
                jax.ShapeDtypeStruct((S5_CH // 128, bsz, t, 128), F32))
    outs = one_dir + one_dir + [tok(AB_V, BF16), tok(AB_V, BF16), u_blocks]
    return pl.pallas_call(
        functools.partial(_inproj0_body, nct),
        grid=(bsz, t // tm),
        in_specs=[ctx_spec, lat_spec, mod_spec, _const_spec((1, d))] + [_const_spec(a.shape) for a in consts],
        out_specs=[o[0] for o in outs],
        out_shape=[o[1] for o in outs],
        compiler_params=_params("arbitrary", "arbitrary"),
        name="inproj_gla_s5",
    )(ctx, x, mods, norm_g.reshape(1, d), *consts)


def _rope(acc, cos_ref, sin_ref, o_ref, scale):
    for grp in range(acc.shape[1] // 128):
        half = grp % 2
        xg = acc[:, grp * 128:(grp + 1) * 128]
        cs = cos_ref[:, half * 128:(half + 1) * 128]
        sn = sin_ref[:, half * 128:(half + 1) * 128]
        out = xg * cs + pltpu.roll(xg, 64, 1) * sn
        o_ref[0, :, grp * 128:(grp + 1) * 128] = (out * scale).astype(o_ref.dtype)


def _moe_residual(x1_ref, g2, yk_ref, w_ref):
    d = x1_ref.shape[2]
    half = d // 2
    y_lo, y_hi = None, None
    for k in range(TOP_K):
        lo, hi = _unpack_rows(yk_ref[k, 0])
        wk = w_ref[0, :, k:k + 1]
        y_lo = lo * wk if y_lo is None else y_lo + lo * wk
        y_hi = hi * wk if y_hi is None else y_hi + hi * wk
    return x1_ref[0, :, 0:half] + g2[:, 0:half] * y_lo, x1_ref[0, :, half:d] + g2[:, half:d] * y_hi


def _inproj1_body(x1_ref, mod0_ref, yk_ref, w_ref, mod_ref, g_ref, cos_ref, sin_ref, wq, wk, wv, wg, ox, oq, ok, ov, og):
    half = x1_ref.shape[2] // 2
    x2_lo, x2_hi = _moe_residual(x1_ref, mod0_ref[0, 0][5:6], yk_ref, w_ref)
    ox[0, :, 0:half] = x2_lo
    ox[0, :, half:2 * half] = x2_hi
    m = mod_ref[0, 0]
    h = _norm_mod(jnp.concatenate([x2_lo, x2_hi], axis=1), g_ref[...], m[0:1], m[1:2]).astype(BF16)
    _rope(_dot(h, wq[...]), cos_ref, sin_ref, oq, 1.0)
    _rope(_dot(h, wk[...]), cos_ref, sin_ref, ok, RET_DK ** -0.5)
    ov[0] = _dot(h, wv[...]).astype(ov.dtype)
    og[0] = _dot(h, wg[...]).astype(og.dtype)


def _inproj1(x1, mods0, yk, w_tok, mods, norm_g, cos_t, sin_t, weights, nct):
    bsz, t, d = x1.shape
    tm = TOKEN_TILE
    x_spec, mod_spec = _tile_specs(nct, d)
    tok = lambda w: pl.BlockSpec((1, tm, w), lambda b, i: (b, i, 0))
    tab_spec = pl.BlockSpec((tm, RET_DK), lambda b, i: (i, 0))
    return pl.pallas_call(
        _inproj1_body,
        grid=(bsz, t // tm),
        in_specs=[x_spec, mod_spec, pl.BlockSpec((TOP_K, 1, tm, d // 2), lambda b, i: (0, b, i, 0)), tok(TOP_K),
                  mod_spec, _const_spec((1, d)), tab_spec, tab_spec] + [_const_spec(w.shape) for w in weights],
        out_specs=[tok(d)] + [tok(w.shape[1]) for w in weights],
        out_shape=[jax.ShapeDtypeStruct((bsz, t, d), F32)]
        + [jax.ShapeDtypeStruct((bsz, t, w.shape[1]), BF16) for w in weights],
        compiler_params=_params("arbitrary", "arbitrary"),
        name="inproj_retention",
    )(x1, mods0, yk.reshape(TOP_K, bsz, t, d // 2), w_tok.reshape(bsz, t, TOP_K), mods, norm_g.reshape(1, d),
      cos_t, sin_t, *weights)


def _backward_chunk(n, n_ctx_chunks, n_chunks):
    return jnp.where(n < n_ctx_chunks, n_ctx_chunks - 1 - n, n_chunks - 1 - (n - n_ctx_chunks))


def _gla_body(qd_f, ki_f, ks_f, ed_f, v_f, qd_b, ki_b, ks_b, ed_b, v_b, hmask_ref, bdmask_ref, o_f, o_b, st_f, st_b):
    c = GLA_CHUNK

    @pl.when(pl.program_id(1) == 0)
    def _():
        st_f[...] = jnp.zeros_like(st_f)
        st_b[...] = jnp.zeros_like(st_b)

    r4 = lax.broadcasted_iota(I32, (GLA_HEADS * c, c), 0) & (c - 1)
    c4 = lax.broadcasted_iota(I32, (GLA_HEADS * c, c), 1)
    dirs = ((qd_f, ki_f, ks_f, ed_f, v_f, o_f, st_f), (qd_b, ki_b, ks_b, ed_b, v_b, o_b, st_b))
    chains = [(bb, d) + dirs[d] for bb in range(qd_f.shape[0]) for d in range(2)]
    scores, inter, grow = [], [], []
    for bb, d, qd_ref, ki_ref, ks_ref, ed_ref, v_ref, o_ref, st_ref in chains:
        q_dec = qd_ref[bb]
        q_heads = jnp.concatenate([q_dec] * GLA_HEADS, axis=0) * hmask_ref[...]
        seen4 = (c4 <= r4) if d == 0 else (c4 >= r4)
        scores.append(jnp.where(seen4, _dot_nt(q_heads, ki_ref[bb]), 0.0).astype(BF16))
        inter.append(_dot_nt(q_dec, st_ref[bb].astype(BF16)))
        grow.append(_dot_tn(v_ref[bb], ks_ref[bb]))
    for (bb, d, qd_ref, ki_ref, ks_ref, ed_ref, v_ref, o_ref, st_ref), sc, o_inter, dst in zip(chains, scores, inter, grow):
        v = v_ref[bb]
        o_intra = jnp.concatenate(
            [_dot(sc[h * c:(h + 1) * c], v[:, h * GLA_DV:(h + 1) * GLA_DV]) for h in range(GLA_HEADS)], axis=1)
        o_ref[bb] = o_intra + o_inter
        st_ref[bb] = st_ref[bb] * ed_ref[bb, 0] + bdmask_ref[...] * dst


def _gla(per_dir, v, n_ctx):
    bsz, t, _ = v.shape
    nc, ncc = t // GLA_CHUNK, n_ctx // GLA_CHUNK
    gb = GLA_BATCH
    fwd = lambda b, n: (b, n, 0)
    bwd = lambda b, n: (b, _backward_chunk(n, ncc, nc), 0)
    hmask = (jnp.arange(AB_QK)[:, None] // GLA_CHUNK == jnp.arange(AB_QK)[None, :] // GLA_DK).astype(BF16)
    bdmask = (jnp.arange(AB_V)[:, None] // GLA_DV == jnp.arange(AB_QK)[None, :] // GLA_DK).astype(F32)

    def specs(idx):
        idx4 = lambda b, n: idx(b, n) + (0,)
        return [pl.BlockSpec((gb, GLA_CHUNK, AB_QK), idx)] * 3 + [pl.BlockSpec((gb, 1, 1, AB_QK), idx4),
                                                                  pl.BlockSpec((gb, GLA_CHUNK, AB_V), idx)]

    return pl.pallas_call(
        _gla_body,
        grid=(bsz // gb, nc),
        in_specs=specs(fwd) + specs(bwd) + [_const_spec(hmask.shape), _const_spec(bdmask.shape)],
        out_specs=[pl.BlockSpec((gb, GLA_CHUNK, AB_V), fwd), pl.BlockSpec((gb, GLA_CHUNK, AB_V), bwd)],
        out_shape=[jax.ShapeDtypeStruct((bsz, t, AB_V), F32)] * 2,
        scratch_shapes=[pltpu.VMEM((gb, AB_V, AB_QK), F32)] * 2,
        compiler_params=_params("arbitrary", "arbitrary"),
        name="gla_scan",
    )(*per_dir[0], v, *per_dir[1], v, hmask, bdmask)


def _cmul(x, y):
    return x[0] * y[0] - x[1] * y[1], x[0] * y[1] + x[1] * y[0]


def _s5_operators(lam_re, lam_im, log_step, b_re, b_im, c_re, c_im):
    ln = S5_CHUNK
    step = jnp.exp(log_step.astype(F32))[..., None]
    lam_re, lam_im = lam_re.astype(F32), lam_im.astype(F32)
    mag = jnp.exp(lam_re * step)
    a = (mag * jnp.cos(lam_im * step), mag * jnp.sin(lam_im * step))
    den = lam_re * lam_re + lam_im * lam_im
    f_re = ((a[0] - 1.0) * lam_re + a[1] * lam_im) / den
    f_im = (a[1] * lam_re - (a[0] - 1.0) * lam_im) / den
    bt_re, bt_im = b_re.transpose(0, 2, 1), b_im.transpose(0, 2, 1)
    bb = _cmul((f_re[:, :, None, :], f_im[:, :, None, :]), (bt_re, bt_im))
    bbt = jnp.concatenate([bb[0], -bb[1]], axis=-1)
    pw = (a[0][:, :, None, :], a[1][:, :, None, :])
    while pw[0].shape[2] < ln:
        top = (pw[0][:, :, -1:, :], pw[1][:, :, -1:, :])
        nxt = _cmul(top, pw)
        pw = (jnp.concatenate([pw[0], nxt[0]], axis=2), jnp.concatenate([pw[1], nxt[1]], axis=2))
    pw = (jnp.concatenate([jnp.ones_like(pw[0][:, :, :1]), pw[0]], axis=2),
          jnp.concatenate([jnp.zeros_like(pw[1][:, :, :1]), pw[1]], axis=2))
    ca = _cmul((c_re[:, :, None], c_im[:, :, None]), (pw[0][:, :, :, None, :], pw[1][:, :, :, None, :]))
    by_dir = lambda arr, lo, flip_d: jnp.stack([jnp.flip(arr[d, :, lo:lo + ln], axis=1) if d == flip_d
                                                else arr[d, :, lo:lo + ln] for d in range(2)])
    rows = lambda arr: arr.reshape(2, S5_GROUPS, ln * S5_GROUP, 2 * S5_P)
    cab = rows(by_dir(jnp.concatenate([ca[0], ca[1]], axis=-1), 0, 1))
    cab2 = rows(by_dir(jnp.concatenate([ca[0], -ca[1]], axis=-1), 1, 1)).astype(BF16)
    pwx = by_dir(jnp.concatenate([pw[0], pw[1]], axis=-1), 0, 0)
    lr, li = pw[0][:, :, ln], pw[1][:, :, ln]
    ac_rows = [jnp.concatenate([lr, lr], -1), jnp.concatenate([-li, li], -1), jnp.concatenate([li, -li], -1)]
    ac = jnp.stack(ac_rows + [jnp.zeros_like(ac_rows[0])] * 5, axis=2)
    return cab, cab2, bbt, pwx, ac


def _s5_group_operators(gg, cab_ref, bbt_ref, pwx_ref, tz, wx):
    ln, ch, p = S5_CHUNK, S5_GROUP, S5_P
    lane = lax.broadcasted_iota(I32, (ch, ln * ch), 1)
    for d in range(2):
        kern = _dot3(bbt_ref[d, gg], cab_ref[d, gg], dot=_dot_nt)
        bt = bbt_ref[d, gg]
        b_re, b_im = bt[:, 0:p], -bt[:, p:2 * p]
        for j in range(ln):
            if d == 0:
                blk = jnp.where(lane >= j * ch, kern if j == 0 else pltpu.roll(kern, j * ch, 1), 0.0)
            else:
                blk = jnp.where(lane < (j + 1) * ch, kern if j == ln - 1 else pltpu.roll(kern, (j + 1) * ch, 1), 0.0)
            tz[gg, d, j * ch:(j + 1) * ch, :] = blk.astype(BF16)
            pr, pi = pwx_ref[d, gg, j:j + 1, 0:p], pwx_ref[d, gg, j:j + 1, p:2 * p]
            x_re, x_im = pr * b_re - pi * b_im, pr * b_im + pi * b_re
            wx[gg, d, j * ch:(j + 1) * ch, :] = jnp.concatenate([x_re, x_im, x_im, x_re], axis=1).astype(BF16)


def _s5_placement(pall):
    rows, cols = pall.shape[1], pall.shape[2]
    row = lax.broadcasted_iota(I32, (rows, cols), 0)
    col = lax.broadcasted_iota(I32, (rows, cols), 1)
    same_token = (row >> 7) == (col >> 4)
    for g8 in range(pall.shape[0]):
        pall[g8] = jnp.where(same_token & ((row & 127) == g8 * S5_GROUP + (col & (S5_GROUP - 1))), 1.0, 0.0).astype(BF16)


def _first_step():
    return (pl.program_id(0) == 0) & (pl.program_id(1) == 0)


def _s5_fold_body(ncs, u_ref, o_ref, pall, ucat):
    @pl.when(_first_step())
    def _():
        _s5_placement(pall)

    for b in range(u_ref.shape[1]):
        for j in range(S5_CHUNK):
            ucat[b * ncs:(b + 1) * ncs, j * 128:(j + 1) * 128] = u_ref[0, b, pl.ds(j, ncs, stride=S5_CHUNK), :].astype(BF16)
    for g8 in range(pall.shape[0]):
        o_ref[g8] = _dot(ucat[...], pall[g8]).astype(BF16)


def _s5_unfold_body(ncs, y_ref, o_ref, pall):
    @pl.when(_first_step())
    def _():
        _s5_placement(pall)

    def token_pair(i2, carry):
        r0 = pl.multiple_of(i2 * 256, 256)
        acc = _dot_nt(y_ref[0], pall[0, pl.ds(r0, 256), :])
        for g8 in range(1, pall.shape[0]):
            acc = acc + _dot_nt(y_ref[g8], pall[g8, pl.ds(r0, 256), :])
        for b in range(o_ref.shape[1]):
            for par in range(2):
                o_ref[0, b, pl.ds(2 * i2 + par, ncs, stride=S5_CHUNK), :] = (
                    acc[b * ncs:(b + 1) * ncs, par * 128:(par + 1) * 128])
        return carry

    lax.fori_loop(0, S5_CHUNK // 2, token_pair, 0)


def _s5_body(ncs_ctx, ncs, rows, u_ref, cab_ref, cab2_ref, bbt_ref, pwx_ref, ac_ref, y_ref, tz, wx, *vecs):
    half = 2 * S5_P
    n_groups = u_ref.shape[0]
    groups = [vecs[6 * gg:6 * gg + 6] for gg in range(n_groups)]
    for gg, (xx_f, xs_f, xx_b, xs_b, _, _) in enumerate(groups):
        _s5_group_operators(gg, cab_ref, bbt_ref, pwx_ref, tz, wx)
        for d, (xx, xs) in enumerate(((xx_f, xs_f), (xx_b, xs_b))):
            r = _dot(u_ref[gg], wx[gg, d])
            xx[...] = r[:, :half]
            xs[...] = r[:, half:]

    def advance(ac, s, s_sw, x, x_sw):
        return ac[0:1] * s + ac[1:2] * s_sw + x, ac[0:1] * s_sw + ac[2:3] * s + x_sw

    def step(n, carry):
        at_f = pl.ds(n, rows, stride=ncs)
        at_b = pl.ds(_backward_chunk(n, ncs_ctx, ncs), rows, stride=ncs)
        out = []
        for gg, (xx_f, xs_f, xx_b, xs_b, sin_f, sin_b) in enumerate(groups):
            s_f, sw_f, s_b, sw_b = carry[4 * gg:4 * gg + 4]
            sin_f[at_f, :] = s_f
            sin_b[at_b, :] = s_b
            out += advance(ac_ref[0, gg], s_f, sw_f, xx_f[at_f, :], xs_f[at_f, :])
            out += advance(ac_ref[1, gg], s_b, sw_b, xx_b[at_b, :], xs_b[at_b, :])
        return tuple(out)

    zero = jnp.zeros((rows, half), F32)
    lax.fori_loop(0, ncs, step, (zero,) * (4 * n_groups))
    for gg, (_, _, _, _, sin_f, sin_b) in enumerate(groups):
        u = u_ref[gg]
        y_ref[gg] = (_dot(u, tz[gg, 0]) + _dot(u, tz[gg, 1]) + _dot_nt(sin_f[...].astype(BF16), cab2_ref[0, gg])
                     + _dot_nt(sin_b[...].astype(BF16), cab2_ref[1, gg])).astype(BF16)


def _s5(u4, ops, n_ctx):
    nq, bsz, t, _ = u4.shape
    ln, lanes = S5_CHUNK, S5_CHUNK * S5_GROUP
    gq = S5_GROUPS // nq
    ncs, ncs_ctx = t // ln, n_ctx // ln
    m = ncs * bsz
    hb = S5_FOLD_BATCH
    tok_spec = pl.BlockSpec((1, hb, t, 128), lambda q, h: (q, h, 0, 0))
    grp_spec = pl.BlockSpec((gq, hb * ncs, lanes), lambda q, h: (q, h, 0))
    pall = pltpu.VMEM((gq, ln * 128, lanes), BF16)
    ug = pl.pallas_call(
        functools.partial(_s5_fold_body, ncs),
        grid=(nq, bsz // hb),
        in_specs=[tok_spec],
        out_specs=grp_spec,
        out_shape=jax.ShapeDtypeStruct((S5_GROUPS, m, lanes), BF16),
        scratch_shapes=[pall, pltpu.VMEM((hb * ncs, ln * 128), BF16)],
        compiler_params=_params("arbitrary", "arbitrary"),
        name="s5_fold",
    )(u4)
    sg = S5_SCAN_GROUPS
    dir_spec = lambda arr: pl.BlockSpec((2, sg) + arr.shape[2:], lambda g: (0, g, 0, 0))
    yg = pl.pallas_call(
        functools.partial(_s5_body, ncs_ctx, ncs, bsz),
        grid=(S5_GROUPS // sg,),
        in_specs=[pl.BlockSpec((sg, m, lanes), lambda g: (g, 0, 0))] + [dir_spec(arr) for arr in ops],
        out_specs=pl.BlockSpec((sg, m, lanes), lambda g: (g, 0, 0)),
        out_shape=jax.ShapeDtypeStruct((S5_GROUPS, m, lanes), BF16),
        scratch_shapes=[pltpu.VMEM((sg, 2, lanes, lanes), BF16)] * 2 + [pltpu.VMEM((m, 2 * S5_P), F32)] * (6 * sg),
        compiler_params=_params("arbitrary"),
        name="s5_scan",
    )(ug, *ops)
    return pl.pallas_call(
        functools.partial(_s5_unfold_body, ncs),
        grid=(nq, bsz // hb),
        in_specs=[grp_spec],
        out_specs=tok_spec,
        out_shape=jax.ShapeDtypeStruct(u4.shape, F32),
        scratch_shapes=[pall],
        compiler_params=_params("arbitrary", "arbitrary"),
        name="s5_unfold",
    )(yg)


def _ret_body(ncc, q_f, k_f, v_f, q_b, k_b, v_b, dmat_ref, rsc_ref, csc_ref, gam_ref, o_f, o_b, st_f, st_b):
    n = pl.program_id(1)

    @pl.when(n == 0)
    def _():
        st_f[...] = jnp.zeros_like(st_f)
        st_b[...] = jnp.zeros_like(st_b)

    dirs = ((q_f, k_f, v_f, o_f, st_f), (q_b, k_b, v_b, o_b, st_b))

    @pl.when(n >= ncc)
    def _():
        for d, (q_ref, k_ref, v_ref, o_ref, st_ref) in enumerate(dirs):
            for h in range(RET_HEADS):
                qh = q_ref[0, :, h * RET_DK:(h + 1) * RET_DK]
                kh = k_ref[0, :, h * RET_DK:(h + 1) * RET_DK]
                vh = v_ref[0, :, h * RET_DV:(h + 1) * RET_DV]
                scores = (_dot_nt(qh, kh) * dmat_ref[d, h]).astype(BF16)
                o = _dot(scores, vh) + rsc_ref[d, h] * _dot(qh, st_ref[h].astype(BF16))
                o_ref[0, :, h * RET_DV:(h + 1) * RET_DV] = o.astype(o_ref.dtype)

    for d, (q_ref, k_ref, v_ref, o_ref, st_ref) in enumerate(dirs):
        for h in range(RET_HEADS):
            kh = k_ref[0, :, h * RET_DK:(h + 1) * RET_DK]
            vh = v_ref[0, :, h * RET_DV:(h + 1) * RET_DV]
            k_state = (kh.astype(F32) * csc_ref[d, h]).astype(BF16)
            st_ref[h] = st_ref[h] * gam_ref[d, h] + _dot_tn(k_state, vh)


def _retention(q, k, v, decay_logit, n_ctx):
    bsz, t, _ = q.shape
    c = RET_CHUNK
    nc, ncc = t // c, n_ctx // c
    nl = nc - ncc
    log_gamma = jax.nn.log_sigmoid(decay_logit.astype(F32))[:, :, None, None]
    i = jnp.arange(c, dtype=F32)
    lag = i[:, None] - i[None, :]
    lag = jnp.stack([lag, -lag])[:, None]
    dmat = jnp.where(lag >= 0, jnp.exp(log_gamma * jnp.maximum(lag, 0.0)), 0.0)
    done = jnp.stack([i + 1.0, c - i])[:, None, :, None]
    rsc = jnp.exp(log_gamma * done)
    csc = jnp.exp(log_gamma * (c - done))
    gam = jnp.exp(log_gamma[:, :, 0, 0] * c)
    fwd = lambda b, n: (b, n, 0)
    bwd = lambda b, n: (b, _backward_chunk(n, ncc, nc), 0)
    o_fwd = lambda b, n: (b, jnp.maximum(n - ncc, 0), 0)
    o_bwd = lambda b, n: (b, nl - 1 - jnp.maximum(n - ncc, 0), 0)

    def specs(idx):
        return [pl.BlockSpec((1, c, RET_QK), idx), pl.BlockSpec((1, c, RET_QK), idx), pl.BlockSpec((1, c, RET_MIX), idx)]

    return pl.pallas_call(
        functools.partial(_ret_body, ncc),
        grid=(bsz, nc),
        in_specs=specs(fwd) + specs(bwd) + [_const_spec(dmat.shape), _const_spec(rsc.shape), _const_spec(csc.shape),
                                            pl.BlockSpec(memory_space=pltpu.SMEM)],
        out_specs=[pl.BlockSpec((1, c, RET_MIX), o_fwd), pl.BlockSpec((1, c, RET_MIX), o_bwd)],
        out_shape=[jax.ShapeDtypeStruct((bsz, nl * c, RET_MIX), BF16)] * 2,
        scratch_shapes=[pltpu.VMEM((RET_HEADS, RET_DK, RET_DV), F32)] * 2,
        compiler_params=_params("arbitrary", "arbitrary"),
        name="retention_scan",
    )(q, k, v, q, k, v, dmat, rsc, csc, gam)


def _route(x, mixed, mod, n2g_ref, wr_ref, br_ref, x1_ref, h2_ref, e_ref, w_ref, r_ref, cnt_ref):
    tm = x.shape[0]
    x1 = x + mod[2:3] * mixed
    x1_ref[0] = x1
    h2 = _norm_mod(x1, n2g_ref[...], mod[3:4], mod[4:5])
    h2_ref[0] = _pack_rows(h2)
    logits = _dot3(wr_ref[...], h2, dot=_dot_nt) + br_ref[...]
    ie = lax.broadcasted_iota(I32, logits.shape, 0)
    tops, picks = [], []
    for _ in range(TOP_K):
        mx = jnp.max(logits, axis=0, keepdims=True)
        pick = jnp.min(jnp.where(logits == mx, ie, N_EXPERTS), axis=0, keepdims=True)
        tops.append(mx)
        picks.append(pick)
        logits = jnp.where(ie == pick, -jnp.inf, logits)
    ex = [jnp.exp(tk - tops[0]) for tk in tops]
    den = ex[0] + ex[1] + ex[2] + ex[3]
    for kk in range(TOP_K):
        w_ref[0, kk:kk + 1, :] = ex[kk] / den
        e_ref[0, kk:kk + 1, :] = picks[kk]

    @pl.when((pl.program_id(0) == 0) & (pl.program_id(1) == 0))
    def _():
        cnt_ref[...] = jnp.zeros_like(cnt_ref)

    earlier = (lax.broadcasted_iota(I32, (tm, tm), 0) < lax.broadcasted_iota(I32, (tm, tm), 1))
    earlier = jnp.where(earlier, 1.0, 0.0).astype(BF16)
    run = cnt_ref[:, 0:1]
    for kk, pick in enumerate(picks):
        onehot = jnp.where(ie == pick, 1.0, 0.0)
        before = _dot(onehot.astype(BF16), earlier) + run
        r_ref[0, kk:kk + 1, :] = jnp.sum(onehot * before, axis=0, keepdims=True).astype(I32)
        run = run + jnp.sum(onehot, axis=1, keepdims=True)
    cnt_ref[...] = jnp.broadcast_to(run, cnt_ref.shape)


def _mix0_body(nct, ctx_ref, lat_ref, mod_ref, of_ref, ob_ref, g_ref, ys_ref, u_ref, gng_ref, dsk_ref, gluw_ref,
               glub_ref, wo_ref, n2g_ref, wr_ref, br_ref, x1_ref, h2_ref, e_ref, w_ref, r_ref, cnt_ref):
    o = of_ref[0] + ob_ref[0]
    heads = []
    for h in range(GLA_HEADS):
        oh = o[:, h * GLA_DV:(h + 1) * GLA_DV]
        heads.append(oh * lax.rsqrt(jnp.mean(oh * oh, axis=-1, keepdims=True) + EPS))
    gla = jnp.concatenate(heads, axis=1) * gng_ref[...] * _silu(g_ref[0].astype(F32))
    lane_blocks = lambda ref: jnp.concatenate([ref[qb, 0] for qb in range(ref.shape[0])], axis=1)
    y = jax.nn.gelu(lane_blocks(ys_ref) + dsk_ref[...] * lane_blocks(u_ref))
    y = y * jax.nn.sigmoid(_dot(y.astype(BF16), gluw_ref[...]) + glub_ref[...])
    mixed = _dot(gla.astype(BF16), wo_ref[0:AB_V]) + _dot(y.astype(BF16), wo_ref[AB_V:AB_V + S5_CH])
    _route(_stream_tile(nct, ctx_ref, lat_ref), mixed, mod_ref[0, 0], n2g_ref, wr_ref, br_ref,
           x1_ref, h2_ref, e_ref, w_ref, r_ref, cnt_ref)


def _mix1_body(x_ref, mod_ref, of_ref, ob_ref, g_ref, ng_ref, wo_ref, n2g_ref, wr_ref, br_ref,
               x1_ref, h2_ref, e_ref, w_ref, r_ref, cnt_ref):
    mixed = None
    for h in range(RET_HEADS):
        sl = slice(h * RET_DV, (h + 1) * RET_DV)
        oh = of_ref[0, :, sl].astype(F32) + ob_ref[0, :, sl].astype(F32)
        mu = jnp.mean(oh, axis=-1, keepdims=True)
        cen = oh - mu
        var = jnp.mean(cen * cen, axis=-1, keepdims=True)
        gated = cen * lax.rsqrt(var + EPS) * ng_ref[:, sl] * _silu(g_ref[0, :, sl].astype(F32))
        part = _dot(gated.astype(BF16), wo_ref[sl])
        mixed = part if mixed is None else mixed + part
    _route(x_ref[0], mixed, mod_ref[0, 0], n2g_ref, wr_ref, br_ref, x1_ref, h2_ref, e_ref, w_ref, r_ref, cnt_ref)


def _mix_call(body, name, stream, mods, tiles, acts, consts, norm2_g, w_router, b_router, n_tok, seg_tile0):
    bsz, _, d = stream[-1].shape
    tm = TOKEN_TILE
    off = lambda b, i: (b, i + seg_tile0, 0)
    loc = lambda b, i: (b, i, 0)
    ntl = bsz * tiles
    flat = lambda b, i: (b * tiles + i, 0, 0)
    in_specs = list(_split_specs(n_tok, d)) if len(stream) == 2 else [pl.BlockSpec((1, tm, d), off)]
    in_specs.append(pl.BlockSpec((1, 1, 6, d), lambda b, i: (b, ((i + seg_tile0) >= n_tok).astype(I32), 0, 0)))
    args = list(stream) + [mods]
    for arr, offset in acts:
        if arr.ndim == 4:
            in_specs.append(pl.BlockSpec((arr.shape[0], 1, tm, arr.shape[3]), lambda b, i: (0, b, i, 0)))
        else:
            in_specs.append(pl.BlockSpec((1, tm, arr.shape[2]), off if offset else loc))
        args.append(arr)
    tail = list(consts) + [norm2_g.reshape(1, d), w_router.T, b_router.reshape(N_EXPERTS, 1)]
    in_specs += [_const_spec(a.shape) for a in tail]
    args += tail
    tok_out = pl.BlockSpec((1, TOP_K, tm), flat)
    return pl.pallas_call(
        body,
        grid=(bsz, tiles),
        in_specs=in_specs,
        out_specs=[pl.BlockSpec((1, tm, d), loc), pl.BlockSpec((1, tm, d // 2), loc), tok_out, tok_out, tok_out,
                   _const_spec((N_EXPERTS, 128))],
        out_shape=[jax.ShapeDtypeStruct((bsz, tiles * tm, d), F32), jax.ShapeDtypeStruct((bsz, tiles * tm, d // 2), U32),
                   jax.ShapeDtypeStruct((ntl, TOP_K, tm), I32), jax.ShapeDtypeStruct((ntl, TOP_K, tm), F32),
                   jax.ShapeDtypeStruct((ntl, TOP_K, tm), I32), jax.ShapeDtypeStruct((N_EXPERTS, 128), F32)],
        compiler_params=_params("arbitrary", "arbitrary"),
        name=name,
    )(*args)


def _cast_rows(src_ref, dst_ref, rows):
    def chunk(j, carry):
        r = pl.multiple_of(j * rows, rows)
        dst_ref[pl.ds(r, rows), :] = src_ref[0, 0, pl.ds(r, rows), :].astype(BF16)
        return carry

    lax.fori_loop(0, dst_ref.shape[0] // rows, chunk, 0)


def _expert_body(be_ref, nu_ref, x_ref, wgu_ref, bgu_ref, wd_ref, bd_ref, o_ref, wgu_bf, wd_bf):
    i = pl.program_id(0)
    live = i < nu_ref[0]
    new_expert = (i == 0) | (be_ref[i] != be_ref[jnp.maximum(i - 1, 0)])

    @pl.when(live & new_expert)
    def _():
        _cast_rows(wgu_ref, wgu_bf, 128)
        _cast_rows(wd_ref, wd_bf, 128)

    @pl.when(live)
    def _():
        x_lo, x_hi = _unpack_rows(x_ref[...])
        half = x_lo.shape[1]
        gu = (_dot(x_lo.astype(BF16), wgu_bf[0:half]) + _dot(x_hi.astype(BF16), wgu_bf[half:2 * half])
              + bgu_ref[0, 0])
        gate = jnp.minimum(gu[:, :D_FF], SWIGLU_LIMIT)
        lin = jnp.clip(gu[:, D_FF:], -SWIGLU_LIMIT, SWIGLU_LIMIT)
        act = gate * jax.nn.sigmoid(SWIGLU_ALPHA * gate) * (lin + 1.0)
        y = _dot(act.astype(BF16), wd_bf[...]) + bd_ref[0, 0]
        o_ref[...] = _pack_rows(y)

    @pl.when(i >= nu_ref[0])
    def _():
        o_ref[...] = jnp.zeros_like(o_ref)


def _experts(xb, block_e, n_used, layer, w_gu, b_gu, w_down, b_down):
    n_slots, half = xb.shape
    d = 2 * half
    n_blocks = n_slots // MOE_BLOCK
    depth = w_gu.shape[0]
    by_expert = lambda i, be, nu: (layer, be[i], 0, 0)
    return pl.pallas_call(
        _expert_body,
        grid_spec=pltpu.PrefetchScalarGridSpec(
            num_scalar_prefetch=2,
            grid=(n_blocks,),
            in_specs=[pl.BlockSpec((MOE_BLOCK, half), lambda i, be, nu: (i, 0)),
                      pl.BlockSpec((1, 1, d, 2 * D_FF), by_expert), pl.BlockSpec((1, 1, 1, 2 * D_FF), by_expert),
                      pl.BlockSpec((1, 1, D_FF, d), by_expert), pl.BlockSpec((1, 1, 1, d), by_expert)],
            out_specs=pl.BlockSpec((MOE_BLOCK, half), lambda i, be, nu: (i, 0)),
            scratch_shapes=[pltpu.VMEM((d, 2 * D_FF), BF16), pltpu.VMEM((D_FF, d), BF16)]),
        out_shape=jax.ShapeDtypeStruct((n_slots, half), U32),
        compiler_params=_params("arbitrary"),
        name="moe_experts",
    )(block_e, n_used, xb, w_gu, b_gu.reshape(depth, N_EXPERTS, 1, 2 * D_FF), w_down,
      b_down.reshape(depth, N_EXPERTS, 1, d))


def _combine_body(x1_ref, mod_ref, yk_ref, w_ref, fg_ref, *refs):
    o_ref = refs[-1]
    d = x1_ref.shape[2]
    half = d // 2
    x2_lo, x2_hi = _moe_residual(x1_ref, mod_ref[0, 0][5:6], yk_ref, w_ref)
    ms = (jnp.sum(x2_lo * x2_lo, axis=-1, keepdims=True) + jnp.sum(x2_hi * x2_hi, axis=-1, keepdims=True)) / d
    r = lax.rsqrt(ms + EPS)
    o_ref[0, :, 0:half] = x2_lo * r * fg_ref[:, 0:half]
    o_ref[0, :, half:d] = x2_hi * r * fg_ref[:, half:d]


def _combine(x1, mods, yk_parts, w_tok, seg_tile0, n_tok, final_g):
    bsz, t, d = x1.shape
    tm = TOKEN_TILE
    pb = bsz // len(yk_parts)
    out = None
    for p, yk in enumerate(yk_parts):
        rows = lambda b, i, b0=p * pb: (b + b0, i, 0)
        mod_rows = lambda b, i, b0=p * pb: (b + b0, ((i + seg_tile0) >= n_tok).astype(I32), 0, 0)
        in_specs = [pl.BlockSpec((1, tm, d), rows), pl.BlockSpec((1, 1, 6, d), mod_rows),
                    pl.BlockSpec((TOP_K, 1, tm, d // 2), lambda b, i: (0, b, i, 0)),
                    pl.BlockSpec((1, tm, TOP_K), rows), _const_spec((1, d))]
        args = [x1, mods, yk.reshape(TOP_K, pb, t, d // 2), w_tok.reshape(bsz, t, TOP_K), final_g.reshape(1, d)]
        aliases = {}
        if out is not None:
            in_specs.append(pl.BlockSpec(memory_space=pl.ANY))
            args.append(out)
            aliases = {len(args) - 1: 0}
        out = pl.pallas_call(
            _combine_body,
            grid=(pb, t // tm),
            in_specs=in_specs,
            out_specs=pl.BlockSpec((1, tm, d), rows),
            out_shape=jax.ShapeDtypeStruct((bsz, t, d), F32),
            input_output_aliases=aliases,
            compiler_params=_params("arbitrary", "arbitrary"),
            name="moe_combine",
        )(*args)
    return out


def _sc_mesh():
    return plsc.VectorSubcoreMesh(core_axis_name="core", subcore_axis_name="subcore",
                                  num_cores=SC_CORES, num_subcores=SC_SUBCORES)


def _sc_worker_base(per_worker):
    return (lax.axis_index("subcore") * SC_CORES + lax.axis_index("core")) * per_worker


def _sc_dispatch(rows, dest, n_slots):
    n, w = rows.shape
    chunk = SC_CHUNK // 2
    per_worker = n // SC_WORKERS
    n_chunks = per_worker // chunk
    assert per_worker * SC_WORKERS == n and n_chunks * chunk == per_worker and n_chunks % 2 == 0

    @functools.partial(
        pl.kernel, mesh=_sc_mesh(), out_type=jax.ShapeDtypeStruct((n_slots, w), rows.dtype),
        scratch_types=([pltpu.VMEM((chunk,), I32)] * (2 * TOP_K) + [pltpu.VMEM((chunk, w), rows.dtype)] * 2
                       + [pltpu.SemaphoreType.DMA] * 2),
        name="moe_dispatch")
    def scatter_rows(rows_hbm, dest_hbm, out_hbm, *scratch):
        slots = [(scratch[s * TOP_K:(s + 1) * TOP_K], scratch[2 * TOP_K + s], scratch[2 * TOP_K + 2 + s])
                 for s in range(2)]
        base0 = _sc_worker_base(per_worker)

        def load(j, slot):
            idx_refs, buf, _ = slot
            base = base0 + j * chunk
            pltpu.sync_copy(rows_hbm.at[pl.ds(base, chunk)], buf)
            for k, idx in enumerate(idx_refs):
                pltpu.sync_copy(dest_hbm.at[pl.ds(k * n + base, chunk)], idx)

        def scatters(slot):
            idx_refs, buf, sem = slot
            return [pltpu.make_async_copy(buf, out_hbm.at[idx], sem) for idx in idx_refs]

        load(0, slots[0])

        @pl.loop(0, n_chunks, step=2)
        def _(j):
            for cp in scatters(slots[0]):
                cp.start()
            load(j + 1, slots[1])
            for cp in scatters(slots[0]):
                cp.wait()
            for cp in scatters(slots[1]):
                cp.start()

            @pl.when(j + 2 < n_chunks)
            def _():
                load(j + 2, slots[0])
            for cp in scatters(slots[1]):
                cp.wait()

    return scatter_rows(rows, dest)


def _sc_gather(table, idx):
    n = idx.shape[0]
    w = table.shape[1]
    per_worker = n // SC_WORKERS
    n_chunks = per_worker // SC_CHUNK
    assert per_worker * SC_WORKERS == n and n_chunks * SC_CHUNK == per_worker and n_chunks % 2 == 0

    @functools.partial(
        pl.kernel, mesh=_sc_mesh(), out_type=jax.ShapeDtypeStruct((n, w), table.dtype),
        scratch_types=([pltpu.VMEM((SC_CHUNK,), I32)] * 2 + [pltpu.VMEM((SC_CHUNK, w), table.dtype)] * 2
                       + [pltpu.SemaphoreType.DMA] * 4),
        name="moe_gather")
    def gather_rows(table_hbm, idx_hbm, out_hbm, idx0, idx1, buf0, buf1, gsem0, gsem1, wsem0, wsem1):
        base0 = _sc_worker_base(per_worker)

        def gather_copy(idx_v, buf, sem):
            return pltpu.make_async_copy(table_hbm.at[idx_v], buf, sem)

        def write_copy(j, buf, sem):
            return pltpu.make_async_copy(buf, out_hbm.at[pl.ds(base0 + j * SC_CHUNK, SC_CHUNK)], sem)

        def start_gather(j, idx_v, buf, sem):
            pltpu.sync_copy(idx_hbm.at[pl.ds(base0 + j * SC_CHUNK, SC_CHUNK)], idx_v)
            gather_copy(idx_v, buf, sem).start()

        start_gather(0, idx0, buf0, gsem0)

        @pl.loop(0, n_chunks, step=2)
        def _(j):
            @pl.when(j > 0)
            def _():
                write_copy(j - 1, buf1, wsem1).wait()
            start_gather(j + 1, idx1, buf1, gsem1)
            gather_copy(idx0, buf0, gsem0).wait()
            write_copy(j, buf0, wsem0).start()

            @pl.when(j + 2 < n_chunks)
            def _():
                write_copy(j, buf0, wsem0).wait()
                start_gather(j + 2, idx0, buf0, gsem0)
            gather_copy(idx1, buf1, gsem1).wait()
            write_copy(j + 1, buf1, wsem1).start()

        write_copy(n_chunks - 2, buf0, wsem0).wait()
        write_copy(n_chunks - 1, buf1, wsem1).wait()

    return gather_rows(table, idx)


def _moe(h2, e_tl, w_tl, r_tl, cnt, layer, w_gu, b_gu, w_down, b_down, parts=1):
    bsz, t, half = h2.shape
    n = bsz * t
    flat = lambda a: a.transpose(1, 0, 2).reshape(TOP_K, n)
    e_k, w_k, r_k = flat(e_tl), flat(w_tl), flat(r_tl)
    counts = cnt[:, 0].astype(I32)
    padded = (counts + MOE_BLOCK - 1) // MOE_BLOCK * MOE_BLOCK
    pad_end = jnp.cumsum(padded)
    pad_start = pad_end - padded
    n_blocks = (n * TOP_K + MOE_BLOCK - 1) // MOE_BLOCK + N_EXPERTS
    block_start = jnp.arange(n_blocks, dtype=I32) * MOE_BLOCK
    block_e = jnp.minimum(jnp.sum((pad_end[None, :] <= block_start[:, None]).astype(I32), axis=1), N_EXPERTS - 1)
    n_used = (pad_end[-1:] // MOE_BLOCK).astype(I32)
    start_k = jnp.sum(jnp.where(e_k[..., None] == jnp.arange(N_EXPERTS, dtype=I32), pad_start, 0), axis=-1)
    dest = (start_k + r_k).reshape(TOP_K * n)
    xb = _sc_dispatch(h2.reshape(n, half), dest, n_blocks * MOE_BLOCK)
    yb = _experts(xb, block_e, n_used, layer, w_gu, b_gu, w_down, b_down)
    dest_parts = dest.reshape(TOP_K, parts, n // parts)
    return [_sc_gather(yb, dest_parts[:, p].reshape(-1)) for p in range(parts)], w_k.T


def _rope_tables(n_ctx, n_lat):
    n_freq = RET_DK // 4
    inv_freq = ROPE_BASE ** (-jnp.arange(n_freq, dtype=F32) / n_freq)
    pos = jnp.arange(n_lat, dtype=I32)
    cos, sin = [], []
    for p in (pos // GRID_W, pos % GRID_W):
        ang = p.astype(F32)[:, None] * inv_freq
        cos += [jnp.cos(ang), jnp.cos(ang)]
        sin += [-jnp.sin(ang), jnp.sin(ang)]
    cos, sin = jnp.concatenate(cos, axis=1), jnp.concatenate(sin, axis=1)
    return (jnp.concatenate([jnp.ones((n_ctx, RET_DK), F32), cos], axis=0),
            jnp.concatenate([jnp.zeros((n_ctx, RET_DK), F32), sin], axis=0))


def kernel(x, c, ctx, c_ctx, ada_w, ada_b, norm1_g, norm2_g, ab_w_in, ab_w_out, gla_wa, gla_ba, gla_norm_g, s5_lam_re, s5_lam_im, s5_log_step, s5_b_re, s5_b_im, s5_c_re, s5_c_im, s5_d, s5_glu_w, s5_glu_b, ret_w_in, ret_w_out, ret_decay_logit, ret_norm_g, moe_w_router, moe_b_router, moe_w_gu, moe_b_gu, moe_w_down, moe_b_down, final_norm_g):
    bsz, n_lat, d = x.shape
    n_ctx = ctx.shape[1]
    depth = ada_w.shape[0]
    assert depth == 2 and d == D_MODEL and bsz == 8, "kernels are laid out for the stated problem shape"
    assert n_ctx % TOKEN_TILE == 0 and n_lat % TOKEN_TILE == 0 and n_lat % GRID_W == 0
    t = n_ctx + n_lat
    nct = n_ctx // TOKEN_TILE

    cvec = jnp.zeros((16, d), F32).at[:bsz].set(c).at[bsz].set(c_ctx)
    mod = _ada_mod(cvec, ada_w, ada_b).reshape(depth, 16, 6, d)
    mods = [jnp.stack([jnp.broadcast_to(mod[l, bsz], (bsz, 6, d)), mod[l, :bsz]], axis=1) for l in range(depth)]

    w_in = ab_w_in[0].astype(BF16)
    cuts = [0, AB_QK, 2 * AB_QK, 2 * AB_QK + AB_V, 2 * AB_QK + 2 * AB_V, 2 * AB_QK + 2 * AB_V + 2 * GLA_RANK,
            w_in.shape[1]]
    pieces = [w_in[:, a:b] for a, b in zip(cuts[:-1], cuts[1:])]
    wa_pad = jnp.zeros((2, 2 * GLA_RANK, AB_QK), F32)
    wa_pad = wa_pad.at[0, :GLA_RANK].set(gla_wa[0, 0]).at[1, GLA_RANK:].set(gla_wa[0, 1])
    outs = _inproj0(ctx, x, mods[0], norm1_g[0], pieces, wa_pad, gla_ba[0].reshape(2, 1, AB_QK), nct)
    v, g, u = outs[8:]
    o_f, o_b = _gla((outs[0:4], outs[4:8]), v, n_ctx)
    ops = _s5_operators(s5_lam_re[0], s5_lam_im[0], s5_log_step[0], s5_b_re[0], s5_b_im[0], s5_c_re[0], s5_c_im[0])
    ys = _s5(u, ops, n_ctx)
    consts = [jnp.tile(gla_norm_g[0], GLA_HEADS).reshape(1, AB_V), s5_d[0].reshape(1, S5_CH),
              s5_glu_w[0].astype(BF16), s5_glu_b[0].reshape(1, S5_CH), ab_w_out[0].astype(BF16)]
    x1, h2, e_tl, w_tl, r_tl, cnt = _mix_call(
        functools.partial(_mix0_body, nct), "mix_gla_s5", (ctx, x), mods[0], t // TOKEN_TILE,
        [(o_f, False), (o_b, False), (g, False), (ys, False), (u, False)], consts,
        norm2_g[0], moe_w_router[0], moe_b_router[0], nct, 0)
    yk, w_tok = _moe(h2, e_tl, w_tl, r_tl, cnt, 0, moe_w_gu, moe_b_gu, moe_w_down, moe_b_down)

    w_in = ret_w_in[0].astype(BF16)
    cuts = [0, RET_QK, 2 * RET_QK, 2 * RET_QK + RET_MIX, w_in.shape[1]]
    pieces = [w_in[:, a:b] for a, b in zip(cuts[:-1], cuts[1:])]
    cos_t, sin_t = _rope_tables(n_ctx, n_lat)
    x_all, q, k, v, g = _inproj1(x1, mods[0], yk[0], w_tok, mods[1], norm1_g[1], cos_t, sin_t, pieces, nct)
    o_f, o_b = _retention(q, k, v, ret_decay_logit[0], n_ctx)
    consts = [ret_norm_g[0].reshape(1, RET_MIX), ret_w_out[0].astype(BF16)]
    x1, h2, e_tl, w_tl, r_tl, cnt = _mix_call(
        _mix1_body, "mix_retention", (x_all,), mods[1], n_lat // TOKEN_TILE,
        [(o_f, False), (o_b, False), (g, True)], consts,
        norm2_g[1], moe_w_router[1], moe_b_router[1], nct, nct)
    yk, w_tok = _moe(h2, e_tl, w_tl, r_tl, cnt, 1, moe_w_gu, moe_b_gu, moe_w_down, moe_b_down, parts=COMBINE_PARTS)
    return _combine(x1, mods[1], yk, w_tok, nct, nct, final_norm_g)
```

```python
import functools
import math

import jax
import jax.numpy as jnp
from jax import lax
from jax.experimental import pallas as pl
from jax.experimental.pallas import tpu as pltpu
from jax.experimental.pallas import tpu_sc as plsc

F32, BF16, I32, U32 = jnp.float32, jnp.bfloat16, jnp.int32, jnp.uint32

D_MODEL = 1024
GRID_W = 64
EPS = 1e-6
GLA_HEADS, GLA_DK, GLA_DV, GLA_RANK, GLA_TAU, GLA_CHUNK = 4, 64, 128, 16, 16.0, 64
GLA_BATCH = 8
AB_QK, AB_V = GLA_HEADS * GLA_DK, GLA_HEADS * GLA_DV
S5_CH, S5_GROUP, S5_GROUPS, S5_P = 512, 16, 32, 64
S5_CHUNK = 16
S5_FOLD_BATCH = 4
S5_SCAN_GROUPS = 2
RET_HEADS, RET_DK, RET_DV = 4, 256, 512
RET_CHUNK = 256
RET_QK, RET_MIX = RET_HEADS * RET_DK, RET_HEADS * RET_DV
ROPE_BASE = 10000.0
N_EXPERTS, TOP_K, D_FF = 32, 4, 1024
SWIGLU_LIMIT, SWIGLU_ALPHA = 7.0, 1.702
MOE_BLOCK = 512
TOKEN_TILE = 256
ADA_TILE = 768
VMEM_LIMIT = 56 * 1024 * 1024
SC_CORES, SC_SUBCORES = 2, 16
SC_WORKERS = SC_CORES * SC_SUBCORES
SC_CHUNK = 64

def _params(*sem):
    return pltpu.CompilerParams(dimension_semantics=sem, vmem_limit_bytes=VMEM_LIMIT)


def _dot(a, b):
    return jnp.dot(a, b, preferred_element_type=F32)


def _dot_nt(a, b):
    return lax.dot_general(a, b, (((1,), (1,)), ((), ())), preferred_element_type=F32)


def _dot_tn(a, b):
    return lax.dot_general(a, b, (((0,), (0,)), ((), ())), preferred_element_type=F32)


def _split(a):
    hi = a.astype(BF16)
    return hi, (a - hi.astype(F32)).astype(BF16)


def _dot3(a, b, dot=_dot):
    ah, al = _split(a)
    bh, bl = _split(b)
    return dot(ah, bh) + (dot(ah, bl) + dot(al, bh))


def _pack_rows(x):
    h = x.shape[1] // 2
    lo = lax.bitcast_convert_type(x[:, 0:h].astype(BF16).astype(F32), U32)
    hi = lax.bitcast_convert_type(x[:, h:2 * h].astype(BF16).astype(F32), U32)
    return hi | (lo >> 16)


def _unpack_rows(p):
    lo = lax.bitcast_convert_type(p << 16, F32)
    hi = lax.bitcast_convert_type(p & jnp.uint32(0xFFFF0000), F32)
    return lo, hi


def _silu(x):
    return x * jax.nn.sigmoid(x)


def _norm_mod(x, g, shift, scale):
    r = lax.rsqrt(jnp.mean(x * x, axis=-1, keepdims=True) + EPS)
    return (x * r * g) * (1.0 + scale) + shift


def _const_spec(shape):
    nd = len(shape)
    return pl.BlockSpec(shape, lambda *_: (0,) * nd)


def _ada_body(c_ref, w_ref, b_ref, o_ref):
    o_ref[0] = _dot3(_silu(c_ref[...]), w_ref[0]) + b_ref[0]


def _ada_mod(cvec, ada_w, ada_b):
    depth, d, n6 = ada_w.shape
    rows = cvec.shape[0]
    return pl.pallas_call(
        _ada_body,
        grid=(depth, n6 // ADA_TILE),
        in_specs=[_const_spec((rows, d)),
                  pl.BlockSpec((1, d, ADA_TILE), lambda l, j: (l, 0, j)),
                  pl.BlockSpec((1, 1, ADA_TILE), lambda l, j: (l, 0, j))],
        out_specs=pl.BlockSpec((1, rows, ADA_TILE), lambda l, j: (l, 0, j)),
        out_shape=jax.ShapeDtypeStruct((depth, rows, n6), F32),
        compiler_params=_params("arbitrary", "arbitrary"),
        name="ada_mod",
    )(cvec, ada_w, ada_b.reshape(depth, 1, n6))


def _tile_specs(nct, d):
    x_spec = pl.BlockSpec((1, TOKEN_TILE, d), lambda b, i: (b, i, 0))
    mod_spec = pl.BlockSpec((1, 1, 6, d), lambda b, i: (b, (i >= nct).astype(I32), 0, 0))
    return x_spec, mod_spec


def _split_specs(nct, d):
    ctx_spec = pl.BlockSpec((1, TOKEN_TILE, d), lambda b, i: (b, jnp.minimum(i, nct - 1), 0))
    lat_spec = pl.BlockSpec((1, TOKEN_TILE, d), lambda b, i: (b, jnp.maximum(i - nct, 0), 0))
    return ctx_spec, lat_spec


def _stream_tile(nct, ctx_ref, lat_ref):
    return jnp.where(pl.program_id(1) < nct, ctx_ref[0], lat_ref[0])


def _inproj0_body(nct, ctx_ref, lat_ref, mod_ref, g_ref, wq, wk, wv, wg, wlow, wu, wa_ref, ba_ref, tri_ref, ones_ref,
                  qd_f, ki_f, ks_f, ed_f, qd_b, ki_b, ks_b, ed_b, ov, og, ou):
    m = mod_ref[0, 0]
    h = _norm_mod(_stream_tile(nct, ctx_ref, lat_ref), g_ref[...], m[0:1], m[1:2]).astype(BF16)
    ov[0] = _dot(h, wv[...]).astype(ov.dtype)
    og[0] = _dot(h, wg[...]).astype(og.dtype)
    u = _dot(h, wu[...])
    for qb in range(ou.shape[0]):
        ou[qb, 0] = u[:, qb * 128:(qb + 1) * 128]
    q = _dot(h, wq[...]) * (GLA_DK ** -0.5)
    k = _dot(h, wk[...])
    low = _dot(h, wlow[...])
    outs = ((qd_f, ki_f, ks_f, ed_f), (qd_b, ki_b, ks_b, ed_b))
    for d, (qd_ref, ki_ref, ks_ref, ed_ref) in enumerate(outs):
        z = _dot3(low, wa_ref[d]) + ba_ref[d]
        log_a = (jnp.minimum(z, 0.0) - jnp.log1p(jnp.exp(-jnp.abs(z)))) * (1.0 / GLA_TAU)
        la_hi, la_lo = _split(log_a)
        cum = _dot(tri_ref[d], la_hi) + _dot(tri_ref[d], la_lo)
        tot = _dot(ones_ref[...], la_hi) + _dot(ones_ref[...], la_lo)
        qd_ref[0] = (q * jnp.exp(cum)).astype(BF16)
        ki_ref[0] = (k * jnp.exp(-cum)).astype(BF16)
        ks_ref[0] = (k * jnp.exp(tot - cum)).astype(BF16)
        for ch in range(TOKEN_TILE // GLA_CHUNK):
            ed_ref[0, ch] = jnp.exp(tot[ch * GLA_CHUNK:ch * GLA_CHUNK + 1])


def _inproj0(ctx, x, mods, norm_g, weights, wa_pad, ba, nct):
    bsz, n_lat, d = x.shape
    t = ctx.shape[1] + n_lat
    tm = TOKEN_TILE
    _, mod_spec = _tile_specs(nct, d)
    ctx_spec, lat_spec = _split_specs(nct, d)
    pos = jnp.arange(tm)
    same_chunk = (pos[:, None] // GLA_CHUNK) == (pos[None, :] // GLA_CHUNK)
    tri = jnp.stack([same_chunk & (pos[None, :] <= pos[:, None]),
                     same_chunk & (pos[None, :] >= pos[:, None])]).astype(BF16)
    ones = same_chunk.astype(BF16)
    consts = list(weights) + [wa_pad, ba, tri, ones]
    tok = lambda w, dt: (pl.BlockSpec((1, tm, w), lambda b, i: (b, i, 0)), jax.ShapeDtypeStruct((bsz, t, w), dt))
    per_chunk = (pl.BlockSpec((1, tm // GLA_CHUNK, 1, AB_QK), lambda b, i: (b, i, 0, 0)),
                 jax.ShapeDtypeStruct((bsz, t // GLA_CHUNK, 1, AB_QK), F32))
    one_dir = [tok(AB_QK, BF16)] * 3 + [per_chunk]
    u_blocks = (pl.BlockSpec((S5_CH // 128, 1, tm, 128), lambda b, i: (0, b, i, 0)),
                jax.ShapeDtypeStruct((S5_CH // 128, bsz, t, 128), F32))
    outs = one_dir + one_dir + [tok(AB_V, BF16), tok(AB_V, BF16), u_blocks]
    return pl.pallas_call(
        functools.partial(_inproj0_body, nct),
        grid=(bsz, t // tm),
        in_specs=[ctx_spec, lat_spec, mod_spec, _const_spec((1, d))] + [_const_spec(a.shape) for a in consts],
        out_specs=[o[0] for o in outs],
        out_shape=[o[1] for o in outs],
        compiler_params=_params("arbitrary", "arbitrary"),
        name="inproj_gla_s5",
    )(ctx, x, mods, norm_g.reshape(1, d), *consts)


def _rope(acc, cos_ref, sin_ref, o_ref, scale):
    for grp in range(acc.shape[1] // 128):
        half = grp % 2
        xg = acc[:, grp * 128:(grp + 1) * 128]
        cs = cos_ref[:, half * 128:(half + 1) * 128]
        sn = sin_ref[:, half * 128:(half + 1) * 128]
        out = xg * cs + pltpu.roll(xg, 64, 1) * sn
        o_ref[0, :, grp * 128:(grp + 1) * 128] = (out * scale).astype(o_ref.dtype)


def _moe_residual(x1_ref, g2, yk_ref, w_ref):
    d = x1_ref.shape[2]
    half = d // 2
    y_lo, y_hi = None, None
    for k in range(TOP_K):
        lo, hi = _unpack_rows(yk_ref[k, 0])
        wk = w_ref[0, :, k:k + 1]
        y_lo = lo * wk if y_lo is None else y_lo + lo * wk
        y_hi = hi * wk if y_hi is None else y_hi + hi * wk
    return x1_ref[0, :, 0:half] + g2[:, 0:half] * y_lo, x1_ref[0, :, half:d] + g2[:, half:d] * y_hi


def _inproj1_body(x1_ref, mod0_ref, yk_ref, w_ref, mod_ref, g_ref, cos_ref, sin_ref, wq, wk, wv, wg, ox, oq, ok, ov, og):
    half = x1_ref.shape[2] // 2
    x2_lo, x2_hi = _moe_residual(x1_ref, mod0_ref[0, 0][5:6], yk_ref, w_ref)
    ox[0, :, 0:half] = x2_lo
    ox[0, :, half:2 * half] = x2_hi
    m = mod_ref[0, 0]
    h = _norm_mod(jnp.concatenate([x2_lo, x2_hi], axis=1), g_ref[...], m[0:1], m[1:2]).astype(BF16)
    _rope(_dot(h, wq[...]), cos_ref, sin_ref, oq, 1.0)
    _rope(_dot(h, wk[...]), cos_ref, sin_ref, ok, RET_DK ** -0.5)
    ov[0] = _dot(h, wv[...]).astype(ov.dtype)
    og[0] = _dot(h, wg[...]).astype(og.dtype)


def _inproj1(x1, mods0, yk, w_tok, mods, norm_g, cos_t, sin_t, weights, nct):
    bsz, t, d = x1.shape
    tm = TOKEN_TILE
    x_spec, mod_spec = _tile_specs(nct, d)
    tok = lambda w: pl.BlockSpec((1, tm, w), lambda b, i: (b, i, 0))
    tab_spec = pl.BlockSpec((tm, RET_DK), lambda b, i: (i, 0))
    return pl.pallas_call(
        _inproj1_body,
        grid=(bsz, t // tm),
        in_specs=[x_spec, mod_spec, pl.BlockSpec((TOP_K, 1, tm, d // 2), lambda b, i: (0, b, i, 0)), tok(TOP_K),
                  mod_spec, _const_spec((1, d)), tab_spec, tab_spec] + [_const_spec(w.shape) for w in weights],
        out_specs=[tok(d)] + [tok(w.shape[1]) for w in weights],
        out_shape=[jax.ShapeDtypeStruct((bsz, t, d), F32)]
        + [jax.ShapeDtypeStruct((bsz, t, w.shape[1]), BF16) for w in weights],
        compiler_params=_params("arbitrary", "arbitrary"),
        name="inproj_retention",
    )(x1, mods0, yk.reshape(TOP_K, bsz, t, d // 2), w_tok.reshape(bsz, t, TOP_K), mods, norm_g.reshape(1, d),
      cos_t, sin_t, *weights)


def _backward_chunk(n, n_ctx_chunks, n_chunks):
    return jnp.where(n < n_ctx_chunks, n_ctx_chunks - 1 - n, n_chunks - 1 - (n - n_ctx_chunks))


def _gla_body(qd_f, ki_f, ks_f, ed_f, v_f, qd_b, ki_b, ks_b, ed_b, v_b, hmask_ref, bdmask_ref, o_f, o_b, st_f, st_b):
    c = GLA_CHUNK

    @pl.when(pl.program_id(1) == 0)
    def _():
        st_f[...] = jnp.zeros_like(st_f)
        st_b[...] = jnp.zeros_like(st_b)

    r4 = lax.broadcasted_iota(I32, (GLA_HEADS * c, c), 0) & (c - 1)
    c4 = lax.broadcasted_iota(I32, (GLA_HEADS * c, c), 1)
    dirs = ((qd_f, ki_f, ks_f, ed_f, v_f, o_f, st_f), (qd_b, ki_b, ks_b, ed_b, v_b, o_b, st_b))
    chains = [(bb, d) + dirs[d] for bb in range(qd_f.shape[0]) for d in range(2)]
    scores, inter, grow = [], [], []
    for bb, d, qd_ref, ki_ref, ks_ref, ed_ref, v_ref, o_ref, st_ref in chains:
        q_dec = qd_ref[bb]
        q_heads = jnp.concatenate([q_dec] * GLA_HEADS, axis=0) * hmask_ref[...]
        seen4 = (c4 <= r4) if d == 0 else (c4 >= r4)
        scores.append(jnp.where(seen4, _dot_nt(q_heads, ki_ref[bb]), 0.0).astype(BF16))
        inter.append(_dot_nt(q_dec, st_ref[bb].astype(BF16)))
        grow.append(_dot_tn(v_ref[bb], ks_ref[bb]))
    for (bb, d, qd_ref, ki_ref, ks_ref, ed_ref, v_ref, o_ref, st_ref), sc, o_inter, dst in zip(chains, scores, inter, grow):
        v = v_ref[bb]
        o_intra = jnp.concatenate(
            [_dot(sc[h * c:(h + 1) * c], v[:, h * GLA_DV:(h + 1) * GLA_DV]) for h in range(GLA_HEADS)], axis=1)
        o_ref[bb] = o_intra + o_inter
        st_ref[bb] = st_ref[bb] * ed_ref[bb, 0] + bdmask_ref[...] * dst


def _gla(per_dir, v, n_ctx):
    bsz, t, _ = v.shape
    nc, ncc = t // GLA_CHUNK, n_ctx // GLA_CHUNK
    gb = GLA_BATCH
    fwd = lambda b, n: (b, n, 0)
    bwd = lambda b, n: (b, _backward_chunk(n, ncc, nc), 0)
    hmask = (jnp.arange(AB_QK)[:, None] // GLA_CHUNK == jnp.arange(AB_QK)[None, :] // GLA_DK).astype(BF16)
    bdmask = (jnp.arange(AB_V)[:, None] // GLA_DV == jnp.arange(AB_QK)[None, :] // GLA_DK).astype(F32)

    def specs(idx):
        idx4 = lambda b, n: idx(b, n) + (0,)
        return [pl.BlockSpec((gb, GLA_CHUNK, AB_QK), idx)] * 3 + [pl.BlockSpec((gb, 1, 1, AB_QK), idx4),
                                                                  pl.BlockSpec((gb, GLA_CHUNK, AB_V), idx)]

    return pl.pallas_call(
        _gla_body,
        grid=(bsz // gb, nc),
        in_specs=specs(fwd) + specs(bwd) + [_const_spec(hmask.shape), _const_spec(bdmask.shape)],
        out_specs=[pl.BlockSpec((gb, GLA_CHUNK, AB_V), fwd), pl.BlockSpec((gb, GLA_CHUNK, AB_V), bwd)],
        out_shape=[jax.ShapeDtypeStruct((bsz, t, AB_V), F32)] * 2,
        scratch_shapes=[pltpu.VMEM((gb, AB_V, AB_QK), F32)] * 2,
        compiler_params=_params("arbitrary", "arbitrary"),
        name="gla_scan",
    )(*per_dir[0], v, *per_dir[1], v, hmask, bdmask)


def _cmul(x, y):
    return x[0] * y[0] - x[1] * y[1], x[0] * y[1] + x[1] * y[0]


def _s5_operators(lam_re, lam_im, log_step, b_re, b_im, c_re, c_im):
    ln = S5_CHUNK
    step = jnp.exp(log_step.astype(F32))[..., None]
    lam_re, lam_im = lam_re.astype(F32), lam_im.astype(F32)
    mag = jnp.exp(lam_re * step)
    a = (mag * jnp.cos(lam_im * step), mag * jnp.sin(lam_im * step))
    den = lam_re * lam_re + lam_im * lam_im
    f_re = ((a[0] - 1.0) * lam_re + a[1] * lam_im) / den
    f_im = (a[1] * lam_re - (a[0] - 1.0) * lam_im) / den
    bt_re, bt_im = b_re.transpose(0, 2, 1), b_im.transpose(0, 2, 1)
    bb = _cmul((f_re[:, :, None, :], f_im[:, :, None, :]), (bt_re, bt_im))
    bbt = jnp.concatenate([bb[0], -bb[1]], axis=-1)
    pw = (a[0][:, :, None, :], a[1][:, :, None, :])
    while pw[0].shape[2] < ln:
        top = (pw[0][:, :, -1:, :], pw[1][:, :, -1:, :])
        nxt = _cmul(top, pw)
        pw = (jnp.concatenate([pw[0], nxt[0]], axis=2), jnp.concatenate([pw[1], nxt[1]], axis=2))
    pw = (jnp.concatenate([jnp.ones_like(pw[0][:, :, :1]), pw[0]], axis=2),
          jnp.concatenate([jnp.zeros_like(pw[1][:, :, :1]), pw[1]], axis=2))
    ca = _cmul((c_re[:, :, None], c_im[:, :, None]), (pw[0][:, :, :, None, :], pw[1][:, :, :, None, :]))
    by_dir = lambda arr, lo, flip_d: jnp.stack([jnp.flip(arr[d, :, lo:lo + ln], axis=1) if d == flip_d
                                                else arr[d, :, lo:lo + ln] for d in range(2)])
    rows = lambda arr: arr.reshape(2, S5_GROUPS, ln * S5_GROUP, 2 * S5_P)
    cab = rows(by_dir(jnp.concatenate([ca[0], ca[1]], axis=-1), 0, 1))
    cab2 = rows(by_dir(jnp.concatenate([ca[0], -ca[1]], axis=-1), 1, 1)).astype(BF16)
    pwx = by_dir(jnp.concatenate([pw[0], pw[1]], axis=-1), 0, 0)
    lr, li = pw[0][:, :, ln], pw[1][:, :, ln]
    ac_rows = [jnp.concatenate([lr, lr], -1), jnp.concatenate([-li, li], -1), jnp.concatenate([li, -li], -1)]
    ac = jnp.stack(ac_rows + [jnp.zeros_like(ac_rows[0])] * 5, axis=2)
    return cab, cab2, bbt, pwx, ac


def _s5_group_operators(gg, cab_ref, bbt_ref, pwx_ref, tz, wx):
    ln, ch, p = S5_CHUNK, S5_GROUP, S5_P
    lane = lax.broadcasted_iota(I32, (ch, ln * ch), 1)
    for d in range(2):
        kern = _dot3(bbt_ref[d, gg], cab_ref[d, gg], dot=_dot_nt)
        bt = bbt_ref[d, gg]
        b_re, b_im = bt[:, 0:p], -bt[:, p:2 * p]
        for j in range(ln):
            if d == 0:
                blk = jnp.where(lane >= j * ch, kern if j == 0 else pltpu.roll(kern, j * ch, 1), 0.0)
            else:
                blk = jnp.where(lane < (j + 1) * ch, kern if j == ln - 1 else pltpu.roll(kern, (j + 1) * ch, 1), 0.0)
            tz[gg, d, j * ch:(j + 1) * ch, :] = blk.astype(BF16)
            pr, pi = pwx_ref[d, gg, j:j + 1, 0:p], pwx_ref[d, gg, j:j + 1, p:2 * p]
            x_re, x_im = pr * b_re - pi * b_im, pr * b_im + pi * b_re
            wx[gg, d, j * ch:(j + 1) * ch, :] = jnp.concatenate([x_re, x_im, x_im, x_re], axis=1).astype(BF16)


def _s5_placement(pall):
    rows, cols = pall.shape[1], pall.shape[2]
    row = lax.broadcasted_iota(I32, (rows, cols), 0)
    col = lax.broadcasted_iota(I32, (rows, cols), 1)
    same_token = (row >> 7) == (col >> 4)
    for g8 in range(pall.shape[0]):
        pall[g8] = jnp.where(same_token & ((row & 127) == g8 * S5_GROUP + (col & (S5_GROUP - 1))), 1.0, 0.0).astype(BF16)


def _first_step():
    return (pl.program_id(0) == 0) & (pl.program_id(1) == 0)


def _s5_fold_body(ncs, u_ref, o_ref, pall, ucat):
    @pl.when(_first_step())
    def _():
        _s5_placement(pall)

    for b in range(u_ref.shape[1]):
        for j in range(S5_CHUNK):
            ucat[b * ncs:(b + 1) * ncs, j * 128:(j + 1) * 128] = u_ref[0, b, pl.ds(j, ncs, stride=S5_CHUNK), :].astype(BF16)
    for g8 in range(pall.shape[0]):
        o_ref[g8] = _dot(ucat[...], pall[g8]).astype(BF16)


def _s5_unfold_body(ncs, y_ref, o_ref, pall):
    @pl.when(_first_step())
    def _():
        _s5_placement(pall)

    def token_pair(i2, carry):
        r0 = pl.multiple_of(i2 * 256, 256)
        acc = _dot_nt(y_ref[0], pall[0, pl.ds(r0, 256), :])
        for g8 in range(1, pall.shape[0]):
            acc = acc + _dot_nt(y_ref[g8], pall[g8, pl.ds(r0, 256), :])
        for b in range(o_ref.shape[1]):
            for par in range(2):
                o_ref[0, b, pl.ds(2 * i2 + par, ncs, stride=S5_CHUNK), :] = (
                    acc[b * ncs:(b + 1) * ncs, par * 128:(par + 1) * 128])
        return carry

    lax.fori_loop(0, S5_CHUNK // 2, token_pair, 0)


def _s5_body(ncs_ctx, ncs, rows, u_ref, cab_ref, cab2_ref, bbt_ref, pwx_ref, ac_ref, y_ref, tz, wx, *vecs):
    half = 2 * S5_P
    n_groups = u_ref.shape[0]
    groups = [vecs[6 * gg:6 * gg + 6] for gg in range(n_groups)]
    for gg, (xx_f, xs_f, xx_b, xs_b, _, _) in enumerate(groups):
        _s5_group_operators(gg, cab_ref, bbt_ref, pwx_ref, tz, wx)
        for d, (xx, xs) in enumerate(((xx_f, xs_f), (xx_b, xs_b))):
            r = _dot(u_ref[gg], wx[gg, d])
            xx[...] = r[:, :half]
            xs[...] = r[:, half:]

    def advance(ac, s, s_sw, x, x_sw):
        return ac[0:1] * s + ac[1:2] * s_sw + x, ac[0:1] * s_sw + ac[2:3] * s + x_sw

    def step(n, carry):
        at_f = pl.ds(n, rows, stride=ncs)
        at_b = pl.ds(_backward_chunk(n, ncs_ctx, ncs), rows, stride=ncs)
        out = []
        for gg, (xx_f, xs_f, xx_b, xs_b, sin_f, sin_b) in enumerate(groups):
            s_f, sw_f, s_b, sw_b = carry[4 * gg:4 * gg + 4]
            sin_f[at_f, :] = s_f
            sin_b[at_b, :] = s_b
            out += advance(ac_ref[0, gg], s_f, sw_f, xx_f[at_f, :], xs_f[at_f, :])
            out += advance(ac_ref[1, gg], s_b, sw_b, xx_b[at_b, :], xs_b[at_b, :])
        return tuple(out)

    zero = jnp.zeros((rows, half), F32)
    lax.fori_loop(0, ncs, step, (zero,) * (4 * n_groups))
    for gg, (_, _, _, _, sin_f, sin_b) in enumerate(groups):
        u = u_ref[gg]
        y_ref[gg] = (_dot(u, tz[gg, 0]) + _dot(u, tz[gg, 1]) + _dot_nt(sin_f[...].astype(BF16), cab2_ref[0, gg])
                     + _dot_nt(sin_b[...].astype(BF16), cab2_ref[1, gg])).astype(BF16)


def _s5(u4, ops, n_ctx):
    nq, bsz, t, _ = u4.shape
    ln, lanes = S5_CHUNK, S5_CHUNK * S5_GROUP
    gq = S5_GROUPS // nq
    ncs, ncs_ctx = t // ln, n_ctx // ln
    m = ncs * bsz
    hb = S5_FOLD_BATCH
    tok_spec = pl.BlockSpec((1, hb, t, 128), lambda q, h: (q, h, 0, 0))
    grp_spec = pl.BlockSpec((gq, hb * ncs, lanes), lambda q, h: (q, h, 0))
    pall = pltpu.VMEM((gq, ln * 128, lanes), BF16)
    ug = pl.pallas_call(
        functools.partial(_s5_fold_body, ncs),
        grid=(nq, bsz // hb),
        in_specs=[tok_spec],
        out_specs=grp_spec,
        out_shape=jax.ShapeDtypeStruct((S5_GROUPS, m, lanes), BF16),
        scratch_shapes=[pall, pltpu.VMEM((hb * ncs, ln * 128), BF16)],
        compiler_params=_params("arbitrary", "arbitrary"),
        name="s5_fold",
    )(u4)
    sg = S5_SCAN_GROUPS
    dir_spec = lambda arr: pl.BlockSpec((2, sg) + arr.shape[2:], lambda g: (0, g, 0, 0))
    yg = pl.pallas_call(
        functools.partial(_s5_body, ncs_ctx, ncs, bsz),
        grid=(S5_GROUPS // sg,),
        in_specs=[pl.BlockSpec((sg, m, lanes), lambda g: (g, 0, 0))] + [dir_spec(arr) for arr in ops],
        out_specs=pl.BlockSpec((sg, m, lanes), lambda g: (g, 0, 0)),
        out_shape=jax.ShapeDtypeStruct((S5_GROUPS, m, lanes), BF16),
        scratch_shapes=[pltpu.VMEM((sg, 2, lanes, lanes), BF16)] * 2 + [pltpu.VMEM((m, 2 * S5_P), F32)] * (6 * sg),
        compiler_params=_params("arbitrary"),
        name="s5_scan",
    )(ug, *ops)
    return pl.pallas_call(
        functools.partial(_s5_unfold_body, ncs),
        grid=(nq, bsz // hb),
        in_specs=[grp_spec],
        out_specs=tok_spec,
        out_shape=jax.ShapeDtypeStruct(u4.shape, F32),
        scratch_shapes=[pall],
        compiler_params=_params("arbitrary", "arbitrary"),
        name="s5_unfold",
    )(yg)


def _ret_body(ncc, q_f, k_f, v_f, q_b, k_b, v_b, dmat_ref, rsc_ref, csc_ref, gam_ref, o_f, o_b, st_f, st_b):
    n = pl.program_id(1)

    @pl.when(n == 0)
    def _():
        st_f[...] = jnp.zeros_like(st_f)
        st_b[...] = jnp.zeros_like(st_b)

    dirs = ((q_f, k_f, v_f, o_f, st_f), (q_b, k_b, v_b, o_b, st_b))

    @pl.when(n >= ncc)
    def _():
        for d, (q_ref, k_ref, v_ref, o_ref, st_ref) in enumerate(dirs):
            for h in range(RET_HEADS):
                qh = q_ref[0, :, h * RET_DK:(h + 1) * RET_DK]
                kh = k_ref[0, :, h * RET_DK:(h + 1) * RET_DK]
                vh = v_ref[0, :, h * RET_DV:(h + 1) * RET_DV]
                scores = (_dot_nt(qh, kh) * dmat_ref[d, h]).astype(BF16)
                o = _dot(scores, vh) + rsc_ref[d, h] * _dot(qh, st_ref[h].astype(BF16))
                o_ref[0, :, h * RET_DV:(h + 1) * RET_DV] = o.astype(o_ref.dtype)

    for d, (q_ref, k_ref, v_ref, o_ref, st_ref) in enumerate(dirs):
        for h in range(RET_HEADS):
            kh = k_ref[0, :, h * RET_DK:(h + 1) * RET_DK]
            vh = v_ref[0, :, h * RET_DV:(h + 1) * RET_DV]
            k_state = (kh.astype(F32) * csc_ref[d, h]).astype(BF16)
            st_ref[h] = st_ref[h] * gam_ref[d, h] + _dot_tn(k_state, vh)


def _retention(q, k, v, decay_logit, n_ctx):
    bsz, t, _ = q.shape
    c = RET_CHUNK
    nc, ncc = t // c, n_ctx // c
    nl = nc - ncc
    log_gamma = jax.nn.log_sigmoid(decay_logit.astype(F32))[:, :, None, None]
    i = jnp.arange(c, dtype=F32)
    lag = i[:, None] - i[None, :]
    lag = jnp.stack([lag, -lag])[:, None]
    dmat = jnp.where(lag >= 0, jnp.exp(log_gamma * jnp.maximum(lag, 0.0)), 0.0)
    done = jnp.stack([i + 1.0, c - i])[:, None, :, None]
    rsc = jnp.exp(log_gamma * done)
    csc = jnp.exp(log_gamma * (c - done))
    gam = jnp.exp(log_gamma[:, :, 0, 0] * c)
    fwd = lambda b, n: (b, n, 0)
    bwd = lambda b, n: (b, _backward_chunk(n, ncc, nc), 0)
    o_fwd = lambda b, n: (b, jnp.maximum(n - ncc, 0), 0)
    o_bwd = lambda b, n: (b, nl - 1 - jnp.maximum(n - ncc, 0), 0)

    def specs(idx):
        return [pl.BlockSpec((1, c, RET_QK), idx), pl.BlockSpec((1, c, RET_QK), idx), pl.BlockSpec((1, c, RET_MIX), idx)]

    return pl.pallas_call(
        functools.partial(_ret_body, ncc),
        grid=(bsz, nc),
        in_specs=specs(fwd) + specs(bwd) + [_const_spec(dmat.shape), _const_spec(rsc.shape), _const_spec(csc.shape),
                                            pl.BlockSpec(memory_space=pltpu.SMEM)],
        out_specs=[pl.BlockSpec((1, c, RET_MIX), o_fwd), pl.BlockSpec((1, c, RET_MIX), o_bwd)],
        out_shape=[jax.ShapeDtypeStruct((bsz, nl * c, RET_MIX), BF16)] * 2,
        scratch_shapes=[pltpu.VMEM((RET_HEADS, RET_DK, RET_DV), F32)] * 2,
        compiler_params=_params("arbitrary", "arbitrary"),
        name="retention_scan",
    )(q, k, v, q, k, v, dmat, rsc, csc, gam)


def _route(x, mixed, mod, n2g_ref, wr_ref, br_ref, x1_ref, h2_ref, e_ref, w_ref, r_ref, cnt_ref):
    tm = x.shape[0]
    x1 = x + mod[2:3] * mixed
    x1_ref[0] = x1
    h2 = _norm_mod(x1, n2g_ref[...], mod[3:4], mod[4:5])
    h2_ref[0] = _pack_rows(h2)
    logits = _dot3(wr_ref[...], h2, dot=_dot_nt) + br_ref[...]
    ie = lax.broadcasted_iota(I32, logits.shape, 0)
    tops, picks = [], []
    for _ in range(TOP_K):
        mx = jnp.max(logits, axis=0, keepdims=True)
        pick = jnp.min(jnp.where(logits == mx, ie, N_EXPERTS), axis=0, keepdims=True)
        tops.append(mx)
        picks.append(pick)
        logits = jnp.where(ie == pick, -jnp.inf, logits)
    ex = [jnp.exp(tk - tops[0]) for tk in tops]
    den = ex[0] + ex[1] + ex[2] + ex[3]
    for kk in range(TOP_K):
        w_ref[0, kk:kk + 1, :] = ex[kk] / den
        e_ref[0, kk:kk + 1, :] = picks[kk]

    @pl.when((pl.program_id(0) == 0) & (pl.program_id(1) == 0))
    def _():
        cnt_ref[...] = jnp.zeros_like(cnt_ref)

    earlier = (lax.broadcasted_iota(I32, (tm, tm), 0) < lax.broadcasted_iota(I32, (tm, tm), 1))
    earlier = jnp.where(earlier, 1.0, 0.0).astype(BF16)
    run = cnt_ref[:, 0:1]
    for kk, pick in enumerate(picks):
        onehot = jnp.where(ie == pick, 1.0, 0.0)
        before = _dot(onehot.astype(BF16), earlier) + run
        r_ref[0, kk:kk + 1, :] = jnp.sum(onehot * before, axis=0, keepdims=True).astype(I32)
        run = run + jnp.sum(onehot, axis=1, keepdims=True)
    cnt_ref[...] = jnp.broadcast_to(run, cnt_ref.shape)


def _mix0_body(nct, ctx_ref, lat_ref, mod_ref, of_ref, ob_ref, g_ref, ys_ref, u_ref, gng_ref, dsk_ref, gluw_ref,
               glub_ref, wo_ref, n2g_ref, wr_ref, br_ref, x1_ref, h2_ref, e_ref, w_ref, r_ref, cnt_ref):
    o = of_ref[0] + ob_ref[0]
    heads = []
    for h in range(GLA_HEADS):
        oh = o[:, h * GLA_DV:(h + 1) * GLA_DV]
        heads.append(oh * lax.rsqrt(jnp.mean(oh * oh, axis=-1, keepdims=True) + EPS))
    gla = jnp.concatenate(heads, axis=1) * gng_ref[...] * _silu(g_ref[0].astype(F32))
    lane_blocks = lambda ref: jnp.concatenate([ref[qb, 0] for qb in range(ref.shape[0])], axis=1)
    y = jax.nn.gelu(lane_blocks(ys_ref) + dsk_ref[...] * lane_blocks(u_ref))
    y = y * jax.nn.sigmoid(_dot(y.astype(BF16), gluw_ref[...]) + glub_ref[...])
    mixed = _dot(gla.astype(BF16), wo_ref[0:AB_V]) + _dot(y.astype(BF16), wo_ref[AB_V:AB_V + S5_CH])
    _route(_stream_tile(nct, ctx_ref, lat_ref), mixed, mod_ref[0, 0], n2g_ref, wr_ref, br_ref,
           x1_ref, h2_ref, e_ref, w_ref, r_ref, cnt_ref)


def _mix1_body(x_ref, mod_ref, of_ref, ob_ref, g_ref, ng_ref, wo_ref, n2g_ref, wr_ref, br_ref,
               x1_ref, h2_ref, e_ref, w_ref, r_ref, cnt_ref):
    mixed = None
    for h in range(RET_HEADS):
        sl = slice(h * RET_DV, (h + 1) * RET_DV)
        oh = of_ref[0, :, sl].astype(F32) + ob_ref[0, :, sl].astype(F32)
        mu = jnp.mean(oh, axis=-1, keepdims=True)
        cen = oh - mu
        var = jnp.mean(cen * cen, axis=-1, keepdims=True)
        gated = cen * lax.rsqrt(var + EPS) * ng_ref[:, sl] * _silu(g_ref[0, :, sl].astype(F32))
        part = _dot(gated.astype(BF16), wo_ref[sl])
        mixed = part if mixed is None else mixed + part
    _route(x_ref[0], mixed, mod_ref[0, 0], n2g_ref, wr_ref, br_ref, x1_ref, h2_ref, e_ref, w_ref, r_ref, cnt_ref)


def _mix_call(body, name, stream, mods, tiles, acts, consts, norm2_g, w_router, b_router, n_tok, seg_tile0):
    bsz, _, d = stream[-1].shape
    tm = TOKEN_TILE
    off = lambda b, i: (b, i + seg_tile0, 0)
    loc = lambda b, i: (b, i, 0)
    ntl = bsz * tiles
    flat = lambda b, i: (b * tiles + i, 0, 0)
    in_specs = list(_split_specs(n_tok, d)) if len(stream) == 2 else [pl.BlockSpec((1, tm, d), off)]
    in_specs.append(pl.BlockSpec((1, 1, 6, d), lambda b, i: (b, ((i + seg_tile0) >= n_tok).astype(I32), 0, 0)))
    args = list(stream) + [mods]
    for arr, offset in acts:
        if arr.ndim == 4:
            in_specs.append(pl.BlockSpec((arr.shape[0], 1, tm, arr.shape[3]), lambda b, i: (0, b, i, 0)))
        else:
            in_specs.append(pl.BlockSpec((1, tm, arr.shape[2]), off if offset else loc))
        args.append(arr)
    tail = list(consts) + [norm2_g.reshape(1, d), w_router.T, b_router.reshape(N_EXPERTS, 1)]
    in_specs += [_const_spec(a.shape) for a in tail]
    args += tail
    tok_out = pl.BlockSpec((1, TOP_K, tm), flat)
    return pl.pallas_call(
        body,
        grid=(bsz, tiles),
        in_specs=in_specs,
        out_specs=[pl.BlockSpec((1, tm, d), loc), pl.BlockSpec((1, tm, d // 2), loc), tok_out, tok_out, tok_out,
                   _const_spec((N_EXPERTS, 128))],
        out_shape=[jax.ShapeDtypeStruct((bsz, tiles * tm, d), F32), jax.ShapeDtypeStruct((bsz, tiles * tm, d // 2), U32),
                   jax.ShapeDtypeStruct((ntl, TOP_K, tm), I32), jax.ShapeDtypeStruct((ntl, TOP_K, tm), F32),
                   jax.ShapeDtypeStruct((ntl, TOP_K, tm), I32), jax.ShapeDtypeStruct((N_EXPERTS, 128), F32)],
        compiler_params=_params("arbitrary", "arbitrary"),
        name=name,
    )(*args)


def _cast_rows(src_ref, dst_ref, rows):
    def chunk(j, carry):
        r = pl.multiple_of(j * rows, rows)
        dst_ref[pl.ds(r, rows), :] = src_ref[0, 0, pl.ds(r, rows), :].astype(BF16)
        return carry

    lax.fori_loop(0, dst_ref.shape[0] // rows, chunk, 0)


def _expert_body(be_ref, nu_ref, x_ref, wgu_ref, bgu_ref, wd_ref, bd_ref, o_ref, wgu_bf, wd_bf):
    i = pl.program_id(0)
    live = i < nu_ref[0]
    new_expert = (i == 0) | (be_ref[i] != be_ref[jnp.maximum(i - 1, 0)])

    @pl.when(live & new_expert)
    def _():
        _cast_rows(wgu_ref, wgu_bf, 128)
        _cast_rows(wd_ref, wd_bf, 128)

    @pl.when(live)
    def _():
        x_lo, x_hi = _unpack_rows(x_ref[...])
        half = x_lo.shape[1]
        gu = (_dot(x_lo.astype(BF16), wgu_bf[0:half]) + _dot(x_hi.astype(BF16), wgu_bf[half:2 * half])
              + bgu_ref[0, 0])
        gate = jnp.minimum(gu[:, :D_FF], SWIGLU_LIMIT)
        lin = jnp.clip(gu[:, D_FF:], -SWIGLU_LIMIT, SWIGLU_LIMIT)
        act = gate * jax.nn.sigmoid(SWIGLU_ALPHA * gate) * (lin + 1.0)
        y = _dot(act.astype(BF16), wd_bf[...]) + bd_ref[0, 0]
        o_ref[...] = _pack_rows(y)

    @pl.when(i >= nu_ref[0])
    def _():
        o_ref[...] = jnp.zeros_like(o_ref)


def _experts(xb, block_e, n_used, layer, w_gu, b_gu, w_down, b_down):
    n_slots, half = xb.shape
    d = 2 * half
    n_blocks = n_slots // MOE_BLOCK
    depth = w_gu.shape[0]
    by_expert = lambda i, be, nu: (layer, be[i], 0, 0)
    return pl.pallas_call(
        _expert_body,
        grid_spec=pltpu.PrefetchScalarGridSpec(
            num_scalar_prefetch=2,
            grid=(n_blocks,),
            in_specs=[pl.BlockSpec((MOE_BLOCK, half), lambda i, be, nu: (i, 0)),
                      pl.BlockSpec((1, 1, d, 2 * D_FF), by_expert), pl.BlockSpec((1, 1, 1, 2 * D_FF), by_expert),
                      pl.BlockSpec((1, 1, D_FF, d), by_expert), pl.BlockSpec((1, 1, 1, d), by_expert)],
            out_specs=pl.BlockSpec((MOE_BLOCK, half), lambda i, be, nu: (i, 0)),
            scratch_shapes=[pltpu.VMEM((d, 2 * D_FF), BF16), pltpu.VMEM((D_FF, d), BF16)]),
        out_shape=jax.ShapeDtypeStruct((n_slots, half), U32),
        compiler_params=_params("arbitrary"),
        name="moe_experts",
    )(block_e, n_used, xb, w_gu, b_gu.reshape(depth, N_EXPERTS, 1, 2 * D_FF), w_down,
      b_down.reshape(depth, N_EXPERTS, 1, d))


def _combine_body(x1_ref, mod_ref, yk_ref, w_ref, fg_ref, o_ref):
    d = x1_ref.shape[2]
    half = d // 2
    x2_lo, x2_hi = _moe_residual(x1_ref, mod_ref[0, 0][5:6], yk_ref, w_ref)
    ms = (jnp.sum(x2_lo * x2_lo, axis=-1, keepdims=True) + jnp.sum(x2_hi * x2_hi, axis=-1, keepdims=True)) / d
    r = lax.rsqrt(ms + EPS)
    o_ref[0, :, 0:half] = x2_lo * r * fg_ref[:, 0:half]
    o_ref[0, :, half:d] = x2_hi * r * fg_ref[:, half:d]


def _combine(x1, mods, yk, w_tok, seg_tile0, n_tok, final_g):
    bsz, t, d = x1.shape
    tm = TOKEN_TILE
    loc = lambda b, i: (b, i, 0)
    return pl.pallas_call(
        _combine_body,
        grid=(bsz, t // tm),
        in_specs=[pl.BlockSpec((1, tm, d), loc),
                  pl.BlockSpec((1, 1, 6, d), lambda b, i: (b, ((i + seg_tile0) >= n_tok).astype(I32), 0, 0)),
                  pl.BlockSpec((TOP_K, 1, tm, d // 2), lambda b, i: (0, b, i, 0)),
                  pl.BlockSpec((1, tm, TOP_K), loc), _const_spec((1, d))],
        out_specs=pl.BlockSpec((1, tm, d), loc),
        out_shape=jax.ShapeDtypeStruct((bsz, t, d), F32),
        compiler_params=_params("arbitrary", "arbitrary"),
        name="moe_combine",
    )(x1, mods, yk.reshape(TOP_K, bsz, t, d // 2), w_tok.reshape(bsz, t, TOP_K), final_g.reshape(1, d))


def _sc_mesh():
    return plsc.VectorSubcoreMesh(core_axis_name="core", subcore_axis_name="subcore",
                                  num_cores=SC_CORES, num_subcores=SC_SUBCORES)


def _sc_worker_base(per_worker):
    return (lax.axis_index("subcore") * SC_CORES + lax.axis_index("core")) * per_worker


def _sc_dispatch(rows, dest, n_slots):
    n, w = rows.shape
    per_worker = n // SC_WORKERS
    assert per_worker * SC_WORKERS == n and per_worker % SC_CHUNK == 0

    @functools.partial(
        pl.kernel, mesh=_sc_mesh(), out_type=jax.ShapeDtypeStruct((n_slots, w), rows.dtype),
        scratch_types=[pltpu.VMEM((SC_CHUNK,), I32)] * TOP_K + [pltpu.VMEM((SC_CHUNK, w), rows.dtype),
                                                                pltpu.SemaphoreType.DMA],
        name="moe_dispatch")
    def scatter_rows(rows_hbm, dest_hbm, out_hbm, *scratch):
        idx_refs, buf, sem = scratch[:TOP_K], scratch[TOP_K], scratch[TOP_K + 1]
        base0 = _sc_worker_base(per_worker)

        @pl.loop(0, per_worker // SC_CHUNK)
        def _(j):
            base = base0 + j * SC_CHUNK
            pltpu.sync_copy(rows_hbm.at[pl.ds(base, SC_CHUNK)], buf)
            for k, idx in enumerate(idx_refs):
                pltpu.sync_copy(dest_hbm.at[pl.ds(k * n + base, SC_CHUNK)], idx)
            copies = [pltpu.make_async_copy(buf, out_hbm.at[idx], sem) for idx in idx_refs]
            for cp in copies:
                cp.start()
            for cp in copies:
                cp.wait()

    return scatter_rows(rows, dest)


def _sc_gather(table, idx):
    n = idx.shape[0]
    w = table.shape[1]
    per_worker = n // SC_WORKERS
    n_chunks = per_worker // SC_CHUNK
    assert per_worker * SC_WORKERS == n and n_chunks * SC_CHUNK == per_worker and n_chunks % 2 == 0

    @functools.partial(
        pl.kernel, mesh=_sc_mesh(), out_type=jax.ShapeDtypeStruct((n, w), table.dtype),
        scratch_types=([pltpu.VMEM((SC_CHUNK,), I32)] * 2 + [pltpu.VMEM((SC_CHUNK, w), table.dtype)] * 2
                       + [pltpu.SemaphoreType.DMA] * 4),
        name="moe_gather")
    def gather_rows(table_hbm, idx_hbm, out_hbm, idx0, idx1, buf0, buf1, gsem0, gsem1, wsem0, wsem1):
        base0 = _sc_worker_base(per_worker)

        def gather_copy(idx_v, buf, sem):
            return pltpu.make_async_copy(table_hbm.at[idx_v], buf, sem)

        def write_copy(j, buf, sem):
            return pltpu.make_async_copy(buf, out_hbm.at[pl.ds(base0 + j * SC_CHUNK, SC_CHUNK)], sem)

        def start_gather(j, idx_v, buf, sem):
            pltpu.sync_copy(idx_hbm.at[pl.ds(base0 + j * SC_CHUNK, SC_CHUNK)], idx_v)
            gather_copy(idx_v, buf, sem).start()

        start_gather(0, idx0, buf0, gsem0)

        @pl.loop(0, n_chunks, step=2)
        def _(j):
            @pl.when(j > 0)
            def _():
                write_copy(j - 1, buf1, wsem1).wait()
            start_gather(j + 1, idx1, buf1, gsem1)
            gather_copy(idx0, buf0, gsem0).wait()
            write_copy(j, buf0, wsem0).start()

            @pl.when(j + 2 < n_chunks)
            def _():
                write_copy(j, buf0, wsem0).wait()
                start_gather(j + 2, idx0, buf0, gsem0)
            gather_copy(idx1, buf1, gsem1).wait()
            write_copy(j + 1, buf1, wsem1).start()

        write_copy(n_chunks - 2, buf0, wsem0).wait()
        write_copy(n_chunks - 1, buf1, wsem1).wait()

    return gather_rows(table, idx)


def _moe(h2, e_tl, w_tl, r_tl, cnt, layer, w_gu, b_gu, w_down, b_down):
    bsz, t, half = h2.shape
    n = bsz * t
    flat = lambda a: a.transpose(1, 0, 2).reshape(TOP_K, n)
    e_k, w_k, r_k = flat(e_tl), flat(w_tl), flat(r_tl)
    counts = cnt[:, 0].astype(I32)
    padded = (counts + MOE_BLOCK - 1) // MOE_BLOCK * MOE_BLOCK
    pad_end = jnp.cumsum(padded)
    pad_start = pad_end - padded
    n_blocks = (n * TOP_K + MOE_BLOCK - 1) // MOE_BLOCK + N_EXPERTS
    block_start = jnp.arange(n_blocks, dtype=I32) * MOE_BLOCK
    block_e = jnp.minimum(jnp.sum((pad_end[None, :] <= block_start[:, None]).astype(I32), axis=1), N_EXPERTS - 1)
    n_used = (pad_end[-1:] // MOE_BLOCK).astype(I32)
    start_k = jnp.sum(jnp.where(e_k[..., None] == jnp.arange(N_EXPERTS, dtype=I32), pad_start, 0), axis=-1)
    dest = (start_k + r_k).reshape(TOP_K * n)
    xb = _sc_dispatch(h2.reshape(n, half), dest, n_blocks * MOE_BLOCK)
    yb = _experts(xb, block_e, n_used, layer, w_gu, b_gu, w_down, b_down)
    return _sc_gather(yb, dest), w_k.T


def _rope_tables(n_ctx, n_lat):
    n_freq = RET_DK // 4
    inv_freq = ROPE_BASE ** (-jnp.arange(n_freq, dtype=F32) / n_freq)
    pos = jnp.arange(n_lat, dtype=I32)
    cos, sin = [], []
    for p in (pos // GRID_W, pos % GRID_W):
        ang = p.astype(F32)[:, None] * inv_freq
        cos += [jnp.cos(ang), jnp.cos(ang)]
        sin += [-jnp.sin(ang), jnp.sin(ang)]
    cos, sin = jnp.concatenate(cos, axis=1), jnp.concatenate(sin, axis=1)
    return (jnp.concatenate([jnp.ones((n_ctx, RET_DK), F32), cos], axis=0),
            jnp.concatenate([jnp.zeros((n_ctx, RET_DK), F32), sin], axis=0))


def kernel(x, c, ctx, c_ctx, ada_w, ada_b, norm1_g, norm2_g, ab_w_in, ab_w_out, gla_wa, gla_ba, gla_norm_g, s5_lam_re, s5_lam_im, s5_log_step, s5_b_re, s5_b_im, s5_c_re, s5_c_im, s5_d, s5_glu_w, s5_glu_b, ret_w_in, ret_w_out, ret_decay_logit, ret_norm_g, moe_w_router, moe_b_router, moe_w_gu, moe_b_gu, moe_w_down, moe_b_down, final_norm_g):
    bsz, n_lat, d = x.shape
    n_ctx = ctx.shape[1]
    depth = ada_w.shape[0]
    assert depth == 2 and d == D_MODEL and bsz == 8, "kernels are laid out for the stated problem shape"
    assert n_ctx % TOKEN_TILE == 0 and n_lat % TOKEN_TILE == 0 and n_lat % GRID_W == 0
    t = n_ctx + n_lat
    nct = n_ctx // TOKEN_TILE

    cvec = jnp.zeros((16, d), F32).at[:bsz].set(c).at[bsz].set(c_ctx)
    mod = _ada_mod(cvec, ada_w, ada_b).reshape(depth, 16, 6, d)
    mods = [jnp.stack([jnp.broadcast_to(mod[l, bsz], (bsz, 6, d)), mod[l, :bsz]], axis=1) for l in range(depth)]

    w_in = ab_w_in[0].astype(BF16)
    cuts = [0, AB_QK, 2 * AB_QK, 2 * AB_QK + AB_V, 2 * AB_QK + 2 * AB_V, 2 * AB_QK + 2 * AB_V + 2 * GLA_RANK,
            w_in.shape[1]]
    pieces = [w_in[:, a:b] for a, b in zip(cuts[:-1], cuts[1:])]
    wa_pad = jnp.zeros((2, 2 * GLA_RANK, AB_QK), F32)
    wa_pad = wa_pad.at[0, :GLA_RANK].set(gla_wa[0, 0]).at[1, GLA_RANK:].set(gla_wa[0, 1])
    outs = _inproj0(ctx, x, mods[0], norm1_g[0], pieces, wa_pad, gla_ba[0].reshape(2, 1, AB_QK), nct)
    v, g, u = outs[8:]
    o_f, o_b = _gla((outs[0:4], outs[4:8]), v, n_ctx)
    ops = _s5_operators(s5_lam_re[0], s5_lam_im[0], s5_log_step[0], s5_b_re[0], s5_b_im[0], s5_c_re[0], s5_c_im[0])
    ys = _s5(u, ops, n_ctx)
    consts = [jnp.tile(gla_norm_g[0], GLA_HEADS).reshape(1, AB_V), s5_d[0].reshape(1, S5_CH),
              s5_glu_w[0].astype(BF16), s5_glu_b[0].reshape(1, S5_CH), ab_w_out[0].astype(BF16)]
    x1, h2, e_tl, w_tl, r_tl, cnt = _mix_call(
        functools.partial(_mix0_body, nct), "mix_gla_s5", (ctx, x), mods[0], t // TOKEN_TILE,
        [(o_f, False), (o_b, False), (g, False), (ys, False), (u, False)], consts,
        norm2_g[0], moe_w_router[0], moe_b_router[0], nct, 0)
    yk, w_tok = _moe(h2, e_tl, w_tl, r_tl, cnt, 0, moe_w_gu, moe_b_gu, moe_w_down, moe_b_down)

    w_in = ret_w_in[0].astype(BF16)
    cuts = [0, RET_QK, 2 * RET_QK, 2 * RET_QK + RET_MIX, w_in.shape[1]]
    pieces = [w_in[:, a:b] for a, b in zip(cuts[:-1], cuts[1:])]
    cos_t, sin_t = _rope_tables(n_ctx, n_lat)
    x_all, q, k, v, g = _inproj1(x1, mods[0], yk, w_tok, mods[1], norm1_g[1], cos_t, sin_t, pieces, nct)
    o_f, o_b = _retention(q, k, v, ret_decay_logit[0], n_ctx)
    consts = [ret_norm_g[0].reshape(1, RET_MIX), ret_w_out[0].astype(BF16)]
    x1, h2, e_tl, w_tl, r_tl, cnt = _mix_call(
        _mix1_body, "mix_retention", (x_all,), mods[1], n_lat // TOKEN_TILE,
        [(o_f, False), (o_b, False), (g, True)], consts,
        norm2_g[1], moe_w_router[1], moe_b_router[1], nct, nct)
    yk, w_tok = _moe(h2, e_tl, w_tl, r_tl, cnt, 1, moe_w_gu, moe_b_gu, moe_w_down, moe_b_down)
    return _combine(x1, mods[1], yk, w_tok, nct, nct, final_norm_g)
```

```python
import functools
import math

import jax
import jax.numpy as jnp
from jax import lax
from jax.experimental import pallas as pl
from jax.experimental.pallas import tpu as pltpu
from jax.experimental.pallas import tpu_sc as plsc

F32, BF16, I32, U32 = jnp.float32, jnp.bfloat16, jnp.int32, jnp.uint32

D_MODEL = 1024
GRID_W = 64
EPS = 1e-6
GLA_HEADS, GLA_DK, GLA_DV, GLA_RANK, GLA_TAU, GLA_CHUNK = 4, 64, 128, 16, 16.0, 64
GLA_BATCH = 8
AB_QK, AB_V = GLA_HEADS * GLA_DK, GLA_HEADS * GLA_DV
S5_CH, S5_GROUP, S5_GROUPS, S5_P = 512, 16, 32, 64
S5_CHUNK = 16
S5_FOLD_BATCH = 4
S5_SCAN_GROUPS = 2
RET_HEADS, RET_DK, RET_DV = 4, 256, 512
RET_CHUNK = 256
RET_QK, RET_MIX = RET_HEADS * RET_DK, RET_HEADS * RET_DV
ROPE_BASE = 10000.0
N_EXPERTS, TOP_K, D_FF = 32, 4, 1024
SWIGLU_LIMIT, SWIGLU_ALPHA = 7.0, 1.702
MOE_BLOCK = 512
TOKEN_TILE = 256
ADA_TILE = 768
VMEM_LIMIT = 56 * 1024 * 1024
SC_CORES, SC_SUBCORES = 2, 16
SC_WORKERS = SC_CORES * SC_SUBCORES
SC_CHUNK = 64

def _params(*sem):
    return pltpu.CompilerParams(dimension_semantics=sem, vmem_limit_bytes=VMEM_LIMIT)


def _dot(a, b):
    return jnp.dot(a, b, preferred_element_type=F32)


def _dot_nt(a, b):
    return lax.dot_general(a, b, (((1,), (1,)), ((), ())), preferred_element_type=F32)


def _dot_tn(a, b):
    return lax.dot_general(a, b, (((0,), (0,)), ((), ())), preferred_element_type=F32)


def _split(a):
    hi = a.astype(BF16)
    return hi, (a - hi.astype(F32)).astype(BF16)


def _dot3(a, b, dot=_dot):
    ah, al = _split(a)
    bh, bl = _split(b)
    return dot(ah, bh) + (dot(ah, bl) + dot(al, bh))


def _pack_rows(x):
    h = x.shape[1] // 2
    lo = lax.bitcast_convert_type(x[:, 0:h].astype(BF16).astype(F32), U32)
    hi = lax.bitcast_convert_type(x[:, h:2 * h].astype(BF16).astype(F32), U32)
    return hi | (lo >> 16)


def _unpack_rows(p):
    lo = lax.bitcast_convert_type(p << 16, F32)
    hi = lax.bitcast_convert_type(p & jnp.uint32(0xFFFF0000), F32)
    return lo, hi


def _silu(x):
    return x * jax.nn.sigmoid(x)


def _norm_mod(x, g, shift, scale):
    r = lax.rsqrt(jnp.mean(x * x, axis=-1, keepdims=True) + EPS)
    return (x * r * g) * (1.0 + scale) + shift


def _const_spec(shape):
    nd = len(shape)
    return pl.BlockSpec(shape, lambda *_: (0,) * nd)


def _ada_body(c_ref, w_ref, b_ref, o_ref):
    o_ref[0] = _dot3(_silu(c_ref[...]), w_ref[0]) + b_ref[0]


def _ada_mod(cvec, ada_w, ada_b):
    depth, d, n6 = ada_w.shape
    rows = cvec.shape[0]
    return pl.pallas_call(
        _ada_body,
        grid=(depth, n6 // ADA_TILE),
        in_specs=[_const_spec((rows, d)),
                  pl.BlockSpec((1, d, ADA_TILE), lambda l, j: (l, 0, j)),
                  pl.BlockSpec((1, 1, ADA_TILE), lambda l, j: (l, 0, j))],
        out_specs=pl.BlockSpec((1, rows, ADA_TILE), lambda l, j: (l, 0, j)),
        out_shape=jax.ShapeDtypeStruct((depth, rows, n6), F32),
        compiler_params=_params("arbitrary", "arbitrary"),
        name="ada_mod",
    )(cvec, ada_w, ada_b.reshape(depth, 1, n6))


def _tile_specs(nct, d):
    x_spec = pl.BlockSpec((1, TOKEN_TILE, d), lambda b, i: (b, i, 0))
    mod_spec = pl.BlockSpec((1, 1, 6, d), lambda b, i: (b, (i >= nct).astype(I32), 0, 0))
    return x_spec, mod_spec


def _split_specs(nct, d):
    ctx_spec = pl.BlockSpec((1, TOKEN_TILE, d), lambda b, i: (b, jnp.minimum(i, nct - 1), 0))
    lat_spec = pl.BlockSpec((1, TOKEN_TILE, d), lambda b, i: (b, jnp.maximum(i - nct, 0), 0))
    return ctx_spec, lat_spec


def _stream_tile(nct, ctx_ref, lat_ref):
    return jnp.where(pl.program_id(1) < nct, ctx_ref[0], lat_ref[0])


def _inproj0_body(nct, ctx_ref, lat_ref, mod_ref, g_ref, wq, wk, wv, wg, wlow, wu, wa_ref, ba_ref, tri_ref, ones_ref,
                  qd_f, ki_f, ks_f, ed_f, qd_b, ki_b, ks_b, ed_b, ov, og, ou):
    m = mod_ref[0, 0]
    h = _norm_mod(_stream_tile(nct, ctx_ref, lat_ref), g_ref[...], m[0:1], m[1:2]).astype(BF16)
    ov[0] = _dot(h, wv[...]).astype(ov.dtype)
    og[0] = _dot(h, wg[...]).astype(og.dtype)
    u = _dot(h, wu[...])
    for qb in range(ou.shape[0]):
        ou[qb, 0] = u[:, qb * 128:(qb + 1) * 128]
    q = _dot(h, wq[...]) * (GLA_DK ** -0.5)
    k = _dot(h, wk[...])
    low = _dot(h, wlow[...])
    outs = ((qd_f, ki_f, ks_f, ed_f), (qd_b, ki_b, ks_b, ed_b))
    for d, (qd_ref, ki_ref, ks_ref, ed_ref) in enumerate(outs):
        z = _dot3(low, wa_ref[d]) + ba_ref[d]
        log_a = (jnp.minimum(z, 0.0) - jnp.log1p(jnp.exp(-jnp.abs(z)))) * (1.0 / GLA_TAU)
        la_hi, la_lo = _split(log_a)
        cum = _dot(tri_ref[d], la_hi) + _dot(tri_ref[d], la_lo)
        tot = _dot(ones_ref[...], la_hi) + _dot(ones_ref[...], la_lo)
        qd_ref[0] = (q * jnp.exp(cum)).astype(BF16)
        ki_ref[0] = (k * jnp.exp(-cum)).astype(BF16)
        ks_ref[0] = (k * jnp.exp(tot - cum)).astype(BF16)
        for ch in range(TOKEN_TILE // GLA_CHUNK):
            ed_ref[0, ch] = jnp.exp(tot[ch * GLA_CHUNK:ch * GLA_CHUNK + 1])


def _inproj0(ctx, x, mods, norm_g, weights, wa_pad, ba, nct):
    bsz, n_lat, d = x.shape
    t = ctx.shape[1] + n_lat
    tm = TOKEN_TILE
    _, mod_spec = _tile_specs(nct, d)
    ctx_spec, lat_spec = _split_specs(nct, d)
    pos = jnp.arange(tm)
    same_chunk = (pos[:, None] // GLA_CHUNK) == (pos[None, :] // GLA_CHUNK)
    tri = jnp.stack([same_chunk & (pos[None, :] <= pos[:, None]),
                     same_chunk & (pos[None, :] >= pos[:, None])]).astype(BF16)
    ones = same_chunk.astype(BF16)
    consts = list(weights) + [wa_pad, ba, tri, ones]
    tok = lambda w, dt: (pl.BlockSpec((1, tm, w), lambda b, i: (b, i, 0)), jax.ShapeDtypeStruct((bsz, t, w), dt))
    per_chunk = (pl.BlockSpec((1, tm // GLA_CHUNK, 1, AB_QK), lambda b, i: (b, i, 0, 0)),
                 jax.ShapeDtypeStruct((bsz, t // GLA_CHUNK, 1, AB_QK), F32))
    one_dir = [tok(AB_QK, BF16)] * 3 + [per_chunk]
    u_blocks = (pl.BlockSpec((S5_CH // 128, 1, tm, 128), lambda b, i: (0, b, i, 0)),
                jax.ShapeDtypeStruct((S5_CH // 128, bsz, t, 128), F32))
    outs = one_dir + one_dir + [tok(AB_V, BF16), tok(AB_V, BF16), u_blocks]
    return pl.pallas_call(
        functools.partial(_inproj0_body, nct),
        grid=(bsz, t // tm),
        in_specs=[ctx_spec, lat_spec, mod_spec, _const_spec((1, d))] + [_const_spec(a.shape) for a in consts],
        out_specs=[o[0] for o in outs],
        out_shape=[o[1] for o in outs],
        compiler_params=_params("arbitrary", "arbitrary"),
        name="inproj_gla_s5",
    )(ctx, x, mods, norm_g.reshape(1, d), *consts)


def _rope(acc, cos_ref, sin_ref, o_ref, scale):
    for grp in range(acc.shape[1] // 128):
        half = grp % 2
        xg = acc[:, grp * 128:(grp + 1) * 128]
        cs = cos_ref[:, half * 128:(half + 1) * 128]
        sn = sin_ref[:, half * 128:(half + 1) * 128]
        out = xg * cs + pltpu.roll(xg, 64, 1) * sn
        o_ref[0, :, grp * 128:(grp + 1) * 128] = (out * scale).astype(o_ref.dtype)


def _moe_residual(x1_ref, g2, yk_ref, w_ref):
    d = x1_ref.shape[2]
    half = d // 2
    y_lo, y_hi = None, None
    for k in range(TOP_K):
        lo, hi = _unpack_rows(yk_ref[k, 0])
        wk = w_ref[0, :, k:k + 1]
        y_lo = lo * wk if y_lo is None else y_lo + lo * wk
        y_hi = hi * wk if y_hi is None else y_hi + hi * wk
    return x1_ref[0, :, 0:half] + g2[:, 0:half] * y_lo, x1_ref[0, :, half:d] + g2[:, half:d] * y_hi


def _inproj1_body(x1_ref, mod0_ref, yk_ref, w_ref, mod_ref, g_ref, cos_ref, sin_ref, wq, wk, wv, wg, ox, oq, ok, ov, og):
    half = x1_ref.shape[2] // 2
    x2_lo, x2_hi = _moe_residual(x1_ref, mod0_ref[0, 0][5:6], yk_ref, w_ref)
    ox[0, :, 0:half] = x2_lo
    ox[0, :, half:2 * half] = x2_hi
    m = mod_ref[0, 0]
    h = _norm_mod(jnp.concatenate([x2_lo, x2_hi], axis=1), g_ref[...], m[0:1], m[1:2]).astype(BF16)
    _rope(_dot(h, wq[...]), cos_ref, sin_ref, oq, 1.0)
    _rope(_dot(h, wk[...]), cos_ref, sin_ref, ok, RET_DK ** -0.5)
    ov[0] = _dot(h, wv[...]).astype(ov.dtype)
    og[0] = _dot(h, wg[...]).astype(og.dtype)


def _inproj1(x1, mods0, yk, w_tok, mods, norm_g, cos_t, sin_t, weights, nct):
    bsz, t, d = x1.shape
    tm = TOKEN_TILE
    x_spec, mod_spec = _tile_specs(nct, d)
    tok = lambda w: pl.BlockSpec((1, tm, w), lambda b, i: (b, i, 0))
    tab_spec = pl.BlockSpec((tm, RET_DK), lambda b, i: (i, 0))
    return pl.pallas_call(
        _inproj1_body,
        grid=(bsz, t // tm),
        in_specs=[x_spec, mod_spec, pl.BlockSpec((TOP_K, 1, tm, d // 2), lambda b, i: (0, b, i, 0)), tok(TOP_K),
                  mod_spec, _const_spec((1, d)), tab_spec, tab_spec] + [_const_spec(w.shape) for w in weights],
        out_specs=[tok(d)] + [tok(w.shape[1]) for w in weights],
        out_shape=[jax.ShapeDtypeStruct((bsz, t, d), F32)]
        + [jax.ShapeDtypeStruct((bsz, t, w.shape[1]), BF16) for w in weights],
        compiler_params=_params("arbitrary", "arbitrary"),
        name="inproj_retention",
    )(x1, mods0, yk.reshape(TOP_K, bsz, t, d // 2), w_tok.reshape(bsz, t, TOP_K), mods, norm_g.reshape(1, d),
      cos_t, sin_t, *weights)


def _backward_chunk(n, n_ctx_chunks, n_chunks):
    return jnp.where(n < n_ctx_chunks, n_ctx_chunks - 1 - n, n_chunks - 1 - (n - n_ctx_chunks))


def _gla_body(qd_f, ki_f, ks_f, ed_f, v_f, qd_b, ki_b, ks_b, ed_b, v_b, hmask_ref, bdmask_ref, o_f, o_b, st_f, st_b):
    c = GLA_CHUNK

    @pl.when(pl.program_id(1) == 0)
    def _():
        st_f[...] = jnp.zeros_like(st_f)
        st_b[...] = jnp.zeros_like(st_b)

    r4 = lax.broadcasted_iota(I32, (GLA_HEADS * c, c), 0) & (c - 1)
    c4 = lax.broadcasted_iota(I32, (GLA_HEADS * c, c), 1)
    dirs = ((qd_f, ki_f, ks_f, ed_f, v_f, o_f, st_f), (qd_b, ki_b, ks_b, ed_b, v_b, o_b, st_b))
    chains = [(bb, d) + dirs[d] for bb in range(qd_f.shape[0]) for d in range(2)]
    scores, inter, grow = [], [], []
    for bb, d, qd_ref, ki_ref, ks_ref, ed_ref, v_ref, o_ref, st_ref in chains:
        q_dec = qd_ref[bb]
        q_heads = jnp.concatenate([q_dec] * GLA_HEADS, axis=0) * hmask_ref[...]
        seen4 = (c4 <= r4) if d == 0 else (c4 >= r4)
        scores.append(jnp.where(seen4, _dot_nt(q_heads, ki_ref[bb]), 0.0).astype(BF16))
        inter.append(_dot_nt(q_dec, st_ref[bb].astype(BF16)))
        grow.append(_dot_tn(v_ref[bb], ks_ref[bb]))
    for (bb, d, qd_ref, ki_ref, ks_ref, ed_ref, v_ref, o_ref, st_ref), sc, o_inter, dst in zip(chains, scores, inter, grow):
        v = v_ref[bb]
        o_intra = jnp.concatenate(
            [_dot(sc[h * c:(h + 1) * c], v[:, h * GLA_DV:(h + 1) * GLA_DV]) for h in range(GLA_HEADS)], axis=1)
        o_ref[bb] = o_intra + o_inter
        st_ref[bb] = st_ref[bb] * ed_ref[bb, 0] + bdmask_ref[...] * dst


def _gla(per_dir, v, n_ctx):
    bsz, t, _ = v.shape
    nc, ncc = t // GLA_CHUNK, n_ctx // GLA_CHUNK
    gb = GLA_BATCH
    fwd = lambda b, n: (b, n, 0)
    bwd = lambda b, n: (b, _backward_chunk(n, ncc, nc), 0)
    hmask = (jnp.arange(AB_QK)[:, None] // GLA_CHUNK == jnp.arange(AB_QK)[None, :] // GLA_DK).astype(BF16)
    bdmask = (jnp.arange(AB_V)[:, None] // GLA_DV == jnp.arange(AB_QK)[None, :] // GLA_DK).astype(F32)

    def specs(idx):
        idx4 = lambda b, n: idx(b, n) + (0,)
        return [pl.BlockSpec((gb, GLA_CHUNK, AB_QK), idx)] * 3 + [pl.BlockSpec((gb, 1, 1, AB_QK), idx4),
                                                                  pl.BlockSpec((gb, GLA_CHUNK, AB_V), idx)]

    return pl.pallas_call(
        _gla_body,
        grid=(bsz // gb, nc),
        in_specs=specs(fwd) + specs(bwd) + [_const_spec(hmask.shape), _const_spec(bdmask.shape)],
        out_specs=[pl.BlockSpec((gb, GLA_CHUNK, AB_V), fwd), pl.BlockSpec((gb, GLA_CHUNK, AB_V), bwd)],
        out_shape=[jax.ShapeDtypeStruct((bsz, t, AB_V), F32)] * 2,
        scratch_shapes=[pltpu.VMEM((gb, AB_V, AB_QK), F32)] * 2,
        compiler_params=_params("arbitrary", "arbitrary"),
        name="gla_scan",
    )(*per_dir[0], v, *per_dir[1], v, hmask, bdmask)


def _cmul(x, y):
    return x[0] * y[0] - x[1] * y[1], x[0] * y[1] + x[1] * y[0]


def _s5_operators(lam_re, lam_im, log_step, b_re, b_im, c_re, c_im):
    ln = S5_CHUNK
    step = jnp.exp(log_step.astype(F32))[..., None]
    lam_re, lam_im = lam_re.astype(F32), lam_im.astype(F32)
    mag = jnp.exp(lam_re * step)
    a = (mag * jnp.cos(lam_im * step), mag * jnp.sin(lam_im * step))
    den = lam_re * lam_re + lam_im * lam_im
    f_re = ((a[0] - 1.0) * lam_re + a[1] * lam_im) / den
    f_im = (a[1] * lam_re - (a[0] - 1.0) * lam_im) / den
    bt_re, bt_im = b_re.transpose(0, 2, 1), b_im.transpose(0, 2, 1)
    bb = _cmul((f_re[:, :, None, :], f_im[:, :, None, :]), (bt_re, bt_im))
    bbt = jnp.concatenate([bb[0], -bb[1]], axis=-1)
    pw = (a[0][:, :, None, :], a[1][:, :, None, :])
    while pw[0].shape[2] < ln:
        top = (pw[0][:, :, -1:, :], pw[1][:, :, -1:, :])
        nxt = _cmul(top, pw)
        pw = (jnp.concatenate([pw[0], nxt[0]], axis=2), jnp.concatenate([pw[1], nxt[1]], axis=2))
    pw = (jnp.concatenate([jnp.ones_like(pw[0][:, :, :1]), pw[0]], axis=2),
          jnp.concatenate([jnp.zeros_like(pw[1][:, :, :1]), pw[1]], axis=2))
    ca = _cmul((c_re[:, :, None], c_im[:, :, None]), (pw[0][:, :, :, None, :], pw[1][:, :, :, None, :]))
    by_dir = lambda arr, lo, flip_d: jnp.stack([jnp.flip(arr[d, :, lo:lo + ln], axis=1) if d == flip_d
                                                else arr[d, :, lo:lo + ln] for d in range(2)])
    rows = lambda arr: arr.reshape(2, S5_GROUPS, ln * S5_GROUP, 2 * S5_P)
    cab = rows(by_dir(jnp.concatenate([ca[0], ca[1]], axis=-1), 0, 1))
    cab2 = rows(by_dir(jnp.concatenate([ca[0], -ca[1]], axis=-1), 1, 1)).astype(BF16)
    pwx = by_dir(jnp.concatenate([pw[0], pw[1]], axis=-1), 0, 0)
    lr, li = pw[0][:, :, ln], pw[1][:, :, ln]
    ac_rows = [jnp.concatenate([lr, lr], -1), jnp.concatenate([-li, li], -1), jnp.concatenate([li, -li], -1)]
    ac = jnp.stack(ac_rows + [jnp.zeros_like(ac_rows[0])] * 5, axis=2)
    return cab, cab2, bbt, pwx, ac


def _s5_group_operators(gg, cab_ref, bbt_ref, pwx_ref, tz, wx):
    ln, ch, p = S5_CHUNK, S5_GROUP, S5_P
    lane = lax.broadcasted_iota(I32, (ch, ln * ch), 1)
    for d in range(2):
        kern = _dot3(bbt_ref[d, gg], cab_ref[d, gg], dot=_dot_nt)
        bt = bbt_ref[d, gg]
        b_re, b_im = bt[:, 0:p], -bt[:, p:2 * p]
        for j in range(ln):
            if d == 0:
                blk = jnp.where(lane >= j * ch, kern if j == 0 else pltpu.roll(kern, j * ch, 1), 0.0)
            else:
                blk = jnp.where(lane < (j + 1) * ch, kern if j == ln - 1 else pltpu.roll(kern, (j + 1) * ch, 1), 0.0)
            tz[gg, d, j * ch:(j + 1) * ch, :] = blk.astype(BF16)
            pr, pi = pwx_ref[d, gg, j:j + 1, 0:p], pwx_ref[d, gg, j:j + 1, p:2 * p]
            x_re, x_im = pr * b_re - pi * b_im, pr * b_im + pi * b_re
            wx[gg, d, j * ch:(j + 1) * ch, :] = jnp.concatenate([x_re, x_im, x_im, x_re], axis=1).astype(BF16)


def _s5_placement(pall):
    rows, cols = pall.shape[1], pall.shape[2]
    row = lax.broadcasted_iota(I32, (rows, cols), 0)
    col = lax.broadcasted_iota(I32, (rows, cols), 1)
    same_token = (row >> 7) == (col >> 4)
    for g8 in range(pall.shape[0]):
        pall[g8] = jnp.where(same_token & ((row & 127) == g8 * S5_GROUP + (col & (S5_GROUP - 1))), 1.0, 0.0).astype(BF16)


def _first_step():
    return (pl.program_id(0) == 0) & (pl.program_id(1) == 0)


def _s5_fold_body(ncs, u_ref, o_ref, pall, ucat):
    @pl.when(_first_step())
    def _():
        _s5_placement(pall)

    for b in range(u_ref.shape[1]):
        for j in range(S5_CHUNK):
            ucat[b * ncs:(b + 1) * ncs, j * 128:(j + 1) * 128] = u_ref[0, b, pl.ds(j, ncs, stride=S5_CHUNK), :].astype(BF16)
    for g8 in range(pall.shape[0]):
        o_ref[g8] = _dot(ucat[...], pall[g8]).astype(BF16)


def _s5_unfold_body(ncs, y_ref, o_ref, pall):
    @pl.when(_first_step())
    def _():
        _s5_placement(pall)

    def token_pair(i2, carry):
        r0 = pl.multiple_of(i2 * 256, 256)
        acc = _dot_nt(y_ref[0], pall[0, pl.ds(r0, 256), :])
        for g8 in range(1, pall.shape[0]):
            acc = acc + _dot_nt(y_ref[g8], pall[g8, pl.ds(r0, 256), :])
        for b in range(o_ref.shape[1]):
            for par in range(2):
                o_ref[0, b, pl.ds(2 * i2 + par, ncs, stride=S5_CHUNK), :] = (
                    acc[b * ncs:(b + 1) * ncs, par * 128:(par + 1) * 128])
        return carry

    lax.fori_loop(0, S5_CHUNK // 2, token_pair, 0)


def _s5_body(ncs_ctx, ncs, rows, u_ref, cab_ref, cab2_ref, bbt_ref, pwx_ref, ac_ref, y_ref, tz, wx, *vecs):
    half = 2 * S5_P
    n_groups = u_ref.shape[0]
    groups = [vecs[6 * gg:6 * gg + 6] for gg in range(n_groups)]
    for gg, (xx_f, xs_f, xx_b, xs_b, _, _) in enumerate(groups):
        _s5_group_operators(gg, cab_ref, bbt_ref, pwx_ref, tz, wx)
        for d, (xx, xs) in enumerate(((xx_f, xs_f), (xx_b, xs_b))):
            r = _dot(u_ref[gg], wx[gg, d])
            xx[...] = r[:, :half]
            xs[...] = r[:, half:]

    def advance(ac, s, s_sw, x, x_sw):
        return ac[0:1] * s + ac[1:2] * s_sw + x, ac[0:1] * s_sw + ac[2:3] * s + x_sw

    def step(n, carry):
        at_f = pl.ds(n, rows, stride=ncs)
        at_b = pl.ds(_backward_chunk(n, ncs_ctx, ncs), rows, stride=ncs)
        out = []
        for gg, (xx_f, xs_f, xx_b, xs_b, sin_f, sin_b) in enumerate(groups):
            s_f, sw_f, s_b, sw_b = carry[4 * gg:4 * gg + 4]
            sin_f[at_f, :] = s_f
            sin_b[at_b, :] = s_b
            out += advance(ac_ref[0, gg], s_f, sw_f, xx_f[at_f, :], xs_f[at_f, :])
            out += advance(ac_ref[1, gg], s_b, sw_b, xx_b[at_b, :], xs_b[at_b, :])
        return tuple(out)

    zero = jnp.zeros((rows, half), F32)
    lax.fori_loop(0, ncs, step, (zero,) * (4 * n_groups))
    for gg, (_, _, _, _, sin_f, sin_b) in enumerate(groups):
        u = u_ref[gg]
        y_ref[gg] = (_dot(u, tz[gg, 0]) + _dot(u, tz[gg, 1]) + _dot_nt(sin_f[...].astype(BF16), cab2_ref[0, gg])
                     + _dot_nt(sin_b[...].astype(BF16), cab2_ref[1, gg])).astype(BF16)


def _s5(u4, ops, n_ctx):
    nq, bsz, t, _ = u4.shape
    ln, lanes = S5_CHUNK, S5_CHUNK * S5_GROUP
    gq = S5_GROUPS // nq
    ncs, ncs_ctx = t // ln, n_ctx // ln
    m = ncs * bsz
    hb = S5_FOLD_BATCH
    tok_spec = pl.BlockSpec((1, hb, t, 128), lambda q, h: (q, h, 0, 0))
    grp_spec = pl.BlockSpec((gq, hb * ncs, lanes), lambda q, h: (q, h, 0))
    pall = pltpu.VMEM((gq, ln * 128, lanes), BF16)
    ug = pl.pallas_call(
        functools.partial(_s5_fold_body, ncs),
        grid=(nq, bsz // hb),
        in_specs=[tok_spec],
        out_specs=grp_spec,
        out_shape=jax.ShapeDtypeStruct((S5_GROUPS, m, lanes), BF16),
        scratch_shapes=[pall, pltpu.VMEM((hb * ncs, ln * 128), BF16)],
        compiler_params=_params("arbitrary", "arbitrary"),
        name="s5_fold",
    )(u4)
    sg = S5_SCAN_GROUPS
    dir_spec = lambda arr: pl.BlockSpec((2, sg) + arr.shape[2:], lambda g: (0, g, 0, 0))
    yg = pl.pallas_call(
        functools.partial(_s5_body, ncs_ctx, ncs, bsz),
        grid=(S5_GROUPS // sg,),
        in_specs=[pl.BlockSpec((sg, m, lanes), lambda g: (g, 0, 0))] + [dir_spec(arr) for arr in ops],
        out_specs=pl.BlockSpec((sg, m, lanes), lambda g: (g, 0, 0)),
        out_shape=jax.ShapeDtypeStruct((S5_GROUPS, m, lanes), BF16),
        scratch_shapes=[pltpu.VMEM((sg, 2, lanes, lanes), BF16)] * 2 + [pltpu.VMEM((m, 2 * S5_P), F32)] * (6 * sg),
        compiler_params=_params("arbitrary"),
        name="s5_scan",
    )(ug, *ops)
    return pl.pallas_call(
        functools.partial(_s5_unfold_body, ncs),
        grid=(nq, bsz // hb),
        in_specs=[grp_spec],
        out_specs=tok_spec,
        out_shape=jax.ShapeDtypeStruct(u4.shape, F32),
        scratch_shapes=[pall],
        compiler_params=_params("arbitrary", "arbitrary"),
        name="s5_unfold",
    )(yg)


def _ret_body(q_f, k_f, v_f, q_b, k_b, v_b, dmat_ref, rsc_ref, csc_ref, gam_ref, o_f, o_b, st_f, st_b):
    @pl.when(pl.program_id(1) == 0)
    def _():
        st_f[...] = jnp.zeros_like(st_f)
        st_b[...] = jnp.zeros_like(st_b)

    dirs = ((q_f, k_f, v_f, o_f, st_f), (q_b, k_b, v_b, o_b, st_b))
    for d, (q_ref, k_ref, v_ref, o_ref, st_ref) in enumerate(dirs):
        for h in range(RET_HEADS):
            qh = q_ref[0, :, h * RET_DK:(h + 1) * RET_DK]
            kh = k_ref[0, :, h * RET_DK:(h + 1) * RET_DK]
            vh = v_ref[0, :, h * RET_DV:(h + 1) * RET_DV]
            st = st_ref[h]
            scores = (_dot_nt(qh, kh) * dmat_ref[d, h]).astype(BF16)
            o = _dot(scores, vh) + rsc_ref[d, h] * _dot(qh, st.astype(BF16))
            o_ref[0, :, h * RET_DV:(h + 1) * RET_DV] = o.astype(o_ref.dtype)
            k_state = (kh.astype(F32) * csc_ref[d, h]).astype(BF16)
            st_ref[h] = st * gam_ref[d, h] + _dot_tn(k_state, vh)


def _retention(q, k, v, decay_logit, n_ctx):
    bsz, t, _ = q.shape
    c = RET_CHUNK
    nc, ncc = t // c, n_ctx // c
    nl = nc - ncc
    log_gamma = jax.nn.log_sigmoid(decay_logit.astype(F32))[:, :, None, None]
    i = jnp.arange(c, dtype=F32)
    lag = i[:, None] - i[None, :]
    lag = jnp.stack([lag, -lag])[:, None]
    dmat = jnp.where(lag >= 0, jnp.exp(log_gamma * jnp.maximum(lag, 0.0)), 0.0)
    done = jnp.stack([i + 1.0, c - i])[:, None, :, None]
    rsc = jnp.exp(log_gamma * done)
    csc = jnp.exp(log_gamma * (c - done))
    gam = jnp.exp(log_gamma[:, :, 0, 0] * c)
    fwd = lambda b, n: (b, n, 0)
    bwd = lambda b, n: (b, _backward_chunk(n, ncc, nc), 0)
    o_fwd = lambda b, n: (b, jnp.maximum(n - ncc, 0), 0)
    o_bwd = lambda b, n: (b, nl - 1 - jnp.maximum(n - ncc, 0), 0)

    def specs(idx):
        return [pl.BlockSpec((1, c, RET_QK), idx), pl.BlockSpec((1, c, RET_QK), idx), pl.BlockSpec((1, c, RET_MIX), idx)]

    return pl.pallas_call(
        _ret_body,
        grid=(bsz, nc),
        in_specs=specs(fwd) + specs(bwd) + [_const_spec(dmat.shape), _const_spec(rsc.shape), _const_spec(csc.shape),
                                            pl.BlockSpec(memory_space=pltpu.SMEM)],
        out_specs=[pl.BlockSpec((1, c, RET_MIX), o_fwd), pl.BlockSpec((1, c, RET_MIX), o_bwd)],
        out_shape=[jax.ShapeDtypeStruct((bsz, nl * c, RET_MIX), BF16)] * 2,
        scratch_shapes=[pltpu.VMEM((RET_HEADS, RET_DK, RET_DV), F32)] * 2,
        compiler_params=_params("arbitrary", "arbitrary"),
        name="retention_scan",
    )(q, k, v, q, k, v, dmat, rsc, csc, gam)


def _zero_counts_at_start(cnt_ref):
    @pl.when(_first_step())
    def _():
        cnt_ref[...] = jnp.zeros_like(cnt_ref)


def _route(x, mixed, mod, n2g_ref, wr_ref, br_ref, x1_ref, h2_ref, e_ref, w_ref, r_ref, cnt_ref):
    tm = x.shape[0]
    x1 = x + mod[2:3] * mixed
    x1_ref[0] = x1
    h2 = _norm_mod(x1, n2g_ref[...], mod[3:4], mod[4:5])
    h2_ref[0] = _pack_rows(h2)
    logits = _dot3(wr_ref[...], h2, dot=_dot_nt) + br_ref[...]
    ie = lax.broadcasted_iota(I32, logits.shape, 0)
    tops, picks = [], []
    for _ in range(TOP_K):
        mx = jnp.max(logits, axis=0, keepdims=True)
        pick = jnp.min(jnp.where(logits == mx, ie, N_EXPERTS), axis=0, keepdims=True)
        tops.append(mx)
        picks.append(pick)
        logits = jnp.where(ie == pick, -jnp.inf, logits)
    ex = [jnp.exp(tk - tops[0]) for tk in tops]
    den = ex[0] + ex[1] + ex[2] + ex[3]
    for kk in range(TOP_K):
        w_ref[0, kk:kk + 1, :] = ex[kk] / den
        e_ref[0, kk:kk + 1, :] = picks[kk]

    earlier = (lax.broadcasted_iota(I32, (tm, tm), 0) < lax.broadcasted_iota(I32, (tm, tm), 1))
    earlier = jnp.where(earlier, 1.0, 0.0).astype(BF16)
    run = cnt_ref[:, 0:1]
    for kk, pick in enumerate(picks):
        onehot = jnp.where(ie == pick, 1.0, 0.0)
        before = _dot(onehot.astype(BF16), earlier) + run
        r_ref[0, kk:kk + 1, :] = jnp.sum(onehot * before, axis=0, keepdims=True).astype(I32)
        run = run + jnp.sum(onehot, axis=1, keepdims=True)
    cnt_ref[...] = jnp.broadcast_to(run, cnt_ref.shape)


def _mix0_body(nct, ctx_ref, lat_ref, mod_ref, of_ref, ob_ref, g_ref, ys_ref, u_ref, gng_ref, dsk_ref, gluw_ref,
               glub_ref, wo_ref, n2g_ref, wr_ref, br_ref, x1_ref, h2_ref, e_ref, w_ref, r_ref, cnt_ref):
    _zero_counts_at_start(cnt_ref)
    o = of_ref[0] + ob_ref[0]
    heads = []
    for h in range(GLA_HEADS):
        oh = o[:, h * GLA_DV:(h + 1) * GLA_DV]
        heads.append(oh * lax.rsqrt(jnp.mean(oh * oh, axis=-1, keepdims=True) + EPS))
    gla = jnp.concatenate(heads, axis=1) * gng_ref[...] * _silu(g_ref[0].astype(F32))
    lane_blocks = lambda ref: jnp.concatenate([ref[qb, 0] for qb in range(ref.shape[0])], axis=1)
    y = jax.nn.gelu(lane_blocks(ys_ref) + dsk_ref[...] * lane_blocks(u_ref))
    y = y * jax.nn.sigmoid(_dot(y.astype(BF16), gluw_ref[...]) + glub_ref[...])
    mixed = _dot(gla.astype(BF16), wo_ref[0:AB_V]) + _dot(y.astype(BF16), wo_ref[AB_V:AB_V + S5_CH])
    _route(_stream_tile(nct, ctx_ref, lat_ref), mixed, mod_ref[0, 0], n2g_ref, wr_ref, br_ref,
           x1_ref, h2_ref, e_ref, w_ref, r_ref, cnt_ref)


def _mix1_body(x_ref, mod_ref, of_ref, ob_ref, g_ref, ng_ref, wo_ref, n2g_ref, wr_ref, br_ref,
               x1_ref, h2_ref, e_ref, w_ref, r_ref, cnt_ref):
    _zero_counts_at_start(cnt_ref)
    mixed = None
    for h in range(RET_HEADS):
        sl = slice(h * RET_DV, (h + 1) * RET_DV)
        oh = of_ref[0, :, sl].astype(F32) + ob_ref[0, :, sl].astype(F32)
        mu = jnp.mean(oh, axis=-1, keepdims=True)
        cen = oh - mu
        var = jnp.mean(cen * cen, axis=-1, keepdims=True)
        gated = cen * lax.rsqrt(var + EPS) * ng_ref[:, sl] * _silu(g_ref[0, :, sl].astype(F32))
        part = _dot(gated.astype(BF16), wo_ref[sl])
        mixed = part if mixed is None else mixed + part
    _route(x_ref[0], mixed, mod_ref[0, 0], n2g_ref, wr_ref, br_ref, x1_ref, h2_ref, e_ref, w_ref, r_ref, cnt_ref)


def _mix_call(body, name, stream, mods, tiles, acts, consts, norm2_g, w_router, b_router, n_tok, seg_tile0):
    bsz, _, d = stream[-1].shape
    tm = TOKEN_TILE
    off = lambda b, i: (b, i + seg_tile0, 0)
    loc = lambda b, i: (b, i, 0)
    ntl = bsz * tiles
    flat = lambda b, i: (b * tiles + i, 0, 0)
    in_specs = list(_split_specs(n_tok, d)) if len(stream) == 2 else [pl.BlockSpec((1, tm, d), off)]
    in_specs.append(pl.BlockSpec((1, 1, 6, d), lambda b, i: (b, ((i + seg_tile0) >= n_tok).astype(I32), 0, 0)))
    args = list(stream) + [mods]
    for arr, offset in acts:
        if arr.ndim == 4:
            in_specs.append(pl.BlockSpec((arr.shape[0], 1, tm, arr.shape[3]), lambda b, i: (0, b, i, 0)))
        else:
            in_specs.append(pl.BlockSpec((1, tm, arr.shape[2]), off if offset else loc))
        args.append(arr)
    tail = list(consts) + [norm2_g.reshape(1, d), w_router.T, b_router.reshape(N_EXPERTS, 1)]
    in_specs += [_const_spec(a.shape) for a in tail]
    args += tail
    tok_out = pl.BlockSpec((1, TOP_K, tm), flat)
    return pl.pallas_call(
        body,
        grid=(bsz, tiles),
        in_specs=in_specs,
        out_specs=[pl.BlockSpec((1, tm, d), loc), pl.BlockSpec((1, tm, d // 2), loc), tok_out, tok_out, tok_out,
                   _const_spec((N_EXPERTS, 128))],
        out_shape=[jax.ShapeDtypeStruct((bsz, tiles * tm, d), F32), jax.ShapeDtypeStruct((bsz, tiles * tm, d // 2), U32),
                   jax.ShapeDtypeStruct((ntl, TOP_K, tm), I32), jax.ShapeDtypeStruct((ntl, TOP_K, tm), F32),
                   jax.ShapeDtypeStruct((ntl, TOP_K, tm), I32), jax.ShapeDtypeStruct((N_EXPERTS, 128), F32)],
        compiler_params=_params("arbitrary", "arbitrary"),
        name=name,
    )(*args)


def _cast_rows(src_ref, dst_ref, rows):
    def chunk(j, carry):
        r = pl.multiple_of(j * rows, rows)
        dst_ref[pl.ds(r, rows), :] = src_ref[0, 0, pl.ds(r, rows), :].astype(BF16)
        return carry

    lax.fori_loop(0, dst_ref.shape[0] // rows, chunk, 0)


def _expert_body(be_ref, nu_ref, x_ref, wgu_ref, bgu_ref, wd_ref, bd_ref, o_ref, wgu_bf, wd_bf):
    i = pl.program_id(0)
    live = i < nu_ref[0]
    new_expert = (i == 0) | (be_ref[i] != be_ref[jnp.maximum(i - 1, 0)])

    @pl.when(live & new_expert)
    def _():
        _cast_rows(wgu_ref, wgu_bf, 128)
        _cast_rows(wd_ref, wd_bf, 128)

    @pl.when(live)
    def _():
        x_lo, x_hi = _unpack_rows(x_ref[...])
        half = x_lo.shape[1]
        gu = (_dot(x_lo.astype(BF16), wgu_bf[0:half]) + _dot(x_hi.astype(BF16), wgu_bf[half:2 * half])
              + bgu_ref[0, 0])
        gate = jnp.minimum(gu[:, :D_FF], SWIGLU_LIMIT)
        lin = jnp.clip(gu[:, D_FF:], -SWIGLU_LIMIT, SWIGLU_LIMIT)
        act = gate * jax.nn.sigmoid(SWIGLU_ALPHA * gate) * (lin + 1.0)
        y = _dot(act.astype(BF16), wd_bf[...]) + bd_ref[0, 0]
        o_ref[...] = _pack_rows(y)

    @pl.when(i >= nu_ref[0])
    def _():
        o_ref[...] = jnp.zeros_like(o_ref)


def _experts(xb, block_e, n_used, layer, w_gu, b_gu, w_down, b_down):
    n_slots, half = xb.shape
    d = 2 * half
    n_blocks = n_slots // MOE_BLOCK
    depth = w_gu.shape[0]
    by_expert = lambda i, be, nu: (layer, be[i], 0, 0)
    return pl.pallas_call(
        _expert_body,
        grid_spec=pltpu.PrefetchScalarGridSpec(
            num_scalar_prefetch=2,
            grid=(n_blocks,),
            in_specs=[pl.BlockSpec((MOE_BLOCK, half), lambda i, be, nu: (i, 0)),
                      pl.BlockSpec((1, 1, d, 2 * D_FF), by_expert), pl.BlockSpec((1, 1, 1, 2 * D_FF), by_expert),
                      pl.BlockSpec((1, 1, D_FF, d), by_expert), pl.BlockSpec((1, 1, 1, d), by_expert)],
            out_specs=pl.BlockSpec((MOE_BLOCK, half), lambda i, be, nu: (i, 0)),
            scratch_shapes=[pltpu.VMEM((d, 2 * D_FF), BF16), pltpu.VMEM((D_FF, d), BF16)]),
        out_shape=jax.ShapeDtypeStruct((n_slots, half), U32),
        compiler_params=_params("arbitrary"),
        name="moe_experts",
    )(block_e, n_used, xb, w_gu, b_gu.reshape(depth, N_EXPERTS, 1, 2 * D_FF), w_down,
      b_down.reshape(depth, N_EXPERTS, 1, d))


def _combine_body(x1_ref, mod_ref, yk_ref, w_ref, fg_ref, o_ref):
    d = x1_ref.shape[2]
    half = d // 2
    x2_lo, x2_hi = _moe_residual(x1_ref, mod_ref[0, 0][5:6], yk_ref, w_ref)
    ms = (jnp.sum(x2_lo * x2_lo, axis=-1, keepdims=True) + jnp.sum(x2_hi * x2_hi, axis=-1, keepdims=True)) / d
    r = lax.rsqrt(ms + EPS)
    o_ref[0, :, 0:half] = x2_lo * r * fg_ref[:, 0:half]
    o_ref[0, :, half:d] = x2_hi * r * fg_ref[:, half:d]


def _combine(x1, mods, yk, w_tok, seg_tile0, n_tok, final_g):
    bsz, t, d = x1.shape
    tm = TOKEN_TILE
    loc = lambda b, i: (b, i, 0)
    return pl.pallas_call(
        _combine_body,
        grid=(bsz, t // tm),
        in_specs=[pl.BlockSpec((1, tm, d), loc),
                  pl.BlockSpec((1, 1, 6, d), lambda b, i: (b, ((i + seg_tile0) >= n_tok).astype(I32), 0, 0)),
                  pl.BlockSpec((TOP_K, 1, tm, d // 2), lambda b, i: (0, b, i, 0)),
                  pl.BlockSpec((1, tm, TOP_K), loc), _const_spec((1, d))],
        out_specs=pl.BlockSpec((1, tm, d), loc),
        out_shape=jax.ShapeDtypeStruct((bsz, t, d), F32),
        compiler_params=_params("arbitrary", "arbitrary"),
        name="moe_combine",
    )(x1, mods, yk.reshape(TOP_K, bsz, t, d // 2), w_tok.reshape(bsz, t, TOP_K), final_g.reshape(1, d))


def _sc_mesh():
    return plsc.VectorSubcoreMesh(core_axis_name="core", subcore_axis_name="subcore",
                                  num_cores=SC_CORES, num_subcores=SC_SUBCORES)


def _sc_worker_base(per_worker):
    return (lax.axis_index("subcore") * SC_CORES + lax.axis_index("core")) * per_worker


def _sc_dispatch(rows, dest, n_slots):
    n, w = rows.shape
    per_worker = n // SC_WORKERS
    assert per_worker * SC_WORKERS == n and per_worker % SC_CHUNK == 0

    @functools.partial(
        pl.kernel, mesh=_sc_mesh(), out_type=jax.ShapeDtypeStruct((n_slots, w), rows.dtype),
        scratch_types=[pltpu.VMEM((SC_CHUNK,), I32)] * TOP_K + [pltpu.VMEM((SC_CHUNK, w), rows.dtype),
                                                                pltpu.SemaphoreType.DMA],
        name="moe_dispatch")
    def scatter_rows(rows_hbm, dest_hbm, out_hbm, *scratch):
        idx_refs, buf, sem = scratch[:TOP_K], scratch[TOP_K], scratch[TOP_K + 1]
        base0 = _sc_worker_base(per_worker)

        @pl.loop(0, per_worker // SC_CHUNK)
        def _(j):
            base = base0 + j * SC_CHUNK
            pltpu.sync_copy(rows_hbm.at[pl.ds(base, SC_CHUNK)], buf)
            for k, idx in enumerate(idx_refs):
                pltpu.sync_copy(dest_hbm.at[pl.ds(k * n + base, SC_CHUNK)], idx)
            copies = [pltpu.make_async_copy(buf, out_hbm.at[idx], sem) for idx in idx_refs]
            for cp in copies:
                cp.start()
            for cp in copies:
                cp.wait()

    return scatter_rows(rows, dest)


def _sc_gather(table, idx):
    n = idx.shape[0]
    w = table.shape[1]
    per_worker = n // SC_WORKERS
    n_chunks = per_worker // SC_CHUNK
    assert per_worker * SC_WORKERS == n and n_chunks * SC_CHUNK == per_worker and n_chunks % 2 == 0

    @functools.partial(
        pl.kernel, mesh=_sc_mesh(), out_type=jax.ShapeDtypeStruct((n, w), table.dtype),
        scratch_types=([pltpu.VMEM((SC_CHUNK,), I32)] * 2 + [pltpu.VMEM((SC_CHUNK, w), table.dtype)] * 2
                       + [pltpu.SemaphoreType.DMA] * 4),
        name="moe_gather")
    def gather_rows(table_hbm, idx_hbm, out_hbm, idx0, idx1, buf0, buf1, gsem0, gsem1, wsem0, wsem1):
        base0 = _sc_worker_base(per_worker)

        def gather_copy(idx_v, buf, sem):
            return pltpu.make_async_copy(table_hbm.at[idx_v], buf, sem)

        def write_copy(j, buf, sem):
            return pltpu.make_async_copy(buf, out_hbm.at[pl.ds(base0 + j * SC_CHUNK, SC_CHUNK)], sem)

        def start_gather(j, idx_v, buf, sem):
            pltpu.sync_copy(idx_hbm.at[pl.ds(base0 + j * SC_CHUNK, SC_CHUNK)], idx_v)
            gather_copy(idx_v, buf, sem).start()

        start_gather(0, idx0, buf0, gsem0)

        @pl.loop(0, n_chunks, step=2)
        def _(j):
            @pl.when(j > 0)
            def _():
                write_copy(j - 1, buf1, wsem1).wait()
            start_gather(j + 1, idx1, buf1, gsem1)
            gather_copy(idx0, buf0, gsem0).wait()
            write_copy(j, buf0, wsem0).start()

            @pl.when(j + 2 < n_chunks)
            def _():
                write_copy(j, buf0, wsem0).wait()
                start_gather(j + 2, idx0, buf0, gsem0)
            gather_copy(idx1, buf1, gsem1).wait()
            write_copy(j + 1, buf1, wsem1).start()

        write_copy(n_chunks - 2, buf0, wsem0).wait()
        write_copy(n_chunks - 1, buf1, wsem1).wait()

    return gather_rows(table, idx)


def _moe(h2, e_tl, w_tl, r_tl, cnt, layer, w_gu, b_gu, w_down, b_down):
    bsz, t, half = h2.shape
    n = bsz * t
    flat = lambda a: a.transpose(1, 0, 2).reshape(TOP_K, n)
    e_k, w_k, r_k = flat(e_tl), flat(w_tl), flat(r_tl)
    counts = cnt[:, 0].astype(I32)
    padded = (counts + MOE_BLOCK - 1) // MOE_BLOCK * MOE_BLOCK
    pad_end = jnp.cumsum(padded)
    pad_start = pad_end - padded
    n_blocks = (n * TOP_K + MOE_BLOCK - 1) // MOE_BLOCK + N_EXPERTS
    block_start = jnp.arange(n_blocks, dtype=I32) * MOE_BLOCK
    block_e = jnp.minimum(jnp.sum((pad_end[None, :] <= block_start[:, None]).astype(I32), axis=1), N_EXPERTS - 1)
    n_used = (pad_end[-1:] // MOE_BLOCK).astype(I32)
    start_k = jnp.sum(jnp.where(e_k[..., None] == jnp.arange(N_EXPERTS, dtype=I32), pad_start, 0), axis=-1)
    dest = (start_k + r_k).reshape(TOP_K * n)
    xb = _sc_dispatch(h2.reshape(n, half), dest, n_blocks * MOE_BLOCK)
    yb = _experts(xb, block_e, n_used, layer, w_gu, b_gu, w_down, b_down)
    return _sc_gather(yb, dest), w_k.T


def _rope_tables(n_ctx, n_lat):
    n_freq = RET_DK // 4
    inv_freq = ROPE_BASE ** (-jnp.arange(n_freq, dtype=F32) / n_freq)
    pos = jnp.arange(n_lat, dtype=I32)
    cos, sin = [], []
    for p in (pos // GRID_W, pos % GRID_W):
        ang = p.astype(F32)[:, None] * inv_freq
        cos += [jnp.cos(ang), jnp.cos(ang)]
        sin += [-jnp.sin(ang), jnp.sin(ang)]
    cos, sin = jnp.concatenate(cos, axis=1), jnp.concatenate(sin, axis=1)
    return (jnp.concatenate([jnp.ones((n_ctx, RET_DK), F32), cos], axis=0),
            jnp.concatenate([jnp.zeros((n_ctx, RET_DK), F32), sin], axis=0))


def kernel(x, c, ctx, c_ctx, ada_w, ada_b, norm1_g, norm2_g, ab_w_in, ab_w_out, gla_wa, gla_ba, gla_norm_g, s5_lam_re, s5_lam_im, s5_log_step, s5_b_re, s5_b_im, s5_c_re, s5_c_im, s5_d, s5_glu_w, s5_glu_b, ret_w_in, ret_w_out, ret_decay_logit, ret_norm_g, moe_w_router, moe_b_router, moe_w_gu, moe_b_gu, moe_w_down, moe_b_down, final_norm_g):
    bsz, n_lat, d = x.shape
    n_ctx = ctx.shape[1]
    depth = ada_w.shape[0]
    assert depth == 2 and d == D_MODEL and bsz == 8, "kernels are laid out for the stated problem shape"
    assert n_ctx % TOKEN_TILE == 0 and n_lat % TOKEN_TILE == 0 and n_lat % GRID_W == 0
    t = n_ctx + n_lat
    nct = n_ctx // TOKEN_TILE

    cvec = jnp.zeros((16, d), F32).at[:bsz].set(c).at[bsz].set(c_ctx)
    mod = _ada_mod(cvec, ada_w, ada_b).reshape(depth, 16, 6, d)
    mods = [jnp.stack([jnp.broadcast_to(mod[l, bsz], (bsz, 6, d)), mod[l, :bsz]], axis=1) for l in range(depth)]

    w_in = ab_w_in[0].astype(BF16)
    cuts = [0, AB_QK, 2 * AB_QK, 2 * AB_QK + AB_V, 2 * AB_QK + 2 * AB_V, 2 * AB_QK + 2 * AB_V + 2 * GLA_RANK,
            w_in.shape[1]]
    pieces = [w_in[:, a:b] for a, b in zip(cuts[:-1], cuts[1:])]
    wa_pad = jnp.zeros((2, 2 * GLA_RANK, AB_QK), F32)
    wa_pad = wa_pad.at[0, :GLA_RANK].set(gla_wa[0, 0]).at[1, GLA_RANK:].set(gla_wa[0, 1])
    outs = _inproj0(ctx, x, mods[0], norm1_g[0], pieces, wa_pad, gla_ba[0].reshape(2, 1, AB_QK), nct)
    v, g, u = outs[8:]
    o_f, o_b = _gla((outs[0:4], outs[4:8]), v, n_ctx)
    ops = _s5_operators(s5_lam_re[0], s5_lam_im[0], s5_log_step[0], s5_b_re[0], s5_b_im[0], s5_c_re[0], s5_c_im[0])
    ys = _s5(u, ops, n_ctx)
    consts = [jnp.tile(gla_norm_g[0], GLA_HEADS).reshape(1, AB_V), s5_d[0].reshape(1, S5_CH),
              s5_glu_w[0].astype(BF16), s5_glu_b[0].reshape(1, S5_CH), ab_w_out[0].astype(BF16)]
    x1, h2, e_tl, w_tl, r_tl, cnt = _mix_call(
        functools.partial(_mix0_body, nct), "mix_gla_s5", (ctx, x), mods[0], t // TOKEN_TILE,
        [(o_f, False), (o_b, False), (g, False), (ys, False), (u, False)], consts,
        norm2_g[0], moe_w_router[0], moe_b_router[0], nct, 0)
    yk, w_tok = _moe(h2, e_tl, w_tl, r_tl, cnt, 0, moe_w_gu, moe_b_gu, moe_w_down, moe_b_down)

    w_in = ret_w_in[0].astype(BF16)
    cuts = [0, RET_QK, 2 * RET_QK, 2 * RET_QK + RET_MIX, w_in.shape[1]]
    pieces = [w_in[:, a:b] for a, b in zip(cuts[:-1], cuts[1:])]
    cos_t, sin_t = _rope_tables(n_ctx, n_lat)
    x_all, q, k, v, g = _inproj1(x1, mods[0], yk, w_tok, mods[1], norm1_g[1], cos_t, sin_t, pieces, nct)
    o_f, o_b = _retention(q, k, v, ret_decay_logit[0], n_ctx)
    consts = [ret_norm_g[0].reshape(1, RET_MIX), ret_w_out[0].astype(BF16)]
    x1, h2, e_tl, w_tl, r_tl, cnt = _mix_call(
        _mix1_body, "mix_retention", (x_all,), mods[1], n_lat // TOKEN_TILE,
        [(o_f, False), (o_b, False), (g, True)], consts,
        norm2_g[1], moe_w_router[1], moe_b_router[1], nct, nct)
    yk, w_tok = _moe(h2, e_tl, w_tl, r_tl, cnt, 1, moe_w_gu, moe_b_gu, moe_w_down, moe_b_down)
    return _combine(x1, mods[1], yk, w_tok, nct, nct, final_norm_g)
```

```python
import functools
import math

import jax
import jax.numpy as jnp
from jax import lax
from jax.experimental import pallas as pl
from jax.experimental.pallas import tpu as pltpu
from jax.experimental.pallas import tpu_sc as plsc

F32, BF16, I32, U32 = jnp.float32, jnp.bfloat16, jnp.int32, jnp.uint32

D_MODEL = 1024
GRID_W = 64
EPS = 1e-6
GLA_HEADS, GLA_DK, GLA_DV, GLA_RANK, GLA_TAU, GLA_CHUNK = 4, 64, 128, 16, 16.0, 64
GLA_BATCH = 8
AB_QK, AB_V = GLA_HEADS * GLA_DK, GLA_HEADS * GLA_DV
S5_CH, S5_GROUP, S5_GROUPS, S5_P = 512, 16, 32, 64
S5_CHUNK = 16
S5_FOLD_BATCH = 4
S5_SCAN_GROUPS = 2
RET_HEADS, RET_DK, RET_DV = 4, 256, 512
RET_CHUNK = 256
RET_QK, RET_MIX = RET_HEADS * RET_DK, RET_HEADS * RET_DV
ROPE_BASE = 10000.0
N_EXPERTS, TOP_K, D_FF = 32, 4, 1024
SWIGLU_LIMIT, SWIGLU_ALPHA = 7.0, 1.702
MOE_BLOCK = 512
TOKEN_TILE = 256
ADA_TILE = 768
VMEM_LIMIT = 56 * 1024 * 1024
SC_CORES, SC_SUBCORES = 2, 16
SC_WORKERS = SC_CORES * SC_SUBCORES
SC_CHUNK = 64

def _params(*sem):
    return pltpu.CompilerParams(dimension_semantics=sem, vmem_limit_bytes=VMEM_LIMIT)


def _dot(a, b):
    return jnp.dot(a, b, preferred_element_type=F32)


def _dot_nt(a, b):
    return lax.dot_general(a, b, (((1,), (1,)), ((), ())), preferred_element_type=F32)


def _dot_tn(a, b):
    return lax.dot_general(a, b, (((0,), (0,)), ((), ())), preferred_element_type=F32)


def _split(a):
    hi = a.astype(BF16)
    return hi, (a - hi.astype(F32)).astype(BF16)


def _dot3(a, b, dot=_dot):
    ah, al = _split(a)
    bh, bl = _split(b)
    return dot(ah, bh) + (dot(ah, bl) + dot(al, bh))


def _pack_rows(x):
    h = x.shape[1] // 2
    lo = lax.bitcast_convert_type(x[:, 0:h].astype(BF16).astype(F32), U32)
    hi = lax.bitcast_convert_type(x[:, h:2 * h].astype(BF16).astype(F32), U32)
    return hi | (lo >> 16)


def _unpack_rows(p):
    lo = lax.bitcast_convert_type(p << 16, F32)
    hi = lax.bitcast_convert_type(p & jnp.uint32(0xFFFF0000), F32)
    return lo, hi


def _silu(x):
    return x * jax.nn.sigmoid(x)


def _norm_mod(x, g, shift, scale):
    r = lax.rsqrt(jnp.mean(x * x, axis=-1, keepdims=True) + EPS)
    return (x * r * g) * (1.0 + scale) + shift


def _const_spec(shape):
    nd = len(shape)
    return pl.BlockSpec(shape, lambda *_: (0,) * nd)


def _ada_body(c_ref, w_ref, b_ref, o_ref):
    o_ref[0] = _dot3(_silu(c_ref[...]), w_ref[0]) + b_ref[0]


def _ada_mod(cvec, ada_w, ada_b):
    depth, d, n6 = ada_w.shape
    rows = cvec.shape[0]
    return pl.pallas_call(
        _ada_body,
        grid=(depth, n6 // ADA_TILE),
        in_specs=[_const_spec((rows, d)),
                  pl.BlockSpec((1, d, ADA_TILE), lambda l, j: (l, 0, j)),
                  pl.BlockSpec((1, 1, ADA_TILE), lambda l, j: (l, 0, j))],
        out_specs=pl.BlockSpec((1, rows, ADA_TILE), lambda l, j: (l, 0, j)),
        out_shape=jax.ShapeDtypeStruct((depth, rows, n6), F32),
        compiler_params=_params("arbitrary", "arbitrary"),
        name="ada_mod",
    )(cvec, ada_w, ada_b.reshape(depth, 1, n6))


def _tile_specs(nct, d):
    x_spec = pl.BlockSpec((1, TOKEN_TILE, d), lambda b, i: (b, i, 0))
    mod_spec = pl.BlockSpec((1, 1, 6, d), lambda b, i: (b, (i >= nct).astype(I32), 0, 0))
    return x_spec, mod_spec


def _split_specs(nct, d):
    ctx_spec = pl.BlockSpec((1, TOKEN_TILE, d), lambda b, i: (b, jnp.minimum(i, nct - 1), 0))
    lat_spec = pl.BlockSpec((1, TOKEN_TILE, d), lambda b, i: (b, jnp.maximum(i - nct, 0), 0))
    return ctx_spec, lat_spec


def _stream_tile(nct, ctx_ref, lat_ref):
    return jnp.where(pl.program_id(1) < nct, ctx_ref[0], lat_ref[0])


def _inproj0_body(nct, ctx_ref, lat_ref, mod_ref, g_ref, wq, wk, wv, wg, wlow, wu, wa_ref, ba_ref, tri_ref, ones_ref,
                  qd_f, ki_f, ks_f, ed_f, qd_b, ki_b, ks_b, ed_b, ov, og, ou):
    m = mod_ref[0, 0]
    h = _norm_mod(_stream_tile(nct, ctx_ref, lat_ref), g_ref[...], m[0:1], m[1:2]).astype(BF16)
    low = _dot(h, wlow[...])
    q = _dot(h, wq[...]) * (GLA_DK ** -0.5)
    k = _dot(h, wk[...])
    outs = ((qd_f, ki_f, ks_f, ed_f), (qd_b, ki_b, ks_b, ed_b))
    for d, (qd_ref, ki_ref, ks_ref, ed_ref) in enumerate(outs):
        z = _dot3(low, wa_ref[d]) + ba_ref[d]
        log_a = (jnp.minimum(z, 0.0) - jnp.log1p(jnp.exp(-jnp.abs(z)))) * (1.0 / GLA_TAU)
        la_hi, la_lo = _split(log_a)
        cum = _dot(tri_ref[d], la_hi) + _dot(tri_ref[d], la_lo)
        tot = _dot(ones_ref[...], la_hi) + _dot(ones_ref[...], la_lo)
        qd_ref[0] = (q * jnp.exp(cum)).astype(BF16)
        ki_ref[0] = (k * jnp.exp(-cum)).astype(BF16)
        ks_ref[0] = (k * jnp.exp(tot - cum)).astype(BF16)
        for ch in range(TOKEN_TILE // GLA_CHUNK):
            ed_ref[0, ch] = jnp.exp(tot[ch * GLA_CHUNK:ch * GLA_CHUNK + 1])
    ov[0] = _dot(h, wv[...]).astype(ov.dtype)
    og[0] = _dot(h, wg[...]).astype(og.dtype)
    u = _dot(h, wu[...])
    for qb in range(ou.shape[0]):
        ou[qb, 0] = u[:, qb * 128:(qb + 1) * 128]


def _inproj0(ctx, x, mods, norm_g, weights, wa_pad, ba, nct):
    bsz, n_lat, d = x.shape
    t = ctx.shape[1] + n_lat
    tm = TOKEN_TILE
    _, mod_spec = _tile_specs(nct, d)
    ctx_spec, lat_spec = _split_specs(nct, d)
    pos = jnp.arange(tm)
    same_chunk = (pos[:, None] // GLA_CHUNK) == (pos[None, :] // GLA_CHUNK)
    tri = jnp.stack([same_chunk & (pos[None, :] <= pos[:, None]),
                     same_chunk & (pos[None, :] >= pos[:, None])]).astype(BF16)
    ones = same_chunk.astype(BF16)
    consts = list(weights) + [wa_pad, ba, tri, ones]
    tok = lambda w, dt: (pl.BlockSpec((1, tm, w), lambda b, i: (b, i, 0)), jax.ShapeDtypeStruct((bsz, t, w), dt))
    per_chunk = (pl.BlockSpec((1, tm // GLA_CHUNK, 1, AB_QK), lambda b, i: (b, i, 0, 0)),
                 jax.ShapeDtypeStruct((bsz, t // GLA_CHUNK, 1, AB_QK), F32))
    one_dir = [tok(AB_QK, BF16)] * 3 + [per_chunk]
    u_blocks = (pl.BlockSpec((S5_CH // 128, 1, tm, 128), lambda b, i: (0, b, i, 0)),
                jax.ShapeDtypeStruct((S5_CH // 128, bsz, t, 128), F32))
    outs = one_dir + one_dir + [tok(AB_V, BF16), tok(AB_V, BF16), u_blocks]
    return pl.pallas_call(
        functools.partial(_inproj0_body, nct),
        grid=(bsz, t // tm),
        in_specs=[ctx_spec, lat_spec, mod_spec, _const_spec((1, d))] + [_const_spec(a.shape) for a in consts],
        out_specs=[o[0] for o in outs],
        out_shape=[o[1] for o in outs],
        compiler_params=_params("arbitrary", "arbitrary"),
        name="inproj_gla_s5",
    )(ctx, x, mods, norm_g.reshape(1, d), *consts)


def _rope(acc, cos_ref, sin_ref, o_ref, scale):
    for grp in range(acc.shape[1] // 128):
        half = grp % 2
        xg = acc[:, grp * 128:(grp + 1) * 128]
        cs = cos_ref[:, half * 128:(half + 1) * 128]
        sn = sin_ref[:, half * 128:(half + 1) * 128]
        out = xg * cs + pltpu.roll(xg, 64, 1) * sn
        o_ref[0, :, grp * 128:(grp + 1) * 128] = (out * scale).astype(o_ref.dtype)


def _moe_residual(x1_ref, g2, yk_ref, w_ref):
    d = x1_ref.shape[2]
    half = d // 2
    y_lo, y_hi = None, None
    for k in range(TOP_K):
        lo, hi = _unpack_rows(yk_ref[k, 0])
        wk = w_ref[0, :, k:k + 1]
        y_lo = lo * wk if y_lo is None else y_lo + lo * wk
        y_hi = hi * wk if y_hi is None else y_hi + hi * wk
    return x1_ref[0, :, 0:half] + g2[:, 0:half] * y_lo, x1_ref[0, :, half:d] + g2[:, half:d] * y_hi


def _inproj1_body(x1_ref, mod0_ref, yk_ref, w_ref, mod_ref, g_ref, cos_ref, sin_ref, wq, wk, wv, wg, ox, oq, ok, ov, og):
    half = x1_ref.shape[2] // 2
    x2_lo, x2_hi = _moe_residual(x1_ref, mod0_ref[0, 0][5:6], yk_ref, w_ref)
    ox[0, :, 0:half] = x2_lo
    ox[0, :, half:2 * half] = x2_hi
    m = mod_ref[0, 0]
    h = _norm_mod(jnp.concatenate([x2_lo, x2_hi], axis=1), g_ref[...], m[0:1], m[1:2]).astype(BF16)
    _rope(_dot(h, wq[...]), cos_ref, sin_ref, oq, 1.0)
    _rope(_dot(h, wk[...]), cos_ref, sin_ref, ok, RET_DK ** -0.5)
    ov[0] = _dot(h, wv[...]).astype(ov.dtype)
    og[0] = _dot(h, wg[...]).astype(og.dtype)


def _inproj1(x1, mods0, yk, w_tok, mods, norm_g, cos_t, sin_t, weights, nct):
    bsz, t, d = x1.shape
    tm = TOKEN_TILE
    x_spec, mod_spec = _tile_specs(nct, d)
    tok = lambda w: pl.BlockSpec((1, tm, w), lambda b, i: (b, i, 0))
    tab_spec = pl.BlockSpec((tm, RET_DK), lambda b, i: (i, 0))
    return pl.pallas_call(
        _inproj1_body,
        grid=(bsz, t // tm),
        in_specs=[x_spec, mod_spec, pl.BlockSpec((TOP_K, 1, tm, d // 2), lambda b, i: (0, b, i, 0)), tok(TOP_K),
                  mod_spec, _const_spec((1, d)), tab_spec, tab_spec] + [_const_spec(w.shape) for w in weights],
        out_specs=[tok(d)] + [tok(w.shape[1]) for w in weights],
        out_shape=[jax.ShapeDtypeStruct((bsz, t, d), F32)]
        + [jax.ShapeDtypeStruct((bsz, t, w.shape[1]), BF16) for w in weights],
        compiler_params=_params("arbitrary", "arbitrary"),
        name="inproj_retention",
    )(x1, mods0, yk.reshape(TOP_K, bsz, t, d // 2), w_tok.reshape(bsz, t, TOP_K), mods, norm_g.reshape(1, d),
      cos_t, sin_t, *weights)


def _backward_chunk(n, n_ctx_chunks, n_chunks):
    return jnp.where(n < n_ctx_chunks, n_ctx_chunks - 1 - n, n_chunks - 1 - (n - n_ctx_chunks))


def _gla_body(qd_f, ki_f, ks_f, ed_f, v_f, qd_b, ki_b, ks_b, ed_b, v_b, hmask_ref, bdmask_ref, o_f, o_b, st_f, st_b):
    c = GLA_CHUNK

    @pl.when(pl.program_id(1) == 0)
    def _():
        st_f[...] = jnp.zeros_like(st_f)
        st_b[...] = jnp.zeros_like(st_b)

    r4 = lax.broadcasted_iota(I32, (GLA_HEADS * c, c), 0) & (c - 1)
    c4 = lax.broadcasted_iota(I32, (GLA_HEADS * c, c), 1)
    dirs = ((qd_f, ki_f, ks_f, ed_f, v_f, o_f, st_f), (qd_b, ki_b, ks_b, ed_b, v_b, o_b, st_b))
    chains = [(bb, d) + dirs[d] for bb in range(qd_f.shape[0]) for d in range(2)]
    scores, inter, grow = [], [], []
    for bb, d, qd_ref, ki_ref, ks_ref, ed_ref, v_ref, o_ref, st_ref in chains:
        q_dec = qd_ref[bb]
        q_heads = jnp.concatenate([q_dec] * GLA_HEADS, axis=0) * hmask_ref[...]
        seen4 = (c4 <= r4) if d == 0 else (c4 >= r4)
        scores.append(jnp.where(seen4, _dot_nt(q_heads, ki_ref[bb]), 0.0).astype(BF16))
        inter.append(_dot_nt(q_dec, st_ref[bb].astype(BF16)))
        grow.append(_dot_tn(v_ref[bb], ks_ref[bb]))
    for (bb, d, qd_ref, ki_ref, ks_ref, ed_ref, v_ref, o_ref, st_ref), sc, o_inter, dst in zip(chains, scores, inter, grow):
        v = v_ref[bb]
        o_intra = jnp.concatenate(
            [_dot(sc[h * c:(h + 1) * c], v[:, h * GLA_DV:(h + 1) * GLA_DV]) for h in range(GLA_HEADS)], axis=1)
        o_ref[bb] = o_intra + o_inter
        st_ref[bb] = st_ref[bb] * ed_ref[bb, 0] + bdmask_ref[...] * dst


def _gla(per_dir, v, n_ctx):
    bsz, t, _ = v.shape
    nc, ncc = t // GLA_CHUNK, n_ctx // GLA_CHUNK
    gb = GLA_BATCH
    fwd = lambda b, n: (b, n, 0)
    bwd = lambda b, n: (b, _backward_chunk(n, ncc, nc), 0)
    hmask = (jnp.arange(AB_QK)[:, None] // GLA_CHUNK == jnp.arange(AB_QK)[None, :] // GLA_DK).astype(BF16)
    bdmask = (jnp.arange(AB_V)[:, None] // GLA_DV == jnp.arange(AB_QK)[None, :] // GLA_DK).astype(F32)

    def specs(idx):
        idx4 = lambda b, n: idx(b, n) + (0,)
        return [pl.BlockSpec((gb, GLA_CHUNK, AB_QK), idx)] * 3 + [pl.BlockSpec((gb, 1, 1, AB_QK), idx4),
                                                                  pl.BlockSpec((gb, GLA_CHUNK, AB_V), idx)]

    return pl.pallas_call(
        _gla_body,
        grid=(bsz // gb, nc),
        in_specs=specs(fwd) + specs(bwd) + [_const_spec(hmask.shape), _const_spec(bdmask.shape)],
        out_specs=[pl.BlockSpec((gb, GLA_CHUNK, AB_V), fwd), pl.BlockSpec((gb, GLA_CHUNK, AB_V), bwd)],
        out_shape=[jax.ShapeDtypeStruct((bsz, t, AB_V), F32)] * 2,
        scratch_shapes=[pltpu.VMEM((gb, AB_V, AB_QK), F32)] * 2,
        compiler_params=_params("arbitrary", "arbitrary"),
        name="gla_scan",
    )(*per_dir[0], v, *per_dir[1], v, hmask, bdmask)


def _cmul(x, y):
    return x[0] * y[0] - x[1] * y[1], x[0] * y[1] + x[1] * y[0]


def _s5_operators(lam_re, lam_im, log_step, b_re, b_im, c_re, c_im):
    ln = S5_CHUNK
    step = jnp.exp(log_step.astype(F32))[..., None]
    lam_re, lam_im = lam_re.astype(F32), lam_im.astype(F32)
    mag = jnp.exp(lam_re * step)
    a = (mag * jnp.cos(lam_im * step), mag * jnp.sin(lam_im * step))
    den = lam_re * lam_re + lam_im * lam_im
    f_re = ((a[0] - 1.0) * lam_re + a[1] * lam_im) / den
    f_im = (a[1] * lam_re - (a[0] - 1.0) * lam_im) / den
    bt_re, bt_im = b_re.transpose(0, 2, 1), b_im.transpose(0, 2, 1)
    bb = _cmul((f_re[:, :, None, :], f_im[:, :, None, :]), (bt_re, bt_im))
    bbt = jnp.concatenate([bb[0], -bb[1]], axis=-1)
    pw = (a[0][:, :, None, :], a[1][:, :, None, :])
    while pw[0].shape[2] < ln:
        top = (pw[0][:, :, -1:, :], pw[1][:, :, -1:, :])
        nxt = _cmul(top, pw)
        pw = (jnp.concatenate([pw[0], nxt[0]], axis=2), jnp.concatenate([pw[1], nxt[1]], axis=2))
    pw = (jnp.concatenate([jnp.ones_like(pw[0][:, :, :1]), pw[0]], axis=2),
          jnp.concatenate([jnp.zeros_like(pw[1][:, :, :1]), pw[1]], axis=2))
    ca = _cmul((c_re[:, :, None], c_im[:, :, None]), (pw[0][:, :, :, None, :], pw[1][:, :, :, None, :]))
    by_dir = lambda arr, lo, flip_d: jnp.stack([jnp.flip(arr[d, :, lo:lo + ln], axis=1) if d == flip_d
                                                else arr[d, :, lo:lo + ln] for d in range(2)])
    rows = lambda arr: arr.reshape(2, S5_GROUPS, ln * S5_GROUP, 2 * S5_P)
    cab = rows(by_dir(jnp.concatenate([ca[0], ca[1]], axis=-1), 0, 1))
    cab2 = rows(by_dir(jnp.concatenate([ca[0], -ca[1]], axis=-1), 1, 1)).astype(BF16)
    pwx = by_dir(jnp.concatenate([pw[0], pw[1]], axis=-1), 0, 0)
    lr, li = pw[0][:, :, ln], pw[1][:, :, ln]
    ac_rows = [jnp.concatenate([lr, lr], -1), jnp.concatenate([-li, li], -1), jnp.concatenate([li, -li], -1)]
    ac = jnp.stack(ac_rows + [jnp.zeros_like(ac_rows[0])] * 5, axis=2)
    return cab, cab2, bbt, pwx, ac


def _s5_group_operators(gg, cab_ref, bbt_ref, pwx_ref, tz, wx):
    ln, ch, p = S5_CHUNK, S5_GROUP, S5_P
    lane = lax.broadcasted_iota(I32, (ch, ln * ch), 1)
    for d in range(2):
        kern = _dot3(bbt_ref[d, gg], cab_ref[d, gg], dot=_dot_nt)
        bt = bbt_ref[d, gg]
        b_re, b_im = bt[:, 0:p], -bt[:, p:2 * p]
        for j in range(ln):
            if d == 0:
                blk = jnp.where(lane >= j * ch, kern if j == 0 else pltpu.roll(kern, j * ch, 1), 0.0)
            else:
                blk = jnp.where(lane < (j + 1) * ch, kern if j == ln - 1 else pltpu.roll(kern, (j + 1) * ch, 1), 0.0)
            tz[gg, d, j * ch:(j + 1) * ch, :] = blk.astype(BF16)
            pr, pi = pwx_ref[d, gg, j:j + 1, 0:p], pwx_ref[d, gg, j:j + 1, p:2 * p]
            x_re, x_im = pr * b_re - pi * b_im, pr * b_im + pi * b_re
            wx[gg, d, j * ch:(j + 1) * ch, :] = jnp.concatenate([x_re, x_im, x_im, x_re], axis=1).astype(BF16)


def _s5_placement(pall):
    rows, cols = pall.shape[1], pall.shape[2]
    row = lax.broadcasted_iota(I32, (rows, cols), 0)
    col = lax.broadcasted_iota(I32, (rows, cols), 1)
    same_token = (row >> 7) == (col >> 4)
    for g8 in range(pall.shape[0]):
        pall[g8] = jnp.where(same_token & ((row & 127) == g8 * S5_GROUP + (col & (S5_GROUP - 1))), 1.0, 0.0).astype(BF16)


def _first_step():
    return (pl.program_id(0) == 0) & (pl.program_id(1) == 0)


def _s5_fold_body(ncs, u_ref, o_ref, pall, ucat):
    @pl.when(_first_step())
    def _():
        _s5_placement(pall)

    for b in range(u_ref.shape[1]):
        for j in range(S5_CHUNK):
            ucat[b * ncs:(b + 1) * ncs, j * 128:(j + 1) * 128] = u_ref[0, b, pl.ds(j, ncs, stride=S5_CHUNK), :].astype(BF16)
    for g8 in range(pall.shape[0]):
        o_ref[g8] = _dot(ucat[...], pall[g8]).astype(BF16)


def _s5_unfold_body(ncs, y_ref, o_ref, pall):
    @pl.when(_first_step())
    def _():
        _s5_placement(pall)

    def token_pair(i2, carry):
        r0 = pl.multiple_of(i2 * 256, 256)
        acc = _dot_nt(y_ref[0], pall[0, pl.ds(r0, 256), :])
        for g8 in range(1, pall.shape[0]):
            acc = acc + _dot_nt(y_ref[g8], pall[g8, pl.ds(r0, 256), :])
        for b in range(o_ref.shape[1]):
            for par in range(2):
                o_ref[0, b, pl.ds(2 * i2 + par, ncs, stride=S5_CHUNK), :] = (
                    acc[b * ncs:(b + 1) * ncs, par * 128:(par + 1) * 128])
        return carry

    lax.fori_loop(0, S5_CHUNK // 2, token_pair, 0)


def _s5_body(ncs_ctx, ncs, rows, u_ref, cab_ref, cab2_ref, bbt_ref, pwx_ref, ac_ref, y_ref, tz, wx, *vecs):
    half = 2 * S5_P
    n_groups = u_ref.shape[0]
    groups = [vecs[6 * gg:6 * gg + 6] for gg in range(n_groups)]
    for gg, (xx_f, xs_f, xx_b, xs_b, _, _) in enumerate(groups):
        _s5_group_operators(gg, cab_ref, bbt_ref, pwx_ref, tz, wx)
        for d, (xx, xs) in enumerate(((xx_f, xs_f), (xx_b, xs_b))):
            r = _dot(u_ref[gg], wx[gg, d])
            xx[...] = r[:, :half]
            xs[...] = r[:, half:]

    def advance(ac, s, s_sw, x, x_sw):
        return ac[0:1] * s + ac[1:2] * s_sw + x, ac[0:1] * s_sw + ac[2:3] * s + x_sw

    def step(n, carry):
        at_f = pl.ds(n, rows, stride=ncs)
        at_b = pl.ds(_backward_chunk(n, ncs_ctx, ncs), rows, stride=ncs)
        out = []
        for gg, (xx_f, xs_f, xx_b, xs_b, sin_f, sin_b) in enumerate(groups):
            s_f, sw_f, s_b, sw_b = carry[4 * gg:4 * gg + 4]
            sin_f[at_f, :] = s_f
            sin_b[at_b, :] = s_b
            out += advance(ac_ref[0, gg], s_f, sw_f, xx_f[at_f, :], xs_f[at_f, :])
            out += advance(ac_ref[1, gg], s_b, sw_b, xx_b[at_b, :], xs_b[at_b, :])
        return tuple(out)

    zero = jnp.zeros((rows, half), F32)
    lax.fori_loop(0, ncs, step, (zero,) * (4 * n_groups))
    for gg, (_, _, _, _, sin_f, sin_b) in enumerate(groups):
        u = u_ref[gg]
        y_ref[gg] = (_dot(u, tz[gg, 0]) + _dot(u, tz[gg, 1]) + _dot_nt(sin_f[...].astype(BF16), cab2_ref[0, gg])
                     + _dot_nt(sin_b[...].astype(BF16), cab2_ref[1, gg])).astype(BF16)


def _s5(u4, ops, n_ctx):
    nq, bsz, t, _ = u4.shape
    ln, lanes = S5_CHUNK, S5_CHUNK * S5_GROUP
    gq = S5_GROUPS // nq
    ncs, ncs_ctx = t // ln, n_ctx // ln
    m = ncs * bsz
    hb = S5_FOLD_BATCH
    tok_spec = pl.BlockSpec((1, hb, t, 128), lambda q, h: (q, h, 0, 0))
    grp_spec = pl.BlockSpec((gq, hb * ncs, lanes), lambda q, h: (q, h, 0))
    pall = pltpu.VMEM((gq, ln * 128, lanes), BF16)
    ug = pl.pallas_call(
        functools.partial(_s5_fold_body, ncs),
        grid=(nq, bsz // hb),
        in_specs=[tok_spec],
        out_specs=grp_spec,
        out_shape=jax.ShapeDtypeStruct((S5_GROUPS, m, lanes), BF16),
        scratch_shapes=[pall, pltpu.VMEM((hb * ncs, ln * 128), BF16)],
        compiler_params=_params("arbitrary", "arbitrary"),
        name="s5_fold",
    )(u4)
    sg = S5_SCAN_GROUPS
    dir_spec = lambda arr: pl.BlockSpec((2, sg) + arr.shape[2:], lambda g: (0, g, 0, 0))
    yg = pl.pallas_call(
        functools.partial(_s5_body, ncs_ctx, ncs, bsz),
        grid=(S5_GROUPS // sg,),
        in_specs=[pl.BlockSpec((sg, m, lanes), lambda g: (g, 0, 0))] + [dir_spec(arr) for arr in ops],
        out_specs=pl.BlockSpec((sg, m, lanes), lambda g: (g, 0, 0)),
        out_shape=jax.ShapeDtypeStruct((S5_GROUPS, m, lanes), BF16),
        scratch_shapes=[pltpu.VMEM((sg, 2, lanes, lanes), BF16)] * 2 + [pltpu.VMEM((m, 2 * S5_P), F32)] * (6 * sg),
        compiler_params=_params("arbitrary"),
        name="s5_scan",
    )(ug, *ops)
    return pl.pallas_call(
        functools.partial(_s5_unfold_body, ncs),
        grid=(nq, bsz // hb),
        in_specs=[grp_spec],
        out_specs=tok_spec,
        out_shape=jax.ShapeDtypeStruct(u4.shape, F32),
        scratch_shapes=[pall],
        compiler_params=_params("arbitrary", "arbitrary"),
        name="s5_unfold",
    )(yg)


def _ret_body(q_f, k_f, v_f, q_b, k_b, v_b, dmat_ref, rsc_ref, csc_ref, gam_ref, o_f, o_b, st_f, st_b):
    @pl.when(pl.program_id(1) == 0)
    def _():
        st_f[...] = jnp.zeros_like(st_f)
        st_b[...] = jnp.zeros_like(st_b)

    dirs = ((q_f, k_f, v_f, o_f, st_f), (q_b, k_b, v_b, o_b, st_b))
    for d, (q_ref, k_ref, v_ref, o_ref, st_ref) in enumerate(dirs):
        for h in range(RET_HEADS):
            qh = q_ref[0, :, h * RET_DK:(h + 1) * RET_DK]
            kh = k_ref[0, :, h * RET_DK:(h + 1) * RET_DK]
            vh = v_ref[0, :, h * RET_DV:(h + 1) * RET_DV]
            st = st_ref[h]
            scores = (_dot_nt(qh, kh) * dmat_ref[d, h]).astype(BF16)
            o = _dot(scores, vh) + rsc_ref[d, h] * _dot(qh, st.astype(BF16))
            o_ref[0, :, h * RET_DV:(h + 1) * RET_DV] = o.astype(o_ref.dtype)
            k_state = (kh.astype(F32) * csc_ref[d, h]).astype(BF16)
            st_ref[h] = st * gam_ref[d, h] + _dot_tn(k_state, vh)


def _retention(q, k, v, decay_logit, n_ctx):
    bsz, t, _ = q.shape
    c = RET_CHUNK
    nc, ncc = t // c, n_ctx // c
    nl = nc - ncc
    log_gamma = jax.nn.log_sigmoid(decay_logit.astype(F32))[:, :, None, None]
    i = jnp.arange(c, dtype=F32)
    lag = i[:, None] - i[None, :]
    lag = jnp.stack([lag, -lag])[:, None]
    dmat = jnp.where(lag >= 0, jnp.exp(log_gamma * jnp.maximum(lag, 0.0)), 0.0)
    done = jnp.stack([i + 1.0, c - i])[:, None, :, None]
    rsc = jnp.exp(log_gamma * done)
    csc = jnp.exp(log_gamma * (c - done))
    gam = jnp.exp(log_gamma[:, :, 0, 0] * c)
    fwd = lambda b, n: (b, n, 0)
    bwd = lambda b, n: (b, _backward_chunk(n, ncc, nc), 0)
    o_fwd = lambda b, n: (b, jnp.maximum(n - ncc, 0), 0)
    o_bwd = lambda b, n: (b, nl - 1 - jnp.maximum(n - ncc, 0), 0)

    def specs(idx):
        return [pl.BlockSpec((1, c, RET_QK), idx), pl.BlockSpec((1, c, RET_QK), idx), pl.BlockSpec((1, c, RET_MIX), idx)]

    return pl.pallas_call(
        _ret_body,
        grid=(bsz, nc),
        in_specs=specs(fwd) + specs(bwd) + [_const_spec(dmat.shape), _const_spec(rsc.shape), _const_spec(csc.shape),
                                            pl.BlockSpec(memory_space=pltpu.SMEM)],
        out_specs=[pl.BlockSpec((1, c, RET_MIX), o_fwd), pl.BlockSpec((1, c, RET_MIX), o_bwd)],
        out_shape=[jax.ShapeDtypeStruct((bsz, nl * c, RET_MIX), BF16)] * 2,
        scratch_shapes=[pltpu.VMEM((RET_HEADS, RET_DK, RET_DV), F32)] * 2,
        compiler_params=_params("arbitrary", "arbitrary"),
        name="retention_scan",
    )(q, k, v, q, k, v, dmat, rsc, csc, gam)


def _zero_counts_at_start(cnt_ref):
    @pl.when(_first_step())
    def _():
        cnt_ref[...] = jnp.zeros_like(cnt_ref)


def _route(x, mixed, mod, n2g_ref, wr_ref, br_ref, x1_ref, h2_ref, e_ref, w_ref, r_ref, cnt_ref):
    tm = x.shape[0]
    x1 = x + mod[2:3] * mixed
    x1_ref[0] = x1
    h2 = _norm_mod(x1, n2g_ref[...], mod[3:4], mod[4:5])
    h2_ref[0] = _pack_rows(h2)
    logits = _dot3(wr_ref[...], h2, dot=_dot_nt) + br_ref[...]
    ie = lax.broadcasted_iota(I32, logits.shape, 0)
    tops, picks = [], []
    for _ in range(TOP_K):
        mx = jnp.max(logits, axis=0, keepdims=True)
        pick = jnp.min(jnp.where(logits == mx, ie, N_EXPERTS), axis=0, keepdims=True)
        tops.append(mx)
        picks.append(pick)
        logits = jnp.where(ie == pick, -jnp.inf, logits)
    ex = [jnp.exp(tk - tops[0]) for tk in tops]
    den = ex[0] + ex[1] + ex[2] + ex[3]
    for kk in range(TOP_K):
        w_ref[0, kk:kk + 1, :] = ex[kk] / den
        e_ref[0, kk:kk + 1, :] = picks[kk]

    earlier = (lax.broadcasted_iota(I32, (tm, tm), 0) < lax.broadcasted_iota(I32, (tm, tm), 1))
    earlier = jnp.where(earlier, 1.0, 0.0).astype(BF16)
    run = cnt_ref[:, 0:1]
    for kk, pick in enumerate(picks):
        onehot = jnp.where(ie == pick, 1.0, 0.0)
        before = _dot(onehot.astype(BF16), earlier) + run
        r_ref[0, kk:kk + 1, :] = jnp.sum(onehot * before, axis=0, keepdims=True).astype(I32)
        run = run + jnp.sum(onehot, axis=1, keepdims=True)
    cnt_ref[...] = jnp.broadcast_to(run, cnt_ref.shape)


def _mix0_body(nct, ctx_ref, lat_ref, mod_ref, of_ref, ob_ref, g_ref, ys_ref, u_ref, gng_ref, dsk_ref, gluw_ref,
               glub_ref, wo_ref, n2g_ref, wr_ref, br_ref, x1_ref, h2_ref, e_ref, w_ref, r_ref, cnt_ref):
    _zero_counts_at_start(cnt_ref)
    o = of_ref[0] + ob_ref[0]
    heads = []
    for h in range(GLA_HEADS):
        oh = o[:, h * GLA_DV:(h + 1) * GLA_DV]
        heads.append(oh * lax.rsqrt(jnp.mean(oh * oh, axis=-1, keepdims=True) + EPS))
    gla = jnp.concatenate(heads, axis=1) * gng_ref[...] * _silu(g_ref[0].astype(F32))
    lane_blocks = lambda ref: jnp.concatenate([ref[qb, 0] for qb in range(ref.shape[0])], axis=1)
    y = jax.nn.gelu(lane_blocks(ys_ref) + dsk_ref[...] * lane_blocks(u_ref))
    y = y * jax.nn.sigmoid(_dot(y.astype(BF16), gluw_ref[...]) + glub_ref[...])
    mixed = _dot(gla.astype(BF16), wo_ref[0:AB_V]) + _dot(y.astype(BF16), wo_ref[AB_V:AB_V + S5_CH])
    _route(_stream_tile(nct, ctx_ref, lat_ref), mixed, mod_ref[0, 0], n2g_ref, wr_ref, br_ref,
           x1_ref, h2_ref, e_ref, w_ref, r_ref, cnt_ref)


def _mix1_body(x_ref, mod_ref, of_ref, ob_ref, g_ref, ng_ref, wo_ref, n2g_ref, wr_ref, br_ref,
               x1_ref, h2_ref, e_ref, w_ref, r_ref, cnt_ref):
    _zero_counts_at_start(cnt_ref)
    mixed = None
    for h in range(RET_HEADS):
        sl = slice(h * RET_DV, (h + 1) * RET_DV)
        oh = of_ref[0, :, sl].astype(F32) + ob_ref[0, :, sl].astype(F32)
        mu = jnp.mean(oh, axis=-1, keepdims=True)
        cen = oh - mu
        var = jnp.mean(cen * cen, axis=-1, keepdims=True)
        gated = cen * lax.rsqrt(var + EPS) * ng_ref[:, sl] * _silu(g_ref[0, :, sl].astype(F32))
        part = _dot(gated.astype(BF16), wo_ref[sl])
        mixed = part if mixed is None else mixed + part
    _route(x_ref[0], mixed, mod_ref[0, 0], n2g_ref, wr_ref, br_ref, x1_ref, h2_ref, e_ref, w_ref, r_ref, cnt_ref)


def _mix_call(body, name, stream, mods, tiles, acts, consts, norm2_g, w_router, b_router, n_tok, seg_tile0):
    bsz, _, d = stream[-1].shape
    tm = TOKEN_TILE
    off = lambda b, i: (b, i + seg_tile0, 0)
    loc = lambda b, i: (b, i, 0)
    ntl = bsz * tiles
    flat = lambda b, i: (b * tiles + i, 0, 0)
    in_specs = list(_split_specs(n_tok, d)) if len(stream) == 2 else [pl.BlockSpec((1, tm, d), off)]
    in_specs.append(pl.BlockSpec((1, 1, 6, d), lambda b, i: (b, ((i + seg_tile0) >= n_tok).astype(I32), 0, 0)))
    args = list(stream) + [mods]
    for arr, offset in acts:
        if arr.ndim == 4:
            in_specs.append(pl.BlockSpec((arr.shape[0], 1, tm, arr.shape[3]), lambda b, i: (0, b, i, 0)))
        else:
            in_specs.append(pl.BlockSpec((1, tm, arr.shape[2]), off if offset else loc))
        args.append(arr)
    tail = list(consts) + [norm2_g.reshape(1, d), w_router.T, b_router.reshape(N_EXPERTS, 1)]
    in_specs += [_const_spec(a.shape) for a in tail]
    args += tail
    tok_out = pl.BlockSpec((1, TOP_K, tm), flat)
    return pl.pallas_call(
        body,
        grid=(bsz, tiles),
        in_specs=in_specs,
        out_specs=[pl.BlockSpec((1, tm, d), loc), pl.BlockSpec((1, tm, d // 2), loc), tok_out, tok_out, tok_out,
                   _const_spec((N_EXPERTS, 128))],
        out_shape=[jax.ShapeDtypeStruct((bsz, tiles * tm, d), F32), jax.ShapeDtypeStruct((bsz, tiles * tm, d // 2), U32),
                   jax.ShapeDtypeStruct((ntl, TOP_K, tm), I32), jax.ShapeDtypeStruct((ntl, TOP_K, tm), F32),
                   jax.ShapeDtypeStruct((ntl, TOP_K, tm), I32), jax.ShapeDtypeStruct((N_EXPERTS, 128), F32)],
        compiler_params=_params("arbitrary", "arbitrary"),
        name=name,
    )(*args)


def _cast_rows(src_ref, dst_ref, rows):
    def chunk(j, carry):
        r = pl.multiple_of(j * rows, rows)
        dst_ref[pl.ds(r, rows), :] = src_ref[0, 0, pl.ds(r, rows), :].astype(BF16)
        return carry

    lax.fori_loop(0, dst_ref.shape[0] // rows, chunk, 0)


def _expert_body(be_ref, nu_ref, x_ref, wgu_ref, bgu_ref, wd_ref, bd_ref, o_ref, wgu_bf, wd_bf):
    i = pl.program_id(0)
    live = i < nu_ref[0]
    new_expert = (i == 0) | (be_ref[i] != be_ref[jnp.maximum(i - 1, 0)])

    @pl.when(live & new_expert)
    def _():
        _cast_rows(wgu_ref, wgu_bf, 128)
        _cast_rows(wd_ref, wd_bf, 128)

    @pl.when(live)
    def _():
        x_lo, x_hi = _unpack_rows(x_ref[...])
        half = x_lo.shape[1]
        gu = (_dot(x_lo.astype(BF16), wgu_bf[0:half]) + _dot(x_hi.astype(BF16), wgu_bf[half:2 * half])
              + bgu_ref[0, 0])
        gate = jnp.minimum(gu[:, :D_FF], SWIGLU_LIMIT)
        lin = jnp.clip(gu[:, D_FF:], -SWIGLU_LIMIT, SWIGLU_LIMIT)
        act = gate * jax.nn.sigmoid(SWIGLU_ALPHA * gate) * (lin + 1.0)
        y = _dot(act.astype(BF16), wd_bf[...]) + bd_ref[0, 0]
        o_ref[...] = _pack_rows(y)

    @pl.when(i >= nu_ref[0])
    def _():
        o_ref[...] = jnp.zeros_like(o_ref)


def _experts(xb, block_e, n_used, layer, w_gu, b_gu, w_down, b_down):
    n_slots, half = xb.shape
    d = 2 * half
    n_blocks = n_slots // MOE_BLOCK
    depth = w_gu.shape[0]
    by_expert = lambda i, be, nu: (layer, be[i], 0, 0)
    return pl.pallas_call(
        _expert_body,
        grid_spec=pltpu.PrefetchScalarGridSpec(
            num_scalar_prefetch=2,
            grid=(n_blocks,),
            in_specs=[pl.BlockSpec((MOE_BLOCK, half), lambda i, be, nu: (i, 0)),
                      pl.BlockSpec((1, 1, d, 2 * D_FF), by_expert), pl.BlockSpec((1, 1, 1, 2 * D_FF), by_expert),
                      pl.BlockSpec((1, 1, D_FF, d), by_expert), pl.BlockSpec((1, 1, 1, d), by_expert)],
            out_specs=pl.BlockSpec((MOE_BLOCK, half), lambda i, be, nu: (i, 0)),
            scratch_shapes=[pltpu.VMEM((d, 2 * D_FF), BF16), pltpu.VMEM((D_FF, d), BF16)]),
        out_shape=jax.ShapeDtypeStruct((n_slots, half), U32),
        compiler_params=_params("arbitrary"),
        name="moe_experts",
    )(block_e, n_used, xb, w_gu, b_gu.reshape(depth, N_EXPERTS, 1, 2 * D_FF), w_down,
      b_down.reshape(depth, N_EXPERTS, 1, d))


def _combine_body(x1_ref, mod_ref, yk_ref, w_ref, fg_ref, o_ref):
    d = x1_ref.shape[2]
    half = d // 2
    x2_lo, x2_hi = _moe_residual(x1_ref, mod_ref[0, 0][5:6], yk_ref, w_ref)
    ms = (jnp.sum(x2_lo * x2_lo, axis=-1, keepdims=True) + jnp.sum(x2_hi * x2_hi, axis=-1, keepdims=True)) / d
    r = lax.rsqrt(ms + EPS)
    o_ref[0, :, 0:half] = x2_lo * r * fg_ref[:, 0:half]
    o_ref[0, :, half:d] = x2_hi * r * fg_ref[:, half:d]


def _combine(x1, mods, yk, w_tok, seg_tile0, n_tok, final_g):
    bsz, t, d = x1.shape
    tm = TOKEN_TILE
    loc = lambda b, i: (b, i, 0)
    return pl.pallas_call(
        _combine_body,
        grid=(bsz, t // tm),
        in_specs=[pl.BlockSpec((1, tm, d), loc),
                  pl.BlockSpec((1, 1, 6, d), lambda b, i: (b, ((i + seg_tile0) >= n_tok).astype(I32), 0, 0)),
                  pl.BlockSpec((TOP_K, 1, tm, d // 2), lambda b, i: (0, b, i, 0)),
                  pl.BlockSpec((1, tm, TOP_K), loc), _const_spec((1, d))],
        out_specs=pl.BlockSpec((1, tm, d), loc),
        out_shape=jax.ShapeDtypeStruct((bsz, t, d), F32),
        compiler_params=_params("arbitrary", "arbitrary"),
        name="moe_combine",
    )(x1, mods, yk.reshape(TOP_K, bsz, t, d // 2), w_tok.reshape(bsz, t, TOP_K), final_g.reshape(1, d))


def _sc_mesh():
    return plsc.VectorSubcoreMesh(core_axis_name="core", subcore_axis_name="subcore",
                                  num_cores=SC_CORES, num_subcores=SC_SUBCORES)


def _sc_worker_base(per_worker):
    return (lax.axis_index("subcore") * SC_CORES + lax.axis_index("core")) * per_worker


def _sc_dispatch(rows, dest, n_slots):
    n, w = rows.shape
    per_worker = n // SC_WORKERS
    assert per_worker * SC_WORKERS == n and per_worker % SC_CHUNK == 0

    @functools.partial(
        pl.kernel, mesh=_sc_mesh(), out_type=jax.ShapeDtypeStruct((n_slots, w), rows.dtype),
        scratch_types=[pltpu.VMEM((SC_CHUNK,), I32)] * TOP_K + [pltpu.VMEM((SC_CHUNK, w), rows.dtype),
                                                                pltpu.SemaphoreType.DMA],
        name="moe_dispatch")
    def scatter_rows(rows_hbm, dest_hbm, out_hbm, *scratch):
        idx_refs, buf, sem = scratch[:TOP_K], scratch[TOP_K], scratch[TOP_K + 1]
        base0 = _sc_worker_base(per_worker)

        @pl.loop(0, per_worker // SC_CHUNK)
        def _(j):
            base = base0 + j * SC_CHUNK
            pltpu.sync_copy(rows_hbm.at[pl.ds(base, SC_CHUNK)], buf)
            for k, idx in enumerate(idx_refs):
                pltpu.sync_copy(dest_hbm.at[pl.ds(k * n + base, SC_CHUNK)], idx)
            copies = [pltpu.make_async_copy(buf, out_hbm.at[idx], sem) for idx in idx_refs]
            for cp in copies:
                cp.start()
            for cp in copies:
                cp.wait()

    return scatter_rows(rows, dest)


def _sc_gather(table, idx):
    n = idx.shape[0]
    w = table.shape[1]
    per_worker = n // SC_WORKERS
    n_chunks = per_worker // SC_CHUNK
    assert per_worker * SC_WORKERS == n and n_chunks * SC_CHUNK == per_worker and n_chunks % 2 == 0

    @functools.partial(
        pl.kernel, mesh=_sc_mesh(), out_type=jax.ShapeDtypeStruct((n, w), table.dtype),
        scratch_types=([pltpu.VMEM((SC_CHUNK,), I32)] * 2 + [pltpu.VMEM((SC_CHUNK, w), table.dtype)] * 2
                       + [pltpu.SemaphoreType.DMA] * 4),
        name="moe_gather")
    def gather_rows(table_hbm, idx_hbm, out_hbm, idx0, idx1, buf0, buf1, gsem0, gsem1, wsem0, wsem1):
        base0 = _sc_worker_base(per_worker)

        def gather_copy(idx_v, buf, sem):
            return pltpu.make_async_copy(table_hbm.at[idx_v], buf, sem)

        def write_copy(j, buf, sem):
            return pltpu.make_async_copy(buf, out_hbm.at[pl.ds(base0 + j * SC_CHUNK, SC_CHUNK)], sem)

        def start_gather(j, idx_v, buf, sem):
            pltpu.sync_copy(idx_hbm.at[pl.ds(base0 + j * SC_CHUNK, SC_CHUNK)], idx_v)
            gather_copy(idx_v, buf, sem).start()

        start_gather(0, idx0, buf0, gsem0)

        @pl.loop(0, n_chunks, step=2)
        def _(j):
            @pl.when(j > 0)
            def _():
                write_copy(j - 1, buf1, wsem1).wait()
            start_gather(j + 1, idx1, buf1, gsem1)
            gather_copy(idx0, buf0, gsem0).wait()
            write_copy(j, buf0, wsem0).start()

            @pl.when(j + 2 < n_chunks)
            def _():
                write_copy(j, buf0, wsem0).wait()
                start_gather(j + 2, idx0, buf0, gsem0)
            gather_copy(idx1, buf1, gsem1).wait()
            write_copy(j + 1, buf1, wsem1).start()

        write_copy(n_chunks - 2, buf0, wsem0).wait()
        write_copy(n_chunks - 1, buf1, wsem1).wait()

    return gather_rows(table, idx)


def _moe(h2, e_tl, w_tl, r_tl, cnt, layer, w_gu, b_gu, w_down, b_down):
    bsz, t, half = h2.shape
    n = bsz * t
    flat = lambda a: a.transpose(1, 0, 2).reshape(TOP_K, n)
    e_k, w_k, r_k = flat(e_tl), flat(w_tl), flat(r_tl)
    counts = cnt[:, 0].astype(I32)
    padded = (counts + MOE_BLOCK - 1) // MOE_BLOCK * MOE_BLOCK
    pad_end = jnp.cumsum(padded)
    pad_start = pad_end - padded
    n_blocks = (n * TOP_K + MOE_BLOCK - 1) // MOE_BLOCK + N_EXPERTS
    block_start = jnp.arange(n_blocks, dtype=I32) * MOE_BLOCK
    block_e = jnp.minimum(jnp.sum((pad_end[None, :] <= block_start[:, None]).astype(I32), axis=1), N_EXPERTS - 1)
    n_used = (pad_end[-1:] // MOE_BLOCK).astype(I32)
    start_k = jnp.sum(jnp.where(e_k[..., None] == jnp.arange(N_EXPERTS, dtype=I32), pad_start, 0), axis=-1)
    dest = (start_k + r_k).reshape(TOP_K * n)
    xb = _sc_dispatch(h2.reshape(n, half), dest, n_blocks * MOE_BLOCK)
    yb = _experts(xb, block_e, n_used, layer, w_gu, b_gu, w_down, b_down)
    return _sc_gather(yb, dest), w_k.T


def _rope_tables(n_ctx, n_lat):
    n_freq = RET_DK // 4
    inv_freq = ROPE_BASE ** (-jnp.arange(n_freq, dtype=F32) / n_freq)
    pos = jnp.arange(n_lat, dtype=I32)
    cos, sin = [], []
    for p in (pos // GRID_W, pos % GRID_W):
        ang = p.astype(F32)[:, None] * inv_freq
        cos += [jnp.cos(ang), jnp.cos(ang)]
        sin += [-jnp.sin(ang), jnp.sin(ang)]
    cos, sin = jnp.concatenate(cos, axis=1), jnp.concatenate(sin, axis=1)
    return (jnp.concatenate([jnp.ones((n_ctx, RET_DK), F32), cos], axis=0),
            jnp.concatenate([jnp.zeros((n_ctx, RET_DK), F32), sin], axis=0))


def kernel(x, c, ctx, c_ctx, ada_w, ada_b, norm1_g, norm2_g, ab_w_in, ab_w_out, gla_wa, gla_ba, gla_norm_g, s5_lam_re, s5_lam_im, s5_log_step, s5_b_re, s5_b_im, s5_c_re, s5_c_im, s5_d, s5_glu_w, s5_glu_b, ret_w_in, ret_w_out, ret_decay_logit, ret_norm_g, moe_w_router, moe_b_router, moe_w_gu, moe_b_gu, moe_w_down, moe_b_down, final_norm_g):
    bsz, n_lat, d = x.shape
    n_ctx = ctx.shape[1]
    depth = ada_w.shape[0]
    assert depth == 2 and d == D_MODEL and bsz == 8, "kernels are laid out for the stated problem shape"
    assert n_ctx % TOKEN_TILE == 0 and n_lat % TOKEN_TILE == 0 and n_lat % GRID_W == 0
    t = n_ctx + n_lat
    nct = n_ctx // TOKEN_TILE

    cvec = jnp.zeros((16, d), F32).at[:bsz].set(c).at[bsz].set(c_ctx)
    mod = _ada_mod(cvec, ada_w, ada_b).reshape(depth, 16, 6, d)
    mods = [jnp.stack([jnp.broadcast_to(mod[l, bsz], (bsz, 6, d)), mod[l, :bsz]], axis=1) for l in range(depth)]

    w_in = ab_w_in[0].astype(BF16)
    cuts = [0, AB_QK, 2 * AB_QK, 2 * AB_QK + AB_V, 2 * AB_QK + 2 * AB_V, 2 * AB_QK + 2 * AB_V + 2 * GLA_RANK,
            w_in.shape[1]]
    pieces = [w_in[:, a:b] for a, b in zip(cuts[:-1], cuts[1:])]
    wa_pad = jnp.zeros((2, 2 * GLA_RANK, AB_QK), F32)
    wa_pad = wa_pad.at[0, :GLA_RANK].set(gla_wa[0, 0]).at[1, GLA_RANK:].set(gla_wa[0, 1])
    outs = _inproj0(ctx, x, mods[0], norm1_g[0], pieces, wa_pad, gla_ba[0].reshape(2, 1, AB_QK), nct)
    v, g, u = outs[8:]
    o_f, o_b = _gla((outs[0:4], outs[4:8]), v, n_ctx)
    ops = _s5_operators(s5_lam_re[0], s5_lam_im[0], s5_log_step[0], s5_b_re[0], s5_b_im[0], s5_c_re[0], s5_c_im[0])
    ys = _s5(u, ops, n_ctx)
    consts = [jnp.tile(gla_norm_g[0], GLA_HEADS).reshape(1, AB_V), s5_d[0].reshape(1, S5_CH),
              s5_glu_w[0].astype(BF16), s5_glu_b[0].reshape(1, S5_CH), ab_w_out[0].astype(BF16)]
    x1, h2, e_tl, w_tl, r_tl, cnt = _mix_call(
        functools.partial(_mix0_body, nct), "mix_gla_s5", (ctx, x), mods[0], t // TOKEN_TILE,
        [(o_f, False), (o_b, False), (g, False), (ys, False), (u, False)], consts,
        norm2_g[0], moe_w_router[0], moe_b_router[0], nct, 0)
    yk, w_tok = _moe(h2, e_tl, w_tl, r_tl, cnt, 0, moe_w_gu, moe_b_gu, moe_w_down, moe_b_down)

    w_in = ret_w_in[0].astype(BF16)
    cuts = [0, RET_QK, 2 * RET_QK, 2 * RET_QK + RET_MIX, w_in.shape[1]]
    pieces = [w_in[:, a:b] for a, b in zip(cuts[:-1], cuts[1:])]
    cos_t, sin_t = _rope_tables(n_ctx, n_lat)
    x_all, q, k, v, g = _inproj1(x1, mods[0], yk, w_tok, mods[1], norm1_g[1], cos_t, sin_t, pieces, nct)
    o_f, o_b = _retention(q, k, v, ret_decay_logit[0], n_ctx)
    consts = [ret_norm_g[0].reshape(1, RET_MIX), ret_w_out[0].astype(BF16)]
    x1, h2, e_tl, w_tl, r_tl, cnt = _mix_call(
        _mix1_body, "mix_retention", (x_all,), mods[1], n_lat // TOKEN_TILE,
        [(o_f, False), (o_b, False), (g, True)], consts,
        norm2_g[1], moe_w_router[1], moe_b_router[1], nct, nct)
    yk, w_tok = _moe(h2, e_tl, w_tl, r_tl, cnt, 1, moe_w_gu, moe_b_gu, moe_w_down, moe_b_down)
    return _combine(x1, mods[1], yk, w_tok, nct, nct, final_norm_g)
```

```python
import functools
import math

import jax
import jax.numpy as jnp
from jax import lax
from jax.experimental import pallas as pl
from jax.experimental.pallas import tpu as pltpu
from jax.experimental.pallas import tpu_sc as plsc

F32, BF16, I32, U32 = jnp.float32, jnp.bfloat16, jnp.int32, jnp.uint32

D_MODEL = 1024
GRID_W = 64
EPS = 1e-6
GLA_HEADS, GLA_DK, GLA_DV, GLA_RANK, GLA_TAU, GLA_CHUNK = 4, 64, 128, 16, 16.0, 64
GLA_BATCH = 8
AB_QK, AB_V = GLA_HEADS * GLA_DK, GLA_HEADS * GLA_DV
S5_CH, S5_GROUP, S5_GROUPS, S5_P = 512, 16, 32, 64
S5_CHUNK = 16
S5_FOLD_BATCH = 4
S5_SCAN_GROUPS = 2
RET_HEADS, RET_DK, RET_DV = 4, 256, 512
RET_CHUNK = 256
RET_QK, RET_MIX = RET_HEADS * RET_DK, RET_HEADS * RET_DV
ROPE_BASE = 10000.0
N_EXPERTS, TOP_K, D_FF = 32, 4, 1024
SWIGLU_LIMIT, SWIGLU_ALPHA = 7.0, 1.702
MOE_BLOCK = 512
TOKEN_TILE = 256
LATENT_TILE = 512
ADA_TILE = 768
VMEM_LIMIT = 56 * 1024 * 1024
SC_CORES, SC_SUBCORES = 2, 16
SC_WORKERS = SC_CORES * SC_SUBCORES
SC_CHUNK = 64
LANES = 128

def _params(*sem):
    return pltpu.CompilerParams(dimension_semantics=sem, vmem_limit_bytes=VMEM_LIMIT)


def _dot(a, b):
    return jnp.dot(a, b, preferred_element_type=F32)


def _dot_nt(a, b):
    return lax.dot_general(a, b, (((1,), (1,)), ((), ())), preferred_element_type=F32)


def _dot_tn(a, b):
    return lax.dot_general(a, b, (((0,), (0,)), ((), ())), preferred_element_type=F32)


def _split(a):
    hi = a.astype(BF16)
    return hi, (a - hi.astype(F32)).astype(BF16)


def _dot3(a, b, dot=_dot):
    ah, al = _split(a)
    bh, bl = _split(b)
    return dot(ah, bh) + (dot(ah, bl) + dot(al, bh))


def _pack_rows(x):
    h = x.shape[1] // 2
    lo = lax.bitcast_convert_type(x[:, 0:h].astype(BF16).astype(F32), U32)
    hi = lax.bitcast_convert_type(x[:, h:2 * h].astype(BF16).astype(F32), U32)
    return hi | (lo >> 16)


def _unpack_rows(p):
    lo = lax.bitcast_convert_type(p << 16, F32)
    hi = lax.bitcast_convert_type(p & jnp.uint32(0xFFFF0000), F32)
    return lo, hi


def _silu(x):
    return x * jax.nn.sigmoid(x)


def _norm_mod(x, g, shift, scale):
    r = lax.rsqrt(jnp.mean(x * x, axis=-1, keepdims=True) + EPS)
    return (x * r * g) * (1.0 + scale) + shift


def _const_spec(shape):
    nd = len(shape)
    return pl.BlockSpec(shape, lambda *_: (0,) * nd)


def _ada_body(c_ref, w_ref, b_ref, o_ref):
    o_ref[0] = _dot3(_silu(c_ref[...]), w_ref[0]) + b_ref[0]


def _ada_mod(cvec, ada_w, ada_b):
    depth, d, n6 = ada_w.shape
    rows = cvec.shape[0]
    return pl.pallas_call(
        _ada_body,
        grid=(depth, n6 // ADA_TILE),
        in_specs=[_const_spec((rows, d)),
                  pl.BlockSpec((1, d, ADA_TILE), lambda l, j: (l, 0, j)),
                  pl.BlockSpec((1, 1, ADA_TILE), lambda l, j: (l, 0, j))],
        out_specs=pl.BlockSpec((1, rows, ADA_TILE), lambda l, j: (l, 0, j)),
        out_shape=jax.ShapeDtypeStruct((depth, rows, n6), F32),
        compiler_params=_params("arbitrary", "arbitrary"),
        name="ada_mod",
    )(cvec, ada_w, ada_b.reshape(depth, 1, n6))


def _tile_specs(nct, d):
    x_spec = pl.BlockSpec((1, TOKEN_TILE, d), lambda b, i: (b, i, 0))
    mod_spec = pl.BlockSpec((1, 1, 6, d), lambda b, i: (b, (i >= nct).astype(I32), 0, 0))
    return x_spec, mod_spec


def _split_specs(nct, d):
    ctx_spec = pl.BlockSpec((1, TOKEN_TILE, d), lambda b, i: (b, jnp.minimum(i, nct - 1), 0))
    lat_spec = pl.BlockSpec((1, TOKEN_TILE, d), lambda b, i: (b, jnp.maximum(i - nct, 0), 0))
    return ctx_spec, lat_spec


def _stream_tile(nct, ctx_ref, lat_ref):
    return jnp.where(pl.program_id(1) < nct, ctx_ref[0], lat_ref[0])


def _inproj0_body(nct, ctx_ref, lat_ref, mod_ref, g_ref, wq, wk, wv, wg, wlow, wu, wa_ref, ba_ref, tri_ref, ones_ref,
                  qd_f, ki_f, ks_f, ed_f, qd_b, ki_b, ks_b, ed_b, ov, og, ou):
    m = mod_ref[0, 0]
    h = _norm_mod(_stream_tile(nct, ctx_ref, lat_ref), g_ref[...], m[0:1], m[1:2]).astype(BF16)
    low = _dot(h, wlow[...])
    q = _dot(h, wq[...]) * (GLA_DK ** -0.5)
    k = _dot(h, wk[...])
    outs = ((qd_f, ki_f, ks_f, ed_f), (qd_b, ki_b, ks_b, ed_b))
    for d, (qd_ref, ki_ref, ks_ref, ed_ref) in enumerate(outs):
        z = _dot3(low, wa_ref[d]) + ba_ref[d]
        log_a = (jnp.minimum(z, 0.0) - jnp.log1p(jnp.exp(-jnp.abs(z)))) * (1.0 / GLA_TAU)
        la_hi, la_lo = _split(log_a)
        cum = _dot(tri_ref[d], la_hi) + _dot(tri_ref[d], la_lo)
        tot = _dot(ones_ref[...], la_hi) + _dot(ones_ref[...], la_lo)
        qd_ref[0] = (q * jnp.exp(cum)).astype(BF16)
        ki_ref[0] = (k * jnp.exp(-cum)).astype(BF16)
        ks_ref[0] = (k * jnp.exp(tot - cum)).astype(BF16)
        for ch in range(TOKEN_TILE // GLA_CHUNK):
            ed_ref[0, ch] = jnp.exp(tot[ch * GLA_CHUNK:ch * GLA_CHUNK + 1])
    ov[0] = _dot(h, wv[...]).astype(ov.dtype)
    og[0] = _dot(h, wg[...]).astype(og.dtype)
    u = _dot(h, wu[...])
    for qb in range(ou.shape[0]):
        ou[qb, 0] = u[:, qb * LANES:(qb + 1) * LANES]


def _inproj0(ctx, x, mods, norm_g, weights, wa_pad, ba, nct):
    bsz, n_lat, d = x.shape
    t = ctx.shape[1] + n_lat
    tm = TOKEN_TILE
    _, mod_spec = _tile_specs(nct, d)
    ctx_spec, lat_spec = _split_specs(nct, d)
    pos = jnp.arange(tm)
    same_chunk = (pos[:, None] // GLA_CHUNK) == (pos[None, :] // GLA_CHUNK)
    tri = jnp.stack([same_chunk & (pos[None, :] <= pos[:, None]),
                     same_chunk & (pos[None, :] >= pos[:, None])]).astype(BF16)
    ones = same_chunk.astype(BF16)
    consts = list(weights) + [wa_pad, ba, tri, ones]
    tok = lambda w, dt: (pl.BlockSpec((1, tm, w), lambda b, i: (b, i, 0)), jax.ShapeDtypeStruct((bsz, t, w), dt))
    per_chunk = (pl.BlockSpec((1, tm // GLA_CHUNK, 1, AB_QK), lambda b, i: (b, i, 0, 0)),
                 jax.ShapeDtypeStruct((bsz, t // GLA_CHUNK, 1, AB_QK), F32))
    one_dir = [tok(AB_QK, BF16)] * 3 + [per_chunk]
    u_blocks = (pl.BlockSpec((S5_CH // LANES, 1, tm, LANES), lambda b, i: (0, b, i, 0)),
                jax.ShapeDtypeStruct((S5_CH // LANES, bsz, t, LANES), F32))
    outs = one_dir + one_dir + [tok(AB_V, BF16), tok(AB_V, BF16), u_blocks]
    return pl.pallas_call(
        functools.partial(_inproj0_body, nct),
        grid=(bsz, t // tm),
        in_specs=[ctx_spec, lat_spec, mod_spec, _const_spec((1, d))] + [_const_spec(a.shape) for a in consts],
        out_specs=[o[0] for o in outs],
        out_shape=[o[1] for o in outs],
        compiler_params=_params("arbitrary", "arbitrary"),
        name="inproj_gla_s5",
    )(ctx, x, mods, norm_g.reshape(1, d), *consts)


def _rope(acc, cos_ref, sin_ref, o_ref, scale):
    for grp in range(acc.shape[1] // LANES):
        half = grp % 2
        xg = acc[:, grp * LANES:(grp + 1) * LANES]
        cs = cos_ref[:, half * LANES:(half + 1) * LANES]
        sn = sin_ref[:, half * LANES:(half + 1) * LANES]
        out = xg * cs + pltpu.roll(xg, LANES // 2, 1) * sn
        o_ref[0, :, grp * LANES:(grp + 1) * LANES] = (out * scale).astype(o_ref.dtype)


def _moe_residual(x1_ref, g2, yk_ref, w_ref):
    d = x1_ref.shape[2]
    half = d // 2
    y_lo, y_hi = None, None
    for k in range(TOP_K):
        lo, hi = _unpack_rows(yk_ref[k, 0])
        wk = w_ref[0, :, k:k + 1]
        y_lo = lo * wk if y_lo is None else y_lo + lo * wk
        y_hi = hi * wk if y_hi is None else y_hi + hi * wk
    return x1_ref[0, :, 0:half] + g2[:, 0:half] * y_lo, x1_ref[0, :, half:d] + g2[:, half:d] * y_hi


def _inproj1_body(x1_ref, mod0_ref, yk_ref, w_ref, mod_ref, g_ref, cos_ref, sin_ref, wq, wk, wv, wg, ox, oq, ok, ov, og):
    half = x1_ref.shape[2] // 2
    x2_lo, x2_hi = _moe_residual(x1_ref, mod0_ref[0, 0][5:6], yk_ref, w_ref)
    ox[0, :, 0:half] = x2_lo
    ox[0, :, half:2 * half] = x2_hi
    m = mod_ref[0, 0]
    h = _norm_mod(jnp.concatenate([x2_lo, x2_hi], axis=1), g_ref[...], m[0:1], m[1:2]).astype(BF16)
    _rope(_dot(h, wq[...]), cos_ref, sin_ref, oq, 1.0)
    _rope(_dot(h, wk[...]), cos_ref, sin_ref, ok, RET_DK ** -0.5)
    ov[0] = _dot(h, wv[...]).astype(ov.dtype)
    og[0] = _dot(h, wg[...]).astype(og.dtype)


def _inproj1(x1, mods0, yk, w_tok, mods, norm_g, cos_t, sin_t, weights, nct):
    bsz, t, d = x1.shape
    tm = TOKEN_TILE
    x_spec, mod_spec = _tile_specs(nct, d)
    tok = lambda w: pl.BlockSpec((1, tm, w), lambda b, i: (b, i, 0))
    lat = lambda w: pl.BlockSpec((1, tm, w), lambda b, i: (b, jnp.maximum(i - nct, 0), 0))
    tab_spec = pl.BlockSpec((tm, RET_DK), lambda b, i: (i, 0))
    wq, wk, wv, wg = weights
    n_lat = t - nct * tm
    return pl.pallas_call(
        _inproj1_body,
        grid=(bsz, t // tm),
        in_specs=[x_spec, mod_spec, pl.BlockSpec((TOP_K, 1, tm, d // 2), lambda b, i: (0, b, i, 0)), tok(TOP_K),
                  mod_spec, _const_spec((1, d)), tab_spec, tab_spec] + [_const_spec(w.shape) for w in weights],
        out_specs=[lat(d), tok(wq.shape[1]), tok(wk.shape[1]), tok(wv.shape[1]), lat(wg.shape[1])],
        out_shape=[jax.ShapeDtypeStruct((bsz, n_lat, d), F32), jax.ShapeDtypeStruct((bsz, t, wq.shape[1]), BF16),
                   jax.ShapeDtypeStruct((bsz, t, wk.shape[1]), BF16), jax.ShapeDtypeStruct((bsz, t, wv.shape[1]), BF16),
                   jax.ShapeDtypeStruct((bsz, n_lat, wg.shape[1]), BF16)],
        compiler_params=_params("arbitrary", "arbitrary"),
        name="inproj_retention",
    )(x1, mods0, yk.reshape(TOP_K, bsz, t, d // 2), w_tok.reshape(bsz, t, TOP_K), mods, norm_g.reshape(1, d),
      cos_t, sin_t, *weights)


def _backward_chunk(n, n_ctx_chunks, n_chunks):
    return jnp.where(n < n_ctx_chunks, n_ctx_chunks - 1 - n, n_chunks - 1 - (n - n_ctx_chunks))


def _gla_body(qd_f, ki_f, ks_f, ed_f, v_f, qd_b, ki_b, ks_b, ed_b, v_b, hmask_ref, bdmask_ref, o_f, o_b, st_f, st_b):
    c = GLA_CHUNK

    @pl.when(pl.program_id(1) == 0)
    def _():
        st_f[...] = jnp.zeros_like(st_f)
        st_b[...] = jnp.zeros_like(st_b)

    r4 = lax.broadcasted_iota(I32, (GLA_HEADS * c, c), 0) & (c - 1)
    c4 = lax.broadcasted_iota(I32, (GLA_HEADS * c, c), 1)
    dirs = ((qd_f, ki_f, ks_f, ed_f, v_f, o_f, st_f), (qd_b, ki_b, ks_b, ed_b, v_b, o_b, st_b))
    chains = [(bb, d) + dirs[d] for bb in range(qd_f.shape[0]) for d in range(2)]
    scores, inter, grow = [], [], []
    for bb, d, qd_ref, ki_ref, ks_ref, ed_ref, v_ref, o_ref, st_ref in chains:
        q_dec = qd_ref[bb]
        q_heads = jnp.concatenate([q_dec] * GLA_HEADS, axis=0) * hmask_ref[...]
        seen4 = (c4 <= r4) if d == 0 else (c4 >= r4)
        scores.append(jnp.where(seen4, _dot_nt(q_heads, ki_ref[bb]), 0.0).astype(BF16))
        inter.append(_dot_nt(q_dec, st_ref[bb].astype(BF16)))
        grow.append(_dot_tn(v_ref[bb], ks_ref[bb]))
    for (bb, d, qd_ref, ki_ref, ks_ref, ed_ref, v_ref, o_ref, st_ref), sc, o_inter, dst in zip(chains, scores, inter, grow):
        v = v_ref[bb]
        o_intra = jnp.concatenate(
            [_dot(sc[h * c:(h + 1) * c], v[:, h * GLA_DV:(h + 1) * GLA_DV]) for h in range(GLA_HEADS)], axis=1)
        o_ref[bb] = o_intra + o_inter
        st_ref[bb] = st_ref[bb] * ed_ref[bb, 0] + bdmask_ref[...] * dst


def _gla(per_dir, v, n_ctx):
    bsz, t, _ = v.shape
    nc, ncc = t // GLA_CHUNK, n_ctx // GLA_CHUNK
    gb = GLA_BATCH
    fwd = lambda b, n: (b, n, 0)
    bwd = lambda b, n: (b, _backward_chunk(n, ncc, nc), 0)
    hmask = (jnp.arange(AB_QK)[:, None] // GLA_CHUNK == jnp.arange(AB_QK)[None, :] // GLA_DK).astype(BF16)
    bdmask = (jnp.arange(AB_V)[:, None] // GLA_DV == jnp.arange(AB_QK)[None, :] // GLA_DK).astype(F32)

    def specs(idx):
        idx4 = lambda b, n: idx(b, n) + (0,)
        return [pl.BlockSpec((gb, GLA_CHUNK, AB_QK), idx)] * 3 + [pl.BlockSpec((gb, 1, 1, AB_QK), idx4),
                                                                  pl.BlockSpec((gb, GLA_CHUNK, AB_V), idx)]

    return pl.pallas_call(
        _gla_body,
        grid=(bsz // gb, nc),
        in_specs=specs(fwd) + specs(bwd) + [_const_spec(hmask.shape), _const_spec(bdmask.shape)],
        out_specs=[pl.BlockSpec((gb, GLA_CHUNK, AB_V), fwd), pl.BlockSpec((gb, GLA_CHUNK, AB_V), bwd)],
        out_shape=[jax.ShapeDtypeStruct((bsz, t, AB_V), F32)] * 2,
        scratch_shapes=[pltpu.VMEM((gb, AB_V, AB_QK), F32)] * 2,
        compiler_params=_params("arbitrary", "arbitrary"),
        name="gla_scan",
    )(*per_dir[0], v, *per_dir[1], v, hmask, bdmask)


def _cmul(x, y):
    return x[0] * y[0] - x[1] * y[1], x[0] * y[1] + x[1] * y[0]


def _s5_operators(lam_re, lam_im, log_step, b_re, b_im, c_re, c_im):
    ln = S5_CHUNK
    step = jnp.exp(log_step.astype(F32))[..., None]
    lam_re, lam_im = lam_re.astype(F32), lam_im.astype(F32)
    mag = jnp.exp(lam_re * step)
    a = (mag * jnp.cos(lam_im * step), mag * jnp.sin(lam_im * step))
    den = lam_re * lam_re + lam_im * lam_im
    f_re = ((a[0] - 1.0) * lam_re + a[1] * lam_im) / den
    f_im = (a[1] * lam_re - (a[0] - 1.0) * lam_im) / den
    bt_re, bt_im = b_re.transpose(0, 2, 1), b_im.transpose(0, 2, 1)
    bb = _cmul((f_re[:, :, None, :], f_im[:, :, None, :]), (bt_re, bt_im))
    bbt = jnp.concatenate([bb[0], -bb[1]], axis=-1)
    pw = (a[0][:, :, None, :], a[1][:, :, None, :])
    while pw[0].shape[2] < ln:
        top = (pw[0][:, :, -1:, :], pw[1][:, :, -1:, :])
        nxt = _cmul(top, pw)
        pw = (jnp.concatenate([pw[0], nxt[0]], axis=2), jnp.concatenate([pw[1], nxt[1]], axis=2))
    pw = (jnp.concatenate([jnp.ones_like(pw[0][:, :, :1]), pw[0]], axis=2),
          jnp.concatenate([jnp.zeros_like(pw[1][:, :, :1]), pw[1]], axis=2))
    ca = _cmul((c_re[:, :, None], c_im[:, :, None]), (pw[0][:, :, :, None, :], pw[1][:, :, :, None, :]))
    by_dir = lambda arr, lo, flip_d: jnp.stack([jnp.flip(arr[d, :, lo:lo + ln], axis=1) if d == flip_d
                                                else arr[d, :, lo:lo + ln] for d in range(2)])
    rows = lambda arr: arr.reshape(2, S5_GROUPS, ln * S5_GROUP, 2 * S5_P)
    cab = rows(by_dir(jnp.concatenate([ca[0], ca[1]], axis=-1), 0, 1))
    cab2 = rows(by_dir(jnp.concatenate([ca[0], -ca[1]], axis=-1), 1, 1)).astype(BF16)
    pwx = by_dir(jnp.concatenate([pw[0], pw[1]], axis=-1), 0, 0)
    lr, li = pw[0][:, :, ln], pw[1][:, :, ln]
    ac_rows = [jnp.concatenate([lr, lr], -1), jnp.concatenate([-li, li], -1), jnp.concatenate([li, -li], -1)]
    ac = jnp.stack(ac_rows + [jnp.zeros_like(ac_rows[0])] * 5, axis=2)
    return cab, cab2, bbt, pwx, ac


def _s5_group_operators(gg, cab_ref, bbt_ref, pwx_ref, tz, wx):
    ln, ch, p = S5_CHUNK, S5_GROUP, S5_P
    lane = lax.broadcasted_iota(I32, (ch, ln * ch), 1)
    for d in range(2):
        kern = _dot3(bbt_ref[d, gg], cab_ref[d, gg], dot=_dot_nt)
        bt = bbt_ref[d, gg]
        b_re, b_im = bt[:, 0:p], -bt[:, p:2 * p]
        for j in range(ln):
            if d == 0:
                blk = jnp.where(lane >= j * ch, kern if j == 0 else pltpu.roll(kern, j * ch, 1), 0.0)
            else:
                blk = jnp.where(lane < (j + 1) * ch, kern if j == ln - 1 else pltpu.roll(kern, (j + 1) * ch, 1), 0.0)
            tz[gg, d, j * ch:(j + 1) * ch, :] = blk.astype(BF16)
            pr, pi = pwx_ref[d, gg, j:j + 1, 0:p], pwx_ref[d, gg, j:j + 1, p:2 * p]
            x_re, x_im = pr * b_re - pi * b_im, pr * b_im + pi * b_re
            wx[gg, d, j * ch:(j + 1) * ch, :] = jnp.concatenate([x_re, x_im, x_im, x_re], axis=1).astype(BF16)


def _s5_placement(pall):
    rows, cols = pall.shape[1], pall.shape[2]
    row = lax.broadcasted_iota(I32, (rows, cols), 0)
    col = lax.broadcasted_iota(I32, (rows, cols), 1)
    same_token = (row // LANES) == (col // S5_GROUP)
    for g8 in range(pall.shape[0]):
        pall[g8] = jnp.where(same_token & ((row % LANES) == g8 * S5_GROUP + (col % S5_GROUP)), 1.0, 0.0).astype(BF16)


def _first_step():
    return (pl.program_id(0) == 0) & (pl.program_id(1) == 0)


def _s5_fold_body(ncs, u_ref, o_ref, pall, ucat):
    @pl.when(_first_step())
    def _():
        _s5_placement(pall)

    for b in range(u_ref.shape[1]):
        for j in range(S5_CHUNK):
            ucat[b * ncs:(b + 1) * ncs, j * LANES:(j + 1) * LANES] = u_ref[0, b, pl.ds(j, ncs, stride=S5_CHUNK), :].astype(BF16)
    for g8 in range(pall.shape[0]):
        o_ref[g8] = _dot(ucat[...], pall[g8]).astype(BF16)


def _s5_unfold_body(ncs, y_ref, o_ref, pall):
    @pl.when(_first_step())
    def _():
        _s5_placement(pall)

    def token_pair(i2, carry):
        r0 = pl.multiple_of(i2 * 2 * LANES, 2 * LANES)
        acc = _dot_nt(y_ref[0], pall[0, pl.ds(r0, 2 * LANES), :])
        for g8 in range(1, pall.shape[0]):
            acc = acc + _dot_nt(y_ref[g8], pall[g8, pl.ds(r0, 2 * LANES), :])
        for b in range(o_ref.shape[1]):
            for par in range(2):
                o_ref[0, b, pl.ds(2 * i2 + par, ncs, stride=S5_CHUNK), :] = (
                    acc[b * ncs:(b + 1) * ncs, par * LANES:(par + 1) * LANES])
        return carry

    lax.fori_loop(0, S5_CHUNK // 2, token_pair, 0)


def _s5_body(ncs_ctx, ncs, rows, u_ref, cab_ref, cab2_ref, bbt_ref, pwx_ref, ac_ref, y_ref, tz, wx, *vecs):
    half = 2 * S5_P
    n_groups = u_ref.shape[0]
    groups = [vecs[6 * gg:6 * gg + 6] for gg in range(n_groups)]
    for gg, (xx_f, xs_f, xx_b, xs_b, _, _) in enumerate(groups):
        _s5_group_operators(gg, cab_ref, bbt_ref, pwx_ref, tz, wx)
        for d, (xx, xs) in enumerate(((xx_f, xs_f), (xx_b, xs_b))):
            r = _dot(u_ref[gg], wx[gg, d])
            xx[...] = r[:, :half]
            xs[...] = r[:, half:]

    def advance(ac, s, s_sw, x, x_sw):
        return ac[0:1] * s + ac[1:2] * s_sw + x, ac[0:1] * s_sw + ac[2:3] * s + x_sw

    def step(n, carry):
        at_f = pl.ds(n, rows, stride=ncs)
        at_b = pl.ds(_backward_chunk(n, ncs_ctx, ncs), rows, stride=ncs)
        out = []
        for gg, (xx_f, xs_f, xx_b, xs_b, sin_f, sin_b) in enumerate(groups):
            s_f, sw_f, s_b, sw_b = carry[4 * gg:4 * gg + 4]
            sin_f[at_f, :] = s_f
            sin_b[at_b, :] = s_b
            out += advance(ac_ref[0, gg], s_f, sw_f, xx_f[at_f, :], xs_f[at_f, :])
            out += advance(ac_ref[1, gg], s_b, sw_b, xx_b[at_b, :], xs_b[at_b, :])
        return tuple(out)

    zero = jnp.zeros((rows, half), F32)
    lax.fori_loop(0, ncs, step, (zero,) * (4 * n_groups))
    for gg, (_, _, _, _, sin_f, sin_b) in enumerate(groups):
        u = u_ref[gg]
        y_ref[gg] = (_dot(u, tz[gg, 0]) + _dot(u, tz[gg, 1]) + _dot_nt(sin_f[...].astype(BF16), cab2_ref[0, gg])
                     + _dot_nt(sin_b[...].astype(BF16), cab2_ref[1, gg])).astype(BF16)


def _s5(u4, ops, n_ctx):
    nq, bsz, t, _ = u4.shape
    ln, lanes = S5_CHUNK, S5_CHUNK * S5_GROUP
    gq = S5_GROUPS // nq
    ncs, ncs_ctx = t // ln, n_ctx // ln
    m = ncs * bsz
    hb = S5_FOLD_BATCH
    tok_spec = pl.BlockSpec((1, hb, t, LANES), lambda q, h: (q, h, 0, 0))
    grp_spec = pl.BlockSpec((gq, hb * ncs, lanes), lambda q, h: (q, h, 0))
    pall = pltpu.VMEM((gq, ln * LANES, lanes), BF16)
    ug = pl.pallas_call(
        functools.partial(_s5_fold_body, ncs),
        grid=(nq, bsz // hb),
        in_specs=[tok_spec],
        out_specs=grp_spec,
        out_shape=jax.ShapeDtypeStruct((S5_GROUPS, m, lanes), BF16),
        scratch_shapes=[pall, pltpu.VMEM((hb * ncs, ln * LANES), BF16)],
        compiler_params=_params("arbitrary", "arbitrary"),
        name="s5_fold",
    )(u4)
    sg = S5_SCAN_GROUPS
    dir_spec = lambda arr: pl.BlockSpec((2, sg) + arr.shape[2:], lambda g: (0, g, 0, 0))
    yg = pl.pallas_call(
        functools.partial(_s5_body, ncs_ctx, ncs, bsz),
        grid=(S5_GROUPS // sg,),
        in_specs=[pl.BlockSpec((sg, m, lanes), lambda g: (g, 0, 0))] + [dir_spec(arr) for arr in ops],
        out_specs=pl.BlockSpec((sg, m, lanes), lambda g: (g, 0, 0)),
        out_shape=jax.ShapeDtypeStruct((S5_GROUPS, m, lanes), BF16),
        scratch_shapes=[pltpu.VMEM((sg, 2, lanes, lanes), BF16)] * 2 + [pltpu.VMEM((m, 2 * S5_P), F32)] * (6 * sg),
        compiler_params=_params("arbitrary"),
        name="s5_scan",
    )(ug, *ops)
    return pl.pallas_call(
        functools.partial(_s5_unfold_body, ncs),
        grid=(nq, bsz // hb),
        in_specs=[grp_spec],
        out_specs=tok_spec,
        out_shape=jax.ShapeDtypeStruct(u4.shape, F32),
        scratch_shapes=[pall],
        compiler_params=_params("arbitrary", "arbitrary"),
        name="s5_unfold",
    )(yg)


def _ret_body(q_f, k_f, v_f, q_b, k_b, v_b, dmat_ref, rsc_ref, csc_ref, gam_ref, o_f, o_b, st_f, st_b):
    @pl.when(pl.program_id(1) == 0)
    def _():
        st_f[...] = jnp.zeros_like(st_f)
        st_b[...] = jnp.zeros_like(st_b)

    dirs = ((q_f, k_f, v_f, o_f, st_f), (q_b, k_b, v_b, o_b, st_b))
    for d, (q_ref, k_ref, v_ref, o_ref, st_ref) in enumerate(dirs):
        for h in range(RET_HEADS):
            qh = q_ref[0, :, h * RET_DK:(h + 1) * RET_DK]
            kh = k_ref[0, :, h * RET_DK:(h + 1) * RET_DK]
            vh = v_ref[0, :, h * RET_DV:(h + 1) * RET_DV]
            st = st_ref[h]
            scores = (_dot_nt(qh, kh) * dmat_ref[d, h]).astype(BF16)
            o = _dot(scores, vh) + rsc_ref[d, h] * _dot(qh, st.astype(BF16))
            o_ref[0, :, h * RET_DV:(h + 1) * RET_DV] = o.astype(o_ref.dtype)
            k_state = (kh.astype(F32) * csc_ref[d, h]).astype(BF16)
            st_ref[h] = st * gam_ref[d, h] + _dot_tn(k_state, vh)


def _retention(q, k, v, decay_logit, n_ctx):
    bsz, t, _ = q.shape
    c = RET_CHUNK
    nc, ncc = t // c, n_ctx // c
    nl = nc - ncc
    log_gamma = jax.nn.log_sigmoid(decay_logit.astype(F32))[:, :, None, None]
    i = jnp.arange(c, dtype=F32)
    lag = i[:, None] - i[None, :]
    lag = jnp.stack([lag, -lag])[:, None]
    dmat = jnp.where(lag >= 0, jnp.exp(log_gamma * jnp.maximum(lag, 0.0)), 0.0)
    done = jnp.stack([i + 1.0, c - i])[:, None, :, None]
    rsc = jnp.exp(log_gamma * done)
    csc = jnp.exp(log_gamma * (c - done))
    gam = jnp.exp(log_gamma[:, :, 0, 0] * c)
    fwd = lambda b, n: (b, n, 0)
    bwd = lambda b, n: (b, _backward_chunk(n, ncc, nc), 0)
    o_fwd = lambda b, n: (b, jnp.maximum(n - ncc, 0), 0)
    o_bwd = lambda b, n: (b, nl - 1 - jnp.maximum(n - ncc, 0), 0)

    def specs(idx):
        return [pl.BlockSpec((1, c, RET_QK), idx), pl.BlockSpec((1, c, RET_QK), idx), pl.BlockSpec((1, c, RET_MIX), idx)]

    return pl.pallas_call(
        _ret_body,
        grid=(bsz, nc),
        in_specs=specs(fwd) + specs(bwd) + [_const_spec(dmat.shape), _const_spec(rsc.shape), _const_spec(csc.shape),
                                            pl.BlockSpec(memory_space=pltpu.SMEM)],
        out_specs=[pl.BlockSpec((1, c, RET_MIX), o_fwd), pl.BlockSpec((1, c, RET_MIX), o_bwd)],
        out_shape=[jax.ShapeDtypeStruct((bsz, nl * c, RET_MIX), BF16)] * 2,
        scratch_shapes=[pltpu.VMEM((RET_HEADS, RET_DK, RET_DV), F32)] * 2,
        compiler_params=_params("arbitrary", "arbitrary"),
        name="retention_scan",
    )(q, k, v, q, k, v, dmat, rsc, csc, gam)


def _zero_counts_at_start(cnt_ref):
    @pl.when(_first_step())
    def _():
        cnt_ref[...] = jnp.zeros_like(cnt_ref)


def _route(x, mixed, mod, n2g_ref, wr_ref, br_ref, x1_ref, h2_ref, e_ref, w_ref, r_ref, cnt_ref):
    tm = x.shape[0]
    x1 = x + mod[2:3] * mixed
    x1_ref[0] = x1
    h2 = _norm_mod(x1, n2g_ref[...], mod[3:4], mod[4:5])
    h2_ref[0] = _pack_rows(h2)
    logits = _dot3(wr_ref[...], h2, dot=_dot_nt) + br_ref[...]
    ie = lax.broadcasted_iota(I32, logits.shape, 0)
    tops, picks = [], []
    for _ in range(TOP_K):
        mx = jnp.max(logits, axis=0, keepdims=True)
        pick = jnp.min(jnp.where(logits == mx, ie, N_EXPERTS), axis=0, keepdims=True)
        tops.append(mx)
        picks.append(pick)
        logits = jnp.where(ie == pick, -jnp.inf, logits)
    ex = [jnp.exp(tk - tops[0]) for tk in tops]
    den = ex[0] + ex[1] + ex[2] + ex[3]
    for kk in range(TOP_K):
        w_ref[0, kk:kk + 1, :] = ex[kk] / den
        e_ref[0, kk:kk + 1, :] = picks[kk]

    earlier = (lax.broadcasted_iota(I32, (tm, tm), 0) < lax.broadcasted_iota(I32, (tm, tm), 1))
    earlier = jnp.where(earlier, 1.0, 0.0).astype(BF16)
    run = cnt_ref[:, 0:1]
    for kk, pick in enumerate(picks):
        onehot = jnp.where(ie == pick, 1.0, 0.0)
        before = _dot(onehot.astype(BF16), earlier) + run
        r_ref[0, kk:kk + 1, :] = jnp.sum(onehot * before, axis=0, keepdims=True).astype(I32)
        run = run + jnp.sum(onehot, axis=1, keepdims=True)
    cnt_ref[...] = jnp.broadcast_to(run, cnt_ref.shape)


def _mix0_body(nct, ctx_ref, lat_ref, mod_ref, of_ref, ob_ref, g_ref, ys_ref, u_ref, gng_ref, dsk_ref, gluw_ref,
               glub_ref, wo_ref, n2g_ref, wr_ref, br_ref, x1_ref, h2_ref, e_ref, w_ref, r_ref, cnt_ref):
    _zero_counts_at_start(cnt_ref)
    o = of_ref[0] + ob_ref[0]
    heads = []
    for h in range(GLA_HEADS):
        oh = o[:, h * GLA_DV:(h + 1) * GLA_DV]
        heads.append(oh * lax.rsqrt(jnp.mean(oh * oh, axis=-1, keepdims=True) + EPS))
    gla = jnp.concatenate(heads, axis=1) * gng_ref[...] * _silu(g_ref[0].astype(F32))
    lane_blocks = lambda ref: jnp.concatenate([ref[qb, 0] for qb in range(ref.shape[0])], axis=1)
    y = jax.nn.gelu(lane_blocks(ys_ref) + dsk_ref[...] * lane_blocks(u_ref))
    y = y * jax.nn.sigmoid(_dot(y.astype(BF16), gluw_ref[...]) + glub_ref[...])
    mixed = _dot(gla.astype(BF16), wo_ref[0:AB_V]) + _dot(y.astype(BF16), wo_ref[AB_V:AB_V + S5_CH])
    _route(_stream_tile(nct, ctx_ref, lat_ref), mixed, mod_ref[0, 0], n2g_ref, wr_ref, br_ref,
           x1_ref, h2_ref, e_ref, w_ref, r_ref, cnt_ref)


def _mix1_body(x_ref, mod_ref, of_ref, ob_ref, g_ref, ng_ref, wo_ref, n2g_ref, wr_ref, br_ref,
               x1_ref, h2_ref, e_ref, w_ref, r_ref, cnt_ref):
    _zero_counts_at_start(cnt_ref)
    mixed = None
    for h in range(RET_HEADS):
        sl = slice(h * RET_DV, (h + 1) * RET_DV)
        oh = of_ref[0, :, sl].astype(F32) + ob_ref[0, :, sl].astype(F32)
        mu = jnp.mean(oh, axis=-1, keepdims=True)
        cen = oh - mu
        var = jnp.mean(cen * cen, axis=-1, keepdims=True)
        gated = cen * lax.rsqrt(var + EPS) * ng_ref[:, sl] * _silu(g_ref[0, :, sl].astype(F32))
        part = _dot(gated.astype(BF16), wo_ref[sl])
        mixed = part if mixed is None else mixed + part
    _route(x_ref[0], mixed, mod_ref[0, 0], n2g_ref, wr_ref, br_ref, x1_ref, h2_ref, e_ref, w_ref, r_ref, cnt_ref)


def _mix_call(body, name, stream, mods, tiles, acts, consts, norm2_g, w_router, b_router, n_tok, seg_tile0,
              tm=TOKEN_TILE):
    bsz, _, d = stream[-1].shape
    off = lambda b, i: (b, i + seg_tile0, 0)
    loc = lambda b, i: (b, i, 0)
    ntl = bsz * tiles
    flat = lambda b, i: (b * tiles + i, 0, 0)
    in_specs = list(_split_specs(n_tok, d)) if len(stream) == 2 else [pl.BlockSpec((1, tm, d), off)]
    in_specs.append(pl.BlockSpec((1, 1, 6, d), lambda b, i: (b, ((i + seg_tile0) >= n_tok).astype(I32), 0, 0)))
    args = list(stream) + [mods]
    for arr, offset in acts:
        if arr.ndim == 4:
            in_specs.append(pl.BlockSpec((arr.shape[0], 1, tm, arr.shape[3]), lambda b, i: (0, b, i, 0)))
        else:
            in_specs.append(pl.BlockSpec((1, tm, arr.shape[2]), off if offset else loc))
        args.append(arr)
    tail = list(consts) + [norm2_g.reshape(1, d), w_router.T, b_router.reshape(N_EXPERTS, 1)]
    in_specs += [_const_spec(a.shape) for a in tail]
    args += tail
    tok_out = pl.BlockSpec((1, TOP_K, tm), flat)
    return pl.pallas_call(
        body,
        grid=(bsz, tiles),
        in_specs=in_specs,
        out_specs=[pl.BlockSpec((1, tm, d), loc), pl.BlockSpec((1, tm, d // 2), loc), tok_out, tok_out, tok_out,
                   _const_spec((N_EXPERTS, LANES))],
        out_shape=[jax.ShapeDtypeStruct((bsz, tiles * tm, d), F32), jax.ShapeDtypeStruct((bsz, tiles * tm, d // 2), U32),
                   jax.ShapeDtypeStruct((ntl, TOP_K, tm), I32), jax.ShapeDtypeStruct((ntl, TOP_K, tm), F32),
                   jax.ShapeDtypeStruct((ntl, TOP_K, tm), I32), jax.ShapeDtypeStruct((N_EXPERTS, LANES), F32)],
        compiler_params=_params("arbitrary", "arbitrary"),
        name=name,
    )(*args)


def _cast_rows(src_ref, dst_ref, rows):
    def chunk(j, carry):
        r = pl.multiple_of(j * rows, rows)
        dst_ref[pl.ds(r, rows), :] = src_ref[0, 0, pl.ds(r, rows), :].astype(BF16)
        return carry

    lax.fori_loop(0, dst_ref.shape[0] // rows, chunk, 0)


def _expert_body(be_ref, nu_ref, x_ref, wgu_ref, bgu_ref, wd_ref, bd_ref, o_ref, wgu_bf, wd_bf):
    i = pl.program_id(0)
    live = i < nu_ref[0]
    new_expert = (i == 0) | (be_ref[i] != be_ref[jnp.maximum(i - 1, 0)])

    @pl.when(live & new_expert)
    def _():
        _cast_rows(wgu_ref, wgu_bf, 128)
        _cast_rows(wd_ref, wd_bf, 128)

    @pl.when(live)
    def _():
        x_lo, x_hi = _unpack_rows(x_ref[...])
        half = x_lo.shape[1]
        gu = (_dot(x_lo.astype(BF16), wgu_bf[0:half]) + _dot(x_hi.astype(BF16), wgu_bf[half:2 * half])
              + bgu_ref[0, 0])
        gate = jnp.minimum(gu[:, :D_FF], SWIGLU_LIMIT)
        lin = jnp.clip(gu[:, D_FF:], -SWIGLU_LIMIT, SWIGLU_LIMIT)
        act = gate * jax.nn.sigmoid(SWIGLU_ALPHA * gate) * (lin + 1.0)
        y = _dot(act.astype(BF16), wd_bf[...]) + bd_ref[0, 0]
        o_ref[...] = _pack_rows(y)

    @pl.when(i >= nu_ref[0])
    def _():
        o_ref[...] = jnp.zeros_like(o_ref)


def _experts(xb, block_e, n_used, layer, w_gu, b_gu, w_down, b_down):
    n_slots, half = xb.shape
    d = 2 * half
    n_blocks = n_slots // MOE_BLOCK
    depth = w_gu.shape[0]
    by_expert = lambda i, be, nu: (layer, be[i], 0, 0)
    return pl.pallas_call(
        _expert_body,
        grid_spec=pltpu.PrefetchScalarGridSpec(
            num_scalar_prefetch=2,
            grid=(n_blocks,),
            in_specs=[pl.BlockSpec((MOE_BLOCK, half), lambda i, be, nu: (i, 0)),
                      pl.BlockSpec((1, 1, d, 2 * D_FF), by_expert), pl.BlockSpec((1, 1, 1, 2 * D_FF), by_expert),
                      pl.BlockSpec((1, 1, D_FF, d), by_expert), pl.BlockSpec((1, 1, 1, d), by_expert)],
            out_specs=pl.BlockSpec((MOE_BLOCK, half), lambda i, be, nu: (i, 0)),
            scratch_shapes=[pltpu.VMEM((d, 2 * D_FF), BF16), pltpu.VMEM((D_FF, d), BF16)]),
        out_shape=jax.ShapeDtypeStruct((n_slots, half), U32),
        compiler_params=_params("arbitrary"),
        name="moe_experts",
    )(block_e, n_used, xb, w_gu, b_gu.reshape(depth, N_EXPERTS, 1, 2 * D_FF), w_down,
      b_down.reshape(depth, N_EXPERTS, 1, d))


def _combine_body(x1_ref, mod_ref, yk_ref, w_ref, fg_ref, o_ref):
    d = x1_ref.shape[2]
    half = d // 2
    x2_lo, x2_hi = _moe_residual(x1_ref, mod_ref[0, 0][5:6], yk_ref, w_ref)
    ms = (jnp.sum(x2_lo * x2_lo, axis=-1, keepdims=True) + jnp.sum(x2_hi * x2_hi, axis=-1, keepdims=True)) / d
    r = lax.rsqrt(ms + EPS)
    o_ref[0, :, 0:half] = x2_lo * r * fg_ref[:, 0:half]
    o_ref[0, :, half:d] = x2_hi * r * fg_ref[:, half:d]


def _combine(x1, mods, yk, w_tok, seg_tile0, n_tok, final_g, tm):
    bsz, t, d = x1.shape
    loc = lambda b, i: (b, i, 0)
    return pl.pallas_call(
        _combine_body,
        grid=(bsz, t // tm),
        in_specs=[pl.BlockSpec((1, tm, d), loc),
                  pl.BlockSpec((1, 1, 6, d), lambda b, i: (b, ((i + seg_tile0) >= n_tok).astype(I32), 0, 0)),
                  pl.BlockSpec((TOP_K, 1, tm, d // 2), lambda b, i: (0, b, i, 0)),
                  pl.BlockSpec((1, tm, TOP_K), loc), _const_spec((1, d))],
        out_specs=pl.BlockSpec((1, tm, d), loc),
        out_shape=jax.ShapeDtypeStruct((bsz, t, d), F32),
        compiler_params=_params("arbitrary", "arbitrary"),
        name="moe_combine",
    )(x1, mods, yk.reshape(TOP_K, bsz, t, d // 2), w_tok.reshape(bsz, t, TOP_K), final_g.reshape(1, d))


def _sc_mesh():
    return plsc.VectorSubcoreMesh(core_axis_name="core", subcore_axis_name="subcore",
                                  num_cores=SC_CORES, num_subcores=SC_SUBCORES)


def _sc_worker_base(per_worker):
    return (lax.axis_index("subcore") * SC_CORES + lax.axis_index("core")) * per_worker


def _sc_dispatch(rows, dest, n_slots):
    n, w = rows.shape
    per_worker = n // SC_WORKERS
    assert per_worker * SC_WORKERS == n and per_worker % SC_CHUNK == 0

    @functools.partial(
        pl.kernel, mesh=_sc_mesh(), out_type=jax.ShapeDtypeStruct((n_slots, w), rows.dtype),
        scratch_types=[pltpu.VMEM((SC_CHUNK,), I32)] * TOP_K + [pltpu.VMEM((SC_CHUNK, w), rows.dtype),
                                                                pltpu.SemaphoreType.DMA],
        name="moe_dispatch")
    def scatter_rows(rows_hbm, dest_hbm, out_hbm, *scratch):
        idx_refs, buf, sem = scratch[:TOP_K], scratch[TOP_K], scratch[TOP_K + 1]
        base0 = _sc_worker_base(per_worker)

        @pl.loop(0, per_worker // SC_CHUNK)
        def _(j):
            base = base0 + j * SC_CHUNK
            pltpu.sync_copy(rows_hbm.at[pl.ds(base, SC_CHUNK)], buf)
            for k, idx in enumerate(idx_refs):
                pltpu.sync_copy(dest_hbm.at[pl.ds(k * n + base, SC_CHUNK)], idx)
            copies = [pltpu.make_async_copy(buf, out_hbm.at[idx], sem) for idx in idx_refs]
            for cp in copies:
                cp.start()
            for cp in copies:
                cp.wait()

    return scatter_rows(rows, dest)


def _sc_gather(table, idx):
    n = idx.shape[0]
    w = table.shape[1]
    per_worker = n // SC_WORKERS
    n_chunks = per_worker // SC_CHUNK
    assert per_worker * SC_WORKERS == n and n_chunks * SC_CHUNK == per_worker and n_chunks % 2 == 0

    @functools.partial(
        pl.kernel, mesh=_sc_mesh(), out_type=jax.ShapeDtypeStruct((n, w), table.dtype),
        scratch_types=([pltpu.VMEM((SC_CHUNK,), I32)] * 2 + [pltpu.VMEM((SC_CHUNK, w), table.dtype)] * 2
                       + [pltpu.SemaphoreType.DMA] * 4),
        name="moe_gather")
    def gather_rows(table_hbm, idx_hbm, out_hbm, idx0, idx1, buf0, buf1, gsem0, gsem1, wsem0, wsem1):
        base0 = _sc_worker_base(per_worker)

        def gather_copy(idx_v, buf, sem):
            return pltpu.make_async_copy(table_hbm.at[idx_v], buf, sem)

        def write_copy(j, buf, sem):
            return pltpu.make_async_copy(buf, out_hbm.at[pl.ds(base0 + j * SC_CHUNK, SC_CHUNK)], sem)

        def start_gather(j, idx_v, buf, sem):
            pltpu.sync_copy(idx_hbm.at[pl.ds(base0 + j * SC_CHUNK, SC_CHUNK)], idx_v)
            gather_copy(idx_v, buf, sem).start()

        start_gather(0, idx0, buf0, gsem0)

        @pl.loop(0, n_chunks, step=2)
        def _(j):
            @pl.when(j > 0)
            def _():
                write_copy(j - 1, buf1, wsem1).wait()
            start_gather(j + 1, idx1, buf1, gsem1)
            gather_copy(idx0, buf0, gsem0).wait()
            write_copy(j, buf0, wsem0).start()

            @pl.when(j + 2 < n_chunks)
            def _():
                write_copy(j, buf0, wsem0).wait()
                start_gather(j + 2, idx0, buf0, gsem0)
            gather_copy(idx1, buf1, gsem1).wait()
            write_copy(j + 1, buf1, wsem1).start()

        write_copy(n_chunks - 2, buf0, wsem0).wait()
        write_copy(n_chunks - 1, buf1, wsem1).wait()

    return gather_rows(table, idx)


def _moe(h2, e_tl, w_tl, r_tl, cnt, layer, w_gu, b_gu, w_down, b_down):
    bsz, t, half = h2.shape
    n = bsz * t
    flat = lambda a: a.transpose(1, 0, 2).reshape(TOP_K, n)
    e_k, w_k, r_k = flat(e_tl), flat(w_tl), flat(r_tl)
    counts = cnt[:, 0].astype(I32)
    padded = (counts + MOE_BLOCK - 1) // MOE_BLOCK * MOE_BLOCK
    pad_end = jnp.cumsum(padded)
    pad_start = pad_end - padded
    n_blocks = (n * TOP_K + MOE_BLOCK - 1) // MOE_BLOCK + N_EXPERTS
    block_start = jnp.arange(n_blocks, dtype=I32) * MOE_BLOCK
    block_e = jnp.minimum(jnp.sum((pad_end[None, :] <= block_start[:, None]).astype(I32), axis=1), N_EXPERTS - 1)
    n_used = (pad_end[-1:] // MOE_BLOCK).astype(I32)
    start_k = jnp.sum(jnp.where(e_k[..., None] == jnp.arange(N_EXPERTS, dtype=I32), pad_start, 0), axis=-1)
    dest = (start_k + r_k).reshape(TOP_K * n)
    xb = _sc_dispatch(h2.reshape(n, half), dest, n_blocks * MOE_BLOCK)
    yb = _experts(xb, block_e, n_used, layer, w_gu, b_gu, w_down, b_down)
    return _sc_gather(yb, dest), w_k.T


def _rope_tables(n_ctx, n_lat):
    n_freq = RET_DK // 4
    inv_freq = ROPE_BASE ** (-jnp.arange(n_freq, dtype=F32) / n_freq)
    pos = jnp.arange(n_lat, dtype=I32)
    cos, sin = [], []
    for p in (pos // GRID_W, pos % GRID_W):
        ang = p.astype(F32)[:, None] * inv_freq
        cos += [jnp.cos(ang), jnp.cos(ang)]
        sin += [-jnp.sin(ang), jnp.sin(ang)]
    cos, sin = jnp.concatenate(cos, axis=1), jnp.concatenate(sin, axis=1)
    return (jnp.concatenate([jnp.ones((n_ctx, RET_DK), F32), cos], axis=0),
            jnp.concatenate([jnp.zeros((n_ctx, RET_DK), F32), sin], axis=0))


def kernel(x, c, ctx, c_ctx, ada_w, ada_b, norm1_g, norm2_g, ab_w_in, ab_w_out, gla_wa, gla_ba, gla_norm_g, s5_lam_re, s5_lam_im, s5_log_step, s5_b_re, s5_b_im, s5_c_re, s5_c_im, s5_d, s5_glu_w, s5_glu_b, ret_w_in, ret_w_out, ret_decay_logit, ret_norm_g, moe_w_router, moe_b_router, moe_w_gu, moe_b_gu, moe_w_down, moe_b_down, final_norm_g):
    bsz, n_lat, d = x.shape
    n_ctx = ctx.shape[1]
    depth = ada_w.shape[0]
    assert depth == 2 and d == D_MODEL and bsz == 8, "kernels are laid out for the stated problem shape"
    assert n_ctx % TOKEN_TILE == 0 and n_lat % LATENT_TILE == 0 and n_lat % GRID_W == 0
    t = n_ctx + n_lat
    nct = n_ctx // TOKEN_TILE

    cvec = jnp.zeros((16, d), F32).at[:bsz].set(c).at[bsz].set(c_ctx)
    mod = _ada_mod(cvec, ada_w, ada_b).reshape(depth, 16, 6, d)
    mods = [jnp.stack([jnp.broadcast_to(mod[l, bsz], (bsz, 6, d)), mod[l, :bsz]], axis=1) for l in range(depth)]

    w_in = ab_w_in[0].astype(BF16)
    cuts = [0, AB_QK, 2 * AB_QK, 2 * AB_QK + AB_V, 2 * AB_QK + 2 * AB_V, 2 * AB_QK + 2 * AB_V + 2 * GLA_RANK,
            w_in.shape[1]]
    pieces = [w_in[:, a:b] for a, b in zip(cuts[:-1], cuts[1:])]
    wa_pad = jnp.zeros((2, 2 * GLA_RANK, AB_QK), F32)
    wa_pad = wa_pad.at[0, :GLA_RANK].set(gla_wa[0, 0]).at[1, GLA_RANK:].set(gla_wa[0, 1])
    outs = _inproj0(ctx, x, mods[0], norm1_g[0], pieces, wa_pad, gla_ba[0].reshape(2, 1, AB_QK), nct)
    v, g, u = outs[8:]
    o_f, o_b = _gla((outs[0:4], outs[4:8]), v, n_ctx)
    ops = _s5_operators(s5_lam_re[0], s5_lam_im[0], s5_log_step[0], s5_b_re[0], s5_b_im[0], s5_c_re[0], s5_c_im[0])
    ys = _s5(u, ops, n_ctx)
    consts = [jnp.tile(gla_norm_g[0], GLA_HEADS).reshape(1, AB_V), s5_d[0].reshape(1, S5_CH),
              s5_glu_w[0].astype(BF16), s5_glu_b[0].reshape(1, S5_CH), ab_w_out[0].astype(BF16)]
    x1, h2, e_tl, w_tl, r_tl, cnt = _mix_call(
        functools.partial(_mix0_body, nct), "mix_gla_s5", (ctx, x), mods[0], t // TOKEN_TILE,
        [(o_f, False), (o_b, False), (g, False), (ys, False), (u, False)], consts,
        norm2_g[0], moe_w_router[0], moe_b_router[0], nct, 0)
    yk, w_tok = _moe(h2, e_tl, w_tl, r_tl, cnt, 0, moe_w_gu, moe_b_gu, moe_w_down, moe_b_down)

    w_in = ret_w_in[0].astype(BF16)
    cuts = [0, RET_QK, 2 * RET_QK, 2 * RET_QK + RET_MIX, w_in.shape[1]]
    pieces = [w_in[:, a:b] for a, b in zip(cuts[:-1], cuts[1:])]
    cos_t, sin_t = _rope_tables(n_ctx, n_lat)
    x2, q, k, v, g = _inproj1(x1, mods[0], yk, w_tok, mods[1], norm1_g[1], cos_t, sin_t, pieces, nct)
    o_f, o_b = _retention(q, k, v, ret_decay_logit[0], n_ctx)
    consts = [ret_norm_g[0].reshape(1, RET_MIX), ret_w_out[0].astype(BF16)]
    x1, h2, e_tl, w_tl, r_tl, cnt = _mix_call(
        _mix1_body, "mix_retention", (x2,), mods[1], n_lat // LATENT_TILE,
        [(o_f, False), (o_b, False), (g, False)], consts,
        norm2_g[1], moe_w_router[1], moe_b_router[1], 0, 0, tm=LATENT_TILE)
    yk, w_tok = _moe(h2, e_tl, w_tl, r_tl, cnt, 1, moe_w_gu, moe_b_gu, moe_w_down, moe_b_down)
    return _combine(x1, mods[1], yk, w_tok, 0, 0, final_norm_g, LATENT_TILE)
```

```python
import functools
import math

import jax
import jax.numpy as jnp
from jax import lax
from jax.experimental import pallas as pl
from jax.experimental.pallas import tpu as pltpu
from jax.experimental.pallas import tpu_sc as plsc

F32, BF16, I32, U32 = jnp.float32, jnp.bfloat16, jnp.int32, jnp.uint32

D_MODEL = 1024
GRID_W = 64
EPS = 1e-6
GLA_HEADS, GLA_DK, GLA_DV, GLA_RANK, GLA_TAU, GLA_CHUNK = 4, 64, 128, 16, 16.0, 64
GLA_BATCH = 8
AB_QK, AB_V = GLA_HEADS * GLA_DK, GLA_HEADS * GLA_DV
S5_CH, S5_GROUP, S5_GROUPS, S5_P = 512, 16, 32, 64
S5_CHUNK = 16
S5_FOLD_BATCH = 4
S5_SCAN_GROUPS = 2
RET_HEADS, RET_DK, RET_DV = 4, 256, 512
RET_CHUNK = 256
RET_QK, RET_MIX = RET_HEADS * RET_DK, RET_HEADS * RET_DV
ROPE_BASE = 10000.0
N_EXPERTS, TOP_K, D_FF = 32, 4, 1024
SWIGLU_LIMIT, SWIGLU_ALPHA = 7.0, 1.702
MOE_BLOCK = 512
TOKEN_TILE = 256
LATENT_TILE = 512
TOKEN_TILE_BATCH = 2
ADA_TILE = 768
VMEM_LIMIT = 56 * 1024 * 1024
SC_CORES, SC_SUBCORES = 2, 16
SC_WORKERS = SC_CORES * SC_SUBCORES
SC_CHUNK = 64
LANES = 128

def _params(*sem):
    return pltpu.CompilerParams(dimension_semantics=sem, vmem_limit_bytes=VMEM_LIMIT)


def _dot(a, b):
    return jnp.dot(a, b, preferred_element_type=F32)


def _dot_nt(a, b):
    return lax.dot_general(a, b, (((1,), (1,)), ((), ())), preferred_element_type=F32)


def _dot_tn(a, b):
    return lax.dot_general(a, b, (((0,), (0,)), ((), ())), preferred_element_type=F32)


def _split(a):
    hi = a.astype(BF16)
    return hi, (a - hi.astype(F32)).astype(BF16)


def _dot3(a, b, dot=_dot):
    ah, al = _split(a)
    bh, bl = _split(b)
    return dot(ah, bh) + (dot(ah, bl) + dot(al, bh))


def _pack_rows(x):
    h = x.shape[1] // 2
    lo = lax.bitcast_convert_type(x[:, 0:h].astype(BF16).astype(F32), U32)
    hi = lax.bitcast_convert_type(x[:, h:2 * h].astype(BF16).astype(F32), U32)
    return hi | (lo >> 16)


def _unpack_rows(p):
    lo = lax.bitcast_convert_type(p << 16, F32)
    hi = lax.bitcast_convert_type(p & jnp.uint32(0xFFFF0000), F32)
    return lo, hi


def _silu(x):
    return x * jax.nn.sigmoid(x)


def _norm_mod(x, g, shift, scale):
    r = lax.rsqrt(jnp.mean(x * x, axis=-1, keepdims=True) + EPS)
    return (x * r * g) * (1.0 + scale) + shift


def _const_spec(shape):
    nd = len(shape)
    return pl.BlockSpec(shape, lambda *_: (0,) * nd, pipeline_mode=pl.Buffered(1))


def _ada_body(c_ref, w_ref, b_ref, o_ref):
    o_ref[0] = _dot3(_silu(c_ref[...]), w_ref[0]) + b_ref[0]


def _ada_mod(cvec, ada_w, ada_b):
    depth, d, n6 = ada_w.shape
    rows = cvec.shape[0]
    return pl.pallas_call(
        _ada_body,
        grid=(depth, n6 // ADA_TILE),
        in_specs=[_const_spec((rows, d)),
                  pl.BlockSpec((1, d, ADA_TILE), lambda l, j: (l, 0, j)),
                  pl.BlockSpec((1, 1, ADA_TILE), lambda l, j: (l, 0, j))],
        out_specs=pl.BlockSpec((1, rows, ADA_TILE), lambda l, j: (l, 0, j)),
        out_shape=jax.ShapeDtypeStruct((depth, rows, n6), F32),
        compiler_params=_params("arbitrary", "arbitrary"),
        name="ada_mod",
    )(cvec, ada_w, ada_b.reshape(depth, 1, n6))


def _split_specs(nct, d, tb=1):
    ctx_spec = pl.BlockSpec((tb, TOKEN_TILE, d), lambda b, i: (b, jnp.minimum(i, nct - 1), 0))
    lat_spec = pl.BlockSpec((tb, TOKEN_TILE, d), lambda b, i: (b, jnp.maximum(i - nct, 0), 0))
    return ctx_spec, lat_spec


def _stream_tile(nct, ctx_ref, lat_ref, bb=0):
    return jnp.where(pl.program_id(1) < nct, ctx_ref[bb], lat_ref[bb])


def _normed_rows(nct, ctx_ref, lat_ref, mod_ref, g_ref):
    tiles = []
    for bb in range(mod_ref.shape[0]):
        m = mod_ref[bb, 0]
        tiles.append(_norm_mod(_stream_tile(nct, ctx_ref, lat_ref, bb), g_ref[...], m[0:1], m[1:2]).astype(BF16))
    return tiles[0] if len(tiles) == 1 else jnp.concatenate(tiles, axis=0)


def _store_rows(o_ref, val):
    tm = o_ref.shape[1]
    for bb in range(o_ref.shape[0]):
        o_ref[bb] = val[bb * tm:(bb + 1) * tm].astype(o_ref.dtype)


def _inproj0_body(nct, ctx_ref, lat_ref, mod_ref, g_ref, wq, wk, wv, wg, wlow, wu, wa_ref, ba_ref, tri_ref, ones_ref,
                  qd_f, ki_f, ks_f, ed_f, qd_b, ki_b, ks_b, ed_b, ov, og, ou):
    h = _normed_rows(nct, ctx_ref, lat_ref, mod_ref, g_ref)
    tm = ctx_ref.shape[1]
    low = _dot(h, wlow[...])
    q = _dot(h, wq[...]) * (GLA_DK ** -0.5)
    k = _dot(h, wk[...])
    outs = ((qd_f, ki_f, ks_f, ed_f), (qd_b, ki_b, ks_b, ed_b))
    for d, (qd_ref, ki_ref, ks_ref, ed_ref) in enumerate(outs):
        z = _dot3(low, wa_ref[d]) + ba_ref[d]
        log_a = (jnp.minimum(z, 0.0) - jnp.log1p(jnp.exp(-jnp.abs(z)))) * (1.0 / GLA_TAU)
        la_hi, la_lo = _split(log_a)
        cums, tots = [], []
        for bb in range(qd_ref.shape[0]):
            hi, lo = la_hi[bb * tm:(bb + 1) * tm], la_lo[bb * tm:(bb + 1) * tm]
            cums.append(_dot(tri_ref[d], hi) + _dot(tri_ref[d], lo))
            tots.append(_dot(ones_ref[...], hi) + _dot(ones_ref[...], lo))
        cum = cums[0] if len(cums) == 1 else jnp.concatenate(cums, axis=0)
        tot = tots[0] if len(tots) == 1 else jnp.concatenate(tots, axis=0)
        _store_rows(qd_ref, q * jnp.exp(cum))
        _store_rows(ki_ref, k * jnp.exp(-cum))
        _store_rows(ks_ref, k * jnp.exp(tot - cum))
        for bb in range(ed_ref.shape[0]):
            for ch in range(tm // GLA_CHUNK):
                row = bb * tm + ch * GLA_CHUNK
                ed_ref[bb, ch] = jnp.exp(tot[row:row + 1])
    _store_rows(ov, _dot(h, wv[...]))
    _store_rows(og, _dot(h, wg[...]))
    u = _dot(h, wu[...])
    for qb in range(ou.shape[0]):
        _store_rows(ou.at[qb], u[:, qb * LANES:(qb + 1) * LANES])


def _inproj0(ctx, x, mods, norm_g, weights, wa_pad, ba, nct):
    bsz, n_lat, d = x.shape
    t = ctx.shape[1] + n_lat
    tm, tb = TOKEN_TILE, TOKEN_TILE_BATCH
    mod_spec = pl.BlockSpec((tb, 1, 6, d), lambda b, i: (b, (i >= nct).astype(I32), 0, 0))
    ctx_spec, lat_spec = _split_specs(nct, d, tb)
    pos = jnp.arange(tm)
    same_chunk = (pos[:, None] // GLA_CHUNK) == (pos[None, :] // GLA_CHUNK)
    tri = jnp.stack([same_chunk & (pos[None, :] <= pos[:, None]),
                     same_chunk & (pos[None, :] >= pos[:, None])]).astype(BF16)
    ones = same_chunk.astype(BF16)
    consts = list(weights) + [wa_pad, ba, tri, ones]
    tok = lambda w, dt: (pl.BlockSpec((tb, tm, w), lambda b, i: (b, i, 0)), jax.ShapeDtypeStruct((bsz, t, w), dt))
    per_chunk = (pl.BlockSpec((tb, tm // GLA_CHUNK, 1, AB_QK), lambda b, i: (b, i, 0, 0)),
                 jax.ShapeDtypeStruct((bsz, t // GLA_CHUNK, 1, AB_QK), F32))
    one_dir = [tok(AB_QK, BF16)] * 3 + [per_chunk]
    u_blocks = (pl.BlockSpec((S5_CH // LANES, tb, tm, LANES), lambda b, i: (0, b, i, 0)),
                jax.ShapeDtypeStruct((S5_CH // LANES, bsz, t, LANES), F32))
    outs = one_dir + one_dir + [tok(AB_V, BF16), tok(AB_V, BF16), u_blocks]
    return pl.pallas_call(
        functools.partial(_inproj0_body, nct),
        grid=(bsz // tb, t // tm),
        in_specs=[ctx_spec, lat_spec, mod_spec, _const_spec((1, d))] + [_const_spec(a.shape) for a in consts],
        out_specs=[o[0] for o in outs],
        out_shape=[o[1] for o in outs],
        compiler_params=_params("arbitrary", "arbitrary"),
        name="inproj_gla_s5",
    )(ctx, x, mods, norm_g.reshape(1, d), *consts)


def _rope(acc, cos_ref, sin_ref, o_ref, scale):
    tm = o_ref.shape[1]
    for grp in range(acc.shape[1] // LANES):
        half = grp % 2
        cs = cos_ref[:, half * LANES:(half + 1) * LANES]
        sn = sin_ref[:, half * LANES:(half + 1) * LANES]
        for bb in range(o_ref.shape[0]):
            xg = acc[bb * tm:(bb + 1) * tm, grp * LANES:(grp + 1) * LANES]
            out = xg * cs + pltpu.roll(xg, LANES // 2, 1) * sn
            o_ref[bb, :, grp * LANES:(grp + 1) * LANES] = (out * scale).astype(o_ref.dtype)


def _moe_residual(x1_ref, g2, yk_ref, w_ref, bb=0):
    d = x1_ref.shape[2]
    half = d // 2
    y_lo, y_hi = None, None
    for k in range(TOP_K):
        lo, hi = _unpack_rows(yk_ref[k, bb])
        wk = w_ref[bb, :, k:k + 1]
        y_lo = lo * wk if y_lo is None else y_lo + lo * wk
        y_hi = hi * wk if y_hi is None else y_hi + hi * wk
    return x1_ref[bb, :, 0:half] + g2[:, 0:half] * y_lo, x1_ref[bb, :, half:d] + g2[:, half:d] * y_hi


def _inproj1_body(x1_ref, mod0_ref, yk_ref, w_ref, mod_ref, g_ref, cos_ref, sin_ref, wq, wk, wv, wg, ox, oq, ok, ov, og):
    half = x1_ref.shape[2] // 2
    tiles = []
    for bb in range(x1_ref.shape[0]):
        x2_lo, x2_hi = _moe_residual(x1_ref, mod0_ref[bb, 0][5:6], yk_ref, w_ref, bb)
        ox[bb, :, 0:half] = x2_lo
        ox[bb, :, half:2 * half] = x2_hi
        m = mod_ref[bb, 0]
        tiles.append(_norm_mod(jnp.concatenate([x2_lo, x2_hi], axis=1), g_ref[...], m[0:1], m[1:2]).astype(BF16))
    h = tiles[0] if len(tiles) == 1 else jnp.concatenate(tiles, axis=0)
    _rope(_dot(h, wq[...]), cos_ref, sin_ref, oq, 1.0)
    _rope(_dot(h, wk[...]), cos_ref, sin_ref, ok, RET_DK ** -0.5)
    _store_rows(ov, _dot(h, wv[...]))
    _store_rows(og, _dot(h, wg[...]))


def _inproj1(x1, mods0, yk, w_tok, mods, norm_g, cos_t, sin_t, weights, nct):
    bsz, t, d = x1.shape
    tm, tb = TOKEN_TILE, TOKEN_TILE_BATCH
    mod_spec = pl.BlockSpec((tb, 1, 6, d), lambda b, i: (b, (i >= nct).astype(I32), 0, 0))
    tok = lambda w: pl.BlockSpec((tb, tm, w), lambda b, i: (b, i, 0))
    lat = lambda w: pl.BlockSpec((tb, tm, w), lambda b, i: (b, jnp.maximum(i - nct, 0), 0))
    tab_spec = pl.BlockSpec((tm, RET_DK), lambda b, i: (i, 0))
    wq, wk, wv, wg = weights
    n_lat = t - nct * tm
    return pl.pallas_call(
        _inproj1_body,
        grid=(bsz // tb, t // tm),
        in_specs=[tok(d), mod_spec, pl.BlockSpec((TOP_K, tb, tm, d // 2), lambda b, i: (0, b, i, 0)), tok(TOP_K),
                  mod_spec, _const_spec((1, d)), tab_spec, tab_spec] + [_const_spec(w.shape) for w in weights],
        out_specs=[lat(d), tok(wq.shape[1]), tok(wk.shape[1]), tok(wv.shape[1]), lat(wg.shape[1])],
        out_shape=[jax.ShapeDtypeStruct((bsz, n_lat, d), F32), jax.ShapeDtypeStruct((bsz, t, wq.shape[1]), BF16),
                   jax.ShapeDtypeStruct((bsz, t, wk.shape[1]), BF16), jax.ShapeDtypeStruct((bsz, t, wv.shape[1]), BF16),
                   jax.ShapeDtypeStruct((bsz, n_lat, wg.shape[1]), BF16)],
        compiler_params=_params("arbitrary", "arbitrary"),
        name="inproj_retention",
    )(x1, mods0, yk.reshape(TOP_K, bsz, t, d // 2), w_tok.reshape(bsz, t, TOP_K), mods, norm_g.reshape(1, d),
      cos_t, sin_t, *weights)


def _backward_chunk(n, n_ctx_chunks, n_chunks):
    return jnp.where(n < n_ctx_chunks, n_ctx_chunks - 1 - n, n_chunks - 1 - (n - n_ctx_chunks))


def _gla_body(qd_f, ki_f, ks_f, ed_f, v_f, qd_b, ki_b, ks_b, ed_b, v_b, hmask_ref, bdmask_ref, o_f, o_b, st_f, st_b):
    c = GLA_CHUNK

    @pl.when(pl.program_id(1) == 0)
    def _():
        st_f[...] = jnp.zeros_like(st_f)
        st_b[...] = jnp.zeros_like(st_b)

    r4 = lax.broadcasted_iota(I32, (GLA_HEADS * c, c), 0) & (c - 1)
    c4 = lax.broadcasted_iota(I32, (GLA_HEADS * c, c), 1)
    dirs = ((qd_f, ki_f, ks_f, ed_f, v_f, o_f, st_f), (qd_b, ki_b, ks_b, ed_b, v_b, o_b, st_b))
    chains = [(bb, d) + dirs[d] for bb in range(qd_f.shape[0]) for d in range(2)]
    scores, inter, grow = [], [], []
    for bb, d, qd_ref, ki_ref, ks_ref, ed_ref, v_ref, o_ref, st_ref in chains:
        q_dec = qd_ref[bb]
        q_heads = jnp.concatenate([q_dec] * GLA_HEADS, axis=0) * hmask_ref[...]
        seen4 = (c4 <= r4) if d == 0 else (c4 >= r4)
        scores.append(jnp.where(seen4, _dot_nt(q_heads, ki_ref[bb]), 0.0).astype(BF16))
        inter.append(_dot_nt(q_dec, st_ref[bb].astype(BF16)))
        grow.append(_dot_tn(v_ref[bb], ks_ref[bb]))
    for (bb, d, qd_ref, ki_ref, ks_ref, ed_ref, v_ref, o_ref, st_ref), sc, o_inter, dst in zip(chains, scores, inter, grow):
        v = v_ref[bb]
        o_intra = jnp.concatenate(
            [_dot(sc[h * c:(h + 1) * c], v[:, h * GLA_DV:(h + 1) * GLA_DV]) for h in range(GLA_HEADS)], axis=1)
        o_ref[bb] = o_intra + o_inter
        st_ref[bb] = st_ref[bb] * ed_ref[bb, 0] + bdmask_ref[...] * dst


def _gla(per_dir, v, n_ctx):
    bsz, t, _ = v.shape
    nc, ncc = t // GLA_CHUNK, n_ctx // GLA_CHUNK
    gb = GLA_BATCH
    fwd = lambda b, n: (b, n, 0)
    bwd = lambda b, n: (b, _backward_chunk(n, ncc, nc), 0)
    hmask = (jnp.arange(AB_QK)[:, None] // GLA_CHUNK == jnp.arange(AB_QK)[None, :] // GLA_DK).astype(BF16)
    bdmask = (jnp.arange(AB_V)[:, None] // GLA_DV == jnp.arange(AB_QK)[None, :] // GLA_DK).astype(F32)

    def specs(idx):
        idx4 = lambda b, n: idx(b, n) + (0,)
        return [pl.BlockSpec((gb, GLA_CHUNK, AB_QK), idx)] * 3 + [pl.BlockSpec((gb, 1, 1, AB_QK), idx4),
                                                                  pl.BlockSpec((gb, GLA_CHUNK, AB_V), idx)]

    return pl.pallas_call(
        _gla_body,
        grid=(bsz // gb, nc),
        in_specs=specs(fwd) + specs(bwd) + [_const_spec(hmask.shape), _const_spec(bdmask.shape)],
        out_specs=[pl.BlockSpec((gb, GLA_CHUNK, AB_V), fwd), pl.BlockSpec((gb, GLA_CHUNK, AB_V), bwd)],
        out_shape=[jax.ShapeDtypeStruct((bsz, t, AB_V), F32)] * 2,
        scratch_shapes=[pltpu.VMEM((gb, AB_V, AB_QK), F32)] * 2,
        compiler_params=_params("arbitrary", "arbitrary"),
        name="gla_scan",
    )(*per_dir[0], v, *per_dir[1], v, hmask, bdmask)


def _cmul(x, y):
    return x[0] * y[0] - x[1] * y[1], x[0] * y[1] + x[1] * y[0]


def _s5_operators(lam_re, lam_im, log_step, b_re, b_im, c_re, c_im):
    ln = S5_CHUNK
    step = jnp.exp(log_step.astype(F32))[..., None]
    lam_re, lam_im = lam_re.astype(F32), lam_im.astype(F32)
    mag = jnp.exp(lam_re * step)
    a = (mag * jnp.cos(lam_im * step), mag * jnp.sin(lam_im * step))
    den = lam_re * lam_re + lam_im * lam_im
    f_re = ((a[0] - 1.0) * lam_re + a[1] * lam_im) / den
    f_im = (a[1] * lam_re - (a[0] - 1.0) * lam_im) / den
    bt_re, bt_im = b_re.transpose(0, 2, 1), b_im.transpose(0, 2, 1)
    bb = _cmul((f_re[:, :, None, :], f_im[:, :, None, :]), (bt_re, bt_im))
    bbt = jnp.concatenate([bb[0], -bb[1]], axis=-1)
    pw = (a[0][:, :, None, :], a[1][:, :, None, :])
    while pw[0].shape[2] < ln:
        top = (pw[0][:, :, -1:, :], pw[1][:, :, -1:, :])
        nxt = _cmul(top, pw)
        pw = (jnp.concatenate([pw[0], nxt[0]], axis=2), jnp.concatenate([pw[1], nxt[1]], axis=2))
    pw = (jnp.concatenate([jnp.ones_like(pw[0][:, :, :1]), pw[0]], axis=2),
          jnp.concatenate([jnp.zeros_like(pw[1][:, :, :1]), pw[1]], axis=2))
    ca = _cmul((c_re[:, :, None], c_im[:, :, None]), (pw[0][:, :, :, None, :], pw[1][:, :, :, None, :]))
    by_dir = lambda arr, lo, flip_d: jnp.stack([jnp.flip(arr[d, :, lo:lo + ln], axis=1) if d == flip_d
                                                else arr[d, :, lo:lo + ln] for d in range(2)])
    rows = lambda arr: arr.reshape(2, S5_GROUPS, ln * S5_GROUP, 2 * S5_P)
    cab = rows(by_dir(jnp.concatenate([ca[0], ca[1]], axis=-1), 0, 1))
    cab2 = rows(by_dir(jnp.concatenate([ca[0], -ca[1]], axis=-1), 1, 1)).astype(BF16)
    pwx = by_dir(jnp.concatenate([pw[0], pw[1]], axis=-1), 0, 0)
    lr, li = pw[0][:, :, ln], pw[1][:, :, ln]
    ac_rows = [jnp.concatenate([lr, lr], -1), jnp.concatenate([-li, li], -1), jnp.concatenate([li, -li], -1)]
    ac = jnp.stack(ac_rows + [jnp.zeros_like(ac_rows[0])] * 5, axis=2)
    return cab, cab2, bbt, pwx, ac


def _s5_group_operators(gg, cab_ref, bbt_ref, pwx_ref, tz, wx):
    ln, ch, p = S5_CHUNK, S5_GROUP, S5_P
    lane = lax.broadcasted_iota(I32, (ch, ln * ch), 1)
    for d in range(2):
        kern = _dot3(bbt_ref[d, gg], cab_ref[d, gg], dot=_dot_nt)
        bt = bbt_ref[d, gg]
        b_re, b_im = bt[:, 0:p], -bt[:, p:2 * p]
        for j in range(ln):
            if d == 0:
                blk = jnp.where(lane >= j * ch, kern if j == 0 else pltpu.roll(kern, j * ch, 1), 0.0)
            else:
                blk = jnp.where(lane < (j + 1) * ch, kern if j == ln - 1 else pltpu.roll(kern, (j + 1) * ch, 1), 0.0)
            tz[gg, d, j * ch:(j + 1) * ch, :] = blk.astype(BF16)
            pr, pi = pwx_ref[d, gg, j:j + 1, 0:p], pwx_ref[d, gg, j:j + 1, p:2 * p]
            x_re, x_im = pr * b_re - pi * b_im, pr * b_im + pi * b_re
            wx[gg, d, j * ch:(j + 1) * ch, :] = jnp.concatenate([x_re, x_im, x_im, x_re], axis=1).astype(BF16)


def _s5_placement(pall):
    rows, cols = pall.shape[1], pall.shape[2]
    row = lax.broadcasted_iota(I32, (rows, cols), 0)
    col = lax.broadcasted_iota(I32, (rows, cols), 1)
    same_token = (row // LANES) == (col // S5_GROUP)
    for g8 in range(pall.shape[0]):
        pall[g8] = jnp.where(same_token & ((row % LANES) == g8 * S5_GROUP + (col % S5_GROUP)), 1.0, 0.0).astype(BF16)


def _first_step():
    return (pl.program_id(0) == 0) & (pl.program_id(1) == 0)


def _s5_fold_body(ncs, u_ref, o_ref, pall, ucat):
    @pl.when(_first_step())
    def _():
        _s5_placement(pall)

    for b in range(u_ref.shape[1]):
        for j in range(S5_CHUNK):
            ucat[b * ncs:(b + 1) * ncs, j * LANES:(j + 1) * LANES] = u_ref[0, b, pl.ds(j, ncs, stride=S5_CHUNK), :].astype(BF16)
    for g8 in range(pall.shape[0]):
        o_ref[g8] = _dot(ucat[...], pall[g8]).astype(BF16)


def _s5_unfold_body(ncs, y_ref, o_ref, pall):
    @pl.when(_first_step())
    def _():
        _s5_placement(pall)

    def token_pair(i2, carry):
        r0 = pl.multiple_of(i2 * 2 * LANES, 2 * LANES)
        acc = _dot_nt(y_ref[0], pall[0, pl.ds(r0, 2 * LANES), :])
        for g8 in range(1, pall.shape[0]):
            acc = acc + _dot_nt(y_ref[g8], pall[g8, pl.ds(r0, 2 * LANES), :])
        for b in range(o_ref.shape[1]):
            for par in range(2):
                o_ref[0, b, pl.ds(2 * i2 + par, ncs, stride=S5_CHUNK), :] = (
                    acc[b * ncs:(b + 1) * ncs, par * LANES:(par + 1) * LANES])
        return carry

    lax.fori_loop(0, S5_CHUNK // 2, token_pair, 0)


def _s5_body(ncs_ctx, ncs, rows, u_ref, cab_ref, cab2_ref, bbt_ref, pwx_ref, ac_ref, y_ref, tz, wx, *vecs):
    half = 2 * S5_P
    n_groups = u_ref.shape[0]
    groups = [vecs[6 * gg:6 * gg + 6] for gg in range(n_groups)]
    for gg, (xx_f, xs_f, xx_b, xs_b, _, _) in enumerate(groups):
        _s5_group_operators(gg, cab_ref, bbt_ref, pwx_ref, tz, wx)
        for d, (xx, xs) in enumerate(((xx_f, xs_f), (xx_b, xs_b))):
            r = _dot(u_ref[gg], wx[gg, d])
            xx[...] = r[:, :half]
            xs[...] = r[:, half:]

    def advance(ac, s, s_sw, x, x_sw):
        return ac[0:1] * s + ac[1:2] * s_sw + x, ac[0:1] * s_sw + ac[2:3] * s + x_sw

    def step(n, carry):
        at_f = pl.ds(n, rows, stride=ncs)
        at_b = pl.ds(_backward_chunk(n, ncs_ctx, ncs), rows, stride=ncs)
        out = []
        for gg, (xx_f, xs_f, xx_b, xs_b, sin_f, sin_b) in enumerate(groups):
            s_f, sw_f, s_b, sw_b = carry[4 * gg:4 * gg + 4]
            sin_f[at_f, :] = s_f
            sin_b[at_b, :] = s_b
            out += advance(ac_ref[0, gg], s_f, sw_f, xx_f[at_f, :], xs_f[at_f, :])
            out += advance(ac_ref[1, gg], s_b, sw_b, xx_b[at_b, :], xs_b[at_b, :])
        return tuple(out)

    zero = jnp.zeros((rows, half), F32)
    lax.fori_loop(0, ncs, step, (zero,) * (4 * n_groups))
    for gg, (_, _, _, _, sin_f, sin_b) in enumerate(groups):
        u = u_ref[gg]
        y_ref[gg] = (_dot(u, tz[gg, 0]) + _dot(u, tz[gg, 1]) + _dot_nt(sin_f[...].astype(BF16), cab2_ref[0, gg])
                     + _dot_nt(sin_b[...].astype(BF16), cab2_ref[1, gg])).astype(BF16)


def _s5(u4, ops, n_ctx):
    nq, bsz, t, _ = u4.shape
    ln, lanes = S5_CHUNK, S5_CHUNK * S5_GROUP
    gq = S5_GROUPS // nq
    ncs, ncs_ctx = t // ln, n_ctx // ln
    m = ncs * bsz
    hb = S5_FOLD_BATCH
    tok_spec = pl.BlockSpec((1, hb, t, LANES), lambda q, h: (q, h, 0, 0))
    grp_spec = pl.BlockSpec((gq, hb * ncs, lanes), lambda q, h: (q, h, 0))
    pall = pltpu.VMEM((gq, ln * LANES, lanes), BF16)
    ug = pl.pallas_call(
        functools.partial(_s5_fold_body, ncs),
        grid=(nq, bsz // hb),
        in_specs=[tok_spec],
        out_specs=grp_spec,
        out_shape=jax.ShapeDtypeStruct((S5_GROUPS, m, lanes), BF16),
        scratch_shapes=[pall, pltpu.VMEM((hb * ncs, ln * LANES), BF16)],
        compiler_params=_params("arbitrary", "arbitrary"),
        name="s5_fold",
    )(u4)
    sg = S5_SCAN_GROUPS
    dir_spec = lambda arr: pl.BlockSpec((2, sg) + arr.shape[2:], lambda g: (0, g, 0, 0))
    yg = pl.pallas_call(
        functools.partial(_s5_body, ncs_ctx, ncs, bsz),
        grid=(S5_GROUPS // sg,),
        in_specs=[pl.BlockSpec((sg, m, lanes), lambda g: (g, 0, 0))] + [dir_spec(arr) for arr in ops],
        out_specs=pl.BlockSpec((sg, m, lanes), lambda g: (g, 0, 0)),
        out_shape=jax.ShapeDtypeStruct((S5_GROUPS, m, lanes), BF16),
        scratch_shapes=[pltpu.VMEM((sg, 2, lanes, lanes), BF16)] * 2 + [pltpu.VMEM((m, 2 * S5_P), F32)] * (6 * sg),
        compiler_params=_params("arbitrary"),
        name="s5_scan",
    )(ug, *ops)
    return pl.pallas_call(
        functools.partial(_s5_unfold_body, ncs),
        grid=(nq, bsz // hb),
        in_specs=[grp_spec],
        out_specs=tok_spec,
        out_shape=jax.ShapeDtypeStruct(u4.shape, F32),
        scratch_shapes=[pall],
        compiler_params=_params("arbitrary", "arbitrary"),
        name="s5_unfold",
    )(yg)


def _ret_body(q_f, k_f, v_f, q_b, k_b, v_b, dmat_ref, rsc_ref, csc_ref, gam_ref, o_f, o_b, st_f, st_b):
    @pl.when(pl.program_id(1) == 0)
    def _():
        st_f[...] = jnp.zeros_like(st_f)
        st_b[...] = jnp.zeros_like(st_b)

    dirs = ((q_f, k_f, v_f, o_f, st_f), (q_b, k_b, v_b, o_b, st_b))
    for d, (q_ref, k_ref, v_ref, o_ref, st_ref) in enumerate(dirs):
        for h in range(RET_HEADS):
            qh = q_ref[0, :, h * RET_DK:(h + 1) * RET_DK]
            kh = k_ref[0, :, h * RET_DK:(h + 1) * RET_DK]
            vh = v_ref[0, :, h * RET_DV:(h + 1) * RET_DV]
            st = st_ref[h]
            scores = (_dot_nt(qh, kh) * dmat_ref[d, h]).astype(BF16)
            o = _dot(scores, vh) + rsc_ref[d, h] * _dot(qh, st.astype(BF16))
            o_ref[0, :, h * RET_DV:(h + 1) * RET_DV] = o.astype(o_ref.dtype)
            k_state = (kh.astype(F32) * csc_ref[d, h]).astype(BF16)
            st_ref[h] = st * gam_ref[d, h] + _dot_tn(k_state, vh)


def _retention(q, k, v, decay_logit, n_ctx):
    bsz, t, _ = q.shape
    c = RET_CHUNK
    nc, ncc = t // c, n_ctx // c
    nl = nc - ncc
    log_gamma = jax.nn.log_sigmoid(decay_logit.astype(F32))[:, :, None, None]
    i = jnp.arange(c, dtype=F32)
    lag = i[:, None] - i[None, :]
    lag = jnp.stack([lag, -lag])[:, None]
    dmat = jnp.where(lag >= 0, jnp.exp(log_gamma * jnp.maximum(lag, 0.0)), 0.0)
    done = jnp.stack([i + 1.0, c - i])[:, None, :, None]
    rsc = jnp.exp(log_gamma * done)
    csc = jnp.exp(log_gamma * (c - done))
    gam = jnp.exp(log_gamma[:, :, 0, 0] * c)
    fwd = lambda b, n: (b, n, 0)
    bwd = lambda b, n: (b, _backward_chunk(n, ncc, nc), 0)
    o_fwd = lambda b, n: (b, jnp.maximum(n - ncc, 0), 0)
    o_bwd = lambda b, n: (b, nl - 1 - jnp.maximum(n - ncc, 0), 0)

    def specs(idx):
        return [pl.BlockSpec((1, c, RET_QK), idx), pl.BlockSpec((1, c, RET_QK), idx), pl.BlockSpec((1, c, RET_MIX), idx)]

    return pl.pallas_call(
        _ret_body,
        grid=(bsz, nc),
        in_specs=specs(fwd) + specs(bwd) + [_const_spec(dmat.shape), _const_spec(rsc.shape), _const_spec(csc.shape),
                                            pl.BlockSpec(memory_space=pltpu.SMEM)],
        out_specs=[pl.BlockSpec((1, c, RET_MIX), o_fwd), pl.BlockSpec((1, c, RET_MIX), o_bwd)],
        out_shape=[jax.ShapeDtypeStruct((bsz, nl * c, RET_MIX), BF16)] * 2,
        scratch_shapes=[pltpu.VMEM((RET_HEADS, RET_DK, RET_DV), F32)] * 2,
        compiler_params=_params("arbitrary", "arbitrary"),
        name="retention_scan",
    )(q, k, v, q, k, v, dmat, rsc, csc, gam)


def _zero_counts_at_start(cnt_ref):
    @pl.when(_first_step())
    def _():
        cnt_ref[...] = jnp.zeros_like(cnt_ref)


def _route(xs, mixed, mods, n2g_ref, wr_ref, br_ref, x1_ref, h2_ref, e_ref, w_ref, r_ref, cnt_ref):
    rows = xs[0].shape[0]
    h2s = []
    for bb, (x, mod) in enumerate(zip(xs, mods)):
        x1 = x + mod[2:3] * mixed[bb * rows:(bb + 1) * rows]
        x1_ref[bb] = x1
        h2s.append(_norm_mod(x1, n2g_ref[...], mod[3:4], mod[4:5]))
        h2_ref[bb] = _pack_rows(h2s[bb])
    h2 = h2s[0] if len(h2s) == 1 else jnp.concatenate(h2s, axis=0)
    tm = h2.shape[0]
    logits = _dot3(wr_ref[...], h2, dot=_dot_nt) + br_ref[...]
    ie = lax.broadcasted_iota(I32, logits.shape, 0)
    tops, picks = [], []
    for _ in range(TOP_K):
        mx = jnp.max(logits, axis=0, keepdims=True)
        pick = jnp.min(jnp.where(logits == mx, ie, N_EXPERTS), axis=0, keepdims=True)
        tops.append(mx)
        picks.append(pick)
        logits = jnp.where(ie == pick, -jnp.inf, logits)
    ex = [jnp.exp(tk - tops[0]) for tk in tops]
    den = ex[0] + ex[1] + ex[2] + ex[3]
    for kk in range(TOP_K):
        w_ref[0, kk:kk + 1, :] = ex[kk] / den
        e_ref[0, kk:kk + 1, :] = picks[kk]

    earlier = (lax.broadcasted_iota(I32, (tm, tm), 0) < lax.broadcasted_iota(I32, (tm, tm), 1))
    earlier = jnp.where(earlier, 1.0, 0.0).astype(BF16)
    run = cnt_ref[:, 0:1]
    for kk, pick in enumerate(picks):
        onehot = jnp.where(ie == pick, 1.0, 0.0)
        before = _dot(onehot.astype(BF16), earlier) + run
        r_ref[0, kk:kk + 1, :] = jnp.sum(onehot * before, axis=0, keepdims=True).astype(I32)
        run = run + jnp.sum(onehot, axis=1, keepdims=True)
    cnt_ref[...] = jnp.broadcast_to(run, cnt_ref.shape)


def _mix0_body(nct, ctx_ref, lat_ref, mod_ref, of_ref, ob_ref, g_ref, ys_ref, u_ref, gng_ref, dsk_ref, gluw_ref,
               glub_ref, wo_ref, n2g_ref, wr_ref, br_ref, x1_ref, h2_ref, e_ref, w_ref, r_ref, cnt_ref):
    _zero_counts_at_start(cnt_ref)
    tb, tm = of_ref.shape[0], of_ref.shape[1]
    rows = lambda ref: ref[...].reshape(tb * tm, ref.shape[-1])
    o = rows(of_ref) + rows(ob_ref)
    heads = []
    for h in range(GLA_HEADS):
        oh = o[:, h * GLA_DV:(h + 1) * GLA_DV]
        heads.append(oh * lax.rsqrt(jnp.mean(oh * oh, axis=-1, keepdims=True) + EPS))
    gla = jnp.concatenate(heads, axis=1) * gng_ref[...] * _silu(rows(g_ref).astype(F32))
    lane_blocks = lambda ref: jnp.concatenate([ref[qb].reshape(tb * tm, LANES) for qb in range(ref.shape[0])], axis=1)
    y = jax.nn.gelu(lane_blocks(ys_ref) + dsk_ref[...] * lane_blocks(u_ref))
    y = y * jax.nn.sigmoid(_dot(y.astype(BF16), gluw_ref[...]) + glub_ref[...])
    mixed = _dot(gla.astype(BF16), wo_ref[0:AB_V]) + _dot(y.astype(BF16), wo_ref[AB_V:AB_V + S5_CH])
    _route([_stream_tile(nct, ctx_ref, lat_ref, bb) for bb in range(tb)], mixed, [mod_ref[bb, 0] for bb in range(tb)],
           n2g_ref, wr_ref, br_ref, x1_ref, h2_ref, e_ref, w_ref, r_ref, cnt_ref)


def _mix1_body(x_ref, mod_ref, of_ref, ob_ref, g_ref, ng_ref, wo_ref, n2g_ref, wr_ref, br_ref,
               x1_ref, h2_ref, e_ref, w_ref, r_ref, cnt_ref):
    _zero_counts_at_start(cnt_ref)
    mixed = None
    for h in range(RET_HEADS):
        sl = slice(h * RET_DV, (h + 1) * RET_DV)
        oh = of_ref[0, :, sl].astype(F32) + ob_ref[0, :, sl].astype(F32)
        mu = jnp.mean(oh, axis=-1, keepdims=True)
        cen = oh - mu
        var = jnp.mean(cen * cen, axis=-1, keepdims=True)
        gated = cen * lax.rsqrt(var + EPS) * ng_ref[:, sl] * _silu(g_ref[0, :, sl].astype(F32))
        part = _dot(gated.astype(BF16), wo_ref[sl])
        mixed = part if mixed is None else mixed + part
    _route([x_ref[0]], mixed, [mod_ref[0, 0]], n2g_ref, wr_ref, br_ref, x1_ref, h2_ref, e_ref, w_ref, r_ref, cnt_ref)


def _mix_call(body, name, stream, mods, tiles, acts, consts, norm2_g, w_router, b_router, n_tok, seg_tile0,
              tm=TOKEN_TILE, tb=1):
    bsz, _, d = stream[-1].shape
    off = lambda b, i: (b, i + seg_tile0, 0)
    loc = lambda b, i: (b, i, 0)
    ntl = bsz // tb * tiles
    flat = lambda b, i: (b * tiles + i, 0, 0)
    in_specs = list(_split_specs(n_tok, d, tb)) if len(stream) == 2 else [pl.BlockSpec((tb, tm, d), off)]
    in_specs.append(pl.BlockSpec((tb, 1, 6, d), lambda b, i: (b, ((i + seg_tile0) >= n_tok).astype(I32), 0, 0)))
    args = list(stream) + [mods]
    for arr, offset in acts:
        if arr.ndim == 4:
            in_specs.append(pl.BlockSpec((arr.shape[0], tb, tm, arr.shape[3]), lambda b, i: (0, b, i, 0)))
        else:
            in_specs.append(pl.BlockSpec((tb, tm, arr.shape[2]), off if offset else loc))
        args.append(arr)
    tail = list(consts) + [norm2_g.reshape(1, d), w_router.T, b_router.reshape(N_EXPERTS, 1)]
    in_specs += [_const_spec(a.shape) for a in tail]
    args += tail
    tok_out = pl.BlockSpec((1, TOP_K, tb * tm), flat)
    return pl.pallas_call(
        body,
        grid=(bsz // tb, tiles),
        in_specs=in_specs,
        out_specs=[pl.BlockSpec((tb, tm, d), loc), pl.BlockSpec((tb, tm, d // 2), loc), tok_out, tok_out, tok_out,
                   _const_spec((N_EXPERTS, LANES))],
        out_shape=[jax.ShapeDtypeStruct((bsz, tiles * tm, d), F32), jax.ShapeDtypeStruct((bsz, tiles * tm, d // 2), U32),
                   jax.ShapeDtypeStruct((ntl, TOP_K, tb * tm), I32), jax.ShapeDtypeStruct((ntl, TOP_K, tb * tm), F32),
                   jax.ShapeDtypeStruct((ntl, TOP_K, tb * tm), I32), jax.ShapeDtypeStruct((N_EXPERTS, LANES), F32)],
        compiler_params=_params("arbitrary", "arbitrary"),
        name=name,
    )(*args)


def _cast_rows(src_ref, dst_ref, rows):
    def chunk(j, carry):
        r = pl.multiple_of(j * rows, rows)
        dst_ref[pl.ds(r, rows), :] = src_ref[0, 0, pl.ds(r, rows), :].astype(BF16)
        return carry

    lax.fori_loop(0, dst_ref.shape[0] // rows, chunk, 0)


def _expert_body(be_ref, nu_ref, x_ref, wgu_ref, bgu_ref, wd_ref, bd_ref, o_ref, wgu_bf, wd_bf):
    i = pl.program_id(0)
    live = i < nu_ref[0]
    new_expert = (i == 0) | (be_ref[i] != be_ref[jnp.maximum(i - 1, 0)])

    @pl.when(live & new_expert)
    def _():
        _cast_rows(wgu_ref, wgu_bf, 128)
        _cast_rows(wd_ref, wd_bf, 128)

    @pl.when(live)
    def _():
        x_lo, x_hi = _unpack_rows(x_ref[...])
        half = x_lo.shape[1]
        gu = (_dot(x_lo.astype(BF16), wgu_bf[0:half]) + _dot(x_hi.astype(BF16), wgu_bf[half:2 * half])
              + bgu_ref[0, 0])
        gate = jnp.minimum(gu[:, :D_FF], SWIGLU_LIMIT)
        lin = jnp.clip(gu[:, D_FF:], -SWIGLU_LIMIT, SWIGLU_LIMIT)
        act = gate * jax.nn.sigmoid(SWIGLU_ALPHA * gate) * (lin + 1.0)
        y = _dot(act.astype(BF16), wd_bf[...]) + bd_ref[0, 0]
        o_ref[...] = _pack_rows(y)

    @pl.when(i >= nu_ref[0])
    def _():
        o_ref[...] = jnp.zeros_like(o_ref)


def _experts(xb, block_e, n_used, layer, w_gu, b_gu, w_down, b_down):
    n_slots, half = xb.shape
    d = 2 * half
    n_blocks = n_slots // MOE_BLOCK
    depth = w_gu.shape[0]
    by_expert = lambda i, be, nu: (layer, be[i], 0, 0)
    return pl.pallas_call(
        _expert_body,
        grid_spec=pltpu.PrefetchScalarGridSpec(
            num_scalar_prefetch=2,
            grid=(n_blocks,),
            in_specs=[pl.BlockSpec((MOE_BLOCK, half), lambda i, be, nu: (i, 0)),
                      pl.BlockSpec((1, 1, d, 2 * D_FF), by_expert), pl.BlockSpec((1, 1, 1, 2 * D_FF), by_expert),
                      pl.BlockSpec((1, 1, D_FF, d), by_expert), pl.BlockSpec((1, 1, 1, d), by_expert)],
            out_specs=pl.BlockSpec((MOE_BLOCK, half), lambda i, be, nu: (i, 0)),
            scratch_shapes=[pltpu.VMEM((d, 2 * D_FF), BF16), pltpu.VMEM((D_FF, d), BF16)]),
        out_shape=jax.ShapeDtypeStruct((n_slots, half), U32),
        compiler_params=_params("arbitrary"),
        name="moe_experts",
    )(block_e, n_used, xb, w_gu, b_gu.reshape(depth, N_EXPERTS, 1, 2 * D_FF), w_down,
      b_down.reshape(depth, N_EXPERTS, 1, d))


def _combine_body(x1_ref, mod_ref, yk_ref, w_ref, fg_ref, o_ref):
    d = x1_ref.shape[2]
    half = d // 2
    x2_lo, x2_hi = _moe_residual(x1_ref, mod_ref[0, 0][5:6], yk_ref, w_ref)
    ms = (jnp.sum(x2_lo * x2_lo, axis=-1, keepdims=True) + jnp.sum(x2_hi * x2_hi, axis=-1, keepdims=True)) / d
    r = lax.rsqrt(ms + EPS)
    o_ref[0, :, 0:half] = x2_lo * r * fg_ref[:, 0:half]
    o_ref[0, :, half:d] = x2_hi * r * fg_ref[:, half:d]


def _combine(x1, mods, yk, w_tok, seg_tile0, n_tok, final_g, tm):
    bsz, t, d = x1.shape
    loc = lambda b, i: (b, i, 0)
    return pl.pallas_call(
        _combine_body,
        grid=(bsz, t // tm),
        in_specs=[pl.BlockSpec((1, tm, d), loc),
                  pl.BlockSpec((1, 1, 6, d), lambda b, i: (b, ((i + seg_tile0) >= n_tok).astype(I32), 0, 0)),
                  pl.BlockSpec((TOP_K, 1, tm, d // 2), lambda b, i: (0, b, i, 0)),
                  pl.BlockSpec((1, tm, TOP_K), loc), _const_spec((1, d))],
        out_specs=pl.BlockSpec((1, tm, d), loc),
        out_shape=jax.ShapeDtypeStruct((bsz, t, d), F32),
        compiler_params=_params("arbitrary", "arbitrary"),
        name="moe_combine",
    )(x1, mods, yk.reshape(TOP_K, bsz, t, d // 2), w_tok.reshape(bsz, t, TOP_K), final_g.reshape(1, d))


def _sc_mesh():
    return plsc.VectorSubcoreMesh(core_axis_name="core", subcore_axis_name="subcore",
                                  num_cores=SC_CORES, num_subcores=SC_SUBCORES)


def _sc_worker_base(per_worker):
    return (lax.axis_index("subcore") * SC_CORES + lax.axis_index("core")) * per_worker


def _sc_dispatch(rows, dest, n_slots):
    n, w = rows.shape
    per_worker = n // SC_WORKERS
    assert per_worker * SC_WORKERS == n and per_worker % SC_CHUNK == 0

    @functools.partial(
        pl.kernel, mesh=_sc_mesh(), out_type=jax.ShapeDtypeStruct((n_slots, w), rows.dtype),
        scratch_types=[pltpu.VMEM((SC_CHUNK,), I32)] * TOP_K + [pltpu.VMEM((SC_CHUNK, w), rows.dtype),
                                                                pltpu.SemaphoreType.DMA],
        name="moe_dispatch")
    def scatter_rows(rows_hbm, dest_hbm, out_hbm, *scratch):
        idx_refs, buf, sem = scratch[:TOP_K], scratch[TOP_K], scratch[TOP_K + 1]
        base0 = _sc_worker_base(per_worker)

        @pl.loop(0, per_worker // SC_CHUNK)
        def _(j):
            base = base0 + j * SC_CHUNK
            pltpu.sync_copy(rows_hbm.at[pl.ds(base, SC_CHUNK)], buf)
            for k, idx in enumerate(idx_refs):
                pltpu.sync_copy(dest_hbm.at[pl.ds(k * n + base, SC_CHUNK)], idx)
            copies = [pltpu.make_async_copy(buf, out_hbm.at[idx], sem) for idx in idx_refs]
            for cp in copies:
                cp.start()
            for cp in copies:
                cp.wait()

    return scatter_rows(rows, dest)


def _sc_gather(table, idx):
    n = idx.shape[0]
    w = table.shape[1]
    per_worker = n // SC_WORKERS
    n_chunks = per_worker // SC_CHUNK
    assert per_worker * SC_WORKERS == n and n_chunks * SC_CHUNK == per_worker and n_chunks % 2 == 0

    @functools.partial(
        pl.kernel, mesh=_sc_mesh(), out_type=jax.ShapeDtypeStruct((n, w), table.dtype),
        scratch_types=([pltpu.VMEM((SC_CHUNK,), I32)] * 2 + [pltpu.VMEM((SC_CHUNK, w), table.dtype)] * 2
                       + [pltpu.SemaphoreType.DMA] * 4),
        name="moe_gather")
    def gather_rows(table_hbm, idx_hbm, out_hbm, idx0, idx1, buf0, buf1, gsem0, gsem1, wsem0, wsem1):
        base0 = _sc_worker_base(per_worker)

        def gather_copy(idx_v, buf, sem):
            return pltpu.make_async_copy(table_hbm.at[idx_v], buf, sem)

        def write_copy(j, buf, sem):
            return pltpu.make_async_copy(buf, out_hbm.at[pl.ds(base0 + j * SC_CHUNK, SC_CHUNK)], sem)

        def start_gather(j, idx_v, buf, sem):
            pltpu.sync_copy(idx_hbm.at[pl.ds(base0 + j * SC_CHUNK, SC_CHUNK)], idx_v)
            gather_copy(idx_v, buf, sem).start()

        start_gather(0, idx0, buf0, gsem0)

        @pl.loop(0, n_chunks, step=2)
        def _(j):
            @pl.when(j > 0)
            def _():
                write_copy(j - 1, buf1, wsem1).wait()
            start_gather(j + 1, idx1, buf1, gsem1)
            gather_copy(idx0, buf0, gsem0).wait()
            write_copy(j, buf0, wsem0).start()

            @pl.when(j + 2 < n_chunks)
            def _():
                write_copy(j, buf0, wsem0).wait()
                start_gather(j + 2, idx0, buf0, gsem0)
            gather_copy(idx1, buf1, gsem1).wait()
            write_copy(j + 1, buf1, wsem1).start()

        write_copy(n_chunks - 2, buf0, wsem0).wait()
        write_copy(n_chunks - 1, buf1, wsem1).wait()

    return gather_rows(table, idx)


def _moe(h2, e_tl, w_tl, r_tl, cnt, layer, w_gu, b_gu, w_down, b_down, tb=1):
    bsz, t, half = h2.shape
    n = bsz * t
    flat = lambda a: a.reshape(bsz // tb, -1, TOP_K, tb, a.shape[2] // tb).transpose(2, 0, 3, 1, 4).reshape(TOP_K, n)
    e_k, w_k, r_k = flat(e_tl), flat(w_tl), flat(r_tl)
    counts = cnt[:, 0].astype(I32)
    padded = (counts + MOE_BLOCK - 1) // MOE_BLOCK * MOE_BLOCK
    pad_end = jnp.cumsum(padded)
    pad_start = pad_end - padded
    n_blocks = (n * TOP_K + MOE_BLOCK - 1) // MOE_BLOCK + N_EXPERTS
    block_start = jnp.arange(n_blocks, dtype=I32) * MOE_BLOCK
    block_e = jnp.minimum(jnp.sum((pad_end[None, :] <= block_start[:, None]).astype(I32), axis=1), N_EXPERTS - 1)
    n_used = (pad_end[-1:] // MOE_BLOCK).astype(I32)
    start_k = jnp.sum(jnp.where(e_k[..., None] == jnp.arange(N_EXPERTS, dtype=I32), pad_start, 0), axis=-1)
    dest = (start_k + r_k).reshape(TOP_K * n)
    xb = _sc_dispatch(h2.reshape(n, half), dest, n_blocks * MOE_BLOCK)
    yb = _experts(xb, block_e, n_used, layer, w_gu, b_gu, w_down, b_down)
    return _sc_gather(yb, dest), w_k.T


def _rope_tables(n_ctx, n_lat):
    n_freq = RET_DK // 4
    inv_freq = ROPE_BASE ** (-jnp.arange(n_freq, dtype=F32) / n_freq)
    pos = jnp.arange(n_lat, dtype=I32)
    cos, sin = [], []
    for p in (pos // GRID_W, pos % GRID_W):
        ang = p.astype(F32)[:, None] * inv_freq
        cos += [jnp.cos(ang), jnp.cos(ang)]
        sin += [-jnp.sin(ang), jnp.sin(ang)]
    cos, sin = jnp.concatenate(cos, axis=1), jnp.concatenate(sin, axis=1)
    return (jnp.concatenate([jnp.ones((n_ctx, RET_DK), F32), cos], axis=0),
            jnp.concatenate([jnp.zeros((n_ctx, RET_DK), F32), sin], axis=0))


def kernel(x, c, ctx, c_ctx, ada_w, ada_b, norm1_g, norm2_g, ab_w_in, ab_w_out, gla_wa, gla_ba, gla_norm_g, s5_lam_re, s5_lam_im, s5_log_step, s5_b_re, s5_b_im, s5_c_re, s5_c_im, s5_d, s5_glu_w, s5_glu_b, ret_w_in, ret_w_out, ret_decay_logit, ret_norm_g, moe_w_router, moe_b_router, moe_w_gu, moe_b_gu, moe_w_down, moe_b_down, final_norm_g):
    bsz, n_lat, d = x.shape
    n_ctx = ctx.shape[1]
    depth = ada_w.shape[0]
    assert depth == 2 and d == D_MODEL and bsz == 8, "kernels are laid out for the stated problem shape"
    assert n_ctx % TOKEN_TILE == 0 and n_lat % LATENT_TILE == 0 and n_lat % GRID_W == 0
    t = n_ctx + n_lat
    nct = n_ctx // TOKEN_TILE

    cvec = jnp.zeros((16, d), F32).at[:bsz].set(c).at[bsz].set(c_ctx)
    mod = _ada_mod(cvec, ada_w, ada_b).reshape(depth, 16, 6, d)
    mods = [jnp.stack([jnp.broadcast_to(mod[l, bsz], (bsz, 6, d)), mod[l, :bsz]], axis=1) for l in range(depth)]

    w_in = ab_w_in[0].astype(BF16)
    cuts = [0, AB_QK, 2 * AB_QK, 2 * AB_QK + AB_V, 2 * AB_QK + 2 * AB_V, 2 * AB_QK + 2 * AB_V + 2 * GLA_RANK,
            w_in.shape[1]]
    pieces = [w_in[:, a:b] for a, b in zip(cuts[:-1], cuts[1:])]
    wa_pad = jnp.zeros((2, 2 * GLA_RANK, AB_QK), F32)
    wa_pad = wa_pad.at[0, :GLA_RANK].set(gla_wa[0, 0]).at[1, GLA_RANK:].set(gla_wa[0, 1])
    outs = _inproj0(ctx, x, mods[0], norm1_g[0], pieces, wa_pad, gla_ba[0].reshape(2, 1, AB_QK), nct)
    v, g, u = outs[8:]
    o_f, o_b = _gla((outs[0:4], outs[4:8]), v, n_ctx)
    ops = _s5_operators(s5_lam_re[0], s5_lam_im[0], s5_log_step[0], s5_b_re[0], s5_b_im[0], s5_c_re[0], s5_c_im[0])
    ys = _s5(u, ops, n_ctx)
    consts = [jnp.tile(gla_norm_g[0], GLA_HEADS).reshape(1, AB_V), s5_d[0].reshape(1, S5_CH),
              s5_glu_w[0].astype(BF16), s5_glu_b[0].reshape(1, S5_CH), ab_w_out[0].astype(BF16)]
    x1, h2, e_tl, w_tl, r_tl, cnt = _mix_call(
        functools.partial(_mix0_body, nct), "mix_gla_s5", (ctx, x), mods[0], t // TOKEN_TILE,
        [(o_f, False), (o_b, False), (g, False), (ys, False), (u, False)], consts,
        norm2_g[0], moe_w_router[0], moe_b_router[0], nct, 0, tb=TOKEN_TILE_BATCH)
    yk, w_tok = _moe(h2, e_tl, w_tl, r_tl, cnt, 0, moe_w_gu, moe_b_gu, moe_w_down, moe_b_down, tb=TOKEN_TILE_BATCH)

    w_in = ret_w_in[0].astype(BF16)
    cuts = [0, RET_QK, 2 * RET_QK, 2 * RET_QK + RET_MIX, w_in.shape[1]]
    pieces = [w_in[:, a:b] for a, b in zip(cuts[:-1], cuts[1:])]
    cos_t, sin_t = _rope_tables(n_ctx, n_lat)
    x2, q, k, v, g = _inproj1(x1, mods[0], yk, w_tok, mods[1], norm1_g[1], cos_t, sin_t, pieces, nct)
    o_f, o_b = _retention(q, k, v, ret_decay_logit[0], n_ctx)
    consts = [ret_norm_g[0].reshape(1, RET_MIX), ret_w_out[0].astype(BF16)]
    x1, h2, e_tl, w_tl, r_tl, cnt = _mix_call(
        _mix1_body, "mix_retention", (x2,), mods[1], n_lat // LATENT_TILE,
        [(o_f, False), (o_b, False), (g, False)], consts,
        norm2_g[1], moe_w_router[1], moe_b_router[1], 0, 0, tm=LATENT_TILE)
    yk, w_tok = _moe(h2, e_tl, w_tl, r_tl, cnt, 1, moe_w_gu, moe_b_gu, moe_w_down, moe_b_down)
    return _combine(x1, mods[1], yk, w_tok, 0, 0, final_norm_g, LATENT_TILE)
```

```python
import functools
import math

import jax
import jax.numpy as jnp
from jax import lax
from jax.experimental import pallas as pl
from jax.experimental.pallas import tpu as pltpu
from jax.experimental.pallas import tpu_sc as plsc

F32, BF16, I32, U32 = jnp.float32, jnp.bfloat16, jnp.int32, jnp.uint32

D_MODEL = 1024
GRID_W = 64
EPS = 1e-6
GLA_HEADS, GLA_DK, GLA_DV, GLA_RANK, GLA_TAU, GLA_CHUNK = 4, 64, 128, 16, 16.0, 64
GLA_BATCH = 8
AB_QK, AB_V = GLA_HEADS * GLA_DK, GLA_HEADS * GLA_DV
S5_CH, S5_GROUP, S5_GROUPS, S5_P = 512, 16, 32, 64
S5_CHUNK = 16
S5_FOLD_BATCH = 4
S5_SCAN_GROUPS = 2
RET_HEADS, RET_DK, RET_DV = 4, 256, 512
RET_CHUNK = 256
RET_BATCH = 2
RET_QK, RET_MIX = RET_HEADS * RET_DK, RET_HEADS * RET_DV
ROPE_BASE = 10000.0
N_EXPERTS, TOP_K, D_FF = 32, 4, 1024
SWIGLU_LIMIT, SWIGLU_ALPHA = 7.0, 1.702
MOE_BLOCK = 512
TOKEN_TILE = 256
LATENT_TILE = 512
TOKEN_TILE_BATCH = 2
ADA_TILE = 768
VMEM_LIMIT = 56 * 1024 * 1024
SC_CORES, SC_SUBCORES = 2, 16
SC_WORKERS = SC_CORES * SC_SUBCORES
SC_CHUNK = 64
LANES = 128

def _params(*sem):
    return pltpu.CompilerParams(dimension_semantics=sem, vmem_limit_bytes=VMEM_LIMIT)


def _dot(a, b):
    return jnp.dot(a, b, preferred_element_type=F32)


def _dot_nt(a, b):
    return lax.dot_general(a, b, (((1,), (1,)), ((), ())), preferred_element_type=F32)


def _dot_tn(a, b):
    return lax.dot_general(a, b, (((0,), (0,)), ((), ())), preferred_element_type=F32)


def _split(a):
    hi = a.astype(BF16)
    return hi, (a - hi.astype(F32)).astype(BF16)


def _dot3(a, b, dot=_dot):
    ah, al = _split(a)
    bh, bl = _split(b)
    return dot(ah, bh) + (dot(ah, bl) + dot(al, bh))


def _pack_rows(x):
    h = x.shape[1] // 2
    lo = lax.bitcast_convert_type(x[:, 0:h].astype(BF16).astype(F32), U32)
    hi = lax.bitcast_convert_type(x[:, h:2 * h].astype(BF16).astype(F32), U32)
    return hi | (lo >> 16)


def _unpack_rows(p):
    lo = lax.bitcast_convert_type(p << 16, F32)
    hi = lax.bitcast_convert_type(p & jnp.uint32(0xFFFF0000), F32)
    return lo, hi


def _silu(x):
    return x * jax.nn.sigmoid(x)


def _norm_mod(x, g, shift, scale):
    r = lax.rsqrt(jnp.mean(x * x, axis=-1, keepdims=True) + EPS)
    return (x * r * g) * (1.0 + scale) + shift


def _const_spec(shape):
    nd = len(shape)
    return pl.BlockSpec(shape, lambda *_: (0,) * nd, pipeline_mode=pl.Buffered(1))


def _ada_body(c_ref, w_ref, b_ref, o_ref):
    o_ref[0] = _dot3(_silu(c_ref[...]), w_ref[0]) + b_ref[0]


def _ada_mod(cvec, ada_w, ada_b):
    depth, d, n6 = ada_w.shape
    rows = cvec.shape[0]
    return pl.pallas_call(
        _ada_body,
        grid=(depth, n6 // ADA_TILE),
        in_specs=[_const_spec((rows, d)),
                  pl.BlockSpec((1, d, ADA_TILE), lambda l, j: (l, 0, j)),
                  pl.BlockSpec((1, 1, ADA_TILE), lambda l, j: (l, 0, j))],
        out_specs=pl.BlockSpec((1, rows, ADA_TILE), lambda l, j: (l, 0, j)),
        out_shape=jax.ShapeDtypeStruct((depth, rows, n6), F32),
        compiler_params=_params("arbitrary", "arbitrary"),
        name="ada_mod",
    )(cvec, ada_w, ada_b.reshape(depth, 1, n6))


def _split_specs(nct, d, tb=1):
    ctx_spec = pl.BlockSpec((tb, TOKEN_TILE, d), lambda b, i: (b, jnp.minimum(i, nct - 1), 0))
    lat_spec = pl.BlockSpec((tb, TOKEN_TILE, d), lambda b, i: (b, jnp.maximum(i - nct, 0), 0))
    return ctx_spec, lat_spec


def _stream_tile(nct, ctx_ref, lat_ref, bb=0):
    return jnp.where(pl.program_id(1) < nct, ctx_ref[bb], lat_ref[bb])


def _normed_rows(nct, ctx_ref, lat_ref, mod_ref, g_ref):
    tiles = []
    for bb in range(mod_ref.shape[0]):
        m = mod_ref[bb, 0]
        tiles.append(_norm_mod(_stream_tile(nct, ctx_ref, lat_ref, bb), g_ref[...], m[0:1], m[1:2]).astype(BF16))
    return tiles[0] if len(tiles) == 1 else jnp.concatenate(tiles, axis=0)


def _store_rows(o_ref, val):
    tm = o_ref.shape[1]
    for bb in range(o_ref.shape[0]):
        o_ref[bb] = val[bb * tm:(bb + 1) * tm].astype(o_ref.dtype)


def _inproj0_body(nct, ctx_ref, lat_ref, mod_ref, g_ref, wq, wk, wv, wg, wlow, wu, wa_ref, ba_ref, tri_ref, ones_ref,
                  qd_f, ki_f, ks_f, ed_f, qd_b, ki_b, ks_b, ed_b, ov, og, ou):
    h = _normed_rows(nct, ctx_ref, lat_ref, mod_ref, g_ref)
    tm = ctx_ref.shape[1]
    low = _dot(h, wlow[...])
    q = _dot(h, wq[...]) * (GLA_DK ** -0.5)
    k = _dot(h, wk[...])
    outs = ((qd_f, ki_f, ks_f, ed_f), (qd_b, ki_b, ks_b, ed_b))
    for d, (qd_ref, ki_ref, ks_ref, ed_ref) in enumerate(outs):
        z = _dot3(low, wa_ref[d]) + ba_ref[d]
        log_a = (jnp.minimum(z, 0.0) - jnp.log1p(jnp.exp(-jnp.abs(z)))) * (1.0 / GLA_TAU)
        la_hi, la_lo = _split(log_a)
        cums, tots = [], []
        for bb in range(qd_ref.shape[0]):
            hi, lo = la_hi[bb * tm:(bb + 1) * tm], la_lo[bb * tm:(bb + 1) * tm]
            cums.append(_dot(tri_ref[d], hi) + _dot(tri_ref[d], lo))
            tots.append(_dot(ones_ref[...], hi) + _dot(ones_ref[...], lo))
        cum = cums[0] if len(cums) == 1 else jnp.concatenate(cums, axis=0)
        tot = tots[0] if len(tots) == 1 else jnp.concatenate(tots, axis=0)
        _store_rows(qd_ref, q * jnp.exp(cum))
        _store_rows(ki_ref, k * jnp.exp(-cum))
        _store_rows(ks_ref, k * jnp.exp(tot - cum))
        for bb in range(ed_ref.shape[0]):
            for ch in range(tm // GLA_CHUNK):
                row = bb * tm + ch * GLA_CHUNK
                ed_ref[bb, ch] = jnp.exp(tot[row:row + 1])
    _store_rows(ov, _dot(h, wv[...]))
    _store_rows(og, _dot(h, wg[...]))
    u = _dot(h, wu[...])
    for qb in range(ou.shape[0]):
        _store_rows(ou.at[qb], u[:, qb * LANES:(qb + 1) * LANES])


def _inproj0(ctx, x, mods, norm_g, weights, wa_pad, ba, nct):
    bsz, n_lat, d = x.shape
    t = ctx.shape[1] + n_lat
    tm, tb = TOKEN_TILE, TOKEN_TILE_BATCH
    mod_spec = pl.BlockSpec((tb, 1, 6, d), lambda b, i: (b, (i >= nct).astype(I32), 0, 0))
    ctx_spec, lat_spec = _split_specs(nct, d, tb)
    pos = jnp.arange(tm)
    same_chunk = (pos[:, None] // GLA_CHUNK) == (pos[None, :] // GLA_CHUNK)
    tri = jnp.stack([same_chunk & (pos[None, :] <= pos[:, None]),
                     same_chunk & (pos[None, :] >= pos[:, None])]).astype(BF16)
    ones = same_chunk.astype(BF16)
    consts = list(weights) + [wa_pad, ba, tri, ones]
    tok = lambda w, dt: (pl.BlockSpec((tb, tm, w), lambda b, i: (b, i, 0)), jax.ShapeDtypeStruct((bsz, t, w), dt))
    per_chunk = (pl.BlockSpec((tb, tm // GLA_CHUNK, 1, AB_QK), lambda b, i: (b, i, 0, 0)),
                 jax.ShapeDtypeStruct((bsz, t // GLA_CHUNK, 1, AB_QK), F32))
    one_dir = [tok(AB_QK, BF16)] * 3 + [per_chunk]
    u_blocks = (pl.BlockSpec((S5_CH // LANES, tb, tm, LANES), lambda b, i: (0, b, i, 0)),
                jax.ShapeDtypeStruct((S5_CH // LANES, bsz, t, LANES), F32))
    outs = one_dir + one_dir + [tok(AB_V, BF16), tok(AB_V, BF16), u_blocks]
    return pl.pallas_call(
        functools.partial(_inproj0_body, nct),
        grid=(bsz // tb, t // tm),
        in_specs=[ctx_spec, lat_spec, mod_spec, _const_spec((1, d))] + [_const_spec(a.shape) for a in consts],
        out_specs=[o[0] for o in outs],
        out_shape=[o[1] for o in outs],
        compiler_params=_params("arbitrary", "arbitrary"),
        name="inproj_gla_s5",
    )(ctx, x, mods, norm_g.reshape(1, d), *consts)


def _rope(acc, cos_ref, sin_ref, o_ref, scale):
    tm = o_ref.shape[1]
    for grp in range(acc.shape[1] // LANES):
        half = grp % 2
        cs = cos_ref[:, half * LANES:(half + 1) * LANES]
        sn = sin_ref[:, half * LANES:(half + 1) * LANES]
        for bb in range(o_ref.shape[0]):
            xg = acc[bb * tm:(bb + 1) * tm, grp * LANES:(grp + 1) * LANES]
            out = xg * cs + pltpu.roll(xg, LANES // 2, 1) * sn
            o_ref[bb, :, grp * LANES:(grp + 1) * LANES] = (out * scale).astype(o_ref.dtype)


def _moe_residual(x1_ref, g2, yk_ref, w_ref, bb=0):
    d = x1_ref.shape[2]
    half = d // 2
    y_lo, y_hi = None, None
    for k in range(TOP_K):
        lo, hi = _unpack_rows(yk_ref[k, bb])
        wk = w_ref[bb, :, k:k + 1]
        y_lo = lo * wk if y_lo is None else y_lo + lo * wk
        y_hi = hi * wk if y_hi is None else y_hi + hi * wk
    return x1_ref[bb, :, 0:half] + g2[:, 0:half] * y_lo, x1_ref[bb, :, half:d] + g2[:, half:d] * y_hi


def _inproj1_body(x1_ref, mod0_ref, yk_ref, w_ref, mod_ref, g_ref, cos_ref, sin_ref, wq, wk, wv, wg, ox, oq, ok, ov, og):
    half = x1_ref.shape[2] // 2
    tiles = []
    for bb in range(x1_ref.shape[0]):
        x2_lo, x2_hi = _moe_residual(x1_ref, mod0_ref[bb, 0][5:6], yk_ref, w_ref, bb)
        ox[bb, :, 0:half] = x2_lo
        ox[bb, :, half:2 * half] = x2_hi
        m = mod_ref[bb, 0]
        tiles.append(_norm_mod(jnp.concatenate([x2_lo, x2_hi], axis=1), g_ref[...], m[0:1], m[1:2]).astype(BF16))
    h = tiles[0] if len(tiles) == 1 else jnp.concatenate(tiles, axis=0)
    _rope(_dot(h, wq[...]), cos_ref, sin_ref, oq, 1.0)
    _rope(_dot(h, wk[...]), cos_ref, sin_ref, ok, RET_DK ** -0.5)
    _store_rows(ov, _dot(h, wv[...]))
    _store_rows(og, _dot(h, wg[...]))


def _inproj1(x1, mods0, yk, w_tok, mods, norm_g, cos_t, sin_t, weights, nct):
    bsz, t, d = x1.shape
    tm, tb = TOKEN_TILE, TOKEN_TILE_BATCH
    mod_spec = pl.BlockSpec((tb, 1, 6, d), lambda b, i: (b, (i >= nct).astype(I32), 0, 0))
    tok = lambda w: pl.BlockSpec((tb, tm, w), lambda b, i: (b, i, 0))
    lat = lambda w: pl.BlockSpec((tb, tm, w), lambda b, i: (b, jnp.maximum(i - nct, 0), 0))
    tab_spec = pl.BlockSpec((tm, RET_DK), lambda b, i: (i, 0))
    wq, wk, wv, wg = weights
    n_lat = t - nct * tm
    return pl.pallas_call(
        _inproj1_body,
        grid=(bsz // tb, t // tm),
        in_specs=[tok(d), mod_spec, pl.BlockSpec((TOP_K, tb, tm, d // 2), lambda b, i: (0, b, i, 0)), tok(TOP_K),
                  mod_spec, _const_spec((1, d)), tab_spec, tab_spec] + [_const_spec(w.shape) for w in weights],
        out_specs=[lat(d), tok(wq.shape[1]), tok(wk.shape[1]), tok(wv.shape[1]), lat(wg.shape[1])],
        out_shape=[jax.ShapeDtypeStruct((bsz, n_lat, d), F32), jax.ShapeDtypeStruct((bsz, t, wq.shape[1]), BF16),
                   jax.ShapeDtypeStruct((bsz, t, wk.shape[1]), BF16), jax.ShapeDtypeStruct((bsz, t, wv.shape[1]), BF16),
                   jax.ShapeDtypeStruct((bsz, n_lat, wg.shape[1]), BF16)],
        compiler_params=_params("arbitrary", "arbitrary"),
        name="inproj_retention",
    )(x1, mods0, yk.reshape(TOP_K, bsz, t, d // 2), w_tok.reshape(bsz, t, TOP_K), mods, norm_g.reshape(1, d),
      cos_t, sin_t, *weights)


def _backward_chunk(n, n_ctx_chunks, n_chunks):
    return jnp.where(n < n_ctx_chunks, n_ctx_chunks - 1 - n, n_chunks - 1 - (n - n_ctx_chunks))


def _gla_body(qd_f, ki_f, ks_f, ed_f, v_f, qd_b, ki_b, ks_b, ed_b, v_b, hmask_ref, bdmask_ref, o_f, o_b, st_f, st_b):
    c = GLA_CHUNK

    @pl.when(pl.program_id(1) == 0)
    def _():
        st_f[...] = jnp.zeros_like(st_f)
        st_b[...] = jnp.zeros_like(st_b)

    r4 = lax.broadcasted_iota(I32, (GLA_HEADS * c, c), 0) & (c - 1)
    c4 = lax.broadcasted_iota(I32, (GLA_HEADS * c, c), 1)
    dirs = ((qd_f, ki_f, ks_f, ed_f, v_f, o_f, st_f), (qd_b, ki_b, ks_b, ed_b, v_b, o_b, st_b))
    chains = [(bb, d) + dirs[d] for bb in range(qd_f.shape[0]) for d in range(2)]
    scores, inter, grow = [], [], []
    for bb, d, qd_ref, ki_ref, ks_ref, ed_ref, v_ref, o_ref, st_ref in chains:
        q_dec = qd_ref[bb]
        q_heads = jnp.concatenate([q_dec] * GLA_HEADS, axis=0) * hmask_ref[...]
        seen4 = (c4 <= r4) if d == 0 else (c4 >= r4)
        scores.append(jnp.where(seen4, _dot_nt(q_heads, ki_ref[bb]), 0.0).astype(BF16))
        inter.append(_dot_nt(q_dec, st_ref[bb].astype(BF16)))
        grow.append(_dot_tn(v_ref[bb], ks_ref[bb]))
    for (bb, d, qd_ref, ki_ref, ks_ref, ed_ref, v_ref, o_ref, st_ref), sc, o_inter, dst in zip(chains, scores, inter, grow):
        v = v_ref[bb]
        o_intra = jnp.concatenate(
            [_dot(sc[h * c:(h + 1) * c], v[:, h * GLA_DV:(h + 1) * GLA_DV]) for h in range(GLA_HEADS)], axis=1)
        o_ref[bb] = o_intra + o_inter
        st_ref[bb] = st_ref[bb] * ed_ref[bb, 0] + bdmask_ref[...] * dst


def _gla(per_dir, v, n_ctx):
    bsz, t, _ = v.shape
    nc, ncc = t // GLA_CHUNK, n_ctx // GLA_CHUNK
    gb = GLA_BATCH
    fwd = lambda b, n: (b, n, 0)
    bwd = lambda b, n: (b, _backward_chunk(n, ncc, nc), 0)
    hmask = (jnp.arange(AB_QK)[:, None] // GLA_CHUNK == jnp.arange(AB_QK)[None, :] // GLA_DK).astype(BF16)
    bdmask = (jnp.arange(AB_V)[:, None] // GLA_DV == jnp.arange(AB_QK)[None, :] // GLA_DK).astype(F32)

    def specs(idx):
        idx4 = lambda b, n: idx(b, n) + (0,)
        return [pl.BlockSpec((gb, GLA_CHUNK, AB_QK), idx)] * 3 + [pl.BlockSpec((gb, 1, 1, AB_QK), idx4),
                                                                  pl.BlockSpec((gb, GLA_CHUNK, AB_V), idx)]

    return pl.pallas_call(
        _gla_body,
        grid=(bsz // gb, nc),
        in_specs=specs(fwd) + specs(bwd) + [_const_spec(hmask.shape), _const_spec(bdmask.shape)],
        out_specs=[pl.BlockSpec((gb, GLA_CHUNK, AB_V), fwd), pl.BlockSpec((gb, GLA_CHUNK, AB_V), bwd)],
        out_shape=[jax.ShapeDtypeStruct((bsz, t, AB_V), F32)] * 2,
        scratch_shapes=[pltpu.VMEM((gb, AB_V, AB_QK), F32)] * 2,
        compiler_params=_params("arbitrary", "arbitrary"),
        name="gla_scan",
    )(*per_dir[0], v, *per_dir[1], v, hmask, bdmask)


def _cmul(x, y):
    return x[0] * y[0] - x[1] * y[1], x[0] * y[1] + x[1] * y[0]


def _s5_operators(lam_re, lam_im, log_step, b_re, b_im, c_re, c_im):
    ln = S5_CHUNK
    step = jnp.exp(log_step.astype(F32))[..., None]
    lam_re, lam_im = lam_re.astype(F32), lam_im.astype(F32)
    mag = jnp.exp(lam_re * step)
    a = (mag * jnp.cos(lam_im * step), mag * jnp.sin(lam_im * step))
    den = lam_re * lam_re + lam_im * lam_im
    f_re = ((a[0] - 1.0) * lam_re + a[1] * lam_im) / den
    f_im = (a[1] * lam_re - (a[0] - 1.0) * lam_im) / den
    bt_re, bt_im = b_re.transpose(0, 2, 1), b_im.transpose(0, 2, 1)
    bb = _cmul((f_re[:, :, None, :], f_im[:, :, None, :]), (bt_re, bt_im))
    bbt = jnp.concatenate([bb[0], -bb[1]], axis=-1)
    pw = (a[0][:, :, None, :], a[1][:, :, None, :])
    while pw[0].shape[2] < ln:
        top = (pw[0][:, :, -1:, :], pw[1][:, :, -1:, :])
        nxt = _cmul(top, pw)
        pw = (jnp.concatenate([pw[0], nxt[0]], axis=2), jnp.concatenate([pw[1], nxt[1]], axis=2))
    pw = (jnp.concatenate([jnp.ones_like(pw[0][:, :, :1]), pw[0]], axis=2),
          jnp.concatenate([jnp.zeros_like(pw[1][:, :, :1]), pw[1]], axis=2))
    ca = _cmul((c_re[:, :, None], c_im[:, :, None]), (pw[0][:, :, :, None, :], pw[1][:, :, :, None, :]))
    by_dir = lambda arr, lo, flip_d: jnp.stack([jnp.flip(arr[d, :, lo:lo + ln], axis=1) if d == flip_d
                                                else arr[d, :, lo:lo + ln] for d in range(2)])
    rows = lambda arr: arr.reshape(2, S5_GROUPS, ln * S5_GROUP, 2 * S5_P)
    cab = rows(by_dir(jnp.concatenate([ca[0], ca[1]], axis=-1), 0, 1))
    cab2 = rows(by_dir(jnp.concatenate([ca[0], -ca[1]], axis=-1), 1, 1)).astype(BF16)
    pwx = by_dir(jnp.concatenate([pw[0], pw[1]], axis=-1), 0, 0)
    lr, li = pw[0][:, :, ln], pw[1][:, :, ln]
    ac_rows = [jnp.concatenate([lr, lr], -1), jnp.concatenate([-li, li], -1), jnp.concatenate([li, -li], -1)]
    ac = jnp.stack(ac_rows + [jnp.zeros_like(ac_rows[0])] * 5, axis=2)
    return cab, cab2, bbt, pwx, ac


def _s5_group_operators(gg, cab_ref, bbt_ref, pwx_ref, tz, wx):
    ln, ch, p = S5_CHUNK, S5_GROUP, S5_P
    lane = lax.broadcasted_iota(I32, (ch, ln * ch), 1)
    for d in range(2):
        kern = _dot3(bbt_ref[d, gg], cab_ref[d, gg], dot=_dot_nt)
        bt = bbt_ref[d, gg]
        b_re, b_im = bt[:, 0:p], -bt[:, p:2 * p]
        for j in range(ln):
            if d == 0:
                blk = jnp.where(lane >= j * ch, kern if j == 0 else pltpu.roll(kern, j * ch, 1), 0.0)
            else:
                blk = jnp.where(lane < (j + 1) * ch, kern if j == ln - 1 else pltpu.roll(kern, (j + 1) * ch, 1), 0.0)
            tz[gg, d, j * ch:(j + 1) * ch, :] = blk.astype(BF16)
            pr, pi = pwx_ref[d, gg, j:j + 1, 0:p], pwx_ref[d, gg, j:j + 1, p:2 * p]
            x_re, x_im = pr * b_re - pi * b_im, pr * b_im + pi * b_re
            wx[gg, d, j * ch:(j + 1) * ch, :] = jnp.concatenate([x_re, x_im, x_im, x_re], axis=1).astype(BF16)


def _s5_placement(pall):
    rows, cols = pall.shape[1], pall.shape[2]
    row = lax.broadcasted_iota(I32, (rows, cols), 0)
    col = lax.broadcasted_iota(I32, (rows, cols), 1)
    same_token = (row // LANES) == (col // S5_GROUP)
    for g8 in range(pall.shape[0]):
        pall[g8] = jnp.where(same_token & ((row % LANES) == g8 * S5_GROUP + (col % S5_GROUP)), 1.0, 0.0).astype(BF16)


def _first_step():
    return (pl.program_id(0) == 0) & (pl.program_id(1) == 0)


def _s5_fold_body(ncs, u_ref, o_ref, pall, ucat):
    @pl.when(_first_step())
    def _():
        _s5_placement(pall)

    for b in range(u_ref.shape[1]):
        for j in range(S5_CHUNK):
            ucat[b * ncs:(b + 1) * ncs, j * LANES:(j + 1) * LANES] = u_ref[0, b, pl.ds(j, ncs, stride=S5_CHUNK), :].astype(BF16)
    for g8 in range(pall.shape[0]):
        o_ref[g8] = _dot(ucat[...], pall[g8]).astype(BF16)


def _s5_unfold_body(ncs, y_ref, o_ref, pall):
    @pl.when(_first_step())
    def _():
        _s5_placement(pall)

    def token_pair(i2, carry):
        r0 = pl.multiple_of(i2 * 2 * LANES, 2 * LANES)
        acc = _dot_nt(y_ref[0], pall[0, pl.ds(r0, 2 * LANES), :])
        for g8 in range(1, pall.shape[0]):
            acc = acc + _dot_nt(y_ref[g8], pall[g8, pl.ds(r0, 2 * LANES), :])
        for b in range(o_ref.shape[1]):
            for par in range(2):
                o_ref[0, b, pl.ds(2 * i2 + par, ncs, stride=S5_CHUNK), :] = (
                    acc[b * ncs:(b + 1) * ncs, par * LANES:(par + 1) * LANES])
        return carry

    lax.fori_loop(0, S5_CHUNK // 2, token_pair, 0)


def _s5_body(ncs_ctx, ncs, rows, u_ref, cab_ref, cab2_ref, bbt_ref, pwx_ref, ac_ref, y_ref, tz, wx, *vecs):
    half = 2 * S5_P
    n_groups = u_ref.shape[0]
    groups = [vecs[6 * gg:6 * gg + 6] for gg in range(n_groups)]
    for gg, (xx_f, xs_f, xx_b, xs_b, _, _) in enumerate(groups):
        _s5_group_operators(gg, cab_ref, bbt_ref, pwx_ref, tz, wx)
        for d, (xx, xs) in enumerate(((xx_f, xs_f), (xx_b, xs_b))):
            r = _dot(u_ref[gg], wx[gg, d])
            xx[...] = r[:, :half]
            xs[...] = r[:, half:]

    def advance(ac, s, s_sw, x, x_sw):
        return ac[0:1] * s + ac[1:2] * s_sw + x, ac[0:1] * s_sw + ac[2:3] * s + x_sw

    def step(n, carry):
        at_f = pl.ds(n, rows, stride=ncs)
        at_b = pl.ds(_backward_chunk(n, ncs_ctx, ncs), rows, stride=ncs)
        out = []
        for gg, (xx_f, xs_f, xx_b, xs_b, sin_f, sin_b) in enumerate(groups):
            s_f, sw_f, s_b, sw_b = carry[4 * gg:4 * gg + 4]
            sin_f[at_f, :] = s_f
            sin_b[at_b, :] = s_b
            out += advance(ac_ref[0, gg], s_f, sw_f, xx_f[at_f, :], xs_f[at_f, :])
            out += advance(ac_ref[1, gg], s_b, sw_b, xx_b[at_b, :], xs_b[at_b, :])
        return tuple(out)

    zero = jnp.zeros((rows, half), F32)
    lax.fori_loop(0, ncs, step, (zero,) * (4 * n_groups))
    for gg, (_, _, _, _, sin_f, sin_b) in enumerate(groups):
        u = u_ref[gg]
        y_ref[gg] = (_dot(u, tz[gg, 0]) + _dot(u, tz[gg, 1]) + _dot_nt(sin_f[...].astype(BF16), cab2_ref[0, gg])
                     + _dot_nt(sin_b[...].astype(BF16), cab2_ref[1, gg])).astype(BF16)


def _s5(u4, ops, n_ctx):
    nq, bsz, t, _ = u4.shape
    ln, lanes = S5_CHUNK, S5_CHUNK * S5_GROUP
    gq = S5_GROUPS // nq
    ncs, ncs_ctx = t // ln, n_ctx // ln
    m = ncs * bsz
    hb = S5_FOLD_BATCH
    tok_spec = pl.BlockSpec((1, hb, t, LANES), lambda q, h: (q, h, 0, 0))
    grp_spec = pl.BlockSpec((gq, hb * ncs, lanes), lambda q, h: (q, h, 0))
    pall = pltpu.VMEM((gq, ln * LANES, lanes), BF16)
    ug = pl.pallas_call(
        functools.partial(_s5_fold_body, ncs),
        grid=(nq, bsz // hb),
        in_specs=[tok_spec],
        out_specs=grp_spec,
        out_shape=jax.ShapeDtypeStruct((S5_GROUPS, m, lanes), BF16),
        scratch_shapes=[pall, pltpu.VMEM((hb * ncs, ln * LANES), BF16)],
        compiler_params=_params("arbitrary", "arbitrary"),
        name="s5_fold",
    )(u4)
    sg = S5_SCAN_GROUPS
    dir_spec = lambda arr: pl.BlockSpec((2, sg) + arr.shape[2:], lambda g: (0, g, 0, 0))
    yg = pl.pallas_call(
        functools.partial(_s5_body, ncs_ctx, ncs, bsz),
        grid=(S5_GROUPS // sg,),
        in_specs=[pl.BlockSpec((sg, m, lanes), lambda g: (g, 0, 0))] + [dir_spec(arr) for arr in ops],
        out_specs=pl.BlockSpec((sg, m, lanes), lambda g: (g, 0, 0)),
        out_shape=jax.ShapeDtypeStruct((S5_GROUPS, m, lanes), BF16),
        scratch_shapes=[pltpu.VMEM((sg, 2, lanes, lanes), BF16)] * 2 + [pltpu.VMEM((m, 2 * S5_P), F32)] * (6 * sg),
        compiler_params=_params("arbitrary"),
        name="s5_scan",
    )(ug, *ops)
    return pl.pallas_call(
        functools.partial(_s5_unfold_body, ncs),
        grid=(nq, bsz // hb),
        in_specs=[grp_spec],
        out_specs=tok_spec,
        out_shape=jax.ShapeDtypeStruct(u4.shape, F32),
        scratch_shapes=[pall],
        compiler_params=_params("arbitrary", "arbitrary"),
        name="s5_unfold",
    )(yg)


def _ret_body(q_f, k_f, v_f, q_b, k_b, v_b, dmat_ref, rsc_ref, csc_ref, gam_ref, o_f, o_b, st_f, st_b):
    @pl.when(pl.program_id(1) == 0)
    def _():
        st_f[...] = jnp.zeros_like(st_f)
        st_b[...] = jnp.zeros_like(st_b)

    dirs = ((q_f, k_f, v_f, o_f, st_f), (q_b, k_b, v_b, o_b, st_b))
    for bb in range(q_f.shape[0]):
        for d, (q_ref, k_ref, v_ref, o_ref, st_ref) in enumerate(dirs):
            for h in range(RET_HEADS):
                qh = q_ref[bb, :, h * RET_DK:(h + 1) * RET_DK]
                kh = k_ref[bb, :, h * RET_DK:(h + 1) * RET_DK]
                vh = v_ref[bb, :, h * RET_DV:(h + 1) * RET_DV]
                st = st_ref[bb, h]
                scores = (_dot_nt(qh, kh) * dmat_ref[d, h]).astype(BF16)
                o = _dot(scores, vh) + rsc_ref[d, h] * _dot(qh, st.astype(BF16))
                o_ref[bb, :, h * RET_DV:(h + 1) * RET_DV] = o.astype(o_ref.dtype)
                k_state = (kh.astype(F32) * csc_ref[d, h]).astype(BF16)
                st_ref[bb, h] = st * gam_ref[d, h] + _dot_tn(k_state, vh)


def _retention(q, k, v, decay_logit, n_ctx):
    bsz, t, _ = q.shape
    c = RET_CHUNK
    nc, ncc = t // c, n_ctx // c
    nl = nc - ncc
    rb = RET_BATCH
    log_gamma = jax.nn.log_sigmoid(decay_logit.astype(F32))[:, :, None, None]
    i = jnp.arange(c, dtype=F32)
    lag = i[:, None] - i[None, :]
    lag = jnp.stack([lag, -lag])[:, None]
    dmat = jnp.where(lag >= 0, jnp.exp(log_gamma * jnp.maximum(lag, 0.0)), 0.0)
    done = jnp.stack([i + 1.0, c - i])[:, None, :, None]
    rsc = jnp.exp(log_gamma * done)
    csc = jnp.exp(log_gamma * (c - done))
    gam = jnp.exp(log_gamma[:, :, 0, 0] * c)
    fwd = lambda b, n: (b, n, 0)
    bwd = lambda b, n: (b, _backward_chunk(n, ncc, nc), 0)
    o_fwd = lambda b, n: (b, jnp.maximum(n - ncc, 0), 0)
    o_bwd = lambda b, n: (b, nl - 1 - jnp.maximum(n - ncc, 0), 0)

    def specs(idx):
        return [pl.BlockSpec((rb, c, RET_QK), idx), pl.BlockSpec((rb, c, RET_QK), idx), pl.BlockSpec((rb, c, RET_MIX), idx)]

    return pl.pallas_call(
        _ret_body,
        grid=(bsz // rb, nc),
        in_specs=specs(fwd) + specs(bwd) + [_const_spec(dmat.shape), _const_spec(rsc.shape), _const_spec(csc.shape),
                                            pl.BlockSpec(memory_space=pltpu.SMEM)],
        out_specs=[pl.BlockSpec((rb, c, RET_MIX), o_fwd), pl.BlockSpec((rb, c, RET_MIX), o_bwd)],
        out_shape=[jax.ShapeDtypeStruct((bsz, nl * c, RET_MIX), BF16)] * 2,
        scratch_shapes=[pltpu.VMEM((rb, RET_HEADS, RET_DK, RET_DV), F32)] * 2,
        compiler_params=_params("arbitrary", "arbitrary"),
        name="retention_scan",
    )(q, k, v, q, k, v, dmat, rsc, csc, gam)


def _zero_counts_at_start(cnt_ref):
    @pl.when(_first_step())
    def _():
        cnt_ref[...] = jnp.zeros_like(cnt_ref)


def _route(xs, mixed, mods, n2g_ref, wr_ref, br_ref, x1_ref, h2_ref, e_ref, w_ref, r_ref, cnt_ref):
    rows = xs[0].shape[0]
    h2s = []
    for bb, (x, mod) in enumerate(zip(xs, mods)):
        x1 = x + mod[2:3] * mixed[bb * rows:(bb + 1) * rows]
        x1_ref[bb] = x1
        h2s.append(_norm_mod(x1, n2g_ref[...], mod[3:4], mod[4:5]))
        h2_ref[bb] = _pack_rows(h2s[bb])
    h2 = h2s[0] if len(h2s) == 1 else jnp.concatenate(h2s, axis=0)
    tm = h2.shape[0]
    logits = _dot3(wr_ref[...], h2, dot=_dot_nt) + br_ref[...]
    ie = lax.broadcasted_iota(I32, logits.shape, 0)
    tops, picks = [], []
    for _ in range(TOP_K):
        mx = jnp.max(logits, axis=0, keepdims=True)
        pick = jnp.min(jnp.where(logits == mx, ie, N_EXPERTS), axis=0, keepdims=True)
        tops.append(mx)
        picks.append(pick)
        logits = jnp.where(ie == pick, -jnp.inf, logits)
    ex = [jnp.exp(tk - tops[0]) for tk in tops]
    den = ex[0] + ex[1] + ex[2] + ex[3]
    for kk in range(TOP_K):
        w_ref[0, kk:kk + 1, :] = ex[kk] / den
        e_ref[0, kk:kk + 1, :] = picks[kk]

    earlier = (lax.broadcasted_iota(I32, (tm, tm), 0) < lax.broadcasted_iota(I32, (tm, tm), 1))
    earlier = jnp.where(earlier, 1.0, 0.0).astype(BF16)
    run = cnt_ref[:, 0:1]
    for kk, pick in enumerate(picks):
        onehot = jnp.where(ie == pick, 1.0, 0.0)
        before = _dot(onehot.astype(BF16), earlier) + run
        r_ref[0, kk:kk + 1, :] = jnp.sum(onehot * before, axis=0, keepdims=True).astype(I32)
        run = run + jnp.sum(onehot, axis=1, keepdims=True)
    cnt_ref[...] = jnp.broadcast_to(run, cnt_ref.shape)


def _mix0_body(nct, ctx_ref, lat_ref, mod_ref, of_ref, ob_ref, g_ref, ys_ref, u_ref, gng_ref, dsk_ref, gluw_ref,
               glub_ref, wo_ref, n2g_ref, wr_ref, br_ref, x1_ref, h2_ref, e_ref, w_ref, r_ref, cnt_ref):
    _zero_counts_at_start(cnt_ref)
    tb, tm = of_ref.shape[0], of_ref.shape[1]
    rows = lambda ref: ref[...].reshape(tb * tm, ref.shape[-1])
    o = rows(of_ref) + rows(ob_ref)
    heads = []
    for h in range(GLA_HEADS):
        oh = o[:, h * GLA_DV:(h + 1) * GLA_DV]
        heads.append(oh * lax.rsqrt(jnp.mean(oh * oh, axis=-1, keepdims=True) + EPS))
    gla = jnp.concatenate(heads, axis=1) * gng_ref[...] * _silu(rows(g_ref).astype(F32))
    lane_blocks = lambda ref: jnp.concatenate([ref[qb].reshape(tb * tm, LANES) for qb in range(ref.shape[0])], axis=1)
    y = jax.nn.gelu(lane_blocks(ys_ref) + dsk_ref[...] * lane_blocks(u_ref))
    y = y * jax.nn.sigmoid(_dot(y.astype(BF16), gluw_ref[...]) + glub_ref[...])
    mixed = _dot(gla.astype(BF16), wo_ref[0:AB_V]) + _dot(y.astype(BF16), wo_ref[AB_V:AB_V + S5_CH])
    _route([_stream_tile(nct, ctx_ref, lat_ref, bb) for bb in range(tb)], mixed, [mod_ref[bb, 0] for bb in range(tb)],
           n2g_ref, wr_ref, br_ref, x1_ref, h2_ref, e_ref, w_ref, r_ref, cnt_ref)


def _mix1_body(x_ref, mod_ref, of_ref, ob_ref, g_ref, ng_ref, wo_ref, n2g_ref, wr_ref, br_ref,
               x1_ref, h2_ref, e_ref, w_ref, r_ref, cnt_ref):
    _zero_counts_at_start(cnt_ref)
    mixed = None
    for h in range(RET_HEADS):
        sl = slice(h * RET_DV, (h + 1) * RET_DV)
        oh = of_ref[0, :, sl].astype(F32) + ob_ref[0, :, sl].astype(F32)
        mu = jnp.mean(oh, axis=-1, keepdims=True)
        cen = oh - mu
        var = jnp.mean(cen * cen, axis=-1, keepdims=True)
        gated = cen * lax.rsqrt(var + EPS) * ng_ref[:, sl] * _silu(g_ref[0, :, sl].astype(F32))
        part = _dot(gated.astype(BF16), wo_ref[sl])
        mixed = part if mixed is None else mixed + part
    _route([x_ref[0]], mixed, [mod_ref[0, 0]], n2g_ref, wr_ref, br_ref, x1_ref, h2_ref, e_ref, w_ref, r_ref, cnt_ref)


def _mix_call(body, name, stream, mods, tiles, acts, consts, norm2_g, w_router, b_router, n_tok, seg_tile0,
              tm=TOKEN_TILE, tb=1):
    bsz, _, d = stream[-1].shape
    off = lambda b, i: (b, i + seg_tile0, 0)
    loc = lambda b, i: (b, i, 0)
    ntl = bsz // tb * tiles
    flat = lambda b, i: (b * tiles + i, 0, 0)
    in_specs = list(_split_specs(n_tok, d, tb)) if len(stream) == 2 else [pl.BlockSpec((tb, tm, d), off)]
    in_specs.append(pl.BlockSpec((tb, 1, 6, d), lambda b, i: (b, ((i + seg_tile0) >= n_tok).astype(I32), 0, 0)))
    args = list(stream) + [mods]
    for arr, offset in acts:
        if arr.ndim == 4:
            in_specs.append(pl.BlockSpec((arr.shape[0], tb, tm, arr.shape[3]), lambda b, i: (0, b, i, 0)))
        else:
            in_specs.append(pl.BlockSpec((tb, tm, arr.shape[2]), off if offset else loc))
        args.append(arr)
    tail = list(consts) + [norm2_g.reshape(1, d), w_router.T, b_router.reshape(N_EXPERTS, 1)]
    in_specs += [_const_spec(a.shape) for a in tail]
    args += tail
    tok_out = pl.BlockSpec((1, TOP_K, tb * tm), flat)
    return pl.pallas_call(
        body,
        grid=(bsz // tb, tiles),
        in_specs=in_specs,
        out_specs=[pl.BlockSpec((tb, tm, d), loc), pl.BlockSpec((tb, tm, d // 2), loc), tok_out, tok_out, tok_out,
                   _const_spec((N_EXPERTS, LANES))],
        out_shape=[jax.ShapeDtypeStruct((bsz, tiles * tm, d), F32), jax.ShapeDtypeStruct((bsz, tiles * tm, d // 2), U32),
                   jax.ShapeDtypeStruct((ntl, TOP_K, tb * tm), I32), jax.ShapeDtypeStruct((ntl, TOP_K, tb * tm), F32),
                   jax.ShapeDtypeStruct((ntl, TOP_K, tb * tm), I32), jax.ShapeDtypeStruct((N_EXPERTS, LANES), F32)],
        compiler_params=_params("arbitrary", "arbitrary"),
        name=name,
    )(*args)


def _cast_rows(src_ref, dst_ref, rows):
    def chunk(j, carry):
        r = pl.multiple_of(j * rows, rows)
        dst_ref[pl.ds(r, rows), :] = src_ref[0, 0, pl.ds(r, rows), :].astype(BF16)
        return carry

    lax.fori_loop(0, dst_ref.shape[0] // rows, chunk, 0)


def _expert_body(be_ref, nu_ref, x_ref, wgu_ref, bgu_ref, wd_ref, bd_ref, o_ref, wgu_bf, wd_bf):
    i = pl.program_id(0)
    live = i < nu_ref[0]
    new_expert = (i == 0) | (be_ref[i] != be_ref[jnp.maximum(i - 1, 0)])

    @pl.when(live & new_expert)
    def _():
        _cast_rows(wgu_ref, wgu_bf, 128)
        _cast_rows(wd_ref, wd_bf, 128)

    @pl.when(live)
    def _():
        x_lo, x_hi = _unpack_rows(x_ref[...])
        half = x_lo.shape[1]
        gu = (_dot(x_lo.astype(BF16), wgu_bf[0:half]) + _dot(x_hi.astype(BF16), wgu_bf[half:2 * half])
              + bgu_ref[0, 0])
        gate = jnp.minimum(gu[:, :D_FF], SWIGLU_LIMIT)
        lin = jnp.clip(gu[:, D_FF:], -SWIGLU_LIMIT, SWIGLU_LIMIT)
        act = gate * jax.nn.sigmoid(SWIGLU_ALPHA * gate) * (lin + 1.0)
        y = _dot(act.astype(BF16), wd_bf[...]) + bd_ref[0, 0]
        o_ref[...] = _pack_rows(y)

    @pl.when(i >= nu_ref[0])
    def _():
        o_ref[...] = jnp.zeros_like(o_ref)


def _experts(xb, block_e, n_used, layer, w_gu, b_gu, w_down, b_down):
    n_slots, half = xb.shape
    d = 2 * half
    n_blocks = n_slots // MOE_BLOCK
    depth = w_gu.shape[0]
    by_expert = lambda i, be, nu: (layer, be[i], 0, 0)
    return pl.pallas_call(
        _expert_body,
        grid_spec=pltpu.PrefetchScalarGridSpec(
            num_scalar_prefetch=2,
            grid=(n_blocks,),
            in_specs=[pl.BlockSpec((MOE_BLOCK, half), lambda i, be, nu: (i, 0)),
                      pl.BlockSpec((1, 1, d, 2 * D_FF), by_expert), pl.BlockSpec((1, 1, 1, 2 * D_FF), by_expert),
                      pl.BlockSpec((1, 1, D_FF, d), by_expert), pl.BlockSpec((1, 1, 1, d), by_expert)],
            out_specs=pl.BlockSpec((MOE_BLOCK, half), lambda i, be, nu: (i, 0)),
            scratch_shapes=[pltpu.VMEM((d, 2 * D_FF), BF16), pltpu.VMEM((D_FF, d), BF16)]),
        out_shape=jax.ShapeDtypeStruct((n_slots, half), U32),
        compiler_params=_params("arbitrary"),
        name="moe_experts",
    )(block_e, n_used, xb, w_gu, b_gu.reshape(depth, N_EXPERTS, 1, 2 * D_FF), w_down,
      b_down.reshape(depth, N_EXPERTS, 1, d))


def _combine_body(x1_ref, mod_ref, yk_ref, w_ref, fg_ref, o_ref):
    d = x1_ref.shape[2]
    half = d // 2
    x2_lo, x2_hi = _moe_residual(x1_ref, mod_ref[0, 0][5:6], yk_ref, w_ref)
    ms = (jnp.sum(x2_lo * x2_lo, axis=-1, keepdims=True) + jnp.sum(x2_hi * x2_hi, axis=-1, keepdims=True)) / d
    r = lax.rsqrt(ms + EPS)
    o_ref[0, :, 0:half] = x2_lo * r * fg_ref[:, 0:half]
    o_ref[0, :, half:d] = x2_hi * r * fg_ref[:, half:d]


def _combine(x1, mods, yk, w_tok, seg_tile0, n_tok, final_g, tm):
    bsz, t, d = x1.shape
    loc = lambda b, i: (b, i, 0)
    return pl.pallas_call(
        _combine_body,
        grid=(bsz, t // tm),
        in_specs=[pl.BlockSpec((1, tm, d), loc),
                  pl.BlockSpec((1, 1, 6, d), lambda b, i: (b, ((i + seg_tile0) >= n_tok).astype(I32), 0, 0)),
                  pl.BlockSpec((TOP_K, 1, tm, d // 2), lambda b, i: (0, b, i, 0)),
                  pl.BlockSpec((1, tm, TOP_K), loc), _const_spec((1, d))],
        out_specs=pl.BlockSpec((1, tm, d), loc),
        out_shape=jax.ShapeDtypeStruct((bsz, t, d), F32),
        compiler_params=_params("arbitrary", "arbitrary"),
        name="moe_combine",
    )(x1, mods, yk.reshape(TOP_K, bsz, t, d // 2), w_tok.reshape(bsz, t, TOP_K), final_g.reshape(1, d))


def _sc_mesh():
    return plsc.VectorSubcoreMesh(core_axis_name="core", subcore_axis_name="subcore",
                                  num_cores=SC_CORES, num_subcores=SC_SUBCORES)


def _sc_worker_base(per_worker):
    return (lax.axis_index("subcore") * SC_CORES + lax.axis_index("core")) * per_worker


def _sc_dispatch(rows, dest, n_slots):
    n, w = rows.shape
    per_worker = n // SC_WORKERS
    assert per_worker * SC_WORKERS == n and per_worker % SC_CHUNK == 0

    @functools.partial(
        pl.kernel, mesh=_sc_mesh(), out_type=jax.ShapeDtypeStruct((n_slots, w), rows.dtype),
        scratch_types=[pltpu.VMEM((SC_CHUNK,), I32)] * TOP_K + [pltpu.VMEM((SC_CHUNK, w), rows.dtype),
                                                                pltpu.SemaphoreType.DMA],
        name="moe_dispatch")
    def scatter_rows(rows_hbm, dest_hbm, out_hbm, *scratch):
        idx_refs, buf, sem = scratch[:TOP_K], scratch[TOP_K], scratch[TOP_K + 1]
        base0 = _sc_worker_base(per_worker)

        @pl.loop(0, per_worker // SC_CHUNK)
        def _(j):
            base = base0 + j * SC_CHUNK
            pltpu.sync_copy(rows_hbm.at[pl.ds(base, SC_CHUNK)], buf)
            for k, idx in enumerate(idx_refs):
                pltpu.sync_copy(dest_hbm.at[pl.ds(k * n + base, SC_CHUNK)], idx)
            copies = [pltpu.make_async_copy(buf, out_hbm.at[idx], sem) for idx in idx_refs]
            for cp in copies:
                cp.start()
            for cp in copies:
                cp.wait()

    return scatter_rows(rows, dest)


def _sc_gather(table, idx):
    n = idx.shape[0]
    w = table.shape[1]
    per_worker = n // SC_WORKERS
    n_chunks = per_worker // SC_CHUNK
    assert per_worker * SC_WORKERS == n and n_chunks * SC_CHUNK == per_worker and n_chunks % 2 == 0

    @functools.partial(
        pl.kernel, mesh=_sc_mesh(), out_type=jax.ShapeDtypeStruct((n, w), table.dtype),
        scratch_types=([pltpu.VMEM((SC_CHUNK,), I32)] * 2 + [pltpu.VMEM((SC_CHUNK, w), table.dtype)] * 2
                       + [pltpu.SemaphoreType.DMA] * 4),
        name="moe_gather")
    def gather_rows(table_hbm, idx_hbm, out_hbm, idx0, idx1, buf0, buf1, gsem0, gsem1, wsem0, wsem1):
        base0 = _sc_worker_base(per_worker)

        def gather_copy(idx_v, buf, sem):
            return pltpu.make_async_copy(table_hbm.at[idx_v], buf, sem)

        def write_copy(j, buf, sem):
            return pltpu.make_async_copy(buf, out_hbm.at[pl.ds(base0 + j * SC_CHUNK, SC_CHUNK)], sem)

        def start_gather(j, idx_v, buf, sem):
            pltpu.sync_copy(idx_hbm.at[pl.ds(base0 + j * SC_CHUNK, SC_CHUNK)], idx_v)
            gather_copy(idx_v, buf, sem).start()

        start_gather(0, idx0, buf0, gsem0)

        @pl.loop(0, n_chunks, step=2)
        def _(j):
            @pl.when(j > 0)
            def _():
                write_copy(j - 1, buf1, wsem1).wait()
            start_gather(j + 1, idx1, buf1, gsem1)
            gather_copy(idx0, buf0, gsem0).wait()
            write_copy(j, buf0, wsem0).start()

            @pl.when(j + 2 < n_chunks)
            def _():
                write_copy(j, buf0, wsem0).wait()
                start_gather(j + 2, idx0, buf0, gsem0)
            gather_copy(idx1, buf1, gsem1).wait()
            write_copy(j + 1, buf1, wsem1).start()

        write_copy(n_chunks - 2, buf0, wsem0).wait()
        write_copy(n_chunks - 1, buf1, wsem1).wait()

    return gather_rows(table, idx)


def _moe(h2, e_tl, w_tl, r_tl, cnt, layer, w_gu, b_gu, w_down, b_down, tb=1):
    bsz, t, half = h2.shape
    n = bsz * t
    flat = lambda a: a.reshape(bsz // tb, -1, TOP_K, tb, a.shape[2] // tb).transpose(2, 0, 3, 1, 4).reshape(TOP_K, n)
    e_k, w_k, r_k = flat(e_tl), flat(w_tl), flat(r_tl)
    counts = cnt[:, 0].astype(I32)
    padded = (counts + MOE_BLOCK - 1) // MOE_BLOCK * MOE_BLOCK
    pad_end = jnp.cumsum(padded)
    pad_start = pad_end - padded
    n_blocks = (n * TOP_K + MOE_BLOCK - 1) // MOE_BLOCK + N_EXPERTS
    block_start = jnp.arange(n_blocks, dtype=I32) * MOE_BLOCK
    block_e = jnp.minimum(jnp.sum((pad_end[None, :] <= block_start[:, None]).astype(I32), axis=1), N_EXPERTS - 1)
    n_used = (pad_end[-1:] // MOE_BLOCK).astype(I32)
    start_k = jnp.sum(jnp.where(e_k[..., None] == jnp.arange(N_EXPERTS, dtype=I32), pad_start, 0), axis=-1)
    dest = (start_k + r_k).reshape(TOP_K * n)
    xb = _sc_dispatch(h2.reshape(n, half), dest, n_blocks * MOE_BLOCK)
    yb = _experts(xb, block_e, n_used, layer, w_gu, b_gu, w_down, b_down)
    return _sc_gather(yb, dest), w_k.T


def _rope_tables(n_ctx, n_lat):
    n_freq = RET_DK // 4
    inv_freq = ROPE_BASE ** (-jnp.arange(n_freq, dtype=F32) / n_freq)
    pos = jnp.arange(n_lat, dtype=I32)
    cos, sin = [], []
    for p in (pos // GRID_W, pos % GRID_W):
        ang = p.astype(F32)[:, None] * inv_freq
        cos += [jnp.cos(ang), jnp.cos(ang)]
        sin += [-jnp.sin(ang), jnp.sin(ang)]
    cos, sin = jnp.concatenate(cos, axis=1), jnp.concatenate(sin, axis=1)
    return (jnp.concatenate([jnp.ones((n_ctx, RET_DK), F32), cos], axis=0),
            jnp.concatenate([jnp.zeros((n_ctx, RET_DK), F32), sin], axis=0))


def kernel(x, c, ctx, c_ctx, ada_w, ada_b, norm1_g, norm2_g, ab_w_in, ab_w_out, gla_wa, gla_ba, gla_norm_g, s5_lam_re, s5_lam_im, s5_log_step, s5_b_re, s5_b_im, s5_c_re, s5_c_im, s5_d, s5_glu_w, s5_glu_b, ret_w_in, ret_w_out, ret_decay_logit, ret_norm_g, moe_w_router, moe_b_router, moe_w_gu, moe_b_gu, moe_w_down, moe_b_down, final_norm_g):
    bsz, n_lat, d = x.shape
    n_ctx = ctx.shape[1]
    depth = ada_w.shape[0]
    assert depth == 2 and d == D_MODEL and bsz == 8, "kernels are laid out for the stated problem shape"
    assert n_ctx % TOKEN_TILE == 0 and n_lat % LATENT_TILE == 0 and n_lat % GRID_W == 0
    t = n_ctx + n_lat
    nct = n_ctx // TOKEN_TILE

    cvec = jnp.zeros((16, d), F32).at[:bsz].set(c).at[bsz].set(c_ctx)
    mod = _ada_mod(cvec, ada_w, ada_b).reshape(depth, 16, 6, d)
    mods = [jnp.stack([jnp.broadcast_to(mod[l, bsz], (bsz, 6, d)), mod[l, :bsz]], axis=1) for l in range(depth)]

    w_in = ab_w_in[0].astype(BF16)
    cuts = [0, AB_QK, 2 * AB_QK, 2 * AB_QK + AB_V, 2 * AB_QK + 2 * AB_V, 2 * AB_QK + 2 * AB_V + 2 * GLA_RANK,
            w_in.shape[1]]
    pieces = [w_in[:, a:b] for a, b in zip(cuts[:-1], cuts[1:])]
    wa_pad = jnp.zeros((2, 2 * GLA_RANK, AB_QK), F32)
    wa_pad = wa_pad.at[0, :GLA_RANK].set(gla_wa[0, 0]).at[1, GLA_RANK:].set(gla_wa[0, 1])
    outs = _inproj0(ctx, x, mods[0], norm1_g[0], pieces, wa_pad, gla_ba[0].reshape(2, 1, AB_QK), nct)
    v, g, u = outs[8:]
    o_f, o_b = _gla((outs[0:4], outs[4:8]), v, n_ctx)
    ops = _s5_operators(s5_lam_re[0], s5_lam_im[0], s5_log_step[0], s5_b_re[0], s5_b_im[0], s5_c_re[0], s5_c_im[0])
    ys = _s5(u, ops, n_ctx)
    consts = [jnp.tile(gla_norm_g[0], GLA_HEADS).reshape(1, AB_V), s5_d[0].reshape(1, S5_CH),
              s5_glu_w[0].astype(BF16), s5_glu_b[0].reshape(1, S5_CH), ab_w_out[0].astype(BF16)]
    x1, h2, e_tl, w_tl, r_tl, cnt = _mix_call(
        functools.partial(_mix0_body, nct), "mix_gla_s5", (ctx, x), mods[0], t // TOKEN_TILE,
        [(o_f, False), (o_b, False), (g, False), (ys, False), (u, False)], consts,
        norm2_g[0], moe_w_router[0], moe_b_router[0], nct, 0, tb=TOKEN_TILE_BATCH)
    yk, w_tok = _moe(h2, e_tl, w_tl, r_tl, cnt, 0, moe_w_gu, moe_b_gu, moe_w_down, moe_b_down, tb=TOKEN_TILE_BATCH)

    w_in = ret_w_in[0].astype(BF16)
    cuts = [0, RET_QK, 2 * RET_QK, 2 * RET_QK + RET_MIX, w_in.shape[1]]
    pieces = [w_in[:, a:b] for a, b in zip(cuts[:-1], cuts[1:])]
    cos_t, sin_t = _rope_tables(n_ctx, n_lat)
    x2, q, k, v, g = _inproj1(x1, mods[0], yk, w_tok, mods[1], norm1_g[1], cos_t, sin_t, pieces, nct)
    o_f, o_b = _retention(q, k, v, ret_decay_logit[0], n_ctx)
    consts = [ret_norm_g[0].reshape(1, RET_MIX), ret_w_out[0].astype(BF16)]
    x1, h2, e_tl, w_tl, r_tl, cnt = _mix_call(
        _mix1_body, "mix_retention", (x2,), mods[1], n_lat // LATENT_TILE,
        [(o_f, False), (o_b, False), (g, False)], consts,
        norm2_g[1], moe_w_router[1], moe_b_router[1], 0, 0, tm=LATENT_TILE)
    yk, w_tok = _moe(h2, e_tl, w_tl, r_tl, cnt, 1, moe_w_gu, moe_b_gu, moe_w_down, moe_b_down)
    return _combine(x1, mods[1], yk, w_tok, 0, 0, final_norm_g, LATENT_TILE)
```

```python
import functools
import math

import jax
import jax.numpy as jnp
from jax import lax
from jax.experimental import pallas as pl
from jax.experimental.pallas import tpu as pltpu
from jax.experimental.pallas import tpu_sc as plsc

F32, BF16, I32, U32 = jnp.float32, jnp.bfloat16, jnp.int32, jnp.uint32

D_MODEL = 1024
GRID_W = 64
EPS = 1e-6
GLA_HEADS, GLA_DK, GLA_DV, GLA_RANK, GLA_TAU, GLA_CHUNK = 4, 64, 128, 16, 16.0, 64
GLA_BATCH = 8
AB_QK, AB_V = GLA_HEADS * GLA_DK, GLA_HEADS * GLA_DV
S5_CH, S5_GROUP, S5_GROUPS, S5_P = 512, 16, 32, 64
S5_CHUNK = 16
S5_FOLD_BATCH = 4
S5_SCAN_GROUPS = 2
RET_HEADS, RET_DK, RET_DV = 4, 256, 512
RET_CHUNK = 256
RET_BATCH = 2
RET_QK, RET_MIX = RET_HEADS * RET_DK, RET_HEADS * RET_DV
ROPE_BASE = 10000.0
N_EXPERTS, TOP_K, D_FF = 32, 4, 1024
SWIGLU_LIMIT, SWIGLU_ALPHA = 7.0, 1.702
MOE_BLOCK = 512
TOKEN_TILE = 256
LATENT_TILE = 512
TOKEN_TILE_BATCH = 2
ADA_TILE = 768
VMEM_LIMIT = 56 * 1024 * 1024
SC_CORES, SC_SUBCORES = 2, 16
SC_WORKERS = SC_CORES * SC_SUBCORES
SC_CHUNK = 64
LANES = 128

def _params(*sem):
    return pltpu.CompilerParams(dimension_semantics=sem, vmem_limit_bytes=VMEM_LIMIT)


def _dot(a, b):
    return jnp.dot(a, b, preferred_element_type=F32)


def _dot_nt(a, b):
    return lax.dot_general(a, b, (((1,), (1,)), ((), ())), preferred_element_type=F32)


def _dot_tn(a, b):
    return lax.dot_general(a, b, (((0,), (0,)), ((), ())), preferred_element_type=F32)


def _split(a):
    hi = a.astype(BF16)
    return hi, (a - hi.astype(F32)).astype(BF16)


def _dot3(a, b, dot=_dot):
    ah, al = _split(a)
    bh, bl = _split(b)
    return dot(ah, bh) + (dot(ah, bl) + dot(al, bh))


def _pack_rows(x):
    h = x.shape[1] // 2
    lo = lax.bitcast_convert_type(x[:, 0:h].astype(BF16).astype(F32), U32)
    hi = lax.bitcast_convert_type(x[:, h:2 * h].astype(BF16).astype(F32), U32)
    return hi | (lo >> 16)


def _unpack_rows(p):
    lo = lax.bitcast_convert_type(p << 16, F32)
    hi = lax.bitcast_convert_type(p & jnp.uint32(0xFFFF0000), F32)
    return lo, hi


def _silu(x):
    return x * jax.nn.sigmoid(x)


def _norm_mod(x, g, shift, scale):
    r = lax.rsqrt(jnp.mean(x * x, axis=-1, keepdims=True) + EPS)
    return (x * r * g) * (1.0 + scale) + shift


def _const_spec(shape):
    nd = len(shape)
    return pl.BlockSpec(shape, lambda *_: (0,) * nd, pipeline_mode=pl.Buffered(1))


def _ada_body(c_ref, w_ref, b_ref, o_ref):
    o_ref[0] = _dot3(_silu(c_ref[...]), w_ref[0]) + b_ref[0]


def _ada_mod(cvec, ada_w, ada_b):
    depth, d, n6 = ada_w.shape
    rows = cvec.shape[0]
    return pl.pallas_call(
        _ada_body,
        grid=(depth, n6 // ADA_TILE),
        in_specs=[_const_spec((rows, d)),
                  pl.BlockSpec((1, d, ADA_TILE), lambda l, j: (l, 0, j)),
                  pl.BlockSpec((1, 1, ADA_TILE), lambda l, j: (l, 0, j))],
        out_specs=pl.BlockSpec((1, rows, ADA_TILE), lambda l, j: (l, 0, j)),
        out_shape=jax.ShapeDtypeStruct((depth, rows, n6), F32),
        compiler_params=_params("arbitrary", "arbitrary"),
        name="ada_mod",
    )(cvec, ada_w, ada_b.reshape(depth, 1, n6))


def _split_specs(nct, d, tb=1):
    ctx_spec = pl.BlockSpec((tb, TOKEN_TILE, d), lambda b, i: (b, jnp.minimum(i, nct - 1), 0))
    lat_spec = pl.BlockSpec((tb, TOKEN_TILE, d), lambda b, i: (b, jnp.maximum(i - nct, 0), 0))
    return ctx_spec, lat_spec


def _stream_tile(nct, ctx_ref, lat_ref, bb=0):
    return jnp.where(pl.program_id(1) < nct, ctx_ref[bb], lat_ref[bb])


def _normed_rows(nct, ctx_ref, lat_ref, mod_ref, g_ref):
    tiles = []
    for bb in range(mod_ref.shape[0]):
        m = mod_ref[bb, 0]
        tiles.append(_norm_mod(_stream_tile(nct, ctx_ref, lat_ref, bb), g_ref[...], m[0:1], m[1:2]).astype(BF16))
    return tiles[0] if len(tiles) == 1 else jnp.concatenate(tiles, axis=0)


def _store_rows(o_ref, val):
    tm = o_ref.shape[1]
    for bb in range(o_ref.shape[0]):
        o_ref[bb] = val[bb * tm:(bb + 1) * tm].astype(o_ref.dtype)


def _inproj0_body(nct, ctx_ref, lat_ref, mod_ref, g_ref, wq, wk, wv, wg, wlow, wu, wa_ref, ba_ref, tri_ref, ones_ref,
                  qd_f, ki_f, ks_f, ed_f, qd_b, ki_b, ks_b, ed_b, ov, og, ou):
    h = _normed_rows(nct, ctx_ref, lat_ref, mod_ref, g_ref)
    tm = ctx_ref.shape[1]
    low = _dot(h, wlow[...])
    q = _dot(h, wq[...]) * (GLA_DK ** -0.5)
    k = _dot(h, wk[...])
    outs = ((qd_f, ki_f, ks_f, ed_f), (qd_b, ki_b, ks_b, ed_b))
    for d, (qd_ref, ki_ref, ks_ref, ed_ref) in enumerate(outs):
        z = _dot3(low, wa_ref[d]) + ba_ref[d]
        log_a = (jnp.minimum(z, 0.0) - jnp.log1p(jnp.exp(-jnp.abs(z)))) * (1.0 / GLA_TAU)
        la_hi, la_lo = _split(log_a)
        cums, tots = [], []
        for bb in range(qd_ref.shape[0]):
            hi, lo = la_hi[bb * tm:(bb + 1) * tm], la_lo[bb * tm:(bb + 1) * tm]
            cums.append(_dot(tri_ref[d], hi) + _dot(tri_ref[d], lo))
            tots.append(_dot(ones_ref[...], hi) + _dot(ones_ref[...], lo))
        cum = cums[0] if len(cums) == 1 else jnp.concatenate(cums, axis=0)
        tot = tots[0] if len(tots) == 1 else jnp.concatenate(tots, axis=0)
        _store_rows(qd_ref, q * jnp.exp(cum))
        _store_rows(ki_ref, k * jnp.exp(-cum))
        _store_rows(ks_ref, k * jnp.exp(tot - cum))
        for bb in range(ed_ref.shape[0]):
            for ch in range(tm // GLA_CHUNK):
                row = bb * tm + ch * GLA_CHUNK
                ed_ref[bb, ch] = jnp.exp(tot[row:row + 1])
    _store_rows(ov, _dot(h, wv[...]))
    _store_rows(og, _dot(h, wg[...]))
    u = _dot(h, wu[...])
    for qb in range(ou.shape[0]):
        _store_rows(ou.at[qb], u[:, qb * LANES:(qb + 1) * LANES])


def _inproj0(ctx, x, mods, norm_g, weights, wa_pad, ba, nct):
    bsz, n_lat, d = x.shape
    t = ctx.shape[1] + n_lat
    tm, tb = TOKEN_TILE, TOKEN_TILE_BATCH
    mod_spec = pl.BlockSpec((tb, 1, 6, d), lambda b, i: (b, (i >= nct).astype(I32), 0, 0))
    ctx_spec, lat_spec = _split_specs(nct, d, tb)
    pos = jnp.arange(tm)
    same_chunk = (pos[:, None] // GLA_CHUNK) == (pos[None, :] // GLA_CHUNK)
    tri = jnp.stack([same_chunk & (pos[None, :] <= pos[:, None]),
                     same_chunk & (pos[None, :] >= pos[:, None])]).astype(BF16)
    ones = same_chunk.astype(BF16)
    consts = list(weights) + [wa_pad, ba, tri, ones]
    tok = lambda w, dt: (pl.BlockSpec((tb, tm, w), lambda b, i: (b, i, 0)), jax.ShapeDtypeStruct((bsz, t, w), dt))
    per_chunk = (pl.BlockSpec((tb, tm // GLA_CHUNK, 1, AB_QK), lambda b, i: (b, i, 0, 0)),
                 jax.ShapeDtypeStruct((bsz, t // GLA_CHUNK, 1, AB_QK), F32))
    one_dir = [tok(AB_QK, BF16)] * 3 + [per_chunk]
    u_blocks = (pl.BlockSpec((S5_CH // LANES, tb, tm, LANES), lambda b, i: (0, b, i, 0)),
                jax.ShapeDtypeStruct((S5_CH // LANES, bsz, t, LANES), F32))
    outs = one_dir + one_dir + [tok(AB_V, BF16), tok(AB_V, BF16), u_blocks]
    return pl.pallas_call(
        functools.partial(_inproj0_body, nct),
        grid=(bsz // tb, t // tm),
        in_specs=[ctx_spec, lat_spec, mod_spec, _const_spec((1, d))] + [_const_spec(a.shape) for a in consts],
        out_specs=[o[0] for o in outs],
        out_shape=[o[1] for o in outs],
        compiler_params=_params("arbitrary", "arbitrary"),
        name="inproj_gla_s5",
    )(ctx, x, mods, norm_g.reshape(1, d), *consts)


def _rope(acc, cos_ref, sin_ref, o_ref, scale):
    tm = o_ref.shape[1]
    for grp in range(acc.shape[1] // LANES):
        half = grp % 2
        cs = cos_ref[:, half * LANES:(half + 1) * LANES]
        sn = sin_ref[:, half * LANES:(half + 1) * LANES]
        for bb in range(o_ref.shape[0]):
            xg = acc[bb * tm:(bb + 1) * tm, grp * LANES:(grp + 1) * LANES]
            out = xg * cs + pltpu.roll(xg, LANES // 2, 1) * sn
            o_ref[bb, :, grp * LANES:(grp + 1) * LANES] = (out * scale).astype(o_ref.dtype)


def _moe_residual(x1_ref, g2, yk_ref, w_ref, bb=0):
    d = x1_ref.shape[2]
    half = d // 2
    y_lo, y_hi = None, None
    for k in range(TOP_K):
        lo, hi = _unpack_rows(yk_ref[k, bb])
        wk = w_ref[bb, :, k:k + 1]
        y_lo = lo * wk if y_lo is None else y_lo + lo * wk
        y_hi = hi * wk if y_hi is None else y_hi + hi * wk
    return x1_ref[bb, :, 0:half] + g2[:, 0:half] * y_lo, x1_ref[bb, :, half:d] + g2[:, half:d] * y_hi


def _inproj1_body(x1_ref, mod0_ref, yk_ref, w_ref, mod_ref, g_ref, cos_ref, sin_ref, wq, wk, wv, wg, ox, oq, ok, ov, og):
    half = x1_ref.shape[2] // 2
    for bb in range(x1_ref.shape[0]):
        x2_lo, x2_hi = _moe_residual(x1_ref, mod0_ref[bb, 0][5:6], yk_ref, w_ref, bb)
        ox[bb, :, 0:half] = x2_lo
        ox[bb, :, half:2 * half] = x2_hi
        m = mod_ref[bb, 0]
        h = _norm_mod(jnp.concatenate([x2_lo, x2_hi], axis=1), g_ref[...], m[0:1], m[1:2]).astype(BF16)
        _rope(_dot(h, wq[...]), cos_ref, sin_ref, oq.at[bb:bb + 1], 1.0)
        _rope(_dot(h, wk[...]), cos_ref, sin_ref, ok.at[bb:bb + 1], RET_DK ** -0.5)
        ov[bb] = _dot(h, wv[...]).astype(ov.dtype)
        og[bb] = _dot(h, wg[...]).astype(og.dtype)


def _inproj1(x1, mods0, yk, w_tok, mods, norm_g, cos_t, sin_t, weights, nct):
    bsz, t, d = x1.shape
    tm, tb = TOKEN_TILE, TOKEN_TILE_BATCH
    mod_spec = pl.BlockSpec((tb, 1, 6, d), lambda b, i: (b, (i >= nct).astype(I32), 0, 0))
    tok = lambda w: pl.BlockSpec((tb, tm, w), lambda b, i: (b, i, 0))
    lat = lambda w: pl.BlockSpec((tb, tm, w), lambda b, i: (b, jnp.maximum(i - nct, 0), 0))
    tab_spec = pl.BlockSpec((tm, RET_DK), lambda b, i: (i, 0))
    wq, wk, wv, wg = weights
    n_lat = t - nct * tm
    return pl.pallas_call(
        _inproj1_body,
        grid=(bsz // tb, t // tm),
        in_specs=[tok(d), mod_spec, pl.BlockSpec((TOP_K, tb, tm, d // 2), lambda b, i: (0, b, i, 0)), tok(TOP_K),
                  mod_spec, _const_spec((1, d)), tab_spec, tab_spec] + [_const_spec(w.shape) for w in weights],
        out_specs=[lat(d), tok(wq.shape[1]), tok(wk.shape[1]), tok(wv.shape[1]), lat(wg.shape[1])],
        out_shape=[jax.ShapeDtypeStruct((bsz, n_lat, d), F32), jax.ShapeDtypeStruct((bsz, t, wq.shape[1]), BF16),
                   jax.ShapeDtypeStruct((bsz, t, wk.shape[1]), BF16), jax.ShapeDtypeStruct((bsz, t, wv.shape[1]), BF16),
                   jax.ShapeDtypeStruct((bsz, n_lat, wg.shape[1]), BF16)],
        compiler_params=_params("arbitrary", "arbitrary"),
        name="inproj_retention",
    )(x1, mods0, yk.reshape(TOP_K, bsz, t, d // 2), w_tok.reshape(bsz, t, TOP_K), mods, norm_g.reshape(1, d),
      cos_t, sin_t, *weights)


def _backward_chunk(n, n_ctx_chunks, n_chunks):
    return jnp.where(n < n_ctx_chunks, n_ctx_chunks - 1 - n, n_chunks - 1 - (n - n_ctx_chunks))


def _gla_body(qd_f, ki_f, ks_f, ed_f, v_f, qd_b, ki_b, ks_b, ed_b, v_b, hmask_ref, bdmask_ref, o_f, o_b, st_f, st_b):
    c = GLA_CHUNK

    @pl.when(pl.program_id(1) == 0)
    def _():
        st_f[...] = jnp.zeros_like(st_f)
        st_b[...] = jnp.zeros_like(st_b)

    r4 = lax.broadcasted_iota(I32, (GLA_HEADS * c, c), 0) & (c - 1)
    c4 = lax.broadcasted_iota(I32, (GLA_HEADS * c, c), 1)
    dirs = ((qd_f, ki_f, ks_f, ed_f, v_f, o_f, st_f), (qd_b, ki_b, ks_b, ed_b, v_b, o_b, st_b))
    chains = [(bb, d) + dirs[d] for bb in range(qd_f.shape[0]) for d in range(2)]
    scores, inter, grow = [], [], []
    for bb, d, qd_ref, ki_ref, ks_ref, ed_ref, v_ref, o_ref, st_ref in chains:
        q_dec = qd_ref[bb]
        q_heads = jnp.concatenate([q_dec] * GLA_HEADS, axis=0) * hmask_ref[...]
        seen4 = (c4 <= r4) if d == 0 else (c4 >= r4)
        scores.append(jnp.where(seen4, _dot_nt(q_heads, ki_ref[bb]), 0.0).astype(BF16))
        inter.append(_dot_nt(q_dec, st_ref[bb].astype(BF16)))
        grow.append(_dot_tn(v_ref[bb], ks_ref[bb]))
    for (bb, d, qd_ref, ki_ref, ks_ref, ed_ref, v_ref, o_ref, st_ref), sc, o_inter, dst in zip(chains, scores, inter, grow):
        v = v_ref[bb]
        o_intra = jnp.concatenate(
            [_dot(sc[h * c:(h + 1) * c], v[:, h * GLA_DV:(h + 1) * GLA_DV]) for h in range(GLA_HEADS)], axis=1)
        o_ref[bb] = o_intra + o_inter
        st_ref[bb] = st_ref[bb] * ed_ref[bb, 0] + bdmask_ref[...] * dst


def _gla(per_dir, v, n_ctx):
    bsz, t, _ = v.shape
    nc, ncc = t // GLA_CHUNK, n_ctx // GLA_CHUNK
    gb = GLA_BATCH
    fwd = lambda b, n: (b, n, 0)
    bwd = lambda b, n: (b, _backward_chunk(n, ncc, nc), 0)
    hmask = (jnp.arange(AB_QK)[:, None] // GLA_CHUNK == jnp.arange(AB_QK)[None, :] // GLA_DK).astype(BF16)
    bdmask = (jnp.arange(AB_V)[:, None] // GLA_DV == jnp.arange(AB_QK)[None, :] // GLA_DK).astype(F32)

    def specs(idx):
        idx4 = lambda b, n: idx(b, n) + (0,)
        return [pl.BlockSpec((gb, GLA_CHUNK, AB_QK), idx)] * 3 + [pl.BlockSpec((gb, 1, 1, AB_QK), idx4),
                                                                  pl.BlockSpec((gb, GLA_CHUNK, AB_V), idx)]

    return pl.pallas_call(
        _gla_body,
        grid=(bsz // gb, nc),
        in_specs=specs(fwd) + specs(bwd) + [_const_spec(hmask.shape), _const_spec(bdmask.shape)],
        out_specs=[pl.BlockSpec((gb, GLA_CHUNK, AB_V), fwd), pl.BlockSpec((gb, GLA_CHUNK, AB_V), bwd)],
        out_shape=[jax.ShapeDtypeStruct((bsz, t, AB_V), F32)] * 2,
        scratch_shapes=[pltpu.VMEM((gb, AB_V, AB_QK), F32)] * 2,
        compiler_params=_params("arbitrary", "arbitrary"),
        name="gla_scan",
    )(*per_dir[0], v, *per_dir[1], v, hmask, bdmask)


def _cmul(x, y):
    return x[0] * y[0] - x[1] * y[1], x[0] * y[1] + x[1] * y[0]


def _s5_operators(lam_re, lam_im, log_step, b_re, b_im, c_re, c_im):
    ln = S5_CHUNK
    step = jnp.exp(log_step.astype(F32))[..., None]
    lam_re, lam_im = lam_re.astype(F32), lam_im.astype(F32)
    mag = jnp.exp(lam_re * step)
    a = (mag * jnp.cos(lam_im * step), mag * jnp.sin(lam_im * step))
    den = lam_re * lam_re + lam_im * lam_im
    f_re = ((a[0] - 1.0) * lam_re + a[1] * lam_im) / den
    f_im = (a[1] * lam_re - (a[0] - 1.0) * lam_im) / den
    bt_re, bt_im = b_re.transpose(0, 2, 1), b_im.transpose(0, 2, 1)
    bb = _cmul((f_re[:, :, None, :], f_im[:, :, None, :]), (bt_re, bt_im))
    bbt = jnp.concatenate([bb[0], -bb[1]], axis=-1)
    pw = (a[0][:, :, None, :], a[1][:, :, None, :])
    while pw[0].shape[2] < ln:
        top = (pw[0][:, :, -1:, :], pw[1][:, :, -1:, :])
        nxt = _cmul(top, pw)
        pw = (jnp.concatenate([pw[0], nxt[0]], axis=2), jnp.concatenate([pw[1], nxt[1]], axis=2))
    pw = (jnp.concatenate([jnp.ones_like(pw[0][:, :, :1]), pw[0]], axis=2),
          jnp.concatenate([jnp.zeros_like(pw[1][:, :, :1]), pw[1]], axis=2))
    ca = _cmul((c_re[:, :, None], c_im[:, :, None]), (pw[0][:, :, :, None, :], pw[1][:, :, :, None, :]))
    by_dir = lambda arr, lo, flip_d: jnp.stack([jnp.flip(arr[d, :, lo:lo + ln], axis=1) if d == flip_d
                                                else arr[d, :, lo:lo + ln] for d in range(2)])
    rows = lambda arr: arr.reshape(2, S5_GROUPS, ln * S5_GROUP, 2 * S5_P)
    cab = rows(by_dir(jnp.concatenate([ca[0], ca[1]], axis=-1), 0, 1))
    cab2 = rows(by_dir(jnp.concatenate([ca[0], -ca[1]], axis=-1), 1, 1)).astype(BF16)
    pwx = by_dir(jnp.concatenate([pw[0], pw[1]], axis=-1), 0, 0)
    lr, li = pw[0][:, :, ln], pw[1][:, :, ln]
    ac_rows = [jnp.concatenate([lr, lr], -1), jnp.concatenate([-li, li], -1), jnp.concatenate([li, -li], -1)]
    ac = jnp.stack(ac_rows + [jnp.zeros_like(ac_rows[0])] * 5, axis=2)
    return cab, cab2, bbt, pwx, ac


def _s5_group_operators(gg, cab_ref, bbt_ref, pwx_ref, tz, wx):
    ln, ch, p = S5_CHUNK, S5_GROUP, S5_P
    lane = lax.broadcasted_iota(I32, (ch, ln * ch), 1)
    for d in range(2):
        kern = _dot3(bbt_ref[d, gg], cab_ref[d, gg], dot=_dot_nt)
        bt = bbt_ref[d, gg]
        b_re, b_im = bt[:, 0:p], -bt[:, p:2 * p]
        for j in range(ln):
            if d == 0:
                blk = jnp.where(lane >= j * ch, kern if j == 0 else pltpu.roll(kern, j * ch, 1), 0.0)
            else:
                blk = jnp.where(lane < (j + 1) * ch, kern if j == ln - 1 else pltpu.roll(kern, (j + 1) * ch, 1), 0.0)
            tz[gg, d, j * ch:(j + 1) * ch, :] = blk.astype(BF16)
            pr, pi = pwx_ref[d, gg, j:j + 1, 0:p], pwx_ref[d, gg, j:j + 1, p:2 * p]
            x_re, x_im = pr * b_re - pi * b_im, pr * b_im + pi * b_re
            wx[gg, d, j * ch:(j + 1) * ch, :] = jnp.concatenate([x_re, x_im, x_im, x_re], axis=1).astype(BF16)


def _s5_placement(pall):
    rows, cols = pall.shape[1], pall.shape[2]
    row = lax.broadcasted_iota(I32, (rows, cols), 0)
    col = lax.broadcasted_iota(I32, (rows, cols), 1)
    same_token = (row // LANES) == (col // S5_GROUP)
    for g8 in range(pall.shape[0]):
        pall[g8] = jnp.where(same_token & ((row % LANES) == g8 * S5_GROUP + (col % S5_GROUP)), 1.0, 0.0).astype(BF16)


def _first_step():
    return (pl.program_id(0) == 0) & (pl.program_id(1) == 0)


def _s5_fold_body(ncs, u_ref, o_ref, pall, ucat):
    @pl.when(_first_step())
    def _():
        _s5_placement(pall)

    for b in range(u_ref.shape[1]):
        for j in range(S5_CHUNK):
            ucat[b * ncs:(b + 1) * ncs, j * LANES:(j + 1) * LANES] = u_ref[0, b, pl.ds(j, ncs, stride=S5_CHUNK), :].astype(BF16)
    for g8 in range(pall.shape[0]):
        o_ref[g8] = _dot(ucat[...], pall[g8]).astype(BF16)


def _s5_unfold_body(ncs, y_ref, o_ref, pall):
    @pl.when(_first_step())
    def _():
        _s5_placement(pall)

    def token_pair(i2, carry):
        r0 = pl.multiple_of(i2 * 2 * LANES, 2 * LANES)
        acc = _dot_nt(y_ref[0], pall[0, pl.ds(r0, 2 * LANES), :])
        for g8 in range(1, pall.shape[0]):
            acc = acc + _dot_nt(y_ref[g8], pall[g8, pl.ds(r0, 2 * LANES), :])
        for b in range(o_ref.shape[1]):
            for par in range(2):
                o_ref[0, b, pl.ds(2 * i2 + par, ncs, stride=S5_CHUNK), :] = (
                    acc[b * ncs:(b + 1) * ncs, par * LANES:(par + 1) * LANES])
        return carry

    lax.fori_loop(0, S5_CHUNK // 2, token_pair, 0)


def _s5_body(ncs_ctx, ncs, rows, u_ref, cab_ref, cab2_ref, bbt_ref, pwx_ref, ac_ref, y_ref, tz, wx, *vecs):
    half = 2 * S5_P
    n_groups = u_ref.shape[0]
    groups = [vecs[6 * gg:6 * gg + 6] for gg in range(n_groups)]
    for gg, (xx_f, xs_f, xx_b, xs_b, _, _) in enumerate(groups):
        _s5_group_operators(gg, cab_ref, bbt_ref, pwx_ref, tz, wx)
        for d, (xx, xs) in enumerate(((xx_f, xs_f), (xx_b, xs_b))):
            r = _dot(u_ref[gg], wx[gg, d])
            xx[...] = r[:, :half]
            xs[...] = r[:, half:]

    def advance(ac, s, s_sw, x, x_sw):
        return ac[0:1] * s + ac[1:2] * s_sw + x, ac[0:1] * s_sw + ac[2:3] * s + x_sw

    def step(n, carry):
        at_f = pl.ds(n, rows, stride=ncs)
        at_b = pl.ds(_backward_chunk(n, ncs_ctx, ncs), rows, stride=ncs)
        out = []
        for gg, (xx_f, xs_f, xx_b, xs_b, sin_f, sin_b) in enumerate(groups):
            s_f, sw_f, s_b, sw_b = carry[4 * gg:4 * gg + 4]
            sin_f[at_f, :] = s_f
            sin_b[at_b, :] = s_b
            out += advance(ac_ref[0, gg], s_f, sw_f, xx_f[at_f, :], xs_f[at_f, :])
            out += advance(ac_ref[1, gg], s_b, sw_b, xx_b[at_b, :], xs_b[at_b, :])
        return tuple(out)

    zero = jnp.zeros((rows, half), F32)
    lax.fori_loop(0, ncs, step, (zero,) * (4 * n_groups))
    for gg, (_, _, _, _, sin_f, sin_b) in enumerate(groups):
        u = u_ref[gg]
        y_ref[gg] = (_dot(u, tz[gg, 0]) + _dot(u, tz[gg, 1]) + _dot_nt(sin_f[...].astype(BF16), cab2_ref[0, gg])
                     + _dot_nt(sin_b[...].astype(BF16), cab2_ref[1, gg])).astype(BF16)


def _s5(u4, ops, n_ctx):
    nq, bsz, t, _ = u4.shape
    ln, lanes = S5_CHUNK, S5_CHUNK * S5_GROUP
    gq = S5_GROUPS // nq
    ncs, ncs_ctx = t // ln, n_ctx // ln
    m = ncs * bsz
    hb = S5_FOLD_BATCH
    tok_spec = pl.BlockSpec((1, hb, t, LANES), lambda q, h: (q, h, 0, 0))
    grp_spec = pl.BlockSpec((gq, hb * ncs, lanes), lambda q, h: (q, h, 0))
    pall = pltpu.VMEM((gq, ln * LANES, lanes), BF16)
    ug = pl.pallas_call(
        functools.partial(_s5_fold_body, ncs),
        grid=(nq, bsz // hb),
        in_specs=[tok_spec],
        out_specs=grp_spec,
        out_shape=jax.ShapeDtypeStruct((S5_GROUPS, m, lanes), BF16),
        scratch_shapes=[pall, pltpu.VMEM((hb * ncs, ln * LANES), BF16)],
        compiler_params=_params("arbitrary", "arbitrary"),
        name="s5_fold",
    )(u4)
    sg = S5_SCAN_GROUPS
    dir_spec = lambda arr: pl.BlockSpec((2, sg) + arr.shape[2:], lambda g: (0, g, 0, 0))
    yg = pl.pallas_call(
        functools.partial(_s5_body, ncs_ctx, ncs, bsz),
        grid=(S5_GROUPS // sg,),
        in_specs=[pl.BlockSpec((sg, m, lanes), lambda g: (g, 0, 0))] + [dir_spec(arr) for arr in ops],
        out_specs=pl.BlockSpec((sg, m, lanes), lambda g: (g, 0, 0)),
        out_shape=jax.ShapeDtypeStruct((S5_GROUPS, m, lanes), BF16),
        scratch_shapes=[pltpu.VMEM((sg, 2, lanes, lanes), BF16)] * 2 + [pltpu.VMEM((m, 2 * S5_P), F32)] * (6 * sg),
        compiler_params=_params("arbitrary"),
        name="s5_scan",
    )(ug, *ops)
    return pl.pallas_call(
        functools.partial(_s5_unfold_body, ncs),
        grid=(nq, bsz // hb),
        in_specs=[grp_spec],
        out_specs=tok_spec,
        out_shape=jax.ShapeDtypeStruct(u4.shape, F32),
        scratch_shapes=[pall],
        compiler_params=_params("arbitrary", "arbitrary"),
        name="s5_unfold",
    )(yg)


def _ret_body(q_f, k_f, v_f, q_b, k_b, v_b, dmat_ref, rsc_ref, csc_ref, gam_ref, o_f, o_b, st_f, st_b):
    @pl.when(pl.program_id(1) == 0)
    def _():
        st_f[...] = jnp.zeros_like(st_f)
        st_b[...] = jnp.zeros_like(st_b)

    dirs = ((q_f, k_f, v_f, o_f, st_f), (q_b, k_b, v_b, o_b, st_b))
    for bb in range(q_f.shape[0]):
        for d, (q_ref, k_ref, v_ref, o_ref, st_ref) in enumerate(dirs):
            for h in range(RET_HEADS):
                qh = q_ref[bb, :, h * RET_DK:(h + 1) * RET_DK]
                kh = k_ref[bb, :, h * RET_DK:(h + 1) * RET_DK]
                vh = v_ref[bb, :, h * RET_DV:(h + 1) * RET_DV]
                st = st_ref[bb, h]
                scores = (_dot_nt(qh, kh) * dmat_ref[d, h]).astype(BF16)
                o = _dot(scores, vh) + rsc_ref[d, h] * _dot(qh, st.astype(BF16))
                o_ref[bb, :, h * RET_DV:(h + 1) * RET_DV] = o.astype(o_ref.dtype)
                k_state = (kh.astype(F32) * csc_ref[d, h]).astype(BF16)
                st_ref[bb, h] = st * gam_ref[d, h] + _dot_tn(k_state, vh)


def _retention(q, k, v, decay_logit, n_ctx):
    bsz, t, _ = q.shape
    c = RET_CHUNK
    nc, ncc = t // c, n_ctx // c
    nl = nc - ncc
    rb = RET_BATCH
    log_gamma = jax.nn.log_sigmoid(decay_logit.astype(F32))[:, :, None, None]
    i = jnp.arange(c, dtype=F32)
    lag = i[:, None] - i[None, :]
    lag = jnp.stack([lag, -lag])[:, None]
    dmat = jnp.where(lag >= 0, jnp.exp(log_gamma * jnp.maximum(lag, 0.0)), 0.0)
    done = jnp.stack([i + 1.0, c - i])[:, None, :, None]
    rsc = jnp.exp(log_gamma * done)
    csc = jnp.exp(log_gamma * (c - done))
    gam = jnp.exp(log_gamma[:, :, 0, 0] * c)
    fwd = lambda b, n: (b, n, 0)
    bwd = lambda b, n: (b, _backward_chunk(n, ncc, nc), 0)
    o_fwd = lambda b, n: (b, jnp.maximum(n - ncc, 0), 0)
    o_bwd = lambda b, n: (b, nl - 1 - jnp.maximum(n - ncc, 0), 0)

    def specs(idx):
        return [pl.BlockSpec((rb, c, RET_QK), idx), pl.BlockSpec((rb, c, RET_QK), idx), pl.BlockSpec((rb, c, RET_MIX), idx)]

    return pl.pallas_call(
        _ret_body,
        grid=(bsz // rb, nc),
        in_specs=specs(fwd) + specs(bwd) + [_const_spec(dmat.shape), _const_spec(rsc.shape), _const_spec(csc.shape),
                                            pl.BlockSpec(memory_space=pltpu.SMEM)],
        out_specs=[pl.BlockSpec((rb, c, RET_MIX), o_fwd), pl.BlockSpec((rb, c, RET_MIX), o_bwd)],
        out_shape=[jax.ShapeDtypeStruct((bsz, nl * c, RET_MIX), BF16)] * 2,
        scratch_shapes=[pltpu.VMEM((rb, RET_HEADS, RET_DK, RET_DV), F32)] * 2,
        compiler_params=_params("arbitrary", "arbitrary"),
        name="retention_scan",
    )(q, k, v, q, k, v, dmat, rsc, csc, gam)


def _zero_counts_at_start(cnt_ref):
    @pl.when(_first_step())
    def _():
        cnt_ref[...] = jnp.zeros_like(cnt_ref)


def _route(xs, mixed, mods, n2g_ref, wr_ref, br_ref, x1_ref, h2_ref, e_ref, w_ref, r_ref, cnt_ref):
    rows = xs[0].shape[0]
    h2s = []
    for bb, (x, mod) in enumerate(zip(xs, mods)):
        x1 = x + mod[2:3] * mixed[bb * rows:(bb + 1) * rows]
        x1_ref[bb] = x1
        h2s.append(_norm_mod(x1, n2g_ref[...], mod[3:4], mod[4:5]))
        h2_ref[bb] = _pack_rows(h2s[bb])
    h2 = h2s[0] if len(h2s) == 1 else jnp.concatenate(h2s, axis=0)
    tm = h2.shape[0]
    logits = _dot3(wr_ref[...], h2, dot=_dot_nt) + br_ref[...]
    ie = lax.broadcasted_iota(I32, logits.shape, 0)
    tops, picks = [], []
    for _ in range(TOP_K):
        mx = jnp.max(logits, axis=0, keepdims=True)
        pick = jnp.min(jnp.where(logits == mx, ie, N_EXPERTS), axis=0, keepdims=True)
        tops.append(mx)
        picks.append(pick)
        logits = jnp.where(ie == pick, -jnp.inf, logits)
    ex = [jnp.exp(tk - tops[0]) for tk in tops]
    den = ex[0] + ex[1] + ex[2] + ex[3]
    for kk in range(TOP_K):
        w_ref[0, kk:kk + 1, :] = ex[kk] / den
        e_ref[0, kk:kk + 1, :] = picks[kk]

    earlier = (lax.broadcasted_iota(I32, (tm, tm), 0) < lax.broadcasted_iota(I32, (tm, tm), 1))
    earlier = jnp.where(earlier, 1.0, 0.0).astype(BF16)
    run = cnt_ref[:, 0:1]
    for kk, pick in enumerate(picks):
        onehot = jnp.where(ie == pick, 1.0, 0.0)
        before = _dot(onehot.astype(BF16), earlier) + run
        r_ref[0, kk:kk + 1, :] = jnp.sum(onehot * before, axis=0, keepdims=True).astype(I32)
        run = run + jnp.sum(onehot, axis=1, keepdims=True)
    cnt_ref[...] = jnp.broadcast_to(run, cnt_ref.shape)


def _mix0_body(nct, ctx_ref, lat_ref, mod_ref, of_ref, ob_ref, g_ref, ys_ref, u_ref, gng_ref, dsk_ref, gluw_ref,
               glub_ref, wo_ref, n2g_ref, wr_ref, br_ref, x1_ref, h2_ref, e_ref, w_ref, r_ref, cnt_ref):
    _zero_counts_at_start(cnt_ref)
    tb, tm = of_ref.shape[0], of_ref.shape[1]
    rows = lambda ref: ref[...].reshape(tb * tm, ref.shape[-1])
    o = rows(of_ref) + rows(ob_ref)
    heads = []
    for h in range(GLA_HEADS):
        oh = o[:, h * GLA_DV:(h + 1) * GLA_DV]
        heads.append(oh * lax.rsqrt(jnp.mean(oh * oh, axis=-1, keepdims=True) + EPS))
    gla = jnp.concatenate(heads, axis=1) * gng_ref[...] * _silu(rows(g_ref).astype(F32))
    lane_blocks = lambda ref: jnp.concatenate([ref[qb].reshape(tb * tm, LANES) for qb in range(ref.shape[0])], axis=1)
    y = jax.nn.gelu(lane_blocks(ys_ref) + dsk_ref[...] * lane_blocks(u_ref))
    y = y * jax.nn.sigmoid(_dot(y.astype(BF16), gluw_ref[...]) + glub_ref[...])
    mixed = _dot(gla.astype(BF16), wo_ref[0:AB_V]) + _dot(y.astype(BF16), wo_ref[AB_V:AB_V + S5_CH])
    _route([_stream_tile(nct, ctx_ref, lat_ref, bb) for bb in range(tb)], mixed, [mod_ref[bb, 0] for bb in range(tb)],
           n2g_ref, wr_ref, br_ref, x1_ref, h2_ref, e_ref, w_ref, r_ref, cnt_ref)


def _mix1_body(x_ref, mod_ref, of_ref, ob_ref, g_ref, ng_ref, wo_ref, n2g_ref, wr_ref, br_ref,
               x1_ref, h2_ref, e_ref, w_ref, r_ref, cnt_ref):
    _zero_counts_at_start(cnt_ref)
    mixed = None
    for h in range(RET_HEADS):
        sl = slice(h * RET_DV, (h + 1) * RET_DV)
        oh = of_ref[0, :, sl].astype(F32) + ob_ref[0, :, sl].astype(F32)
        mu = jnp.mean(oh, axis=-1, keepdims=True)
        cen = oh - mu
        var = jnp.mean(cen * cen, axis=-1, keepdims=True)
        gated = cen * lax.rsqrt(var + EPS) * ng_ref[:, sl] * _silu(g_ref[0, :, sl].astype(F32))
        part = _dot(gated.astype(BF16), wo_ref[sl])
        mixed = part if mixed is None else mixed + part
    _route([x_ref[0]], mixed, [mod_ref[0, 0]], n2g_ref, wr_ref, br_ref, x1_ref, h2_ref, e_ref, w_ref, r_ref, cnt_ref)


def _mix_call(body, name, stream, mods, tiles, acts, consts, norm2_g, w_router, b_router, n_tok, seg_tile0,
              tm=TOKEN_TILE, tb=1):
    bsz, _, d = stream[-1].shape
    off = lambda b, i: (b, i + seg_tile0, 0)
    loc = lambda b, i: (b, i, 0)
    ntl = bsz // tb * tiles
    flat = lambda b, i: (b * tiles + i, 0, 0)
    in_specs = list(_split_specs(n_tok, d, tb)) if len(stream) == 2 else [pl.BlockSpec((tb, tm, d), off)]
    in_specs.append(pl.BlockSpec((tb, 1, 6, d), lambda b, i: (b, ((i + seg_tile0) >= n_tok).astype(I32), 0, 0)))
    args = list(stream) + [mods]
    for arr, offset in acts:
        if arr.ndim == 4:
            in_specs.append(pl.BlockSpec((arr.shape[0], tb, tm, arr.shape[3]), lambda b, i: (0, b, i, 0)))
        else:
            in_specs.append(pl.BlockSpec((tb, tm, arr.shape[2]), off if offset else loc))
        args.append(arr)
    tail = list(consts) + [norm2_g.reshape(1, d), w_router.T, b_router.reshape(N_EXPERTS, 1)]
    in_specs += [_const_spec(a.shape) for a in tail]
    args += tail
    tok_out = pl.BlockSpec((1, TOP_K, tb * tm), flat)
    return pl.pallas_call(
        body,
        grid=(bsz // tb, tiles),
        in_specs=in_specs,
        out_specs=[pl.BlockSpec((tb, tm, d), loc), pl.BlockSpec((tb, tm, d // 2), loc), tok_out, tok_out, tok_out,
                   _const_spec((N_EXPERTS, LANES))],
        out_shape=[jax.ShapeDtypeStruct((bsz, tiles * tm, d), F32), jax.ShapeDtypeStruct((bsz, tiles * tm, d // 2), U32),
                   jax.ShapeDtypeStruct((ntl, TOP_K, tb * tm), I32), jax.ShapeDtypeStruct((ntl, TOP_K, tb * tm), F32),
                   jax.ShapeDtypeStruct((ntl, TOP_K, tb * tm), I32), jax.ShapeDtypeStruct((N_EXPERTS, LANES), F32)],
        compiler_params=_params("arbitrary", "arbitrary"),
        name=name,
    )(*args)


def _cast_rows(src_ref, dst_ref, rows):
    def chunk(j, carry):
        r = pl.multiple_of(j * rows, rows)
        dst_ref[pl.ds(r, rows), :] = src_ref[0, 0, pl.ds(r, rows), :].astype(BF16)
        return carry

    lax.fori_loop(0, dst_ref.shape[0] // rows, chunk, 0)


def _expert_body(be_ref, nu_ref, x_ref, wgu_ref, bgu_ref, wd_ref, bd_ref, o_ref, wgu_bf, wd_bf):
    i = pl.program_id(0)
    live = i < nu_ref[0]
    new_expert = (i == 0) | (be_ref[i] != be_ref[jnp.maximum(i - 1, 0)])

    @pl.when(live & new_expert)
    def _():
        _cast_rows(wgu_ref, wgu_bf, 128)
        _cast_rows(wd_ref, wd_bf, 128)

    @pl.when(live)
    def _():
        x_lo, x_hi = _unpack_rows(x_ref[...])
        half = x_lo.shape[1]
        gu = (_dot(x_lo.astype(BF16), wgu_bf[0:half]) + _dot(x_hi.astype(BF16), wgu_bf[half:2 * half])
              + bgu_ref[0, 0])
        gate = jnp.minimum(gu[:, :D_FF], SWIGLU_LIMIT)
        lin = jnp.clip(gu[:, D_FF:], -SWIGLU_LIMIT, SWIGLU_LIMIT)
        act = gate * jax.nn.sigmoid(SWIGLU_ALPHA * gate) * (lin + 1.0)
        y = _dot(act.astype(BF16), wd_bf[...]) + bd_ref[0, 0]
        o_ref[...] = _pack_rows(y)

    @pl.when(i >= nu_ref[0])
    def _():
        o_ref[...] = jnp.zeros_like(o_ref)


def _experts(xb, block_e, n_used, layer, w_gu, b_gu, w_down, b_down):
    n_slots, half = xb.shape
    d = 2 * half
    n_blocks = n_slots // MOE_BLOCK
    depth = w_gu.shape[0]
    by_expert = lambda i, be, nu: (layer, be[i], 0, 0)
    return pl.pallas_call(
        _expert_body,
        grid_spec=pltpu.PrefetchScalarGridSpec(
            num_scalar_prefetch=2,
            grid=(n_blocks,),
            in_specs=[pl.BlockSpec((MOE_BLOCK, half), lambda i, be, nu: (i, 0)),
                      pl.BlockSpec((1, 1, d, 2 * D_FF), by_expert), pl.BlockSpec((1, 1, 1, 2 * D_FF), by_expert),
                      pl.BlockSpec((1, 1, D_FF, d), by_expert), pl.BlockSpec((1, 1, 1, d), by_expert)],
            out_specs=pl.BlockSpec((MOE_BLOCK, half), lambda i, be, nu: (i, 0)),
            scratch_shapes=[pltpu.VMEM((d, 2 * D_FF), BF16), pltpu.VMEM((D_FF, d), BF16)]),
        out_shape=jax.ShapeDtypeStruct((n_slots, half), U32),
        compiler_params=_params("arbitrary"),
        name="moe_experts",
    )(block_e, n_used, xb, w_gu, b_gu.reshape(depth, N_EXPERTS, 1, 2 * D_FF), w_down,
      b_down.reshape(depth, N_EXPERTS, 1, d))


def _combine_body(x1_ref, mod_ref, yk_ref, w_ref, fg_ref, o_ref):
    d = x1_ref.shape[2]
    half = d // 2
    x2_lo, x2_hi = _moe_residual(x1_ref, mod_ref[0, 0][5:6], yk_ref, w_ref)
    ms = (jnp.sum(x2_lo * x2_lo, axis=-1, keepdims=True) + jnp.sum(x2_hi * x2_hi, axis=-1, keepdims=True)) / d
    r = lax.rsqrt(ms + EPS)
    o_ref[0, :, 0:half] = x2_lo * r * fg_ref[:, 0:half]
    o_ref[0, :, half:d] = x2_hi * r * fg_ref[:, half:d]


def _combine(x1, mods, yk, w_tok, seg_tile0, n_tok, final_g, tm):
    bsz, t, d = x1.shape
    loc = lambda b, i: (b, i, 0)
    return pl.pallas_call(
        _combine_body,
        grid=(bsz, t // tm),
        in_specs=[pl.BlockSpec((1, tm, d), loc),
                  pl.BlockSpec((1, 1, 6, d), lambda b, i: (b, ((i + seg_tile0) >= n_tok).astype(I32), 0, 0)),
                  pl.BlockSpec((TOP_K, 1, tm, d // 2), lambda b, i: (0, b, i, 0)),
                  pl.BlockSpec((1, tm, TOP_K), loc), _const_spec((1, d))],
        out_specs=pl.BlockSpec((1, tm, d), loc),
        out_shape=jax.ShapeDtypeStruct((bsz, t, d), F32),
        compiler_params=_params("arbitrary", "arbitrary"),
        name="moe_combine",
    )(x1, mods, yk.reshape(TOP_K, bsz, t, d // 2), w_tok.reshape(bsz, t, TOP_K), final_g.reshape(1, d))


def _sc_mesh():
    return plsc.VectorSubcoreMesh(core_axis_name="core", subcore_axis_name="subcore",
                                  num_cores=SC_CORES, num_subcores=SC_SUBCORES)


def _sc_worker_base(per_worker):
    return (lax.axis_index("subcore") * SC_CORES + lax.axis_index("core")) * per_worker


def _sc_dispatch(rows, dest, n_slots):
    n, w = rows.shape
    per_worker = n // SC_WORKERS
    assert per_worker * SC_WORKERS == n and per_worker % SC_CHUNK == 0

    @functools.partial(
        pl.kernel, mesh=_sc_mesh(), out_type=jax.ShapeDtypeStruct((n_slots, w), rows.dtype),
        scratch_types=[pltpu.VMEM((SC_CHUNK,), I32)] * TOP_K + [pltpu.VMEM((SC_CHUNK, w), rows.dtype),
                                                                pltpu.SemaphoreType.DMA],
        name="moe_dispatch")
    def scatter_rows(rows_hbm, dest_hbm, out_hbm, *scratch):
        idx_refs, buf, sem = scratch[:TOP_K], scratch[TOP_K], scratch[TOP_K + 1]
        base0 = _sc_worker_base(per_worker)

        @pl.loop(0, per_worker // SC_CHUNK)
        def _(j):
            base = base0 + j * SC_CHUNK
            pltpu.sync_copy(rows_hbm.at[pl.ds(base, SC_CHUNK)], buf)
            for k, idx in enumerate(idx_refs):
                pltpu.sync_copy(dest_hbm.at[pl.ds(k * n + base, SC_CHUNK)], idx)
            copies = [pltpu.make_async_copy(buf, out_hbm.at[idx], sem) for idx in idx_refs]
            for cp in copies:
                cp.start()
            for cp in copies:
                cp.wait()

    return scatter_rows(rows, dest)


def _sc_gather(table, idx):
    n = idx.shape[0]
    w = table.shape[1]
    per_worker = n // SC_WORKERS
    n_chunks = per_worker // SC_CHUNK
    assert per_worker * SC_WORKERS == n and n_chunks * SC_CHUNK == per_worker and n_chunks % 2 == 0

    @functools.partial(
        pl.kernel, mesh=_sc_mesh(), out_type=jax.ShapeDtypeStruct((n, w), table.dtype),
        scratch_types=([pltpu.VMEM((SC_CHUNK,), I32)] * 2 + [pltpu.VMEM((SC_CHUNK, w), table.dtype)] * 2
                       + [pltpu.SemaphoreType.DMA] * 4),
        name="moe_gather")
    def gather_rows(table_hbm, idx_hbm, out_hbm, idx0, idx1, buf0, buf1, gsem0, gsem1, wsem0, wsem1):
        base0 = _sc_worker_base(per_worker)

        def gather_copy(idx_v, buf, sem):
            return pltpu.make_async_copy(table_hbm.at[idx_v], buf, sem)

        def write_copy(j, buf, sem):
            return pltpu.make_async_copy(buf, out_hbm.at[pl.ds(base0 + j * SC_CHUNK, SC_CHUNK)], sem)

        def start_gather(j, idx_v, buf, sem):
            pltpu.sync_copy(idx_hbm.at[pl.ds(base0 + j * SC_CHUNK, SC_CHUNK)], idx_v)
            gather_copy(idx_v, buf, sem).start()

        start_gather(0, idx0, buf0, gsem0)

        @pl.loop(0, n_chunks, step=2)
        def _(j):
            @pl.when(j > 0)
            def _():
                write_copy(j - 1, buf1, wsem1).wait()
            start_gather(j + 1, idx1, buf1, gsem1)
            gather_copy(idx0, buf0, gsem0).wait()
            write_copy(j, buf0, wsem0).start()

            @pl.when(j + 2 < n_chunks)
            def _():
                write_copy(j, buf0, wsem0).wait()
                start_gather(j + 2, idx0, buf0, gsem0)
            gather_copy(idx1, buf1, gsem1).wait()
            write_copy(j + 1, buf1, wsem1).start()

        write_copy(n_chunks - 2, buf0, wsem0).wait()
        write_copy(n_chunks - 1, buf1, wsem1).wait()

    return gather_rows(table, idx)


def _moe(h2, e_tl, w_tl, r_tl, cnt, layer, w_gu, b_gu, w_down, b_down, tb=1):
    bsz, t, half = h2.shape
    n = bsz * t
    flat = lambda a: a.reshape(bsz // tb, -1, TOP_K, tb, a.shape[2] // tb).transpose(2, 0, 3, 1, 4).reshape(TOP_K, n)
    e_k, w_k, r_k = flat(e_tl), flat(w_tl), flat(r_tl)
    counts = cnt[:, 0].astype(I32)
    padded = (counts + MOE_BLOCK - 1) // MOE_BLOCK * MOE_BLOCK
    pad_end = jnp.cumsum(padded)
    pad_start = pad_end - padded
    n_blocks = (n * TOP_K + MOE_BLOCK - 1) // MOE_BLOCK + N_EXPERTS
    block_start = jnp.arange(n_blocks, dtype=I32) * MOE_BLOCK
    block_e = jnp.minimum(jnp.sum((pad_end[None, :] <= block_start[:, None]).astype(I32), axis=1), N_EXPERTS - 1)
    n_used = (pad_end[-1:] // MOE_BLOCK).astype(I32)
    start_k = jnp.sum(jnp.where(e_k[..., None] == jnp.arange(N_EXPERTS, dtype=I32), pad_start, 0), axis=-1)
    dest = (start_k + r_k).reshape(TOP_K * n)
    xb = _sc_dispatch(h2.reshape(n, half), dest, n_blocks * MOE_BLOCK)
    yb = _experts(xb, block_e, n_used, layer, w_gu, b_gu, w_down, b_down)
    return _sc_gather(yb, dest), w_k.T


def _rope_tables(n_ctx, n_lat):
    n_freq = RET_DK // 4
    inv_freq = ROPE_BASE ** (-jnp.arange(n_freq, dtype=F32) / n_freq)
    pos = jnp.arange(n_lat, dtype=I32)
    cos, sin = [], []
    for p in (pos // GRID_W, pos % GRID_W):
        ang = p.astype(F32)[:, None] * inv_freq
        cos += [jnp.cos(ang), jnp.cos(ang)]
        sin += [-jnp.sin(ang), jnp.sin(ang)]
    cos, sin = jnp.concatenate(cos, axis=1), jnp.concatenate(sin, axis=1)
    return (jnp.concatenate([jnp.ones((n_ctx, RET_DK), F32), cos], axis=0),
            jnp.concatenate([jnp.zeros((n_ctx, RET_DK), F32), sin], axis=0))


def kernel(x, c, ctx, c_ctx, ada_w, ada_b, norm1_g, norm2_g, ab_w_in, ab_w_out, gla_wa, gla_ba, gla_norm_g, s5_lam_re, s5_lam_im, s5_log_step, s5_b_re, s5_b_im, s5_c_re, s5_c_im, s5_d, s5_glu_w, s5_glu_b, ret_w_in, ret_w_out, ret_decay_logit, ret_norm_g, moe_w_router, moe_b_router, moe_w_gu, moe_b_gu, moe_w_down, moe_b_down, final_norm_g):
    bsz, n_lat, d = x.shape
    n_ctx = ctx.shape[1]
    depth = ada_w.shape[0]
    assert depth == 2 and d == D_MODEL and bsz == 8, "kernels are laid out for the stated problem shape"
    assert n_ctx % TOKEN_TILE == 0 and n_lat % LATENT_TILE == 0 and n_lat % GRID_W == 0
    t = n_ctx + n_lat
    nct = n_ctx // TOKEN_TILE

    cvec = jnp.zeros((16, d), F32).at[:bsz].set(c).at[bsz].set(c_ctx)
    mod = _ada_mod(cvec, ada_w, ada_b).reshape(depth, 16, 6, d)
    mods = [jnp.stack([jnp.broadcast_to(mod[l, bsz], (bsz, 6, d)), mod[l, :bsz]], axis=1) for l in range(depth)]

    w_in = ab_w_in[0].astype(BF16)
    cuts = [0, AB_QK, 2 * AB_QK, 2 * AB_QK + AB_V, 2 * AB_QK + 2 * AB_V, 2 * AB_QK + 2 * AB_V + 2 * GLA_RANK,
            w_in.shape[1]]
    pieces = [w_in[:, a:b] for a, b in zip(cuts[:-1], cuts[1:])]
    wa_pad = jnp.zeros((2, 2 * GLA_RANK, AB_QK), F32)
    wa_pad = wa_pad.at[0, :GLA_RANK].set(gla_wa[0, 0]).at[1, GLA_RANK:].set(gla_wa[0, 1])
    outs = _inproj0(ctx, x, mods[0], norm1_g[0], pieces, wa_pad, gla_ba[0].reshape(2, 1, AB_QK), nct)
    v, g, u = outs[8:]
    o_f, o_b = _gla((outs[0:4], outs[4:8]), v, n_ctx)
    ops = _s5_operators(s5_lam_re[0], s5_lam_im[0], s5_log_step[0], s5_b_re[0], s5_b_im[0], s5_c_re[0], s5_c_im[0])
    ys = _s5(u, ops, n_ctx)
    consts = [jnp.tile(gla_norm_g[0], GLA_HEADS).reshape(1, AB_V), s5_d[0].reshape(1, S5_CH),
              s5_glu_w[0].astype(BF16), s5_glu_b[0].reshape(1, S5_CH), ab_w_out[0].astype(BF16)]
    x1, h2, e_tl, w_tl, r_tl, cnt = _mix_call(
        functools.partial(_mix0_body, nct), "mix_gla_s5", (ctx, x), mods[0], t // TOKEN_TILE,
        [(o_f, False), (o_b, False), (g, False), (ys, False), (u, False)], consts,
        norm2_g[0], moe_w_router[0], moe_b_router[0], nct, 0, tb=TOKEN_TILE_BATCH)
    yk, w_tok = _moe(h2, e_tl, w_tl, r_tl, cnt, 0, moe_w_gu, moe_b_gu, moe_w_down, moe_b_down, tb=TOKEN_TILE_BATCH)

    w_in = ret_w_in[0].astype(BF16)
    cuts = [0, RET_QK, 2 * RET_QK, 2 * RET_QK + RET_MIX, w_in.shape[1]]
    pieces = [w_in[:, a:b] for a, b in zip(cuts[:-1], cuts[1:])]
    cos_t, sin_t = _rope_tables(n_ctx, n_lat)
    x2, q, k, v, g = _inproj1(x1, mods[0], yk, w_tok, mods[1], norm1_g[1], cos_t, sin_t, pieces, nct)
    o_f, o_b = _retention(q, k, v, ret_decay_logit[0], n_ctx)
    consts = [ret_norm_g[0].reshape(1, RET_MIX), ret_w_out[0].astype(BF16)]
    x1, h2, e_tl, w_tl, r_tl, cnt = _mix_call(
        _mix1_body, "mix_retention", (x2,), mods[1], n_lat // LATENT_TILE,
        [(o_f, False), (o_b, False), (g, False)], consts,
        norm2_g[1], moe_w_router[1], moe_b_router[1], 0, 0, tm=LATENT_TILE)
    yk, w_tok = _moe(h2, e_tl, w_tl, r_tl, cnt, 1, moe_w_gu, moe_b_gu, moe_w_down, moe_b_down)
    return _combine(x1, mods[1], yk, w_tok, 0, 0, final_norm_g, LATENT_TILE)
```

```python
import functools
import math

import jax
import jax.numpy as jnp
from jax import lax
from jax.experimental import pallas as pl
from jax.experimental.pallas import tpu as pltpu
from jax.experimental.pallas import tpu_sc as plsc

F32, BF16, I32, U32 = jnp.float32, jnp.bfloat16, jnp.int32, jnp.uint32

D_MODEL = 1024
GRID_W = 64
EPS = 1e-6
GLA_HEADS, GLA_DK, GLA_DV, GLA_RANK, GLA_TAU, GLA_CHUNK = 4, 64, 128, 16, 16.0, 64
GLA_BATCH = 8
AB_QK, AB_V = GLA_HEADS * GLA_DK, GLA_HEADS * GLA_DV
S5_CH, S5_GROUP, S5_GROUPS, S5_P = 512, 16, 32, 64
S5_CHUNK = 16
S5_FOLD_BATCH = 4
S5_SCAN_GROUPS = 2
RET_HEADS, RET_DK, RET_DV = 4, 256, 512
RET_CHUNK = 256
RET_BATCH = 2
RET_QK, RET_MIX = RET_HEADS * RET_DK, RET_HEADS * RET_DV
ROPE_BASE = 10000.0
N_EXPERTS, TOP_K, D_FF = 32, 4, 1024
SWIGLU_LIMIT, SWIGLU_ALPHA = 7.0, 1.702
MOE_BLOCK = 512
EXPERT_ROW_PARTS = 2
TOKEN_TILE = 256
LATENT_TILE = 512
TOKEN_TILE_BATCH = 2
ADA_TILE = 768
VMEM_LIMIT = 56 * 1024 * 1024
SC_CORES, SC_SUBCORES = 2, 16
SC_WORKERS = SC_CORES * SC_SUBCORES
SC_CHUNK = 64
LANES = 128

def _params(*sem):
    return pltpu.CompilerParams(dimension_semantics=sem, vmem_limit_bytes=VMEM_LIMIT)


def _dot(a, b):
    return jnp.dot(a, b, preferred_element_type=F32)


def _dot_nt(a, b):
    return lax.dot_general(a, b, (((1,), (1,)), ((), ())), preferred_element_type=F32)


def _dot_tn(a, b):
    return lax.dot_general(a, b, (((0,), (0,)), ((), ())), preferred_element_type=F32)


def _split(a):
    hi = a.astype(BF16)
    return hi, (a - hi.astype(F32)).astype(BF16)


def _dot3(a, b, dot=_dot):
    ah, al = _split(a)
    bh, bl = _split(b)
    return dot(ah, bh) + (dot(ah, bl) + dot(al, bh))


def _pack_rows(x):
    h = x.shape[1] // 2
    lo = lax.bitcast_convert_type(x[:, 0:h].astype(BF16).astype(F32), U32)
    hi = lax.bitcast_convert_type(x[:, h:2 * h].astype(BF16).astype(F32), U32)
    return hi | (lo >> 16)


def _unpack_rows(p):
    lo = lax.bitcast_convert_type(p << 16, F32)
    hi = lax.bitcast_convert_type(p & jnp.uint32(0xFFFF0000), F32)
    return lo, hi


def _silu(x):
    return x * jax.nn.sigmoid(x)


def _norm_mod(x, g, shift, scale):
    r = lax.rsqrt(jnp.mean(x * x, axis=-1, keepdims=True) + EPS)
    return (x * r * g) * (1.0 + scale) + shift


def _const_spec(shape):
    nd = len(shape)
    return pl.BlockSpec(shape, lambda *_: (0,) * nd, pipeline_mode=pl.Buffered(1))


def _ada_body(c_ref, w_ref, b_ref, o_ref):
    o_ref[0] = _dot3(_silu(c_ref[...]), w_ref[0]) + b_ref[0]


def _ada_mod(cvec, ada_w, ada_b):
    depth, d, n6 = ada_w.shape
    rows = cvec.shape[0]
    return pl.pallas_call(
        _ada_body,
        grid=(depth, n6 // ADA_TILE),
        in_specs=[_const_spec((rows, d)),
                  pl.BlockSpec((1, d, ADA_TILE), lambda l, j: (l, 0, j)),
                  pl.BlockSpec((1, 1, ADA_TILE), lambda l, j: (l, 0, j))],
        out_specs=pl.BlockSpec((1, rows, ADA_TILE), lambda l, j: (l, 0, j)),
        out_shape=jax.ShapeDtypeStruct((depth, rows, n6), F32),
        compiler_params=_params("arbitrary", "arbitrary"),
        name="ada_mod",
    )(cvec, ada_w, ada_b.reshape(depth, 1, n6))


def _split_specs(nct, d, tb=1):
    ctx_spec = pl.BlockSpec((tb, TOKEN_TILE, d), lambda b, i: (b, jnp.minimum(i, nct - 1), 0))
    lat_spec = pl.BlockSpec((tb, TOKEN_TILE, d), lambda b, i: (b, jnp.maximum(i - nct, 0), 0))
    return ctx_spec, lat_spec


def _stream_tile(nct, ctx_ref, lat_ref, bb=0):
    return jnp.where(pl.program_id(1) < nct, ctx_ref[bb], lat_ref[bb])


def _inproj0_body(nct, ctx_ref, lat_ref, mod_ref, g_ref, wq, wk, wv, wg, wlow, wu, wa_ref, ba_ref, tri_ref, ones_ref,
                  qd_f, ki_f, ks_f, ed_f, qd_b, ki_b, ks_b, ed_b, ov, og, ou):
    tm = ctx_ref.shape[1]
    for bb in range(mod_ref.shape[0]):
        m = mod_ref[bb, 0]
        h = _norm_mod(_stream_tile(nct, ctx_ref, lat_ref, bb), g_ref[...], m[0:1], m[1:2]).astype(BF16)
        low = _dot(h, wlow[...])
        q = _dot(h, wq[...]) * (GLA_DK ** -0.5)
        k = _dot(h, wk[...])
        outs = ((qd_f, ki_f, ks_f, ed_f), (qd_b, ki_b, ks_b, ed_b))
        for d, (qd_ref, ki_ref, ks_ref, ed_ref) in enumerate(outs):
            z = _dot3(low, wa_ref[d]) + ba_ref[d]
            log_a = (jnp.minimum(z, 0.0) - jnp.log1p(jnp.exp(-jnp.abs(z)))) * (1.0 / GLA_TAU)
            la_hi, la_lo = _split(log_a)
            cum = _dot(tri_ref[d], la_hi) + _dot(tri_ref[d], la_lo)
            tot = _dot(ones_ref[...], la_hi) + _dot(ones_ref[...], la_lo)
            qd_ref[bb] = (q * jnp.exp(cum)).astype(BF16)
            ki_ref[bb] = (k * jnp.exp(-cum)).astype(BF16)
            ks_ref[bb] = (k * jnp.exp(tot - cum)).astype(BF16)
            for ch in range(tm // GLA_CHUNK):
                ed_ref[bb, ch] = jnp.exp(tot[ch * GLA_CHUNK:ch * GLA_CHUNK + 1])
        ov[bb] = _dot(h, wv[...]).astype(ov.dtype)
        og[bb] = _dot(h, wg[...]).astype(og.dtype)
        u = _dot(h, wu[...])
        for qb in range(ou.shape[0]):
            ou[qb, bb] = u[:, qb * LANES:(qb + 1) * LANES]


def _inproj0(ctx, x, mods, norm_g, weights, wa_pad, ba, nct):
    bsz, n_lat, d = x.shape
    t = ctx.shape[1] + n_lat
    tm, tb = TOKEN_TILE, TOKEN_TILE_BATCH
    mod_spec = pl.BlockSpec((tb, 1, 6, d), lambda b, i: (b, (i >= nct).astype(I32), 0, 0))
    ctx_spec, lat_spec = _split_specs(nct, d, tb)
    pos = jnp.arange(tm)
    same_chunk = (pos[:, None] // GLA_CHUNK) == (pos[None, :] // GLA_CHUNK)
    tri = jnp.stack([same_chunk & (pos[None, :] <= pos[:, None]),
                     same_chunk & (pos[None, :] >= pos[:, None])]).astype(BF16)
    ones = same_chunk.astype(BF16)
    consts = list(weights) + [wa_pad, ba, tri, ones]
    tok = lambda w, dt: (pl.BlockSpec((tb, tm, w), lambda b, i: (b, i, 0)), jax.ShapeDtypeStruct((bsz, t, w), dt))
    per_chunk = (pl.BlockSpec((tb, tm // GLA_CHUNK, 1, AB_QK), lambda b, i: (b, i, 0, 0)),
                 jax.ShapeDtypeStruct((bsz, t // GLA_CHUNK, 1, AB_QK), F32))
    one_dir = [tok(AB_QK, BF16)] * 3 + [per_chunk]
    u_blocks = (pl.BlockSpec((S5_CH // LANES, tb, tm, LANES), lambda b, i: (0, b, i, 0)),
                jax.ShapeDtypeStruct((S5_CH // LANES, bsz, t, LANES), F32))
    outs = one_dir + one_dir + [tok(AB_V, BF16), tok(AB_V, BF16), u_blocks]
    return pl.pallas_call(
        functools.partial(_inproj0_body, nct),
        grid=(bsz // tb, t // tm),
        in_specs=[ctx_spec, lat_spec, mod_spec, _const_spec((1, d))] + [_const_spec(a.shape) for a in consts],
        out_specs=[o[0] for o in outs],
        out_shape=[o[1] for o in outs],
        compiler_params=_params("arbitrary", "arbitrary"),
        name="inproj_gla_s5",
    )(ctx, x, mods, norm_g.reshape(1, d), *consts)


def _rope(acc, cos_ref, sin_ref, o_ref, scale):
    tm = o_ref.shape[1]
    for grp in range(acc.shape[1] // LANES):
        half = grp % 2
        cs = cos_ref[:, half * LANES:(half + 1) * LANES]
        sn = sin_ref[:, half * LANES:(half + 1) * LANES]
        for bb in range(o_ref.shape[0]):
            xg = acc[bb * tm:(bb + 1) * tm, grp * LANES:(grp + 1) * LANES]
            out = xg * cs + pltpu.roll(xg, LANES // 2, 1) * sn
            o_ref[bb, :, grp * LANES:(grp + 1) * LANES] = (out * scale).astype(o_ref.dtype)


def _moe_residual(x1_ref, g2, yk_ref, w_ref, bb=0):
    d = x1_ref.shape[2]
    half = d // 2
    y_lo, y_hi = None, None
    for k in range(TOP_K):
        lo, hi = _unpack_rows(yk_ref[k, bb])
        wk = w_ref[bb, :, k:k + 1]
        y_lo = lo * wk if y_lo is None else y_lo + lo * wk
        y_hi = hi * wk if y_hi is None else y_hi + hi * wk
    return x1_ref[bb, :, 0:half] + g2[:, 0:half] * y_lo, x1_ref[bb, :, half:d] + g2[:, half:d] * y_hi


def _inproj1_body(x1_ref, mod0_ref, yk_ref, w_ref, mod_ref, g_ref, cos_ref, sin_ref, wq, wk, wv, wg, ox, oq, ok, ov, og):
    half = x1_ref.shape[2] // 2
    for bb in range(x1_ref.shape[0]):
        x2_lo, x2_hi = _moe_residual(x1_ref, mod0_ref[bb, 0][5:6], yk_ref, w_ref, bb)
        ox[bb, :, 0:half] = x2_lo
        ox[bb, :, half:2 * half] = x2_hi
        m = mod_ref[bb, 0]
        h = _norm_mod(jnp.concatenate([x2_lo, x2_hi], axis=1), g_ref[...], m[0:1], m[1:2]).astype(BF16)
        _rope(_dot(h, wq[...]), cos_ref, sin_ref, oq.at[bb:bb + 1], 1.0)
        _rope(_dot(h, wk[...]), cos_ref, sin_ref, ok.at[bb:bb + 1], RET_DK ** -0.5)
        ov[bb] = _dot(h, wv[...]).astype(ov.dtype)
        og[bb] = _dot(h, wg[...]).astype(og.dtype)


def _inproj1(x1, mods0, yk, w_tok, mods, norm_g, cos_t, sin_t, weights, nct):
    bsz, t, d = x1.shape
    tm, tb = TOKEN_TILE, TOKEN_TILE_BATCH
    mod_spec = pl.BlockSpec((tb, 1, 6, d), lambda b, i: (b, (i >= nct).astype(I32), 0, 0))
    tok = lambda w: pl.BlockSpec((tb, tm, w), lambda b, i: (b, i, 0))
    lat = lambda w: pl.BlockSpec((tb, tm, w), lambda b, i: (b, jnp.maximum(i - nct, 0), 0))
    tab_spec = pl.BlockSpec((tm, RET_DK), lambda b, i: (i, 0))
    wq, wk, wv, wg = weights
    n_lat = t - nct * tm
    return pl.pallas_call(
        _inproj1_body,
        grid=(bsz // tb, t // tm),
        in_specs=[tok(d), mod_spec, pl.BlockSpec((TOP_K, tb, tm, d // 2), lambda b, i: (0, b, i, 0)), tok(TOP_K),
                  mod_spec, _const_spec((1, d)), tab_spec, tab_spec] + [_const_spec(w.shape) for w in weights],
        out_specs=[lat(d), tok(wq.shape[1]), tok(wk.shape[1]), tok(wv.shape[1]), lat(wg.shape[1])],
        out_shape=[jax.ShapeDtypeStruct((bsz, n_lat, d), F32), jax.ShapeDtypeStruct((bsz, t, wq.shape[1]), BF16),
                   jax.ShapeDtypeStruct((bsz, t, wk.shape[1]), BF16), jax.ShapeDtypeStruct((bsz, t, wv.shape[1]), BF16),
                   jax.ShapeDtypeStruct((bsz, n_lat, wg.shape[1]), BF16)],
        compiler_params=_params("arbitrary", "arbitrary"),
        name="inproj_retention",
    )(x1, mods0, yk.reshape(TOP_K, bsz, t, d // 2), w_tok.reshape(bsz, t, TOP_K), mods, norm_g.reshape(1, d),
      cos_t, sin_t, *weights)


def _backward_chunk(n, n_ctx_chunks, n_chunks):
    return jnp.where(n < n_ctx_chunks, n_ctx_chunks - 1 - n, n_chunks - 1 - (n - n_ctx_chunks))


def _gla_body(qd_f, ki_f, ks_f, ed_f, v_f, qd_b, ki_b, ks_b, ed_b, v_b, hmask_ref, bdmask_ref, o_f, o_b, st_f, st_b):
    c = GLA_CHUNK

    @pl.when(pl.program_id(1) == 0)
    def _():
        st_f[...] = jnp.zeros_like(st_f)
        st_b[...] = jnp.zeros_like(st_b)

    r4 = lax.broadcasted_iota(I32, (GLA_HEADS * c, c), 0) & (c - 1)
    c4 = lax.broadcasted_iota(I32, (GLA_HEADS * c, c), 1)
    dirs = ((qd_f, ki_f, ks_f, ed_f, v_f, o_f, st_f), (qd_b, ki_b, ks_b, ed_b, v_b, o_b, st_b))
    chains = [(bb, d) + dirs[d] for bb in range(qd_f.shape[0]) for d in range(2)]
    scores, inter, grow = [], [], []
    for bb, d, qd_ref, ki_ref, ks_ref, ed_ref, v_ref, o_ref, st_ref in chains:
        q_dec = qd_ref[bb]
        q_heads = jnp.concatenate([q_dec] * GLA_HEADS, axis=0) * hmask_ref[...]
        seen4 = (c4 <= r4) if d == 0 else (c4 >= r4)
        scores.append(jnp.where(seen4, _dot_nt(q_heads, ki_ref[bb]), 0.0).astype(BF16))
        inter.append(_dot_nt(q_dec, st_ref[bb].astype(BF16)))
        grow.append(_dot_tn(v_ref[bb], ks_ref[bb]))
    for (bb, d, qd_ref, ki_ref, ks_ref, ed_ref, v_ref, o_ref, st_ref), sc, o_inter, dst in zip(chains, scores, inter, grow):
        v = v_ref[bb]
        o_intra = jnp.concatenate(
            [_dot(sc[h * c:(h + 1) * c], v[:, h * GLA_DV:(h + 1) * GLA_DV]) for h in range(GLA_HEADS)], axis=1)
        o_ref[bb] = o_intra + o_inter
        st_ref[bb] = st_ref[bb] * ed_ref[bb, 0] + bdmask_ref[...] * dst


def _gla(per_dir, v, n_ctx):
    bsz, t, _ = v.shape
    nc, ncc = t // GLA_CHUNK, n_ctx // GLA_CHUNK
    gb = GLA_BATCH
    fwd = lambda b, n: (b, n, 0)
    bwd = lambda b, n: (b, _backward_chunk(n, ncc, nc), 0)
    hmask = (jnp.arange(AB_QK)[:, None] // GLA_CHUNK == jnp.arange(AB_QK)[None, :] // GLA_DK).astype(BF16)
    bdmask = (jnp.arange(AB_V)[:, None] // GLA_DV == jnp.arange(AB_QK)[None, :] // GLA_DK).astype(F32)

    def specs(idx):
        idx4 = lambda b, n: idx(b, n) + (0,)
        return [pl.BlockSpec((gb, GLA_CHUNK, AB_QK), idx)] * 3 + [pl.BlockSpec((gb, 1, 1, AB_QK), idx4),
                                                                  pl.BlockSpec((gb, GLA_CHUNK, AB_V), idx)]

    return pl.pallas_call(
        _gla_body,
        grid=(bsz // gb, nc),
        in_specs=specs(fwd) + specs(bwd) + [_const_spec(hmask.shape), _const_spec(bdmask.shape)],
        out_specs=[pl.BlockSpec((gb, GLA_CHUNK, AB_V), fwd), pl.BlockSpec((gb, GLA_CHUNK, AB_V), bwd)],
        out_shape=[jax.ShapeDtypeStruct((bsz, t, AB_V), F32)] * 2,
        scratch_shapes=[pltpu.VMEM((gb, AB_V, AB_QK), F32)] * 2,
        compiler_params=_params("arbitrary", "arbitrary"),
        name="gla_scan",
    )(*per_dir[0], v, *per_dir[1], v, hmask, bdmask)


def _cmul(x, y):
    return x[0] * y[0] - x[1] * y[1], x[0] * y[1] + x[1] * y[0]


def _s5_operators(lam_re, lam_im, log_step, b_re, b_im, c_re, c_im):
    ln = S5_CHUNK
    step = jnp.exp(log_step.astype(F32))[..., None]
    lam_re, lam_im = lam_re.astype(F32), lam_im.astype(F32)
    mag = jnp.exp(lam_re * step)
    a = (mag * jnp.cos(lam_im * step), mag * jnp.sin(lam_im * step))
    den = lam_re * lam_re + lam_im * lam_im
    f_re = ((a[0] - 1.0) * lam_re + a[1] * lam_im) / den
    f_im = (a[1] * lam_re - (a[0] - 1.0) * lam_im) / den
    bt_re, bt_im = b_re.transpose(0, 2, 1), b_im.transpose(0, 2, 1)
    bb = _cmul((f_re[:, :, None, :], f_im[:, :, None, :]), (bt_re, bt_im))
    bbt = jnp.concatenate([bb[0], -bb[1]], axis=-1)
    pw = (a[0][:, :, None, :], a[1][:, :, None, :])
    while pw[0].shape[2] < ln:
        top = (pw[0][:, :, -1:, :], pw[1][:, :, -1:, :])
        nxt = _cmul(top, pw)
        pw = (jnp.concatenate([pw[0], nxt[0]], axis=2), jnp.concatenate([pw[1], nxt[1]], axis=2))
    pw = (jnp.concatenate([jnp.ones_like(pw[0][:, :, :1]), pw[0]], axis=2),
          jnp.concatenate([jnp.zeros_like(pw[1][:, :, :1]), pw[1]], axis=2))
    ca = _cmul((c_re[:, :, None], c_im[:, :, None]), (pw[0][:, :, :, None, :], pw[1][:, :, :, None, :]))
    by_dir = lambda arr, lo, flip_d: jnp.stack([jnp.flip(arr[d, :, lo:lo + ln], axis=1) if d == flip_d
                                                else arr[d, :, lo:lo + ln] for d in range(2)])
    rows = lambda arr: arr.reshape(2, S5_GROUPS, ln * S5_GROUP, 2 * S5_P)
    cab = rows(by_dir(jnp.concatenate([ca[0], ca[1]], axis=-1), 0, 1))
    cab2 = rows(by_dir(jnp.concatenate([ca[0], -ca[1]], axis=-1), 1, 1)).astype(BF16)
    pwx = by_dir(jnp.concatenate([pw[0], pw[1]], axis=-1), 0, 0)
    lr, li = pw[0][:, :, ln], pw[1][:, :, ln]
    ac_rows = [jnp.concatenate([lr, lr], -1), jnp.concatenate([-li, li], -1), jnp.concatenate([li, -li], -1)]
    ac = jnp.stack(ac_rows + [jnp.zeros_like(ac_rows[0])] * 5, axis=2)
    return cab, cab2, bbt, pwx, ac


def _s5_group_operators(gg, cab_ref, bbt_ref, pwx_ref, tz, wx):
    ln, ch, p = S5_CHUNK, S5_GROUP, S5_P
    lane = lax.broadcasted_iota(I32, (ch, ln * ch), 1)
    for d in range(2):
        kern = _dot3(bbt_ref[d, gg], cab_ref[d, gg], dot=_dot_nt)
        bt = bbt_ref[d, gg]
        b_re, b_im = bt[:, 0:p], -bt[:, p:2 * p]
        for j in range(ln):
            if d == 0:
                blk = jnp.where(lane >= j * ch, kern if j == 0 else pltpu.roll(kern, j * ch, 1), 0.0)
            else:
                blk = jnp.where(lane < (j + 1) * ch, kern if j == ln - 1 else pltpu.roll(kern, (j + 1) * ch, 1), 0.0)
            tz[gg, d, j * ch:(j + 1) * ch, :] = blk.astype(BF16)
            pr, pi = pwx_ref[d, gg, j:j + 1, 0:p], pwx_ref[d, gg, j:j + 1, p:2 * p]
            x_re, x_im = pr * b_re - pi * b_im, pr * b_im + pi * b_re
            wx[gg, d, j * ch:(j + 1) * ch, :] = jnp.concatenate([x_re, x_im, x_im, x_re], axis=1).astype(BF16)


def _s5_placement(pall):
    rows, cols = pall.shape[1], pall.shape[2]
    row = lax.broadcasted_iota(I32, (rows, cols), 0)
    col = lax.broadcasted_iota(I32, (rows, cols), 1)
    same_token = (row // LANES) == (col // S5_GROUP)
    for g8 in range(pall.shape[0]):
        pall[g8] = jnp.where(same_token & ((row % LANES) == g8 * S5_GROUP + (col % S5_GROUP)), 1.0, 0.0).astype(BF16)


def _first_step():
    return (pl.program_id(0) == 0) & (pl.program_id(1) == 0)


def _s5_fold_body(ncs, u_ref, o_ref, pall, ucat):
    @pl.when(_first_step())
    def _():
        _s5_placement(pall)

    for b in range(u_ref.shape[1]):
        for j in range(S5_CHUNK):
            ucat[b * ncs:(b + 1) * ncs, j * LANES:(j + 1) * LANES] = u_ref[0, b, pl.ds(j, ncs, stride=S5_CHUNK), :].astype(BF16)
    for g8 in range(pall.shape[0]):
        o_ref[g8] = _dot(ucat[...], pall[g8]).astype(BF16)


def _s5_unfold_body(ncs, y_ref, o_ref, pall):
    @pl.when(_first_step())
    def _():
        _s5_placement(pall)

    def token_pair(i2, carry):
        r0 = pl.multiple_of(i2 * 2 * LANES, 2 * LANES)
        acc = _dot_nt(y_ref[0], pall[0, pl.ds(r0, 2 * LANES), :])
        for g8 in range(1, pall.shape[0]):
            acc = acc + _dot_nt(y_ref[g8], pall[g8, pl.ds(r0, 2 * LANES), :])
        for b in range(o_ref.shape[1]):
            for par in range(2):
                o_ref[0, b, pl.ds(2 * i2 + par, ncs, stride=S5_CHUNK), :] = (
                    acc[b * ncs:(b + 1) * ncs, par * LANES:(par + 1) * LANES])
        return carry

    lax.fori_loop(0, S5_CHUNK // 2, token_pair, 0)


def _s5_body(ncs_ctx, ncs, rows, u_ref, cab_ref, cab2_ref, bbt_ref, pwx_ref, ac_ref, y_ref, tz, wx, *vecs):
    half = 2 * S5_P
    n_groups = u_ref.shape[0]
    groups = [vecs[6 * gg:6 * gg + 6] for gg in range(n_groups)]
    for gg, (xx_f, xs_f, xx_b, xs_b, _, _) in enumerate(groups):
        _s5_group_operators(gg, cab_ref, bbt_ref, pwx_ref, tz, wx)
        for d, (xx, xs) in enumerate(((xx_f, xs_f), (xx_b, xs_b))):
            r = _dot(u_ref[gg], wx[gg, d])
            xx[...] = r[:, :half]
            xs[...] = r[:, half:]

    def advance(ac, s, s_sw, x, x_sw):
        return ac[0:1] * s + ac[1:2] * s_sw + x, ac[0:1] * s_sw + ac[2:3] * s + x_sw

    def step(n, carry):
        at_f = pl.ds(n, rows, stride=ncs)
        at_b = pl.ds(_backward_chunk(n, ncs_ctx, ncs), rows, stride=ncs)
        out = []
        for gg, (xx_f, xs_f, xx_b, xs_b, sin_f, sin_b) in enumerate(groups):
            s_f, sw_f, s_b, sw_b = carry[4 * gg:4 * gg + 4]
            sin_f[at_f, :] = s_f
            sin_b[at_b, :] = s_b
            out += advance(ac_ref[0, gg], s_f, sw_f, xx_f[at_f, :], xs_f[at_f, :])
            out += advance(ac_ref[1, gg], s_b, sw_b, xx_b[at_b, :], xs_b[at_b, :])
        return tuple(out)

    zero = jnp.zeros((rows, half), F32)
    lax.fori_loop(0, ncs, step, (zero,) * (4 * n_groups))
    for gg, (_, _, _, _, sin_f, sin_b) in enumerate(groups):
        u = u_ref[gg]
        y_ref[gg] = (_dot(u, tz[gg, 0]) + _dot(u, tz[gg, 1]) + _dot_nt(sin_f[...].astype(BF16), cab2_ref[0, gg])
                     + _dot_nt(sin_b[...].astype(BF16), cab2_ref[1, gg])).astype(BF16)


def _s5(u4, ops, n_ctx):
    nq, bsz, t, _ = u4.shape
    ln, lanes = S5_CHUNK, S5_CHUNK * S5_GROUP
    gq = S5_GROUPS // nq
    ncs, ncs_ctx = t // ln, n_ctx // ln
    m = ncs * bsz
    hb = S5_FOLD_BATCH
    tok_spec = pl.BlockSpec((1, hb, t, LANES), lambda q, h: (q, h, 0, 0))
    grp_spec = pl.BlockSpec((gq, hb * ncs, lanes), lambda q, h: (q, h, 0))
    pall = pltpu.VMEM((gq, ln * LANES, lanes), BF16)
    ug = pl.pallas_call(
        functools.partial(_s5_fold_body, ncs),
        grid=(nq, bsz // hb),
        in_specs=[tok_spec],
        out_specs=grp_spec,
        out_shape=jax.ShapeDtypeStruct((S5_GROUPS, m, lanes), BF16),
        scratch_shapes=[pall, pltpu.VMEM((hb * ncs, ln * LANES), BF16)],
        compiler_params=_params("arbitrary", "arbitrary"),
        name="s5_fold",
    )(u4)
    sg = S5_SCAN_GROUPS
    dir_spec = lambda arr: pl.BlockSpec((2, sg) + arr.shape[2:], lambda g: (0, g, 0, 0))
    yg = pl.pallas_call(
        functools.partial(_s5_body, ncs_ctx, ncs, bsz),
        grid=(S5_GROUPS // sg,),
        in_specs=[pl.BlockSpec((sg, m, lanes), lambda g: (g, 0, 0))] + [dir_spec(arr) for arr in ops],
        out_specs=pl.BlockSpec((sg, m, lanes), lambda g: (g, 0, 0)),
        out_shape=jax.ShapeDtypeStruct((S5_GROUPS, m, lanes), BF16),
        scratch_shapes=[pltpu.VMEM((sg, 2, lanes, lanes), BF16)] * 2 + [pltpu.VMEM((m, 2 * S5_P), F32)] * (6 * sg),
        compiler_params=_params("arbitrary"),
        name="s5_scan",
    )(ug, *ops)
    return pl.pallas_call(
        functools.partial(_s5_unfold_body, ncs),
        grid=(nq, bsz // hb),
        in_specs=[grp_spec],
        out_specs=tok_spec,
        out_shape=jax.ShapeDtypeStruct(u4.shape, F32),
        scratch_shapes=[pall],
        compiler_params=_params("arbitrary", "arbitrary"),
        name="s5_unfold",
    )(yg)


def _ret_body(q_f, k_f, v_f, q_b, k_b, v_b, dmat_ref, rsc_ref, csc_ref, gam_ref, o_f, o_b, st_f, st_b):
    @pl.when(pl.program_id(1) == 0)
    def _():
        st_f[...] = jnp.zeros_like(st_f)
        st_b[...] = jnp.zeros_like(st_b)

    dirs = ((q_f, k_f, v_f, o_f, st_f), (q_b, k_b, v_b, o_b, st_b))
    for bb in range(q_f.shape[0]):
        for d, (q_ref, k_ref, v_ref, o_ref, st_ref) in enumerate(dirs):
            for h in range(RET_HEADS):
                qh = q_ref[bb, :, h * RET_DK:(h + 1) * RET_DK]
                kh = k_ref[bb, :, h * RET_DK:(h + 1) * RET_DK]
                vh = v_ref[bb, :, h * RET_DV:(h + 1) * RET_DV]
                st = st_ref[bb, h]
                scores = (_dot_nt(qh, kh) * dmat_ref[d, h]).astype(BF16)
                o = _dot(scores, vh) + rsc_ref[d, h] * _dot(qh, st.astype(BF16))
                o_ref[bb, :, h * RET_DV:(h + 1) * RET_DV] = o.astype(o_ref.dtype)
                k_state = (kh.astype(F32) * csc_ref[d, h]).astype(BF16)
                st_ref[bb, h] = st * gam_ref[d, h] + _dot_tn(k_state, vh)


def _retention(q, k, v, decay_logit, n_ctx):
    bsz, t, _ = q.shape
    c = RET_CHUNK
    nc, ncc = t // c, n_ctx // c
    nl = nc - ncc
    rb = RET_BATCH
    log_gamma = jax.nn.log_sigmoid(decay_logit.astype(F32))[:, :, None, None]
    i = jnp.arange(c, dtype=F32)
    lag = i[:, None] - i[None, :]
    lag = jnp.stack([lag, -lag])[:, None]
    dmat = jnp.where(lag >= 0, jnp.exp(log_gamma * jnp.maximum(lag, 0.0)), 0.0)
    done = jnp.stack([i + 1.0, c - i])[:, None, :, None]
    rsc = jnp.exp(log_gamma * done)
    csc = jnp.exp(log_gamma * (c - done))
    gam = jnp.exp(log_gamma[:, :, 0, 0] * c)
    fwd = lambda b, n: (b, n, 0)
    bwd = lambda b, n: (b, _backward_chunk(n, ncc, nc), 0)
    o_fwd = lambda b, n: (b, jnp.maximum(n - ncc, 0), 0)
    o_bwd = lambda b, n: (b, nl - 1 - jnp.maximum(n - ncc, 0), 0)

    def specs(idx):
        return [pl.BlockSpec((rb, c, RET_QK), idx), pl.BlockSpec((rb, c, RET_QK), idx), pl.BlockSpec((rb, c, RET_MIX), idx)]

    return pl.pallas_call(
        _ret_body,
        grid=(bsz // rb, nc),
        in_specs=specs(fwd) + specs(bwd) + [_const_spec(dmat.shape), _const_spec(rsc.shape), _const_spec(csc.shape),
                                            pl.BlockSpec(memory_space=pltpu.SMEM)],
        out_specs=[pl.BlockSpec((rb, c, RET_MIX), o_fwd), pl.BlockSpec((rb, c, RET_MIX), o_bwd)],
        out_shape=[jax.ShapeDtypeStruct((bsz, nl * c, RET_MIX), BF16)] * 2,
        scratch_shapes=[pltpu.VMEM((rb, RET_HEADS, RET_DK, RET_DV), F32)] * 2,
        compiler_params=_params("arbitrary", "arbitrary"),
        name="retention_scan",
    )(q, k, v, q, k, v, dmat, rsc, csc, gam)


def _zero_counts_at_start(cnt_ref):
    @pl.when(_first_step())
    def _():
        cnt_ref[...] = jnp.zeros_like(cnt_ref)


def _route(xs, mixed, mods, n2g_ref, wr_ref, br_ref, x1_ref, h2_ref, e_ref, w_ref, r_ref, cnt_ref):
    rows = xs[0].shape[0]
    h2s = []
    for bb, (x, mod) in enumerate(zip(xs, mods)):
        x1 = x + mod[2:3] * mixed[bb * rows:(bb + 1) * rows]
        x1_ref[bb] = x1
        h2s.append(_norm_mod(x1, n2g_ref[...], mod[3:4], mod[4:5]))
        h2_ref[bb] = _pack_rows(h2s[bb])
    h2 = h2s[0] if len(h2s) == 1 else jnp.concatenate(h2s, axis=0)
    tm = h2.shape[0]
    logits = _dot3(wr_ref[...], h2, dot=_dot_nt) + br_ref[...]
    ie = lax.broadcasted_iota(I32, logits.shape, 0)
    tops, picks = [], []
    for _ in range(TOP_K):
        mx = jnp.max(logits, axis=0, keepdims=True)
        pick = jnp.min(jnp.where(logits == mx, ie, N_EXPERTS), axis=0, keepdims=True)
        tops.append(mx)
        picks.append(pick)
        logits = jnp.where(ie == pick, -jnp.inf, logits)
    ex = [jnp.exp(tk - tops[0]) for tk in tops]
    den = ex[0] + ex[1] + ex[2] + ex[3]
    for kk in range(TOP_K):
        w_ref[0, kk:kk + 1, :] = ex[kk] / den
        e_ref[0, kk:kk + 1, :] = picks[kk]

    earlier = (lax.broadcasted_iota(I32, (tm, tm), 0) < lax.broadcasted_iota(I32, (tm, tm), 1))
    earlier = jnp.where(earlier, 1.0, 0.0).astype(BF16)
    run = cnt_ref[:, 0:1]
    for kk, pick in enumerate(picks):
        onehot = jnp.where(ie == pick, 1.0, 0.0)
        before = _dot(onehot.astype(BF16), earlier) + run
        r_ref[0, kk:kk + 1, :] = jnp.sum(onehot * before, axis=0, keepdims=True).astype(I32)
        run = run + jnp.sum(onehot, axis=1, keepdims=True)
    cnt_ref[...] = jnp.broadcast_to(run, cnt_ref.shape)


def _mix0_body(nct, ctx_ref, lat_ref, mod_ref, of_ref, ob_ref, g_ref, ys_ref, u_ref, gng_ref, dsk_ref, gluw_ref,
               glub_ref, wo_ref, n2g_ref, wr_ref, br_ref, x1_ref, h2_ref, e_ref, w_ref, r_ref, cnt_ref):
    _zero_counts_at_start(cnt_ref)
    tb, tm = of_ref.shape[0], of_ref.shape[1]
    rows = lambda ref: ref[...].reshape(tb * tm, ref.shape[-1])
    o = rows(of_ref) + rows(ob_ref)
    heads = []
    for h in range(GLA_HEADS):
        oh = o[:, h * GLA_DV:(h + 1) * GLA_DV]
        heads.append(oh * lax.rsqrt(jnp.mean(oh * oh, axis=-1, keepdims=True) + EPS))
    gla = jnp.concatenate(heads, axis=1) * gng_ref[...] * _silu(rows(g_ref).astype(F32))
    lane_blocks = lambda ref: jnp.concatenate([ref[qb].reshape(tb * tm, LANES) for qb in range(ref.shape[0])], axis=1)
    y = jax.nn.gelu(lane_blocks(ys_ref) + dsk_ref[...] * lane_blocks(u_ref))
    y = y * jax.nn.sigmoid(_dot(y.astype(BF16), gluw_ref[...]) + glub_ref[...])
    mixed = _dot(gla.astype(BF16), wo_ref[0:AB_V]) + _dot(y.astype(BF16), wo_ref[AB_V:AB_V + S5_CH])
    _route([_stream_tile(nct, ctx_ref, lat_ref, bb) for bb in range(tb)], mixed, [mod_ref[bb, 0] for bb in range(tb)],
           n2g_ref, wr_ref, br_ref, x1_ref, h2_ref, e_ref, w_ref, r_ref, cnt_ref)


def _mix1_body(x_ref, mod_ref, of_ref, ob_ref, g_ref, ng_ref, wo_ref, n2g_ref, wr_ref, br_ref,
               x1_ref, h2_ref, e_ref, w_ref, r_ref, cnt_ref):
    _zero_counts_at_start(cnt_ref)
    mixed = None
    for h in range(RET_HEADS):
        sl = slice(h * RET_DV, (h + 1) * RET_DV)
        oh = of_ref[0, :, sl].astype(F32) + ob_ref[0, :, sl].astype(F32)
        mu = jnp.mean(oh, axis=-1, keepdims=True)
        cen = oh - mu
        var = jnp.mean(cen * cen, axis=-1, keepdims=True)
        gated = cen * lax.rsqrt(var + EPS) * ng_ref[:, sl] * _silu(g_ref[0, :, sl].astype(F32))
        part = _dot(gated.astype(BF16), wo_ref[sl])
        mixed = part if mixed is None else mixed + part
    _route([x_ref[0]], mixed, [mod_ref[0, 0]], n2g_ref, wr_ref, br_ref, x1_ref, h2_ref, e_ref, w_ref, r_ref, cnt_ref)


def _mix_call(body, name, stream, mods, tiles, acts, consts, norm2_g, w_router, b_router, n_tok, seg_tile0,
              tm=TOKEN_TILE, tb=1):
    bsz, _, d = stream[-1].shape
    off = lambda b, i: (b, i + seg_tile0, 0)
    loc = lambda b, i: (b, i, 0)
    ntl = bsz // tb * tiles
    flat = lambda b, i: (b * tiles + i, 0, 0)
    in_specs = list(_split_specs(n_tok, d, tb)) if len(stream) == 2 else [pl.BlockSpec((tb, tm, d), off)]
    in_specs.append(pl.BlockSpec((tb, 1, 6, d), lambda b, i: (b, ((i + seg_tile0) >= n_tok).astype(I32), 0, 0)))
    args = list(stream) + [mods]
    for arr, offset in acts:
        if arr.ndim == 4:
            in_specs.append(pl.BlockSpec((arr.shape[0], tb, tm, arr.shape[3]), lambda b, i: (0, b, i, 0)))
        else:
            in_specs.append(pl.BlockSpec((tb, tm, arr.shape[2]), off if offset else loc))
        args.append(arr)
    tail = list(consts) + [norm2_g.reshape(1, d), w_router.T, b_router.reshape(N_EXPERTS, 1)]
    in_specs += [_const_spec(a.shape) for a in tail]
    args += tail
    tok_out = pl.BlockSpec((1, TOP_K, tb * tm), flat)
    return pl.pallas_call(
        body,
        grid=(bsz // tb, tiles),
        in_specs=in_specs,
        out_specs=[pl.BlockSpec((tb, tm, d), loc), pl.BlockSpec((tb, tm, d // 2), loc), tok_out, tok_out, tok_out,
                   _const_spec((N_EXPERTS, LANES))],
        out_shape=[jax.ShapeDtypeStruct((bsz, tiles * tm, d), F32), jax.ShapeDtypeStruct((bsz, tiles * tm, d // 2), U32),
                   jax.ShapeDtypeStruct((ntl, TOP_K, tb * tm), I32), jax.ShapeDtypeStruct((ntl, TOP_K, tb * tm), F32),
                   jax.ShapeDtypeStruct((ntl, TOP_K, tb * tm), I32), jax.ShapeDtypeStruct((N_EXPERTS, LANES), F32)],
        compiler_params=_params("arbitrary", "arbitrary"),
        name=name,
    )(*args)


def _cast_rows(src_ref, dst_ref, rows):
    def chunk(j, carry):
        r = pl.multiple_of(j * rows, rows)
        dst_ref[pl.ds(r, rows), :] = src_ref[0, 0, pl.ds(r, rows), :].astype(BF16)
        return carry

    lax.fori_loop(0, dst_ref.shape[0] // rows, chunk, 0)


def _expert_body(be_ref, nu_ref, x_ref, wgu_ref, bgu_ref, wd_ref, bd_ref, o_ref, wgu_bf, wd_bf):
    i = pl.program_id(0)
    live = i < nu_ref[0]
    new_expert = (i == 0) | (be_ref[i] != be_ref[jnp.maximum(i - 1, 0)])

    @pl.when(live & new_expert)
    def _():
        _cast_rows(wgu_ref, wgu_bf, 128)
        _cast_rows(wd_ref, wd_bf, 128)

    @pl.when(live)
    def _():
        rows = x_ref.shape[0] // EXPERT_ROW_PARTS
        for part in range(EXPERT_ROW_PARTS):
            sl = slice(part * rows, (part + 1) * rows)
            x_lo, x_hi = _unpack_rows(x_ref[sl, :])
            half = x_lo.shape[1]
            gu = (_dot(x_lo.astype(BF16), wgu_bf[0:half]) + _dot(x_hi.astype(BF16), wgu_bf[half:2 * half])
                  + bgu_ref[0, 0])
            gate = jnp.minimum(gu[:, :D_FF], SWIGLU_LIMIT)
            lin = jnp.clip(gu[:, D_FF:], -SWIGLU_LIMIT, SWIGLU_LIMIT)
            act = gate * jax.nn.sigmoid(SWIGLU_ALPHA * gate) * (lin + 1.0)
            y = _dot(act.astype(BF16), wd_bf[...]) + bd_ref[0, 0]
            o_ref[sl, :] = _pack_rows(y)

    @pl.when(i >= nu_ref[0])
    def _():
        o_ref[...] = jnp.zeros_like(o_ref)


def _experts(xb, block_e, n_used, layer, w_gu, b_gu, w_down, b_down):
    n_slots, half = xb.shape
    d = 2 * half
    n_blocks = n_slots // MOE_BLOCK
    depth = w_gu.shape[0]
    by_expert = lambda i, be, nu: (layer, be[i], 0, 0)
    return pl.pallas_call(
        _expert_body,
        grid_spec=pltpu.PrefetchScalarGridSpec(
            num_scalar_prefetch=2,
            grid=(n_blocks,),
            in_specs=[pl.BlockSpec((MOE_BLOCK, half), lambda i, be, nu: (i, 0)),
                      pl.BlockSpec((1, 1, d, 2 * D_FF), by_expert), pl.BlockSpec((1, 1, 1, 2 * D_FF), by_expert),
                      pl.BlockSpec((1, 1, D_FF, d), by_expert), pl.BlockSpec((1, 1, 1, d), by_expert)],
            out_specs=pl.BlockSpec((MOE_BLOCK, half), lambda i, be, nu: (i, 0)),
            scratch_shapes=[pltpu.VMEM((d, 2 * D_FF), BF16), pltpu.VMEM((D_FF, d), BF16)]),
        out_shape=jax.ShapeDtypeStruct((n_slots, half), U32),
        compiler_params=_params("arbitrary"),
        name="moe_experts",
    )(block_e, n_used, xb, w_gu, b_gu.reshape(depth, N_EXPERTS, 1, 2 * D_FF), w_down,
      b_down.reshape(depth, N_EXPERTS, 1, d))


def _combine_body(x1_ref, mod_ref, yk_ref, w_ref, fg_ref, o_ref):
    d = x1_ref.shape[2]
    half = d // 2
    x2_lo, x2_hi = _moe_residual(x1_ref, mod_ref[0, 0][5:6], yk_ref, w_ref)
    ms = (jnp.sum(x2_lo * x2_lo, axis=-1, keepdims=True) + jnp.sum(x2_hi * x2_hi, axis=-1, keepdims=True)) / d
    r = lax.rsqrt(ms + EPS)
    o_ref[0, :, 0:half] = x2_lo * r * fg_ref[:, 0:half]
    o_ref[0, :, half:d] = x2_hi * r * fg_ref[:, half:d]


def _combine(x1, mods, yk, w_tok, seg_tile0, n_tok, final_g, tm):
    bsz, t, d = x1.shape
    loc = lambda b, i: (b, i, 0)
    return pl.pallas_call(
        _combine_body,
        grid=(bsz, t // tm),
        in_specs=[pl.BlockSpec((1, tm, d), loc),
                  pl.BlockSpec((1, 1, 6, d), lambda b, i: (b, ((i + seg_tile0) >= n_tok).astype(I32), 0, 0)),
                  pl.BlockSpec((TOP_K, 1, tm, d // 2), lambda b, i: (0, b, i, 0)),
                  pl.BlockSpec((1, tm, TOP_K), loc), _const_spec((1, d))],
        out_specs=pl.BlockSpec((1, tm, d), loc),
        out_shape=jax.ShapeDtypeStruct((bsz, t, d), F32),
        compiler_params=_params("arbitrary", "arbitrary"),
        name="moe_combine",
    )(x1, mods, yk.reshape(TOP_K, bsz, t, d // 2), w_tok.reshape(bsz, t, TOP_K), final_g.reshape(1, d))


def _sc_mesh():
    return plsc.VectorSubcoreMesh(core_axis_name="core", subcore_axis_name="subcore",
                                  num_cores=SC_CORES, num_subcores=SC_SUBCORES)


def _sc_worker_base(per_worker):
    return (lax.axis_index("subcore") * SC_CORES + lax.axis_index("core")) * per_worker


def _sc_dispatch(rows, dest, n_slots):
    n, w = rows.shape
    per_worker = n // SC_WORKERS
    assert per_worker * SC_WORKERS == n and per_worker % SC_CHUNK == 0

    @functools.partial(
        pl.kernel, mesh=_sc_mesh(), out_type=jax.ShapeDtypeStruct((n_slots, w), rows.dtype),
        scratch_types=[pltpu.VMEM((SC_CHUNK,), I32)] * TOP_K + [pltpu.VMEM((SC_CHUNK, w), rows.dtype),
                                                                pltpu.SemaphoreType.DMA],
        name="moe_dispatch")
    def scatter_rows(rows_hbm, dest_hbm, out_hbm, *scratch):
        idx_refs, buf, sem = scratch[:TOP_K], scratch[TOP_K], scratch[TOP_K + 1]
        base0 = _sc_worker_base(per_worker)

        @pl.loop(0, per_worker // SC_CHUNK)
        def _(j):
            base = base0 + j * SC_CHUNK
            pltpu.sync_copy(rows_hbm.at[pl.ds(base, SC_CHUNK)], buf)
            for k, idx in enumerate(idx_refs):
                pltpu.sync_copy(dest_hbm.at[pl.ds(k * n + base, SC_CHUNK)], idx)
            copies = [pltpu.make_async_copy(buf, out_hbm.at[idx], sem) for idx in idx_refs]
            for cp in copies:
                cp.start()
            for cp in copies:
                cp.wait()

    return scatter_rows(rows, dest)


def _sc_gather(table, idx):
    n = idx.shape[0]
    w = table.shape[1]
    per_worker = n // SC_WORKERS
    n_chunks = per_worker // SC_CHUNK
    assert per_worker * SC_WORKERS == n and n_chunks * SC_CHUNK == per_worker and n_chunks % 2 == 0

    @functools.partial(
        pl.kernel, mesh=_sc_mesh(), out_type=jax.ShapeDtypeStruct((n, w), table.dtype),
        scratch_types=([pltpu.VMEM((SC_CHUNK,), I32)] * 2 + [pltpu.VMEM((SC_CHUNK, w), table.dtype)] * 2
                       + [pltpu.SemaphoreType.DMA] * 4),
        name="moe_gather")
    def gather_rows(table_hbm, idx_hbm, out_hbm, idx0, idx1, buf0, buf1, gsem0, gsem1, wsem0, wsem1):
        base0 = _sc_worker_base(per_worker)

        def gather_copy(idx_v, buf, sem):
            return pltpu.make_async_copy(table_hbm.at[idx_v], buf, sem)

        def write_copy(j, buf, sem):
            return pltpu.make_async_copy(buf, out_hbm.at[pl.ds(base0 + j * SC_CHUNK, SC_CHUNK)], sem)

        def start_gather(j, idx_v, buf, sem):
            pltpu.sync_copy(idx_hbm.at[pl.ds(base0 + j * SC_CHUNK, SC_CHUNK)], idx_v)
            gather_copy(idx_v, buf, sem).start()

        start_gather(0, idx0, buf0, gsem0)

        @pl.loop(0, n_chunks, step=2)
        def _(j):
            @pl.when(j > 0)
            def _():
                write_copy(j - 1, buf1, wsem1).wait()
            start_gather(j + 1, idx1, buf1, gsem1)
            gather_copy(idx0, buf0, gsem0).wait()
            write_copy(j, buf0, wsem0).start()

            @pl.when(j + 2 < n_chunks)
            def _():
                write_copy(j, buf0, wsem0).wait()
                start_gather(j + 2, idx0, buf0, gsem0)
            gather_copy(idx1, buf1, gsem1).wait()
            write_copy(j + 1, buf1, wsem1).start()

        write_copy(n_chunks - 2, buf0, wsem0).wait()
        write_copy(n_chunks - 1, buf1, wsem1).wait()

    return gather_rows(table, idx)


def _moe(h2, e_tl, w_tl, r_tl, cnt, layer, w_gu, b_gu, w_down, b_down, tb=1):
    bsz, t, half = h2.shape
    n = bsz * t
    flat = lambda a: a.reshape(bsz // tb, -1, TOP_K, tb, a.shape[2] // tb).transpose(2, 0, 3, 1, 4).reshape(TOP_K, n)
    e_k, w_k, r_k = flat(e_tl), flat(w_tl), flat(r_tl)
    counts = cnt[:, 0].astype(I32)
    padded = (counts + MOE_BLOCK - 1) // MOE_BLOCK * MOE_BLOCK
    pad_end = jnp.cumsum(padded)
    pad_start = pad_end - padded
    n_blocks = (n * TOP_K + MOE_BLOCK - 1) // MOE_BLOCK + N_EXPERTS
    block_start = jnp.arange(n_blocks, dtype=I32) * MOE_BLOCK
    block_e = jnp.minimum(jnp.sum((pad_end[None, :] <= block_start[:, None]).astype(I32), axis=1), N_EXPERTS - 1)
    n_used = (pad_end[-1:] // MOE_BLOCK).astype(I32)
    start_k = jnp.sum(jnp.where(e_k[..., None] == jnp.arange(N_EXPERTS, dtype=I32), pad_start, 0), axis=-1)
    dest = (start_k + r_k).reshape(TOP_K * n)
    xb = _sc_dispatch(h2.reshape(n, half), dest, n_blocks * MOE_BLOCK)
    yb = _experts(xb, block_e, n_used, layer, w_gu, b_gu, w_down, b_down)
    return _sc_gather(yb, dest), w_k.T


def _rope_tables(n_ctx, n_lat):
    n_freq = RET_DK // 4
    inv_freq = ROPE_BASE ** (-jnp.arange(n_freq, dtype=F32) / n_freq)
    pos = jnp.arange(n_lat, dtype=I32)
    cos, sin = [], []
    for p in (pos // GRID_W, pos % GRID_W):
        ang = p.astype(F32)[:, None] * inv_freq
        cos += [jnp.cos(ang), jnp.cos(ang)]
        sin += [-jnp.sin(ang), jnp.sin(ang)]
    cos, sin = jnp.concatenate(cos, axis=1), jnp.concatenate(sin, axis=1)
    return (jnp.concatenate([jnp.ones((n_ctx, RET_DK), F32), cos], axis=0),
            jnp.concatenate([jnp.zeros((n_ctx, RET_DK), F32), sin], axis=0))


def kernel(x, c, ctx, c_ctx, ada_w, ada_b, norm1_g, norm2_g, ab_w_in, ab_w_out, gla_wa, gla_ba, gla_norm_g, s5_lam_re, s5_lam_im, s5_log_step, s5_b_re, s5_b_im, s5_c_re, s5_c_im, s5_d, s5_glu_w, s5_glu_b, ret_w_in, ret_w_out, ret_decay_logit, ret_norm_g, moe_w_router, moe_b_router, moe_w_gu, moe_b_gu, moe_w_down, moe_b_down, final_norm_g):
    bsz, n_lat, d = x.shape
    n_ctx = ctx.shape[1]
    depth = ada_w.shape[0]
    assert depth == 2 and d == D_MODEL and bsz == 8, "kernels are laid out for the stated problem shape"
    assert n_ctx % TOKEN_TILE == 0 and n_lat % LATENT_TILE == 0 and n_lat % GRID_W == 0
    t = n_ctx + n_lat
    nct = n_ctx // TOKEN_TILE

    cvec = jnp.zeros((16, d), F32).at[:bsz].set(c).at[bsz].set(c_ctx)
    mod = _ada_mod(cvec, ada_w, ada_b).reshape(depth, 16, 6, d)
    mods = [jnp.stack([jnp.broadcast_to(mod[l, bsz], (bsz, 6, d)), mod[l, :bsz]], axis=1) for l in range(depth)]

    w_in = ab_w_in[0].astype(BF16)
    cuts = [0, AB_QK, 2 * AB_QK, 2 * AB_QK + AB_V, 2 * AB_QK + 2 * AB_V, 2 * AB_QK + 2 * AB_V + 2 * GLA_RANK,
            w_in.shape[1]]
    pieces = [w_in[:, a:b] for a, b in zip(cuts[:-1], cuts[1:])]
    wa_pad = jnp.zeros((2, 2 * GLA_RANK, AB_QK), F32)
    wa_pad = wa_pad.at[0, :GLA_RANK].set(gla_wa[0, 0]).at[1, GLA_RANK:].set(gla_wa[0, 1])
    outs = _inproj0(ctx, x, mods[0], norm1_g[0], pieces, wa_pad, gla_ba[0].reshape(2, 1, AB_QK), nct)
    v, g, u = outs[8:]
    o_f, o_b = _gla((outs[0:4], outs[4:8]), v, n_ctx)
    ops = _s5_operators(s5_lam_re[0], s5_lam_im[0], s5_log_step[0], s5_b_re[0], s5_b_im[0], s5_c_re[0], s5_c_im[0])
    ys = _s5(u, ops, n_ctx)
    consts = [jnp.tile(gla_norm_g[0], GLA_HEADS).reshape(1, AB_V), s5_d[0].reshape(1, S5_CH),
              s5_glu_w[0].astype(BF16), s5_glu_b[0].reshape(1, S5_CH), ab_w_out[0].astype(BF16)]
    x1, h2, e_tl, w_tl, r_tl, cnt = _mix_call(
        functools.partial(_mix0_body, nct), "mix_gla_s5", (ctx, x), mods[0], t // TOKEN_TILE,
        [(o_f, False), (o_b, False), (g, False), (ys, False), (u, False)], consts,
        norm2_g[0], moe_w_router[0], moe_b_router[0], nct, 0, tb=TOKEN_TILE_BATCH)
    yk, w_tok = _moe(h2, e_tl, w_tl, r_tl, cnt, 0, moe_w_gu, moe_b_gu, moe_w_down, moe_b_down, tb=TOKEN_TILE_BATCH)

    w_in = ret_w_in[0].astype(BF16)
    cuts = [0, RET_QK, 2 * RET_QK, 2 * RET_QK + RET_MIX, w_in.shape[1]]
    pieces = [w_in[:, a:b] for a, b in zip(cuts[:-1], cuts[1:])]
    cos_t, sin_t = _rope_tables(n_ctx, n_lat)
    x2, q, k, v, g = _inproj1(x1, mods[0], yk, w_tok, mods[1], norm1_g[1], cos_t, sin_t, pieces, nct)
    o_f, o_b = _retention(q, k, v, ret_decay_logit[0], n_ctx)
    consts = [ret_norm_g[0].reshape(1, RET_MIX), ret_w_out[0].astype(BF16)]
    x1, h2, e_tl, w_tl, r_tl, cnt = _mix_call(
        _mix1_body, "mix_retention", (x2,), mods[1], n_lat // LATENT_TILE,
        [(o_f, False), (o_b, False), (g, False)], consts,
        norm2_g[1], moe_w_router[1], moe_b_router[1], 0, 0, tm=LATENT_TILE)
    yk, w_tok = _moe(h2, e_tl, w_tl, r_tl, cnt, 1, moe_w_gu, moe_b_gu, moe_w_down, moe_b_down)
    return _combine(x1, mods[1], yk, w_tok, 0, 0, final_norm_g, LATENT_TILE)
```

```python
import functools
import math

import jax
import jax.numpy as jnp
from jax import lax
from jax.experimental import pallas as pl
from jax.experimental.pallas import tpu as pltpu
from jax.experimental.pallas import tpu_sc as plsc

F32, BF16, I32, U32 = jnp.float32, jnp.bfloat16, jnp.int32, jnp.uint32

D_MODEL = 1024
GRID_W = 64
EPS = 1e-6
GLA_HEADS, GLA_DK, GLA_DV, GLA_RANK, GLA_TAU, GLA_CHUNK = 4, 64, 128, 16, 16.0, 64
GLA_BATCH = 8
AB_QK, AB_V = GLA_HEADS * GLA_DK, GLA_HEADS * GLA_DV
S5_CH, S5_GROUP, S5_GROUPS, S5_P = 512, 16, 32, 64
S5_CHUNK = 16
S5_FOLD_BATCH = 4
S5_SCAN_GROUPS = 2
RET_HEADS, RET_DK, RET_DV = 4, 256, 512
RET_CHUNK = 256
RET_BATCH = 2
RET_QK, RET_MIX = RET_HEADS * RET_DK, RET_HEADS * RET_DV
ROPE_BASE = 10000.0
N_EXPERTS, TOP_K, D_FF = 32, 4, 1024
SWIGLU_LIMIT, SWIGLU_ALPHA = 7.0, 1.702
MOE_BLOCK = 1024
TOKEN_TILE = 256
LATENT_TILE = 512
TOKEN_TILE_BATCH = 2
ADA_TILE = 768
VMEM_LIMIT = 56 * 1024 * 1024
SC_CORES, SC_SUBCORES = 2, 16
SC_WORKERS = SC_CORES * SC_SUBCORES
SC_CHUNK = 64
LANES = 128

def _params(*sem):
    return pltpu.CompilerParams(dimension_semantics=sem, vmem_limit_bytes=VMEM_LIMIT)


def _dot(a, b):
    return jnp.dot(a, b, preferred_element_type=F32)


def _dot_nt(a, b):
    return lax.dot_general(a, b, (((1,), (1,)), ((), ())), preferred_element_type=F32)


def _dot_tn(a, b):
    return lax.dot_general(a, b, (((0,), (0,)), ((), ())), preferred_element_type=F32)


def _split(a):
    hi = a.astype(BF16)
    return hi, (a - hi.astype(F32)).astype(BF16)


def _dot3(a, b, dot=_dot):
    ah, al = _split(a)
    bh, bl = _split(b)
    return dot(ah, bh) + (dot(ah, bl) + dot(al, bh))


def _pack_rows(x):
    h = x.shape[1] // 2
    lo = lax.bitcast_convert_type(x[:, 0:h].astype(BF16).astype(F32), U32)
    hi = lax.bitcast_convert_type(x[:, h:2 * h].astype(BF16).astype(F32), U32)
    return hi | (lo >> 16)


def _unpack_rows(p):
    lo = lax.bitcast_convert_type(p << 16, F32)
    hi = lax.bitcast_convert_type(p & jnp.uint32(0xFFFF0000), F32)
    return lo, hi


def _silu(x):
    return x * jax.nn.sigmoid(x)


def _norm_mod(x, g, shift, scale):
    r = lax.rsqrt(jnp.mean(x * x, axis=-1, keepdims=True) + EPS)
    return (x * r * g) * (1.0 + scale) + shift


def _const_spec(shape):
    nd = len(shape)
    return pl.BlockSpec(shape, lambda *_: (0,) * nd, pipeline_mode=pl.Buffered(1))


def _ada_body(c_ref, w_ref, b_ref, o_ref):
    o_ref[0] = _dot3(_silu(c_ref[...]), w_ref[0]) + b_ref[0]


def _ada_mod(cvec, ada_w, ada_b):
    depth, d, n6 = ada_w.shape
    rows = cvec.shape[0]
    return pl.pallas_call(
        _ada_body,
        grid=(depth, n6 // ADA_TILE),
        in_specs=[_const_spec((rows, d)),
                  pl.BlockSpec((1, d, ADA_TILE), lambda l, j: (l, 0, j)),
                  pl.BlockSpec((1, 1, ADA_TILE), lambda l, j: (l, 0, j))],
        out_specs=pl.BlockSpec((1, rows, ADA_TILE), lambda l, j: (l, 0, j)),
        out_shape=jax.ShapeDtypeStruct((depth, rows, n6), F32),
        compiler_params=_params("arbitrary", "arbitrary"),
        name="ada_mod",
    )(cvec, ada_w, ada_b.reshape(depth, 1, n6))


def _split_specs(nct, d, tb=1):
    ctx_spec = pl.BlockSpec((tb, TOKEN_TILE, d), lambda b, i: (b, jnp.minimum(i, nct - 1), 0))
    lat_spec = pl.BlockSpec((tb, TOKEN_TILE, d), lambda b, i: (b, jnp.maximum(i - nct, 0), 0))
    return ctx_spec, lat_spec


def _stream_tile(nct, ctx_ref, lat_ref, bb=0):
    return jnp.where(pl.program_id(1) < nct, ctx_ref[bb], lat_ref[bb])


def _normed_rows(nct, ctx_ref, lat_ref, mod_ref, g_ref):
    tiles = []
    for bb in range(mod_ref.shape[0]):
        m = mod_ref[bb, 0]
        tiles.append(_norm_mod(_stream_tile(nct, ctx_ref, lat_ref, bb), g_ref[...], m[0:1], m[1:2]).astype(BF16))
    return tiles[0] if len(tiles) == 1 else jnp.concatenate(tiles, axis=0)


def _store_rows(o_ref, val):
    tm = o_ref.shape[1]
    for bb in range(o_ref.shape[0]):
        o_ref[bb] = val[bb * tm:(bb + 1) * tm].astype(o_ref.dtype)


def _inproj0_body(nct, ctx_ref, lat_ref, mod_ref, g_ref, wq, wk, wv, wg, wlow, wu, wa_ref, ba_ref, tri_ref, ones_ref,
                  qd_f, ki_f, ks_f, ed_f, qd_b, ki_b, ks_b, ed_b, ov, og, ou):
    h = _normed_rows(nct, ctx_ref, lat_ref, mod_ref, g_ref)
    tm = ctx_ref.shape[1]
    low = _dot(h, wlow[...])
    q = _dot(h, wq[...]) * (GLA_DK ** -0.5)
    k = _dot(h, wk[...])
    outs = ((qd_f, ki_f, ks_f, ed_f), (qd_b, ki_b, ks_b, ed_b))
    for d, (qd_ref, ki_ref, ks_ref, ed_ref) in enumerate(outs):
        z = _dot3(low, wa_ref[d]) + ba_ref[d]
        log_a = (jnp.minimum(z, 0.0) - jnp.log1p(jnp.exp(-jnp.abs(z)))) * (1.0 / GLA_TAU)
        la_hi, la_lo = _split(log_a)
        cums, tots = [], []
        for bb in range(qd_ref.shape[0]):
            hi, lo = la_hi[bb * tm:(bb + 1) * tm], la_lo[bb * tm:(bb + 1) * tm]
            cums.append(_dot(tri_ref[d], hi) + _dot(tri_ref[d], lo))
            tots.append(_dot(ones_ref[...], hi) + _dot(ones_ref[...], lo))
        cum = cums[0] if len(cums) == 1 else jnp.concatenate(cums, axis=0)
        tot = tots[0] if len(tots) == 1 else jnp.concatenate(tots, axis=0)
        _store_rows(qd_ref, q * jnp.exp(cum))
        _store_rows(ki_ref, k * jnp.exp(-cum))
        _store_rows(ks_ref, k * jnp.exp(tot - cum))
        for bb in range(ed_ref.shape[0]):
            for ch in range(tm // GLA_CHUNK):
                row = bb * tm + ch * GLA_CHUNK
                ed_ref[bb, ch] = jnp.exp(tot[row:row + 1])
    _store_rows(ov, _dot(h, wv[...]))
    _store_rows(og, _dot(h, wg[...]))
    u = _dot(h, wu[...])
    for qb in range(ou.shape[0]):
        _store_rows(ou.at[qb], u[:, qb * LANES:(qb + 1) * LANES])


def _inproj0(ctx, x, mods, norm_g, weights, wa_pad, ba, nct):
    bsz, n_lat, d = x.shape
    t = ctx.shape[1] + n_lat
    tm, tb = TOKEN_TILE, TOKEN_TILE_BATCH
    mod_spec = pl.BlockSpec((tb, 1, 6, d), lambda b, i: (b, (i >= nct).astype(I32), 0, 0))
    ctx_spec, lat_spec = _split_specs(nct, d, tb)
    pos = jnp.arange(tm)
    same_chunk = (pos[:, None] // GLA_CHUNK) == (pos[None, :] // GLA_CHUNK)
    tri = jnp.stack([same_chunk & (pos[None, :] <= pos[:, None]),
                     same_chunk & (pos[None, :] >= pos[:, None])]).astype(BF16)
    ones = same_chunk.astype(BF16)
    consts = list(weights) + [wa_pad, ba, tri, ones]
    tok = lambda w, dt: (pl.BlockSpec((tb, tm, w), lambda b, i: (b, i, 0)), jax.ShapeDtypeStruct((bsz, t, w), dt))
    per_chunk = (pl.BlockSpec((tb, tm // GLA_CHUNK, 1, AB_QK), lambda b, i: (b, i, 0, 0)),
                 jax.ShapeDtypeStruct((bsz, t // GLA_CHUNK, 1, AB_QK), F32))
    one_dir = [tok(AB_QK, BF16)] * 3 + [per_chunk]
    u_blocks = (pl.BlockSpec((S5_CH // LANES, tb, tm, LANES), lambda b, i: (0, b, i, 0)),
                jax.ShapeDtypeStruct((S5_CH // LANES, bsz, t, LANES), F32))
    outs = one_dir + one_dir + [tok(AB_V, BF16), tok(AB_V, BF16), u_blocks]
    return pl.pallas_call(
        functools.partial(_inproj0_body, nct),
        grid=(bsz // tb, t // tm),
        in_specs=[ctx_spec, lat_spec, mod_spec, _const_spec((1, d))] + [_const_spec(a.shape) for a in consts],
        out_specs=[o[0] for o in outs],
        out_shape=[o[1] for o in outs],
        compiler_params=_params("arbitrary", "arbitrary"),
        name="inproj_gla_s5",
    )(ctx, x, mods, norm_g.reshape(1, d), *consts)


def _rope(acc, cos_ref, sin_ref, o_ref, scale):
    tm = o_ref.shape[1]
    for grp in range(acc.shape[1] // LANES):
        half = grp % 2
        cs = cos_ref[:, half * LANES:(half + 1) * LANES]
        sn = sin_ref[:, half * LANES:(half + 1) * LANES]
        for bb in range(o_ref.shape[0]):
            xg = acc[bb * tm:(bb + 1) * tm, grp * LANES:(grp + 1) * LANES]
            out = xg * cs + pltpu.roll(xg, LANES // 2, 1) * sn
            o_ref[bb, :, grp * LANES:(grp + 1) * LANES] = (out * scale).astype(o_ref.dtype)


def _moe_residual(x1_ref, g2, yk_ref, w_ref, bb=0):
    d = x1_ref.shape[2]
    half = d // 2
    y_lo, y_hi = None, None
    for k in range(TOP_K):
        lo, hi = _unpack_rows(yk_ref[k, bb])
        wk = w_ref[bb, :, k:k + 1]
        y_lo = lo * wk if y_lo is None else y_lo + lo * wk
        y_hi = hi * wk if y_hi is None else y_hi + hi * wk
    return x1_ref[bb, :, 0:half] + g2[:, 0:half] * y_lo, x1_ref[bb, :, half:d] + g2[:, half:d] * y_hi


def _inproj1_body(x1_ref, mod0_ref, yk_ref, w_ref, mod_ref, g_ref, cos_ref, sin_ref, wq, wk, wv, wg, ox, oq, ok, ov, og):
    half = x1_ref.shape[2] // 2
    for bb in range(x1_ref.shape[0]):
        x2_lo, x2_hi = _moe_residual(x1_ref, mod0_ref[bb, 0][5:6], yk_ref, w_ref, bb)
        ox[bb, :, 0:half] = x2_lo
        ox[bb, :, half:2 * half] = x2_hi
        m = mod_ref[bb, 0]
        h = _norm_mod(jnp.concatenate([x2_lo, x2_hi], axis=1), g_ref[...], m[0:1], m[1:2]).astype(BF16)
        _rope(_dot(h, wq[...]), cos_ref, sin_ref, oq.at[bb:bb + 1], 1.0)
        _rope(_dot(h, wk[...]), cos_ref, sin_ref, ok.at[bb:bb + 1], RET_DK ** -0.5)
        ov[bb] = _dot(h, wv[...]).astype(ov.dtype)
        og[bb] = _dot(h, wg[...]).astype(og.dtype)


def _inproj1(x1, mods0, yk, w_tok, mods, norm_g, cos_t, sin_t, weights, nct):
    bsz, t, d = x1.shape
    tm, tb = TOKEN_TILE, TOKEN_TILE_BATCH
    mod_spec = pl.BlockSpec((tb, 1, 6, d), lambda b, i: (b, (i >= nct).astype(I32), 0, 0))
    tok = lambda w: pl.BlockSpec((tb, tm, w), lambda b, i: (b, i, 0))
    lat = lambda w: pl.BlockSpec((tb, tm, w), lambda b, i: (b, jnp.maximum(i - nct, 0), 0))
    tab_spec = pl.BlockSpec((tm, RET_DK), lambda b, i: (i, 0))
    wq, wk, wv, wg = weights
    n_lat = t - nct * tm
    return pl.pallas_call(
        _inproj1_body,
        grid=(bsz // tb, t // tm),
        in_specs=[tok(d), mod_spec, pl.BlockSpec((TOP_K, tb, tm, d // 2), lambda b, i: (0, b, i, 0)), tok(TOP_K),
                  mod_spec, _const_spec((1, d)), tab_spec, tab_spec] + [_const_spec(w.shape) for w in weights],
        out_specs=[lat(d), tok(wq.shape[1]), tok(wk.shape[1]), tok(wv.shape[1]), lat(wg.shape[1])],
        out_shape=[jax.ShapeDtypeStruct((bsz, n_lat, d), F32), jax.ShapeDtypeStruct((bsz, t, wq.shape[1]), BF16),
                   jax.ShapeDtypeStruct((bsz, t, wk.shape[1]), BF16), jax.ShapeDtypeStruct((bsz, t, wv.shape[1]), BF16),
                   jax.ShapeDtypeStruct((bsz, n_lat, wg.shape[1]), BF16)],
        compiler_params=_params("arbitrary", "arbitrary"),
        name="inproj_retention",
    )(x1, mods0, yk.reshape(TOP_K, bsz, t, d // 2), w_tok.reshape(bsz, t, TOP_K), mods, norm_g.reshape(1, d),
      cos_t, sin_t, *weights)


def _backward_chunk(n, n_ctx_chunks, n_chunks):
    return jnp.where(n < n_ctx_chunks, n_ctx_chunks - 1 - n, n_chunks - 1 - (n - n_ctx_chunks))


def _gla_body(qd_f, ki_f, ks_f, ed_f, v_f, qd_b, ki_b, ks_b, ed_b, v_b, hmask_ref, bdmask_ref, o_f, o_b, st_f, st_b):
    c = GLA_CHUNK

    @pl.when(pl.program_id(1) == 0)
    def _():
        st_f[...] = jnp.zeros_like(st_f)
        st_b[...] = jnp.zeros_like(st_b)

    r4 = lax.broadcasted_iota(I32, (GLA_HEADS * c, c), 0) & (c - 1)
    c4 = lax.broadcasted_iota(I32, (GLA_HEADS * c, c), 1)
    dirs = ((qd_f, ki_f, ks_f, ed_f, v_f, o_f, st_f), (qd_b, ki_b, ks_b, ed_b, v_b, o_b, st_b))
    chains = [(bb, d) + dirs[d] for bb in range(qd_f.shape[0]) for d in range(2)]
    scores, inter, grow = [], [], []
    for bb, d, qd_ref, ki_ref, ks_ref, ed_ref, v_ref, o_ref, st_ref in chains:
        q_dec = qd_ref[bb]
        q_heads = jnp.concatenate([q_dec] * GLA_HEADS, axis=0) * hmask_ref[...]
        seen4 = (c4 <= r4) if d == 0 else (c4 >= r4)
        scores.append(jnp.where(seen4, _dot_nt(q_heads, ki_ref[bb]), 0.0).astype(BF16))
        inter.append(_dot_nt(q_dec, st_ref[bb].astype(BF16)))
        grow.append(_dot_tn(v_ref[bb], ks_ref[bb]))
    for (bb, d, qd_ref, ki_ref, ks_ref, ed_ref, v_ref, o_ref, st_ref), sc, o_inter, dst in zip(chains, scores, inter, grow):
        v = v_ref[bb]
        o_intra = jnp.concatenate(
            [_dot(sc[h * c:(h + 1) * c], v[:, h * GLA_DV:(h + 1) * GLA_DV]) for h in range(GLA_HEADS)], axis=1)
        o_ref[bb] = o_intra + o_inter
        st_ref[bb] = st_ref[bb] * ed_ref[bb, 0] + bdmask_ref[...] * dst


def _gla(per_dir, v, n_ctx):
    bsz, t, _ = v.shape
    nc, ncc = t // GLA_CHUNK, n_ctx // GLA_CHUNK
    gb = GLA_BATCH
    fwd = lambda b, n: (b, n, 0)
    bwd = lambda b, n: (b, _backward_chunk(n, ncc, nc), 0)
    hmask = (jnp.arange(AB_QK)[:, None] // GLA_CHUNK == jnp.arange(AB_QK)[None, :] // GLA_DK).astype(BF16)
    bdmask = (jnp.arange(AB_V)[:, None] // GLA_DV == jnp.arange(AB_QK)[None, :] // GLA_DK).astype(F32)

    def specs(idx):
        idx4 = lambda b, n: idx(b, n) + (0,)
        return [pl.BlockSpec((gb, GLA_CHUNK, AB_QK), idx)] * 3 + [pl.BlockSpec((gb, 1, 1, AB_QK), idx4),
                                                                  pl.BlockSpec((gb, GLA_CHUNK, AB_V), idx)]

    return pl.pallas_call(
        _gla_body,
        grid=(bsz // gb, nc),
        in_specs=specs(fwd) + specs(bwd) + [_const_spec(hmask.shape), _const_spec(bdmask.shape)],
        out_specs=[pl.BlockSpec((gb, GLA_CHUNK, AB_V), fwd), pl.BlockSpec((gb, GLA_CHUNK, AB_V), bwd)],
        out_shape=[jax.ShapeDtypeStruct((bsz, t, AB_V), F32)] * 2,
        scratch_shapes=[pltpu.VMEM((gb, AB_V, AB_QK), F32)] * 2,
        compiler_params=_params("arbitrary", "arbitrary"),
        name="gla_scan",
    )(*per_dir[0], v, *per_dir[1], v, hmask, bdmask)


def _cmul(x, y):
    return x[0] * y[0] - x[1] * y[1], x[0] * y[1] + x[1] * y[0]


def _s5_operators(lam_re, lam_im, log_step, b_re, b_im, c_re, c_im):
    ln = S5_CHUNK
    step = jnp.exp(log_step.astype(F32))[..., None]
    lam_re, lam_im = lam_re.astype(F32), lam_im.astype(F32)
    mag = jnp.exp(lam_re * step)
    a = (mag * jnp.cos(lam_im * step), mag * jnp.sin(lam_im * step))
    den = lam_re * lam_re + lam_im * lam_im
    f_re = ((a[0] - 1.0) * lam_re + a[1] * lam_im) / den
    f_im = (a[1] * lam_re - (a[0] - 1.0) * lam_im) / den
    bt_re, bt_im = b_re.transpose(0, 2, 1), b_im.transpose(0, 2, 1)
    bb = _cmul((f_re[:, :, None, :], f_im[:, :, None, :]), (bt_re, bt_im))
    bbt = jnp.concatenate([bb[0], -bb[1]], axis=-1)
    pw = (a[0][:, :, None, :], a[1][:, :, None, :])
    while pw[0].shape[2] < ln:
        top = (pw[0][:, :, -1:, :], pw[1][:, :, -1:, :])
        nxt = _cmul(top, pw)
        pw = (jnp.concatenate([pw[0], nxt[0]], axis=2), jnp.concatenate([pw[1], nxt[1]], axis=2))
    pw = (jnp.concatenate([jnp.ones_like(pw[0][:, :, :1]), pw[0]], axis=2),
          jnp.concatenate([jnp.zeros_like(pw[1][:, :, :1]), pw[1]], axis=2))
    ca = _cmul((c_re[:, :, None], c_im[:, :, None]), (pw[0][:, :, :, None, :], pw[1][:, :, :, None, :]))
    by_dir = lambda arr, lo, flip_d: jnp.stack([jnp.flip(arr[d, :, lo:lo + ln], axis=1) if d == flip_d
                                                else arr[d, :, lo:lo + ln] for d in range(2)])
    rows = lambda arr: arr.reshape(2, S5_GROUPS, ln * S5_GROUP, 2 * S5_P)
    cab = rows(by_dir(jnp.concatenate([ca[0], ca[1]], axis=-1), 0, 1))
    cab2 = rows(by_dir(jnp.concatenate([ca[0], -ca[1]], axis=-1), 1, 1)).astype(BF16)
    pwx = by_dir(jnp.concatenate([pw[0], pw[1]], axis=-1), 0, 0)
    lr, li = pw[0][:, :, ln], pw[1][:, :, ln]
    ac_rows = [jnp.concatenate([lr, lr], -1), jnp.concatenate([-li, li], -1), jnp.concatenate([li, -li], -1)]
    ac = jnp.stack(ac_rows + [jnp.zeros_like(ac_rows[0])] * 5, axis=2)
    return cab, cab2, bbt, pwx, ac


def _s5_group_operators(gg, cab_ref, bbt_ref, pwx_ref, tz, wx):
    ln, ch, p = S5_CHUNK, S5_GROUP, S5_P
    lane = lax.broadcasted_iota(I32, (ch, ln * ch), 1)
    for d in range(2):
        kern = _dot3(bbt_ref[d, gg], cab_ref[d, gg], dot=_dot_nt)
        bt = bbt_ref[d, gg]
        b_re, b_im = bt[:, 0:p], -bt[:, p:2 * p]
        for j in range(ln):
            if d == 0:
                blk = jnp.where(lane >= j * ch, kern if j == 0 else pltpu.roll(kern, j * ch, 1), 0.0)
            else:
                blk = jnp.where(lane < (j + 1) * ch, kern if j == ln - 1 else pltpu.roll(kern, (j + 1) * ch, 1), 0.0)
            tz[gg, d, j * ch:(j + 1) * ch, :] = blk.astype(BF16)
            pr, pi = pwx_ref[d, gg, j:j + 1, 0:p], pwx_ref[d, gg, j:j + 1, p:2 * p]
            x_re, x_im = pr * b_re - pi * b_im, pr * b_im + pi * b_re
            wx[gg, d, j * ch:(j + 1) * ch, :] = jnp.concatenate([x_re, x_im, x_im, x_re], axis=1).astype(BF16)


def _s5_placement(pall):
    rows, cols = pall.shape[1], pall.shape[2]
    row = lax.broadcasted_iota(I32, (rows, cols), 0)
    col = lax.broadcasted_iota(I32, (rows, cols), 1)
    same_token = (row // LANES) == (col // S5_GROUP)
    for g8 in range(pall.shape[0]):
        pall[g8] = jnp.where(same_token & ((row % LANES) == g8 * S5_GROUP + (col % S5_GROUP)), 1.0, 0.0).astype(BF16)


def _first_step():
    return (pl.program_id(0) == 0) & (pl.program_id(1) == 0)


def _s5_fold_body(ncs, u_ref, o_ref, pall, ucat):
    @pl.when(_first_step())
    def _():
        _s5_placement(pall)

    for b in range(u_ref.shape[1]):
        for j in range(S5_CHUNK):
            ucat[b * ncs:(b + 1) * ncs, j * LANES:(j + 1) * LANES] = u_ref[0, b, pl.ds(j, ncs, stride=S5_CHUNK), :].astype(BF16)
    for g8 in range(pall.shape[0]):
        o_ref[g8] = _dot(ucat[...], pall[g8]).astype(BF16)


def _s5_unfold_body(ncs, y_ref, o_ref, pall):
    @pl.when(_first_step())
    def _():
        _s5_placement(pall)

    def token_pair(i2, carry):
        r0 = pl.multiple_of(i2 * 2 * LANES, 2 * LANES)
        acc = _dot_nt(y_ref[0], pall[0, pl.ds(r0, 2 * LANES), :])
        for g8 in range(1, pall.shape[0]):
            acc = acc + _dot_nt(y_ref[g8], pall[g8, pl.ds(r0, 2 * LANES), :])
        for b in range(o_ref.shape[1]):
            for par in range(2):
                o_ref[0, b, pl.ds(2 * i2 + par, ncs, stride=S5_CHUNK), :] = (
                    acc[b * ncs:(b + 1) * ncs, par * LANES:(par + 1) * LANES])
        return carry

    lax.fori_loop(0, S5_CHUNK // 2, token_pair, 0)


def _s5_body(ncs_ctx, ncs, rows, u_ref, cab_ref, cab2_ref, bbt_ref, pwx_ref, ac_ref, y_ref, tz, wx, *vecs):
    half = 2 * S5_P
    n_groups = u_ref.shape[0]
    groups = [vecs[6 * gg:6 * gg + 6] for gg in range(n_groups)]
    for gg, (xx_f, xs_f, xx_b, xs_b, _, _) in enumerate(groups):
        _s5_group_operators(gg, cab_ref, bbt_ref, pwx_ref, tz, wx)
        for d, (xx, xs) in enumerate(((xx_f, xs_f), (xx_b, xs_b))):
            r = _dot(u_ref[gg], wx[gg, d])
            xx[...] = r[:, :half]
            xs[...] = r[:, half:]

    def advance(ac, s, s_sw, x, x_sw):
        return ac[0:1] * s + ac[1:2] * s_sw + x, ac[0:1] * s_sw + ac[2:3] * s + x_sw

    def step(n, carry):
        at_f = pl.ds(n, rows, stride=ncs)
        at_b = pl.ds(_backward_chunk(n, ncs_ctx, ncs), rows, stride=ncs)
        out = []
        for gg, (xx_f, xs_f, xx_b, xs_b, sin_f, sin_b) in enumerate(groups):
            s_f, sw_f, s_b, sw_b = carry[4 * gg:4 * gg + 4]
            sin_f[at_f, :] = s_f
            sin_b[at_b, :] = s_b
            out += advance(ac_ref[0, gg], s_f, sw_f, xx_f[at_f, :], xs_f[at_f, :])
            out += advance(ac_ref[1, gg], s_b, sw_b, xx_b[at_b, :], xs_b[at_b, :])
        return tuple(out)

    zero = jnp.zeros((rows, half), F32)
    lax.fori_loop(0, ncs, step, (zero,) * (4 * n_groups))
    for gg, (_, _, _, _, sin_f, sin_b) in enumerate(groups):
        u = u_ref[gg]
        y_ref[gg] = (_dot(u, tz[gg, 0]) + _dot(u, tz[gg, 1]) + _dot_nt(sin_f[...].astype(BF16), cab2_ref[0, gg])
                     + _dot_nt(sin_b[...].astype(BF16), cab2_ref[1, gg])).astype(BF16)


def _s5(u4, ops, n_ctx):
    nq, bsz, t, _ = u4.shape
    ln, lanes = S5_CHUNK, S5_CHUNK * S5_GROUP
    gq = S5_GROUPS // nq
    ncs, ncs_ctx = t // ln, n_ctx // ln
    m = ncs * bsz
    hb = S5_FOLD_BATCH
    tok_spec = pl.BlockSpec((1, hb, t, LANES), lambda q, h: (q, h, 0, 0))
    grp_spec = pl.BlockSpec((gq, hb * ncs, lanes), lambda q, h: (q, h, 0))
    pall = pltpu.VMEM((gq, ln * LANES, lanes), BF16)
    ug = pl.pallas_call(
        functools.partial(_s5_fold_body, ncs),
        grid=(nq, bsz // hb),
        in_specs=[tok_spec],
        out_specs=grp_spec,
        out_shape=jax.ShapeDtypeStruct((S5_GROUPS, m, lanes), BF16),
        scratch_shapes=[pall, pltpu.VMEM((hb * ncs, ln * LANES), BF16)],
        compiler_params=_params("arbitrary", "arbitrary"),
        name="s5_fold",
    )(u4)
    sg = S5_SCAN_GROUPS
    dir_spec = lambda arr: pl.BlockSpec((2, sg) + arr.shape[2:], lambda g: (0, g, 0, 0))
    yg = pl.pallas_call(
        functools.partial(_s5_body, ncs_ctx, ncs, bsz),
        grid=(S5_GROUPS // sg,),
        in_specs=[pl.BlockSpec((sg, m, lanes), lambda g: (g, 0, 0))] + [dir_spec(arr) for arr in ops],
        out_specs=pl.BlockSpec((sg, m, lanes), lambda g: (g, 0, 0)),
        out_shape=jax.ShapeDtypeStruct((S5_GROUPS, m, lanes), BF16),
        scratch_shapes=[pltpu.VMEM((sg, 2, lanes, lanes), BF16)] * 2 + [pltpu.VMEM((m, 2 * S5_P), F32)] * (6 * sg),
        compiler_params=_params("arbitrary"),
        name="s5_scan",
    )(ug, *ops)
    return pl.pallas_call(
        functools.partial(_s5_unfold_body, ncs),
        grid=(nq, bsz // hb),
        in_specs=[grp_spec],
        out_specs=tok_spec,
        out_shape=jax.ShapeDtypeStruct(u4.shape, F32),
        scratch_shapes=[pall],
        compiler_params=_params("arbitrary", "arbitrary"),
        name="s5_unfold",
    )(yg)


def _ret_body(q_f, k_f, v_f, q_b, k_b, v_b, dmat_ref, rsc_ref, csc_ref, gam_ref, o_f, o_b, st_f, st_b):
    @pl.when(pl.program_id(1) == 0)
    def _():
        st_f[...] = jnp.zeros_like(st_f)
        st_b[...] = jnp.zeros_like(st_b)

    dirs = ((q_f, k_f, v_f, o_f, st_f), (q_b, k_b, v_b, o_b, st_b))
    for bb in range(q_f.shape[0]):
        for d, (q_ref, k_ref, v_ref, o_ref, st_ref) in enumerate(dirs):
            for h in range(RET_HEADS):
                qh = q_ref[bb, :, h * RET_DK:(h + 1) * RET_DK]
                kh = k_ref[bb, :, h * RET_DK:(h + 1) * RET_DK]
                vh = v_ref[bb, :, h * RET_DV:(h + 1) * RET_DV]
                st = st_ref[bb, h]
                scores = (_dot_nt(qh, kh) * dmat_ref[d, h]).astype(BF16)
                o = _dot(scores, vh) + rsc_ref[d, h] * _dot(qh, st.astype(BF16))
                o_ref[bb, :, h * RET_DV:(h + 1) * RET_DV] = o.astype(o_ref.dtype)
                k_state = (kh.astype(F32) * csc_ref[d, h]).astype(BF16)
                st_ref[bb, h] = st * gam_ref[d, h] + _dot_tn(k_state, vh)


def _retention(q, k, v, decay_logit, n_ctx):
    bsz, t, _ = q.shape
    c = RET_CHUNK
    nc, ncc = t // c, n_ctx // c
    nl = nc - ncc
    rb = RET_BATCH
    log_gamma = jax.nn.log_sigmoid(decay_logit.astype(F32))[:, :, None, None]
    i = jnp.arange(c, dtype=F32)
    lag = i[:, None] - i[None, :]
    lag = jnp.stack([lag, -lag])[:, None]
    dmat = jnp.where(lag >= 0, jnp.exp(log_gamma * jnp.maximum(lag, 0.0)), 0.0)
    done = jnp.stack([i + 1.0, c - i])[:, None, :, None]
    rsc = jnp.exp(log_gamma * done)
    csc = jnp.exp(log_gamma * (c - done))
    gam = jnp.exp(log_gamma[:, :, 0, 0] * c)
    fwd = lambda b, n: (b, n, 0)
    bwd = lambda b, n: (b, _backward_chunk(n, ncc, nc), 0)
    o_fwd = lambda b, n: (b, jnp.maximum(n - ncc, 0), 0)
    o_bwd = lambda b, n: (b, nl - 1 - jnp.maximum(n - ncc, 0), 0)

    def specs(idx):
        return [pl.BlockSpec((rb, c, RET_QK), idx), pl.BlockSpec((rb, c, RET_QK), idx), pl.BlockSpec((rb, c, RET_MIX), idx)]

    return pl.pallas_call(
        _ret_body,
        grid=(bsz // rb, nc),
        in_specs=specs(fwd) + specs(bwd) + [_const_spec(dmat.shape), _const_spec(rsc.shape), _const_spec(csc.shape),
                                            pl.BlockSpec(memory_space=pltpu.SMEM)],
        out_specs=[pl.BlockSpec((rb, c, RET_MIX), o_fwd), pl.BlockSpec((rb, c, RET_MIX), o_bwd)],
        out_shape=[jax.ShapeDtypeStruct((bsz, nl * c, RET_MIX), BF16)] * 2,
        scratch_shapes=[pltpu.VMEM((rb, RET_HEADS, RET_DK, RET_DV), F32)] * 2,
        compiler_params=_params("arbitrary", "arbitrary"),
        name="retention_scan",
    )(q, k, v, q, k, v, dmat, rsc, csc, gam)


def _zero_counts_at_start(cnt_ref):
    @pl.when(_first_step())
    def _():
        cnt_ref[...] = jnp.zeros_like(cnt_ref)


def _route(xs, mixed, mods, n2g_ref, wr_ref, br_ref, x1_ref, h2_ref, e_ref, w_ref, r_ref, cnt_ref):
    rows = xs[0].shape[0]
    h2s = []
    for bb, (x, mod) in enumerate(zip(xs, mods)):
        x1 = x + mod[2:3] * mixed[bb * rows:(bb + 1) * rows]
        x1_ref[bb] = x1
        h2s.append(_norm_mod(x1, n2g_ref[...], mod[3:4], mod[4:5]))
        h2_ref[bb] = _pack_rows(h2s[bb])
    h2 = h2s[0] if len(h2s) == 1 else jnp.concatenate(h2s, axis=0)
    tm = h2.shape[0]
    logits = _dot3(wr_ref[...], h2, dot=_dot_nt) + br_ref[...]
    ie = lax.broadcasted_iota(I32, logits.shape, 0)
    tops, picks = [], []
    for _ in range(TOP_K):
        mx = jnp.max(logits, axis=0, keepdims=True)
        pick = jnp.min(jnp.where(logits == mx, ie, N_EXPERTS), axis=0, keepdims=True)
        tops.append(mx)
        picks.append(pick)
        logits = jnp.where(ie == pick, -jnp.inf, logits)
    ex = [jnp.exp(tk - tops[0]) for tk in tops]
    den = ex[0] + ex[1] + ex[2] + ex[3]
    for kk in range(TOP_K):
        w_ref[0, kk:kk + 1, :] = ex[kk] / den
        e_ref[0, kk:kk + 1, :] = picks[kk]

    earlier = (lax.broadcasted_iota(I32, (tm, tm), 0) < lax.broadcasted_iota(I32, (tm, tm), 1))
    earlier = jnp.where(earlier, 1.0, 0.0).astype(BF16)
    run = cnt_ref[:, 0:1]
    for kk, pick in enumerate(picks):
        onehot = jnp.where(ie == pick, 1.0, 0.0)
        before = _dot(onehot.astype(BF16), earlier) + run
        r_ref[0, kk:kk + 1, :] = jnp.sum(onehot * before, axis=0, keepdims=True).astype(I32)
        run = run + jnp.sum(onehot, axis=1, keepdims=True)
    cnt_ref[...] = jnp.broadcast_to(run, cnt_ref.shape)


def _mix0_body(nct, ctx_ref, lat_ref, mod_ref, of_ref, ob_ref, g_ref, ys_ref, u_ref, gng_ref, dsk_ref, gluw_ref,
               glub_ref, wo_ref, n2g_ref, wr_ref, br_ref, x1_ref, h2_ref, e_ref, w_ref, r_ref, cnt_ref):
    _zero_counts_at_start(cnt_ref)
    tb, tm = of_ref.shape[0], of_ref.shape[1]
    rows = lambda ref: ref[...].reshape(tb * tm, ref.shape[-1])
    o = rows(of_ref) + rows(ob_ref)
    heads = []
    for h in range(GLA_HEADS):
        oh = o[:, h * GLA_DV:(h + 1) * GLA_DV]
        heads.append(oh * lax.rsqrt(jnp.mean(oh * oh, axis=-1, keepdims=True) + EPS))
    gla = jnp.concatenate(heads, axis=1) * gng_ref[...] * _silu(rows(g_ref).astype(F32))
    lane_blocks = lambda ref: jnp.concatenate([ref[qb].reshape(tb * tm, LANES) for qb in range(ref.shape[0])], axis=1)
    y = jax.nn.gelu(lane_blocks(ys_ref) + dsk_ref[...] * lane_blocks(u_ref))
    y = y * jax.nn.sigmoid(_dot(y.astype(BF16), gluw_ref[...]) + glub_ref[...])
    mixed = _dot(gla.astype(BF16), wo_ref[0:AB_V]) + _dot(y.astype(BF16), wo_ref[AB_V:AB_V + S5_CH])
    _route([_stream_tile(nct, ctx_ref, lat_ref, bb) for bb in range(tb)], mixed, [mod_ref[bb, 0] for bb in range(tb)],
           n2g_ref, wr_ref, br_ref, x1_ref, h2_ref, e_ref, w_ref, r_ref, cnt_ref)


def _mix1_body(x_ref, mod_ref, of_ref, ob_ref, g_ref, ng_ref, wo_ref, n2g_ref, wr_ref, br_ref,
               x1_ref, h2_ref, e_ref, w_ref, r_ref, cnt_ref):
    _zero_counts_at_start(cnt_ref)
    mixed = None
    for h in range(RET_HEADS):
        sl = slice(h * RET_DV, (h + 1) * RET_DV)
        oh = of_ref[0, :, sl].astype(F32) + ob_ref[0, :, sl].astype(F32)
        mu = jnp.mean(oh, axis=-1, keepdims=True)
        cen = oh - mu
        var = jnp.mean(cen * cen, axis=-1, keepdims=True)
        gated = cen * lax.rsqrt(var + EPS) * ng_ref[:, sl] * _silu(g_ref[0, :, sl].astype(F32))
        part = _dot(gated.astype(BF16), wo_ref[sl])
        mixed = part if mixed is None else mixed + part
    _route([x_ref[0]], mixed, [mod_ref[0, 0]], n2g_ref, wr_ref, br_ref, x1_ref, h2_ref, e_ref, w_ref, r_ref, cnt_ref)


def _mix_call(body, name, stream, mods, tiles, acts, consts, norm2_g, w_router, b_router, n_tok, seg_tile0,
              tm=TOKEN_TILE, tb=1):
    bsz, _, d = stream[-1].shape
    off = lambda b, i: (b, i + seg_tile0, 0)
    loc = lambda b, i: (b, i, 0)
    ntl = bsz // tb * tiles
    flat = lambda b, i: (b * tiles + i, 0, 0)
    in_specs = list(_split_specs(n_tok, d, tb)) if len(stream) == 2 else [pl.BlockSpec((tb, tm, d), off)]
    in_specs.append(pl.BlockSpec((tb, 1, 6, d), lambda b, i: (b, ((i + seg_tile0) >= n_tok).astype(I32), 0, 0)))
    args = list(stream) + [mods]
    for arr, offset in acts:
        if arr.ndim == 4:
            in_specs.append(pl.BlockSpec((arr.shape[0], tb, tm, arr.shape[3]), lambda b, i: (0, b, i, 0)))
        else:
            in_specs.append(pl.BlockSpec((tb, tm, arr.shape[2]), off if offset else loc))
        args.append(arr)
    tail = list(consts) + [norm2_g.reshape(1, d), w_router.T, b_router.reshape(N_EXPERTS, 1)]
    in_specs += [_const_spec(a.shape) for a in tail]
    args += tail
    tok_out = pl.BlockSpec((1, TOP_K, tb * tm), flat)
    return pl.pallas_call(
        body,
        grid=(bsz // tb, tiles),
        in_specs=in_specs,
        out_specs=[pl.BlockSpec((tb, tm, d), loc), pl.BlockSpec((tb, tm, d // 2), loc), tok_out, tok_out, tok_out,
                   _const_spec((N_EXPERTS, LANES))],
        out_shape=[jax.ShapeDtypeStruct((bsz, tiles * tm, d), F32), jax.ShapeDtypeStruct((bsz, tiles * tm, d // 2), U32),
                   jax.ShapeDtypeStruct((ntl, TOP_K, tb * tm), I32), jax.ShapeDtypeStruct((ntl, TOP_K, tb * tm), F32),
                   jax.ShapeDtypeStruct((ntl, TOP_K, tb * tm), I32), jax.ShapeDtypeStruct((N_EXPERTS, LANES), F32)],
        compiler_params=_params("arbitrary", "arbitrary"),
        name=name,
    )(*args)


def _cast_rows(src_ref, dst_ref, rows):
    def chunk(j, carry):
        r = pl.multiple_of(j * rows, rows)
        dst_ref[pl.ds(r, rows), :] = src_ref[0, 0, pl.ds(r, rows), :].astype(BF16)
        return carry

    lax.fori_loop(0, dst_ref.shape[0] // rows, chunk, 0)


def _expert_body(be_ref, nu_ref, x_ref, wgu_ref, bgu_ref, wd_ref, bd_ref, o_ref, wgu_bf, wd_bf):
    i = pl.program_id(0)
    live = i < nu_ref[0]
    new_expert = (i == 0) | (be_ref[i] != be_ref[jnp.maximum(i - 1, 0)])

    @pl.when(live & new_expert)
    def _():
        _cast_rows(wgu_ref, wgu_bf, 128)
        _cast_rows(wd_ref, wd_bf, 128)

    @pl.when(live)
    def _():
        x_lo, x_hi = _unpack_rows(x_ref[...])
        half = x_lo.shape[1]
        gu = (_dot(x_lo.astype(BF16), wgu_bf[0:half]) + _dot(x_hi.astype(BF16), wgu_bf[half:2 * half])
              + bgu_ref[0, 0])
        gate = jnp.minimum(gu[:, :D_FF], SWIGLU_LIMIT)
        lin = jnp.clip(gu[:, D_FF:], -SWIGLU_LIMIT, SWIGLU_LIMIT)
        act = gate * jax.nn.sigmoid(SWIGLU_ALPHA * gate) * (lin + 1.0)
        y = _dot(act.astype(BF16), wd_bf[...]) + bd_ref[0, 0]
        o_ref[...] = _pack_rows(y)

    @pl.when(i >= nu_ref[0])
    def _():
        o_ref[...] = jnp.zeros_like(o_ref)


def _experts(xb, block_e, n_used, layer, w_gu, b_gu, w_down, b_down):
    n_slots, half = xb.shape
    d = 2 * half
    n_blocks = n_slots // MOE_BLOCK
    depth = w_gu.shape[0]
    by_expert = lambda i, be, nu: (layer, be[i], 0, 0)
    return pl.pallas_call(
        _expert_body,
        grid_spec=pltpu.PrefetchScalarGridSpec(
            num_scalar_prefetch=2,
            grid=(n_blocks,),
            in_specs=[pl.BlockSpec((MOE_BLOCK, half), lambda i, be, nu: (i, 0)),
                      pl.BlockSpec((1, 1, d, 2 * D_FF), by_expert), pl.BlockSpec((1, 1, 1, 2 * D_FF), by_expert),
                      pl.BlockSpec((1, 1, D_FF, d), by_expert), pl.BlockSpec((1, 1, 1, d), by_expert)],
            out_specs=pl.BlockSpec((MOE_BLOCK, half), lambda i, be, nu: (i, 0)),
            scratch_shapes=[pltpu.VMEM((d, 2 * D_FF), BF16), pltpu.VMEM((D_FF, d), BF16)]),
        out_shape=jax.ShapeDtypeStruct((n_slots, half), U32),
        compiler_params=_params("arbitrary"),
        name="moe_experts",
    )(block_e, n_used, xb, w_gu, b_gu.reshape(depth, N_EXPERTS, 1, 2 * D_FF), w_down,
      b_down.reshape(depth, N_EXPERTS, 1, d))


def _combine_body(x1_ref, mod_ref, yk_ref, w_ref, fg_ref, o_ref):
    d = x1_ref.shape[2]
    half = d // 2
    x2_lo, x2_hi = _moe_residual(x1_ref, mod_ref[0, 0][5:6], yk_ref, w_ref)
    ms = (jnp.sum(x2_lo * x2_lo, axis=-1, keepdims=True) + jnp.sum(x2_hi * x2_hi, axis=-1, keepdims=True)) / d
    r = lax.rsqrt(ms + EPS)
    o_ref[0, :, 0:half] = x2_lo * r * fg_ref[:, 0:half]
    o_ref[0, :, half:d] = x2_hi * r * fg_ref[:, half:d]


def _combine(x1, mods, yk, w_tok, seg_tile0, n_tok, final_g, tm):
    bsz, t, d = x1.shape
    loc = lambda b, i: (b, i, 0)
    return pl.pallas_call(
        _combine_body,
        grid=(bsz, t // tm),
        in_specs=[pl.BlockSpec((1, tm, d), loc),
                  pl.BlockSpec((1, 1, 6, d), lambda b, i: (b, ((i + seg_tile0) >= n_tok).astype(I32), 0, 0)),
                  pl.BlockSpec((TOP_K, 1, tm, d // 2), lambda b, i: (0, b, i, 0)),
                  pl.BlockSpec((1, tm, TOP_K), loc), _const_spec((1, d))],
        out_specs=pl.BlockSpec((1, tm, d), loc),
        out_shape=jax.ShapeDtypeStruct((bsz, t, d), F32),
        compiler_params=_params("arbitrary", "arbitrary"),
        name="moe_combine",
    )(x1, mods, yk.reshape(TOP_K, bsz, t, d // 2), w_tok.reshape(bsz, t, TOP_K), final_g.reshape(1, d))


def _sc_mesh():
    return plsc.VectorSubcoreMesh(core_axis_name="core", subcore_axis_name="subcore",
                                  num_cores=SC_CORES, num_subcores=SC_SUBCORES)


def _sc_worker_base(per_worker):
    return (lax.axis_index("subcore") * SC_CORES + lax.axis_index("core")) * per_worker


def _sc_dispatch(rows, dest, n_slots):
    n, w = rows.shape
    per_worker = n // SC_WORKERS
    assert per_worker * SC_WORKERS == n and per_worker % SC_CHUNK == 0

    @functools.partial(
        pl.kernel, mesh=_sc_mesh(), out_type=jax.ShapeDtypeStruct((n_slots, w), rows.dtype),
        scratch_types=[pltpu.VMEM((SC_CHUNK,), I32)] * TOP_K + [pltpu.VMEM((SC_CHUNK, w), rows.dtype),
                                                                pltpu.SemaphoreType.DMA],
        name="moe_dispatch")
    def scatter_rows(rows_hbm, dest_hbm, out_hbm, *scratch):
        idx_refs, buf, sem = scratch[:TOP_K], scratch[TOP_K], scratch[TOP_K + 1]
        base0 = _sc_worker_base(per_worker)

        @pl.loop(0, per_worker // SC_CHUNK)
        def _(j):
            base = base0 + j * SC_CHUNK
            pltpu.sync_copy(rows_hbm.at[pl.ds(base, SC_CHUNK)], buf)
            for k, idx in enumerate(idx_refs):
                pltpu.sync_copy(dest_hbm.at[pl.ds(k * n + base, SC_CHUNK)], idx)
            copies = [pltpu.make_async_copy(buf, out_hbm.at[idx], sem) for idx in idx_refs]
            for cp in copies:
                cp.start()
            for cp in copies:
                cp.wait()

    return scatter_rows(rows, dest)


def _sc_gather(table, idx):
    n = idx.shape[0]
    w = table.shape[1]
    per_worker = n // SC_WORKERS
    n_chunks = per_worker // SC_CHUNK
    assert per_worker * SC_WORKERS == n and n_chunks * SC_CHUNK == per_worker and n_chunks % 2 == 0

    @functools.partial(
        pl.kernel, mesh=_sc_mesh(), out_type=jax.ShapeDtypeStruct((n, w), table.dtype),
        scratch_types=([pltpu.VMEM((SC_CHUNK,), I32)] * 2 + [pltpu.VMEM((SC_CHUNK, w), table.dtype)] * 2
                       + [pltpu.SemaphoreType.DMA] * 4),
        name="moe_gather")
    def gather_rows(table_hbm, idx_hbm, out_hbm, idx0, idx1, buf0, buf1, gsem0, gsem1, wsem0, wsem1):
        base0 = _sc_worker_base(per_worker)

        def gather_copy(idx_v, buf, sem):
            return pltpu.make_async_copy(table_hbm.at[idx_v], buf, sem)

        def write_copy(j, buf, sem):
            return pltpu.make_async_copy(buf, out_hbm.at[pl.ds(base0 + j * SC_CHUNK, SC_CHUNK)], sem)

        def start_gather(j, idx_v, buf, sem):
            pltpu.sync_copy(idx_hbm.at[pl.ds(base0 + j * SC_CHUNK, SC_CHUNK)], idx_v)
            gather_copy(idx_v, buf, sem).start()

        start_gather(0, idx0, buf0, gsem0)

        @pl.loop(0, n_chunks, step=2)
        def _(j):
            @pl.when(j > 0)
            def _():
                write_copy(j - 1, buf1, wsem1).wait()
            start_gather(j + 1, idx1, buf1, gsem1)
            gather_copy(idx0, buf0, gsem0).wait()
            write_copy(j, buf0, wsem0).start()

            @pl.when(j + 2 < n_chunks)
            def _():
                write_copy(j, buf0, wsem0).wait()
                start_gather(j + 2, idx0, buf0, gsem0)
            gather_copy(idx1, buf1, gsem1).wait()
            write_copy(j + 1, buf1, wsem1).start()

        write_copy(n_chunks - 2, buf0, wsem0).wait()
        write_copy(n_chunks - 1, buf1, wsem1).wait()

    return gather_rows(table, idx)


def _moe(h2, e_tl, w_tl, r_tl, cnt, layer, w_gu, b_gu, w_down, b_down, tb=1):
    bsz, t, half = h2.shape
    n = bsz * t
    flat = lambda a: a.reshape(bsz // tb, -1, TOP_K, tb, a.shape[2] // tb).transpose(2, 0, 3, 1, 4).reshape(TOP_K, n)
    e_k, w_k, r_k = flat(e_tl), flat(w_tl), flat(r_tl)
    counts = cnt[:, 0].astype(I32)
    padded = (counts + MOE_BLOCK - 1) // MOE_BLOCK * MOE_BLOCK
    pad_end = jnp.cumsum(padded)
    pad_start = pad_end - padded
    n_blocks = (n * TOP_K + MOE_BLOCK - 1) // MOE_BLOCK + N_EXPERTS
    block_start = jnp.arange(n_blocks, dtype=I32) * MOE_BLOCK
    block_e = jnp.minimum(jnp.sum((pad_end[None, :] <= block_start[:, None]).astype(I32), axis=1), N_EXPERTS - 1)
    n_used = (pad_end[-1:] // MOE_BLOCK).astype(I32)
    start_k = jnp.sum(jnp.where(e_k[..., None] == jnp.arange(N_EXPERTS, dtype=I32), pad_start, 0), axis=-1)
    dest = (start_k + r_k).reshape(TOP_K * n)
    xb = _sc_dispatch(h2.reshape(n, half), dest, n_blocks * MOE_BLOCK)
    yb = _experts(xb, block_e, n_used, layer, w_gu, b_gu, w_down, b_down)
    return _sc_gather(yb, dest), w_k.T


def _rope_tables(n_ctx, n_lat):
    n_freq = RET_DK // 4
    inv_freq = ROPE_BASE ** (-jnp.arange(n_freq, dtype=F32) / n_freq)
    pos = jnp.arange(n_lat, dtype=I32)
    cos, sin = [], []
    for p in (pos // GRID_W, pos % GRID_W):
        ang = p.astype(F32)[:, None] * inv_freq
        cos += [jnp.cos(ang), jnp.cos(ang)]
        sin += [-jnp.sin(ang), jnp.sin(ang)]
    cos, sin = jnp.concatenate(cos, axis=1), jnp.concatenate(sin, axis=1)
    return (jnp.concatenate([jnp.ones((n_ctx, RET_DK), F32), cos], axis=0),
            jnp.concatenate([jnp.zeros((n_ctx, RET_DK), F32), sin], axis=0))


def kernel(x, c, ctx, c_ctx, ada_w, ada_b, norm1_g, norm2_g, ab_w_in, ab_w_out, gla_wa, gla_ba, gla_norm_g, s5_lam_re, s5_lam_im, s5_log_step, s5_b_re, s5_b_im, s5_c_re, s5_c_im, s5_d, s5_glu_w, s5_glu_b, ret_w_in, ret_w_out, ret_decay_logit, ret_norm_g, moe_w_router, moe_b_router, moe_w_gu, moe_b_gu, moe_w_down, moe_b_down, final_norm_g):
    bsz, n_lat, d = x.shape
    n_ctx = ctx.shape[1]
    depth = ada_w.shape[0]
    assert depth == 2 and d == D_MODEL and bsz == 8, "kernels are laid out for the stated problem shape"
    assert n_ctx % TOKEN_TILE == 0 and n_lat % LATENT_TILE == 0 and n_lat % GRID_W == 0
    t = n_ctx + n_lat
    nct = n_ctx // TOKEN_TILE

    cvec = jnp.zeros((16, d), F32).at[:bsz].set(c).at[bsz].set(c_ctx)
    mod = _ada_mod(cvec, ada_w, ada_b).reshape(depth, 16, 6, d)
    mods = [jnp.stack([jnp.broadcast_to(mod[l, bsz], (bsz, 6, d)), mod[l, :bsz]], axis=1) for l in range(depth)]

    w_in = ab_w_in[0].astype(BF16)
    cuts = [0, AB_QK, 2 * AB_QK, 2 * AB_QK + AB_V, 2 * AB_QK + 2 * AB_V, 2 * AB_QK + 2 * AB_V + 2 * GLA_RANK,
            w_in.shape[1]]
    pieces = [w_in[:, a:b] for a, b in zip(cuts[:-1], cuts[1:])]
    wa_pad = jnp.zeros((2, 2 * GLA_RANK, AB_QK), F32)
    wa_pad = wa_pad.at[0, :GLA_RANK].set(gla_wa[0, 0]).at[1, GLA_RANK:].set(gla_wa[0, 1])
    outs = _inproj0(ctx, x, mods[0], norm1_g[0], pieces, wa_pad, gla_ba[0].reshape(2, 1, AB_QK), nct)
    v, g, u = outs[8:]
    o_f, o_b = _gla((outs[0:4], outs[4:8]), v, n_ctx)
    ops = _s5_operators(s5_lam_re[0], s5_lam_im[0], s5_log_step[0], s5_b_re[0], s5_b_im[0], s5_c_re[0], s5_c_im[0])
    ys = _s5(u, ops, n_ctx)
    consts = [jnp.tile(gla_norm_g[0], GLA_HEADS).reshape(1, AB_V), s5_d[0].reshape(1, S5_CH),
              s5_glu_w[0].astype(BF16), s5_glu_b[0].reshape(1, S5_CH), ab_w_out[0].astype(BF16)]
    x1, h2, e_tl, w_tl, r_tl, cnt = _mix_call(
        functools.partial(_mix0_body, nct), "mix_gla_s5", (ctx, x), mods[0], t // TOKEN_TILE,
        [(o_f, False), (o_b, False), (g, False), (ys, False), (u, False)], consts,
        norm2_g[0], moe_w_router[0], moe_b_router[0], nct, 0, tb=TOKEN_TILE_BATCH)
    yk, w_tok = _moe(h2, e_tl, w_tl, r_tl, cnt, 0, moe_w_gu, moe_b_gu, moe_w_down, moe_b_down, tb=TOKEN_TILE_BATCH)

    w_in = ret_w_in[0].astype(BF16)
    cuts = [0, RET_QK, 2 * RET_QK, 2 * RET_QK + RET_MIX, w_in.shape[1]]
    pieces = [w_in[:, a:b] for a, b in zip(cuts[:-1], cuts[1:])]
    cos_t, sin_t = _rope_tables(n_ctx, n_lat)
    x2, q, k, v, g = _inproj1(x1, mods[0], yk, w_tok, mods[1], norm1_g[1], cos_t, sin_t, pieces, nct)
    o_f, o_b = _retention(q, k, v, ret_decay_logit[0], n_ctx)
    consts = [ret_norm_g[0].reshape(1, RET_MIX), ret_w_out[0].astype(BF16)]
    x1, h2, e_tl, w_tl, r_tl, cnt = _mix_call(
        _mix1_body, "mix_retention", (x2,), mods[1], n_lat // LATENT_TILE,
        [(o_f, False), (o_b, False), (g, False)], consts,
        norm2_g[1], moe_w_router[1], moe_b_router[1], 0, 0, tm=LATENT_TILE)
    yk, w_tok = _moe(h2, e_tl, w_tl, r_tl, cnt, 1, moe_w_gu, moe_b_gu, moe_w_down, moe_b_down)
    return _combine(x1, mods[1], yk, w_tok, 0, 0, final_norm_g, LATENT_TILE)
```

```python
import functools
import math

import jax
import jax.numpy as jnp
from jax import lax
from jax.experimental import pallas as pl
from jax.experimental.pallas import tpu as pltpu
from jax.experimental.pallas import tpu_sc as plsc

F32, BF16, I32, U32 = jnp.float32, jnp.bfloat16, jnp.int32, jnp.uint32

D_MODEL = 1024
GRID_W = 64
EPS = 1e-6
GLA_HEADS, GLA_DK, GLA_DV, GLA_RANK, GLA_TAU, GLA_CHUNK = 4, 64, 128, 16, 16.0, 64
GLA_BATCH = 8
AB_QK, AB_V = GLA_HEADS * GLA_DK, GLA_HEADS * GLA_DV
S5_CH, S5_GROUP, S5_GROUPS, S5_P = 512, 16, 32, 64
S5_CHUNK = 16
S5_FOLD_BATCH = 4
S5_SCAN_GROUPS = 2
RET_HEADS, RET_DK, RET_DV = 4, 256, 512
RET_CHUNK = 256
RET_BATCH = 2
RET_QK, RET_MIX = RET_HEADS * RET_DK, RET_HEADS * RET_DV
ROPE_BASE = 10000.0
N_EXPERTS, TOP_K, D_FF = 32, 4, 1024
SWIGLU_LIMIT, SWIGLU_ALPHA = 7.0, 1.702
MOE_BLOCK = 1024
TOKEN_TILE = 256
LATENT_TILE = 512
TOKEN_TILE_BATCH = 2
ADA_TILE = 768
VMEM_LIMIT = 56 * 1024 * 1024
SC_CORES, SC_SUBCORES = 2, 16
SC_WORKERS = SC_CORES * SC_SUBCORES
SC_CHUNK = 64
LANES = 128

def _params(*sem):
    return pltpu.CompilerParams(dimension_semantics=sem, vmem_limit_bytes=VMEM_LIMIT)


def _dot(a, b):
    return jnp.dot(a, b, preferred_element_type=F32)


def _dot_nt(a, b):
    return lax.dot_general(a, b, (((1,), (1,)), ((), ())), preferred_element_type=F32)


def _dot_tn(a, b):
    return lax.dot_general(a, b, (((0,), (0,)), ((), ())), preferred_element_type=F32)


def _split(a):
    hi = a.astype(BF16)
    return hi, (a - hi.astype(F32)).astype(BF16)


def _dot3(a, b, dot=_dot):
    ah, al = _split(a)
    bh, bl = _split(b)
    return dot(ah, bh) + (dot(ah, bl) + dot(al, bh))


def _pack_rows(x):
    h = x.shape[1] // 2
    lo = lax.bitcast_convert_type(x[:, 0:h].astype(BF16).astype(F32), U32)
    hi = lax.bitcast_convert_type(x[:, h:2 * h].astype(BF16).astype(F32), U32)
    return hi | (lo >> 16)


def _unpack_rows(p):
    lo = lax.bitcast_convert_type(p << 16, F32)
    hi = lax.bitcast_convert_type(p & jnp.uint32(0xFFFF0000), F32)
    return lo, hi


def _silu(x):
    return x * jax.nn.sigmoid(x)


def _norm_mod(x, g, shift, scale):
    r = lax.rsqrt(jnp.mean(x * x, axis=-1, keepdims=True) + EPS)
    return (x * r * g) * (1.0 + scale) + shift


def _const_spec(shape):
    nd = len(shape)
    return pl.BlockSpec(shape, lambda *_: (0,) * nd, pipeline_mode=pl.Buffered(1))


def _ada_body(c_ref, w_ref, b_ref, o_ref):
    o_ref[0] = _dot3(_silu(c_ref[...]), w_ref[0]) + b_ref[0]


def _ada_mod(cvec, ada_w, ada_b):
    depth, d, n6 = ada_w.shape
    rows = cvec.shape[0]
    return pl.pallas_call(
        _ada_body,
        grid=(depth, n6 // ADA_TILE),
        in_specs=[_const_spec((rows, d)),
                  pl.BlockSpec((1, d, ADA_TILE), lambda l, j: (l, 0, j)),
                  pl.BlockSpec((1, 1, ADA_TILE), lambda l, j: (l, 0, j))],
        out_specs=pl.BlockSpec((1, rows, ADA_TILE), lambda l, j: (l, 0, j)),
        out_shape=jax.ShapeDtypeStruct((depth, rows, n6), F32),
        compiler_params=_params("arbitrary", "arbitrary"),
        name="ada_mod",
    )(cvec, ada_w, ada_b.reshape(depth, 1, n6))


def _split_specs(nct, d, tb=1):
    ctx_spec = pl.BlockSpec((tb, TOKEN_TILE, d), lambda b, i: (b, jnp.minimum(i, nct - 1), 0))
    lat_spec = pl.BlockSpec((tb, TOKEN_TILE, d), lambda b, i: (b, jnp.maximum(i - nct, 0), 0))
    return ctx_spec, lat_spec


def _stream_tile(nct, ctx_ref, lat_ref, bb=0):
    return jnp.where(pl.program_id(1) < nct, ctx_ref[bb], lat_ref[bb])


def _normed_rows(nct, ctx_ref, lat_ref, mod_ref, g_ref):
    tiles = []
    for bb in range(mod_ref.shape[0]):
        m = mod_ref[bb, 0]
        tiles.append(_norm_mod(_stream_tile(nct, ctx_ref, lat_ref, bb), g_ref[...], m[0:1], m[1:2]).astype(BF16))
    return tiles[0] if len(tiles) == 1 else jnp.concatenate(tiles, axis=0)


def _store_rows(o_ref, val):
    tm = o_ref.shape[1]
    for bb in range(o_ref.shape[0]):
        o_ref[bb] = val[bb * tm:(bb + 1) * tm].astype(o_ref.dtype)


def _inproj0_body(nct, ctx_ref, lat_ref, mod_ref, g_ref, wq, wk, wv, wg, wlow, wu, wa_ref, ba_ref, tri_ref, ones_ref,
                  qd_f, ki_f, ks_f, ed_f, qd_b, ki_b, ks_b, ed_b, ov, og, ou):
    h = _normed_rows(nct, ctx_ref, lat_ref, mod_ref, g_ref)
    tm = ctx_ref.shape[1]
    low = _dot(h, wlow[...])
    q = _dot(h, wq[...]) * (GLA_DK ** -0.5)
    k = _dot(h, wk[...])
    outs = ((qd_f, ki_f, ks_f, ed_f), (qd_b, ki_b, ks_b, ed_b))
    for d, (qd_ref, ki_ref, ks_ref, ed_ref) in enumerate(outs):
        z = _dot3(low, wa_ref[d]) + ba_ref[d]
        log_a = (jnp.minimum(z, 0.0) - jnp.log1p(jnp.exp(-jnp.abs(z)))) * (1.0 / GLA_TAU)
        la_hi, la_lo = _split(log_a)
        cums, tots = [], []
        for bb in range(qd_ref.shape[0]):
            hi, lo = la_hi[bb * tm:(bb + 1) * tm], la_lo[bb * tm:(bb + 1) * tm]
            cums.append(_dot(tri_ref[d], hi) + _dot(tri_ref[d], lo))
            tots.append(_dot(ones_ref[...], hi) + _dot(ones_ref[...], lo))
        cum = cums[0] if len(cums) == 1 else jnp.concatenate(cums, axis=0)
        tot = tots[0] if len(tots) == 1 else jnp.concatenate(tots, axis=0)
        _store_rows(qd_ref, q * jnp.exp(cum))
        _store_rows(ki_ref, k * jnp.exp(-cum))
        _store_rows(ks_ref, k * jnp.exp(tot - cum))
        for bb in range(ed_ref.shape[0]):
            for ch in range(tm // GLA_CHUNK):
                row = bb * tm + ch * GLA_CHUNK
                ed_ref[bb, ch] = jnp.exp(tot[row:row + 1])
    _store_rows(ov, _dot(h, wv[...]))
    _store_rows(og, _dot(h, wg[...]))
    u = _dot(h, wu[...])
    for qb in range(ou.shape[0]):
        _store_rows(ou.at[qb], u[:, qb * LANES:(qb + 1) * LANES])


def _inproj0(ctx, x, mods, norm_g, weights, wa_pad, ba, nct):
    bsz, n_lat, d = x.shape
    t = ctx.shape[1] + n_lat
    tm, tb = TOKEN_TILE, TOKEN_TILE_BATCH
    mod_spec = pl.BlockSpec((tb, 1, 6, d), lambda b, i: (b, (i >= nct).astype(I32), 0, 0))
    ctx_spec, lat_spec = _split_specs(nct, d, tb)
    pos = jnp.arange(tm)
    same_chunk = (pos[:, None] // GLA_CHUNK) == (pos[None, :] // GLA_CHUNK)
    tri = jnp.stack([same_chunk & (pos[None, :] <= pos[:, None]),
                     same_chunk & (pos[None, :] >= pos[:, None])]).astype(BF16)
    ones = same_chunk.astype(BF16)
    consts = list(weights) + [wa_pad, ba, tri, ones]
    tok = lambda w, dt: (pl.BlockSpec((tb, tm, w), lambda b, i: (b, i, 0)), jax.ShapeDtypeStruct((bsz, t, w), dt))
    per_chunk = (pl.BlockSpec((tb, tm // GLA_CHUNK, 1, AB_QK), lambda b, i: (b, i, 0, 0)),
                 jax.ShapeDtypeStruct((bsz, t // GLA_CHUNK, 1, AB_QK), F32))
    one_dir = [tok(AB_QK, BF16)] * 3 + [per_chunk]
    u_blocks = (pl.BlockSpec((S5_CH // LANES, tb, tm, LANES), lambda b, i: (0, b, i, 0)),
                jax.ShapeDtypeStruct((S5_CH // LANES, bsz, t, LANES), F32))
    outs = one_dir + one_dir + [tok(AB_V, BF16), tok(AB_V, BF16), u_blocks]
    return pl.pallas_call(
        functools.partial(_inproj0_body, nct),
        grid=(bsz // tb, t // tm),
        in_specs=[ctx_spec, lat_spec, mod_spec, _const_spec((1, d))] + [_const_spec(a.shape) for a in consts],
        out_specs=[o[0] for o in outs],
        out_shape=[o[1] for o in outs],
        compiler_params=_params("arbitrary", "arbitrary"),
        name="inproj_gla_s5",
    )(ctx, x, mods, norm_g.reshape(1, d), *consts)


def _rope(acc, cos_ref, sin_ref, o_ref, scale):
    tm = o_ref.shape[1]
    for grp in range(acc.shape[1] // LANES):
        half = grp % 2
        cs = cos_ref[:, half * LANES:(half + 1) * LANES]
        sn = sin_ref[:, half * LANES:(half + 1) * LANES]
        for bb in range(o_ref.shape[0]):
            xg = acc[bb * tm:(bb + 1) * tm, grp * LANES:(grp + 1) * LANES]
            out = xg * cs + pltpu.roll(xg, LANES // 2, 1) * sn
            o_ref[bb, :, grp * LANES:(grp + 1) * LANES] = (out * scale).astype(o_ref.dtype)


def _moe_residual(x1_ref, g2, yk_ref, w_ref, bb=0):
    d = x1_ref.shape[2]
    half = d // 2
    y_lo, y_hi = None, None
    for k in range(TOP_K):
        lo, hi = _unpack_rows(yk_ref[k, bb])
        wk = w_ref[bb, :, k:k + 1]
        y_lo = lo * wk if y_lo is None else y_lo + lo * wk
        y_hi = hi * wk if y_hi is None else y_hi + hi * wk
    return x1_ref[bb, :, 0:half] + g2[:, 0:half] * y_lo, x1_ref[bb, :, half:d] + g2[:, half:d] * y_hi


def _inproj1_body(x1_ref, mod0_ref, yk_ref, w_ref, mod_ref, g_ref, cos_ref, sin_ref, wq, wk, wv, wg, ox, oq, ok, ov, og):
    half = x1_ref.shape[2] // 2
    for bb in range(x1_ref.shape[0]):
        x2_lo, x2_hi = _moe_residual(x1_ref, mod0_ref[bb, 0][5:6], yk_ref, w_ref, bb)
        ox[bb, :, 0:half] = x2_lo
        ox[bb, :, half:2 * half] = x2_hi
        m = mod_ref[bb, 0]
        h = _norm_mod(jnp.concatenate([x2_lo, x2_hi], axis=1), g_ref[...], m[0:1], m[1:2]).astype(BF16)
        _rope(_dot(h, wq[...]), cos_ref, sin_ref, oq.at[bb:bb + 1], 1.0)
        _rope(_dot(h, wk[...]), cos_ref, sin_ref, ok.at[bb:bb + 1], RET_DK ** -0.5)
        ov[bb] = _dot(h, wv[...]).astype(ov.dtype)
        og[bb] = _dot(h, wg[...]).astype(og.dtype)


def _inproj1(x1, mods0, yk, w_tok, mods, norm_g, cos_t, sin_t, weights, nct):
    bsz, t, d = x1.shape
    tm, tb = TOKEN_TILE, TOKEN_TILE_BATCH
    mod_spec = pl.BlockSpec((tb, 1, 6, d), lambda b, i: (b, (i >= nct).astype(I32), 0, 0))
    tok = lambda w: pl.BlockSpec((tb, tm, w), lambda b, i: (b, i, 0))
    lat = lambda w: pl.BlockSpec((tb, tm, w), lambda b, i: (b, jnp.maximum(i - nct, 0), 0))
    tab_spec = pl.BlockSpec((tm, RET_DK), lambda b, i: (i, 0))
    wq, wk, wv, wg = weights
    n_lat = t - nct * tm
    return pl.pallas_call(
        _inproj1_body,
        grid=(bsz // tb, t // tm),
        in_specs=[tok(d), mod_spec, pl.BlockSpec((TOP_K, tb, tm, d // 2), lambda b, i: (0, b, i, 0)), tok(TOP_K),
                  mod_spec, _const_spec((1, d)), tab_spec, tab_spec] + [_const_spec(w.shape) for w in weights],
        out_specs=[lat(d), tok(wq.shape[1]), tok(wk.shape[1]), tok(wv.shape[1]), lat(wg.shape[1])],
        out_shape=[jax.ShapeDtypeStruct((bsz, n_lat, d), F32), jax.ShapeDtypeStruct((bsz, t, wq.shape[1]), BF16),
                   jax.ShapeDtypeStruct((bsz, t, wk.shape[1]), BF16), jax.ShapeDtypeStruct((bsz, t, wv.shape[1]), BF16),
                   jax.ShapeDtypeStruct((bsz, n_lat, wg.shape[1]), BF16)],
        compiler_params=_params("arbitrary", "arbitrary"),
        name="inproj_retention",
    )(x1, mods0, yk.reshape(TOP_K, bsz, t, d // 2), w_tok.reshape(bsz, t, TOP_K), mods, norm_g.reshape(1, d),
      cos_t, sin_t, *weights)


def _backward_chunk(n, n_ctx_chunks, n_chunks):
    return jnp.where(n < n_ctx_chunks, n_ctx_chunks - 1 - n, n_chunks - 1 - (n - n_ctx_chunks))


def _gla_body(qd_f, ki_f, ks_f, ed_f, v_f, qd_b, ki_b, ks_b, ed_b, v_b, hmask_ref, bdmask_ref, o_f, o_b, st_f, st_b):
    c = GLA_CHUNK

    @pl.when(pl.program_id(1) == 0)
    def _():
        st_f[...] = jnp.zeros_like(st_f)
        st_b[...] = jnp.zeros_like(st_b)

    r4 = lax.broadcasted_iota(I32, (GLA_HEADS * c, c), 0) & (c - 1)
    c4 = lax.broadcasted_iota(I32, (GLA_HEADS * c, c), 1)
    dirs = ((qd_f, ki_f, ks_f, ed_f, v_f, o_f, st_f), (qd_b, ki_b, ks_b, ed_b, v_b, o_b, st_b))
    chains = [(bb, d) + dirs[d] for bb in range(qd_f.shape[0]) for d in range(2)]
    scores, inter, grow = [], [], []
    for bb, d, qd_ref, ki_ref, ks_ref, ed_ref, v_ref, o_ref, st_ref in chains:
        q_dec = qd_ref[bb]
        q_heads = jnp.concatenate([q_dec] * GLA_HEADS, axis=0) * hmask_ref[...]
        seen4 = (c4 <= r4) if d == 0 else (c4 >= r4)
        scores.append(jnp.where(seen4, _dot_nt(q_heads, ki_ref[bb]), 0.0).astype(BF16))
        inter.append(_dot_nt(q_dec, st_ref[bb].astype(BF16)))
        grow.append(_dot_tn(v_ref[bb], ks_ref[bb]))
    for (bb, d, qd_ref, ki_ref, ks_ref, ed_ref, v_ref, o_ref, st_ref), sc, o_inter, dst in zip(chains, scores, inter, grow):
        v = v_ref[bb]
        o_intra = jnp.concatenate(
            [_dot(sc[h * c:(h + 1) * c], v[:, h * GLA_DV:(h + 1) * GLA_DV]) for h in range(GLA_HEADS)], axis=1)
        o_ref[bb] = o_intra + o_inter
        st_ref[bb] = st_ref[bb] * ed_ref[bb, 0] + bdmask_ref[...] * dst


def _gla(per_dir, v, n_ctx):
    bsz, t, _ = v.shape
    nc, ncc = t // GLA_CHUNK, n_ctx // GLA_CHUNK
    gb = GLA_BATCH
    fwd = lambda b, n: (b, n, 0)
    bwd = lambda b, n: (b, _backward_chunk(n, ncc, nc), 0)
    hmask = (jnp.arange(AB_QK)[:, None] // GLA_CHUNK == jnp.arange(AB_QK)[None, :] // GLA_DK).astype(BF16)
    bdmask = (jnp.arange(AB_V)[:, None] // GLA_DV == jnp.arange(AB_QK)[None, :] // GLA_DK).astype(F32)

    def specs(idx):
        idx4 = lambda b, n: idx(b, n) + (0,)
        return [pl.BlockSpec((gb, GLA_CHUNK, AB_QK), idx)] * 3 + [pl.BlockSpec((gb, 1, 1, AB_QK), idx4),
                                                                  pl.BlockSpec((gb, GLA_CHUNK, AB_V), idx)]

    return pl.pallas_call(
        _gla_body,
        grid=(bsz // gb, nc),
        in_specs=specs(fwd) + specs(bwd) + [_const_spec(hmask.shape), _const_spec(bdmask.shape)],
        out_specs=[pl.BlockSpec((gb, GLA_CHUNK, AB_V), fwd), pl.BlockSpec((gb, GLA_CHUNK, AB_V), bwd)],
        out_shape=[jax.ShapeDtypeStruct((bsz, t, AB_V), F32)] * 2,
        scratch_shapes=[pltpu.VMEM((gb, AB_V, AB_QK), F32)] * 2,
        compiler_params=_params("arbitrary", "arbitrary"),
        name="gla_scan",
    )(*per_dir[0], v, *per_dir[1], v, hmask, bdmask)


def _cmul(x, y):
    return x[0] * y[0] - x[1] * y[1], x[0] * y[1] + x[1] * y[0]


def _s5_operators(lam_re, lam_im, log_step, b_re, b_im, c_re, c_im):
    ln = S5_CHUNK
    step = jnp.exp(log_step.astype(F32))[..., None]
    lam_re, lam_im = lam_re.astype(F32), lam_im.astype(F32)
    mag = jnp.exp(lam_re * step)
    a = (mag * jnp.cos(lam_im * step), mag * jnp.sin(lam_im * step))
    den = lam_re * lam_re + lam_im * lam_im
    f_re = ((a[0] - 1.0) * lam_re + a[1] * lam_im) / den
    f_im = (a[1] * lam_re - (a[0] - 1.0) * lam_im) / den
    bt_re, bt_im = b_re.transpose(0, 2, 1), b_im.transpose(0, 2, 1)
    bb = _cmul((f_re[:, :, None, :], f_im[:, :, None, :]), (bt_re, bt_im))
    bbt = jnp.concatenate([bb[0], -bb[1]], axis=-1)
    pw = (a[0][:, :, None, :], a[1][:, :, None, :])
    while pw[0].shape[2] < ln:
        top = (pw[0][:, :, -1:, :], pw[1][:, :, -1:, :])
        nxt = _cmul(top, pw)
        pw = (jnp.concatenate([pw[0], nxt[0]], axis=2), jnp.concatenate([pw[1], nxt[1]], axis=2))
    pw = (jnp.concatenate([jnp.ones_like(pw[0][:, :, :1]), pw[0]], axis=2),
          jnp.concatenate([jnp.zeros_like(pw[1][:, :, :1]), pw[1]], axis=2))
    ca = _cmul((c_re[:, :, None], c_im[:, :, None]), (pw[0][:, :, :, None, :], pw[1][:, :, :, None, :]))
    by_dir = lambda arr, lo, flip_d: jnp.stack([jnp.flip(arr[d, :, lo:lo + ln], axis=1) if d == flip_d
                                                else arr[d, :, lo:lo + ln] for d in range(2)])
    rows = lambda arr: arr.reshape(2, S5_GROUPS, ln * S5_GROUP, 2 * S5_P)
    cab = rows(by_dir(jnp.concatenate([ca[0], ca[1]], axis=-1), 0, 1))
    cab2 = rows(by_dir(jnp.concatenate([ca[0], -ca[1]], axis=-1), 1, 1)).astype(BF16)
    pwx = by_dir(jnp.concatenate([pw[0], pw[1]], axis=-1), 0, 0)
    lr, li = pw[0][:, :, ln], pw[1][:, :, ln]
    ac_rows = [jnp.concatenate([lr, lr], -1), jnp.concatenate([-li, li], -1), jnp.concatenate([li, -li], -1)]
    ac = jnp.stack(ac_rows + [jnp.zeros_like(ac_rows[0])] * 5, axis=2)
    return cab, cab2, bbt, pwx, ac


def _s5_group_operators(gg, cab_ref, bbt_ref, pwx_ref, tz, wx):
    ln, ch, p = S5_CHUNK, S5_GROUP, S5_P
    lane = lax.broadcasted_iota(I32, (ch, ln * ch), 1)
    for d in range(2):
        kern = _dot3(bbt_ref[d, gg], cab_ref[d, gg], dot=_dot_nt)
        bt = bbt_ref[d, gg]
        b_re, b_im = bt[:, 0:p], -bt[:, p:2 * p]
        for j in range(ln):
            if d == 0:
                blk = jnp.where(lane >= j * ch, kern if j == 0 else pltpu.roll(kern, j * ch, 1), 0.0)
            else:
                blk = jnp.where(lane < (j + 1) * ch, kern if j == ln - 1 else pltpu.roll(kern, (j + 1) * ch, 1), 0.0)
            tz[gg, d, j * ch:(j + 1) * ch, :] = blk.astype(BF16)
            pr, pi = pwx_ref[d, gg, j:j + 1, 0:p], pwx_ref[d, gg, j:j + 1, p:2 * p]
            x_re, x_im = pr * b_re - pi * b_im, pr * b_im + pi * b_re
            wx[gg, d, j * ch:(j + 1) * ch, :] = jnp.concatenate([x_re, x_im, x_im, x_re], axis=1).astype(BF16)


def _s5_placement(pall):
    rows, cols = pall.shape[1], pall.shape[2]
    row = lax.broadcasted_iota(I32, (rows, cols), 0)
    col = lax.broadcasted_iota(I32, (rows, cols), 1)
    same_token = (row // LANES) == (col // S5_GROUP)
    for g8 in range(pall.shape[0]):
        pall[g8] = jnp.where(same_token & ((row % LANES) == g8 * S5_GROUP + (col % S5_GROUP)), 1.0, 0.0).astype(BF16)


def _first_step():
    return (pl.program_id(0) == 0) & (pl.program_id(1) == 0)


def _s5_fold_body(ncs, u_ref, o_ref, pall, ucat):
    @pl.when(_first_step())
    def _():
        _s5_placement(pall)

    for b in range(u_ref.shape[1]):
        for j in range(S5_CHUNK):
            ucat[b * ncs:(b + 1) * ncs, j * LANES:(j + 1) * LANES] = u_ref[0, b, pl.ds(j, ncs, stride=S5_CHUNK), :].astype(BF16)
    for g8 in range(pall.shape[0]):
        o_ref[g8] = _dot(ucat[...], pall[g8]).astype(BF16)


def _s5_unfold_body(ncs, y_ref, o_ref, pall):
    @pl.when(_first_step())
    def _():
        _s5_placement(pall)

    def token_pair(i2, carry):
        r0 = pl.multiple_of(i2 * 2 * LANES, 2 * LANES)
        acc = _dot_nt(y_ref[0], pall[0, pl.ds(r0, 2 * LANES), :])
        for g8 in range(1, pall.shape[0]):
            acc = acc + _dot_nt(y_ref[g8], pall[g8, pl.ds(r0, 2 * LANES), :])
        for b in range(o_ref.shape[1]):
            for par in range(2):
                o_ref[0, b, pl.ds(2 * i2 + par, ncs, stride=S5_CHUNK), :] = (
                    acc[b * ncs:(b + 1) * ncs, par * LANES:(par + 1) * LANES])
        return carry

    lax.fori_loop(0, S5_CHUNK // 2, token_pair, 0)


def _s5_body(ncs_ctx, ncs, rows, u_ref, cab_ref, cab2_ref, bbt_ref, pwx_ref, ac_ref, y_ref, tz, wx, *vecs):
    half = 2 * S5_P
    n_groups = u_ref.shape[0]
    groups = [vecs[6 * gg:6 * gg + 6] for gg in range(n_groups)]
    for gg, (xx_f, xs_f, xx_b, xs_b, _, _) in enumerate(groups):
        _s5_group_operators(gg, cab_ref, bbt_ref, pwx_ref, tz, wx)
        for d, (xx, xs) in enumerate(((xx_f, xs_f), (xx_b, xs_b))):
            r = _dot(u_ref[gg], wx[gg, d])
            xx[...] = r[:, :half]
            xs[...] = r[:, half:]

    def advance(ac, s, s_sw, x, x_sw):
        return ac[0:1] * s + ac[1:2] * s_sw + x, ac[0:1] * s_sw + ac[2:3] * s + x_sw

    def step(n, carry):
        at_f = pl.ds(n, rows, stride=ncs)
        at_b = pl.ds(_backward_chunk(n, ncs_ctx, ncs), rows, stride=ncs)
        out = []
        for gg, (xx_f, xs_f, xx_b, xs_b, sin_f, sin_b) in enumerate(groups):
            s_f, sw_f, s_b, sw_b = carry[4 * gg:4 * gg + 4]
            sin_f[at_f, :] = s_f
            sin_b[at_b, :] = s_b
            out += advance(ac_ref[0, gg], s_f, sw_f, xx_f[at_f, :], xs_f[at_f, :])
            out += advance(ac_ref[1, gg], s_b, sw_b, xx_b[at_b, :], xs_b[at_b, :])
        return tuple(out)

    zero = jnp.zeros((rows, half), F32)
    lax.fori_loop(0, ncs, step, (zero,) * (4 * n_groups))
    for gg, (_, _, _, _, sin_f, sin_b) in enumerate(groups):
        u = u_ref[gg]
        y_ref[gg] = (_dot(u, tz[gg, 0]) + _dot(u, tz[gg, 1]) + _dot_nt(sin_f[...].astype(BF16), cab2_ref[0, gg])
                     + _dot_nt(sin_b[...].astype(BF16), cab2_ref[1, gg])).astype(BF16)


def _s5(u4, ops, n_ctx):
    nq, bsz, t, _ = u4.shape
    ln, lanes = S5_CHUNK, S5_CHUNK * S5_GROUP
    gq = S5_GROUPS // nq
    ncs, ncs_ctx = t // ln, n_ctx // ln
    m = ncs * bsz
    hb = S5_FOLD_BATCH
    tok_spec = pl.BlockSpec((1, hb, t, LANES), lambda q, h: (q, h, 0, 0))
    grp_spec = pl.BlockSpec((gq, hb * ncs, lanes), lambda q, h: (q, h, 0))
    pall = pltpu.VMEM((gq, ln * LANES, lanes), BF16)
    ug = pl.pallas_call(
        functools.partial(_s5_fold_body, ncs),
        grid=(nq, bsz // hb),
        in_specs=[tok_spec],
        out_specs=grp_spec,
        out_shape=jax.ShapeDtypeStruct((S5_GROUPS, m, lanes), BF16),
        scratch_shapes=[pall, pltpu.VMEM((hb * ncs, ln * LANES), BF16)],
        compiler_params=_params("arbitrary", "arbitrary"),
        name="s5_fold",
    )(u4)
    sg = S5_SCAN_GROUPS
    dir_spec = lambda arr: pl.BlockSpec((2, sg) + arr.shape[2:], lambda g: (0, g, 0, 0))
    yg = pl.pallas_call(
        functools.partial(_s5_body, ncs_ctx, ncs, bsz),
        grid=(S5_GROUPS // sg,),
        in_specs=[pl.BlockSpec((sg, m, lanes), lambda g: (g, 0, 0))] + [dir_spec(arr) for arr in ops],
        out_specs=pl.BlockSpec((sg, m, lanes), lambda g: (g, 0, 0)),
        out_shape=jax.ShapeDtypeStruct((S5_GROUPS, m, lanes), BF16),
        scratch_shapes=[pltpu.VMEM((sg, 2, lanes, lanes), BF16)] * 2 + [pltpu.VMEM((m, 2 * S5_P), F32)] * (6 * sg),
        compiler_params=_params("arbitrary"),
        name="s5_scan",
    )(ug, *ops)
    return pl.pallas_call(
        functools.partial(_s5_unfold_body, ncs),
        grid=(nq, bsz // hb),
        in_specs=[grp_spec],
        out_specs=tok_spec,
        out_shape=jax.ShapeDtypeStruct(u4.shape, F32),
        scratch_shapes=[pall],
        compiler_params=_params("arbitrary", "arbitrary"),
        name="s5_unfold",
    )(yg)


def _ret_body(q_f, k_f, v_f, q_b, k_b, v_b, dmat_ref, rsc_ref, csc_ref, gam_ref, o_f, o_b, st_f, st_b):
    @pl.when(pl.program_id(1) == 0)
    def _():
        st_f[...] = jnp.zeros_like(st_f)
        st_b[...] = jnp.zeros_like(st_b)

    dirs = ((q_f, k_f, v_f, o_f, st_f), (q_b, k_b, v_b, o_b, st_b))
    for bb in range(q_f.shape[0]):
        for d, (q_ref, k_ref, v_ref, o_ref, st_ref) in enumerate(dirs):
            for h in range(RET_HEADS):
                qh = q_ref[bb, :, h * RET_DK:(h + 1) * RET_DK]
                kh = k_ref[bb, :, h * RET_DK:(h + 1) * RET_DK]
                vh = v_ref[bb, :, h * RET_DV:(h + 1) * RET_DV]
                st = st_ref[bb, h]
                scores = (_dot_nt(qh, kh) * dmat_ref[d, h]).astype(BF16)
                o = _dot(scores, vh) + rsc_ref[d, h] * _dot(qh, st.astype(BF16))
                o_ref[bb, :, h * RET_DV:(h + 1) * RET_DV] = o.astype(o_ref.dtype)
                k_state = (kh.astype(F32) * csc_ref[d, h]).astype(BF16)
                st_ref[bb, h] = st * gam_ref[d, h] + _dot_tn(k_state, vh)


def _retention(q, k, v, decay_logit, n_ctx):
    bsz, t, _ = q.shape
    c = RET_CHUNK
    nc, ncc = t // c, n_ctx // c
    nl = nc - ncc
    rb = RET_BATCH
    log_gamma = jax.nn.log_sigmoid(decay_logit.astype(F32))[:, :, None, None]
    i = jnp.arange(c, dtype=F32)
    lag = i[:, None] - i[None, :]
    lag = jnp.stack([lag, -lag])[:, None]
    dmat = jnp.where(lag >= 0, jnp.exp(log_gamma * jnp.maximum(lag, 0.0)), 0.0)
    done = jnp.stack([i + 1.0, c - i])[:, None, :, None]
    rsc = jnp.exp(log_gamma * done)
    csc = jnp.exp(log_gamma * (c - done))
    gam = jnp.exp(log_gamma[:, :, 0, 0] * c)
    fwd = lambda b, n: (b, n, 0)
    bwd = lambda b, n: (b, _backward_chunk(n, ncc, nc), 0)
    o_fwd = lambda b, n: (b, jnp.maximum(n - ncc, 0), 0)
    o_bwd = lambda b, n: (b, nl - 1 - jnp.maximum(n - ncc, 0), 0)

    def specs(idx):
        return [pl.BlockSpec((rb, c, RET_QK), idx), pl.BlockSpec((rb, c, RET_QK), idx), pl.BlockSpec((rb, c, RET_MIX), idx)]

    return pl.pallas_call(
        _ret_body,
        grid=(bsz // rb, nc),
        in_specs=specs(fwd) + specs(bwd) + [_const_spec(dmat.shape), _const_spec(rsc.shape), _const_spec(csc.shape),
                                            pl.BlockSpec(memory_space=pltpu.SMEM)],
        out_specs=[pl.BlockSpec((rb, c, RET_MIX), o_fwd), pl.BlockSpec((rb, c, RET_MIX), o_bwd)],
        out_shape=[jax.ShapeDtypeStruct((bsz, nl * c, RET_MIX), BF16)] * 2,
        scratch_shapes=[pltpu.VMEM((rb, RET_HEADS, RET_DK, RET_DV), F32)] * 2,
        compiler_params=_params("arbitrary", "arbitrary"),
        name="retention_scan",
    )(q, k, v, q, k, v, dmat, rsc, csc, gam)


def _zero_counts_at_start(cnt_ref):
    @pl.when(_first_step())
    def _():
        cnt_ref[...] = jnp.zeros_like(cnt_ref)


def _route(xs, mixed, mods, n2g_ref, wr_ref, br_ref, x1_ref, h2_ref, e_ref, w_ref, r_ref, cnt_ref):
    rows = xs[0].shape[0]
    h2s = []
    for bb, (x, mod) in enumerate(zip(xs, mods)):
        x1 = x + mod[2:3] * mixed[bb * rows:(bb + 1) * rows]
        x1_ref[bb] = x1
        h2s.append(_norm_mod(x1, n2g_ref[...], mod[3:4], mod[4:5]))
        h2_ref[bb] = _pack_rows(h2s[bb])
    h2 = h2s[0] if len(h2s) == 1 else jnp.concatenate(h2s, axis=0)
    tm = h2.shape[0]
    logits = _dot3(wr_ref[...], h2, dot=_dot_nt) + br_ref[...]
    ie = lax.broadcasted_iota(I32, logits.shape, 0)
    tops, picks = [], []
    for _ in range(TOP_K):
        mx = jnp.max(logits, axis=0, keepdims=True)
        pick = jnp.min(jnp.where(logits == mx, ie, N_EXPERTS), axis=0, keepdims=True)
        tops.append(mx)
        picks.append(pick)
        logits = jnp.where(ie == pick, -jnp.inf, logits)
    ex = [jnp.exp(tk - tops[0]) for tk in tops]
    den = ex[0] + ex[1] + ex[2] + ex[3]
    for kk in range(TOP_K):
        w_ref[0, kk:kk + 1, :] = ex[kk] / den
        e_ref[0, kk:kk + 1, :] = picks[kk]

    earlier = (lax.broadcasted_iota(I32, (tm, tm), 0) < lax.broadcasted_iota(I32, (tm, tm), 1))
    earlier = jnp.where(earlier, 1.0, 0.0).astype(BF16)
    run = cnt_ref[:, 0:1]
    for kk, pick in enumerate(picks):
        onehot = jnp.where(ie == pick, 1.0, 0.0)
        before = _dot(onehot.astype(BF16), earlier) + run
        r_ref[0, kk:kk + 1, :] = jnp.sum(onehot * before, axis=0, keepdims=True).astype(I32)
        run = run + jnp.sum(onehot, axis=1, keepdims=True)
    cnt_ref[...] = jnp.broadcast_to(run, cnt_ref.shape)


def _mix0_body(nct, ctx_ref, lat_ref, mod_ref, of_ref, ob_ref, g_ref, ys_ref, u_ref, gng_ref, dsk_ref, gluw_ref,
               glub_ref, wo_ref, n2g_ref, wr_ref, br_ref, x1_ref, h2_ref, e_ref, w_ref, r_ref, cnt_ref):
    _zero_counts_at_start(cnt_ref)
    tb, tm = of_ref.shape[0], of_ref.shape[1]
    rows = lambda ref: ref[...].reshape(tb * tm, ref.shape[-1])
    o = rows(of_ref) + rows(ob_ref)
    heads = []
    for h in range(GLA_HEADS):
        oh = o[:, h * GLA_DV:(h + 1) * GLA_DV]
        heads.append(oh * lax.rsqrt(jnp.mean(oh * oh, axis=-1, keepdims=True) + EPS))
    gla = jnp.concatenate(heads, axis=1) * gng_ref[...] * _silu(rows(g_ref).astype(F32))
    lane_blocks = lambda ref: jnp.concatenate([ref[qb].reshape(tb * tm, LANES) for qb in range(ref.shape[0])], axis=1)
    y = jax.nn.gelu(lane_blocks(ys_ref) + dsk_ref[...] * lane_blocks(u_ref))
    y = y * jax.nn.sigmoid(_dot(y.astype(BF16), gluw_ref[...]) + glub_ref[...])
    mixed = _dot(gla.astype(BF16), wo_ref[0:AB_V]) + _dot(y.astype(BF16), wo_ref[AB_V:AB_V + S5_CH])
    _route([_stream_tile(nct, ctx_ref, lat_ref, bb) for bb in range(tb)], mixed, [mod_ref[bb, 0] for bb in range(tb)],
           n2g_ref, wr_ref, br_ref, x1_ref, h2_ref, e_ref, w_ref, r_ref, cnt_ref)


def _mix1_body(x_ref, mod_ref, of_ref, ob_ref, g_ref, ng_ref, wo_ref, n2g_ref, wr_ref, br_ref,
               x1_ref, h2_ref, e_ref, w_ref, r_ref, cnt_ref):
    _zero_counts_at_start(cnt_ref)
    mixed = None
    for h in range(RET_HEADS):
        sl = slice(h * RET_DV, (h + 1) * RET_DV)
        oh = of_ref[0, :, sl].astype(F32) + ob_ref[0, :, sl].astype(F32)
        mu = jnp.mean(oh, axis=-1, keepdims=True)
        cen = oh - mu
        var = jnp.mean(cen * cen, axis=-1, keepdims=True)
        gated = cen * lax.rsqrt(var + EPS) * ng_ref[:, sl] * _silu(g_ref[0, :, sl].astype(F32))
        part = _dot(gated.astype(BF16), wo_ref[sl])
        mixed = part if mixed is None else mixed + part
    _route([x_ref[0]], mixed, [mod_ref[0, 0]], n2g_ref, wr_ref, br_ref, x1_ref, h2_ref, e_ref, w_ref, r_ref, cnt_ref)


def _mix_call(body, name, stream, mods, tiles, acts, consts, norm2_g, w_router, b_router, n_tok, seg_tile0,
              tm=TOKEN_TILE, tb=1):
    bsz, _, d = stream[-1].shape
    off = lambda b, i: (b, i + seg_tile0, 0)
    loc = lambda b, i: (b, i, 0)
    ntl = bsz // tb * tiles
    flat = lambda b, i: (b * tiles + i, 0, 0)
    in_specs = list(_split_specs(n_tok, d, tb)) if len(stream) == 2 else [pl.BlockSpec((tb, tm, d), off)]
    in_specs.append(pl.BlockSpec((tb, 1, 6, d), lambda b, i: (b, ((i + seg_tile0) >= n_tok).astype(I32), 0, 0)))
    args = list(stream) + [mods]
    for arr, offset in acts:
        if arr.ndim == 4:
            in_specs.append(pl.BlockSpec((arr.shape[0], tb, tm, arr.shape[3]), lambda b, i: (0, b, i, 0)))
        else:
            in_specs.append(pl.BlockSpec((tb, tm, arr.shape[2]), off if offset else loc))
        args.append(arr)
    tail = list(consts) + [norm2_g.reshape(1, d), w_router.T, b_router.reshape(N_EXPERTS, 1)]
    in_specs += [_const_spec(a.shape) for a in tail]
    args += tail
    tok_out = pl.BlockSpec((1, TOP_K, tb * tm), flat)
    return pl.pallas_call(
        body,
        grid=(bsz // tb, tiles),
        in_specs=in_specs,
        out_specs=[pl.BlockSpec((tb, tm, d), loc), pl.BlockSpec((tb, tm, d // 2), loc), tok_out, tok_out, tok_out,
                   _const_spec((N_EXPERTS, LANES))],
        out_shape=[jax.ShapeDtypeStruct((bsz, tiles * tm, d), F32), jax.ShapeDtypeStruct((bsz, tiles * tm, d // 2), U32),
                   jax.ShapeDtypeStruct((ntl, TOP_K, tb * tm), I32), jax.ShapeDtypeStruct((ntl, TOP_K, tb * tm), F32),
                   jax.ShapeDtypeStruct((ntl, TOP_K, tb * tm), I32), jax.ShapeDtypeStruct((N_EXPERTS, LANES), F32)],
        compiler_params=_params("arbitrary", "arbitrary"),
        name=name,
    )(*args)


def _cast_rows(src_ref, dst_ref, rows):
    def chunk(j, carry):
        r = pl.multiple_of(j * rows, rows)
        dst_ref[pl.ds(r, rows), :] = src_ref[0, 0, pl.ds(r, rows), :].astype(BF16)
        return carry

    lax.fori_loop(0, dst_ref.shape[0] // rows, chunk, 0)


def _expert_mlp(x, wgu_bf, bgu_ref, wd_bf, bd_ref):
    x_lo, x_hi = _unpack_rows(x)
    half = x_lo.shape[1]
    gu = (_dot(x_lo.astype(BF16), wgu_bf[0:half]) + _dot(x_hi.astype(BF16), wgu_bf[half:2 * half])
          + bgu_ref[0, 0])
    gate = jnp.minimum(gu[:, :D_FF], SWIGLU_LIMIT)
    lin = jnp.clip(gu[:, D_FF:], -SWIGLU_LIMIT, SWIGLU_LIMIT)
    act = gate * jax.nn.sigmoid(SWIGLU_ALPHA * gate) * (lin + 1.0)
    return _pack_rows(_dot(act.astype(BF16), wd_bf[...]) + bd_ref[0, 0])


def _expert_body(be_ref, rows_ref, x_ref, wgu_ref, bgu_ref, wd_ref, bd_ref, o_ref, wgu_bf, wd_bf):
    i = pl.program_id(0)
    rows = rows_ref[i]
    half_block = x_ref.shape[0] // 2
    new_expert = (i == 0) | (be_ref[i] != be_ref[jnp.maximum(i - 1, 0)])

    @pl.when((rows > 0) & new_expert)
    def _():
        _cast_rows(wgu_ref, wgu_bf, 128)
        _cast_rows(wd_ref, wd_bf, 128)

    @pl.when(rows > half_block)
    def _():
        o_ref[...] = _expert_mlp(x_ref[...], wgu_bf, bgu_ref, wd_bf, bd_ref)

    @pl.when((rows > 0) & (rows <= half_block))
    def _():
        o_ref[0:half_block, :] = _expert_mlp(x_ref[0:half_block, :], wgu_bf, bgu_ref, wd_bf, bd_ref)
        o_ref[half_block:, :] = jnp.zeros((half_block, o_ref.shape[1]), o_ref.dtype)

    @pl.when(rows == 0)
    def _():
        o_ref[...] = jnp.zeros_like(o_ref)


def _experts(xb, block_e, block_rows, layer, w_gu, b_gu, w_down, b_down):
    n_slots, half = xb.shape
    d = 2 * half
    n_blocks = n_slots // MOE_BLOCK
    depth = w_gu.shape[0]
    by_expert = lambda i, be, rows: (layer, be[i], 0, 0)
    return pl.pallas_call(
        _expert_body,
        grid_spec=pltpu.PrefetchScalarGridSpec(
            num_scalar_prefetch=2,
            grid=(n_blocks,),
            in_specs=[pl.BlockSpec((MOE_BLOCK, half), lambda i, be, rows: (i, 0)),
                      pl.BlockSpec((1, 1, d, 2 * D_FF), by_expert), pl.BlockSpec((1, 1, 1, 2 * D_FF), by_expert),
                      pl.BlockSpec((1, 1, D_FF, d), by_expert), pl.BlockSpec((1, 1, 1, d), by_expert)],
            out_specs=pl.BlockSpec((MOE_BLOCK, half), lambda i, be, rows: (i, 0)),
            scratch_shapes=[pltpu.VMEM((d, 2 * D_FF), BF16), pltpu.VMEM((D_FF, d), BF16)]),
        out_shape=jax.ShapeDtypeStruct((n_slots, half), U32),
        compiler_params=_params("arbitrary"),
        name="moe_experts",
    )(block_e, block_rows, xb, w_gu, b_gu.reshape(depth, N_EXPERTS, 1, 2 * D_FF), w_down,
      b_down.reshape(depth, N_EXPERTS, 1, d))


def _combine_body(x1_ref, mod_ref, yk_ref, w_ref, fg_ref, o_ref):
    d = x1_ref.shape[2]
    half = d // 2
    x2_lo, x2_hi = _moe_residual(x1_ref, mod_ref[0, 0][5:6], yk_ref, w_ref)
    ms = (jnp.sum(x2_lo * x2_lo, axis=-1, keepdims=True) + jnp.sum(x2_hi * x2_hi, axis=-1, keepdims=True)) / d
    r = lax.rsqrt(ms + EPS)
    o_ref[0, :, 0:half] = x2_lo * r * fg_ref[:, 0:half]
    o_ref[0, :, half:d] = x2_hi * r * fg_ref[:, half:d]


def _combine(x1, mods, yk, w_tok, seg_tile0, n_tok, final_g, tm):
    bsz, t, d = x1.shape
    loc = lambda b, i: (b, i, 0)
    return pl.pallas_call(
        _combine_body,
        grid=(bsz, t // tm),
        in_specs=[pl.BlockSpec((1, tm, d), loc),
                  pl.BlockSpec((1, 1, 6, d), lambda b, i: (b, ((i + seg_tile0) >= n_tok).astype(I32), 0, 0)),
                  pl.BlockSpec((TOP_K, 1, tm, d // 2), lambda b, i: (0, b, i, 0)),
                  pl.BlockSpec((1, tm, TOP_K), loc), _const_spec((1, d))],
        out_specs=pl.BlockSpec((1, tm, d), loc),
        out_shape=jax.ShapeDtypeStruct((bsz, t, d), F32),
        compiler_params=_params("arbitrary", "arbitrary"),
        name="moe_combine",
    )(x1, mods, yk.reshape(TOP_K, bsz, t, d // 2), w_tok.reshape(bsz, t, TOP_K), final_g.reshape(1, d))


def _sc_mesh():
    return plsc.VectorSubcoreMesh(core_axis_name="core", subcore_axis_name="subcore",
                                  num_cores=SC_CORES, num_subcores=SC_SUBCORES)


def _sc_worker_base(per_worker):
    return (lax.axis_index("subcore") * SC_CORES + lax.axis_index("core")) * per_worker


def _sc_dispatch(rows, dest, n_slots):
    n, w = rows.shape
    per_worker = n // SC_WORKERS
    assert per_worker * SC_WORKERS == n and per_worker % SC_CHUNK == 0

    @functools.partial(
        pl.kernel, mesh=_sc_mesh(), out_type=jax.ShapeDtypeStruct((n_slots, w), rows.dtype),
        scratch_types=[pltpu.VMEM((SC_CHUNK,), I32)] * TOP_K + [pltpu.VMEM((SC_CHUNK, w), rows.dtype),
                                                                pltpu.SemaphoreType.DMA],
        name="moe_dispatch")
    def scatter_rows(rows_hbm, dest_hbm, out_hbm, *scratch):
        idx_refs, buf, sem = scratch[:TOP_K], scratch[TOP_K], scratch[TOP_K + 1]
        base0 = _sc_worker_base(per_worker)

        @pl.loop(0, per_worker // SC_CHUNK)
        def _(j):
            base = base0 + j * SC_CHUNK
            pltpu.sync_copy(rows_hbm.at[pl.ds(base, SC_CHUNK)], buf)
            for k, idx in enumerate(idx_refs):
                pltpu.sync_copy(dest_hbm.at[pl.ds(k * n + base, SC_CHUNK)], idx)
            copies = [pltpu.make_async_copy(buf, out_hbm.at[idx], sem) for idx in idx_refs]
            for cp in copies:
                cp.start()
            for cp in copies:
                cp.wait()

    return scatter_rows(rows, dest)


def _sc_gather(table, idx):
    n = idx.shape[0]
    w = table.shape[1]
    per_worker = n // SC_WORKERS
    n_chunks = per_worker // SC_CHUNK
    assert per_worker * SC_WORKERS == n and n_chunks * SC_CHUNK == per_worker and n_chunks % 2 == 0

    @functools.partial(
        pl.kernel, mesh=_sc_mesh(), out_type=jax.ShapeDtypeStruct((n, w), table.dtype),
        scratch_types=([pltpu.VMEM((SC_CHUNK,), I32)] * 2 + [pltpu.VMEM((SC_CHUNK, w), table.dtype)] * 2
                       + [pltpu.SemaphoreType.DMA] * 4),
        name="moe_gather")
    def gather_rows(table_hbm, idx_hbm, out_hbm, idx0, idx1, buf0, buf1, gsem0, gsem1, wsem0, wsem1):
        base0 = _sc_worker_base(per_worker)

        def gather_copy(idx_v, buf, sem):
            return pltpu.make_async_copy(table_hbm.at[idx_v], buf, sem)

        def write_copy(j, buf, sem):
            return pltpu.make_async_copy(buf, out_hbm.at[pl.ds(base0 + j * SC_CHUNK, SC_CHUNK)], sem)

        def start_gather(j, idx_v, buf, sem):
            pltpu.sync_copy(idx_hbm.at[pl.ds(base0 + j * SC_CHUNK, SC_CHUNK)], idx_v)
            gather_copy(idx_v, buf, sem).start()

        start_gather(0, idx0, buf0, gsem0)

        @pl.loop(0, n_chunks, step=2)
        def _(j):
            @pl.when(j > 0)
            def _():
                write_copy(j - 1, buf1, wsem1).wait()
            start_gather(j + 1, idx1, buf1, gsem1)
            gather_copy(idx0, buf0, gsem0).wait()
            write_copy(j, buf0, wsem0).start()

            @pl.when(j + 2 < n_chunks)
            def _():
                write_copy(j, buf0, wsem0).wait()
                start_gather(j + 2, idx0, buf0, gsem0)
            gather_copy(idx1, buf1, gsem1).wait()
            write_copy(j + 1, buf1, wsem1).start()

        write_copy(n_chunks - 2, buf0, wsem0).wait()
        write_copy(n_chunks - 1, buf1, wsem1).wait()

    return gather_rows(table, idx)


def _moe(h2, e_tl, w_tl, r_tl, cnt, layer, w_gu, b_gu, w_down, b_down, tb=1):
    bsz, t, half = h2.shape
    n = bsz * t
    flat = lambda a: a.reshape(bsz // tb, -1, TOP_K, tb, a.shape[2] // tb).transpose(2, 0, 3, 1, 4).reshape(TOP_K, n)
    e_k, w_k, r_k = flat(e_tl), flat(w_tl), flat(r_tl)
    counts = cnt[:, 0].astype(I32)
    padded = (counts + MOE_BLOCK - 1) // MOE_BLOCK * MOE_BLOCK
    pad_end = jnp.cumsum(padded)
    pad_start = pad_end - padded
    n_blocks = (n * TOP_K + MOE_BLOCK - 1) // MOE_BLOCK + N_EXPERTS
    block_start = jnp.arange(n_blocks, dtype=I32) * MOE_BLOCK
    block_e = jnp.minimum(jnp.sum((pad_end[None, :] <= block_start[:, None]).astype(I32), axis=1), N_EXPERTS - 1)
    block_rows = jnp.clip(counts[block_e] - (block_start - pad_start[block_e]), 0, MOE_BLOCK).astype(I32)
    start_k = jnp.sum(jnp.where(e_k[..., None] == jnp.arange(N_EXPERTS, dtype=I32), pad_start, 0), axis=-1)
    dest = (start_k + r_k).reshape(TOP_K * n)
    xb = _sc_dispatch(h2.reshape(n, half), dest, n_blocks * MOE_BLOCK)
    yb = _experts(xb, block_e, block_rows, layer, w_gu, b_gu, w_down, b_down)
    return _sc_gather(yb, dest), w_k.T


def _rope_tables(n_ctx, n_lat):
    n_freq = RET_DK // 4
    inv_freq = ROPE_BASE ** (-jnp.arange(n_freq, dtype=F32) / n_freq)
    pos = jnp.arange(n_lat, dtype=I32)
    cos, sin = [], []
    for p in (pos // GRID_W, pos % GRID_W):
        ang = p.astype(F32)[:, None] * inv_freq
        cos += [jnp.cos(ang), jnp.cos(ang)]
        sin += [-jnp.sin(ang), jnp.sin(ang)]
    cos, sin = jnp.concatenate(cos, axis=1), jnp.concatenate(sin, axis=1)
    return (jnp.concatenate([jnp.ones((n_ctx, RET_DK), F32), cos], axis=0),
            jnp.concatenate([jnp.zeros((n_ctx, RET_DK), F32), sin], axis=0))


def kernel(x, c, ctx, c_ctx, ada_w, ada_b, norm1_g, norm2_g, ab_w_in, ab_w_out, gla_wa, gla_ba, gla_norm_g, s5_lam_re, s5_lam_im, s5_log_step, s5_b_re, s5_b_im, s5_c_re, s5_c_im, s5_d, s5_glu_w, s5_glu_b, ret_w_in, ret_w_out, ret_decay_logit, ret_norm_g, moe_w_router, moe_b_router, moe_w_gu, moe_b_gu, moe_w_down, moe_b_down, final_norm_g):
    bsz, n_lat, d = x.shape
    n_ctx = ctx.shape[1]
    depth = ada_w.shape[0]
    assert depth == 2 and d == D_MODEL and bsz == 8, "kernels are laid out for the stated problem shape"
    assert n_ctx % TOKEN_TILE == 0 and n_lat % LATENT_TILE == 0 and n_lat % GRID_W == 0
    t = n_ctx + n_lat
    nct = n_ctx // TOKEN_TILE

    cvec = jnp.zeros((16, d), F32).at[:bsz].set(c).at[bsz].set(c_ctx)
    mod = _ada_mod(cvec, ada_w, ada_b).reshape(depth, 16, 6, d)
    mods = [jnp.stack([jnp.broadcast_to(mod[l, bsz], (bsz, 6, d)), mod[l, :bsz]], axis=1) for l in range(depth)]

    w_in = ab_w_in[0].astype(BF16)
    cuts = [0, AB_QK, 2 * AB_QK, 2 * AB_QK + AB_V, 2 * AB_QK + 2 * AB_V, 2 * AB_QK + 2 * AB_V + 2 * GLA_RANK,
            w_in.shape[1]]
    pieces = [w_in[:, a:b] for a, b in zip(cuts[:-1], cuts[1:])]
    wa_pad = jnp.zeros((2, 2 * GLA_RANK, AB_QK), F32)
    wa_pad = wa_pad.at[0, :GLA_RANK].set(gla_wa[0, 0]).at[1, GLA_RANK:].set(gla_wa[0, 1])
    outs = _inproj0(ctx, x, mods[0], norm1_g[0], pieces, wa_pad, gla_ba[0].reshape(2, 1, AB_QK), nct)
    v, g, u = outs[8:]
    o_f, o_b = _gla((outs[0:4], outs[4:8]), v, n_ctx)
    ops = _s5_operators(s5_lam_re[0], s5_lam_im[0], s5_log_step[0], s5_b_re[0], s5_b_im[0], s5_c_re[0], s5_c_im[0])
    ys = _s5(u, ops, n_ctx)
    consts = [jnp.tile(gla_norm_g[0], GLA_HEADS).reshape(1, AB_V), s5_d[0].reshape(1, S5_CH),
              s5_glu_w[0].astype(BF16), s5_glu_b[0].reshape(1, S5_CH), ab_w_out[0].astype(BF16)]
    x1, h2, e_tl, w_tl, r_tl, cnt = _mix_call(
        functools.partial(_mix0_body, nct), "mix_gla_s5", (ctx, x), mods[0], t // TOKEN_TILE,
        [(o_f, False), (o_b, False), (g, False), (ys, False), (u, False)], consts,
        norm2_g[0], moe_w_router[0], moe_b_router[0], nct, 0, tb=TOKEN_TILE_BATCH)
    yk, w_tok = _moe(h2, e_tl, w_tl, r_tl, cnt, 0, moe_w_gu, moe_b_gu, moe_w_down, moe_b_down, tb=TOKEN_TILE_BATCH)

    w_in = ret_w_in[0].astype(BF16)
    cuts = [0, RET_QK, 2 * RET_QK, 2 * RET_QK + RET_MIX, w_in.shape[1]]
    pieces = [w_in[:, a:b] for a, b in zip(cuts[:-1], cuts[1:])]
    cos_t, sin_t = _rope_tables(n_ctx, n_lat)
    x2, q, k, v, g = _inproj1(x1, mods[0], yk, w_tok, mods[1], norm1_g[1], cos_t, sin_t, pieces, nct)
    o_f, o_b = _retention(q, k, v, ret_decay_logit[0], n_ctx)
    consts = [ret_norm_g[0].reshape(1, RET_MIX), ret_w_out[0].astype(BF16)]
    x1, h2, e_tl, w_tl, r_tl, cnt = _mix_call(
        _mix1_body, "mix_retention", (x2,), mods[1], n_lat // LATENT_TILE,
        [(o_f, False), (o_b, False), (g, False)], consts,
        norm2_g[1], moe_w_router[1], moe_b_router[1], 0, 0, tm=LATENT_TILE)
    yk, w_tok = _moe(h2, e_tl, w_tl, r_tl, cnt, 1, moe_w_gu, moe_b_gu, moe_w_down, moe_b_down)
    return _combine(x1, mods[1], yk, w_tok, 0, 0, final_norm_g, LATENT_TILE)
```

```python
import functools
import math

import jax
import jax.numpy as jnp
from jax import lax
from jax.experimental import pallas as pl
from jax.experimental.pallas import tpu as pltpu
from jax.experimental.pallas import tpu_sc as plsc

F32, BF16, I32, U32 = jnp.float32, jnp.bfloat16, jnp.int32, jnp.uint32

D_MODEL = 1024
GRID_W = 64
EPS = 1e-6
GLA_HEADS, GLA_DK, GLA_DV, GLA_RANK, GLA_TAU, GLA_CHUNK = 4, 64, 128, 16, 16.0, 64
GLA_BATCH = 8
AB_QK, AB_V = GLA_HEADS * GLA_DK, GLA_HEADS * GLA_DV
S5_CH, S5_GROUP, S5_GROUPS, S5_P = 512, 16, 32, 64
S5_CHUNK = 16
S5_FOLD_BATCH = 4
S5_SCAN_GROUPS = 2
RET_HEADS, RET_DK, RET_DV = 4, 256, 512
RET_CHUNK = 256
RET_BATCH = 2
RET_QK, RET_MIX = RET_HEADS * RET_DK, RET_HEADS * RET_DV
ROPE_BASE = 10000.0
N_EXPERTS, TOP_K, D_FF = 32, 4, 1024
SWIGLU_LIMIT, SWIGLU_ALPHA = 7.0, 1.702
MOE_BLOCK = 1024
EXPERT_BLOCK_PARTS = 4
TOKEN_TILE = 256
LATENT_TILE = 512
TOKEN_TILE_BATCH = 2
ADA_TILE = 768
VMEM_LIMIT = 56 * 1024 * 1024
SC_CORES, SC_SUBCORES = 2, 16
SC_WORKERS = SC_CORES * SC_SUBCORES
SC_CHUNK = 64
LANES = 128

def _params(*sem):
    return pltpu.CompilerParams(dimension_semantics=sem, vmem_limit_bytes=VMEM_LIMIT)


def _dot(a, b):
    return jnp.dot(a, b, preferred_element_type=F32)


def _dot_nt(a, b):
    return lax.dot_general(a, b, (((1,), (1,)), ((), ())), preferred_element_type=F32)


def _dot_tn(a, b):
    return lax.dot_general(a, b, (((0,), (0,)), ((), ())), preferred_element_type=F32)


def _split(a):
    hi = a.astype(BF16)
    return hi, (a - hi.astype(F32)).astype(BF16)


def _dot3(a, b, dot=_dot):
    ah, al = _split(a)
    bh, bl = _split(b)
    return dot(ah, bh) + (dot(ah, bl) + dot(al, bh))


def _pack_rows(x):
    h = x.shape[1] // 2
    lo = lax.bitcast_convert_type(x[:, 0:h].astype(BF16).astype(F32), U32)
    hi = lax.bitcast_convert_type(x[:, h:2 * h].astype(BF16).astype(F32), U32)
    return hi | (lo >> 16)


def _unpack_rows(p):
    lo = lax.bitcast_convert_type(p << 16, F32)
    hi = lax.bitcast_convert_type(p & jnp.uint32(0xFFFF0000), F32)
    return lo, hi


def _silu(x):
    return x * jax.nn.sigmoid(x)


def _norm_mod(x, g, shift, scale):
    r = lax.rsqrt(jnp.mean(x * x, axis=-1, keepdims=True) + EPS)
    return (x * r * g) * (1.0 + scale) + shift


def _const_spec(shape):
    nd = len(shape)
    return pl.BlockSpec(shape, lambda *_: (0,) * nd, pipeline_mode=pl.Buffered(1))


def _ada_body(c_ref, w_ref, b_ref, o_ref):
    o_ref[0] = _dot3(_silu(c_ref[...]), w_ref[0]) + b_ref[0]


def _ada_mod(cvec, ada_w, ada_b):
    depth, d, n6 = ada_w.shape
    rows = cvec.shape[0]
    return pl.pallas_call(
        _ada_body,
        grid=(depth, n6 // ADA_TILE),
        in_specs=[_const_spec((rows, d)),
                  pl.BlockSpec((1, d, ADA_TILE), lambda l, j: (l, 0, j)),
                  pl.BlockSpec((1, 1, ADA_TILE), lambda l, j: (l, 0, j))],
        out_specs=pl.BlockSpec((1, rows, ADA_TILE), lambda l, j: (l, 0, j)),
        out_shape=jax.ShapeDtypeStruct((depth, rows, n6), F32),
        compiler_params=_params("arbitrary", "arbitrary"),
        name="ada_mod",
    )(cvec, ada_w, ada_b.reshape(depth, 1, n6))


def _split_specs(nct, d, tb=1):
    ctx_spec = pl.BlockSpec((tb, TOKEN_TILE, d), lambda b, i: (b, jnp.minimum(i, nct - 1), 0))
    lat_spec = pl.BlockSpec((tb, TOKEN_TILE, d), lambda b, i: (b, jnp.maximum(i - nct, 0), 0))
    return ctx_spec, lat_spec


def _stream_tile(nct, ctx_ref, lat_ref, bb=0):
    return jnp.where(pl.program_id(1) < nct, ctx_ref[bb], lat_ref[bb])


def _normed_rows(nct, ctx_ref, lat_ref, mod_ref, g_ref):
    tiles = []
    for bb in range(mod_ref.shape[0]):
        m = mod_ref[bb, 0]
        tiles.append(_norm_mod(_stream_tile(nct, ctx_ref, lat_ref, bb), g_ref[...], m[0:1], m[1:2]).astype(BF16))
    return tiles[0] if len(tiles) == 1 else jnp.concatenate(tiles, axis=0)


def _store_rows(o_ref, val):
    tm = o_ref.shape[1]
    for bb in range(o_ref.shape[0]):
        o_ref[bb] = val[bb * tm:(bb + 1) * tm].astype(o_ref.dtype)


def _inproj0_body(nct, ctx_ref, lat_ref, mod_ref, g_ref, wq, wk, wv, wg, wlow, wu, wa_ref, ba_ref, tri_ref, ones_ref,
                  qd_f, ki_f, ks_f, ed_f, qd_b, ki_b, ks_b, ed_b, ov, og, ou):
    h = _normed_rows(nct, ctx_ref, lat_ref, mod_ref, g_ref)
    tm = ctx_ref.shape[1]
    low = _dot(h, wlow[...])
    q = _dot(h, wq[...]) * (GLA_DK ** -0.5)
    k = _dot(h, wk[...])
    outs = ((qd_f, ki_f, ks_f, ed_f), (qd_b, ki_b, ks_b, ed_b))
    for d, (qd_ref, ki_ref, ks_ref, ed_ref) in enumerate(outs):
        z = _dot3(low, wa_ref[d]) + ba_ref[d]
        log_a = (jnp.minimum(z, 0.0) - jnp.log1p(jnp.exp(-jnp.abs(z)))) * (1.0 / GLA_TAU)
        la_hi, la_lo = _split(log_a)
        cums, tots = [], []
        for bb in range(qd_ref.shape[0]):
            hi, lo = la_hi[bb * tm:(bb + 1) * tm], la_lo[bb * tm:(bb + 1) * tm]
            cums.append(_dot(tri_ref[d], hi) + _dot(tri_ref[d], lo))
            tots.append(_dot(ones_ref[...], hi) + _dot(ones_ref[...], lo))
        cum = cums[0] if len(cums) == 1 else jnp.concatenate(cums, axis=0)
        tot = tots[0] if len(tots) == 1 else jnp.concatenate(tots, axis=0)
        _store_rows(qd_ref, q * jnp.exp(cum))
        _store_rows(ki_ref, k * jnp.exp(-cum))
        _store_rows(ks_ref, k * jnp.exp(tot - cum))
        for bb in range(ed_ref.shape[0]):
            for ch in range(tm // GLA_CHUNK):
                row = bb * tm + ch * GLA_CHUNK
                ed_ref[bb, ch] = jnp.exp(tot[row:row + 1])
    _store_rows(ov, _dot(h, wv[...]))
    _store_rows(og, _dot(h, wg[...]))
    u = _dot(h, wu[...])
    for qb in range(ou.shape[0]):
        _store_rows(ou.at[qb], u[:, qb * LANES:(qb + 1) * LANES])


def _inproj0(ctx, x, mods, norm_g, weights, wa_pad, ba, nct):
    bsz, n_lat, d = x.shape
    t = ctx.shape[1] + n_lat
    tm, tb = TOKEN_TILE, TOKEN_TILE_BATCH
    mod_spec = pl.BlockSpec((tb, 1, 6, d), lambda b, i: (b, (i >= nct).astype(I32), 0, 0))
    ctx_spec, lat_spec = _split_specs(nct, d, tb)
    pos = jnp.arange(tm)
    same_chunk = (pos[:, None] // GLA_CHUNK) == (pos[None, :] // GLA_CHUNK)
    tri = jnp.stack([same_chunk & (pos[None, :] <= pos[:, None]),
                     same_chunk & (pos[None, :] >= pos[:, None])]).astype(BF16)
    ones = same_chunk.astype(BF16)
    consts = list(weights) + [wa_pad, ba, tri, ones]
    tok = lambda w, dt: (pl.BlockSpec((tb, tm, w), lambda b, i: (b, i, 0)), jax.ShapeDtypeStruct((bsz, t, w), dt))
    per_chunk = (pl.BlockSpec((tb, tm // GLA_CHUNK, 1, AB_QK), lambda b, i: (b, i, 0, 0)),
                 jax.ShapeDtypeStruct((bsz, t // GLA_CHUNK, 1, AB_QK), F32))
    one_dir = [tok(AB_QK, BF16)] * 3 + [per_chunk]
    u_blocks = (pl.BlockSpec((S5_CH // LANES, tb, tm, LANES), lambda b, i: (0, b, i, 0)),
                jax.ShapeDtypeStruct((S5_CH // LANES, bsz, t, LANES), F32))
    outs = one_dir + one_dir + [tok(AB_V, BF16), tok(AB_V, BF16), u_blocks]
    return pl.pallas_call(
        functools.partial(_inproj0_body, nct),
        grid=(bsz // tb, t // tm),
        in_specs=[ctx_spec, lat_spec, mod_spec, _const_spec((1, d))] + [_const_spec(a.shape) for a in consts],
        out_specs=[o[0] for o in outs],
        out_shape=[o[1] for o in outs],
        compiler_params=_params("arbitrary", "arbitrary"),
        name="inproj_gla_s5",
    )(ctx, x, mods, norm_g.reshape(1, d), *consts)


def _rope(acc, cos_ref, sin_ref, o_ref, scale):
    tm = o_ref.shape[1]
    for grp in range(acc.shape[1] // LANES):
        half = grp % 2
        cs = cos_ref[:, half * LANES:(half + 1) * LANES]
        sn = sin_ref[:, half * LANES:(half + 1) * LANES]
        for bb in range(o_ref.shape[0]):
            xg = acc[bb * tm:(bb + 1) * tm, grp * LANES:(grp + 1) * LANES]
            out = xg * cs + pltpu.roll(xg, LANES // 2, 1) * sn
            o_ref[bb, :, grp * LANES:(grp + 1) * LANES] = (out * scale).astype(o_ref.dtype)


def _moe_residual(x1_ref, g2, yk_ref, w_ref, bb=0):
    d = x1_ref.shape[2]
    half = d // 2
    y_lo, y_hi = None, None
    for k in range(TOP_K):
        lo, hi = _unpack_rows(yk_ref[k, bb])
        wk = w_ref[bb, :, k:k + 1]
        y_lo = lo * wk if y_lo is None else y_lo + lo * wk
        y_hi = hi * wk if y_hi is None else y_hi + hi * wk
    return x1_ref[bb, :, 0:half] + g2[:, 0:half] * y_lo, x1_ref[bb, :, half:d] + g2[:, half:d] * y_hi


def _inproj1_body(x1_ref, mod0_ref, yk_ref, w_ref, mod_ref, g_ref, cos_ref, sin_ref, wq, wk, wv, wg, ox, oq, ok, ov, og):
    half = x1_ref.shape[2] // 2
    for bb in range(x1_ref.shape[0]):
        x2_lo, x2_hi = _moe_residual(x1_ref, mod0_ref[bb, 0][5:6], yk_ref, w_ref, bb)
        ox[bb, :, 0:half] = x2_lo
        ox[bb, :, half:2 * half] = x2_hi
        m = mod_ref[bb, 0]
        h = _norm_mod(jnp.concatenate([x2_lo, x2_hi], axis=1), g_ref[...], m[0:1], m[1:2]).astype(BF16)
        _rope(_dot(h, wq[...]), cos_ref, sin_ref, oq.at[bb:bb + 1], 1.0)
        _rope(_dot(h, wk[...]), cos_ref, sin_ref, ok.at[bb:bb + 1], RET_DK ** -0.5)
        ov[bb] = _dot(h, wv[...]).astype(ov.dtype)
        og[bb] = _dot(h, wg[...]).astype(og.dtype)


def _inproj1(x1, mods0, yk, w_tok, mods, norm_g, cos_t, sin_t, weights, nct):
    bsz, t, d = x1.shape
    tm, tb = TOKEN_TILE, TOKEN_TILE_BATCH
    mod_spec = pl.BlockSpec((tb, 1, 6, d), lambda b, i: (b, (i >= nct).astype(I32), 0, 0))
    tok = lambda w: pl.BlockSpec((tb, tm, w), lambda b, i: (b, i, 0))
    lat = lambda w: pl.BlockSpec((tb, tm, w), lambda b, i: (b, jnp.maximum(i - nct, 0), 0))
    tab_spec = pl.BlockSpec((tm, RET_DK), lambda b, i: (i, 0))
    wq, wk, wv, wg = weights
    n_lat = t - nct * tm
    return pl.pallas_call(
        _inproj1_body,
        grid=(bsz // tb, t // tm),
        in_specs=[tok(d), mod_spec, pl.BlockSpec((TOP_K, tb, tm, d // 2), lambda b, i: (0, b, i, 0)), tok(TOP_K),
                  mod_spec, _const_spec((1, d)), tab_spec, tab_spec] + [_const_spec(w.shape) for w in weights],
        out_specs=[lat(d), tok(wq.shape[1]), tok(wk.shape[1]), tok(wv.shape[1]), lat(wg.shape[1])],
        out_shape=[jax.ShapeDtypeStruct((bsz, n_lat, d), F32), jax.ShapeDtypeStruct((bsz, t, wq.shape[1]), BF16),
                   jax.ShapeDtypeStruct((bsz, t, wk.shape[1]), BF16), jax.ShapeDtypeStruct((bsz, t, wv.shape[1]), BF16),
                   jax.ShapeDtypeStruct((bsz, n_lat, wg.shape[1]), BF16)],
        compiler_params=_params("arbitrary", "arbitrary"),
        name="inproj_retention",
    )(x1, mods0, yk.reshape(TOP_K, bsz, t, d // 2), w_tok.reshape(bsz, t, TOP_K), mods, norm_g.reshape(1, d),
      cos_t, sin_t, *weights)


def _backward_chunk(n, n_ctx_chunks, n_chunks):
    return jnp.where(n < n_ctx_chunks, n_ctx_chunks - 1 - n, n_chunks - 1 - (n - n_ctx_chunks))


def _gla_body(qd_f, ki_f, ks_f, ed_f, v_f, qd_b, ki_b, ks_b, ed_b, v_b, hmask_ref, bdmask_ref, o_f, o_b, st_f, st_b):
    c = GLA_CHUNK

    @pl.when(pl.program_id(1) == 0)
    def _():
        st_f[...] = jnp.zeros_like(st_f)
        st_b[...] = jnp.zeros_like(st_b)

    r4 = lax.broadcasted_iota(I32, (GLA_HEADS * c, c), 0) & (c - 1)
    c4 = lax.broadcasted_iota(I32, (GLA_HEADS * c, c), 1)
    dirs = ((qd_f, ki_f, ks_f, ed_f, v_f, o_f, st_f), (qd_b, ki_b, ks_b, ed_b, v_b, o_b, st_b))
    chains = [(bb, d) + dirs[d] for bb in range(qd_f.shape[0]) for d in range(2)]
    scores, inter, grow = [], [], []
    for bb, d, qd_ref, ki_ref, ks_ref, ed_ref, v_ref, o_ref, st_ref in chains:
        q_dec = qd_ref[bb]
        q_heads = jnp.concatenate([q_dec] * GLA_HEADS, axis=0) * hmask_ref[...]
        seen4 = (c4 <= r4) if d == 0 else (c4 >= r4)
        scores.append(jnp.where(seen4, _dot_nt(q_heads, ki_ref[bb]), 0.0).astype(BF16))
        inter.append(_dot_nt(q_dec, st_ref[bb].astype(BF16)))
        grow.append(_dot_tn(v_ref[bb], ks_ref[bb]))
    for (bb, d, qd_ref, ki_ref, ks_ref, ed_ref, v_ref, o_ref, st_ref), sc, o_inter, dst in zip(chains, scores, inter, grow):
        v = v_ref[bb]
        o_intra = jnp.concatenate(
            [_dot(sc[h * c:(h + 1) * c], v[:, h * GLA_DV:(h + 1) * GLA_DV]) for h in range(GLA_HEADS)], axis=1)
        o_ref[bb] = o_intra + o_inter
        st_ref[bb] = st_ref[bb] * ed_ref[bb, 0] + bdmask_ref[...] * dst


def _gla(per_dir, v, n_ctx):
    bsz, t, _ = v.shape
    nc, ncc = t // GLA_CHUNK, n_ctx // GLA_CHUNK
    gb = GLA_BATCH
    fwd = lambda b, n: (b, n, 0)
    bwd = lambda b, n: (b, _backward_chunk(n, ncc, nc), 0)
    hmask = (jnp.arange(AB_QK)[:, None] // GLA_CHUNK == jnp.arange(AB_QK)[None, :] // GLA_DK).astype(BF16)
    bdmask = (jnp.arange(AB_V)[:, None] // GLA_DV == jnp.arange(AB_QK)[None, :] // GLA_DK).astype(F32)

    def specs(idx):
        idx4 = lambda b, n: idx(b, n) + (0,)
        return [pl.BlockSpec((gb, GLA_CHUNK, AB_QK), idx)] * 3 + [pl.BlockSpec((gb, 1, 1, AB_QK), idx4),
                                                                  pl.BlockSpec((gb, GLA_CHUNK, AB_V), idx)]

    return pl.pallas_call(
        _gla_body,
        grid=(bsz // gb, nc),
        in_specs=specs(fwd) + specs(bwd) + [_const_spec(hmask.shape), _const_spec(bdmask.shape)],
        out_specs=[pl.BlockSpec((gb, GLA_CHUNK, AB_V), fwd), pl.BlockSpec((gb, GLA_CHUNK, AB_V), bwd)],
        out_shape=[jax.ShapeDtypeStruct((bsz, t, AB_V), F32)] * 2,
        scratch_shapes=[pltpu.VMEM((gb, AB_V, AB_QK), F32)] * 2,
        compiler_params=_params("arbitrary", "arbitrary"),
        name="gla_scan",
    )(*per_dir[0], v, *per_dir[1], v, hmask, bdmask)


def _cmul(x, y):
    return x[0] * y[0] - x[1] * y[1], x[0] * y[1] + x[1] * y[0]


def _s5_operators(lam_re, lam_im, log_step, b_re, b_im, c_re, c_im):
    ln = S5_CHUNK
    step = jnp.exp(log_step.astype(F32))[..., None]
    lam_re, lam_im = lam_re.astype(F32), lam_im.astype(F32)
    mag = jnp.exp(lam_re * step)
    a = (mag * jnp.cos(lam_im * step), mag * jnp.sin(lam_im * step))
    den = lam_re * lam_re + lam_im * lam_im
    f_re = ((a[0] - 1.0) * lam_re + a[1] * lam_im) / den
    f_im = (a[1] * lam_re - (a[0] - 1.0) * lam_im) / den
    bt_re, bt_im = b_re.transpose(0, 2, 1), b_im.transpose(0, 2, 1)
    bb = _cmul((f_re[:, :, None, :], f_im[:, :, None, :]), (bt_re, bt_im))
    bbt = jnp.concatenate([bb[0], -bb[1]], axis=-1)
    pw = (a[0][:, :, None, :], a[1][:, :, None, :])
    while pw[0].shape[2] < ln:
        top = (pw[0][:, :, -1:, :], pw[1][:, :, -1:, :])
        nxt = _cmul(top, pw)
        pw = (jnp.concatenate([pw[0], nxt[0]], axis=2), jnp.concatenate([pw[1], nxt[1]], axis=2))
    pw = (jnp.concatenate([jnp.ones_like(pw[0][:, :, :1]), pw[0]], axis=2),
          jnp.concatenate([jnp.zeros_like(pw[1][:, :, :1]), pw[1]], axis=2))
    ca = _cmul((c_re[:, :, None], c_im[:, :, None]), (pw[0][:, :, :, None, :], pw[1][:, :, :, None, :]))
    by_dir = lambda arr, lo, flip_d: jnp.stack([jnp.flip(arr[d, :, lo:lo + ln], axis=1) if d == flip_d
                                                else arr[d, :, lo:lo + ln] for d in range(2)])
    rows = lambda arr: arr.reshape(2, S5_GROUPS, ln * S5_GROUP, 2 * S5_P)
    cab = rows(by_dir(jnp.concatenate([ca[0], ca[1]], axis=-1), 0, 1))
    cab2 = rows(by_dir(jnp.concatenate([ca[0], -ca[1]], axis=-1), 1, 1)).astype(BF16)
    pwx = by_dir(jnp.concatenate([pw[0], pw[1]], axis=-1), 0, 0)
    lr, li = pw[0][:, :, ln], pw[1][:, :, ln]
    ac_rows = [jnp.concatenate([lr, lr], -1), jnp.concatenate([-li, li], -1), jnp.concatenate([li, -li], -1)]
    ac = jnp.stack(ac_rows + [jnp.zeros_like(ac_rows[0])] * 5, axis=2)
    return cab, cab2, bbt, pwx, ac


def _s5_group_operators(gg, cab_ref, bbt_ref, pwx_ref, tz, wx):
    ln, ch, p = S5_CHUNK, S5_GROUP, S5_P
    lane = lax.broadcasted_iota(I32, (ch, ln * ch), 1)
    for d in range(2):
        kern = _dot3(bbt_ref[d, gg], cab_ref[d, gg], dot=_dot_nt)
        bt = bbt_ref[d, gg]
        b_re, b_im = bt[:, 0:p], -bt[:, p:2 * p]
        for j in range(ln):
            if d == 0:
                blk = jnp.where(lane >= j * ch, kern if j == 0 else pltpu.roll(kern, j * ch, 1), 0.0)
            else:
                blk = jnp.where(lane < (j + 1) * ch, kern if j == ln - 1 else pltpu.roll(kern, (j + 1) * ch, 1), 0.0)
            tz[gg, d, j * ch:(j + 1) * ch, :] = blk.astype(BF16)
            pr, pi = pwx_ref[d, gg, j:j + 1, 0:p], pwx_ref[d, gg, j:j + 1, p:2 * p]
            x_re, x_im = pr * b_re - pi * b_im, pr * b_im + pi * b_re
            wx[gg, d, j * ch:(j + 1) * ch, :] = jnp.concatenate([x_re, x_im, x_im, x_re], axis=1).astype(BF16)


def _s5_placement(pall):
    rows, cols = pall.shape[1], pall.shape[2]
    row = lax.broadcasted_iota(I32, (rows, cols), 0)
    col = lax.broadcasted_iota(I32, (rows, cols), 1)
    same_token = (row // LANES) == (col // S5_GROUP)
    for g8 in range(pall.shape[0]):
        pall[g8] = jnp.where(same_token & ((row % LANES) == g8 * S5_GROUP + (col % S5_GROUP)), 1.0, 0.0).astype(BF16)


def _first_step():
    return (pl.program_id(0) == 0) & (pl.program_id(1) == 0)


def _s5_fold_body(ncs, u_ref, o_ref, pall, ucat):
    @pl.when(_first_step())
    def _():
        _s5_placement(pall)

    for b in range(u_ref.shape[1]):
        for j in range(S5_CHUNK):
            ucat[b * ncs:(b + 1) * ncs, j * LANES:(j + 1) * LANES] = u_ref[0, b, pl.ds(j, ncs, stride=S5_CHUNK), :].astype(BF16)
    for g8 in range(pall.shape[0]):
        o_ref[g8] = _dot(ucat[...], pall[g8]).astype(BF16)


def _s5_unfold_body(ncs, y_ref, o_ref, pall):
    @pl.when(_first_step())
    def _():
        _s5_placement(pall)

    def token_pair(i2, carry):
        r0 = pl.multiple_of(i2 * 2 * LANES, 2 * LANES)
        acc = _dot_nt(y_ref[0], pall[0, pl.ds(r0, 2 * LANES), :])
        for g8 in range(1, pall.shape[0]):
            acc = acc + _dot_nt(y_ref[g8], pall[g8, pl.ds(r0, 2 * LANES), :])
        for b in range(o_ref.shape[1]):
            for par in range(2):
                o_ref[0, b, pl.ds(2 * i2 + par, ncs, stride=S5_CHUNK), :] = (
                    acc[b * ncs:(b + 1) * ncs, par * LANES:(par + 1) * LANES])
        return carry

    lax.fori_loop(0, S5_CHUNK // 2, token_pair, 0)


def _s5_body(ncs_ctx, ncs, rows, u_ref, cab_ref, cab2_ref, bbt_ref, pwx_ref, ac_ref, y_ref, tz, wx, *vecs):
    half = 2 * S5_P
    n_groups = u_ref.shape[0]
    groups = [vecs[6 * gg:6 * gg + 6] for gg in range(n_groups)]
    for gg, (xx_f, xs_f, xx_b, xs_b, _, _) in enumerate(groups):
        _s5_group_operators(gg, cab_ref, bbt_ref, pwx_ref, tz, wx)
        for d, (xx, xs) in enumerate(((xx_f, xs_f), (xx_b, xs_b))):
            r = _dot(u_ref[gg], wx[gg, d])
            xx[...] = r[:, :half]
            xs[...] = r[:, half:]

    def advance(ac, s, s_sw, x, x_sw):
        return ac[0:1] * s + ac[1:2] * s_sw + x, ac[0:1] * s_sw + ac[2:3] * s + x_sw

    def step(n, carry):
        at_f = pl.ds(n, rows, stride=ncs)
        at_b = pl.ds(_backward_chunk(n, ncs_ctx, ncs), rows, stride=ncs)
        out = []
        for gg, (xx_f, xs_f, xx_b, xs_b, sin_f, sin_b) in enumerate(groups):
            s_f, sw_f, s_b, sw_b = carry[4 * gg:4 * gg + 4]
            sin_f[at_f, :] = s_f
            sin_b[at_b, :] = s_b
            out += advance(ac_ref[0, gg], s_f, sw_f, xx_f[at_f, :], xs_f[at_f, :])
            out += advance(ac_ref[1, gg], s_b, sw_b, xx_b[at_b, :], xs_b[at_b, :])
        return tuple(out)

    zero = jnp.zeros((rows, half), F32)
    lax.fori_loop(0, ncs, step, (zero,) * (4 * n_groups))
    for gg, (_, _, _, _, sin_f, sin_b) in enumerate(groups):
        u = u_ref[gg]
        y_ref[gg] = (_dot(u, tz[gg, 0]) + _dot(u, tz[gg, 1]) + _dot_nt(sin_f[...].astype(BF16), cab2_ref[0, gg])
                     + _dot_nt(sin_b[...].astype(BF16), cab2_ref[1, gg])).astype(BF16)


def _s5(u4, ops, n_ctx):
    nq, bsz, t, _ = u4.shape
    ln, lanes = S5_CHUNK, S5_CHUNK * S5_GROUP
    gq = S5_GROUPS // nq
    ncs, ncs_ctx = t // ln, n_ctx // ln
    m = ncs * bsz
    hb = S5_FOLD_BATCH
    tok_spec = pl.BlockSpec((1, hb, t, LANES), lambda q, h: (q, h, 0, 0))
    grp_spec = pl.BlockSpec((gq, hb * ncs, lanes), lambda q, h: (q, h, 0))
    pall = pltpu.VMEM((gq, ln * LANES, lanes), BF16)
    ug = pl.pallas_call(
        functools.partial(_s5_fold_body, ncs),
        grid=(nq, bsz // hb),
        in_specs=[tok_spec],
        out_specs=grp_spec,
        out_shape=jax.ShapeDtypeStruct((S5_GROUPS, m, lanes), BF16),
        scratch_shapes=[pall, pltpu.VMEM((hb * ncs, ln * LANES), BF16)],
        compiler_params=_params("arbitrary", "arbitrary"),
        name="s5_fold",
    )(u4)
    sg = S5_SCAN_GROUPS
    dir_spec = lambda arr: pl.BlockSpec((2, sg) + arr.shape[2:], lambda g: (0, g, 0, 0))
    yg = pl.pallas_call(
        functools.partial(_s5_body, ncs_ctx, ncs, bsz),
        grid=(S5_GROUPS // sg,),
        in_specs=[pl.BlockSpec((sg, m, lanes), lambda g: (g, 0, 0))] + [dir_spec(arr) for arr in ops],
        out_specs=pl.BlockSpec((sg, m, lanes), lambda g: (g, 0, 0)),
        out_shape=jax.ShapeDtypeStruct((S5_GROUPS, m, lanes), BF16),
        scratch_shapes=[pltpu.VMEM((sg, 2, lanes, lanes), BF16)] * 2 + [pltpu.VMEM((m, 2 * S5_P), F32)] * (6 * sg),
        compiler_params=_params("arbitrary"),
        name="s5_scan",
    )(ug, *ops)
    return pl.pallas_call(
        functools.partial(_s5_unfold_body, ncs),
        grid=(nq, bsz // hb),
        in_specs=[grp_spec],
        out_specs=tok_spec,
        out_shape=jax.ShapeDtypeStruct(u4.shape, F32),
        scratch_shapes=[pall],
        compiler_params=_params("arbitrary", "arbitrary"),
        name="s5_unfold",
    )(yg)


def _ret_body(q_f, k_f, v_f, q_b, k_b, v_b, dmat_ref, rsc_ref, csc_ref, gam_ref, o_f, o_b, st_f, st_b):
    @pl.when(pl.program_id(1) == 0)
    def _():
        st_f[...] = jnp.zeros_like(st_f)
        st_b[...] = jnp.zeros_like(st_b)

    dirs = ((q_f, k_f, v_f, o_f, st_f), (q_b, k_b, v_b, o_b, st_b))
    for bb in range(q_f.shape[0]):
        for d, (q_ref, k_ref, v_ref, o_ref, st_ref) in enumerate(dirs):
            for h in range(RET_HEADS):
                qh = q_ref[bb, :, h * RET_DK:(h + 1) * RET_DK]
                kh = k_ref[bb, :, h * RET_DK:(h + 1) * RET_DK]
                vh = v_ref[bb, :, h * RET_DV:(h + 1) * RET_DV]
                st = st_ref[bb, h]
                scores = (_dot_nt(qh, kh) * dmat_ref[d, h]).astype(BF16)
                o = _dot(scores, vh) + rsc_ref[d, h] * _dot(qh, st.astype(BF16))
                o_ref[bb, :, h * RET_DV:(h + 1) * RET_DV] = o.astype(o_ref.dtype)
                k_state = (kh.astype(F32) * csc_ref[d, h]).astype(BF16)
                st_ref[bb, h] = st * gam_ref[d, h] + _dot_tn(k_state, vh)


def _retention(q, k, v, decay_logit, n_ctx):
    bsz, t, _ = q.shape
    c = RET_CHUNK
    nc, ncc = t // c, n_ctx // c
    nl = nc - ncc
    rb = RET_BATCH
    log_gamma = jax.nn.log_sigmoid(decay_logit.astype(F32))[:, :, None, None]
    i = jnp.arange(c, dtype=F32)
    lag = i[:, None] - i[None, :]
    lag = jnp.stack([lag, -lag])[:, None]
    dmat = jnp.where(lag >= 0, jnp.exp(log_gamma * jnp.maximum(lag, 0.0)), 0.0)
    done = jnp.stack([i + 1.0, c - i])[:, None, :, None]
    rsc = jnp.exp(log_gamma * done)
    csc = jnp.exp(log_gamma * (c - done))
    gam = jnp.exp(log_gamma[:, :, 0, 0] * c)
    fwd = lambda b, n: (b, n, 0)
    bwd = lambda b, n: (b, _backward_chunk(n, ncc, nc), 0)
    o_fwd = lambda b, n: (b, jnp.maximum(n - ncc, 0), 0)
    o_bwd = lambda b, n: (b, nl - 1 - jnp.maximum(n - ncc, 0), 0)

    def specs(idx):
        return [pl.BlockSpec((rb, c, RET_QK), idx), pl.BlockSpec((rb, c, RET_QK), idx), pl.BlockSpec((rb, c, RET_MIX), idx)]

    return pl.pallas_call(
        _ret_body,
        grid=(bsz // rb, nc),
        in_specs=specs(fwd) + specs(bwd) + [_const_spec(dmat.shape), _const_spec(rsc.shape), _const_spec(csc.shape),
                                            pl.BlockSpec(memory_space=pltpu.SMEM)],
        out_specs=[pl.BlockSpec((rb, c, RET_MIX), o_fwd), pl.BlockSpec((rb, c, RET_MIX), o_bwd)],
        out_shape=[jax.ShapeDtypeStruct((bsz, nl * c, RET_MIX), BF16)] * 2,
        scratch_shapes=[pltpu.VMEM((rb, RET_HEADS, RET_DK, RET_DV), F32)] * 2,
        compiler_params=_params("arbitrary", "arbitrary"),
        name="retention_scan",
    )(q, k, v, q, k, v, dmat, rsc, csc, gam)


def _zero_counts_at_start(cnt_ref):
    @pl.when(_first_step())
    def _():
        cnt_ref[...] = jnp.zeros_like(cnt_ref)


def _route(xs, mixed, mods, n2g_ref, wr_ref, br_ref, x1_ref, h2_ref, e_ref, w_ref, r_ref, cnt_ref):
    rows = xs[0].shape[0]
    h2s = []
    for bb, (x, mod) in enumerate(zip(xs, mods)):
        x1 = x + mod[2:3] * mixed[bb * rows:(bb + 1) * rows]
        x1_ref[bb] = x1
        h2s.append(_norm_mod(x1, n2g_ref[...], mod[3:4], mod[4:5]))
        h2_ref[bb] = _pack_rows(h2s[bb])
    h2 = h2s[0] if len(h2s) == 1 else jnp.concatenate(h2s, axis=0)
    tm = h2.shape[0]
    logits = _dot3(wr_ref[...], h2, dot=_dot_nt) + br_ref[...]
    ie = lax.broadcasted_iota(I32, logits.shape, 0)
    tops, picks = [], []
    for _ in range(TOP_K):
        mx = jnp.max(logits, axis=0, keepdims=True)
        pick = jnp.min(jnp.where(logits == mx, ie, N_EXPERTS), axis=0, keepdims=True)
        tops.append(mx)
        picks.append(pick)
        logits = jnp.where(ie == pick, -jnp.inf, logits)
    ex = [jnp.exp(tk - tops[0]) for tk in tops]
    den = ex[0] + ex[1] + ex[2] + ex[3]
    for kk in range(TOP_K):
        w_ref[0, kk:kk + 1, :] = ex[kk] / den
        e_ref[0, kk:kk + 1, :] = picks[kk]

    earlier = (lax.broadcasted_iota(I32, (tm, tm), 0) < lax.broadcasted_iota(I32, (tm, tm), 1))
    earlier = jnp.where(earlier, 1.0, 0.0).astype(BF16)
    run = cnt_ref[:, 0:1]
    for kk, pick in enumerate(picks):
        onehot = jnp.where(ie == pick, 1.0, 0.0)
        before = _dot(onehot.astype(BF16), earlier) + run
        r_ref[0, kk:kk + 1, :] = jnp.sum(onehot * before, axis=0, keepdims=True).astype(I32)
        run = run + jnp.sum(onehot, axis=1, keepdims=True)
    cnt_ref[...] = jnp.broadcast_to(run, cnt_ref.shape)


def _mix0_body(nct, ctx_ref, lat_ref, mod_ref, of_ref, ob_ref, g_ref, ys_ref, u_ref, gng_ref, dsk_ref, gluw_ref,
               glub_ref, wo_ref, n2g_ref, wr_ref, br_ref, x1_ref, h2_ref, e_ref, w_ref, r_ref, cnt_ref):
    _zero_counts_at_start(cnt_ref)
    tb, tm = of_ref.shape[0], of_ref.shape[1]
    rows = lambda ref: ref[...].reshape(tb * tm, ref.shape[-1])
    o = rows(of_ref) + rows(ob_ref)
    heads = []
    for h in range(GLA_HEADS):
        oh = o[:, h * GLA_DV:(h + 1) * GLA_DV]
        heads.append(oh * lax.rsqrt(jnp.mean(oh * oh, axis=-1, keepdims=True) + EPS))
    gla = jnp.concatenate(heads, axis=1) * gng_ref[...] * _silu(rows(g_ref).astype(F32))
    lane_blocks = lambda ref: jnp.concatenate([ref[qb].reshape(tb * tm, LANES) for qb in range(ref.shape[0])], axis=1)
    y = jax.nn.gelu(lane_blocks(ys_ref) + dsk_ref[...] * lane_blocks(u_ref))
    y = y * jax.nn.sigmoid(_dot(y.astype(BF16), gluw_ref[...]) + glub_ref[...])
    mixed = _dot(gla.astype(BF16), wo_ref[0:AB_V]) + _dot(y.astype(BF16), wo_ref[AB_V:AB_V + S5_CH])
    _route([_stream_tile(nct, ctx_ref, lat_ref, bb) for bb in range(tb)], mixed, [mod_ref[bb, 0] for bb in range(tb)],
           n2g_ref, wr_ref, br_ref, x1_ref, h2_ref, e_ref, w_ref, r_ref, cnt_ref)


def _mix1_body(x_ref, mod_ref, of_ref, ob_ref, g_ref, ng_ref, wo_ref, n2g_ref, wr_ref, br_ref,
               x1_ref, h2_ref, e_ref, w_ref, r_ref, cnt_ref):
    _zero_counts_at_start(cnt_ref)
    mixed = None
    for h in range(RET_HEADS):
        sl = slice(h * RET_DV, (h + 1) * RET_DV)
        oh = of_ref[0, :, sl].astype(F32) + ob_ref[0, :, sl].astype(F32)
        mu = jnp.mean(oh, axis=-1, keepdims=True)
        cen = oh - mu
        var = jnp.mean(cen * cen, axis=-1, keepdims=True)
        gated = cen * lax.rsqrt(var + EPS) * ng_ref[:, sl] * _silu(g_ref[0, :, sl].astype(F32))
        part = _dot(gated.astype(BF16), wo_ref[sl])
        mixed = part if mixed is None else mixed + part
    _route([x_ref[0]], mixed, [mod_ref[0, 0]], n2g_ref, wr_ref, br_ref, x1_ref, h2_ref, e_ref, w_ref, r_ref, cnt_ref)


def _mix_call(body, name, stream, mods, tiles, acts, consts, norm2_g, w_router, b_router, n_tok, seg_tile0,
              tm=TOKEN_TILE, tb=1):
    bsz, _, d = stream[-1].shape
    off = lambda b, i: (b, i + seg_tile0, 0)
    loc = lambda b, i: (b, i, 0)
    ntl = bsz // tb * tiles
    flat = lambda b, i: (b * tiles + i, 0, 0)
    in_specs = list(_split_specs(n_tok, d, tb)) if len(stream) == 2 else [pl.BlockSpec((tb, tm, d), off)]
    in_specs.append(pl.BlockSpec((tb, 1, 6, d), lambda b, i: (b, ((i + seg_tile0) >= n_tok).astype(I32), 0, 0)))
    args = list(stream) + [mods]
    for arr, offset in acts:
        if arr.ndim == 4:
            in_specs.append(pl.BlockSpec((arr.shape[0], tb, tm, arr.shape[3]), lambda b, i: (0, b, i, 0)))
        else:
            in_specs.append(pl.BlockSpec((tb, tm, arr.shape[2]), off if offset else loc))
        args.append(arr)
    tail = list(consts) + [norm2_g.reshape(1, d), w_router.T, b_router.reshape(N_EXPERTS, 1)]
    in_specs += [_const_spec(a.shape) for a in tail]
    args += tail
    tok_out = pl.BlockSpec((1, TOP_K, tb * tm), flat)
    return pl.pallas_call(
        body,
        grid=(bsz // tb, tiles),
        in_specs=in_specs,
        out_specs=[pl.BlockSpec((tb, tm, d), loc), pl.BlockSpec((tb, tm, d // 2), loc), tok_out, tok_out, tok_out,
                   _const_spec((N_EXPERTS, LANES))],
        out_shape=[jax.ShapeDtypeStruct((bsz, tiles * tm, d), F32), jax.ShapeDtypeStruct((bsz, tiles * tm, d // 2), U32),
                   jax.ShapeDtypeStruct((ntl, TOP_K, tb * tm), I32), jax.ShapeDtypeStruct((ntl, TOP_K, tb * tm), F32),
                   jax.ShapeDtypeStruct((ntl, TOP_K, tb * tm), I32), jax.ShapeDtypeStruct((N_EXPERTS, LANES), F32)],
        compiler_params=_params("arbitrary", "arbitrary"),
        name=name,
    )(*args)


def _cast_rows(src_ref, dst_ref, rows):
    def chunk(j, carry):
        r = pl.multiple_of(j * rows, rows)
        dst_ref[pl.ds(r, rows), :] = src_ref[0, 0, pl.ds(r, rows), :].astype(BF16)
        return carry

    lax.fori_loop(0, dst_ref.shape[0] // rows, chunk, 0)


def _expert_mlp(x, wgu_bf, bgu_ref, wd_bf, bd_ref):
    x_lo, x_hi = _unpack_rows(x)
    half = x_lo.shape[1]
    gu = (_dot(x_lo.astype(BF16), wgu_bf[0:half]) + _dot(x_hi.astype(BF16), wgu_bf[half:2 * half])
          + bgu_ref[0, 0])
    gate = jnp.minimum(gu[:, :D_FF], SWIGLU_LIMIT)
    lin = jnp.clip(gu[:, D_FF:], -SWIGLU_LIMIT, SWIGLU_LIMIT)
    act = gate * jax.nn.sigmoid(SWIGLU_ALPHA * gate) * (lin + 1.0)
    return _pack_rows(_dot(act.astype(BF16), wd_bf[...]) + bd_ref[0, 0])


def _expert_body(be_ref, rows_ref, x_ref, wgu_ref, bgu_ref, wd_ref, bd_ref, o_ref, wgu_bf, wd_bf):
    i = pl.program_id(0)
    rows = rows_ref[i]
    part = x_ref.shape[0] // EXPERT_BLOCK_PARTS
    new_expert = (i == 0) | (be_ref[i] != be_ref[jnp.maximum(i - 1, 0)])

    @pl.when((rows > 0) & new_expert)
    def _():
        _cast_rows(wgu_ref, wgu_bf, 128)
        _cast_rows(wd_ref, wd_bf, 128)

    for p in range(1, EXPERT_BLOCK_PARTS + 1):
        @pl.when((rows > (p - 1) * part) & (rows <= p * part))
        def _():
            o_ref[0:p * part, :] = _expert_mlp(x_ref[0:p * part, :], wgu_bf, bgu_ref, wd_bf, bd_ref)
            if p < EXPERT_BLOCK_PARTS:
                o_ref[p * part:, :] = jnp.zeros((o_ref.shape[0] - p * part, o_ref.shape[1]), o_ref.dtype)

    @pl.when(rows == 0)
    def _():
        o_ref[...] = jnp.zeros_like(o_ref)


def _experts(xb, block_e, block_rows, layer, w_gu, b_gu, w_down, b_down):
    n_slots, half = xb.shape
    d = 2 * half
    n_blocks = n_slots // MOE_BLOCK
    depth = w_gu.shape[0]
    by_expert = lambda i, be, rows: (layer, be[i], 0, 0)
    return pl.pallas_call(
        _expert_body,
        grid_spec=pltpu.PrefetchScalarGridSpec(
            num_scalar_prefetch=2,
            grid=(n_blocks,),
            in_specs=[pl.BlockSpec((MOE_BLOCK, half), lambda i, be, rows: (i, 0)),
                      pl.BlockSpec((1, 1, d, 2 * D_FF), by_expert), pl.BlockSpec((1, 1, 1, 2 * D_FF), by_expert),
                      pl.BlockSpec((1, 1, D_FF, d), by_expert), pl.BlockSpec((1, 1, 1, d), by_expert)],
            out_specs=pl.BlockSpec((MOE_BLOCK, half), lambda i, be, rows: (i, 0)),
            scratch_shapes=[pltpu.VMEM((d, 2 * D_FF), BF16), pltpu.VMEM((D_FF, d), BF16)]),
        out_shape=jax.ShapeDtypeStruct((n_slots, half), U32),
        compiler_params=_params("arbitrary"),
        name="moe_experts",
    )(block_e, block_rows, xb, w_gu, b_gu.reshape(depth, N_EXPERTS, 1, 2 * D_FF), w_down,
      b_down.reshape(depth, N_EXPERTS, 1, d))


def _combine_body(x1_ref, mod_ref, yk_ref, w_ref, fg_ref, o_ref):
    d = x1_ref.shape[2]
    half = d // 2
    x2_lo, x2_hi = _moe_residual(x1_ref, mod_ref[0, 0][5:6], yk_ref, w_ref)
    ms = (jnp.sum(x2_lo * x2_lo, axis=-1, keepdims=True) + jnp.sum(x2_hi * x2_hi, axis=-1, keepdims=True)) / d
    r = lax.rsqrt(ms + EPS)
    o_ref[0, :, 0:half] = x2_lo * r * fg_ref[:, 0:half]
    o_ref[0, :, half:d] = x2_hi * r * fg_ref[:, half:d]


def _combine(x1, mods, yk, w_tok, seg_tile0, n_tok, final_g, tm):
    bsz, t, d = x1.shape
    loc = lambda b, i: (b, i, 0)
    return pl.pallas_call(
        _combine_body,
        grid=(bsz, t // tm),
        in_specs=[pl.BlockSpec((1, tm, d), loc),
                  pl.BlockSpec((1, 1, 6, d), lambda b, i: (b, ((i + seg_tile0) >= n_tok).astype(I32), 0, 0)),
                  pl.BlockSpec((TOP_K, 1, tm, d // 2), lambda b, i: (0, b, i, 0)),
                  pl.BlockSpec((1, tm, TOP_K), loc), _const_spec((1, d))],
        out_specs=pl.BlockSpec((1, tm, d), loc),
        out_shape=jax.ShapeDtypeStruct((bsz, t, d), F32),
        compiler_params=_params("arbitrary", "arbitrary"),
        name="moe_combine",
    )(x1, mods, yk.reshape(TOP_K, bsz, t, d // 2), w_tok.reshape(bsz, t, TOP_K), final_g.reshape(1, d))


def _sc_mesh():
    return plsc.VectorSubcoreMesh(core_axis_name="core", subcore_axis_name="subcore",
                                  num_cores=SC_CORES, num_subcores=SC_SUBCORES)


def _sc_worker_base(per_worker):
    return (lax.axis_index("subcore") * SC_CORES + lax.axis_index("core")) * per_worker


def _sc_dispatch(rows, dest, n_slots):
    n, w = rows.shape
    per_worker = n // SC_WORKERS
    assert per_worker * SC_WORKERS == n and per_worker % SC_CHUNK == 0

    @functools.partial(
        pl.kernel, mesh=_sc_mesh(), out_type=jax.ShapeDtypeStruct((n_slots, w), rows.dtype),
        scratch_types=[pltpu.VMEM((SC_CHUNK,), I32)] * TOP_K + [pltpu.VMEM((SC_CHUNK, w), rows.dtype),
                                                                pltpu.SemaphoreType.DMA],
        name="moe_dispatch")
    def scatter_rows(rows_hbm, dest_hbm, out_hbm, *scratch):
        idx_refs, buf, sem = scratch[:TOP_K], scratch[TOP_K], scratch[TOP_K + 1]
        base0 = _sc_worker_base(per_worker)

        @pl.loop(0, per_worker // SC_CHUNK)
        def _(j):
            base = base0 + j * SC_CHUNK
            pltpu.sync_copy(rows_hbm.at[pl.ds(base, SC_CHUNK)], buf)
            for k, idx in enumerate(idx_refs):
                pltpu.sync_copy(dest_hbm.at[pl.ds(k * n + base, SC_CHUNK)], idx)
            copies = [pltpu.make_async_copy(buf, out_hbm.at[idx], sem) for idx in idx_refs]
            for cp in copies:
                cp.start()
            for cp in copies:
                cp.wait()

    return scatter_rows(rows, dest)


def _sc_gather(table, idx):
    n = idx.shape[0]
    w = table.shape[1]
    per_worker = n // SC_WORKERS
    n_chunks = per_worker // SC_CHUNK
    assert per_worker * SC_WORKERS == n and n_chunks * SC_CHUNK == per_worker and n_chunks % 2 == 0

    @functools.partial(
        pl.kernel, mesh=_sc_mesh(), out_type=jax.ShapeDtypeStruct((n, w), table.dtype),
        scratch_types=([pltpu.VMEM((SC_CHUNK,), I32)] * 2 + [pltpu.VMEM((SC_CHUNK, w), table.dtype)] * 2
                       + [pltpu.SemaphoreType.DMA] * 4),
        name="moe_gather")
    def gather_rows(table_hbm, idx_hbm, out_hbm, idx0, idx1, buf0, buf1, gsem0, gsem1, wsem0, wsem1):
        base0 = _sc_worker_base(per_worker)

        def gather_copy(idx_v, buf, sem):
            return pltpu.make_async_copy(table_hbm.at[idx_v], buf, sem)

        def write_copy(j, buf, sem):
            return pltpu.make_async_copy(buf, out_hbm.at[pl.ds(base0 + j * SC_CHUNK, SC_CHUNK)], sem)

        def start_gather(j, idx_v, buf, sem):
            pltpu.sync_copy(idx_hbm.at[pl.ds(base0 + j * SC_CHUNK, SC_CHUNK)], idx_v)
            gather_copy(idx_v, buf, sem).start()

        start_gather(0, idx0, buf0, gsem0)

        @pl.loop(0, n_chunks, step=2)
        def _(j):
            @pl.when(j > 0)
            def _():
                write_copy(j - 1, buf1, wsem1).wait()
            start_gather(j + 1, idx1, buf1, gsem1)
            gather_copy(idx0, buf0, gsem0).wait()
            write_copy(j, buf0, wsem0).start()

            @pl.when(j + 2 < n_chunks)
            def _():
                write_copy(j, buf0, wsem0).wait()
                start_gather(j + 2, idx0, buf0, gsem0)
            gather_copy(idx1, buf1, gsem1).wait()
            write_copy(j + 1, buf1, wsem1).start()

        write_copy(n_chunks - 2, buf0, wsem0).wait()
        write_copy(n_chunks - 1, buf1, wsem1).wait()

    return gather_rows(table, idx)


def _moe(h2, e_tl, w_tl, r_tl, cnt, layer, w_gu, b_gu, w_down, b_down, tb=1):
    bsz, t, half = h2.shape
    n = bsz * t
    flat = lambda a: a.reshape(bsz // tb, -1, TOP_K, tb, a.shape[2] // tb).transpose(2, 0, 3, 1, 4).reshape(TOP_K, n)
    e_k, w_k, r_k = flat(e_tl), flat(w_tl), flat(r_tl)
    counts = cnt[:, 0].astype(I32)
    padded = (counts + MOE_BLOCK - 1) // MOE_BLOCK * MOE_BLOCK
    pad_end = jnp.cumsum(padded)
    pad_start = pad_end - padded
    n_blocks = (n * TOP_K + MOE_BLOCK - 1) // MOE_BLOCK + N_EXPERTS
    block_start = jnp.arange(n_blocks, dtype=I32) * MOE_BLOCK
    block_e = jnp.minimum(jnp.sum((pad_end[None, :] <= block_start[:, None]).astype(I32), axis=1), N_EXPERTS - 1)
    block_rows = jnp.clip(counts[block_e] - (block_start - pad_start[block_e]), 0, MOE_BLOCK).astype(I32)
    start_k = jnp.sum(jnp.where(e_k[..., None] == jnp.arange(N_EXPERTS, dtype=I32), pad_start, 0), axis=-1)
    dest = (start_k + r_k).reshape(TOP_K * n)
    xb = _sc_dispatch(h2.reshape(n, half), dest, n_blocks * MOE_BLOCK)
    yb = _experts(xb, block_e, block_rows, layer, w_gu, b_gu, w_down, b_down)
    return _sc_gather(yb, dest), w_k.T


def _rope_tables(n_ctx, n_lat):
    n_freq = RET_DK // 4
    inv_freq = ROPE_BASE ** (-jnp.arange(n_freq, dtype=F32) / n_freq)
    pos = jnp.arange(n_lat, dtype=I32)
    cos, sin = [], []
    for p in (pos // GRID_W, pos % GRID_W):
        ang = p.astype(F32)[:, None] * inv_freq
        cos += [jnp.cos(ang), jnp.cos(ang)]
        sin += [-jnp.sin(ang), jnp.sin(ang)]
    cos, sin = jnp.concatenate(cos, axis=1), jnp.concatenate(sin, axis=1)
    return (jnp.concatenate([jnp.ones((n_ctx, RET_DK), F32), cos], axis=0),
            jnp.concatenate([jnp.zeros((n_ctx, RET_DK), F32), sin], axis=0))


def kernel(x, c, ctx, c_ctx, ada_w, ada_b, norm1_g, norm2_g, ab_w_in, ab_w_out, gla_wa, gla_ba, gla_norm_g, s5_lam_re, s5_lam_im, s5_log_step, s5_b_re, s5_b_im, s5_c_re, s5_c_im, s5_d, s5_glu_w, s5_glu_b, ret_w_in, ret_w_out, ret_decay_logit, ret_norm_g, moe_w_router, moe_b_router, moe_w_gu, moe_b_gu, moe_w_down, moe_b_down, final_norm_g):
    bsz, n_lat, d = x.shape
    n_ctx = ctx.shape[1]
    depth = ada_w.shape[0]
    assert depth == 2 and d == D_MODEL and bsz == 8, "kernels are laid out for the stated problem shape"
    assert n_ctx % TOKEN_TILE == 0 and n_lat % LATENT_TILE == 0 and n_lat % GRID_W == 0
    t = n_ctx + n_lat
    nct = n_ctx // TOKEN_TILE

    cvec = jnp.zeros((16, d), F32).at[:bsz].set(c).at[bsz].set(c_ctx)
    mod = _ada_mod(cvec, ada_w, ada_b).reshape(depth, 16, 6, d)
    mods = [jnp.stack([jnp.broadcast_to(mod[l, bsz], (bsz, 6, d)), mod[l, :bsz]], axis=1) for l in range(depth)]

    w_in = ab_w_in[0].astype(BF16)
    cuts = [0, AB_QK, 2 * AB_QK, 2 * AB_QK + AB_V, 2 * AB_QK + 2 * AB_V, 2 * AB_QK + 2 * AB_V + 2 * GLA_RANK,
            w_in.shape[1]]
    pieces = [w_in[:, a:b] for a, b in zip(cuts[:-1], cuts[1:])]
    wa_pad = jnp.zeros((2, 2 * GLA_RANK, AB_QK), F32)
    wa_pad = wa_pad.at[0, :GLA_RANK].set(gla_wa[0, 0]).at[1, GLA_RANK:].set(gla_wa[0, 1])
    outs = _inproj0(ctx, x, mods[0], norm1_g[0], pieces, wa_pad, gla_ba[0].reshape(2, 1, AB_QK), nct)
    v, g, u = outs[8:]
    o_f, o_b = _gla((outs[0:4], outs[4:8]), v, n_ctx)
    ops = _s5_operators(s5_lam_re[0], s5_lam_im[0], s5_log_step[0], s5_b_re[0], s5_b_im[0], s5_c_re[0], s5_c_im[0])
    ys = _s5(u, ops, n_ctx)
    consts = [jnp.tile(gla_norm_g[0], GLA_HEADS).reshape(1, AB_V), s5_d[0].reshape(1, S5_CH),
              s5_glu_w[0].astype(BF16), s5_glu_b[0].reshape(1, S5_CH), ab_w_out[0].astype(BF16)]
    x1, h2, e_tl, w_tl, r_tl, cnt = _mix_call(
        functools.partial(_mix0_body, nct), "mix_gla_s5", (ctx, x), mods[0], t // TOKEN_TILE,
        [(o_f, False), (o_b, False), (g, False), (ys, False), (u, False)], consts,
        norm2_g[0], moe_w_router[0], moe_b_router[0], nct, 0, tb=TOKEN_TILE_BATCH)
    yk, w_tok = _moe(h2, e_tl, w_tl, r_tl, cnt, 0, moe_w_gu, moe_b_gu, moe_w_down, moe_b_down, tb=TOKEN_TILE_BATCH)

    w_in = ret_w_in[0].astype(BF16)
    cuts = [0, RET_QK, 2 * RET_QK, 2 * RET_QK + RET_MIX, w_in.shape[1]]
    pieces = [w_in[:, a:b] for a, b in zip(cuts[:-1], cuts[1:])]
    cos_t, sin_t = _rope_tables(n_ctx, n_lat)
    x2, q, k, v, g = _inproj1(x1, mods[0], yk, w_tok, mods[1], norm1_g[1], cos_t, sin_t, pieces, nct)
    o_f, o_b = _retention(q, k, v, ret_decay_logit[0], n_ctx)
    consts = [ret_norm_g[0].reshape(1, RET_MIX), ret_w_out[0].astype(BF16)]
    x1, h2, e_tl, w_tl, r_tl, cnt = _mix_call(
        _mix1_body, "mix_retention", (x2,), mods[1], n_lat // LATENT_TILE,
        [(o_f, False), (o_b, False), (g, False)], consts,
        norm2_g[1], moe_w_router[1], moe_b_router[1], 0, 0, tm=LATENT_TILE)
    yk, w_tok = _moe(h2, e_tl, w_tl, r_tl, cnt, 1, moe_w_gu, moe_b_gu, moe_w_down, moe_b_down)
    return _combine(x1, mods[1], yk, w_tok, 0, 0, final_norm_g, LATENT_TILE)
```

```python
import functools
import math

import jax
import jax.numpy as jnp
from jax import lax
from jax.experimental import pallas as pl
from jax.experimental.pallas import tpu as pltpu
from jax.experimental.pallas import tpu_sc as plsc

F32, BF16, I32, U32 = jnp.float32, jnp.bfloat16, jnp.int32, jnp.uint32

D_MODEL = 1024
GRID_W = 64
EPS = 1e-6
GLA_HEADS, GLA_DK, GLA_DV, GLA_RANK, GLA_TAU, GLA_CHUNK = 4, 64, 128, 16, 16.0, 64
GLA_BATCH = 8
AB_QK, AB_V = GLA_HEADS * GLA_DK, GLA_HEADS * GLA_DV
S5_CH, S5_GROUP, S5_GROUPS, S5_P = 512, 16, 32, 64
S5_CHUNK = 16
S5_FOLD_BATCH = 4
S5_SCAN_GROUPS = 2
RET_HEADS, RET_DK, RET_DV = 4, 256, 512
RET_CHUNK = 256
RET_BATCH = 2
RET_QK, RET_MIX = RET_HEADS * RET_DK, RET_HEADS * RET_DV
ROPE_BASE = 10000.0
N_EXPERTS, TOP_K, D_FF = 32, 4, 1024
SWIGLU_LIMIT, SWIGLU_ALPHA = 7.0, 1.702
MOE_BLOCK = 1024
EXPERT_BLOCK_PARTS = 4
TOKEN_TILE = 256
LATENT_TILE = 512
TOKEN_TILE_BATCH = 2
ADA_TILE = 768
VMEM_LIMIT = 56 * 1024 * 1024
SC_CORES, SC_SUBCORES = 2, 16
SC_WORKERS = SC_CORES * SC_SUBCORES
SC_CHUNK = 64
LANES = 128

def _params(*sem):
    return pltpu.CompilerParams(dimension_semantics=sem, vmem_limit_bytes=VMEM_LIMIT)


def _dot(a, b):
    return jnp.dot(a, b, preferred_element_type=F32)


def _dot_nt(a, b):
    return lax.dot_general(a, b, (((1,), (1,)), ((), ())), preferred_element_type=F32)


def _dot_tn(a, b):
    return lax.dot_general(a, b, (((0,), (0,)), ((), ())), preferred_element_type=F32)


def _split(a):
    hi = a.astype(BF16)
    return hi, (a - hi.astype(F32)).astype(BF16)


def _dot3(a, b, dot=_dot):
    ah, al = _split(a)
    bh, bl = _split(b)
    return dot(ah, bh) + (dot(ah, bl) + dot(al, bh))


def _pack_rows(x):
    h = x.shape[1] // 2
    lo = lax.bitcast_convert_type(x[:, 0:h].astype(BF16).astype(F32), U32)
    hi = lax.bitcast_convert_type(x[:, h:2 * h].astype(BF16).astype(F32), U32)
    return hi | (lo >> 16)


def _unpack_rows(p):
    lo = lax.bitcast_convert_type(p << 16, F32)
    hi = lax.bitcast_convert_type(p & jnp.uint32(0xFFFF0000), F32)
    return lo, hi


def _silu(x):
    return x * jax.nn.sigmoid(x)


def _norm_mod(x, g, shift, scale):
    r = lax.rsqrt(jnp.mean(x * x, axis=-1, keepdims=True) + EPS)
    return (x * r * g) * (1.0 + scale) + shift


def _const_spec(shape):
    nd = len(shape)
    return pl.BlockSpec(shape, lambda *_: (0,) * nd, pipeline_mode=pl.Buffered(1))


def _ada_body(c_ref, w_ref, b_ref, o_ref):
    o_ref[0] = _dot3(_silu(c_ref[...]), w_ref[0]) + b_ref[0]


def _ada_mod(cvec, ada_w, ada_b):
    depth, d, n6 = ada_w.shape
    rows = cvec.shape[0]
    return pl.pallas_call(
        _ada_body,
        grid=(depth, n6 // ADA_TILE),
        in_specs=[_const_spec((rows, d)),
                  pl.BlockSpec((1, d, ADA_TILE), lambda l, j: (l, 0, j)),
                  pl.BlockSpec((1, 1, ADA_TILE), lambda l, j: (l, 0, j))],
        out_specs=pl.BlockSpec((1, rows, ADA_TILE), lambda l, j: (l, 0, j)),
        out_shape=jax.ShapeDtypeStruct((depth, rows, n6), F32),
        compiler_params=_params("arbitrary", "arbitrary"),
        name="ada_mod",
    )(cvec, ada_w, ada_b.reshape(depth, 1, n6))


def _split_specs(nct, d, tb=1):
    ctx_spec = pl.BlockSpec((tb, TOKEN_TILE, d), lambda b, i: (b, jnp.minimum(i, nct - 1), 0))
    lat_spec = pl.BlockSpec((tb, TOKEN_TILE, d), lambda b, i: (b, jnp.maximum(i - nct, 0), 0))
    return ctx_spec, lat_spec


def _stream_tile(nct, ctx_ref, lat_ref, bb=0):
    return jnp.where(pl.program_id(1) < nct, ctx_ref[bb], lat_ref[bb])


def _normed_rows(nct, ctx_ref, lat_ref, mod_ref, g_ref):
    tiles = []
    for bb in range(mod_ref.shape[0]):
        m = mod_ref[bb, 0]
        tiles.append(_norm_mod(_stream_tile(nct, ctx_ref, lat_ref, bb), g_ref[...], m[0:1], m[1:2]).astype(BF16))
    return tiles[0] if len(tiles) == 1 else jnp.concatenate(tiles, axis=0)


def _store_rows(o_ref, val):
    tm = o_ref.shape[1]
    for bb in range(o_ref.shape[0]):
        o_ref[bb] = val[bb * tm:(bb + 1) * tm].astype(o_ref.dtype)


def _inproj0_body(nct, ctx_ref, lat_ref, mod_ref, g_ref, wq, wk, wv, wg, wlow, wu, wa_ref, ba_ref, tri_ref, ones_ref,
                  qd_f, ki_f, ks_f, ed_f, qd_b, ki_b, ks_b, ed_b, ov, og, ou):
    h = _normed_rows(nct, ctx_ref, lat_ref, mod_ref, g_ref)
    tm = ctx_ref.shape[1]
    low = _dot(h, wlow[...])
    q = _dot(h, wq[...]) * (GLA_DK ** -0.5)
    k = _dot(h, wk[...])
    outs = ((qd_f, ki_f, ks_f, ed_f), (qd_b, ki_b, ks_b, ed_b))
    for d, (qd_ref, ki_ref, ks_ref, ed_ref) in enumerate(outs):
        z = _dot3(low, wa_ref[d]) + ba_ref[d]
        log_a = (jnp.minimum(z, 0.0) - jnp.log1p(jnp.exp(-jnp.abs(z)))) * (1.0 / GLA_TAU)
        la_hi, la_lo = _split(log_a)
        cums, tots = [], []
        for bb in range(qd_ref.shape[0]):
            hi, lo = la_hi[bb * tm:(bb + 1) * tm], la_lo[bb * tm:(bb + 1) * tm]
            cums.append(_dot(tri_ref[d], hi) + _dot(tri_ref[d], lo))
            tots.append(_dot(ones_ref[...], hi) + _dot(ones_ref[...], lo))
        cum = cums[0] if len(cums) == 1 else jnp.concatenate(cums, axis=0)
        tot = tots[0] if len(tots) == 1 else jnp.concatenate(tots, axis=0)
        _store_rows(qd_ref, q * jnp.exp(cum))
        _store_rows(ki_ref, k * jnp.exp(-cum))
        _store_rows(ks_ref, k * jnp.exp(tot - cum))
        for bb in range(ed_ref.shape[0]):
            for ch in range(tm // GLA_CHUNK):
                row = bb * tm + ch * GLA_CHUNK
                ed_ref[bb, ch] = jnp.exp(tot[row:row + 1])
    _store_rows(ov, _dot(h, wv[...]))
    _store_rows(og, _dot(h, wg[...]))
    u = _dot(h, wu[...])
    for qb in range(ou.shape[0]):
        _store_rows(ou.at[qb], u[:, qb * LANES:(qb + 1) * LANES])


def _inproj0(ctx, x, mods, norm_g, weights, wa_pad, ba, nct):
    bsz, n_lat, d = x.shape
    t = ctx.shape[1] + n_lat
    tm, tb = TOKEN_TILE, TOKEN_TILE_BATCH
    mod_spec = pl.BlockSpec((tb, 1, 6, d), lambda b, i: (b, (i >= nct).astype(I32), 0, 0))
    ctx_spec, lat_spec = _split_specs(nct, d, tb)
    pos = jnp.arange(tm)
    same_chunk = (pos[:, None] // GLA_CHUNK) == (pos[None, :] // GLA_CHUNK)
    tri = jnp.stack([same_chunk & (pos[None, :] <= pos[:, None]),
                     same_chunk & (pos[None, :] >= pos[:, None])]).astype(BF16)
    ones = same_chunk.astype(BF16)
    consts = list(weights) + [wa_pad, ba, tri, ones]
    tok = lambda w, dt: (pl.BlockSpec((tb, tm, w), lambda b, i: (b, i, 0)), jax.ShapeDtypeStruct((bsz, t, w), dt))
    per_chunk = (pl.BlockSpec((tb, tm // GLA_CHUNK, 1, AB_QK), lambda b, i: (b, i, 0, 0)),
                 jax.ShapeDtypeStruct((bsz, t // GLA_CHUNK, 1, AB_QK), F32))
    one_dir = [tok(AB_QK, BF16)] * 3 + [per_chunk]
    u_blocks = (pl.BlockSpec((S5_CH // LANES, tb, tm, LANES), lambda b, i: (0, b, i, 0)),
                jax.ShapeDtypeStruct((S5_CH // LANES, bsz, t, LANES), F32))
    outs = one_dir + one_dir + [tok(AB_V, BF16), tok(AB_V, BF16), u_blocks]
    return pl.pallas_call(
        functools.partial(_inproj0_body, nct),
        grid=(bsz // tb, t // tm),
        in_specs=[ctx_spec, lat_spec, mod_spec, _const_spec((1, d))] + [_const_spec(a.shape) for a in consts],
        out_specs=[o[0] for o in outs],
        out_shape=[o[1] for o in outs],
        compiler_params=_params("arbitrary", "arbitrary"),
        name="inproj_gla_s5",
    )(ctx, x, mods, norm_g.reshape(1, d), *consts)


def _rope(acc, cos_ref, sin_ref, o_ref, scale):
    tm = o_ref.shape[1]
    for grp in range(acc.shape[1] // LANES):
        half = grp % 2
        cs = cos_ref[:, half * LANES:(half + 1) * LANES]
        sn = sin_ref[:, half * LANES:(half + 1) * LANES]
        for bb in range(o_ref.shape[0]):
            xg = acc[bb * tm:(bb + 1) * tm, grp * LANES:(grp + 1) * LANES]
            out = xg * cs + pltpu.roll(xg, LANES // 2, 1) * sn
            o_ref[bb, :, grp * LANES:(grp + 1) * LANES] = (out * scale).astype(o_ref.dtype)


def _moe_residual(x1_ref, g2, yk_ref, w_ref, bb=0):
    d = x1_ref.shape[2]
    half = d // 2
    y_lo, y_hi = None, None
    for k in range(TOP_K):
        lo, hi = _unpack_rows(yk_ref[k, bb])
        wk = w_ref[bb, :, k:k + 1]
        y_lo = lo * wk if y_lo is None else y_lo + lo * wk
        y_hi = hi * wk if y_hi is None else y_hi + hi * wk
    return x1_ref[bb, :, 0:half] + g2[:, 0:half] * y_lo, x1_ref[bb, :, half:d] + g2[:, half:d] * y_hi


def _inproj1_body(x1_ref, mod0_ref, yk_ref, w_ref, mod_ref, g_ref, cos_ref, sin_ref, wq, wk, wv, wg, ox, oq, ok, ov, og):
    half = x1_ref.shape[2] // 2
    for bb in range(x1_ref.shape[0]):
        x2_lo, x2_hi = _moe_residual(x1_ref, mod0_ref[bb, 0][5:6], yk_ref, w_ref, bb)
        ox[bb, :, 0:half] = x2_lo
        ox[bb, :, half:2 * half] = x2_hi
        m = mod_ref[bb, 0]
        h = _norm_mod(jnp.concatenate([x2_lo, x2_hi], axis=1), g_ref[...], m[0:1], m[1:2]).astype(BF16)
        _rope(_dot(h, wq[...]), cos_ref, sin_ref, oq.at[bb:bb + 1], 1.0)
        _rope(_dot(h, wk[...]), cos_ref, sin_ref, ok.at[bb:bb + 1], RET_DK ** -0.5)
        ov[bb] = _dot(h, wv[...]).astype(ov.dtype)
        og[bb] = _dot(h, wg[...]).astype(og.dtype)


def _inproj1(x1, mods0, yk, w_tok, mods, norm_g, cos_t, sin_t, weights, nct):
    bsz, t, d = x1.shape
    tm, tb = TOKEN_TILE, TOKEN_TILE_BATCH
    mod_spec = pl.BlockSpec((tb, 1, 6, d), lambda b, i: (b, (i >= nct).astype(I32), 0, 0))
    tok = lambda w: pl.BlockSpec((tb, tm, w), lambda b, i: (b, i, 0))
    lat = lambda w: pl.BlockSpec((tb, tm, w), lambda b, i: (b, jnp.maximum(i - nct, 0), 0))
    tab_spec = pl.BlockSpec((tm, RET_DK), lambda b, i: (i, 0))
    wq, wk, wv, wg = weights
    n_lat = t - nct * tm
    return pl.pallas_call(
        _inproj1_body,
        grid=(bsz // tb, t // tm),
        in_specs=[tok(d), mod_spec, pl.BlockSpec((TOP_K, tb, tm, d // 2), lambda b, i: (0, b, i, 0)), tok(TOP_K),
                  mod_spec, _const_spec((1, d)), tab_spec, tab_spec] + [_const_spec(w.shape) for w in weights],
        out_specs=[lat(d), tok(wq.shape[1]), tok(wk.shape[1]), tok(wv.shape[1]), lat(wg.shape[1])],
        out_shape=[jax.ShapeDtypeStruct((bsz, n_lat, d), F32), jax.ShapeDtypeStruct((bsz, t, wq.shape[1]), BF16),
                   jax.ShapeDtypeStruct((bsz, t, wk.shape[1]), BF16), jax.ShapeDtypeStruct((bsz, t, wv.shape[1]), BF16),
                   jax.ShapeDtypeStruct((bsz, n_lat, wg.shape[1]), BF16)],
        compiler_params=_params("arbitrary", "arbitrary"),
        name="inproj_retention",
    )(x1, mods0, yk.reshape(TOP_K, bsz, t, d // 2), w_tok.reshape(bsz, t, TOP_K), mods, norm_g.reshape(1, d),
      cos_t, sin_t, *weights)


def _backward_chunk(n, n_ctx_chunks, n_chunks):
    return jnp.where(n < n_ctx_chunks, n_ctx_chunks - 1 - n, n_chunks - 1 - (n - n_ctx_chunks))


def _gla_body(qd_f, ki_f, ks_f, ed_f, v_f, qd_b, ki_b, ks_b, ed_b, v_b, hmask_ref, bdmask_ref, o_f, o_b, st_f, st_b):
    c = GLA_CHUNK

    @pl.when(pl.program_id(1) == 0)
    def _():
        st_f[...] = jnp.zeros_like(st_f)
        st_b[...] = jnp.zeros_like(st_b)

    r4 = lax.broadcasted_iota(I32, (GLA_HEADS * c, c), 0) & (c - 1)
    c4 = lax.broadcasted_iota(I32, (GLA_HEADS * c, c), 1)
    dirs = ((qd_f, ki_f, ks_f, ed_f, v_f, o_f, st_f), (qd_b, ki_b, ks_b, ed_b, v_b, o_b, st_b))
    chains = [(bb, d) + dirs[d] for bb in range(qd_f.shape[0]) for d in range(2)]
    scores, inter, grow = [], [], []
    for bb, d, qd_ref, ki_ref, ks_ref, ed_ref, v_ref, o_ref, st_ref in chains:
        q_dec = qd_ref[bb]
        q_heads = jnp.concatenate([q_dec] * GLA_HEADS, axis=0) * hmask_ref[...]
        seen4 = (c4 <= r4) if d == 0 else (c4 >= r4)
        scores.append(jnp.where(seen4, _dot_nt(q_heads, ki_ref[bb]), 0.0).astype(BF16))
        inter.append(_dot_nt(q_dec, st_ref[bb].astype(BF16)))
        grow.append(_dot_tn(v_ref[bb], ks_ref[bb]))
    for (bb, d, qd_ref, ki_ref, ks_ref, ed_ref, v_ref, o_ref, st_ref), sc, o_inter, dst in zip(chains, scores, inter, grow):
        v = v_ref[bb]
        o_intra = jnp.concatenate(
            [_dot(sc[h * c:(h + 1) * c], v[:, h * GLA_DV:(h + 1) * GLA_DV]) for h in range(GLA_HEADS)], axis=1)
        o_ref[bb] = o_intra + o_inter
        st_ref[bb] = st_ref[bb] * ed_ref[bb, 0] + bdmask_ref[...] * dst


def _gla(per_dir, v, n_ctx):
    bsz, t, _ = v.shape
    nc, ncc = t // GLA_CHUNK, n_ctx // GLA_CHUNK
    gb = GLA_BATCH
    fwd = lambda b, n: (b, n, 0)
    bwd = lambda b, n: (b, _backward_chunk(n, ncc, nc), 0)
    hmask = (jnp.arange(AB_QK)[:, None] // GLA_CHUNK == jnp.arange(AB_QK)[None, :] // GLA_DK).astype(BF16)
    bdmask = (jnp.arange(AB_V)[:, None] // GLA_DV == jnp.arange(AB_QK)[None, :] // GLA_DK).astype(F32)

    def specs(idx):
        idx4 = lambda b, n: idx(b, n) + (0,)
        return [pl.BlockSpec((gb, GLA_CHUNK, AB_QK), idx)] * 3 + [pl.BlockSpec((gb, 1, 1, AB_QK), idx4),
                                                                  pl.BlockSpec((gb, GLA_CHUNK, AB_V), idx)]

    return pl.pallas_call(
        _gla_body,
        grid=(bsz // gb, nc),
        in_specs=specs(fwd) + specs(bwd) + [_const_spec(hmask.shape), _const_spec(bdmask.shape)],
        out_specs=[pl.BlockSpec((gb, GLA_CHUNK, AB_V), fwd), pl.BlockSpec((gb, GLA_CHUNK, AB_V), bwd)],
        out_shape=[jax.ShapeDtypeStruct((bsz, t, AB_V), F32)] * 2,
        scratch_shapes=[pltpu.VMEM((gb, AB_V, AB_QK), F32)] * 2,
        compiler_params=_params("arbitrary", "arbitrary"),
        name="gla_scan",
    )(*per_dir[0], v, *per_dir[1], v, hmask, bdmask)


def _cmul(x, y):
    return x[0] * y[0] - x[1] * y[1], x[0] * y[1] + x[1] * y[0]


def _s5_operators(lam_re, lam_im, log_step, b_re, b_im, c_re, c_im):
    ln = S5_CHUNK
    step = jnp.exp(log_step.astype(F32))[..., None]
    lam_re, lam_im = lam_re.astype(F32), lam_im.astype(F32)
    mag = jnp.exp(lam_re * step)
    a = (mag * jnp.cos(lam_im * step), mag * jnp.sin(lam_im * step))
    den = lam_re * lam_re + lam_im * lam_im
    f_re = ((a[0] - 1.0) * lam_re + a[1] * lam_im) / den
    f_im = (a[1] * lam_re - (a[0] - 1.0) * lam_im) / den
    bt_re, bt_im = b_re.transpose(0, 2, 1), b_im.transpose(0, 2, 1)
    bb = _cmul((f_re[:, :, None, :], f_im[:, :, None, :]), (bt_re, bt_im))
    bbt = jnp.concatenate([bb[0], -bb[1]], axis=-1)
    pw = (a[0][:, :, None, :], a[1][:, :, None, :])
    while pw[0].shape[2] < ln:
        top = (pw[0][:, :, -1:, :], pw[1][:, :, -1:, :])
        nxt = _cmul(top, pw)
        pw = (jnp.concatenate([pw[0], nxt[0]], axis=2), jnp.concatenate([pw[1], nxt[1]], axis=2))
    pw = (jnp.concatenate([jnp.ones_like(pw[0][:, :, :1]), pw[0]], axis=2),
          jnp.concatenate([jnp.zeros_like(pw[1][:, :, :1]), pw[1]], axis=2))
    ca = _cmul((c_re[:, :, None], c_im[:, :, None]), (pw[0][:, :, :, None, :], pw[1][:, :, :, None, :]))
    by_dir = lambda arr, lo, flip_d: jnp.stack([jnp.flip(arr[d, :, lo:lo + ln], axis=1) if d == flip_d
                                                else arr[d, :, lo:lo + ln] for d in range(2)])
    rows = lambda arr: arr.reshape(2, S5_GROUPS, ln * S5_GROUP, 2 * S5_P)
    cab = rows(by_dir(jnp.concatenate([ca[0], ca[1]], axis=-1), 0, 1))
    cab2 = rows(by_dir(jnp.concatenate([ca[0], -ca[1]], axis=-1), 1, 1)).astype(BF16)
    pwx = by_dir(jnp.concatenate([pw[0], pw[1]], axis=-1), 0, 0)
    lr, li = pw[0][:, :, ln], pw[1][:, :, ln]
    ac_rows = [jnp.concatenate([lr, lr], -1), jnp.concatenate([-li, li], -1), jnp.concatenate([li, -li], -1)]
    ac = jnp.stack(ac_rows + [jnp.zeros_like(ac_rows[0])] * 5, axis=2)
    return cab, cab2, bbt, pwx, ac


def _s5_group_operators(gg, cab_ref, bbt_ref, pwx_ref, tz, wx):
    ln, ch, p = S5_CHUNK, S5_GROUP, S5_P
    lane = lax.broadcasted_iota(I32, (ch, ln * ch), 1)
    for d in range(2):
        kern = _dot3(bbt_ref[d, gg], cab_ref[d, gg], dot=_dot_nt)
        bt = bbt_ref[d, gg]
        b_re, b_im = bt[:, 0:p], -bt[:, p:2 * p]
        for j in range(ln):
            if d == 0:
                blk = jnp.where(lane >= j * ch, kern if j == 0 else pltpu.roll(kern, j * ch, 1), 0.0)
            else:
                blk = jnp.where(lane < (j + 1) * ch, kern if j == ln - 1 else pltpu.roll(kern, (j + 1) * ch, 1), 0.0)
            tz[gg, d, j * ch:(j + 1) * ch, :] = blk.astype(BF16)
            pr, pi = pwx_ref[d, gg, j:j + 1, 0:p], pwx_ref[d, gg, j:j + 1, p:2 * p]
            x_re, x_im = pr * b_re - pi * b_im, pr * b_im + pi * b_re
            wx[gg, d, j * ch:(j + 1) * ch, :] = jnp.concatenate([x_re, x_im, x_im, x_re], axis=1).astype(BF16)


def _s5_placement(pall):
    rows, cols = pall.shape[1], pall.shape[2]
    row = lax.broadcasted_iota(I32, (rows, cols), 0)
    col = lax.broadcasted_iota(I32, (rows, cols), 1)
    same_token = (row // LANES) == (col // S5_GROUP)
    for g8 in range(pall.shape[0]):
        pall[g8] = jnp.where(same_token & ((row % LANES) == g8 * S5_GROUP + (col % S5_GROUP)), 1.0, 0.0).astype(BF16)


def _first_step():
    return (pl.program_id(0) == 0) & (pl.program_id(1) == 0)


def _s5_fold_body(ncs, u_ref, o_ref, pall, ucat):
    @pl.when(_first_step())
    def _():
        _s5_placement(pall)

    for b in range(u_ref.shape[1]):
        for j in range(S5_CHUNK):
            ucat[b * ncs:(b + 1) * ncs, j * LANES:(j + 1) * LANES] = u_ref[0, b, pl.ds(j, ncs, stride=S5_CHUNK), :].astype(BF16)
    for g8 in range(pall.shape[0]):
        o_ref[g8] = _dot(ucat[...], pall[g8]).astype(BF16)


def _s5_unfold_body(ncs, y_ref, o_ref, pall):
    @pl.when(_first_step())
    def _():
        _s5_placement(pall)

    def token_pair(i2, carry):
        r0 = pl.multiple_of(i2 * 2 * LANES, 2 * LANES)
        acc = _dot_nt(y_ref[0], pall[0, pl.ds(r0, 2 * LANES), :])
        for g8 in range(1, pall.shape[0]):
            acc = acc + _dot_nt(y_ref[g8], pall[g8, pl.ds(r0, 2 * LANES), :])
        for b in range(o_ref.shape[1]):
            for par in range(2):
                o_ref[0, b, pl.ds(2 * i2 + par, ncs, stride=S5_CHUNK), :] = (
                    acc[b * ncs:(b + 1) * ncs, par * LANES:(par + 1) * LANES])
        return carry

    lax.fori_loop(0, S5_CHUNK // 2, token_pair, 0)


def _s5_body(ncs_ctx, ncs, rows, u_ref, cab_ref, cab2_ref, bbt_ref, pwx_ref, ac_ref, y_ref, tz, wx, *vecs):
    half = 2 * S5_P
    n_groups = u_ref.shape[0]
    groups = [vecs[6 * gg:6 * gg + 6] for gg in range(n_groups)]
    for gg, (xx_f, xs_f, xx_b, xs_b, _, _) in enumerate(groups):
        _s5_group_operators(gg, cab_ref, bbt_ref, pwx_ref, tz, wx)
        for d, (xx, xs) in enumerate(((xx_f, xs_f), (xx_b, xs_b))):
            r = _dot(u_ref[gg], wx[gg, d])
            xx[...] = r[:, :half]
            xs[...] = r[:, half:]

    def advance(ac, s, s_sw, x, x_sw):
        return ac[0:1] * s + ac[1:2] * s_sw + x, ac[0:1] * s_sw + ac[2:3] * s + x_sw

    def step(n, carry):
        at_f = pl.ds(n, rows, stride=ncs)
        at_b = pl.ds(_backward_chunk(n, ncs_ctx, ncs), rows, stride=ncs)
        out = []
        for gg, (xx_f, xs_f, xx_b, xs_b, sin_f, sin_b) in enumerate(groups):
            s_f, sw_f, s_b, sw_b = carry[4 * gg:4 * gg + 4]
            sin_f[at_f, :] = s_f
            sin_b[at_b, :] = s_b
            out += advance(ac_ref[0, gg], s_f, sw_f, xx_f[at_f, :], xs_f[at_f, :])
            out += advance(ac_ref[1, gg], s_b, sw_b, xx_b[at_b, :], xs_b[at_b, :])
        return tuple(out)

    zero = jnp.zeros((rows, half), F32)
    lax.fori_loop(0, ncs, step, (zero,) * (4 * n_groups))
    for gg, (_, _, _, _, sin_f, sin_b) in enumerate(groups):
        u = u_ref[gg]
        y_ref[gg] = (_dot(u, tz[gg, 0]) + _dot(u, tz[gg, 1]) + _dot_nt(sin_f[...].astype(BF16), cab2_ref[0, gg])
                     + _dot_nt(sin_b[...].astype(BF16), cab2_ref[1, gg])).astype(BF16)


def _s5(u4, ops, n_ctx):
    nq, bsz, t, _ = u4.shape
    ln, lanes = S5_CHUNK, S5_CHUNK * S5_GROUP
    gq = S5_GROUPS // nq
    ncs, ncs_ctx = t // ln, n_ctx // ln
    m = ncs * bsz
    hb = S5_FOLD_BATCH
    tok_spec = pl.BlockSpec((1, hb, t, LANES), lambda q, h: (q, h, 0, 0))
    grp_spec = pl.BlockSpec((gq, hb * ncs, lanes), lambda q, h: (q, h, 0))
    pall = pltpu.VMEM((gq, ln * LANES, lanes), BF16)
    ug = pl.pallas_call(
        functools.partial(_s5_fold_body, ncs),
        grid=(nq, bsz // hb),
        in_specs=[tok_spec],
        out_specs=grp_spec,
        out_shape=jax.ShapeDtypeStruct((S5_GROUPS, m, lanes), BF16),
        scratch_shapes=[pall, pltpu.VMEM((hb * ncs, ln * LANES), BF16)],
        compiler_params=_params("arbitrary", "arbitrary"),
        name="s5_fold",
    )(u4)
    sg = S5_SCAN_GROUPS
    dir_spec = lambda arr: pl.BlockSpec((2, sg) + arr.shape[2:], lambda g: (0, g, 0, 0))
    yg = pl.pallas_call(
        functools.partial(_s5_body, ncs_ctx, ncs, bsz),
        grid=(S5_GROUPS // sg,),
        in_specs=[pl.BlockSpec((sg, m, lanes), lambda g: (g, 0, 0))] + [dir_spec(arr) for arr in ops],
        out_specs=pl.BlockSpec((sg, m, lanes), lambda g: (g, 0, 0)),
        out_shape=jax.ShapeDtypeStruct((S5_GROUPS, m, lanes), BF16),
        scratch_shapes=[pltpu.VMEM((sg, 2, lanes, lanes), BF16)] * 2 + [pltpu.VMEM((m, 2 * S5_P), F32)] * (6 * sg),
        compiler_params=_params("arbitrary"),
        name="s5_scan",
    )(ug, *ops)
    return pl.pallas_call(
        functools.partial(_s5_unfold_body, ncs),
        grid=(nq, bsz // hb),
        in_specs=[grp_spec],
        out_specs=tok_spec,
        out_shape=jax.ShapeDtypeStruct(u4.shape, F32),
        scratch_shapes=[pall],
        compiler_params=_params("arbitrary", "arbitrary"),
        name="s5_unfold",
    )(yg)


def _ret_body(q_f, k_f, v_f, q_b, k_b, v_b, dmat_ref, rsc_ref, csc_ref, gam_ref, o_f, o_b, st_f, st_b):
    @pl.when(pl.program_id(1) == 0)
    def _():
        st_f[...] = jnp.zeros_like(st_f)
        st_b[...] = jnp.zeros_like(st_b)

    dirs = ((q_f, k_f, v_f, o_f, st_f), (q_b, k_b, v_b, o_b, st_b))
    for bb in range(q_f.shape[0]):
        for d, (q_ref, k_ref, v_ref, o_ref, st_ref) in enumerate(dirs):
            for h in range(RET_HEADS):
                qh = q_ref[bb, :, h * RET_DK:(h + 1) * RET_DK]
                kh = k_ref[bb, :, h * RET_DK:(h + 1) * RET_DK]
                vh = v_ref[bb, :, h * RET_DV:(h + 1) * RET_DV]
                st = st_ref[bb, h]
                scores = (_dot_nt(qh, kh) * dmat_ref[d, h]).astype(BF16)
                o = _dot(scores, vh) + rsc_ref[d, h] * _dot(qh, st.astype(BF16))
                o_ref[bb, :, h * RET_DV:(h + 1) * RET_DV] = o.astype(o_ref.dtype)
                k_state = (kh.astype(F32) * csc_ref[d, h]).astype(BF16)
                st_ref[bb, h] = st * gam_ref[d, h] + _dot_tn(k_state, vh)


def _retention(q, k, v, decay_logit, n_ctx):
    bsz, t, _ = q.shape
    c = RET_CHUNK
    nc, ncc = t // c, n_ctx // c
    nl = nc - ncc
    rb = RET_BATCH
    log_gamma = jax.nn.log_sigmoid(decay_logit.astype(F32))[:, :, None, None]
    i = jnp.arange(c, dtype=F32)
    lag = i[:, None] - i[None, :]
    lag = jnp.stack([lag, -lag])[:, None]
    dmat = jnp.where(lag >= 0, jnp.exp(log_gamma * jnp.maximum(lag, 0.0)), 0.0)
    done = jnp.stack([i + 1.0, c - i])[:, None, :, None]
    rsc = jnp.exp(log_gamma * done)
    csc = jnp.exp(log_gamma * (c - done))
    gam = jnp.exp(log_gamma[:, :, 0, 0] * c)
    fwd = lambda b, n: (b, n, 0)
    bwd = lambda b, n: (b, _backward_chunk(n, ncc, nc), 0)
    o_fwd = lambda b, n: (b, jnp.maximum(n - ncc, 0), 0)
    o_bwd = lambda b, n: (b, nl - 1 - jnp.maximum(n - ncc, 0), 0)

    def specs(idx):
        return [pl.BlockSpec((rb, c, RET_QK), idx), pl.BlockSpec((rb, c, RET_QK), idx), pl.BlockSpec((rb, c, RET_MIX), idx)]

    return pl.pallas_call(
        _ret_body,
        grid=(bsz // rb, nc),
        in_specs=specs(fwd) + specs(bwd) + [_const_spec(dmat.shape), _const_spec(rsc.shape), _const_spec(csc.shape),
                                            pl.BlockSpec(memory_space=pltpu.SMEM)],
        out_specs=[pl.BlockSpec((rb, c, RET_MIX), o_fwd), pl.BlockSpec((rb, c, RET_MIX), o_bwd)],
        out_shape=[jax.ShapeDtypeStruct((bsz, nl * c, RET_MIX), BF16)] * 2,
        scratch_shapes=[pltpu.VMEM((rb, RET_HEADS, RET_DK, RET_DV), F32)] * 2,
        compiler_params=_params("arbitrary", "arbitrary"),
        name="retention_scan",
    )(q, k, v, q, k, v, dmat, rsc, csc, gam)


def _zero_counts_at_start(cnt_ref):
    @pl.when(_first_step())
    def _():
        cnt_ref[...] = jnp.zeros_like(cnt_ref)


def _route(xs, mixed, mods, n2g_ref, wr_ref, br_ref, x1_ref, h2_ref, e_ref, w_ref, r_ref, cnt_ref):
    rows = xs[0].shape[0]
    h2s = []
    for bb, (x, mod) in enumerate(zip(xs, mods)):
        x1 = x + mod[2:3] * mixed[bb * rows:(bb + 1) * rows]
        x1_ref[bb] = x1
        h2s.append(_norm_mod(x1, n2g_ref[...], mod[3:4], mod[4:5]))
        h2_ref[bb] = _pack_rows(h2s[bb])
    h2 = h2s[0] if len(h2s) == 1 else jnp.concatenate(h2s, axis=0)
    tm = h2.shape[0]
    logits = _dot3(wr_ref[...], h2, dot=_dot_nt) + br_ref[...]
    ie = lax.broadcasted_iota(I32, logits.shape, 0)
    tops, picks = [], []
    for _ in range(TOP_K):
        mx = jnp.max(logits, axis=0, keepdims=True)
        pick = jnp.min(jnp.where(logits == mx, ie, N_EXPERTS), axis=0, keepdims=True)
        tops.append(mx)
        picks.append(pick)
        logits = jnp.where(ie == pick, -jnp.inf, logits)
    ex = [jnp.exp(tk - tops[0]) for tk in tops]
    den = ex[0] + ex[1] + ex[2] + ex[3]
    for kk in range(TOP_K):
        w_ref[0, kk:kk + 1, :] = ex[kk] / den
        e_ref[0, kk:kk + 1, :] = picks[kk]

    earlier = (lax.broadcasted_iota(I32, (tm, tm), 0) < lax.broadcasted_iota(I32, (tm, tm), 1))
    earlier = jnp.where(earlier, 1.0, 0.0).astype(BF16)
    run = cnt_ref[:, 0:1]
    for kk, pick in enumerate(picks):
        onehot = jnp.where(ie == pick, 1.0, 0.0)
        before = _dot(onehot.astype(BF16), earlier) + run
        r_ref[0, kk:kk + 1, :] = jnp.sum(onehot * before, axis=0, keepdims=True).astype(I32)
        run = run + jnp.sum(onehot, axis=1, keepdims=True)
    cnt_ref[...] = jnp.broadcast_to(run, cnt_ref.shape)


def _mix0_body(nct, ctx_ref, lat_ref, mod_ref, of_ref, ob_ref, g_ref, ys_ref, u_ref, gng_ref, dsk_ref, gluw_ref,
               glub_ref, wo_ref, n2g_ref, wr_ref, br_ref, x1_ref, h2_ref, e_ref, w_ref, r_ref, cnt_ref):
    _zero_counts_at_start(cnt_ref)
    tb, tm = of_ref.shape[0], of_ref.shape[1]
    rows = lambda ref: ref[...].reshape(tb * tm, ref.shape[-1])
    o = rows(of_ref) + rows(ob_ref)
    heads = []
    for h in range(GLA_HEADS):
        oh = o[:, h * GLA_DV:(h + 1) * GLA_DV]
        heads.append(oh * lax.rsqrt(jnp.mean(oh * oh, axis=-1, keepdims=True) + EPS))
    gla = jnp.concatenate(heads, axis=1) * gng_ref[...] * _silu(rows(g_ref).astype(F32))
    lane_blocks = lambda ref: jnp.concatenate([ref[qb].reshape(tb * tm, LANES) for qb in range(ref.shape[0])], axis=1)
    y = jax.nn.gelu(lane_blocks(ys_ref) + dsk_ref[...] * lane_blocks(u_ref))
    y = y * jax.nn.sigmoid(_dot(y.astype(BF16), gluw_ref[...]) + glub_ref[...])
    mixed = _dot(gla.astype(BF16), wo_ref[0:AB_V]) + _dot(y.astype(BF16), wo_ref[AB_V:AB_V + S5_CH])
    _route([_stream_tile(nct, ctx_ref, lat_ref, bb) for bb in range(tb)], mixed, [mod_ref[bb, 0] for bb in range(tb)],
           n2g_ref, wr_ref, br_ref, x1_ref, h2_ref, e_ref, w_ref, r_ref, cnt_ref)


def _mix1_body(x_ref, mod_ref, of_ref, ob_ref, g_ref, ng_ref, wo_ref, n2g_ref, wr_ref, br_ref,
               x1_ref, h2_ref, e_ref, w_ref, r_ref, cnt_ref):
    _zero_counts_at_start(cnt_ref)
    mixed = None
    for h in range(RET_HEADS):
        sl = slice(h * RET_DV, (h + 1) * RET_DV)
        oh = of_ref[0, :, sl].astype(F32) + ob_ref[0, :, sl].astype(F32)
        mu = jnp.mean(oh, axis=-1, keepdims=True)
        cen = oh - mu
        var = jnp.mean(cen * cen, axis=-1, keepdims=True)
        gated = cen * lax.rsqrt(var + EPS) * ng_ref[:, sl] * _silu(g_ref[0, :, sl].astype(F32))
        part = _dot(gated.astype(BF16), wo_ref[sl])
        mixed = part if mixed is None else mixed + part
    _route([x_ref[0]], mixed, [mod_ref[0, 0]], n2g_ref, wr_ref, br_ref, x1_ref, h2_ref, e_ref, w_ref, r_ref, cnt_ref)


def _mix_call(body, name, stream, mods, tiles, acts, consts, norm2_g, w_router, b_router, n_tok, seg_tile0,
              tm=TOKEN_TILE, tb=1):
    bsz, _, d = stream[-1].shape
    off = lambda b, i: (b, i + seg_tile0, 0)
    loc = lambda b, i: (b, i, 0)
    ntl = bsz // tb * tiles
    flat = lambda b, i: (b * tiles + i, 0, 0)
    in_specs = list(_split_specs(n_tok, d, tb)) if len(stream) == 2 else [pl.BlockSpec((tb, tm, d), off)]
    in_specs.append(pl.BlockSpec((tb, 1, 6, d), lambda b, i: (b, ((i + seg_tile0) >= n_tok).astype(I32), 0, 0)))
    args = list(stream) + [mods]
    for arr, offset in acts:
        if arr.ndim == 4:
            in_specs.append(pl.BlockSpec((arr.shape[0], tb, tm, arr.shape[3]), lambda b, i: (0, b, i, 0)))
        else:
            in_specs.append(pl.BlockSpec((tb, tm, arr.shape[2]), off if offset else loc))
        args.append(arr)
    tail = list(consts) + [norm2_g.reshape(1, d), w_router.T, b_router.reshape(N_EXPERTS, 1)]
    in_specs += [_const_spec(a.shape) for a in tail]
    args += tail
    tok_out = pl.BlockSpec((1, TOP_K, tb * tm), flat)
    return pl.pallas_call(
        body,
        grid=(bsz // tb, tiles),
        in_specs=in_specs,
        out_specs=[pl.BlockSpec((tb, tm, d), loc), pl.BlockSpec((tb, tm, d // 2), loc), tok_out, tok_out, tok_out,
                   _const_spec((N_EXPERTS, LANES))],
        out_shape=[jax.ShapeDtypeStruct((bsz, tiles * tm, d), F32), jax.ShapeDtypeStruct((bsz, tiles * tm, d // 2), U32),
                   jax.ShapeDtypeStruct((ntl, TOP_K, tb * tm), I32), jax.ShapeDtypeStruct((ntl, TOP_K, tb * tm), F32),
                   jax.ShapeDtypeStruct((ntl, TOP_K, tb * tm), I32), jax.ShapeDtypeStruct((N_EXPERTS, LANES), F32)],
        compiler_params=_params("arbitrary", "arbitrary"),
        name=name,
    )(*args)


def _cast_rows(src_ref, dst_ref, rows):
    def chunk(j, carry):
        r = pl.multiple_of(j * rows, rows)
        dst_ref[pl.ds(r, rows), :] = src_ref[pl.ds(r, rows), :].astype(BF16)
        return carry

    lax.fori_loop(0, dst_ref.shape[0] // rows, chunk, 0)


def _expert_mlp(x, wgu_bf, bgu_ref, wd_bf, bd_ref):
    x_lo, x_hi = _unpack_rows(x)
    half = x_lo.shape[1]
    gu = (_dot(x_lo.astype(BF16), wgu_bf[0:half]) + _dot(x_hi.astype(BF16), wgu_bf[half:2 * half])
          + bgu_ref[0, 0])
    gate = jnp.minimum(gu[:, :D_FF], SWIGLU_LIMIT)
    lin = jnp.clip(gu[:, D_FF:], -SWIGLU_LIMIT, SWIGLU_LIMIT)
    act = gate * jax.nn.sigmoid(SWIGLU_ALPHA * gate) * (lin + 1.0)
    return _pack_rows(_dot(act.astype(BF16), wd_bf[...]) + bd_ref[0, 0])


def _expert_body(layer, be_ref, rows_ref, slot_ref, next_ref, x_ref, wgu_hbm, bgu_ref, wd_hbm, bd_ref, o_ref,
                 wgu_f32, wd_f32, wgu_bf, wd_bf, sems):
    i = pl.program_id(0)
    rows = rows_ref[i]
    part = x_ref.shape[0] // EXPERT_BLOCK_PARTS
    new_expert = (i == 0) | (be_ref[i] != be_ref[jnp.maximum(i - 1, 0)])

    def weight_copies(expert, slot):
        return (pltpu.make_async_copy(wgu_hbm.at[layer, expert], wgu_f32.at[slot], sems.at[0, slot]),
                pltpu.make_async_copy(wd_hbm.at[layer, expert], wd_f32.at[slot], sems.at[1, slot]))

    @pl.when((rows > 0) & new_expert)
    def _():
        slot = slot_ref[i]

        @pl.when(i == 0)
        def _():
            for cp in weight_copies(be_ref[i], slot):
                cp.start()

        @pl.when(next_ref[i] >= 0)
        def _():
            for cp in weight_copies(next_ref[i], 1 - slot):
                cp.start()

        for cp in weight_copies(be_ref[i], slot):
            cp.wait()
        _cast_rows(wgu_f32.at[slot], wgu_bf, 128)
        _cast_rows(wd_f32.at[slot], wd_bf, 128)

    for p in range(1, EXPERT_BLOCK_PARTS + 1):
        @pl.when((rows > (p - 1) * part) & (rows <= p * part))
        def _():
            o_ref[0:p * part, :] = _expert_mlp(x_ref[0:p * part, :], wgu_bf, bgu_ref, wd_bf, bd_ref)
            if p < EXPERT_BLOCK_PARTS:
                o_ref[p * part:, :] = jnp.zeros((o_ref.shape[0] - p * part, o_ref.shape[1]), o_ref.dtype)

    @pl.when(rows == 0)
    def _():
        o_ref[...] = jnp.zeros_like(o_ref)


def _experts(xb, block_e, block_rows, block_slot, block_next, layer, w_gu, b_gu, w_down, b_down):
    n_slots, half = xb.shape
    d = 2 * half
    n_blocks = n_slots // MOE_BLOCK
    depth = w_gu.shape[0]
    by_expert = lambda i, be, *_: (layer, be[i], 0, 0)
    rows_spec = pl.BlockSpec((MOE_BLOCK, half), lambda i, *_: (i, 0))
    return pl.pallas_call(
        functools.partial(_expert_body, layer),
        grid_spec=pltpu.PrefetchScalarGridSpec(
            num_scalar_prefetch=4,
            grid=(n_blocks,),
            in_specs=[rows_spec, pl.BlockSpec(memory_space=pl.ANY), pl.BlockSpec((1, 1, 1, 2 * D_FF), by_expert),
                      pl.BlockSpec(memory_space=pl.ANY), pl.BlockSpec((1, 1, 1, d), by_expert)],
            out_specs=rows_spec,
            scratch_shapes=[pltpu.VMEM((2, d, 2 * D_FF), F32), pltpu.VMEM((2, D_FF, d), F32),
                            pltpu.VMEM((d, 2 * D_FF), BF16), pltpu.VMEM((D_FF, d), BF16),
                            pltpu.SemaphoreType.DMA((2, 2))]),
        out_shape=jax.ShapeDtypeStruct((n_slots, half), U32),
        compiler_params=_params("arbitrary"),
        name="moe_experts",
    )(block_e, block_rows, block_slot, block_next, xb, w_gu, b_gu.reshape(depth, N_EXPERTS, 1, 2 * D_FF), w_down,
      b_down.reshape(depth, N_EXPERTS, 1, d))


def _combine_body(x1_ref, mod_ref, yk_ref, w_ref, fg_ref, o_ref):
    d = x1_ref.shape[2]
    half = d // 2
    x2_lo, x2_hi = _moe_residual(x1_ref, mod_ref[0, 0][5:6], yk_ref, w_ref)
    ms = (jnp.sum(x2_lo * x2_lo, axis=-1, keepdims=True) + jnp.sum(x2_hi * x2_hi, axis=-1, keepdims=True)) / d
    r = lax.rsqrt(ms + EPS)
    o_ref[0, :, 0:half] = x2_lo * r * fg_ref[:, 0:half]
    o_ref[0, :, half:d] = x2_hi * r * fg_ref[:, half:d]


def _combine(x1, mods, yk, w_tok, seg_tile0, n_tok, final_g, tm):
    bsz, t, d = x1.shape
    loc = lambda b, i: (b, i, 0)
    return pl.pallas_call(
        _combine_body,
        grid=(bsz, t // tm),
        in_specs=[pl.BlockSpec((1, tm, d), loc),
                  pl.BlockSpec((1, 1, 6, d), lambda b, i: (b, ((i + seg_tile0) >= n_tok).astype(I32), 0, 0)),
                  pl.BlockSpec((TOP_K, 1, tm, d // 2), lambda b, i: (0, b, i, 0)),
                  pl.BlockSpec((1, tm, TOP_K), loc), _const_spec((1, d))],
        out_specs=pl.BlockSpec((1, tm, d), loc),
        out_shape=jax.ShapeDtypeStruct((bsz, t, d), F32),
        compiler_params=_params("arbitrary", "arbitrary"),
        name="moe_combine",
    )(x1, mods, yk.reshape(TOP_K, bsz, t, d // 2), w_tok.reshape(bsz, t, TOP_K), final_g.reshape(1, d))


def _sc_mesh():
    return plsc.VectorSubcoreMesh(core_axis_name="core", subcore_axis_name="subcore",
                                  num_cores=SC_CORES, num_subcores=SC_SUBCORES)


def _sc_worker_base(per_worker):
    return (lax.axis_index("subcore") * SC_CORES + lax.axis_index("core")) * per_worker


def _sc_dispatch(rows, dest, n_slots):
    n, w = rows.shape
    per_worker = n // SC_WORKERS
    assert per_worker * SC_WORKERS == n and per_worker % SC_CHUNK == 0

    @functools.partial(
        pl.kernel, mesh=_sc_mesh(), out_type=jax.ShapeDtypeStruct((n_slots, w), rows.dtype),
        scratch_types=[pltpu.VMEM((SC_CHUNK,), I32)] * TOP_K + [pltpu.VMEM((SC_CHUNK, w), rows.dtype),
                                                                pltpu.SemaphoreType.DMA],
        name="moe_dispatch")
    def scatter_rows(rows_hbm, dest_hbm, out_hbm, *scratch):
        idx_refs, buf, sem = scratch[:TOP_K], scratch[TOP_K], scratch[TOP_K + 1]
        base0 = _sc_worker_base(per_worker)

        @pl.loop(0, per_worker // SC_CHUNK)
        def _(j):
            base = base0 + j * SC_CHUNK
            pltpu.sync_copy(rows_hbm.at[pl.ds(base, SC_CHUNK)], buf)
            for k, idx in enumerate(idx_refs):
                pltpu.sync_copy(dest_hbm.at[pl.ds(k * n + base, SC_CHUNK)], idx)
            copies = [pltpu.make_async_copy(buf, out_hbm.at[idx], sem) for idx in idx_refs]
            for cp in copies:
                cp.start()
            for cp in copies:
                cp.wait()

    return scatter_rows(rows, dest)


def _sc_gather(table, idx):
    n = idx.shape[0]
    w = table.shape[1]
    per_worker = n // SC_WORKERS
    n_chunks = per_worker // SC_CHUNK
    assert per_worker * SC_WORKERS == n and n_chunks * SC_CHUNK == per_worker and n_chunks % 2 == 0

    @functools.partial(
        pl.kernel, mesh=_sc_mesh(), out_type=jax.ShapeDtypeStruct((n, w), table.dtype),
        scratch_types=([pltpu.VMEM((SC_CHUNK,), I32)] * 2 + [pltpu.VMEM((SC_CHUNK, w), table.dtype)] * 2
                       + [pltpu.SemaphoreType.DMA] * 4),
        name="moe_gather")
    def gather_rows(table_hbm, idx_hbm, out_hbm, idx0, idx1, buf0, buf1, gsem0, gsem1, wsem0, wsem1):
        base0 = _sc_worker_base(per_worker)

        def gather_copy(idx_v, buf, sem):
            return pltpu.make_async_copy(table_hbm.at[idx_v], buf, sem)

        def write_copy(j, buf, sem):
            return pltpu.make_async_copy(buf, out_hbm.at[pl.ds(base0 + j * SC_CHUNK, SC_CHUNK)], sem)

        def start_gather(j, idx_v, buf, sem):
            pltpu.sync_copy(idx_hbm.at[pl.ds(base0 + j * SC_CHUNK, SC_CHUNK)], idx_v)
            gather_copy(idx_v, buf, sem).start()

        start_gather(0, idx0, buf0, gsem0)

        @pl.loop(0, n_chunks, step=2)
        def _(j):
            @pl.when(j > 0)
            def _():
                write_copy(j - 1, buf1, wsem1).wait()
            start_gather(j + 1, idx1, buf1, gsem1)
            gather_copy(idx0, buf0, gsem0).wait()
            write_copy(j, buf0, wsem0).start()

            @pl.when(j + 2 < n_chunks)
            def _():
                write_copy(j, buf0, wsem0).wait()
                start_gather(j + 2, idx0, buf0, gsem0)
            gather_copy(idx1, buf1, gsem1).wait()
            write_copy(j + 1, buf1, wsem1).start()

        write_copy(n_chunks - 2, buf0, wsem0).wait()
        write_copy(n_chunks - 1, buf1, wsem1).wait()

    return gather_rows(table, idx)


def _moe(h2, e_tl, w_tl, r_tl, cnt, layer, w_gu, b_gu, w_down, b_down, tb=1):
    bsz, t, half = h2.shape
    n = bsz * t
    flat = lambda a: a.reshape(bsz // tb, -1, TOP_K, tb, a.shape[2] // tb).transpose(2, 0, 3, 1, 4).reshape(TOP_K, n)
    e_k, w_k, r_k = flat(e_tl), flat(w_tl), flat(r_tl)
    counts = cnt[:, 0].astype(I32)
    padded = (counts + MOE_BLOCK - 1) // MOE_BLOCK * MOE_BLOCK
    pad_end = jnp.cumsum(padded)
    pad_start = pad_end - padded
    n_blocks = (n * TOP_K + MOE_BLOCK - 1) // MOE_BLOCK + N_EXPERTS
    block_start = jnp.arange(n_blocks, dtype=I32) * MOE_BLOCK
    block_e = jnp.minimum(jnp.sum((pad_end[None, :] <= block_start[:, None]).astype(I32), axis=1), N_EXPERTS - 1)
    block_rows = jnp.clip(counts[block_e] - (block_start - pad_start[block_e]), 0, MOE_BLOCK).astype(I32)
    experts = jnp.arange(N_EXPERTS, dtype=I32)
    busy = padded > 0
    block_slot = ((jnp.cumsum(busy.astype(I32)) - 1) % 2)[block_e].astype(I32)
    later = jnp.where(busy[None, :] & (experts[None, :] > experts[:, None]), experts[None, :], N_EXPERTS)
    next_busy = jnp.min(later, axis=1)
    block_next = jnp.where(next_busy < N_EXPERTS, next_busy, -1)[block_e].astype(I32)
    start_k = jnp.sum(jnp.where(e_k[..., None] == jnp.arange(N_EXPERTS, dtype=I32), pad_start, 0), axis=-1)
    dest = (start_k + r_k).reshape(TOP_K * n)
    xb = _sc_dispatch(h2.reshape(n, half), dest, n_blocks * MOE_BLOCK)
    yb = _experts(xb, block_e, block_rows, block_slot, block_next, layer, w_gu, b_gu, w_down, b_down)
    return _sc_gather(yb, dest), w_k.T


def _rope_tables(n_ctx, n_lat):
    n_freq = RET_DK // 4
    inv_freq = ROPE_BASE ** (-jnp.arange(n_freq, dtype=F32) / n_freq)
    pos = jnp.arange(n_lat, dtype=I32)
    cos, sin = [], []
    for p in (pos // GRID_W, pos % GRID_W):
        ang = p.astype(F32)[:, None] * inv_freq
        cos += [jnp.cos(ang), jnp.cos(ang)]
        sin += [-jnp.sin(ang), jnp.sin(ang)]
    cos, sin = jnp.concatenate(cos, axis=1), jnp.concatenate(sin, axis=1)
    return (jnp.concatenate([jnp.ones((n_ctx, RET_DK), F32), cos], axis=0),
            jnp.concatenate([jnp.zeros((n_ctx, RET_DK), F32), sin], axis=0))


def kernel(x, c, ctx, c_ctx, ada_w, ada_b, norm1_g, norm2_g, ab_w_in, ab_w_out, gla_wa, gla_ba, gla_norm_g, s5_lam_re, s5_lam_im, s5_log_step, s5_b_re, s5_b_im, s5_c_re, s5_c_im, s5_d, s5_glu_w, s5_glu_b, ret_w_in, ret_w_out, ret_decay_logit, ret_norm_g, moe_w_router, moe_b_router, moe_w_gu, moe_b_gu, moe_w_down, moe_b_down, final_norm_g):
    bsz, n_lat, d = x.shape
    n_ctx = ctx.shape[1]
    depth = ada_w.shape[0]
    assert depth == 2 and d == D_MODEL and bsz == 8, "kernels are laid out for the stated problem shape"
    assert n_ctx % TOKEN_TILE == 0 and n_lat % LATENT_TILE == 0 and n_lat % GRID_W == 0
    t = n_ctx + n_lat
    nct = n_ctx // TOKEN_TILE

    cvec = jnp.zeros((16, d), F32).at[:bsz].set(c).at[bsz].set(c_ctx)
    mod = _ada_mod(cvec, ada_w, ada_b).reshape(depth, 16, 6, d)
    mods = [jnp.stack([jnp.broadcast_to(mod[l, bsz], (bsz, 6, d)), mod[l, :bsz]], axis=1) for l in range(depth)]

    w_in = ab_w_in[0].astype(BF16)
    cuts = [0, AB_QK, 2 * AB_QK, 2 * AB_QK + AB_V, 2 * AB_QK + 2 * AB_V, 2 * AB_QK + 2 * AB_V + 2 * GLA_RANK,
            w_in.shape[1]]
    pieces = [w_in[:, a:b] for a, b in zip(cuts[:-1], cuts[1:])]
    wa_pad = jnp.zeros((2, 2 * GLA_RANK, AB_QK), F32)
    wa_pad = wa_pad.at[0, :GLA_RANK].set(gla_wa[0, 0]).at[1, GLA_RANK:].set(gla_wa[0, 1])
    outs = _inproj0(ctx, x, mods[0], norm1_g[0], pieces, wa_pad, gla_ba[0].reshape(2, 1, AB_QK), nct)
    v, g, u = outs[8:]
    o_f, o_b = _gla((outs[0:4], outs[4:8]), v, n_ctx)
    ops = _s5_operators(s5_lam_re[0], s5_lam_im[0], s5_log_step[0], s5_b_re[0], s5_b_im[0], s5_c_re[0], s5_c_im[0])
    ys = _s5(u, ops, n_ctx)
    consts = [jnp.tile(gla_norm_g[0], GLA_HEADS).reshape(1, AB_V), s5_d[0].reshape(1, S5_CH),
              s5_glu_w[0].astype(BF16), s5_glu_b[0].reshape(1, S5_CH), ab_w_out[0].astype(BF16)]
    x1, h2, e_tl, w_tl, r_tl, cnt = _mix_call(
        functools.partial(_mix0_body, nct), "mix_gla_s5", (ctx, x), mods[0], t // TOKEN_TILE,
        [(o_f, False), (o_b, False), (g, False), (ys, False), (u, False)], consts,
        norm2_g[0], moe_w_router[0], moe_b_router[0], nct, 0, tb=TOKEN_TILE_BATCH)
    yk, w_tok = _moe(h2, e_tl, w_tl, r_tl, cnt, 0, moe_w_gu, moe_b_gu, moe_w_down, moe_b_down, tb=TOKEN_TILE_BATCH)

    w_in = ret_w_in[0].astype(BF16)
    cuts = [0, RET_QK, 2 * RET_QK, 2 * RET_QK + RET_MIX, w_in.shape[1]]
    pieces = [w_in[:, a:b] for a, b in zip(cuts[:-1], cuts[1:])]
    cos_t, sin_t = _rope_tables(n_ctx, n_lat)
    x2, q, k, v, g = _inproj1(x1, mods[0], yk, w_tok, mods[1], norm1_g[1], cos_t, sin_t, pieces, nct)
    o_f, o_b = _retention(q, k, v, ret_decay_logit[0], n_ctx)
    consts = [ret_norm_g[0].reshape(1, RET_MIX), ret_w_out[0].astype(BF16)]
    x1, h2, e_tl, w_tl, r_tl, cnt = _mix_call(
        _mix1_body, "mix_retention", (x2,), mods[1], n_lat // LATENT_TILE,
        [(o_f, False), (o_b, False), (g, False)], consts,
        norm2_g[1], moe_w_router[1], moe_b_router[1], 0, 0, tm=LATENT_TILE)
    yk, w_tok = _moe(h2, e_tl, w_tl, r_tl, cnt, 1, moe_w_gu, moe_b_gu, moe_w_down, moe_b_down)
    return _combine(x1, mods[1], yk, w_tok, 0, 0, final_norm_g, LATENT_TILE)
```

```python
import functools
import math

import jax
import jax.numpy as jnp
from jax import lax
from jax.experimental import pallas as pl
from jax.experimental.pallas import tpu as pltpu
from jax.experimental.pallas import tpu_sc as plsc

F32, BF16, I32, U32 = jnp.float32, jnp.bfloat16, jnp.int32, jnp.uint32

D_MODEL = 1024
GRID_W = 64
EPS = 1e-6
GLA_HEADS, GLA_DK, GLA_DV, GLA_RANK, GLA_TAU, GLA_CHUNK = 4, 64, 128, 16, 16.0, 64
GLA_BATCH = 8
AB_QK, AB_V = GLA_HEADS * GLA_DK, GLA_HEADS * GLA_DV
S5_CH, S5_GROUP, S5_GROUPS, S5_P = 512, 16, 32, 64
S5_CHUNK = 16
S5_FOLD_BATCH = 4
S5_SCAN_GROUPS = 2
RET_HEADS, RET_DK, RET_DV = 4, 256, 512
RET_CHUNK = 256
RET_BATCH = 2
RET_QK, RET_MIX = RET_HEADS * RET_DK, RET_HEADS * RET_DV
ROPE_BASE = 10000.0
N_EXPERTS, TOP_K, D_FF = 32, 4, 1024
SWIGLU_LIMIT, SWIGLU_ALPHA = 7.0, 1.702
MOE_BLOCK = 1024
EXPERT_BLOCK_PARTS = 4
TOKEN_TILE = 256
LATENT_TILE = 512
TOKEN_TILE_BATCH = 2
INPROJ0_TILE_BATCH = 4
ADA_TILE = 768
VMEM_LIMIT = 56 * 1024 * 1024
SC_CORES, SC_SUBCORES = 2, 16
SC_WORKERS = SC_CORES * SC_SUBCORES
SC_CHUNK = 64
LANES = 128

def _params(*sem):
    return pltpu.CompilerParams(dimension_semantics=sem, vmem_limit_bytes=VMEM_LIMIT)


def _dot(a, b):
    return jnp.dot(a, b, preferred_element_type=F32)


def _dot_nt(a, b):
    return lax.dot_general(a, b, (((1,), (1,)), ((), ())), preferred_element_type=F32)


def _dot_tn(a, b):
    return lax.dot_general(a, b, (((0,), (0,)), ((), ())), preferred_element_type=F32)


def _split(a):
    hi = a.astype(BF16)
    return hi, (a - hi.astype(F32)).astype(BF16)


def _dot3(a, b, dot=_dot):
    ah, al = _split(a)
    bh, bl = _split(b)
    return dot(ah, bh) + (dot(ah, bl) + dot(al, bh))


def _pack_rows(x):
    h = x.shape[1] // 2
    lo = lax.bitcast_convert_type(x[:, 0:h].astype(BF16).astype(F32), U32)
    hi = lax.bitcast_convert_type(x[:, h:2 * h].astype(BF16).astype(F32), U32)
    return hi | (lo >> 16)


def _unpack_rows(p):
    lo = lax.bitcast_convert_type(p << 16, F32)
    hi = lax.bitcast_convert_type(p & jnp.uint32(0xFFFF0000), F32)
    return lo, hi


def _silu(x):
    return x * jax.nn.sigmoid(x)


def _norm_mod(x, g, shift, scale):
    r = lax.rsqrt(jnp.mean(x * x, axis=-1, keepdims=True) + EPS)
    return (x * r * g) * (1.0 + scale) + shift


def _const_spec(shape):
    nd = len(shape)
    return pl.BlockSpec(shape, lambda *_: (0,) * nd, pipeline_mode=pl.Buffered(1))


def _ada_body(c_ref, w_ref, b_ref, o_ref):
    o_ref[0] = _dot3(_silu(c_ref[...]), w_ref[0]) + b_ref[0]


def _ada_mod(cvec, ada_w, ada_b):
    depth, d, n6 = ada_w.shape
    rows = cvec.shape[0]
    return pl.pallas_call(
        _ada_body,
        grid=(depth, n6 // ADA_TILE),
        in_specs=[_const_spec((rows, d)),
                  pl.BlockSpec((1, d, ADA_TILE), lambda l, j: (l, 0, j)),
                  pl.BlockSpec((1, 1, ADA_TILE), lambda l, j: (l, 0, j))],
        out_specs=pl.BlockSpec((1, rows, ADA_TILE), lambda l, j: (l, 0, j)),
        out_shape=jax.ShapeDtypeStruct((depth, rows, n6), F32),
        compiler_params=_params("arbitrary", "arbitrary"),
        name="ada_mod",
    )(cvec, ada_w, ada_b.reshape(depth, 1, n6))


def _split_specs(nct, d, tb=1):
    ctx_spec = pl.BlockSpec((tb, TOKEN_TILE, d), lambda b, i: (b, jnp.minimum(i, nct - 1), 0))
    lat_spec = pl.BlockSpec((tb, TOKEN_TILE, d), lambda b, i: (b, jnp.maximum(i - nct, 0), 0))
    return ctx_spec, lat_spec


def _stream_tile(nct, ctx_ref, lat_ref, bb=0):
    return jnp.where(pl.program_id(1) < nct, ctx_ref[bb], lat_ref[bb])


def _normed_rows(nct, ctx_ref, lat_ref, mod_ref, g_ref):
    tiles = []
    for bb in range(mod_ref.shape[0]):
        m = mod_ref[bb, 0]
        tiles.append(_norm_mod(_stream_tile(nct, ctx_ref, lat_ref, bb), g_ref[...], m[0:1], m[1:2]).astype(BF16))
    return tiles[0] if len(tiles) == 1 else jnp.concatenate(tiles, axis=0)


def _store_rows(o_ref, val):
    tm = o_ref.shape[1]
    for bb in range(o_ref.shape[0]):
        o_ref[bb] = val[bb * tm:(bb + 1) * tm].astype(o_ref.dtype)


def _inproj0_body(nct, ctx_ref, lat_ref, mod_ref, g_ref, wq, wk, wv, wg, wlow, wu, wa_ref, ba_ref, tri_ref, ones_ref,
                  qd_f, ki_f, ks_f, ed_f, qd_b, ki_b, ks_b, ed_b, ov, og, ou):
    h = _normed_rows(nct, ctx_ref, lat_ref, mod_ref, g_ref)
    tm = ctx_ref.shape[1]
    low = _dot(h, wlow[...])
    q = _dot(h, wq[...]) * (GLA_DK ** -0.5)
    k = _dot(h, wk[...])
    outs = ((qd_f, ki_f, ks_f, ed_f), (qd_b, ki_b, ks_b, ed_b))
    for d, (qd_ref, ki_ref, ks_ref, ed_ref) in enumerate(outs):
        z = _dot3(low, wa_ref[d]) + ba_ref[d]
        log_a = (jnp.minimum(z, 0.0) - jnp.log1p(jnp.exp(-jnp.abs(z)))) * (1.0 / GLA_TAU)
        la_hi, la_lo = _split(log_a)
        cums, tots = [], []
        for bb in range(qd_ref.shape[0]):
            hi, lo = la_hi[bb * tm:(bb + 1) * tm], la_lo[bb * tm:(bb + 1) * tm]
            cums.append(_dot(tri_ref[d], hi) + _dot(tri_ref[d], lo))
            tots.append(_dot(ones_ref[...], hi) + _dot(ones_ref[...], lo))
        cum = cums[0] if len(cums) == 1 else jnp.concatenate(cums, axis=0)
        tot = tots[0] if len(tots) == 1 else jnp.concatenate(tots, axis=0)
        _store_rows(qd_ref, q * jnp.exp(cum))
        _store_rows(ki_ref, k * jnp.exp(-cum))
        _store_rows(ks_ref, k * jnp.exp(tot - cum))
        for bb in range(ed_ref.shape[0]):
            for ch in range(tm // GLA_CHUNK):
                row = bb * tm + ch * GLA_CHUNK
                ed_ref[bb, ch] = jnp.exp(tot[row:row + 1])
    _store_rows(ov, _dot(h, wv[...]))
    _store_rows(og, _dot(h, wg[...]))
    u = _dot(h, wu[...])
    for qb in range(ou.shape[0]):
        _store_rows(ou.at[qb], u[:, qb * LANES:(qb + 1) * LANES])


def _inproj0(ctx, x, mods, norm_g, weights, wa_pad, ba, nct):
    bsz, n_lat, d = x.shape
    t = ctx.shape[1] + n_lat
    tm, tb = TOKEN_TILE, INPROJ0_TILE_BATCH
    mod_spec = pl.BlockSpec((tb, 1, 6, d), lambda b, i: (b, (i >= nct).astype(I32), 0, 0))
    ctx_spec, lat_spec = _split_specs(nct, d, tb)
    pos = jnp.arange(tm)
    same_chunk = (pos[:, None] // GLA_CHUNK) == (pos[None, :] // GLA_CHUNK)
    tri = jnp.stack([same_chunk & (pos[None, :] <= pos[:, None]),
                     same_chunk & (pos[None, :] >= pos[:, None])]).astype(BF16)
    ones = same_chunk.astype(BF16)
    consts = list(weights) + [wa_pad, ba, tri, ones]
    tok = lambda w, dt: (pl.BlockSpec((tb, tm, w), lambda b, i: (b, i, 0)), jax.ShapeDtypeStruct((bsz, t, w), dt))
    per_chunk = (pl.BlockSpec((tb, tm // GLA_CHUNK, 1, AB_QK), lambda b, i: (b, i, 0, 0)),
                 jax.ShapeDtypeStruct((bsz, t // GLA_CHUNK, 1, AB_QK), F32))
    one_dir = [tok(AB_QK, BF16)] * 3 + [per_chunk]
    u_blocks = (pl.BlockSpec((S5_CH // LANES, tb, tm, LANES), lambda b, i: (0, b, i, 0)),
                jax.ShapeDtypeStruct((S5_CH // LANES, bsz, t, LANES), F32))
    outs = one_dir + one_dir + [tok(AB_V, BF16), tok(AB_V, BF16), u_blocks]
    return pl.pallas_call(
        functools.partial(_inproj0_body, nct),
        grid=(bsz // tb, t // tm),
        in_specs=[ctx_spec, lat_spec, mod_spec, _const_spec((1, d))] + [_const_spec(a.shape) for a in consts],
        out_specs=[o[0] for o in outs],
        out_shape=[o[1] for o in outs],
        compiler_params=_params("arbitrary", "arbitrary"),
        name="inproj_gla_s5",
    )(ctx, x, mods, norm_g.reshape(1, d), *consts)


def _rope(acc, cos_ref, sin_ref, o_ref, scale):
    tm = o_ref.shape[1]
    for grp in range(acc.shape[1] // LANES):
        half = grp % 2
        cs = cos_ref[:, half * LANES:(half + 1) * LANES]
        sn = sin_ref[:, half * LANES:(half + 1) * LANES]
        for bb in range(o_ref.shape[0]):
            xg = acc[bb * tm:(bb + 1) * tm, grp * LANES:(grp + 1) * LANES]
            out = xg * cs + pltpu.roll(xg, LANES // 2, 1) * sn
            o_ref[bb, :, grp * LANES:(grp + 1) * LANES] = (out * scale).astype(o_ref.dtype)


def _moe_residual(x1_ref, g2, yk_ref, w_ref, bb=0):
    d = x1_ref.shape[2]
    half = d // 2
    y_lo, y_hi = None, None
    for k in range(TOP_K):
        lo, hi = _unpack_rows(yk_ref[k, bb])
        wk = w_ref[bb, :, k:k + 1]
        y_lo = lo * wk if y_lo is None else y_lo + lo * wk
        y_hi = hi * wk if y_hi is None else y_hi + hi * wk
    return x1_ref[bb, :, 0:half] + g2[:, 0:half] * y_lo, x1_ref[bb, :, half:d] + g2[:, half:d] * y_hi


def _inproj1_body(x1_ref, mod0_ref, yk_ref, w_ref, mod_ref, g_ref, cos_ref, sin_ref, wq, wk, wv, wg, ox, oq, ok, ov, og):
    half = x1_ref.shape[2] // 2
    for bb in range(x1_ref.shape[0]):
        x2_lo, x2_hi = _moe_residual(x1_ref, mod0_ref[bb, 0][5:6], yk_ref, w_ref, bb)
        ox[bb, :, 0:half] = x2_lo
        ox[bb, :, half:2 * half] = x2_hi
        m = mod_ref[bb, 0]
        h = _norm_mod(jnp.concatenate([x2_lo, x2_hi], axis=1), g_ref[...], m[0:1], m[1:2]).astype(BF16)
        _rope(_dot(h, wq[...]), cos_ref, sin_ref, oq.at[bb:bb + 1], 1.0)
        _rope(_dot(h, wk[...]), cos_ref, sin_ref, ok.at[bb:bb + 1], RET_DK ** -0.5)
        ov[bb] = _dot(h, wv[...]).astype(ov.dtype)
        og[bb] = _dot(h, wg[...]).astype(og.dtype)


def _inproj1(x1, mods0, yk, w_tok, mods, norm_g, cos_t, sin_t, weights, nct):
    bsz, t, d = x1.shape
    tm, tb = TOKEN_TILE, TOKEN_TILE_BATCH
    mod_spec = pl.BlockSpec((tb, 1, 6, d), lambda b, i: (b, (i >= nct).astype(I32), 0, 0))
    tok = lambda w: pl.BlockSpec((tb, tm, w), lambda b, i: (b, i, 0))
    lat = lambda w: pl.BlockSpec((tb, tm, w), lambda b, i: (b, jnp.maximum(i - nct, 0), 0))
    tab_spec = pl.BlockSpec((tm, RET_DK), lambda b, i: (i, 0))
    wq, wk, wv, wg = weights
    n_lat = t - nct * tm
    return pl.pallas_call(
        _inproj1_body,
        grid=(bsz // tb, t // tm),
        in_specs=[tok(d), mod_spec, pl.BlockSpec((TOP_K, tb, tm, d // 2), lambda b, i: (0, b, i, 0)), tok(TOP_K),
                  mod_spec, _const_spec((1, d)), tab_spec, tab_spec] + [_const_spec(w.shape) for w in weights],
        out_specs=[lat(d), tok(wq.shape[1]), tok(wk.shape[1]), tok(wv.shape[1]), lat(wg.shape[1])],
        out_shape=[jax.ShapeDtypeStruct((bsz, n_lat, d), F32), jax.ShapeDtypeStruct((bsz, t, wq.shape[1]), BF16),
                   jax.ShapeDtypeStruct((bsz, t, wk.shape[1]), BF16), jax.ShapeDtypeStruct((bsz, t, wv.shape[1]), BF16),
                   jax.ShapeDtypeStruct((bsz, n_lat, wg.shape[1]), BF16)],
        compiler_params=_params("arbitrary", "arbitrary"),
        name="inproj_retention",
    )(x1, mods0, yk.reshape(TOP_K, bsz, t, d // 2), w_tok.reshape(bsz, t, TOP_K), mods, norm_g.reshape(1, d),
      cos_t, sin_t, *weights)


def _backward_chunk(n, n_ctx_chunks, n_chunks):
    return jnp.where(n < n_ctx_chunks, n_ctx_chunks - 1 - n, n_chunks - 1 - (n - n_ctx_chunks))


def _gla_body(qd_f, ki_f, ks_f, ed_f, v_f, qd_b, ki_b, ks_b, ed_b, v_b, hmask_ref, bdmask_ref, o_f, o_b, st_f, st_b):
    c = GLA_CHUNK

    @pl.when(pl.program_id(1) == 0)
    def _():
        st_f[...] = jnp.zeros_like(st_f)
        st_b[...] = jnp.zeros_like(st_b)

    r4 = lax.broadcasted_iota(I32, (GLA_HEADS * c, c), 0) & (c - 1)
    c4 = lax.broadcasted_iota(I32, (GLA_HEADS * c, c), 1)
    dirs = ((qd_f, ki_f, ks_f, ed_f, v_f, o_f, st_f), (qd_b, ki_b, ks_b, ed_b, v_b, o_b, st_b))
    chains = [(bb, d) + dirs[d] for bb in range(qd_f.shape[0]) for d in range(2)]
    scores, inter, grow = [], [], []
    for bb, d, qd_ref, ki_ref, ks_ref, ed_ref, v_ref, o_ref, st_ref in chains:
        q_dec = qd_ref[bb]
        q_heads = jnp.concatenate([q_dec] * GLA_HEADS, axis=0) * hmask_ref[...]
        seen4 = (c4 <= r4) if d == 0 else (c4 >= r4)
        scores.append(jnp.where(seen4, _dot_nt(q_heads, ki_ref[bb]), 0.0).astype(BF16))
        inter.append(_dot_nt(q_dec, st_ref[bb].astype(BF16)))
        grow.append(_dot_tn(v_ref[bb], ks_ref[bb]))
    for (bb, d, qd_ref, ki_ref, ks_ref, ed_ref, v_ref, o_ref, st_ref), sc, o_inter, dst in zip(chains, scores, inter, grow):
        v = v_ref[bb]
        o_intra = jnp.concatenate(
            [_dot(sc[h * c:(h + 1) * c], v[:, h * GLA_DV:(h + 1) * GLA_DV]) for h in range(GLA_HEADS)], axis=1)
        o_ref[bb] = o_intra + o_inter
        st_ref[bb] = st_ref[bb] * ed_ref[bb, 0] + bdmask_ref[...] * dst


def _gla(per_dir, v, n_ctx):
    bsz, t, _ = v.shape
    nc, ncc = t // GLA_CHUNK, n_ctx // GLA_CHUNK
    gb = GLA_BATCH
    fwd = lambda b, n: (b, n, 0)
    bwd = lambda b, n: (b, _backward_chunk(n, ncc, nc), 0)
    hmask = (jnp.arange(AB_QK)[:, None] // GLA_CHUNK == jnp.arange(AB_QK)[None, :] // GLA_DK).astype(BF16)
    bdmask = (jnp.arange(AB_V)[:, None] // GLA_DV == jnp.arange(AB_QK)[None, :] // GLA_DK).astype(F32)

    def specs(idx):
        idx4 = lambda b, n: idx(b, n) + (0,)
        return [pl.BlockSpec((gb, GLA_CHUNK, AB_QK), idx)] * 3 + [pl.BlockSpec((gb, 1, 1, AB_QK), idx4),
                                                                  pl.BlockSpec((gb, GLA_CHUNK, AB_V), idx)]

    return pl.pallas_call(
        _gla_body,
        grid=(bsz // gb, nc),
        in_specs=specs(fwd) + specs(bwd) + [_const_spec(hmask.shape), _const_spec(bdmask.shape)],
        out_specs=[pl.BlockSpec((gb, GLA_CHUNK, AB_V), fwd), pl.BlockSpec((gb, GLA_CHUNK, AB_V), bwd)],
        out_shape=[jax.ShapeDtypeStruct((bsz, t, AB_V), F32)] * 2,
        scratch_shapes=[pltpu.VMEM((gb, AB_V, AB_QK), F32)] * 2,
        compiler_params=_params("arbitrary", "arbitrary"),
        name="gla_scan",
    )(*per_dir[0], v, *per_dir[1], v, hmask, bdmask)


def _cmul(x, y):
    return x[0] * y[0] - x[1] * y[1], x[0] * y[1] + x[1] * y[0]


def _s5_operators(lam_re, lam_im, log_step, b_re, b_im, c_re, c_im):
    ln = S5_CHUNK
    step = jnp.exp(log_step.astype(F32))[..., None]
    lam_re, lam_im = lam_re.astype(F32), lam_im.astype(F32)
    mag = jnp.exp(lam_re * step)
    a = (mag * jnp.cos(lam_im * step), mag * jnp.sin(lam_im * step))
    den = lam_re * lam_re + lam_im * lam_im
    f_re = ((a[0] - 1.0) * lam_re + a[1] * lam_im) / den
    f_im = (a[1] * lam_re - (a[0] - 1.0) * lam_im) / den
    bt_re, bt_im = b_re.transpose(0, 2, 1), b_im.transpose(0, 2, 1)
    bb = _cmul((f_re[:, :, None, :], f_im[:, :, None, :]), (bt_re, bt_im))
    bbt = jnp.concatenate([bb[0], -bb[1]], axis=-1)
    pw = (a[0][:, :, None, :], a[1][:, :, None, :])
    while pw[0].shape[2] < ln:
        top = (pw[0][:, :, -1:, :], pw[1][:, :, -1:, :])
        nxt = _cmul(top, pw)
        pw = (jnp.concatenate([pw[0], nxt[0]], axis=2), jnp.concatenate([pw[1], nxt[1]], axis=2))
    pw = (jnp.concatenate([jnp.ones_like(pw[0][:, :, :1]), pw[0]], axis=2),
          jnp.concatenate([jnp.zeros_like(pw[1][:, :, :1]), pw[1]], axis=2))
    ca = _cmul((c_re[:, :, None], c_im[:, :, None]), (pw[0][:, :, :, None, :], pw[1][:, :, :, None, :]))
    by_dir = lambda arr, lo, flip_d: jnp.stack([jnp.flip(arr[d, :, lo:lo + ln], axis=1) if d == flip_d
                                                else arr[d, :, lo:lo + ln] for d in range(2)])
    rows = lambda arr: arr.reshape(2, S5_GROUPS, ln * S5_GROUP, 2 * S5_P)
    cab = rows(by_dir(jnp.concatenate([ca[0], ca[1]], axis=-1), 0, 1))
    cab2 = rows(by_dir(jnp.concatenate([ca[0], -ca[1]], axis=-1), 1, 1)).astype(BF16)
    pwx = by_dir(jnp.concatenate([pw[0], pw[1]], axis=-1), 0, 0)
    lr, li = pw[0][:, :, ln], pw[1][:, :, ln]
    ac_rows = [jnp.concatenate([lr, lr], -1), jnp.concatenate([-li, li], -1), jnp.concatenate([li, -li], -1)]
    ac = jnp.stack(ac_rows + [jnp.zeros_like(ac_rows[0])] * 5, axis=2)
    return cab, cab2, bbt, pwx, ac


def _s5_group_operators(gg, cab_ref, bbt_ref, pwx_ref, tz, wx):
    ln, ch, p = S5_CHUNK, S5_GROUP, S5_P
    lane = lax.broadcasted_iota(I32, (ch, ln * ch), 1)
    for d in range(2):
        kern = _dot3(bbt_ref[d, gg], cab_ref[d, gg], dot=_dot_nt)
        bt = bbt_ref[d, gg]
        b_re, b_im = bt[:, 0:p], -bt[:, p:2 * p]
        for j in range(ln):
            if d == 0:
                blk = jnp.where(lane >= j * ch, kern if j == 0 else pltpu.roll(kern, j * ch, 1), 0.0)
            else:
                blk = jnp.where(lane < (j + 1) * ch, kern if j == ln - 1 else pltpu.roll(kern, (j + 1) * ch, 1), 0.0)
            tz[gg, d, j * ch:(j + 1) * ch, :] = blk.astype(BF16)
            pr, pi = pwx_ref[d, gg, j:j + 1, 0:p], pwx_ref[d, gg, j:j + 1, p:2 * p]
            x_re, x_im = pr * b_re - pi * b_im, pr * b_im + pi * b_re
            wx[gg, d, j * ch:(j + 1) * ch, :] = jnp.concatenate([x_re, x_im, x_im, x_re], axis=1).astype(BF16)


def _s5_placement(pall):
    rows, cols = pall.shape[1], pall.shape[2]
    row = lax.broadcasted_iota(I32, (rows, cols), 0)
    col = lax.broadcasted_iota(I32, (rows, cols), 1)
    same_token = (row // LANES) == (col // S5_GROUP)
    for g8 in range(pall.shape[0]):
        pall[g8] = jnp.where(same_token & ((row % LANES) == g8 * S5_GROUP + (col % S5_GROUP)), 1.0, 0.0).astype(BF16)


def _first_step():
    return (pl.program_id(0) == 0) & (pl.program_id(1) == 0)


def _s5_fold_body(ncs, u_ref, o_ref, pall, ucat):
    @pl.when(_first_step())
    def _():
        _s5_placement(pall)

    for b in range(u_ref.shape[1]):
        for j in range(S5_CHUNK):
            ucat[b * ncs:(b + 1) * ncs, j * LANES:(j + 1) * LANES] = u_ref[0, b, pl.ds(j, ncs, stride=S5_CHUNK), :].astype(BF16)
    for g8 in range(pall.shape[0]):
        o_ref[g8] = _dot(ucat[...], pall[g8]).astype(BF16)


def _s5_unfold_body(ncs, y_ref, o_ref, pall):
    @pl.when(_first_step())
    def _():
        _s5_placement(pall)

    def token_pair(i2, carry):
        r0 = pl.multiple_of(i2 * 2 * LANES, 2 * LANES)
        acc = _dot_nt(y_ref[0], pall[0, pl.ds(r0, 2 * LANES), :])
        for g8 in range(1, pall.shape[0]):
            acc = acc + _dot_nt(y_ref[g8], pall[g8, pl.ds(r0, 2 * LANES), :])
        for b in range(o_ref.shape[1]):
            for par in range(2):
                o_ref[0, b, pl.ds(2 * i2 + par, ncs, stride=S5_CHUNK), :] = (
                    acc[b * ncs:(b + 1) * ncs, par * LANES:(par + 1) * LANES])
        return carry

    lax.fori_loop(0, S5_CHUNK // 2, token_pair, 0)


def _s5_body(ncs_ctx, ncs, rows, u_ref, cab_ref, cab2_ref, bbt_ref, pwx_ref, ac_ref, y_ref, tz, wx, *vecs):
    half = 2 * S5_P
    n_groups = u_ref.shape[0]
    groups = [vecs[6 * gg:6 * gg + 6] for gg in range(n_groups)]
    for gg, (xx_f, xs_f, xx_b, xs_b, _, _) in enumerate(groups):
        _s5_group_operators(gg, cab_ref, bbt_ref, pwx_ref, tz, wx)
        for d, (xx, xs) in enumerate(((xx_f, xs_f), (xx_b, xs_b))):
            r = _dot(u_ref[gg], wx[gg, d])
            xx[...] = r[:, :half]
            xs[...] = r[:, half:]

    def advance(ac, s, s_sw, x, x_sw):
        return ac[0:1] * s + ac[1:2] * s_sw + x, ac[0:1] * s_sw + ac[2:3] * s + x_sw

    def step(n, carry):
        at_f = pl.ds(n, rows, stride=ncs)
        at_b = pl.ds(_backward_chunk(n, ncs_ctx, ncs), rows, stride=ncs)
        out = []
        for gg, (xx_f, xs_f, xx_b, xs_b, sin_f, sin_b) in enumerate(groups):
            s_f, sw_f, s_b, sw_b = carry[4 * gg:4 * gg + 4]
            sin_f[at_f, :] = s_f
            sin_b[at_b, :] = s_b
            out += advance(ac_ref[0, gg], s_f, sw_f, xx_f[at_f, :], xs_f[at_f, :])
            out += advance(ac_ref[1, gg], s_b, sw_b, xx_b[at_b, :], xs_b[at_b, :])
        return tuple(out)

    zero = jnp.zeros((rows, half), F32)
    lax.fori_loop(0, ncs, step, (zero,) * (4 * n_groups))
    for gg, (_, _, _, _, sin_f, sin_b) in enumerate(groups):
        u = u_ref[gg]
        y_ref[gg] = (_dot(u, tz[gg, 0]) + _dot(u, tz[gg, 1]) + _dot_nt(sin_f[...].astype(BF16), cab2_ref[0, gg])
                     + _dot_nt(sin_b[...].astype(BF16), cab2_ref[1, gg])).astype(BF16)


def _s5(u4, ops, n_ctx):
    nq, bsz, t, _ = u4.shape
    ln, lanes = S5_CHUNK, S5_CHUNK * S5_GROUP
    gq = S5_GROUPS // nq
    ncs, ncs_ctx = t // ln, n_ctx // ln
    m = ncs * bsz
    hb = S5_FOLD_BATCH
    tok_spec = pl.BlockSpec((1, hb, t, LANES), lambda q, h: (q, h, 0, 0))
    grp_spec = pl.BlockSpec((gq, hb * ncs, lanes), lambda q, h: (q, h, 0))
    pall = pltpu.VMEM((gq, ln * LANES, lanes), BF16)
    ug = pl.pallas_call(
        functools.partial(_s5_fold_body, ncs),
        grid=(nq, bsz // hb),
        in_specs=[tok_spec],
        out_specs=grp_spec,
        out_shape=jax.ShapeDtypeStruct((S5_GROUPS, m, lanes), BF16),
        scratch_shapes=[pall, pltpu.VMEM((hb * ncs, ln * LANES), BF16)],
        compiler_params=_params("arbitrary", "arbitrary"),
        name="s5_fold",
    )(u4)
    sg = S5_SCAN_GROUPS
    dir_spec = lambda arr: pl.BlockSpec((2, sg) + arr.shape[2:], lambda g: (0, g, 0, 0))
    yg = pl.pallas_call(
        functools.partial(_s5_body, ncs_ctx, ncs, bsz),
        grid=(S5_GROUPS // sg,),
        in_specs=[pl.BlockSpec((sg, m, lanes), lambda g: (g, 0, 0))] + [dir_spec(arr) for arr in ops],
        out_specs=pl.BlockSpec((sg, m, lanes), lambda g: (g, 0, 0)),
        out_shape=jax.ShapeDtypeStruct((S5_GROUPS, m, lanes), BF16),
        scratch_shapes=[pltpu.VMEM((sg, 2, lanes, lanes), BF16)] * 2 + [pltpu.VMEM((m, 2 * S5_P), F32)] * (6 * sg),
        compiler_params=_params("arbitrary"),
        name="s5_scan",
    )(ug, *ops)
    return pl.pallas_call(
        functools.partial(_s5_unfold_body, ncs),
        grid=(nq, bsz // hb),
        in_specs=[grp_spec],
        out_specs=tok_spec,
        out_shape=jax.ShapeDtypeStruct(u4.shape, F32),
        scratch_shapes=[pall],
        compiler_params=_params("arbitrary", "arbitrary"),
        name="s5_unfold",
    )(yg)


def _ret_body(q_f, k_f, v_f, q_b, k_b, v_b, dmat_ref, rsc_ref, csc_ref, gam_ref, o_f, o_b, st_f, st_b):
    @pl.when(pl.program_id(1) == 0)
    def _():
        st_f[...] = jnp.zeros_like(st_f)
        st_b[...] = jnp.zeros_like(st_b)

    dirs = ((q_f, k_f, v_f, o_f, st_f), (q_b, k_b, v_b, o_b, st_b))
    for bb in range(q_f.shape[0]):
        for d, (q_ref, k_ref, v_ref, o_ref, st_ref) in enumerate(dirs):
            for h in range(RET_HEADS):
                qh = q_ref[bb, :, h * RET_DK:(h + 1) * RET_DK]
                kh = k_ref[bb, :, h * RET_DK:(h + 1) * RET_DK]
                vh = v_ref[bb, :, h * RET_DV:(h + 1) * RET_DV]
                st = st_ref[bb, h]
                scores = (_dot_nt(qh, kh) * dmat_ref[d, h]).astype(BF16)
                o = _dot(scores, vh) + rsc_ref[d, h] * _dot(qh, st.astype(BF16))
                o_ref[bb, :, h * RET_DV:(h + 1) * RET_DV] = o.astype(o_ref.dtype)
                k_state = (kh.astype(F32) * csc_ref[d, h]).astype(BF16)
                st_ref[bb, h] = st * gam_ref[d, h] + _dot_tn(k_state, vh)


def _retention(q, k, v, decay_logit, n_ctx):
    bsz, t, _ = q.shape
    c = RET_CHUNK
    nc, ncc = t // c, n_ctx // c
    nl = nc - ncc
    rb = RET_BATCH
    log_gamma = jax.nn.log_sigmoid(decay_logit.astype(F32))[:, :, None, None]
    i = jnp.arange(c, dtype=F32)
    lag = i[:, None] - i[None, :]
    lag = jnp.stack([lag, -lag])[:, None]
    dmat = jnp.where(lag >= 0, jnp.exp(log_gamma * jnp.maximum(lag, 0.0)), 0.0)
    done = jnp.stack([i + 1.0, c - i])[:, None, :, None]
    rsc = jnp.exp(log_gamma * done)
    csc = jnp.exp(log_gamma * (c - done))
    gam = jnp.exp(log_gamma[:, :, 0, 0] * c)
    fwd = lambda b, n: (b, n, 0)
    bwd = lambda b, n: (b, _backward_chunk(n, ncc, nc), 0)
    o_fwd = lambda b, n: (b, jnp.maximum(n - ncc, 0), 0)
    o_bwd = lambda b, n: (b, nl - 1 - jnp.maximum(n - ncc, 0), 0)

    def specs(idx):
        return [pl.BlockSpec((rb, c, RET_QK), idx), pl.BlockSpec((rb, c, RET_QK), idx), pl.BlockSpec((rb, c, RET_MIX), idx)]

    return pl.pallas_call(
        _ret_body,
        grid=(bsz // rb, nc),
        in_specs=specs(fwd) + specs(bwd) + [_const_spec(dmat.shape), _const_spec(rsc.shape), _const_spec(csc.shape),
                                            pl.BlockSpec(memory_space=pltpu.SMEM)],
        out_specs=[pl.BlockSpec((rb, c, RET_MIX), o_fwd), pl.BlockSpec((rb, c, RET_MIX), o_bwd)],
        out_shape=[jax.ShapeDtypeStruct((bsz, nl * c, RET_MIX), BF16)] * 2,
        scratch_shapes=[pltpu.VMEM((rb, RET_HEADS, RET_DK, RET_DV), F32)] * 2,
        compiler_params=_params("arbitrary", "arbitrary"),
        name="retention_scan",
    )(q, k, v, q, k, v, dmat, rsc, csc, gam)


def _zero_counts_at_start(cnt_ref):
    @pl.when(_first_step())
    def _():
        cnt_ref[...] = jnp.zeros_like(cnt_ref)


def _route(xs, mixed, mods, n2g_ref, wr_ref, br_ref, x1_ref, h2_ref, e_ref, w_ref, r_ref, cnt_ref):
    rows = xs[0].shape[0]
    h2s = []
    for bb, (x, mod) in enumerate(zip(xs, mods)):
        x1 = x + mod[2:3] * mixed[bb * rows:(bb + 1) * rows]
        x1_ref[bb] = x1
        h2s.append(_norm_mod(x1, n2g_ref[...], mod[3:4], mod[4:5]))
        h2_ref[bb] = _pack_rows(h2s[bb])
    h2 = h2s[0] if len(h2s) == 1 else jnp.concatenate(h2s, axis=0)
    tm = h2.shape[0]
    logits = _dot3(wr_ref[...], h2, dot=_dot_nt) + br_ref[...]
    ie = lax.broadcasted_iota(I32, logits.shape, 0)
    tops, picks = [], []
    for _ in range(TOP_K):
        mx = jnp.max(logits, axis=0, keepdims=True)
        pick = jnp.min(jnp.where(logits == mx, ie, N_EXPERTS), axis=0, keepdims=True)
        tops.append(mx)
        picks.append(pick)
        logits = jnp.where(ie == pick, -jnp.inf, logits)
    ex = [jnp.exp(tk - tops[0]) for tk in tops]
    den = ex[0] + ex[1] + ex[2] + ex[3]
    for kk in range(TOP_K):
        w_ref[0, kk:kk + 1, :] = ex[kk] / den
        e_ref[0, kk:kk + 1, :] = picks[kk]

    earlier = (lax.broadcasted_iota(I32, (tm, tm), 0) < lax.broadcasted_iota(I32, (tm, tm), 1))
    earlier = jnp.where(earlier, 1.0, 0.0).astype(BF16)
    run = cnt_ref[:, 0:1]
    for kk, pick in enumerate(picks):
        onehot = jnp.where(ie == pick, 1.0, 0.0)
        before = _dot(onehot.astype(BF16), earlier) + run
        r_ref[0, kk:kk + 1, :] = jnp.sum(onehot * before, axis=0, keepdims=True).astype(I32)
        run = run + jnp.sum(onehot, axis=1, keepdims=True)
    cnt_ref[...] = jnp.broadcast_to(run, cnt_ref.shape)


def _mix0_body(nct, ctx_ref, lat_ref, mod_ref, of_ref, ob_ref, g_ref, ys_ref, u_ref, gng_ref, dsk_ref, gluw_ref,
               glub_ref, wo_ref, n2g_ref, wr_ref, br_ref, x1_ref, h2_ref, e_ref, w_ref, r_ref, cnt_ref):
    _zero_counts_at_start(cnt_ref)
    tb, tm = of_ref.shape[0], of_ref.shape[1]
    rows = lambda ref: ref[...].reshape(tb * tm, ref.shape[-1])
    o = rows(of_ref) + rows(ob_ref)
    heads = []
    for h in range(GLA_HEADS):
        oh = o[:, h * GLA_DV:(h + 1) * GLA_DV]
        heads.append(oh * lax.rsqrt(jnp.mean(oh * oh, axis=-1, keepdims=True) + EPS))
    gla = jnp.concatenate(heads, axis=1) * gng_ref[...] * _silu(rows(g_ref).astype(F32))
    lane_blocks = lambda ref: jnp.concatenate([ref[qb].reshape(tb * tm, LANES) for qb in range(ref.shape[0])], axis=1)
    y = jax.nn.gelu(lane_blocks(ys_ref) + dsk_ref[...] * lane_blocks(u_ref))
    y = y * jax.nn.sigmoid(_dot(y.astype(BF16), gluw_ref[...]) + glub_ref[...])
    mixed = _dot(gla.astype(BF16), wo_ref[0:AB_V]) + _dot(y.astype(BF16), wo_ref[AB_V:AB_V + S5_CH])
    _route([_stream_tile(nct, ctx_ref, lat_ref, bb) for bb in range(tb)], mixed, [mod_ref[bb, 0] for bb in range(tb)],
           n2g_ref, wr_ref, br_ref, x1_ref, h2_ref, e_ref, w_ref, r_ref, cnt_ref)


def _mix1_body(x_ref, mod_ref, of_ref, ob_ref, g_ref, ng_ref, wo_ref, n2g_ref, wr_ref, br_ref,
               x1_ref, h2_ref, e_ref, w_ref, r_ref, cnt_ref):
    _zero_counts_at_start(cnt_ref)
    mixed = None
    for h in range(RET_HEADS):
        sl = slice(h * RET_DV, (h + 1) * RET_DV)
        oh = of_ref[0, :, sl].astype(F32) + ob_ref[0, :, sl].astype(F32)
        mu = jnp.mean(oh, axis=-1, keepdims=True)
        cen = oh - mu
        var = jnp.mean(cen * cen, axis=-1, keepdims=True)
        gated = cen * lax.rsqrt(var + EPS) * ng_ref[:, sl] * _silu(g_ref[0, :, sl].astype(F32))
        part = _dot(gated.astype(BF16), wo_ref[sl])
        mixed = part if mixed is None else mixed + part
    _route([x_ref[0]], mixed, [mod_ref[0, 0]], n2g_ref, wr_ref, br_ref, x1_ref, h2_ref, e_ref, w_ref, r_ref, cnt_ref)


def _mix_call(body, name, stream, mods, tiles, acts, consts, norm2_g, w_router, b_router, n_tok, seg_tile0,
              tm=TOKEN_TILE, tb=1):
    bsz, _, d = stream[-1].shape
    off = lambda b, i: (b, i + seg_tile0, 0)
    loc = lambda b, i: (b, i, 0)
    ntl = bsz // tb * tiles
    flat = lambda b, i: (b * tiles + i, 0, 0)
    in_specs = list(_split_specs(n_tok, d, tb)) if len(stream) == 2 else [pl.BlockSpec((tb, tm, d), off)]
    in_specs.append(pl.BlockSpec((tb, 1, 6, d), lambda b, i: (b, ((i + seg_tile0) >= n_tok).astype(I32), 0, 0)))
    args = list(stream) + [mods]
    for arr, offset in acts:
        if arr.ndim == 4:
            in_specs.append(pl.BlockSpec((arr.shape[0], tb, tm, arr.shape[3]), lambda b, i: (0, b, i, 0)))
        else:
            in_specs.append(pl.BlockSpec((tb, tm, arr.shape[2]), off if offset else loc))
        args.append(arr)
    tail = list(consts) + [norm2_g.reshape(1, d), w_router.T, b_router.reshape(N_EXPERTS, 1)]
    in_specs += [_const_spec(a.shape) for a in tail]
    args += tail
    tok_out = pl.BlockSpec((1, TOP_K, tb * tm), flat)
    return pl.pallas_call(
        body,
        grid=(bsz // tb, tiles),
        in_specs=in_specs,
        out_specs=[pl.BlockSpec((tb, tm, d), loc), pl.BlockSpec((tb, tm, d // 2), loc), tok_out, tok_out, tok_out,
                   _const_spec((N_EXPERTS, LANES))],
        out_shape=[jax.ShapeDtypeStruct((bsz, tiles * tm, d), F32), jax.ShapeDtypeStruct((bsz, tiles * tm, d // 2), U32),
                   jax.ShapeDtypeStruct((ntl, TOP_K, tb * tm), I32), jax.ShapeDtypeStruct((ntl, TOP_K, tb * tm), F32),
                   jax.ShapeDtypeStruct((ntl, TOP_K, tb * tm), I32), jax.ShapeDtypeStruct((N_EXPERTS, LANES), F32)],
        compiler_params=_params("arbitrary", "arbitrary"),
        name=name,
    )(*args)


def _cast_rows(src_ref, dst_ref, rows):
    def chunk(j, carry):
        r = pl.multiple_of(j * rows, rows)
        dst_ref[pl.ds(r, rows), :] = src_ref[pl.ds(r, rows), :].astype(BF16)
        return carry

    lax.fori_loop(0, dst_ref.shape[0] // rows, chunk, 0)


def _expert_mlp(x, wgu_bf, bgu_ref, wd_bf, bd_ref):
    x_lo, x_hi = _unpack_rows(x)
    half = x_lo.shape[1]
    gu = (_dot(x_lo.astype(BF16), wgu_bf[0:half]) + _dot(x_hi.astype(BF16), wgu_bf[half:2 * half])
          + bgu_ref[0, 0])
    gate = jnp.minimum(gu[:, :D_FF], SWIGLU_LIMIT)
    lin = jnp.clip(gu[:, D_FF:], -SWIGLU_LIMIT, SWIGLU_LIMIT)
    act = gate * jax.nn.sigmoid(SWIGLU_ALPHA * gate) * (lin + 1.0)
    return _pack_rows(_dot(act.astype(BF16), wd_bf[...]) + bd_ref[0, 0])


def _expert_body(layer, be_ref, rows_ref, slot_ref, next_ref, x_ref, wgu_hbm, bgu_ref, wd_hbm, bd_ref, o_ref,
                 wgu_f32, wd_f32, wgu_bf, wd_bf, sems):
    i = pl.program_id(0)
    rows = rows_ref[i]
    part = x_ref.shape[0] // EXPERT_BLOCK_PARTS
    new_expert = (i == 0) | (be_ref[i] != be_ref[jnp.maximum(i - 1, 0)])

    def weight_copies(expert, slot):
        return (pltpu.make_async_copy(wgu_hbm.at[layer, expert], wgu_f32.at[slot], sems.at[0, slot]),
                pltpu.make_async_copy(wd_hbm.at[layer, expert], wd_f32.at[slot], sems.at[1, slot]))

    @pl.when((rows > 0) & new_expert)
    def _():
        slot = slot_ref[i]

        @pl.when(i == 0)
        def _():
            for cp in weight_copies(be_ref[i], slot):
                cp.start()

        @pl.when(next_ref[i] >= 0)
        def _():
            for cp in weight_copies(next_ref[i], 1 - slot):
                cp.start()

        for cp in weight_copies(be_ref[i], slot):
            cp.wait()
        _cast_rows(wgu_f32.at[slot], wgu_bf, 128)
        _cast_rows(wd_f32.at[slot], wd_bf, 128)

    for p in range(1, EXPERT_BLOCK_PARTS + 1):
        @pl.when((rows > (p - 1) * part) & (rows <= p * part))
        def _():
            o_ref[0:p * part, :] = _expert_mlp(x_ref[0:p * part, :], wgu_bf, bgu_ref, wd_bf, bd_ref)
            if p < EXPERT_BLOCK_PARTS:
                o_ref[p * part:, :] = jnp.zeros((o_ref.shape[0] - p * part, o_ref.shape[1]), o_ref.dtype)

    @pl.when(rows == 0)
    def _():
        o_ref[...] = jnp.zeros_like(o_ref)


def _experts(xb, block_e, block_rows, block_slot, block_next, layer, w_gu, b_gu, w_down, b_down):
    n_slots, half = xb.shape
    d = 2 * half
    n_blocks = n_slots // MOE_BLOCK
    depth = w_gu.shape[0]
    by_expert = lambda i, be, *_: (layer, be[i], 0, 0)
    rows_spec = pl.BlockSpec((MOE_BLOCK, half), lambda i, *_: (i, 0))
    return pl.pallas_call(
        functools.partial(_expert_body, layer),
        grid_spec=pltpu.PrefetchScalarGridSpec(
            num_scalar_prefetch=4,
            grid=(n_blocks,),
            in_specs=[rows_spec, pl.BlockSpec(memory_space=pl.ANY), pl.BlockSpec((1, 1, 1, 2 * D_FF), by_expert),
                      pl.BlockSpec(memory_space=pl.ANY), pl.BlockSpec((1, 1, 1, d), by_expert)],
            out_specs=rows_spec,
            scratch_shapes=[pltpu.VMEM((2, d, 2 * D_FF), F32), pltpu.VMEM((2, D_FF, d), F32),
                            pltpu.VMEM((d, 2 * D_FF), BF16), pltpu.VMEM((D_FF, d), BF16),
                            pltpu.SemaphoreType.DMA((2, 2))]),
        out_shape=jax.ShapeDtypeStruct((n_slots, half), U32),
        compiler_params=_params("arbitrary"),
        name="moe_experts",
    )(block_e, block_rows, block_slot, block_next, xb, w_gu, b_gu.reshape(depth, N_EXPERTS, 1, 2 * D_FF), w_down,
      b_down.reshape(depth, N_EXPERTS, 1, d))


def _combine_body(x1_ref, mod_ref, yk_ref, w_ref, fg_ref, o_ref):
    d = x1_ref.shape[2]
    half = d // 2
    x2_lo, x2_hi = _moe_residual(x1_ref, mod_ref[0, 0][5:6], yk_ref, w_ref)
    ms = (jnp.sum(x2_lo * x2_lo, axis=-1, keepdims=True) + jnp.sum(x2_hi * x2_hi, axis=-1, keepdims=True)) / d
    r = lax.rsqrt(ms + EPS)
    o_ref[0, :, 0:half] = x2_lo * r * fg_ref[:, 0:half]
    o_ref[0, :, half:d] = x2_hi * r * fg_ref[:, half:d]


def _combine(x1, mods, yk, w_tok, seg_tile0, n_tok, final_g, tm):
    bsz, t, d = x1.shape
    loc = lambda b, i: (b, i, 0)
    return pl.pallas_call(
        _combine_body,
        grid=(bsz, t // tm),
        in_specs=[pl.BlockSpec((1, tm, d), loc),
                  pl.BlockSpec((1, 1, 6, d), lambda b, i: (b, ((i + seg_tile0) >= n_tok).astype(I32), 0, 0)),
                  pl.BlockSpec((TOP_K, 1, tm, d // 2), lambda b, i: (0, b, i, 0)),
                  pl.BlockSpec((1, tm, TOP_K), loc), _const_spec((1, d))],
        out_specs=pl.BlockSpec((1, tm, d), loc),
        out_shape=jax.ShapeDtypeStruct((bsz, t, d), F32),
        compiler_params=_params("arbitrary", "arbitrary"),
        name="moe_combine",
    )(x1, mods, yk.reshape(TOP_K, bsz, t, d // 2), w_tok.reshape(bsz, t, TOP_K), final_g.reshape(1, d))


def _sc_mesh():
    return plsc.VectorSubcoreMesh(core_axis_name="core", subcore_axis_name="subcore",
                                  num_cores=SC_CORES, num_subcores=SC_SUBCORES)


def _sc_worker_base(per_worker):
    return (lax.axis_index("subcore") * SC_CORES + lax.axis_index("core")) * per_worker


def _sc_dispatch(rows, dest, n_slots):
    n, w = rows.shape
    per_worker = n // SC_WORKERS
    assert per_worker * SC_WORKERS == n and per_worker % SC_CHUNK == 0

    @functools.partial(
        pl.kernel, mesh=_sc_mesh(), out_type=jax.ShapeDtypeStruct((n_slots, w), rows.dtype),
        scratch_types=[pltpu.VMEM((SC_CHUNK,), I32)] * TOP_K + [pltpu.VMEM((SC_CHUNK, w), rows.dtype),
                                                                pltpu.SemaphoreType.DMA],
        name="moe_dispatch")
    def scatter_rows(rows_hbm, dest_hbm, out_hbm, *scratch):
        idx_refs, buf, sem = scratch[:TOP_K], scratch[TOP_K], scratch[TOP_K + 1]
        base0 = _sc_worker_base(per_worker)

        @pl.loop(0, per_worker // SC_CHUNK)
        def _(j):
            base = base0 + j * SC_CHUNK
            pltpu.sync_copy(rows_hbm.at[pl.ds(base, SC_CHUNK)], buf)
            for k, idx in enumerate(idx_refs):
                pltpu.sync_copy(dest_hbm.at[pl.ds(k * n + base, SC_CHUNK)], idx)
            copies = [pltpu.make_async_copy(buf, out_hbm.at[idx], sem) for idx in idx_refs]
            for cp in copies:
                cp.start()
            for cp in copies:
                cp.wait()

    return scatter_rows(rows, dest)


def _sc_gather(table, idx):
    n = idx.shape[0]
    w = table.shape[1]
    per_worker = n // SC_WORKERS
    n_chunks = per_worker // SC_CHUNK
    assert per_worker * SC_WORKERS == n and n_chunks * SC_CHUNK == per_worker and n_chunks % 2 == 0

    @functools.partial(
        pl.kernel, mesh=_sc_mesh(), out_type=jax.ShapeDtypeStruct((n, w), table.dtype),
        scratch_types=([pltpu.VMEM((SC_CHUNK,), I32)] * 2 + [pltpu.VMEM((SC_CHUNK, w), table.dtype)] * 2
                       + [pltpu.SemaphoreType.DMA] * 4),
        name="moe_gather")
    def gather_rows(table_hbm, idx_hbm, out_hbm, idx0, idx1, buf0, buf1, gsem0, gsem1, wsem0, wsem1):
        base0 = _sc_worker_base(per_worker)

        def gather_copy(idx_v, buf, sem):
            return pltpu.make_async_copy(table_hbm.at[idx_v], buf, sem)

        def write_copy(j, buf, sem):
            return pltpu.make_async_copy(buf, out_hbm.at[pl.ds(base0 + j * SC_CHUNK, SC_CHUNK)], sem)

        def start_gather(j, idx_v, buf, sem):
            pltpu.sync_copy(idx_hbm.at[pl.ds(base0 + j * SC_CHUNK, SC_CHUNK)], idx_v)
            gather_copy(idx_v, buf, sem).start()

        start_gather(0, idx0, buf0, gsem0)

        @pl.loop(0, n_chunks, step=2)
        def _(j):
            @pl.when(j > 0)
            def _():
                write_copy(j - 1, buf1, wsem1).wait()
            start_gather(j + 1, idx1, buf1, gsem1)
            gather_copy(idx0, buf0, gsem0).wait()
            write_copy(j, buf0, wsem0).start()

            @pl.when(j + 2 < n_chunks)
            def _():
                write_copy(j, buf0, wsem0).wait()
                start_gather(j + 2, idx0, buf0, gsem0)
            gather_copy(idx1, buf1, gsem1).wait()
            write_copy(j + 1, buf1, wsem1).start()

        write_copy(n_chunks - 2, buf0, wsem0).wait()
        write_copy(n_chunks - 1, buf1, wsem1).wait()

    return gather_rows(table, idx)


def _moe(h2, e_tl, w_tl, r_tl, cnt, layer, w_gu, b_gu, w_down, b_down, tb=1):
    bsz, t, half = h2.shape
    n = bsz * t
    flat = lambda a: a.reshape(bsz // tb, -1, TOP_K, tb, a.shape[2] // tb).transpose(2, 0, 3, 1, 4).reshape(TOP_K, n)
    e_k, w_k, r_k = flat(e_tl), flat(w_tl), flat(r_tl)
    counts = cnt[:, 0].astype(I32)
    padded = (counts + MOE_BLOCK - 1) // MOE_BLOCK * MOE_BLOCK
    pad_end = jnp.cumsum(padded)
    pad_start = pad_end - padded
    n_blocks = (n * TOP_K + MOE_BLOCK - 1) // MOE_BLOCK + N_EXPERTS
    block_start = jnp.arange(n_blocks, dtype=I32) * MOE_BLOCK
    block_e = jnp.minimum(jnp.sum((pad_end[None, :] <= block_start[:, None]).astype(I32), axis=1), N_EXPERTS - 1)
    block_rows = jnp.clip(counts[block_e] - (block_start - pad_start[block_e]), 0, MOE_BLOCK).astype(I32)
    experts = jnp.arange(N_EXPERTS, dtype=I32)
    busy = padded > 0
    block_slot = ((jnp.cumsum(busy.astype(I32)) - 1) % 2)[block_e].astype(I32)
    later = jnp.where(busy[None, :] & (experts[None, :] > experts[:, None]), experts[None, :], N_EXPERTS)
    next_busy = jnp.min(later, axis=1)
    block_next = jnp.where(next_busy < N_EXPERTS, next_busy, -1)[block_e].astype(I32)
    start_k = jnp.sum(jnp.where(e_k[..., None] == jnp.arange(N_EXPERTS, dtype=I32), pad_start, 0), axis=-1)
    dest = (start_k + r_k).reshape(TOP_K * n)
    xb = _sc_dispatch(h2.reshape(n, half), dest, n_blocks * MOE_BLOCK)
    yb = _experts(xb, block_e, block_rows, block_slot, block_next, layer, w_gu, b_gu, w_down, b_down)
    return _sc_gather(yb, dest), w_k.T


def _rope_tables(n_ctx, n_lat):
    n_freq = RET_DK // 4
    inv_freq = ROPE_BASE ** (-jnp.arange(n_freq, dtype=F32) / n_freq)
    pos = jnp.arange(n_lat, dtype=I32)
    cos, sin = [], []
    for p in (pos // GRID_W, pos % GRID_W):
        ang = p.astype(F32)[:, None] * inv_freq
        cos += [jnp.cos(ang), jnp.cos(ang)]
        sin += [-jnp.sin(ang), jnp.sin(ang)]
    cos, sin = jnp.concatenate(cos, axis=1), jnp.concatenate(sin, axis=1)
    return (jnp.concatenate([jnp.ones((n_ctx, RET_DK), F32), cos], axis=0),
            jnp.concatenate([jnp.zeros((n_ctx, RET_DK), F32), sin], axis=0))


def kernel(x, c, ctx, c_ctx, ada_w, ada_b, norm1_g, norm2_g, ab_w_in, ab_w_out, gla_wa, gla_ba, gla_norm_g, s5_lam_re, s5_lam_im, s5_log_step, s5_b_re, s5_b_im, s5_c_re, s5_c_im, s5_d, s5_glu_w, s5_glu_b, ret_w_in, ret_w_out, ret_decay_logit, ret_norm_g, moe_w_router, moe_b_router, moe_w_gu, moe_b_gu, moe_w_down, moe_b_down, final_norm_g):
    bsz, n_lat, d = x.shape
    n_ctx = ctx.shape[1]
    depth = ada_w.shape[0]
    assert depth == 2 and d == D_MODEL and bsz == 8, "kernels are laid out for the stated problem shape"
    assert n_ctx % TOKEN_TILE == 0 and n_lat % LATENT_TILE == 0 and n_lat % GRID_W == 0
    t = n_ctx + n_lat
    nct = n_ctx // TOKEN_TILE

    cvec = jnp.zeros((16, d), F32).at[:bsz].set(c).at[bsz].set(c_ctx)
    mod = _ada_mod(cvec, ada_w, ada_b).reshape(depth, 16, 6, d)
    mods = [jnp.stack([jnp.broadcast_to(mod[l, bsz], (bsz, 6, d)), mod[l, :bsz]], axis=1) for l in range(depth)]

    w_in = ab_w_in[0].astype(BF16)
    cuts = [0, AB_QK, 2 * AB_QK, 2 * AB_QK + AB_V, 2 * AB_QK + 2 * AB_V, 2 * AB_QK + 2 * AB_V + 2 * GLA_RANK,
            w_in.shape[1]]
    pieces = [w_in[:, a:b] for a, b in zip(cuts[:-1], cuts[1:])]
    wa_pad = jnp.zeros((2, 2 * GLA_RANK, AB_QK), F32)
    wa_pad = wa_pad.at[0, :GLA_RANK].set(gla_wa[0, 0]).at[1, GLA_RANK:].set(gla_wa[0, 1])
    outs = _inproj0(ctx, x, mods[0], norm1_g[0], pieces, wa_pad, gla_ba[0].reshape(2, 1, AB_QK), nct)
    v, g, u = outs[8:]
    o_f, o_b = _gla((outs[0:4], outs[4:8]), v, n_ctx)
    ops = _s5_operators(s5_lam_re[0], s5_lam_im[0], s5_log_step[0], s5_b_re[0], s5_b_im[0], s5_c_re[0], s5_c_im[0])
    ys = _s5(u, ops, n_ctx)
    consts = [jnp.tile(gla_norm_g[0], GLA_HEADS).reshape(1, AB_V), s5_d[0].reshape(1, S5_CH),
              s5_glu_w[0].astype(BF16), s5_glu_b[0].reshape(1, S5_CH), ab_w_out[0].astype(BF16)]
    x1, h2, e_tl, w_tl, r_tl, cnt = _mix_call(
        functools.partial(_mix0_body, nct), "mix_gla_s5", (ctx, x), mods[0], t // TOKEN_TILE,
        [(o_f, False), (o_b, False), (g, False), (ys, False), (u, False)], consts,
        norm2_g[0], moe_w_router[0], moe_b_router[0], nct, 0, tb=TOKEN_TILE_BATCH)
    yk, w_tok = _moe(h2, e_tl, w_tl, r_tl, cnt, 0, moe_w_gu, moe_b_gu, moe_w_down, moe_b_down, tb=TOKEN_TILE_BATCH)

    w_in = ret_w_in[0].astype(BF16)
    cuts = [0, RET_QK, 2 * RET_QK, 2 * RET_QK + RET_MIX, w_in.shape[1]]
    pieces = [w_in[:, a:b] for a, b in zip(cuts[:-1], cuts[1:])]
    cos_t, sin_t = _rope_tables(n_ctx, n_lat)
    x2, q, k, v, g = _inproj1(x1, mods[0], yk, w_tok, mods[1], norm1_g[1], cos_t, sin_t, pieces, nct)
    o_f, o_b = _retention(q, k, v, ret_decay_logit[0], n_ctx)
    consts = [ret_norm_g[0].reshape(1, RET_MIX), ret_w_out[0].astype(BF16)]
    x1, h2, e_tl, w_tl, r_tl, cnt = _mix_call(
        _mix1_body, "mix_retention", (x2,), mods[1], n_lat // LATENT_TILE,
        [(o_f, False), (o_b, False), (g, False)], consts,
        norm2_g[1], moe_w_router[1], moe_b_router[1], 0, 0, tm=LATENT_TILE)
    yk, w_tok = _moe(h2, e_tl, w_tl, r_tl, cnt, 1, moe_w_gu, moe_b_gu, moe_w_down, moe_b_down)
    return _combine(x1, mods[1], yk, w_tok, 0, 0, final_norm_g, LATENT_TILE)
```

```python
import functools
import math

import jax
import jax.numpy as jnp
from jax import lax
from jax.experimental import pallas as pl
from jax.experimental.pallas import tpu as pltpu
from jax.experimental.pallas import tpu_sc as plsc

F32, BF16, I32, U32 = jnp.float32, jnp.bfloat16, jnp.int32, jnp.uint32

D_MODEL = 1024
GRID_W = 64
EPS = 1e-6
GLA_HEADS, GLA_DK, GLA_DV, GLA_RANK, GLA_TAU, GLA_CHUNK = 4, 64, 128, 16, 16.0, 64
GLA_BATCH = 8
AB_QK, AB_V = GLA_HEADS * GLA_DK, GLA_HEADS * GLA_DV
S5_CH, S5_GROUP, S5_GROUPS, S5_P = 512, 16, 32, 64
S5_CHUNK = 16
S5_FOLD_BATCH = 4
S5_SCAN_GROUPS = 2
RET_HEADS, RET_DK, RET_DV = 4, 256, 512
RET_CHUNK = 256
RET_BATCH = 2
RET_QK, RET_MIX = RET_HEADS * RET_DK, RET_HEADS * RET_DV
ROPE_BASE = 10000.0
N_EXPERTS, TOP_K, D_FF = 32, 4, 1024
SWIGLU_LIMIT, SWIGLU_ALPHA = 7.0, 1.702
MOE_BLOCK = 1024
EXPERT_BLOCK_PARTS = 4
TOKEN_TILE = 256
LATENT_TILE = 512
TOKEN_TILE_BATCH = 2
INPROJ0_TILE_BATCH = 4
ADA_TILE = 768
VMEM_LIMIT = 56 * 1024 * 1024
SC_CORES, SC_SUBCORES = 2, 16
SC_WORKERS = SC_CORES * SC_SUBCORES
SC_CHUNK = 64
LANES = 128

def _params(*sem):
    return pltpu.CompilerParams(dimension_semantics=sem, vmem_limit_bytes=VMEM_LIMIT)


def _dot(a, b):
    return jnp.dot(a, b, preferred_element_type=F32)


def _dot_nt(a, b):
    return lax.dot_general(a, b, (((1,), (1,)), ((), ())), preferred_element_type=F32)


def _dot_tn(a, b):
    return lax.dot_general(a, b, (((0,), (0,)), ((), ())), preferred_element_type=F32)


def _split(a):
    hi = a.astype(BF16)
    return hi, (a - hi.astype(F32)).astype(BF16)


def _dot3(a, b, dot=_dot):
    ah, al = _split(a)
    bh, bl = _split(b)
    return dot(ah, bh) + (dot(ah, bl) + dot(al, bh))


def _pack_rows(x):
    h = x.shape[1] // 2
    lo = lax.bitcast_convert_type(x[:, 0:h].astype(BF16).astype(F32), U32)
    hi = lax.bitcast_convert_type(x[:, h:2 * h].astype(BF16).astype(F32), U32)
    return hi | (lo >> 16)


def _unpack_rows(p):
    lo = lax.bitcast_convert_type(p << 16, F32)
    hi = lax.bitcast_convert_type(p & jnp.uint32(0xFFFF0000), F32)
    return lo, hi


def _silu(x):
    return x * jax.nn.sigmoid(x)


def _norm_mod(x, g, shift, scale):
    r = lax.rsqrt(jnp.mean(x * x, axis=-1, keepdims=True) + EPS)
    return (x * r * g) * (1.0 + scale) + shift


def _const_spec(shape):
    nd = len(shape)
    return pl.BlockSpec(shape, lambda *_: (0,) * nd, pipeline_mode=pl.Buffered(1))


def _ada_body(c_ref, w_ref, b_ref, o_ref):
    o_ref[0] = _dot3(_silu(c_ref[...]), w_ref[0]) + b_ref[0]


def _ada_mod(cvec, ada_w, ada_b):
    depth, d, n6 = ada_w.shape
    rows = cvec.shape[0]
    return pl.pallas_call(
        _ada_body,
        grid=(depth, n6 // ADA_TILE),
        in_specs=[_const_spec((rows, d)),
                  pl.BlockSpec((1, d, ADA_TILE), lambda l, j: (l, 0, j)),
                  pl.BlockSpec((1, 1, ADA_TILE), lambda l, j: (l, 0, j))],
        out_specs=pl.BlockSpec((1, rows, ADA_TILE), lambda l, j: (l, 0, j)),
        out_shape=jax.ShapeDtypeStruct((depth, rows, n6), F32),
        compiler_params=_params("arbitrary", "arbitrary"),
        name="ada_mod",
    )(cvec, ada_w, ada_b.reshape(depth, 1, n6))


def _split_specs(nct, d, tb=1):
    ctx_spec = pl.BlockSpec((tb, TOKEN_TILE, d), lambda b, i: (b, jnp.minimum(i, nct - 1), 0))
    lat_spec = pl.BlockSpec((tb, TOKEN_TILE, d), lambda b, i: (b, jnp.maximum(i - nct, 0), 0))
    return ctx_spec, lat_spec


def _stream_tile(nct, ctx_ref, lat_ref, bb=0):
    return jnp.where(pl.program_id(1) < nct, ctx_ref[bb], lat_ref[bb])


def _normed_rows(nct, ctx_ref, lat_ref, mod_ref, g_ref):
    tiles = []
    for bb in range(mod_ref.shape[0]):
        m = mod_ref[bb, 0]
        tiles.append(_norm_mod(_stream_tile(nct, ctx_ref, lat_ref, bb), g_ref[...], m[0:1], m[1:2]).astype(BF16))
    return tiles[0] if len(tiles) == 1 else jnp.concatenate(tiles, axis=0)


def _store_rows(o_ref, val):
    tm = o_ref.shape[1]
    for bb in range(o_ref.shape[0]):
        o_ref[bb] = val[bb * tm:(bb + 1) * tm].astype(o_ref.dtype)


def _inproj0_body(nct, ctx_ref, lat_ref, mod_ref, g_ref, wq, wk, wv, wg, wlow, wu, wa_ref, ba_ref, tri_ref, ones_ref,
                  qd_f, ki_f, ks_f, ed_f, qd_b, ki_b, ks_b, ed_b, ov, og, ou):
    h = _normed_rows(nct, ctx_ref, lat_ref, mod_ref, g_ref)
    tm = ctx_ref.shape[1]
    low = _dot(h, wlow[...])
    q = _dot(h, wq[...]) * (GLA_DK ** -0.5)
    k = _dot(h, wk[...])
    outs = ((qd_f, ki_f, ks_f, ed_f), (qd_b, ki_b, ks_b, ed_b))
    for d, (qd_ref, ki_ref, ks_ref, ed_ref) in enumerate(outs):
        z = _dot3(low, wa_ref[d]) + ba_ref[d]
        log_a = (jnp.minimum(z, 0.0) - jnp.log1p(jnp.exp(-jnp.abs(z)))) * (1.0 / GLA_TAU)
        la_hi, la_lo = _split(log_a)
        cums, tots = [], []
        for bb in range(qd_ref.shape[0]):
            hi, lo = la_hi[bb * tm:(bb + 1) * tm], la_lo[bb * tm:(bb + 1) * tm]
            cums.append(_dot(tri_ref[d], hi) + _dot(tri_ref[d], lo))
            tots.append(_dot(ones_ref[...], hi) + _dot(ones_ref[...], lo))
        cum = cums[0] if len(cums) == 1 else jnp.concatenate(cums, axis=0)
        tot = tots[0] if len(tots) == 1 else jnp.concatenate(tots, axis=0)
        _store_rows(qd_ref, q * jnp.exp(cum))
        _store_rows(ki_ref, k * jnp.exp(-cum))
        _store_rows(ks_ref, k * jnp.exp(tot - cum))
        for bb in range(ed_ref.shape[0]):
            for ch in range(tm // GLA_CHUNK):
                row = bb * tm + ch * GLA_CHUNK
                ed_ref[bb, ch] = jnp.exp(tot[row:row + 1])
    _store_rows(ov, _dot(h, wv[...]))
    _store_rows(og, _dot(h, wg[...]))
    u = _dot(h, wu[...])
    for qb in range(ou.shape[0]):
        _store_rows(ou.at[qb], u[:, qb * LANES:(qb + 1) * LANES])


def _inproj0(ctx, x, mods, norm_g, weights, wa_pad, ba, nct):
    bsz, n_lat, d = x.shape
    t = ctx.shape[1] + n_lat
    tm, tb = TOKEN_TILE, INPROJ0_TILE_BATCH
    mod_spec = pl.BlockSpec((tb, 1, 6, d), lambda b, i: (b, (i >= nct).astype(I32), 0, 0))
    ctx_spec, lat_spec = _split_specs(nct, d, tb)
    pos = jnp.arange(tm)
    same_chunk = (pos[:, None] // GLA_CHUNK) == (pos[None, :] // GLA_CHUNK)
    tri = jnp.stack([same_chunk & (pos[None, :] <= pos[:, None]),
                     same_chunk & (pos[None, :] >= pos[:, None])]).astype(BF16)
    ones = same_chunk.astype(BF16)
    consts = list(weights) + [wa_pad, ba, tri, ones]
    tok = lambda w, dt: (pl.BlockSpec((tb, tm, w), lambda b, i: (b, i, 0)), jax.ShapeDtypeStruct((bsz, t, w), dt))
    per_chunk = (pl.BlockSpec((tb, tm // GLA_CHUNK, 1, AB_QK), lambda b, i: (b, i, 0, 0)),
                 jax.ShapeDtypeStruct((bsz, t // GLA_CHUNK, 1, AB_QK), F32))
    one_dir = [tok(AB_QK, BF16)] * 3 + [per_chunk]
    u_blocks = (pl.BlockSpec((S5_CH // LANES, tb, tm, LANES), lambda b, i: (0, b, i, 0)),
                jax.ShapeDtypeStruct((S5_CH // LANES, bsz, t, LANES), F32))
    outs = one_dir + one_dir + [tok(AB_V, BF16), tok(AB_V, BF16), u_blocks]
    return pl.pallas_call(
        functools.partial(_inproj0_body, nct),
        grid=(bsz // tb, t // tm),
        in_specs=[ctx_spec, lat_spec, mod_spec, _const_spec((1, d))] + [_const_spec(a.shape) for a in consts],
        out_specs=[o[0] for o in outs],
        out_shape=[o[1] for o in outs],
        compiler_params=_params("arbitrary", "arbitrary"),
        name="inproj_gla_s5",
    )(ctx, x, mods, norm_g.reshape(1, d), *consts)


def _rope(acc, cos_ref, sin_ref, o_ref, scale):
    tm = o_ref.shape[1]
    for grp in range(acc.shape[1] // LANES):
        half = grp % 2
        cs = cos_ref[:, half * LANES:(half + 1) * LANES]
        sn = sin_ref[:, half * LANES:(half + 1) * LANES]
        for bb in range(o_ref.shape[0]):
            xg = acc[bb * tm:(bb + 1) * tm, grp * LANES:(grp + 1) * LANES]
            out = xg * cs + pltpu.roll(xg, LANES // 2, 1) * sn
            o_ref[bb, :, grp * LANES:(grp + 1) * LANES] = (out * scale).astype(o_ref.dtype)


def _moe_residual(x1_ref, g2, yk_ref, w_ref, bb=0):
    d = x1_ref.shape[2]
    half = d // 2
    y_lo, y_hi = None, None
    for k in range(TOP_K):
        lo, hi = _unpack_rows(yk_ref[k, bb])
        wk = w_ref[bb, :, k:k + 1]
        y_lo = lo * wk if y_lo is None else y_lo + lo * wk
        y_hi = hi * wk if y_hi is None else y_hi + hi * wk
    return x1_ref[bb, :, 0:half] + g2[:, 0:half] * y_lo, x1_ref[bb, :, half:d] + g2[:, half:d] * y_hi


def _inproj1_body(x1_ref, mod0_ref, yk_ref, w_ref, mod_ref, g_ref, cos_ref, sin_ref, wq, wk, wv, wg, ox, oq, ok, ov, og):
    half = x1_ref.shape[2] // 2
    for bb in range(x1_ref.shape[0]):
        x2_lo, x2_hi = _moe_residual(x1_ref, mod0_ref[bb, 0][5:6], yk_ref, w_ref, bb)
        ox[bb, :, 0:half] = x2_lo
        ox[bb, :, half:2 * half] = x2_hi
        m = mod_ref[bb, 0]
        h = _norm_mod(jnp.concatenate([x2_lo, x2_hi], axis=1), g_ref[...], m[0:1], m[1:2]).astype(BF16)
        _rope(_dot(h, wq[...]), cos_ref, sin_ref, oq.at[bb:bb + 1], 1.0)
        _rope(_dot(h, wk[...]), cos_ref, sin_ref, ok.at[bb:bb + 1], RET_DK ** -0.5)
        ov[bb] = _dot(h, wv[...]).astype(ov.dtype)
        og[bb] = _dot(h, wg[...]).astype(og.dtype)


def _inproj1(x1, mods0, yk, w_tok, mods, norm_g, cos_t, sin_t, weights, nct):
    bsz, t, d = x1.shape
    tm, tb = TOKEN_TILE, TOKEN_TILE_BATCH
    mod_spec = pl.BlockSpec((tb, 1, 6, d), lambda b, i: (b, (i >= nct).astype(I32), 0, 0))
    tok = lambda w: pl.BlockSpec((tb, tm, w), lambda b, i: (b, i, 0))
    lat = lambda w: pl.BlockSpec((tb, tm, w), lambda b, i: (b, jnp.maximum(i - nct, 0), 0))
    tab_spec = pl.BlockSpec((tm, RET_DK), lambda b, i: (i, 0))
    wq, wk, wv, wg = weights
    n_lat = t - nct * tm
    return pl.pallas_call(
        _inproj1_body,
        grid=(bsz // tb, t // tm),
        in_specs=[tok(d), mod_spec, pl.BlockSpec((TOP_K, tb, tm, d // 2), lambda b, i: (0, b, i, 0)), tok(TOP_K),
                  mod_spec, _const_spec((1, d)), tab_spec, tab_spec] + [_const_spec(w.shape) for w in weights],
        out_specs=[lat(d), tok(wq.shape[1]), tok(wk.shape[1]), tok(wv.shape[1]), lat(wg.shape[1])],
        out_shape=[jax.ShapeDtypeStruct((bsz, n_lat, d), F32), jax.ShapeDtypeStruct((bsz, t, wq.shape[1]), BF16),
                   jax.ShapeDtypeStruct((bsz, t, wk.shape[1]), BF16), jax.ShapeDtypeStruct((bsz, t, wv.shape[1]), BF16),
                   jax.ShapeDtypeStruct((bsz, n_lat, wg.shape[1]), BF16)],
        compiler_params=_params("arbitrary", "arbitrary"),
        name="inproj_retention",
    )(x1, mods0, yk.reshape(TOP_K, bsz, t, d // 2), w_tok.reshape(bsz, t, TOP_K), mods, norm_g.reshape(1, d),
      cos_t, sin_t, *weights)


def _backward_chunk(n, n_ctx_chunks, n_chunks):
    return jnp.where(n < n_ctx_chunks, n_ctx_chunks - 1 - n, n_chunks - 1 - (n - n_ctx_chunks))


def _gla_body(qd_f, ki_f, ks_f, ed_f, v_f, qd_b, ki_b, ks_b, ed_b, v_b, hmask_ref, bdmask_ref, o_f, o_b, st_f, st_b):
    c = GLA_CHUNK

    @pl.when(pl.program_id(1) == 0)
    def _():
        st_f[...] = jnp.zeros_like(st_f)
        st_b[...] = jnp.zeros_like(st_b)

    r4 = lax.broadcasted_iota(I32, (GLA_HEADS * c, c), 0) & (c - 1)
    c4 = lax.broadcasted_iota(I32, (GLA_HEADS * c, c), 1)
    dirs = ((qd_f, ki_f, ks_f, ed_f, v_f, o_f, st_f), (qd_b, ki_b, ks_b, ed_b, v_b, o_b, st_b))
    chains = [(bb, d) + dirs[d] for bb in range(qd_f.shape[0]) for d in range(2)]
    scores, inter, grow = [], [], []
    for bb, d, qd_ref, ki_ref, ks_ref, ed_ref, v_ref, o_ref, st_ref in chains:
        q_dec = qd_ref[bb]
        q_heads = jnp.concatenate([q_dec] * GLA_HEADS, axis=0) * hmask_ref[...]
        seen4 = (c4 <= r4) if d == 0 else (c4 >= r4)
        scores.append(jnp.where(seen4, _dot_nt(q_heads, ki_ref[bb]), 0.0).astype(BF16))
        inter.append(_dot_nt(q_dec, st_ref[bb].astype(BF16)))
        grow.append(_dot_tn(v_ref[bb], ks_ref[bb]))
    for (bb, d, qd_ref, ki_ref, ks_ref, ed_ref, v_ref, o_ref, st_ref), sc, o_inter, dst in zip(chains, scores, inter, grow):
        v = v_ref[bb]
        o_intra = jnp.concatenate(
            [_dot(sc[h * c:(h + 1) * c], v[:, h * GLA_DV:(h + 1) * GLA_DV]) for h in range(GLA_HEADS)], axis=1)
        o_ref[bb] = o_intra + o_inter
        st_ref[bb] = st_ref[bb] * ed_ref[bb, 0] + bdmask_ref[...] * dst


def _gla(per_dir, v, n_ctx):
    bsz, t, _ = v.shape
    nc, ncc = t // GLA_CHUNK, n_ctx // GLA_CHUNK
    gb = GLA_BATCH
    fwd = lambda b, n: (b, n, 0)
    bwd = lambda b, n: (b, _backward_chunk(n, ncc, nc), 0)
    hmask = (jnp.arange(AB_QK)[:, None] // GLA_CHUNK == jnp.arange(AB_QK)[None, :] // GLA_DK).astype(BF16)
    bdmask = (jnp.arange(AB_V)[:, None] // GLA_DV == jnp.arange(AB_QK)[None, :] // GLA_DK).astype(F32)

    def specs(idx):
        idx4 = lambda b, n: idx(b, n) + (0,)
        return [pl.BlockSpec((gb, GLA_CHUNK, AB_QK), idx)] * 3 + [pl.BlockSpec((gb, 1, 1, AB_QK), idx4),
                                                                  pl.BlockSpec((gb, GLA_CHUNK, AB_V), idx)]

    return pl.pallas_call(
        _gla_body,
        grid=(bsz // gb, nc),
        in_specs=specs(fwd) + specs(bwd) + [_const_spec(hmask.shape), _const_spec(bdmask.shape)],
        out_specs=[pl.BlockSpec((gb, GLA_CHUNK, AB_V), fwd), pl.BlockSpec((gb, GLA_CHUNK, AB_V), bwd)],
        out_shape=[jax.ShapeDtypeStruct((bsz, t, AB_V), F32)] * 2,
        scratch_shapes=[pltpu.VMEM((gb, AB_V, AB_QK), F32)] * 2,
        compiler_params=_params("arbitrary", "arbitrary"),
        name="gla_scan",
    )(*per_dir[0], v, *per_dir[1], v, hmask, bdmask)


def _cmul(x, y):
    return x[0] * y[0] - x[1] * y[1], x[0] * y[1] + x[1] * y[0]


def _s5_operators(lam_re, lam_im, log_step, b_re, b_im, c_re, c_im):
    ln = S5_CHUNK
    step = jnp.exp(log_step.astype(F32))[..., None]
    lam_re, lam_im = lam_re.astype(F32), lam_im.astype(F32)
    mag = jnp.exp(lam_re * step)
    a = (mag * jnp.cos(lam_im * step), mag * jnp.sin(lam_im * step))
    den = lam_re * lam_re + lam_im * lam_im
    f_re = ((a[0] - 1.0) * lam_re + a[1] * lam_im) / den
    f_im = (a[1] * lam_re - (a[0] - 1.0) * lam_im) / den
    bt_re, bt_im = b_re.transpose(0, 2, 1), b_im.transpose(0, 2, 1)
    bb = _cmul((f_re[:, :, None, :], f_im[:, :, None, :]), (bt_re, bt_im))
    bbt = jnp.concatenate([bb[0], -bb[1]], axis=-1)
    pw = (a[0][:, :, None, :], a[1][:, :, None, :])
    while pw[0].shape[2] < ln:
        top = (pw[0][:, :, -1:, :], pw[1][:, :, -1:, :])
        nxt = _cmul(top, pw)
        pw = (jnp.concatenate([pw[0], nxt[0]], axis=2), jnp.concatenate([pw[1], nxt[1]], axis=2))
    pw = (jnp.concatenate([jnp.ones_like(pw[0][:, :, :1]), pw[0]], axis=2),
          jnp.concatenate([jnp.zeros_like(pw[1][:, :, :1]), pw[1]], axis=2))
    ca = _cmul((c_re[:, :, None], c_im[:, :, None]), (pw[0][:, :, :, None, :], pw[1][:, :, :, None, :]))
    by_dir = lambda arr, lo, flip_d: jnp.stack([jnp.flip(arr[d, :, lo:lo + ln], axis=1) if d == flip_d
                                                else arr[d, :, lo:lo + ln] for d in range(2)])
    rows = lambda arr: arr.reshape(2, S5_GROUPS, ln * S5_GROUP, 2 * S5_P)
    cab = rows(by_dir(jnp.concatenate([ca[0], ca[1]], axis=-1), 0, 1))
    cab2 = rows(by_dir(jnp.concatenate([ca[0], -ca[1]], axis=-1), 1, 1)).astype(BF16)
    pwx = by_dir(jnp.concatenate([pw[0], pw[1]], axis=-1), 0, 0)
    lr, li = pw[0][:, :, ln], pw[1][:, :, ln]
    ac_rows = [jnp.concatenate([lr, lr], -1), jnp.concatenate([-li, li], -1), jnp.concatenate([li, -li], -1)]
    ac = jnp.stack(ac_rows + [jnp.zeros_like(ac_rows[0])] * 5, axis=2)
    return cab, cab2, bbt, pwx, ac


def _s5_group_operators(gg, cab_ref, bbt_ref, pwx_ref, tz, wx):
    ln, ch, p = S5_CHUNK, S5_GROUP, S5_P
    lane = lax.broadcasted_iota(I32, (ch, ln * ch), 1)
    for d in range(2):
        kern = _dot3(bbt_ref[d, gg], cab_ref[d, gg], dot=_dot_nt)
        bt = bbt_ref[d, gg]
        b_re, b_im = bt[:, 0:p], -bt[:, p:2 * p]
        for j in range(ln):
            if d == 0:
                blk = jnp.where(lane >= j * ch, kern if j == 0 else pltpu.roll(kern, j * ch, 1), 0.0)
            else:
                blk = jnp.where(lane < (j + 1) * ch, kern if j == ln - 1 else pltpu.roll(kern, (j + 1) * ch, 1), 0.0)
            tz[gg, d, j * ch:(j + 1) * ch, :] = blk.astype(BF16)
            pr, pi = pwx_ref[d, gg, j:j + 1, 0:p], pwx_ref[d, gg, j:j + 1, p:2 * p]
            x_re, x_im = pr * b_re - pi * b_im, pr * b_im + pi * b_re
            wx[gg, d, j * ch:(j + 1) * ch, :] = jnp.concatenate([x_re, x_im, x_im, x_re], axis=1).astype(BF16)


def _s5_placement(pall):
    rows, cols = pall.shape[1], pall.shape[2]
    row = lax.broadcasted_iota(I32, (rows, cols), 0)
    col = lax.broadcasted_iota(I32, (rows, cols), 1)
    same_token = (row // LANES) == (col // S5_GROUP)
    for g8 in range(pall.shape[0]):
        pall[g8] = jnp.where(same_token & ((row % LANES) == g8 * S5_GROUP + (col % S5_GROUP)), 1.0, 0.0).astype(BF16)


def _first_step():
    return (pl.program_id(0) == 0) & (pl.program_id(1) == 0)


def _s5_fold_body(ncs, u_ref, o_ref, pall, ucat):
    @pl.when(_first_step())
    def _():
        _s5_placement(pall)

    for b in range(u_ref.shape[1]):
        for j in range(S5_CHUNK):
            ucat[b * ncs:(b + 1) * ncs, j * LANES:(j + 1) * LANES] = u_ref[0, b, pl.ds(j, ncs, stride=S5_CHUNK), :].astype(BF16)
    for g8 in range(pall.shape[0]):
        o_ref[g8] = _dot(ucat[...], pall[g8]).astype(BF16)


def _s5_unfold_body(ncs, y_ref, o_ref, pall):
    @pl.when(_first_step())
    def _():
        _s5_placement(pall)

    def token_pair(i2, carry):
        r0 = pl.multiple_of(i2 * 2 * LANES, 2 * LANES)
        acc = _dot_nt(y_ref[0], pall[0, pl.ds(r0, 2 * LANES), :])
        for g8 in range(1, pall.shape[0]):
            acc = acc + _dot_nt(y_ref[g8], pall[g8, pl.ds(r0, 2 * LANES), :])
        for b in range(o_ref.shape[1]):
            for par in range(2):
                o_ref[0, b, pl.ds(2 * i2 + par, ncs, stride=S5_CHUNK), :] = (
                    acc[b * ncs:(b + 1) * ncs, par * LANES:(par + 1) * LANES])
        return carry

    lax.fori_loop(0, S5_CHUNK // 2, token_pair, 0)


def _s5_body(ncs_ctx, ncs, rows, u_ref, cab_ref, cab2_ref, bbt_ref, pwx_ref, ac_ref, y_ref, tz, wx, *vecs):
    half = 2 * S5_P
    n_groups = u_ref.shape[0]
    groups = [vecs[6 * gg:6 * gg + 6] for gg in range(n_groups)]
    for gg, (xx_f, xs_f, xx_b, xs_b, _, _) in enumerate(groups):
        _s5_group_operators(gg, cab_ref, bbt_ref, pwx_ref, tz, wx)
        for d, (xx, xs) in enumerate(((xx_f, xs_f), (xx_b, xs_b))):
            r = _dot(u_ref[gg], wx[gg, d])
            xx[...] = r[:, :half]
            xs[...] = r[:, half:]

    def advance(ac, s, s_sw, x, x_sw):
        return ac[0:1] * s + ac[1:2] * s_sw + x, ac[0:1] * s_sw + ac[2:3] * s + x_sw

    def step(n, carry):
        at_f = pl.ds(n, rows, stride=ncs)
        at_b = pl.ds(_backward_chunk(n, ncs_ctx, ncs), rows, stride=ncs)
        out = []
        for gg, (xx_f, xs_f, xx_b, xs_b, sin_f, sin_b) in enumerate(groups):
            s_f, sw_f, s_b, sw_b = carry[4 * gg:4 * gg + 4]
            sin_f[at_f, :] = s_f
            sin_b[at_b, :] = s_b
            out += advance(ac_ref[0, gg], s_f, sw_f, xx_f[at_f, :], xs_f[at_f, :])
            out += advance(ac_ref[1, gg], s_b, sw_b, xx_b[at_b, :], xs_b[at_b, :])
        return tuple(out)

    zero = jnp.zeros((rows, half), F32)
    lax.fori_loop(0, ncs, step, (zero,) * (4 * n_groups))
    for gg, (_, _, _, _, sin_f, sin_b) in enumerate(groups):
        u = u_ref[gg]
        y_ref[gg] = (_dot(u, tz[gg, 0]) + _dot(u, tz[gg, 1]) + _dot_nt(sin_f[...].astype(BF16), cab2_ref[0, gg])
                     + _dot_nt(sin_b[...].astype(BF16), cab2_ref[1, gg])).astype(BF16)


def _s5(u4, ops, n_ctx):
    nq, bsz, t, _ = u4.shape
    ln, lanes = S5_CHUNK, S5_CHUNK * S5_GROUP
    gq = S5_GROUPS // nq
    ncs, ncs_ctx = t // ln, n_ctx // ln
    m = ncs * bsz
    hb = S5_FOLD_BATCH
    tok_spec = pl.BlockSpec((1, hb, t, LANES), lambda q, h: (q, h, 0, 0))
    grp_spec = pl.BlockSpec((gq, hb * ncs, lanes), lambda q, h: (q, h, 0))
    pall = pltpu.VMEM((gq, ln * LANES, lanes), BF16)
    ug = pl.pallas_call(
        functools.partial(_s5_fold_body, ncs),
        grid=(nq, bsz // hb),
        in_specs=[tok_spec],
        out_specs=grp_spec,
        out_shape=jax.ShapeDtypeStruct((S5_GROUPS, m, lanes), BF16),
        scratch_shapes=[pall, pltpu.VMEM((hb * ncs, ln * LANES), BF16)],
        compiler_params=_params("arbitrary", "arbitrary"),
        name="s5_fold",
    )(u4)
    sg = S5_SCAN_GROUPS
    dir_spec = lambda arr: pl.BlockSpec((2, sg) + arr.shape[2:], lambda g: (0, g, 0, 0))
    yg = pl.pallas_call(
        functools.partial(_s5_body, ncs_ctx, ncs, bsz),
        grid=(S5_GROUPS // sg,),
        in_specs=[pl.BlockSpec((sg, m, lanes), lambda g: (g, 0, 0))] + [dir_spec(arr) for arr in ops],
        out_specs=pl.BlockSpec((sg, m, lanes), lambda g: (g, 0, 0)),
        out_shape=jax.ShapeDtypeStruct((S5_GROUPS, m, lanes), BF16),
        scratch_shapes=[pltpu.VMEM((sg, 2, lanes, lanes), BF16)] * 2 + [pltpu.VMEM((m, 2 * S5_P), F32)] * (6 * sg),
        compiler_params=_params("arbitrary"),
        name="s5_scan",
    )(ug, *ops)
    return pl.pallas_call(
        functools.partial(_s5_unfold_body, ncs),
        grid=(nq, bsz // hb),
        in_specs=[grp_spec],
        out_specs=tok_spec,
        out_shape=jax.ShapeDtypeStruct(u4.shape, F32),
        scratch_shapes=[pall],
        compiler_params=_params("arbitrary", "arbitrary"),
        name="s5_unfold",
    )(yg)


def _ret_body(q_f, k_f, v_f, q_b, k_b, v_b, dmat_ref, rsc_ref, csc_ref, gam_ref, o_f, o_b, st_f, st_b):
    @pl.when(pl.program_id(1) == 0)
    def _():
        st_f[...] = jnp.zeros_like(st_f)
        st_b[...] = jnp.zeros_like(st_b)

    dirs = ((q_f, k_f, v_f, o_f, st_f), (q_b, k_b, v_b, o_b, st_b))
    for bb in range(q_f.shape[0]):
        for d, (q_ref, k_ref, v_ref, o_ref, st_ref) in enumerate(dirs):
            for h in range(RET_HEADS):
                qh = q_ref[bb, :, h * RET_DK:(h + 1) * RET_DK]
                kh = k_ref[bb, :, h * RET_DK:(h + 1) * RET_DK]
                vh = v_ref[bb, :, h * RET_DV:(h + 1) * RET_DV]
                st = st_ref[bb, h]
                scores = (_dot_nt(qh, kh) * dmat_ref[d, h]).astype(BF16)
                o = _dot(scores, vh) + rsc_ref[d, h] * _dot(qh, st.astype(BF16))
                o_ref[bb, :, h * RET_DV:(h + 1) * RET_DV] = o.astype(o_ref.dtype)
                k_state = (kh.astype(F32) * csc_ref[d, h]).astype(BF16)
                st_ref[bb, h] = st * gam_ref[d, h] + _dot_tn(k_state, vh)


def _retention(q, k, v, decay_logit, n_ctx):
    bsz, t, _ = q.shape
    c = RET_CHUNK
    nc, ncc = t // c, n_ctx // c
    nl = nc - ncc
    rb = RET_BATCH
    log_gamma = jax.nn.log_sigmoid(decay_logit.astype(F32))[:, :, None, None]
    i = jnp.arange(c, dtype=F32)
    lag = i[:, None] - i[None, :]
    lag = jnp.stack([lag, -lag])[:, None]
    dmat = jnp.where(lag >= 0, jnp.exp(log_gamma * jnp.maximum(lag, 0.0)), 0.0)
    done = jnp.stack([i + 1.0, c - i])[:, None, :, None]
    rsc = jnp.exp(log_gamma * done)
    csc = jnp.exp(log_gamma * (c - done))
    gam = jnp.exp(log_gamma[:, :, 0, 0] * c)
    fwd = lambda b, n: (b, n, 0)
    bwd = lambda b, n: (b, _backward_chunk(n, ncc, nc), 0)
    o_fwd = lambda b, n: (b, jnp.maximum(n - ncc, 0), 0)
    o_bwd = lambda b, n: (b, nl - 1 - jnp.maximum(n - ncc, 0), 0)

    def specs(idx):
        return [pl.BlockSpec((rb, c, RET_QK), idx), pl.BlockSpec((rb, c, RET_QK), idx), pl.BlockSpec((rb, c, RET_MIX), idx)]

    return pl.pallas_call(
        _ret_body,
        grid=(bsz // rb, nc),
        in_specs=specs(fwd) + specs(bwd) + [_const_spec(dmat.shape), _const_spec(rsc.shape), _const_spec(csc.shape),
                                            pl.BlockSpec(memory_space=pltpu.SMEM)],
        out_specs=[pl.BlockSpec((rb, c, RET_MIX), o_fwd), pl.BlockSpec((rb, c, RET_MIX), o_bwd)],
        out_shape=[jax.ShapeDtypeStruct((bsz, nl * c, RET_MIX), BF16)] * 2,
        scratch_shapes=[pltpu.VMEM((rb, RET_HEADS, RET_DK, RET_DV), F32)] * 2,
        compiler_params=_params("arbitrary", "arbitrary"),
        name="retention_scan",
    )(q, k, v, q, k, v, dmat, rsc, csc, gam)


def _zero_counts_at_start(cnt_ref):
    @pl.when(_first_step())
    def _():
        cnt_ref[...] = jnp.zeros_like(cnt_ref)


def _route(xs, mixed, mods, n2g_ref, wr_ref, br_ref, x1_ref, h2_ref, e_ref, w_ref, r_ref, cnt_ref):
    rows = xs[0].shape[0]
    h2s = []
    for bb, (x, mod) in enumerate(zip(xs, mods)):
        x1 = x + mod[2:3] * mixed[bb * rows:(bb + 1) * rows]
        x1_ref[bb] = x1
        h2s.append(_norm_mod(x1, n2g_ref[...], mod[3:4], mod[4:5]))
        h2_ref[bb] = _pack_rows(h2s[bb])
    h2 = h2s[0] if len(h2s) == 1 else jnp.concatenate(h2s, axis=0)
    tm = h2.shape[0]
    logits = _dot3(wr_ref[...], h2, dot=_dot_nt) + br_ref[...]
    ie = lax.broadcasted_iota(I32, logits.shape, 0)
    tops, picks = [], []
    for _ in range(TOP_K):
        mx = jnp.max(logits, axis=0, keepdims=True)
        pick = jnp.min(jnp.where(logits == mx, ie, N_EXPERTS), axis=0, keepdims=True)
        tops.append(mx)
        picks.append(pick)
        logits = jnp.where(ie == pick, -jnp.inf, logits)
    ex = [jnp.exp(tk - tops[0]) for tk in tops]
    den = ex[0] + ex[1] + ex[2] + ex[3]
    for kk in range(TOP_K):
        w_ref[0, kk:kk + 1, :] = ex[kk] / den
        e_ref[0, kk:kk + 1, :] = picks[kk]

    earlier = (lax.broadcasted_iota(I32, (tm, tm), 0) < lax.broadcasted_iota(I32, (tm, tm), 1))
    earlier = jnp.where(earlier, 1.0, 0.0).astype(BF16)
    run = cnt_ref[:, 0:1]
    for kk, pick in enumerate(picks):
        onehot = jnp.where(ie == pick, 1.0, 0.0)
        before = _dot(onehot.astype(BF16), earlier) + run
        r_ref[0, kk:kk + 1, :] = jnp.sum(onehot * before, axis=0, keepdims=True).astype(I32)
        run = run + jnp.sum(onehot, axis=1, keepdims=True)
    cnt_ref[...] = jnp.broadcast_to(run, cnt_ref.shape)


def _mix0_body(nct, ctx_ref, lat_ref, mod_ref, of_ref, ob_ref, g_ref, ys_ref, u_ref, gng_ref, dsk_ref, gluw_ref,
               glub_ref, wo_ref, n2g_ref, wr_ref, br_ref, x1_ref, h2_ref, e_ref, w_ref, r_ref, cnt_ref):
    _zero_counts_at_start(cnt_ref)
    tb, tm = of_ref.shape[0], of_ref.shape[1]
    rows = lambda ref: ref[...].reshape(tb * tm, ref.shape[-1])
    o = rows(of_ref) + rows(ob_ref)
    heads = []
    for h in range(GLA_HEADS):
        oh = o[:, h * GLA_DV:(h + 1) * GLA_DV]
        heads.append(oh * lax.rsqrt(jnp.mean(oh * oh, axis=-1, keepdims=True) + EPS))
    gla = jnp.concatenate(heads, axis=1) * gng_ref[...] * _silu(rows(g_ref).astype(F32))
    lane_blocks = lambda ref: jnp.concatenate([ref[qb].reshape(tb * tm, LANES) for qb in range(ref.shape[0])], axis=1)
    y = jax.nn.gelu(lane_blocks(ys_ref) + dsk_ref[...] * lane_blocks(u_ref))
    y = y * jax.nn.sigmoid(_dot(y.astype(BF16), gluw_ref[...]) + glub_ref[...])
    mixed = _dot(gla.astype(BF16), wo_ref[0:AB_V]) + _dot(y.astype(BF16), wo_ref[AB_V:AB_V + S5_CH])
    _route([_stream_tile(nct, ctx_ref, lat_ref, bb) for bb in range(tb)], mixed, [mod_ref[bb, 0] for bb in range(tb)],
           n2g_ref, wr_ref, br_ref, x1_ref, h2_ref, e_ref, w_ref, r_ref, cnt_ref)


def _mix1_body(x_ref, mod_ref, of_ref, ob_ref, g_ref, ng_ref, wo_ref, n2g_ref, wr_ref, br_ref,
               x1_ref, h2_ref, e_ref, w_ref, r_ref, cnt_ref):
    _zero_counts_at_start(cnt_ref)
    mixed = None
    for h in range(RET_HEADS):
        sl = slice(h * RET_DV, (h + 1) * RET_DV)
        oh = of_ref[0, :, sl].astype(F32) + ob_ref[0, :, sl].astype(F32)
        mu = jnp.mean(oh, axis=-1, keepdims=True)
        cen = oh - mu
        var = jnp.mean(cen * cen, axis=-1, keepdims=True)
        gated = cen * lax.rsqrt(var + EPS) * ng_ref[:, sl] * _silu(g_ref[0, :, sl].astype(F32))
        part = _dot(gated.astype(BF16), wo_ref[sl])
        mixed = part if mixed is None else mixed + part
    _route([x_ref[0]], mixed, [mod_ref[0, 0]], n2g_ref, wr_ref, br_ref, x1_ref, h2_ref, e_ref, w_ref, r_ref, cnt_ref)


def _mix_call(body, name, stream, mods, tiles, acts, consts, norm2_g, w_router, b_router, n_tok, seg_tile0,
              tm=TOKEN_TILE, tb=1):
    bsz, _, d = stream[-1].shape
    off = lambda b, i: (b, i + seg_tile0, 0)
    loc = lambda b, i: (b, i, 0)
    ntl = bsz // tb * tiles
    flat = lambda b, i: (b * tiles + i, 0, 0)
    in_specs = list(_split_specs(n_tok, d, tb)) if len(stream) == 2 else [pl.BlockSpec((tb, tm, d), off)]
    in_specs.append(pl.BlockSpec((tb, 1, 6, d), lambda b, i: (b, ((i + seg_tile0) >= n_tok).astype(I32), 0, 0)))
    args = list(stream) + [mods]
    for arr, offset in acts:
        if arr.ndim == 4:
            in_specs.append(pl.BlockSpec((arr.shape[0], tb, tm, arr.shape[3]), lambda b, i: (0, b, i, 0)))
        else:
            in_specs.append(pl.BlockSpec((tb, tm, arr.shape[2]), off if offset else loc))
        args.append(arr)
    tail = list(consts) + [norm2_g.reshape(1, d), w_router.T, b_router.reshape(N_EXPERTS, 1)]
    in_specs += [_const_spec(a.shape) for a in tail]
    args += tail
    tok_out = pl.BlockSpec((1, TOP_K, tb * tm), flat)
    return pl.pallas_call(
        body,
        grid=(bsz // tb, tiles),
        in_specs=in_specs,
        out_specs=[pl.BlockSpec((tb, tm, d), loc), pl.BlockSpec((tb, tm, d // 2), loc), tok_out, tok_out, tok_out,
                   _const_spec((N_EXPERTS, LANES))],
        out_shape=[jax.ShapeDtypeStruct((bsz, tiles * tm, d), F32), jax.ShapeDtypeStruct((bsz, tiles * tm, d // 2), U32),
                   jax.ShapeDtypeStruct((ntl, TOP_K, tb * tm), I32), jax.ShapeDtypeStruct((ntl, TOP_K, tb * tm), F32),
                   jax.ShapeDtypeStruct((ntl, TOP_K, tb * tm), I32), jax.ShapeDtypeStruct((N_EXPERTS, LANES), F32)],
        compiler_params=_params("arbitrary", "arbitrary"),
        name=name,
    )(*args)


def _cast_rows(src_ref, dst_ref, rows):
    def chunk(j, carry):
        r = pl.multiple_of(j * rows, rows)
        dst_ref[pl.ds(r, rows), :] = src_ref[pl.ds(r, rows), :].astype(BF16)
        return carry

    lax.fori_loop(0, dst_ref.shape[0] // rows, chunk, 0)


def _expert_mlp(x, wgu_bf, bgu_ref, wd_bf, bd_ref):
    x_lo, x_hi = _unpack_rows(x)
    half = x_lo.shape[1]
    gu = (_dot(x_lo.astype(BF16), wgu_bf[0:half]) + _dot(x_hi.astype(BF16), wgu_bf[half:2 * half])
          + bgu_ref[0, 0])
    gate = jnp.minimum(gu[:, :D_FF], SWIGLU_LIMIT)
    lin = jnp.clip(gu[:, D_FF:], -SWIGLU_LIMIT, SWIGLU_LIMIT)
    act = gate * jax.nn.sigmoid(SWIGLU_ALPHA * gate) * (lin + 1.0)
    return _pack_rows(_dot(act.astype(BF16), wd_bf[...]) + bd_ref[0, 0])


def _expert_body(layer, be_ref, rows_ref, slot_ref, next_ref, x_ref, wgu_hbm, bgu_ref, wd_hbm, bd_ref, o_ref,
                 wgu_f32, wd_f32, wgu_bf, wd_bf, sems):
    i = pl.program_id(0)
    rows = rows_ref[i]
    part = x_ref.shape[0] // EXPERT_BLOCK_PARTS
    new_expert = (i == 0) | (be_ref[i] != be_ref[jnp.maximum(i - 1, 0)])

    def weight_copies(expert, slot):
        return (pltpu.make_async_copy(wgu_hbm.at[layer, expert], wgu_f32.at[slot], sems.at[0, slot]),
                pltpu.make_async_copy(wd_hbm.at[layer, expert], wd_f32.at[slot], sems.at[1, slot]))

    @pl.when((rows > 0) & new_expert)
    def _():
        slot = slot_ref[i]

        @pl.when(i == 0)
        def _():
            for cp in weight_copies(be_ref[i], slot):
                cp.start()

        @pl.when(next_ref[i] >= 0)
        def _():
            for cp in weight_copies(next_ref[i], 1 - slot):
                cp.start()

        for cp in weight_copies(be_ref[i], slot):
            cp.wait()
        _cast_rows(wgu_f32.at[slot], wgu_bf, 128)
        _cast_rows(wd_f32.at[slot], wd_bf, 128)

    for p in range(1, EXPERT_BLOCK_PARTS + 1):
        @pl.when((rows > (p - 1) * part) & (rows <= p * part))
        def _():
            o_ref[0:p * part, :] = _expert_mlp(x_ref[0:p * part, :], wgu_bf, bgu_ref, wd_bf, bd_ref)
            if p < EXPERT_BLOCK_PARTS:
                o_ref[p * part:, :] = jnp.zeros((o_ref.shape[0] - p * part, o_ref.shape[1]), o_ref.dtype)

    @pl.when(rows == 0)
    def _():
        o_ref[...] = jnp.zeros_like(o_ref)


def _experts(xb, block_e, block_rows, block_slot, block_next, layer, w_gu, b_gu, w_down, b_down):
    n_slots, half = xb.shape
    d = 2 * half
    n_blocks = n_slots // MOE_BLOCK
    depth = w_gu.shape[0]
    by_expert = lambda i, be, *_: (layer, be[i], 0, 0)
    rows_spec = pl.BlockSpec((MOE_BLOCK, half), lambda i, *_: (i, 0))
    return pl.pallas_call(
        functools.partial(_expert_body, layer),
        grid_spec=pltpu.PrefetchScalarGridSpec(
            num_scalar_prefetch=4,
            grid=(n_blocks,),
            in_specs=[rows_spec, pl.BlockSpec(memory_space=pl.ANY), pl.BlockSpec((1, 1, 1, 2 * D_FF), by_expert),
                      pl.BlockSpec(memory_space=pl.ANY), pl.BlockSpec((1, 1, 1, d), by_expert)],
            out_specs=rows_spec,
            scratch_shapes=[pltpu.VMEM((2, d, 2 * D_FF), F32), pltpu.VMEM((2, D_FF, d), F32),
                            pltpu.VMEM((d, 2 * D_FF), BF16), pltpu.VMEM((D_FF, d), BF16),
                            pltpu.SemaphoreType.DMA((2, 2))]),
        out_shape=jax.ShapeDtypeStruct((n_slots, half), U32),
        compiler_params=_params("arbitrary"),
        name="moe_experts",
    )(block_e, block_rows, block_slot, block_next, xb, w_gu, b_gu.reshape(depth, N_EXPERTS, 1, 2 * D_FF), w_down,
      b_down.reshape(depth, N_EXPERTS, 1, d))


def _combine_body(x1_ref, mod_ref, yk_ref, w_ref, fg_ref, o_ref):
    d = x1_ref.shape[2]
    half = d // 2
    x2_lo, x2_hi = _moe_residual(x1_ref, mod_ref[0, 0][5:6], yk_ref, w_ref)
    ms = (jnp.sum(x2_lo * x2_lo, axis=-1, keepdims=True) + jnp.sum(x2_hi * x2_hi, axis=-1, keepdims=True)) / d
    r = lax.rsqrt(ms + EPS)
    o_ref[0, :, 0:half] = x2_lo * r * fg_ref[:, 0:half]
    o_ref[0, :, half:d] = x2_hi * r * fg_ref[:, half:d]


def _combine(x1, mods, yk, w_tok, seg_tile0, n_tok, final_g, tm):
    bsz, t, d = x1.shape
    loc = lambda b, i: (b, i, 0)
    return pl.pallas_call(
        _combine_body,
        grid=(bsz, t // tm),
        in_specs=[pl.BlockSpec((1, tm, d), loc),
                  pl.BlockSpec((1, 1, 6, d), lambda b, i: (b, ((i + seg_tile0) >= n_tok).astype(I32), 0, 0)),
                  pl.BlockSpec((TOP_K, 1, tm, d // 2), lambda b, i: (0, b, i, 0)),
                  pl.BlockSpec((1, tm, TOP_K), loc), _const_spec((1, d))],
        out_specs=pl.BlockSpec((1, tm, d), loc),
        out_shape=jax.ShapeDtypeStruct((bsz, t, d), F32),
        compiler_params=_params("arbitrary", "arbitrary"),
        name="moe_combine",
    )(x1, mods, yk.reshape(TOP_K, bsz, t, d // 2), w_tok.reshape(bsz, t, TOP_K), final_g.reshape(1, d))


def _sc_mesh():
    return plsc.VectorSubcoreMesh(core_axis_name="core", subcore_axis_name="subcore",
                                  num_cores=SC_CORES, num_subcores=SC_SUBCORES)


def _sc_worker_base(per_worker):
    return (lax.axis_index("subcore") * SC_CORES + lax.axis_index("core")) * per_worker


def _sc_dispatch(rows, dest, n_slots):
    n, w = rows.shape
    per_worker = n // SC_WORKERS
    assert per_worker * SC_WORKERS == n and per_worker % SC_CHUNK == 0

    @functools.partial(
        pl.kernel, mesh=_sc_mesh(), out_type=jax.ShapeDtypeStruct((n_slots, w), rows.dtype),
        scratch_types=[pltpu.VMEM((SC_CHUNK,), I32)] * TOP_K + [pltpu.VMEM((SC_CHUNK, w), rows.dtype),
                                                                pltpu.SemaphoreType.DMA],
        name="moe_dispatch")
    def scatter_rows(rows_hbm, dest_hbm, out_hbm, *scratch):
        idx_refs, buf, sem = scratch[:TOP_K], scratch[TOP_K], scratch[TOP_K + 1]
        base0 = _sc_worker_base(per_worker)

        @pl.loop(0, per_worker // SC_CHUNK)
        def _(j):
            base = base0 + j * SC_CHUNK
            pltpu.sync_copy(rows_hbm.at[pl.ds(base, SC_CHUNK)], buf)
            for k, idx in enumerate(idx_refs):
                pltpu.sync_copy(dest_hbm.at[pl.ds(k * n + base, SC_CHUNK)], idx)
            copies = [pltpu.make_async_copy(buf, out_hbm.at[idx], sem) for idx in idx_refs]
            for cp in copies:
                cp.start()
            for cp in copies:
                cp.wait()

    return scatter_rows(rows, dest)


def _sc_gather(table, idx):
    n = idx.shape[0]
    w = table.shape[1]
    per_worker = n // SC_WORKERS
    n_chunks = per_worker // SC_CHUNK
    assert per_worker * SC_WORKERS == n and n_chunks * SC_CHUNK == per_worker and n_chunks % 2 == 0

    @functools.partial(
        pl.kernel, mesh=_sc_mesh(), out_type=jax.ShapeDtypeStruct((n, w), table.dtype),
        scratch_types=([pltpu.VMEM((SC_CHUNK,), I32)] * 2 + [pltpu.VMEM((SC_CHUNK, w), table.dtype)] * 2
                       + [pltpu.SemaphoreType.DMA] * 4),
        name="moe_gather")
    def gather_rows(table_hbm, idx_hbm, out_hbm, idx0, idx1, buf0, buf1, gsem0, gsem1, wsem0, wsem1):
        base0 = _sc_worker_base(per_worker)

        def gather_copy(idx_v, buf, sem):
            return pltpu.make_async_copy(table_hbm.at[idx_v], buf, sem)

        def write_copy(j, buf, sem):
            return pltpu.make_async_copy(buf, out_hbm.at[pl.ds(base0 + j * SC_CHUNK, SC_CHUNK)], sem)

        def start_gather(j, idx_v, buf, sem):
            pltpu.sync_copy(idx_hbm.at[pl.ds(base0 + j * SC_CHUNK, SC_CHUNK)], idx_v)
            gather_copy(idx_v, buf, sem).start()

        start_gather(0, idx0, buf0, gsem0)

        @pl.loop(0, n_chunks, step=2)
        def _(j):
            @pl.when(j > 0)
            def _():
                write_copy(j - 1, buf1, wsem1).wait()
            start_gather(j + 1, idx1, buf1, gsem1)
            gather_copy(idx0, buf0, gsem0).wait()
            write_copy(j, buf0, wsem0).start()

            @pl.when(j + 2 < n_chunks)
            def _():
                write_copy(j, buf0, wsem0).wait()
                start_gather(j + 2, idx0, buf0, gsem0)
            gather_copy(idx1, buf1, gsem1).wait()
            write_copy(j + 1, buf1, wsem1).start()

        write_copy(n_chunks - 2, buf0, wsem0).wait()
        write_copy(n_chunks - 1, buf1, wsem1).wait()

    return gather_rows(table, idx)


def _moe(h2, e_tl, w_tl, r_tl, cnt, layer, w_gu, b_gu, w_down, b_down, tb=1):
    bsz, t, half = h2.shape
    n = bsz * t
    flat = lambda a: a.reshape(bsz // tb, -1, TOP_K, tb, a.shape[2] // tb).transpose(2, 0, 3, 1, 4).reshape(TOP_K, n)
    e_k, w_k, r_k = flat(e_tl), flat(w_tl), flat(r_tl)
    counts = cnt[:, 0].astype(I32)
    padded = (counts + MOE_BLOCK - 1) // MOE_BLOCK * MOE_BLOCK
    pad_end = jnp.cumsum(padded)
    pad_start = pad_end - padded
    n_blocks = (n * TOP_K + MOE_BLOCK - 1) // MOE_BLOCK + N_EXPERTS
    block_start = jnp.arange(n_blocks, dtype=I32) * MOE_BLOCK
    block_e = jnp.minimum(jnp.sum((pad_end[None, :] <= block_start[:, None]).astype(I32), axis=1), N_EXPERTS - 1)
    block_rows = jnp.clip(counts[block_e] - (block_start - pad_start[block_e]), 0, MOE_BLOCK).astype(I32)
    experts = jnp.arange(N_EXPERTS, dtype=I32)
    busy = padded > 0
    block_slot = ((jnp.cumsum(busy.astype(I32)) - 1) % 2)[block_e].astype(I32)
    later = jnp.where(busy[None, :] & (experts[None, :] > experts[:, None]), experts[None, :], N_EXPERTS)
    next_busy = jnp.min(later, axis=1)
    block_next = jnp.where(next_busy < N_EXPERTS, next_busy, -1)[block_e].astype(I32)
    start_k = jnp.sum(jnp.where(e_k[..., None] == jnp.arange(N_EXPERTS, dtype=I32), pad_start, 0), axis=-1)
    dest = (start_k + r_k).reshape(TOP_K * n)
    xb = _sc_dispatch(h2.reshape(n, half), dest, n_blocks * MOE_BLOCK)
    yb = _experts(xb, block_e, block_rows, block_slot, block_next, layer, w_gu, b_gu, w_down, b_down)
    return _sc_gather(yb, dest), w_k.T


def _rope_tables(n_ctx, n_lat):
    n_freq = RET_DK // 4
    inv_freq = ROPE_BASE ** (-jnp.arange(n_freq, dtype=F32) / n_freq)
    pos = jnp.arange(n_lat, dtype=I32)
    cos, sin = [], []
    for p in (pos // GRID_W, pos % GRID_W):
        ang = p.astype(F32)[:, None] * inv_freq
        cos += [jnp.cos(ang), jnp.cos(ang)]
        sin += [-jnp.sin(ang), jnp.sin(ang)]
    cos, sin = jnp.concatenate(cos, axis=1), jnp.concatenate(sin, axis=1)
    return (jnp.concatenate([jnp.ones((n_ctx, RET_DK), F32), cos], axis=0),
            jnp.concatenate([jnp.zeros((n_ctx, RET_DK), F32), sin], axis=0))


def kernel(x, c, ctx, c_ctx, ada_w, ada_b, norm1_g, norm2_g, ab_w_in, ab_w_out, gla_wa, gla_ba, gla_norm_g, s5_lam_re, s5_lam_im, s5_log_step, s5_b_re, s5_b_im, s5_c_re, s5_c_im, s5_d, s5_glu_w, s5_glu_b, ret_w_in, ret_w_out, ret_decay_logit, ret_norm_g, moe_w_router, moe_b_router, moe_w_gu, moe_b_gu, moe_w_down, moe_b_down, final_norm_g):
    bsz, n_lat, d = x.shape
    n_ctx = ctx.shape[1]
    depth = ada_w.shape[0]
    assert depth == 2 and d == D_MODEL and bsz == 8, "kernels are laid out for the stated problem shape"
    assert n_ctx % TOKEN_TILE == 0 and n_lat % LATENT_TILE == 0 and n_lat % GRID_W == 0
    t = n_ctx + n_lat
    nct = n_ctx // TOKEN_TILE

    cvec = jnp.zeros((16, d), F32).at[:bsz].set(c).at[bsz].set(c_ctx)
    mod = _ada_mod(cvec, ada_w, ada_b).reshape(depth, 16, 6, d)
    mods = [jnp.stack([jnp.broadcast_to(mod[l, bsz], (bsz, 6, d)), mod[l, :bsz]], axis=1) for l in range(depth)]

    w_in = ab_w_in[0].astype(BF16)
    cuts = [0, AB_QK, 2 * AB_QK, 2 * AB_QK + AB_V, 2 * AB_QK + 2 * AB_V, 2 * AB_QK + 2 * AB_V + 2 * GLA_RANK,
            w_in.shape[1]]
    pieces = [w_in[:, a:b] for a, b in zip(cuts[:-1], cuts[1:])]
    wa_pad = jnp.zeros((2, 2 * GLA_RANK, AB_QK), F32)
    wa_pad = wa_pad.at[0, :GLA_RANK].set(gla_wa[0, 0]).at[1, GLA_RANK:].set(gla_wa[0, 1])
    outs = _inproj0(ctx, x, mods[0], norm1_g[0], pieces, wa_pad, gla_ba[0].reshape(2, 1, AB_QK), nct)
    v, g, u = outs[8:]
    o_f, o_b = _gla((outs[0:4], outs[4:8]), v, n_ctx)
    ops = _s5_operators(s5_lam_re[0], s5_lam_im[0], s5_log_step[0], s5_b_re[0], s5_b_im[0], s5_c_re[0], s5_c_im[0])
    ys = _s5(u, ops, n_ctx)
    consts = [jnp.tile(gla_norm_g[0], GLA_HEADS).reshape(1, AB_V), s5_d[0].reshape(1, S5_CH),
              s5_glu_w[0].astype(BF16), s5_glu_b[0].reshape(1, S5_CH), ab_w_out[0].astype(BF16)]
    x1, h2, e_tl, w_tl, r_tl, cnt = _mix_call(
        functools.partial(_mix0_body, nct), "mix_gla_s5", (ctx, x), mods[0], t // TOKEN_TILE,
        [(o_f, False), (o_b, False), (g, False), (ys, False), (u, False)], consts,
        norm2_g[0], moe_w_router[0], moe_b_router[0], nct, 0, tb=INPROJ0_TILE_BATCH)
    yk, w_tok = _moe(h2, e_tl, w_tl, r_tl, cnt, 0, moe_w_gu, moe_b_gu, moe_w_down, moe_b_down, tb=INPROJ0_TILE_BATCH)

    w_in = ret_w_in[0].astype(BF16)
    cuts = [0, RET_QK, 2 * RET_QK, 2 * RET_QK + RET_MIX, w_in.shape[1]]
    pieces = [w_in[:, a:b] for a, b in zip(cuts[:-1], cuts[1:])]
    cos_t, sin_t = _rope_tables(n_ctx, n_lat)
    x2, q, k, v, g = _inproj1(x1, mods[0], yk, w_tok, mods[1], norm1_g[1], cos_t, sin_t, pieces, nct)
    o_f, o_b = _retention(q, k, v, ret_decay_logit[0], n_ctx)
    consts = [ret_norm_g[0].reshape(1, RET_MIX), ret_w_out[0].astype(BF16)]
    x1, h2, e_tl, w_tl, r_tl, cnt = _mix_call(
        _mix1_body, "mix_retention", (x2,), mods[1], n_lat // LATENT_TILE,
        [(o_f, False), (o_b, False), (g, False)], consts,
        norm2_g[1], moe_w_router[1], moe_b_router[1], 0, 0, tm=LATENT_TILE)
    yk, w_tok = _moe(h2, e_tl, w_tl, r_tl, cnt, 1, moe_w_gu, moe_b_gu, moe_w_down, moe_b_down)
    return _combine(x1, mods[1], yk, w_tok, 0, 0, final_norm_g, LATENT_TILE)
```
